```python
import functools
import jax, jax.numpy as jnp
from jax import lax
import numpy as np

D_MODEL = 1024
BATCH = 32
SEQ = 256
DEPTH = 1
DEC_BATCH = 4
DEC_SEQ = 2048
PAST_LEN = 256

GRID_W = 64
N_ATT_HEADS = 8
ATT_HEAD_DIM = 64
ATT_WIDTH = N_ATT_HEADS * ATT_HEAD_DIM
WIN_ROWS = 8
WIN_COLS = 16
N_ML_HEADS = 4
ML_HEAD_DIM = 128
ML_WIDTH = N_ML_HEADS * ML_HEAD_DIM
MIX_WIDTH = ATT_WIDTH + ML_WIDTH
N_GATES = 4 * N_ML_HEADS
PROJ_WIDTH = 3 * ATT_WIDTH + 4 * ML_WIDTH + N_GATES
ML_CHUNK = 64
D_FF = 2816
CONV_WIDTH = 3
ROPE_THETA = 10000.0
Q_BLOCK = 128
EPS = 1e-6

kernel_name = 'hybrid_natten_mlstm_prefix_dit_step'


def rmsnorm(x, g):
    xf = x.astype(jnp.float32)
    y = xf * lax.rsqrt(jnp.mean(xf * xf, axis=-1, keepdims=True) + EPS)
    return y.astype(x.dtype) * g


def adaln(cvec, w_mod, b_mod):
    mod = jax.nn.silu(cvec) @ w_mod + b_mod
    return jnp.split(mod[:, None, :], 6, axis=-1)


def axial_rope(x):
    T, Dh = x.shape[1], x.shape[-1]
    half = Dh // 2
    quarter = half // 2
    pos = jnp.arange(T)
    inv_freq = ROPE_THETA ** (-jnp.arange(quarter, dtype=jnp.float32) / quarter)

    def rot(xa, p):
        ang = p.astype(jnp.float32)[:, None] * inv_freq[None, :]
        cos = jnp.cos(ang)[None, :, None, :]
        sin = jnp.sin(ang)[None, :, None, :]
        x1, x2 = xa[..., :quarter], xa[..., quarter:]
        return jnp.concatenate([x1 * cos - x2 * sin, x1 * sin + x2 * cos], axis=-1)

    out = jnp.concatenate([rot(x[..., :half], pos // GRID_W), rot(x[..., half:], pos % GRID_W)], axis=-1)
    return out.astype(x.dtype)


def context_attention(q, k, v):
    B, S, H, Dh = q.shape
    scale = Dh ** -0.5
    qb = q.reshape(B, S // Q_BLOCK, Q_BLOCK, H, Dh).transpose(1, 0, 2, 3, 4)

    def blk(qi):
        s = jnp.einsum('bqhd,bkhd->bhqk', qi, k).astype(jnp.float32) * scale
        p = jax.nn.softmax(s, axis=-1).astype(v.dtype)
        return jnp.einsum('bhqk,bkhd->bqhd', p, v)

    out = lax.map(blk, qb)
    return out.transpose(1, 0, 2, 3, 4).reshape(B, S, H * Dh)


def neighborhood_attention(q, k, v, k_ctx, v_ctx, rpb):
    B, T, H, Dh = q.shape
    scale = Dh ** -0.5
    rows = T // GRID_W
    kr = min(WIN_ROWS, rows)
    r = jnp.arange(rows)
    rs = jnp.clip(r - kr // 2, 0, rows - kr)
    key_rows = rs[:, None] + jnp.arange(kr)[None, :]
    col = jnp.arange(GRID_W)
    cs = jnp.clip(col - WIN_COLS // 2, 0, GRID_W - WIN_COLS)
    in_win = (col[None, :] >= cs[:, None]) & (col[None, :] < cs[:, None] + WIN_COLS)
    dr_idx = key_rows - r[:, None] + (WIN_ROWS - 1)
    dc_idx = jnp.clip(col[None, :] - col[:, None] + (WIN_COLS - 1), 0, 2 * WIN_COLS - 2)
    bias = rpb.astype(jnp.float32)[:, dr_idx][:, :, :, dc_idx]
    bias = jnp.where(in_win[None, None, None], bias, -jnp.inf)
    bias = bias.transpose(1, 0, 3, 2, 4).reshape(rows, H, GRID_W, kr * GRID_W)
    kg = k.reshape(B, rows, GRID_W, H, Dh)[:, key_rows].reshape(B, rows, kr * GRID_W, H, Dh)
    vg = v.reshape(B, rows, GRID_W, H, Dh)[:, key_rows].reshape(B, rows, kr * GRID_W, H, Dh)
    qg = q.reshape(B, rows, GRID_W, H, Dh)
    n_loc = kr * GRID_W

    def row_block(args):
        qr, kb, vb, br = args
        s_loc = jnp.einsum('bqhd,bkhd->bhqk', qr, kb).astype(jnp.float32) * scale + br[None]
        s_ctx = jnp.einsum('bqhd,bphd->bhqp', qr, k_ctx).astype(jnp.float32) * scale
        p = jax.nn.softmax(jnp.concatenate([s_loc, s_ctx], axis=-1), axis=-1).astype(v.dtype)
        return (jnp.einsum('bhqk,bkhd->bqhd', p[..., :n_loc], vb)
                + jnp.einsum('bhqp,bphd->bqhd', p[..., n_loc:], v_ctx))

    out = lax.map(row_block, (qg.transpose(1, 0, 2, 3, 4), kg.transpose(1, 0, 2, 3, 4),
                              vg.transpose(1, 0, 2, 3, 4), bias))
    return out.transpose(1, 0, 2, 3, 4).reshape(B, T, H * Dh)


def mlstm_scan(q, k, v, log_i, log_f, state):
    B, T, H, D = q.shape
    nc = T // ML_CHUNK

    def chunks(a):
        return a.reshape(B, nc, ML_CHUNK, H, D).transpose(1, 0, 3, 2, 4)

    def gchunks(a):
        return a.reshape(B, nc, ML_CHUNK, H).transpose(1, 0, 3, 2)

    causal = jnp.tril(jnp.ones((ML_CHUNK, ML_CHUNK), dtype=bool))

    def step(carry, xs):
        C0, n0, m0 = carry
        qc, kc, vc, li, lf = xs
        b = jnp.cumsum(lf, axis=-1)
        dmat = jnp.where(causal, b[..., :, None] - b[..., None, :] + li[..., None, :], -jnp.inf)
        inter = b + m0[..., None]
        m = jnp.maximum(inter, jnp.max(dmat, axis=-1))
        w_inter = jnp.exp(inter - m)
        s = jnp.einsum('bhtd,bhsd->bhts', qc, kc) * jnp.exp(dmat - m[..., None])
        num = (jnp.einsum('bhts,bhse->bhte', s, vc)
               + w_inter[..., None] * jnp.einsum('bhtd,bhde->bhte', qc, C0))
        den = jnp.sum(s, axis=-1) + w_inter * jnp.einsum('bhtd,bhd->bht', qc, n0)
        h = num / jnp.maximum(jnp.abs(den), jnp.exp(-m))[..., None]
        b_last = b[..., -1]
        w_end = b_last[..., None] - b + li
        m_new = jnp.maximum(b_last + m0, jnp.max(w_end, axis=-1))
        decay = jnp.exp(b_last + m0 - m_new)
        w = jnp.exp(w_end - m_new[..., None])
        C_new = decay[..., None, None] * C0 + jnp.einsum('bhs,bhsd,bhse->bhde', w, kc, vc)
        n_new = decay[..., None] * n0 + jnp.einsum('bhs,bhsd->bhd', w, kc)
        return (C_new, n_new, m_new), h

    state, h = lax.scan(step, state, (chunks(q), chunks(k), chunks(v), gchunks(log_i), gchunks(log_f)))
    return h.transpose(1, 0, 3, 2, 4).reshape(B, T, H, D), state


def mlstm_bidir(q, k, v, gates, st_f, st_b):
    li_f, f_f, li_b, f_b = jnp.split(gates, 4, axis=-1)
    h_f, st_f = mlstm_scan(q, k, v, li_f, jax.nn.log_sigmoid(f_f), st_f)

    def rev(a):
        return jnp.flip(a, axis=1)

    h_b, st_b = mlstm_scan(rev(q), rev(k), rev(v), rev(li_b), rev(jax.nn.log_sigmoid(f_b)), st_b)
    return h_f + rev(h_b), st_f, st_b


def conv_ffn(h, w_up, conv_w, conv_b, w_down):
    u = h @ w_up
    up = jnp.pad(u, ((0, 0), (1, 1), (0, 0)))
    u = up[:, :-2] * conv_w[0] + up[:, 1:-1] * conv_w[1] + up[:, 2:] * conv_w[2] + conv_b
    gate, val = jnp.split(u, 2, axis=-1)
    return (jax.nn.silu(gate) * val) @ w_down


def mixer(h, w_in, b_gate, q_g, k_g, ml_g, w_out, attend, latent, st_f, st_b):
    B, T, _ = h.shape
    f32 = jnp.float32
    splits = [ATT_WIDTH, 2 * ATT_WIDTH, 3 * ATT_WIDTH, 3 * ATT_WIDTH + ML_WIDTH,
              3 * ATT_WIDTH + 2 * ML_WIDTH, 3 * ATT_WIDTH + 3 * ML_WIDTH, 3 * ATT_WIDTH + 4 * ML_WIDTH]
    aq, ak, av, mq, mk, mv, mo, gt = jnp.split(h @ w_in, splits, axis=-1)
    aq = rmsnorm(aq.reshape(B, T, N_ATT_HEADS, ATT_HEAD_DIM), q_g)
    ak = rmsnorm(ak.reshape(B, T, N_ATT_HEADS, ATT_HEAD_DIM), k_g)
    av = av.reshape(B, T, N_ATT_HEADS, ATT_HEAD_DIM)
    o_att = attend(aq, ak, av)
    mq = mq.reshape(B, T, N_ML_HEADS, ML_HEAD_DIM).astype(f32)
    mk = mk.reshape(B, T, N_ML_HEADS, ML_HEAD_DIM).astype(f32) * ML_HEAD_DIM ** -0.5
    mv = mv.reshape(B, T, N_ML_HEADS, ML_HEAD_DIM).astype(f32)
    if latent:
        mq = axial_rope(mq)
        mk = axial_rope(mk)
    gates = gt.astype(f32) + b_gate.astype(f32)
    h_ml, st_f, st_b = mlstm_bidir(mq, mk, mv, gates, st_f, st_b)
    o_gate = jax.nn.sigmoid(mo.astype(f32)).reshape(B, T, N_ML_HEADS, ML_HEAD_DIM)
    o_ml = rmsnorm(h_ml, ml_g.reshape(N_ML_HEADS, ML_HEAD_DIM).astype(f32)) * o_gate
    o = jnp.concatenate([o_att, o_ml.reshape(B, T, ML_WIDTH).astype(h.dtype)], axis=-1)
    return o @ w_out, ak, av, st_f, st_b


def layer(x, cvec, attend, latent, st_f, st_b, w_mod, b_mod, norm1_g, norm2_g, w_in, b_gate,
          q_g, k_g, ml_g, w_out, w_up, conv_w, conv_b, w_down):
    sh1, sc1, g1, sh2, sc2, g2 = adaln(cvec, w_mod, b_mod)
    h = rmsnorm(x, norm1_g) * (1 + sc1) + sh1
    out, ak, av, st_f, st_b = mixer(h, w_in, b_gate, q_g, k_g, ml_g, w_out, attend, latent, st_f, st_b)
    x = x + g1 * out
    h = rmsnorm(x, norm2_g) * (1 + sc2) + sh2
    x = x + g2 * conv_ffn(h, w_up, conv_w, conv_b, w_down)
    return x, ak, av, st_f, st_b


def setup_inputs(seed: int = 0) -> dict:
    key = jax.random.key(seed)
    ks = jax.random.split(key, 26)
    f32 = jnp.float32

    def nrm(k, shape, s=1.0):
        return jax.random.normal(k, shape, f32) * s

    L = DEPTH
    i_bias = nrm(ks[0], (L, 2, 1, N_ML_HEADS), 0.1)
    f_bias = jnp.linspace(3.0, 6.0, N_ML_HEADS, dtype=f32) + nrm(ks[1], (L, 2, 1, N_ML_HEADS), 0.1)
    b_gate = jnp.concatenate([i_bias, f_bias], axis=2).reshape(L, N_GATES)
    return {
        'x_prompt': nrm(ks[2], (BATCH, SEQ, D_MODEL)),
        'x_sample': nrm(ks[3], (DEC_BATCH, DEC_SEQ, D_MODEL)),
        'cache_k': nrm(ks[4], (DEC_BATCH, L, PAST_LEN, N_ATT_HEADS, ATT_HEAD_DIM)),
        'cache_v': nrm(ks[5], (DEC_BATCH, L, PAST_LEN, N_ATT_HEADS, ATT_HEAD_DIM)),
        'state_C': nrm(ks[6], (DEC_BATCH, L, 2, N_ML_HEADS, ML_HEAD_DIM, ML_HEAD_DIM), 0.1),
        'state_n': nrm(ks[7], (DEC_BATCH, L, 2, N_ML_HEADS, ML_HEAD_DIM), 0.1),
        'state_m': nrm(ks[8], (DEC_BATCH, L, 2, N_ML_HEADS), 0.5),
        'c': nrm(ks[9], (DEC_BATCH, D_MODEL)),
        'c_ctx': nrm(ks[10], (D_MODEL,)),
        'w_mod': nrm(ks[11], (L, D_MODEL, 6 * D_MODEL), 0.5 * D_MODEL ** -0.5),
        'b_mod': nrm(ks[12], (L, 6 * D_MODEL), 0.02),
        'norm1_g': 1.0 + nrm(ks[13], (L, D_MODEL), 0.02),
        'norm2_g': 1.0 + nrm(ks[14], (L, D_MODEL), 0.02),
        'w_in': nrm(ks[15], (L, D_MODEL, PROJ_WIDTH), D_MODEL ** -0.5),
        'b_gate': b_gate,
        'q_norm_g': 1.0 + nrm(ks[16], (L, ATT_HEAD_DIM), 0.02),
        'k_norm_g': 1.0 + nrm(ks[17], (L, ATT_HEAD_DIM), 0.02),
        'rpb': nrm(ks[18], (L, N_ATT_HEADS, 2 * WIN_ROWS - 1, 2 * WIN_COLS - 1), 0.1),
        'ml_norm_g': 1.0 + nrm(ks[19], (L, ML_WIDTH), 0.02),
        'w_out': nrm(ks[20], (L, MIX_WIDTH, D_MODEL), MIX_WIDTH ** -0.5),
        'w_up': nrm(ks[21], (L, D_MODEL, 2 * D_FF), D_MODEL ** -0.5),
        'conv_w': nrm(ks[22], (L, CONV_WIDTH, 2 * D_FF), CONV_WIDTH ** -0.5),
        'conv_b': nrm(ks[23], (L, 2 * D_FF), 0.02),
        'w_down': nrm(ks[24], (L, D_FF, D_MODEL), D_FF ** -0.5),
    }


def reference(x_prompt, x_sample, cache_k, cache_v, state_C, state_n, state_m, c, c_ctx,
              w_mod, b_mod, norm1_g, norm2_g, w_in, b_gate, q_norm_g, k_norm_g, rpb, ml_norm_g,
              w_out, w_up, conv_w, conv_b, w_down):
    f32 = jnp.float32
    x = x_prompt
    B = x_prompt.shape[0]
    zero_state = (jnp.zeros((B, N_ML_HEADS, ML_HEAD_DIM, ML_HEAD_DIM), f32),
                  jnp.zeros((B, N_ML_HEADS, ML_HEAD_DIM), f32),
                  jnp.zeros((B, N_ML_HEADS), f32))
    ks_, vs_, Cs_, ns_, ms_ = [], [], [], [], []
    for l in range(DEPTH):
        x, ak, av, st_f, st_b = layer(
            x, c_ctx[None, :], context_attention, False, zero_state, zero_state,
            w_mod[l], b_mod[l], norm1_g[l], norm2_g[l], w_in[l], b_gate[l], q_norm_g[l], k_norm_g[l],
            ml_norm_g[l], w_out[l], w_up[l], conv_w[l], conv_b[l], w_down[l])
        ks_.append(ak)
        vs_.append(av)
        Cs_.append(jnp.stack([st_f[0], st_b[0]], axis=1))
        ns_.append(jnp.stack([st_f[1], st_b[1]], axis=1))
        ms_.append(jnp.stack([st_f[2], st_b[2]], axis=1))
    y_prompt = x
    new_cache_k = jnp.stack(ks_, axis=1)
    new_cache_v = jnp.stack(vs_, axis=1)
    new_state_C = jnp.stack(Cs_, axis=1)
    new_state_n = jnp.stack(ns_, axis=1)
    new_state_m = jnp.stack(ms_, axis=1)
    x = x_sample
    for l in range(DEPTH):
        attend = functools.partial(neighborhood_attention, k_ctx=cache_k[:, l], v_ctx=cache_v[:, l], rpb=rpb[l])
        st_f = (state_C[:, l, 0].astype(f32), state_n[:, l, 0].astype(f32), state_m[:, l, 0].astype(f32))
        st_b = (state_C[:, l, 1].astype(f32), state_n[:, l, 1].astype(f32), state_m[:, l, 1].astype(f32))
        x, _, _, _, _ = layer(
            x, c, attend, True, st_f, st_b,
            w_mod[l], b_mod[l], norm1_g[l], norm2_g[l], w_in[l], b_gate[l], q_norm_g[l], k_norm_g[l],
            ml_norm_g[l], w_out[l], w_up[l], conv_w[l], conv_b[l], w_down[l])
    y_sample = x
    return (y_prompt, y_sample, new_cache_k, new_cache_v, new_state_C, new_state_n, new_state_m)
```

```python
import functools

import jax
import jax.numpy as jnp
from jax import lax
from jax.experimental import pallas as pl
from jax.experimental.pallas import tpu as pltpu

F32 = jnp.float32
BF16 = jnp.bfloat16

GRID_W = 64
N_ATT_HEADS = 8
ATT_HEAD_DIM = 64
ATT_WIDTH = N_ATT_HEADS * ATT_HEAD_DIM
WIN_ROWS = 8
WIN_COLS = 16
N_ML_HEADS = 4
ML_HEAD_DIM = 128
ML_WIDTH = N_ML_HEADS * ML_HEAD_DIM
ROPE_THETA = 10000.0
EPS = 1e-6

LANES = 128
MXU_WIDTH = 256
HALO = 16
TOKEN_TILE = 512
VMEM_LIMIT = 56 * 1024 * 1024


def _dot(a, b):
    return jnp.dot(a, b, preferred_element_type=F32)


def _dot_nt(a, b):
    return lax.dot_general(a, b, (((1,), (1,)), ((), ())), preferred_element_type=F32)


def _dot_tn(a, b):
    return lax.dot_general(a, b, (((0,), (0,)), ((), ())), preferred_element_type=F32)


def _const_spec(shape):
    nd = len(shape)
    return pl.BlockSpec(shape, lambda *_: (0,) * nd, pipeline_mode=pl.Buffered(1))


def _params(n_axes):
    return pltpu.CompilerParams(dimension_semantics=("arbitrary",) * n_axes,
                                vmem_limit_bytes=VMEM_LIMIT)


def _token_tile(n_tok, seq_len):
    tm = min(TOKEN_TILE, n_tok)
    while n_tok % tm or (seq_len % tm and tm % seq_len):
        tm //= 2
    return tm


def _log_sigmoid(x):
    return jnp.minimum(x, 0.0) - jnp.log1p(jnp.exp(-jnp.abs(x)))


def _mod_kernel(c_ref, w_ref, b_ref, o_ref):
    c = c_ref[...]
    s = c * jax.nn.sigmoid(c)
    o_ref[...] = _dot(s.astype(BF16), w_ref[...].astype(BF16)) + b_ref[...]


def _modulation(cvecs, w_mod, b_mod):
    r, d = cvecs.shape
    n = w_mod.shape[1]
    tn = d
    out = pl.pallas_call(
        _mod_kernel,
        grid=(n // tn,),
        in_specs=[pl.BlockSpec((r, d), lambda j: (0, 0)),
                  pl.BlockSpec((d, tn), lambda j: (0, j)),
                  pl.BlockSpec((1, tn), lambda j: (0, j))],
        out_specs=pl.BlockSpec((r, tn), lambda j: (0, j)),
        out_shape=jax.ShapeDtypeStruct((r, n), F32),
        compiler_params=_params(1),
        name="adaln_mod",
    )(cvecs, w_mod, b_mod.reshape(1, n))
    return out.reshape(r, 1, n)


def _rope(x, cos, sin_signed):
    lane = lax.broadcasted_iota(jnp.int32, x.shape, 1)
    partner = jnp.where((lane & 32) == 0, pltpu.roll(x, LANES - 32, 1), pltpu.roll(x, 32, 1))
    return x * cos + partner * sin_signed


def _inproj_kernel(*refs, latent):
    if latent:
        (x_ref, mod_ref, g1_ref, watt_ref, wml_ref, wg_ref, bg_ref, qg_ref, kg_ref, bd_ref, cos_ref, sin_ref,
         aq_ref, ak_ref, av_ref, mq_ref, mk_ref, mv_ref, mo_ref, gt_ref) = refs
    else:
        (x_ref, mod_ref, g1_ref, watt_ref, wml_ref, wg_ref, bg_ref, qg_ref, kg_ref, bd_ref,
         aq_ref, ak_ref, av_ref, mq_ref, mk_ref, mv_ref, mo_ref, gt_ref) = refs
    x = x_ref[...]
    d = x.shape[-1]
    mod = mod_ref[0]
    sh1, sc1 = mod[:, 0:d], mod[:, d:2 * d]
    y = x * lax.rsqrt(jnp.mean(x * x, axis=-1, keepdims=True) + EPS)
    h = (y * g1_ref[...]) * (1.0 + sc1) + sh1
    hb = h.astype(BF16)

    def head_norm(a, g):
        ss = _dot((a * a).astype(BF16), bd_ref[...])
        return a * lax.rsqrt(ss * (1.0 / ATT_HEAD_DIM) + EPS) * g

    att = _dot(hb, watt_ref[...])
    w = ATT_WIDTH
    aq_ref[...] = (head_norm(att[:, 0:w], qg_ref[...]) * ATT_HEAD_DIM ** -0.5).astype(aq_ref.dtype)
    ak_ref[...] = head_norm(att[:, w:2 * w], kg_ref[...]).astype(ak_ref.dtype)
    av_ref[...] = att[:, 2 * w:3 * w].astype(av_ref.dtype)

    w = ML_WIDTH
    mq = _dot(hb, wml_ref[:, 0:w])
    mk = _dot(hb, wml_ref[:, w:2 * w]) * ML_HEAD_DIM ** -0.5
    if latent:
        cos, sin = cos_ref[...], sin_ref[...]
        for hh in range(N_ML_HEADS):
            sl = slice(hh * ML_HEAD_DIM, (hh + 1) * ML_HEAD_DIM)
            mq_ref[:, sl] = _rope(mq[:, sl], cos, sin).astype(BF16)
            mk_ref[:, sl] = _rope(mk[:, sl], cos, sin).astype(BF16)
    else:
        mq_ref[...] = mq.astype(BF16)
        mk_ref[...] = mk.astype(BF16)
    mv_ref[...] = _dot(hb, wml_ref[:, 2 * w:3 * w]).astype(BF16)
    mo_ref[...] = _dot(hb, wml_ref[:, 3 * w:4 * w])
    gt_ref[...] = _dot(hb, wg_ref[...]) + bg_ref[...]


def _in_projection(x, mod, mod_row, seq_len, wts, rope, *, latent, kv_dtype):
    n_tok, d = x.shape
    tm = _token_tile(n_tok, seq_len)
    tiles_per_seq = max(seq_len // tm, 1)

    def tok(width):
        return pl.BlockSpec((tm, width), lambda i: (i, 0))

    in_specs = [tok(d),
                pl.BlockSpec((1, 1, mod.shape[-1]), lambda i: (mod_row(i, tm), 0, 0)),
                _const_spec((1, d)),
                _const_spec(wts["w_att"].shape), _const_spec(wts["w_ml"].shape),
                _const_spec(wts["w_g"].shape), _const_spec((1, LANES)),
                _const_spec((1, ATT_WIDTH)), _const_spec((1, ATT_WIDTH)),
                _const_spec((ATT_WIDTH, ATT_WIDTH))]
    args = [x, mod, wts["norm1_g"], wts["w_att"], wts["w_ml"], wts["w_g"], wts["b_g"],
            wts["q_g"], wts["k_g"], wts["blockdiag"]]
    if latent:
        in_specs += [pl.BlockSpec((tm, LANES), lambda i: (i % tiles_per_seq, 0))] * 2
        args += [rope[0], rope[1]]
    out_shape = [jax.ShapeDtypeStruct((n_tok, ATT_WIDTH), BF16),
                 jax.ShapeDtypeStruct((n_tok, ATT_WIDTH), kv_dtype),
                 jax.ShapeDtypeStruct((n_tok, ATT_WIDTH), kv_dtype),
                 jax.ShapeDtypeStruct((n_tok, ML_WIDTH), BF16),
                 jax.ShapeDtypeStruct((n_tok, ML_WIDTH), BF16),
                 jax.ShapeDtypeStruct((n_tok, ML_WIDTH), BF16),
                 jax.ShapeDtypeStruct((n_tok, ML_WIDTH), F32),
                 jax.ShapeDtypeStruct((n_tok, LANES), F32)]
    out_specs = [tok(ATT_WIDTH)] * 3 + [tok(ML_WIDTH)] * 4 + [tok(LANES)]
    return pl.pallas_call(
        functools.partial(_inproj_kernel, latent=latent),
        grid=(n_tok // tm,),
        in_specs=in_specs,
        out_specs=out_specs,
        out_shape=out_shape,
        compiler_params=_params(1),
        name="in_proj_latent" if latent else "in_proj_ctx",
    )(*args)


def _ctx_attn_kernel(q_ref, k_ref, v_ref, o_ref):
    for hh in range(N_ATT_HEADS):
        sl = slice(hh * ATT_HEAD_DIM, (hh + 1) * ATT_HEAD_DIM)
        q = q_ref[:, sl]
        k = k_ref[:, sl].astype(BF16)
        v = v_ref[:, sl].astype(BF16)
        s = _dot_nt(q, k)
        p = jnp.exp(s - jnp.max(s, axis=-1, keepdims=True))
        l = jnp.sum(p, axis=-1, keepdims=True)
        o_ref[:, sl] = (_dot(p.astype(BF16), v) / l).astype(o_ref.dtype)


def _context_attention(q, k, v, seq_len):
    n_tok = q.shape[0]
    spec = pl.BlockSpec((seq_len, ATT_WIDTH), lambda b: (b, 0))
    return pl.pallas_call(
        _ctx_attn_kernel,
        grid=(n_tok // seq_len,),
        in_specs=[spec, spec, spec],
        out_specs=spec,
        out_shape=jax.ShapeDtypeStruct((n_tok, ATT_WIDTH), BF16),
        compiler_params=_params(1),
        name="ctx_attn",
    )(q, k, v)


def _nbr_attn_kernel(q_ref, k_ref, v_ref, kc_ref, vc_ref, bias_ref, o_ref, *, rows, kr):
    r = pl.program_id(1)
    rs = jnp.clip(r - kr // 2, 0, rows - kr)
    start = pl.multiple_of(rs * GRID_W, GRID_W)
    n_loc = kr * GRID_W
    for hh in range(N_ATT_HEADS):
        sl = slice(hh * ATT_HEAD_DIM, (hh + 1) * ATT_HEAD_DIM)
        q = q_ref[:, sl]
        kw = k_ref[pl.ds(start, n_loc), sl]
        vw = v_ref[pl.ds(start, n_loc), sl]
        kc = kc_ref[0, :, sl].astype(BF16)
        vc = vc_ref[0, :, sl].astype(BF16)
        s_loc = _dot_nt(q, kw) + bias_ref[0, hh]
        s_ctx = _dot_nt(q, kc)
        m = jnp.maximum(jnp.max(s_loc, axis=-1, keepdims=True), jnp.max(s_ctx, axis=-1, keepdims=True))
        p_loc = jnp.exp(s_loc - m)
        p_ctx = jnp.exp(s_ctx - m)
        l = jnp.sum(p_loc, axis=-1, keepdims=True) + jnp.sum(p_ctx, axis=-1, keepdims=True)
        o = _dot(p_loc.astype(BF16), vw) + _dot(p_ctx.astype(BF16), vc)
        o_ref[:, sl] = (o / l).astype(o_ref.dtype)


def _nbr_bias_table(rpb, rows):
    kr = min(WIN_ROWS, rows)
    col = jnp.arange(GRID_W)
    cs = jnp.clip(col - WIN_COLS // 2, 0, GRID_W - WIN_COLS)
    in_win = (col[None, :] >= cs[:, None]) & (col[None, :] < cs[:, None] + WIN_COLS)
    dc_idx = jnp.clip(col[None, :] - col[:, None] + (WIN_COLS - 1), 0, 2 * WIN_COLS - 2)
    e = jnp.arange(kr)
    dr_idx = jnp.arange(kr)[None, :] - e[:, None] + (WIN_ROWS - 1)
    b = rpb.astype(F32)[:, dr_idx][:, :, :, dc_idx]
    b = jnp.where(in_win[None, None, None], b, -jnp.inf)
    return b.transpose(1, 0, 3, 2, 4).reshape(kr, N_ATT_HEADS, GRID_W, kr * GRID_W)


def _neighborhood_attention(q, k, v, k_ctx, v_ctx, rpb, n_batch, seq_len):
    rows = seq_len // GRID_W
    kr = min(WIN_ROWS, rows)
    bias = _nbr_bias_table(rpb, rows)
    past = k_ctx.shape[1]

    def bias_idx(b, r):
        return (r - jnp.clip(r - kr // 2, 0, rows - kr), 0, 0, 0)

    seq_spec = pl.BlockSpec((seq_len, ATT_WIDTH), lambda b, r: (b, 0))
    ctx_spec = pl.BlockSpec((1, past, ATT_WIDTH), lambda b, r: (b, 0, 0))
    row_spec = pl.BlockSpec((GRID_W, ATT_WIDTH), lambda b, r: (b * rows + r, 0))
    return pl.pallas_call(
        functools.partial(_nbr_attn_kernel, rows=rows, kr=kr),
        grid=(n_batch, rows),
        in_specs=[row_spec, seq_spec, seq_spec, ctx_spec, ctx_spec,
                  pl.BlockSpec((1, N_ATT_HEADS, GRID_W, kr * GRID_W), bias_idx)],
        out_specs=row_spec,
        out_shape=jax.ShapeDtypeStruct((n_batch * seq_len, ATT_WIDTH), BF16),
        compiler_params=_params(2),
        name="nbr_attn",
    )(q, k, v, k_ctx, v_ctx, bias)


def _mlstm_chunk(q, k, v, li_col, lf_col, c0, n0, m0, *, reverse):
    n = q.shape[0]
    t_idx = lax.broadcasted_iota(jnp.int32, (n, n), 0)
    s_idx = lax.broadcasted_iota(jnp.int32, (n, n), 1)
    eye = t_idx == s_idx
    mask = (s_idx >= t_idx) if reverse else (s_idx <= t_idx)
    mask_t = (t_idx >= s_idx) if reverse else (t_idx <= s_idx)
    b_row = jnp.sum(jnp.where(mask_t, lf_col, 0.0), axis=0, keepdims=True)
    li_row = jnp.sum(jnp.where(eye, li_col, 0.0), axis=0, keepdims=True)
    b_col = jnp.sum(jnp.where(eye, b_row, 0.0), axis=1, keepdims=True)
    dmat = jnp.where(mask, b_col + (li_row - b_row), -jnp.inf)
    inter = b_col + m0
    m = jnp.maximum(inter, jnp.max(dmat, axis=1, keepdims=True))
    w_inter = jnp.exp(inter - m)
    s = _dot_nt(q, k) * jnp.exp(dmat - m)
    num = _dot(s.astype(BF16), v) + w_inter * _dot(q, c0.astype(BF16))
    qn = jnp.sum(q.astype(F32) * n0, axis=1, keepdims=True)
    den = jnp.sum(s, axis=1, keepdims=True) + w_inter * qn
    h = num / jnp.maximum(jnp.abs(den), jnp.exp(-m))
    b_last = jnp.sum(lf_col, axis=0, keepdims=True)
    w_end = b_last + (li_col - b_col)
    m_new = jnp.maximum(b_last + m0, jnp.max(w_end, axis=0, keepdims=True))
    decay = jnp.exp(b_last + m0 - m_new)
    kw = k.astype(F32) * jnp.exp(w_end - m_new)
    c_new = decay * c0 + _dot_tn(kw.astype(BF16), v)
    n_new = decay * n0 + jnp.sum(kw, axis=0, keepdims=True)
    return h, c_new, n_new, m_new


def _mlstm_kernel(*refs, chunk, n_chunks, has_state):
    if has_state:
        (q_ref, k_ref, v_ref, mo_ref, gt_ref, g_ref, c0_ref, n0_ref, m0_ref,
         o_ref, c_ref, n_ref, m_ref, hf_ref, hb_ref) = refs
        c_ref[...] = c0_ref[...]
        n_ref[...] = n0_ref[...]
        m_ref[...] = m0_ref[...]
    else:
        (q_ref, k_ref, v_ref, mo_ref, gt_ref, g_ref,
         o_ref, c_ref, n_ref, m_ref, hf_ref, hb_ref) = refs
        c_ref[...] = jnp.zeros_like(c_ref)
        n_ref[...] = jnp.zeros_like(n_ref)
        m_ref[...] = jnp.zeros_like(m_ref)
    dh = ML_HEAD_DIM

    def run_chunk(c, hh, direction):
        rows = pl.ds(pl.multiple_of(c * chunk, chunk), chunk)
        sl = slice(hh * dh, (hh + 1) * dh)
        gt = gt_ref[rows, :]
        lane = 4 * hh + 2 * direction
        h, c_new, n_new, m_new = _mlstm_chunk(
            q_ref[rows, sl], k_ref[rows, sl], v_ref[rows, sl],
            gt[:, lane:lane + 1], _log_sigmoid(gt[:, lane + 1:lane + 2]),
            c_ref[0, direction, hh], n_ref[0, direction, hh], m_ref[0, direction, hh][:, 0:1],
            reverse=direction == 1)
        (hb_ref if direction else hf_ref)[rows, sl] = h
        c_ref[0, direction, hh] = c_new
        n_ref[0, direction, hh] = n_new
        m_ref[0, direction, hh] = jnp.broadcast_to(m_new, (1, dh))

    def body(i, _):
        for hh in range(N_ML_HEADS):
            run_chunk(i, hh, 0)
            run_chunk(n_chunks - 1 - i, hh, 1)
        return 0

    def finish(c, _):
        rows = pl.ds(pl.multiple_of(c * chunk, chunk), chunk)
        for hh in range(N_ML_HEADS):
            sl = slice(hh * dh, (hh + 1) * dh)
            hs = hf_ref[rows, sl] + hb_ref[rows, sl]
            y = hs * lax.rsqrt(jnp.mean(hs * hs, axis=-1, keepdims=True) + EPS) * g_ref[:, sl]
            o_ref[rows, sl] = (y * jax.nn.sigmoid(mo_ref[rows, sl])).astype(o_ref.dtype)
        return 0

    if n_chunks == 1:
        body(0, 0)
        finish(0, 0)
    else:
        lax.fori_loop(0, n_chunks, body, 0)
        lax.fori_loop(0, n_chunks, finish, 0)


def _mlstm(mq, mk, mv, mo, gt, ml_g, state, n_batch, seq_len, chunk):
    dh = ML_HEAD_DIM
    nh = N_ML_HEADS
    has_state = state is not None
    seq = pl.BlockSpec((seq_len, nh * dh), lambda b: (b, 0))
    st_c = pl.BlockSpec((1, 2, nh, dh, dh), lambda b: (b, 0, 0, 0, 0))
    st_v = pl.BlockSpec((1, 2, nh, 1, dh), lambda b: (b, 0, 0, 0, 0))
    in_specs = [seq, seq, seq, seq, pl.BlockSpec((seq_len, LANES), lambda b: (b, 0)), _const_spec((1, nh * dh))]
    args = [mq, mk, mv, mo, gt, ml_g]
    if has_state:
        in_specs += [st_c, st_v, st_v]
        args += list(state)
    out_shape = [jax.ShapeDtypeStruct((n_batch * seq_len, nh * dh), BF16),
                 jax.ShapeDtypeStruct((n_batch, 2, nh, dh, dh), F32),
                 jax.ShapeDtypeStruct((n_batch, 2, nh, 1, dh), F32),
                 jax.ShapeDtypeStruct((n_batch, 2, nh, 1, dh), F32)]
    return pl.pallas_call(
        functools.partial(_mlstm_kernel, chunk=chunk, n_chunks=seq_len // chunk, has_state=has_state),
        grid=(n_batch,),
        in_specs=in_specs,
        out_specs=[seq, st_c, st_v, st_v],
        out_shape=out_shape,
        scratch_shapes=[pltpu.VMEM((seq_len, nh * dh), F32), pltpu.VMEM((seq_len, nh * dh), F32)],
        compiler_params=_params(1),
        name="mlstm_latent" if has_state else "mlstm_ctx",
    )(*args)


def _outproj_kernel(oa_ref, om_ref, x_ref, mod_ref, g2_ref, wa_ref, wm_ref, x1_ref, h2_ref):
    d = x_ref.shape[-1]
    mod = mod_ref[0]
    g1 = mod[:, 2 * d:3 * d]
    sh2, sc2 = mod[:, 3 * d:4 * d], mod[:, 4 * d:5 * d]
    out = _dot(oa_ref[...], wa_ref[...]) + _dot(om_ref[...], wm_ref[...])
    x1 = x_ref[...] + g1 * out
    x1_ref[...] = x1
    y = x1 * lax.rsqrt(jnp.mean(x1 * x1, axis=-1, keepdims=True) + EPS)
    h2_ref[...] = ((y * g2_ref[...]) * (1.0 + sc2) + sh2).astype(h2_ref.dtype)


def _out_projection(o_att, o_ml, x, mod, mod_row, seq_len, wts):
    n_tok, d = x.shape
    tm = _token_tile(n_tok, seq_len)

    def tok(width):
        return pl.BlockSpec((tm, width), lambda i: (i, 0))

    return pl.pallas_call(
        _outproj_kernel,
        grid=(n_tok // tm,),
        in_specs=[tok(ATT_WIDTH), tok(ML_WIDTH), tok(d),
                  pl.BlockSpec((1, 1, mod.shape[-1]), lambda i: (mod_row(i, tm), 0, 0)),
                  _const_spec((1, d)), _const_spec((ATT_WIDTH, d)), _const_spec((ML_WIDTH, d))],
        out_specs=[tok(d), tok(d)],
        out_shape=[jax.ShapeDtypeStruct((n_tok, d), F32), jax.ShapeDtypeStruct((n_tok, d), BF16)],
        compiler_params=_params(1),
        name="out_proj",
    )(o_att, o_ml, x, mod, wts["norm2_g"], wts["w_out_att"], wts["w_out_ml"])


def _ffn_kernel(h_ref, hprev_ref, hnext_ref, x1_ref, mod_ref, wg_ref, wv_ref, cwg_ref, cwv_ref,
                cbg_ref, cbv_ref, wd_ref, y_ref, lhs_ref, acc_ref, *, seq_len, n_col_tiles):
    tm, d = x1_ref.shape
    lhs_ref[0:tm, :] = h_ref[...]
    halo_row = lax.broadcasted_iota(jnp.int32, (HALO, d), 0)
    lhs_ref[tm:tm + HALO, :] = jnp.where(halo_row < HALO // 2, hnext_ref[...], hprev_ref[...])
    acc_ref[...] = jnp.zeros_like(acc_ref)
    pos = (pl.program_id(0) * tm + lax.broadcasted_iota(jnp.int32, (tm, 1), 0)) % seq_len
    first = pos == 0
    last = pos == seq_len - 1

    def conv(u, cw, cb):
        prev = jnp.where(first, 0.0, pltpu.roll(u, 1, 0)[0:tm])
        nxt = jnp.where(last, 0.0, pltpu.roll(u, tm + HALO - 1, 0)[0:tm])
        return prev * cw[0:1] + u[0:tm] * cw[1:2] + nxt * cw[2:3] + cb

    def body(j, _):
        lhs = lhs_ref[...]
        gate = conv(_dot(lhs, wg_ref[j]), cwg_ref[j], cbg_ref[j])
        val = conv(_dot(lhs, wv_ref[j]), cwv_ref[j], cbv_ref[j])
        act = (gate * jax.nn.sigmoid(gate)) * val
        acc_ref[...] += _dot(act.astype(BF16), wd_ref[j])
        return 0

    lax.fori_loop(0, n_col_tiles, body, 0)
    g2 = mod_ref[0][:, 5 * d:6 * d]
    y_ref[...] = x1_ref[...] + g2 * acc_ref[...]


def _conv_ffn(h2, x1, mod, mod_row, seq_len, wts):
    n_tok, d = x1.shape
    tm = _token_tile(n_tok, seq_len)
    hpt = tm // HALO
    n_halo = n_tok // HALO
    nct, _, tn = wts["w_up_gate"].shape

    def tok(width):
        return pl.BlockSpec((tm, width), lambda i: (i, 0))

    in_specs = [tok(d),
                pl.BlockSpec((HALO, d), lambda i: (jnp.maximum(i * hpt - 1, 0), 0)),
                pl.BlockSpec((HALO, d), lambda i: (jnp.minimum((i + 1) * hpt, n_halo - 1), 0)),
                tok(d),
                pl.BlockSpec((1, 1, mod.shape[-1]), lambda i: (mod_row(i, tm), 0, 0)),
                _const_spec((nct, d, tn)), _const_spec((nct, d, tn)),
                _const_spec((nct, 3, tn)), _const_spec((nct, 3, tn)),
                _const_spec((nct, 1, tn)), _const_spec((nct, 1, tn)),
                _const_spec((nct, tn, d))]
    return pl.pallas_call(
        functools.partial(_ffn_kernel, seq_len=seq_len, n_col_tiles=nct),
        grid=(n_tok // tm,),
        in_specs=in_specs,
        out_specs=tok(d),
        out_shape=jax.ShapeDtypeStruct((n_tok, d), F32),
        scratch_shapes=[pltpu.VMEM((tm + HALO, d), BF16), pltpu.VMEM((tm, d), F32)],
        compiler_params=_params(1),
        name="conv_ffn",
    )(h2, h2, h2, x1, mod, wts["w_up_gate"], wts["w_up_val"], wts["conv_w_gate"], wts["conv_w_val"],
      wts["conv_b_gate"], wts["conv_b_val"], wts["w_down"])


def _layer_weights(norm1_g, norm2_g, w_in, b_gate, q_g, k_g, ml_g, w_out, w_up, conv_w, conv_b, w_down):
    d = w_in.shape[0]
    a, m, nh = ATT_WIDTH, ML_WIDTH, N_ML_HEADS
    w_gates = w_in[:, 3 * a + 4 * m:].reshape(d, 4, nh).transpose(0, 2, 1).reshape(d, 4 * nh)
    w_g = jnp.pad(w_gates, ((0, 0), (0, LANES - 4 * nh)))
    b_g = jnp.pad(b_gate.astype(F32).reshape(4, nh).T.reshape(1, 4 * nh), ((0, 0), (0, LANES - 4 * nh)))
    head_id = jnp.arange(a) // ATT_HEAD_DIM
    d_ff = w_down.shape[0]
    tn = MXU_WIDTH
    nct = d_ff // tn

    def col_tiles(w):
        w = w.reshape(w.shape[0], 2, nct, tn).transpose(1, 2, 0, 3)
        return w[0], w[1]

    up_g, up_v = col_tiles(w_up.astype(BF16))
    cw_g, cw_v = col_tiles(conv_w.astype(F32))
    cb_g, cb_v = col_tiles(conv_b.astype(F32).reshape(1, -1))
    return {
        "norm1_g": norm1_g.reshape(1, d), "norm2_g": norm2_g.reshape(1, d),
        "w_att": w_in[:, :3 * a].astype(BF16), "w_ml": w_in[:, 3 * a:3 * a + 4 * m].astype(BF16),
        "w_g": w_g.astype(BF16), "b_g": b_g,
        "q_g": jnp.tile(q_g, N_ATT_HEADS).reshape(1, a), "k_g": jnp.tile(k_g, N_ATT_HEADS).reshape(1, a),
        "blockdiag": (head_id[:, None] == head_id[None, :]).astype(BF16),
        "ml_g": ml_g.reshape(1, m),
        "w_out_att": w_out[:a].astype(BF16), "w_out_ml": w_out[a:].astype(BF16),
        "w_up_gate": up_g, "w_up_val": up_v, "conv_w_gate": cw_g, "conv_w_val": cw_v,
        "conv_b_gate": cb_g, "conv_b_val": cb_v,
        "w_down": w_down.astype(BF16).reshape(nct, tn, w_down.shape[1]),
    }


def _rope_tables(seq_len):
    quarter = ML_HEAD_DIM // 4
    pos = jnp.arange(seq_len)
    inv_freq = ROPE_THETA ** (-jnp.arange(quarter, dtype=F32) / quarter)
    ang_r = (pos // GRID_W).astype(F32)[:, None] * inv_freq[None, :]
    ang_c = (pos % GRID_W).astype(F32)[:, None] * inv_freq[None, :]
    cos = jnp.concatenate([jnp.cos(ang_r)] * 2 + [jnp.cos(ang_c)] * 2, axis=-1)
    sin = jnp.concatenate([-jnp.sin(ang_r), jnp.sin(ang_r), -jnp.sin(ang_c), jnp.sin(ang_c)], axis=-1)
    return cos, sin


def _layer(x, mod, mod_row, n_batch, seq_len, wts, *, latent, ctx_kv=None, rpb=None, state=None, rope=None):
    chunk = min(MXU_WIDTH, seq_len)
    kv_dtype = BF16 if latent else F32
    aq, ak, av, mq, mk, mv, mo, gt = _in_projection(
        x, mod, mod_row, seq_len, wts, rope, latent=latent, kv_dtype=kv_dtype)
    if latent:
        o_att = _neighborhood_attention(aq, ak, av, ctx_kv[0], ctx_kv[1], rpb, n_batch, seq_len)
    else:
        o_att = _context_attention(aq, ak, av, seq_len)
    o_ml, c_f, n_f, m_f = _mlstm(mq, mk, mv, mo, gt, wts["ml_g"], state, n_batch, seq_len, chunk)
    x1, h2 = _out_projection(o_att, o_ml, x, mod, mod_row, seq_len, wts)
    y = _conv_ffn(h2, x1, mod, mod_row, seq_len, wts)
    return y, ak, av, (c_f, n_f[:, :, :, 0, :], m_f[:, :, :, 0, 0])


def kernel(x_prompt, x_sample, cache_k, cache_v, state_C, state_n, state_m, c, c_ctx, w_mod, b_mod, norm1_g,
           norm2_g, w_in, b_gate, q_norm_g, k_norm_g, rpb, ml_norm_g, w_out, w_up, conv_w, conv_b, w_down):
    batch, seq, d = x_prompt.shape
    dec_batch, dec_seq, _ = x_sample.shape
    depth = w_mod.shape[0]
    past = cache_k.shape[2]
    cvecs = jnp.concatenate([c_ctx[None, :], c], axis=0).astype(F32)
    rope = _rope_tables(dec_seq)

    def ctx_row(i, tm):
        return 0 * i

    def lat_row(i, tm):
        return 1 + (i * tm) // dec_seq

    xp = x_prompt.reshape(batch * seq, d)
    xs = x_sample.reshape(dec_batch * dec_seq, d)
    ks, vs, cs, ns, ms = [], [], [], [], []
    for l in range(depth):
        wts = _layer_weights(norm1_g[l], norm2_g[l], w_in[l], b_gate[l], q_norm_g[l], k_norm_g[l],
                             ml_norm_g[l], w_out[l], w_up[l], conv_w[l], conv_b[l], w_down[l])
        mod = _modulation(cvecs, w_mod[l], b_mod[l])

        xp, ak, av, st = _layer(xp, mod, ctx_row, batch, seq, wts, latent=False)
        ks.append(ak.reshape(batch, seq, N_ATT_HEADS, ATT_HEAD_DIM))
        vs.append(av.reshape(batch, seq, N_ATT_HEADS, ATT_HEAD_DIM))
        cs.append(st[0])
        ns.append(st[1])
        ms.append(st[2])

        state = (state_C[:, l].astype(F32),
                 state_n[:, l].astype(F32)[:, :, :, None, :],
                 jnp.broadcast_to(state_m[:, l].astype(F32)[:, :, :, None, None],
                                  (dec_batch, 2, N_ML_HEADS, 1, ML_HEAD_DIM)))
        ctx_kv = (cache_k[:, l].reshape(dec_batch, past, ATT_WIDTH), cache_v[:, l].reshape(dec_batch, past, ATT_WIDTH))
        xs, _, _, _ = _layer(xs, mod, lat_row, dec_batch, dec_seq, wts, latent=True, ctx_kv=ctx_kv,
                             rpb=rpb[l], state=state, rope=rope)
    return (xp.reshape(batch, seq, d), xs.reshape(dec_batch, dec_seq, d),
            jnp.stack(ks, axis=1), jnp.stack(vs, axis=1),
            jnp.stack(cs, axis=1), jnp.stack(ns, axis=1), jnp.stack(ms, axis=1))
```

```python
import functools

import jax
import jax.numpy as jnp
from jax import lax
from jax.experimental import pallas as pl
from jax.experimental.pallas import tpu as pltpu

F32 = jnp.float32
BF16 = jnp.bfloat16

GRID_W = 64
N_ATT_HEADS = 8
ATT_HEAD_DIM = 64
ATT_WIDTH = N_ATT_HEADS * ATT_HEAD_DIM
WIN_ROWS = 8
WIN_COLS = 16
N_ML_HEADS = 4
ML_HEAD_DIM = 128
ML_WIDTH = N_ML_HEADS * ML_HEAD_DIM
ROPE_THETA = 10000.0
EPS = 1e-6

LANES = 128
MXU_WIDTH = 256
ML_CHUNK = 256
HALO = 16
TOKEN_TILE = 512
VMEM_LIMIT = 56 * 1024 * 1024


def _dot(a, b):
    return jnp.dot(a, b, preferred_element_type=F32)


def _dot_nt(a, b):
    return lax.dot_general(a, b, (((1,), (1,)), ((), ())), preferred_element_type=F32)


def _dot_tn(a, b):
    return lax.dot_general(a, b, (((0,), (0,)), ((), ())), preferred_element_type=F32)


def _const_spec(shape):
    nd = len(shape)
    return pl.BlockSpec(shape, lambda *_: (0,) * nd, pipeline_mode=pl.Buffered(1))


def _params(n_axes):
    return pltpu.CompilerParams(dimension_semantics=("arbitrary",) * n_axes,
                                vmem_limit_bytes=VMEM_LIMIT)


def _token_tile(n_tok, seq_len):
    tm = min(TOKEN_TILE, n_tok)
    while n_tok % tm or (seq_len % tm and tm % seq_len):
        tm //= 2
    return tm


def _log_sigmoid(x):
    return jnp.minimum(x, 0.0) - jnp.log1p(jnp.exp(-jnp.abs(x)))


def _mod_kernel(c_ref, w_ref, b_ref, o_ref):
    c = c_ref[...]
    s = c * jax.nn.sigmoid(c)
    o_ref[...] = _dot(s.astype(BF16), w_ref[...].astype(BF16)) + b_ref[...]


def _modulation(cvecs, w_mod, b_mod):
    r, d = cvecs.shape
    n = w_mod.shape[1]
    tn = d
    out = pl.pallas_call(
        _mod_kernel,
        grid=(n // tn,),
        in_specs=[pl.BlockSpec((r, d), lambda j: (0, 0)),
                  pl.BlockSpec((d, tn), lambda j: (0, j)),
                  pl.BlockSpec((1, tn), lambda j: (0, j))],
        out_specs=pl.BlockSpec((r, tn), lambda j: (0, j)),
        out_shape=jax.ShapeDtypeStruct((r, n), F32),
        compiler_params=_params(1),
        name="adaln_mod",
    )(cvecs, w_mod, b_mod.reshape(1, n))
    return out.reshape(r, 1, n)


def _rope(x, cos, sin_signed):
    lane = lax.broadcasted_iota(jnp.int32, x.shape, 1)
    partner = jnp.where((lane & 32) == 0, pltpu.roll(x, LANES - 32, 1), pltpu.roll(x, 32, 1))
    return x * cos + partner * sin_signed


def _inproj_kernel(*refs, latent):
    if latent:
        (x_ref, mod_ref, g1_ref, watt_ref, wml_ref, wg_ref, bg_ref, qg_ref, kg_ref, bd_ref, cos_ref, sin_ref,
         aq_ref, ak_ref, av_ref, mq_ref, mk_ref, mv_ref, mo_ref, gt_ref) = refs
    else:
        (x_ref, mod_ref, g1_ref, watt_ref, wml_ref, wg_ref, bg_ref, qg_ref, kg_ref, bd_ref,
         aq_ref, ak_ref, av_ref, mq_ref, mk_ref, mv_ref, mo_ref, gt_ref) = refs
    x = x_ref[...]
    d = x.shape[-1]
    mod = mod_ref[0]
    sh1, sc1 = mod[:, 0:d], mod[:, d:2 * d]
    y = x * lax.rsqrt(jnp.mean(x * x, axis=-1, keepdims=True) + EPS)
    h = (y * g1_ref[...]) * (1.0 + sc1) + sh1
    hb = h.astype(BF16)

    def head_norm(a, g):
        ss = _dot((a * a).astype(BF16), bd_ref[...])
        return a * lax.rsqrt(ss * (1.0 / ATT_HEAD_DIM) + EPS) * g

    att = _dot(hb, watt_ref[...])
    w = ATT_WIDTH
    aq_ref[...] = (head_norm(att[:, 0:w], qg_ref[...]) * ATT_HEAD_DIM ** -0.5).astype(aq_ref.dtype)
    ak_ref[...] = head_norm(att[:, w:2 * w], kg_ref[...]).astype(ak_ref.dtype)
    av_ref[...] = att[:, 2 * w:3 * w].astype(av_ref.dtype)

    w = ML_WIDTH
    mq = _dot(hb, wml_ref[:, 0:w])
    mk = _dot(hb, wml_ref[:, w:2 * w]) * ML_HEAD_DIM ** -0.5
    if latent:
        cos, sin = cos_ref[...], sin_ref[...]
        for hh in range(N_ML_HEADS):
            sl = slice(hh * ML_HEAD_DIM, (hh + 1) * ML_HEAD_DIM)
            mq_ref[:, sl] = _rope(mq[:, sl], cos, sin).astype(BF16)
            mk_ref[:, sl] = _rope(mk[:, sl], cos, sin).astype(BF16)
    else:
        mq_ref[...] = mq.astype(BF16)
        mk_ref[...] = mk.astype(BF16)
    mv_ref[...] = _dot(hb, wml_ref[:, 2 * w:3 * w]).astype(BF16)
    mo_ref[...] = _dot(hb, wml_ref[:, 3 * w:4 * w])
    gt_ref[...] = _dot(hb, wg_ref[...]) + bg_ref[...]


def _in_projection(x, mod, mod_row, seq_len, wts, rope, *, latent, kv_dtype):
    n_tok, d = x.shape
    tm = _token_tile(n_tok, seq_len)
    tiles_per_seq = max(seq_len // tm, 1)

    def tok(width):
        return pl.BlockSpec((tm, width), lambda i: (i, 0))

    in_specs = [tok(d),
                pl.BlockSpec((1, 1, mod.shape[-1]), lambda i: (mod_row(i, tm), 0, 0)),
                _const_spec((1, d)),
                _const_spec(wts["w_att"].shape), _const_spec(wts["w_ml"].shape),
                _const_spec(wts["w_g"].shape), _const_spec((1, 2 * LANES)),
                _const_spec((1, ATT_WIDTH)), _const_spec((1, ATT_WIDTH)),
                _const_spec((ATT_WIDTH, ATT_WIDTH))]
    args = [x, mod, wts["norm1_g"], wts["w_att"], wts["w_ml"], wts["w_g"], wts["b_g"],
            wts["q_g"], wts["k_g"], wts["blockdiag"]]
    if latent:
        in_specs += [pl.BlockSpec((tm, LANES), lambda i: (i % tiles_per_seq, 0))] * 2
        args += [rope[0], rope[1]]
    out_shape = [jax.ShapeDtypeStruct((n_tok, ATT_WIDTH), BF16),
                 jax.ShapeDtypeStruct((n_tok, ATT_WIDTH), kv_dtype),
                 jax.ShapeDtypeStruct((n_tok, ATT_WIDTH), kv_dtype),
                 jax.ShapeDtypeStruct((n_tok, ML_WIDTH), BF16),
                 jax.ShapeDtypeStruct((n_tok, ML_WIDTH), BF16),
                 jax.ShapeDtypeStruct((n_tok, ML_WIDTH), BF16),
                 jax.ShapeDtypeStruct((n_tok, ML_WIDTH), F32),
                 jax.ShapeDtypeStruct((n_tok, 2 * LANES), F32)]
    out_specs = [tok(ATT_WIDTH)] * 3 + [tok(ML_WIDTH)] * 4 + [tok(2 * LANES)]
    return pl.pallas_call(
        functools.partial(_inproj_kernel, latent=latent),
        grid=(n_tok // tm,),
        in_specs=in_specs,
        out_specs=out_specs,
        out_shape=out_shape,
        compiler_params=_params(1),
        name="in_proj_latent" if latent else "in_proj_ctx",
    )(*args)


def _ctx_attn_kernel(q_ref, k_ref, v_ref, o_ref):
    for hh in range(N_ATT_HEADS):
        sl = slice(hh * ATT_HEAD_DIM, (hh + 1) * ATT_HEAD_DIM)
        q = q_ref[:, sl]
        k = k_ref[:, sl].astype(BF16)
        v = v_ref[:, sl].astype(BF16)
        s = _dot_nt(q, k)
        p = jnp.exp(s - jnp.max(s, axis=-1, keepdims=True))
        l = jnp.sum(p, axis=-1, keepdims=True)
        o_ref[:, sl] = (_dot(p.astype(BF16), v) / l).astype(o_ref.dtype)


def _context_attention(q, k, v, seq_len):
    n_tok = q.shape[0]
    spec = pl.BlockSpec((seq_len, ATT_WIDTH), lambda b: (b, 0))
    return pl.pallas_call(
        _ctx_attn_kernel,
        grid=(n_tok // seq_len,),
        in_specs=[spec, spec, spec],
        out_specs=spec,
        out_shape=jax.ShapeDtypeStruct((n_tok, ATT_WIDTH), BF16),
        compiler_params=_params(1),
        name="ctx_attn",
    )(q, k, v)


def _nbr_attn_kernel(q_ref, k_ref, v_ref, kc_ref, vc_ref, bias_ref, o_ref, *, rows, kr):
    r = pl.program_id(1)
    rs = jnp.clip(r - kr // 2, 0, rows - kr)
    start = pl.multiple_of(rs * GRID_W, GRID_W)
    n_loc = kr * GRID_W
    for hh in range(N_ATT_HEADS):
        sl = slice(hh * ATT_HEAD_DIM, (hh + 1) * ATT_HEAD_DIM)
        q = q_ref[:, sl]
        kw = k_ref[pl.ds(start, n_loc), sl]
        vw = v_ref[pl.ds(start, n_loc), sl]
        kc = kc_ref[0, :, sl].astype(BF16)
        vc = vc_ref[0, :, sl].astype(BF16)
        s_loc = _dot_nt(q, kw) + bias_ref[0, hh]
        s_ctx = _dot_nt(q, kc)
        m = jnp.maximum(jnp.max(s_loc, axis=-1, keepdims=True), jnp.max(s_ctx, axis=-1, keepdims=True))
        p_loc = jnp.exp(s_loc - m)
        p_ctx = jnp.exp(s_ctx - m)
        l = jnp.sum(p_loc, axis=-1, keepdims=True) + jnp.sum(p_ctx, axis=-1, keepdims=True)
        o = _dot(p_loc.astype(BF16), vw) + _dot(p_ctx.astype(BF16), vc)
        o_ref[:, sl] = (o / l).astype(o_ref.dtype)


def _nbr_bias_table(rpb, rows):
    kr = min(WIN_ROWS, rows)
    col = jnp.arange(GRID_W)
    cs = jnp.clip(col - WIN_COLS // 2, 0, GRID_W - WIN_COLS)
    in_win = (col[None, :] >= cs[:, None]) & (col[None, :] < cs[:, None] + WIN_COLS)
    dc_idx = jnp.clip(col[None, :] - col[:, None] + (WIN_COLS - 1), 0, 2 * WIN_COLS - 2)
    n_dr, n_dc = rpb.shape[1], rpb.shape[2]
    onehot = (dc_idx[None] == jnp.arange(n_dc)[:, None, None]).astype(F32)
    t = jnp.einsum("hrc,cqk->hrqk", rpb.astype(F32), onehot, precision=lax.Precision.HIGHEST)
    t = jnp.where(in_win[None, None], t, -jnp.inf)
    t = t.transpose(0, 2, 1, 3).reshape(N_ATT_HEADS, GRID_W, n_dr * GRID_W)
    lo = [(WIN_ROWS - 1 - e) * GRID_W for e in range(kr)]
    return jnp.stack([t[:, :, s:s + kr * GRID_W] for s in lo], axis=0)


def _neighborhood_attention(q, k, v, k_ctx, v_ctx, rpb, n_batch, seq_len):
    rows = seq_len // GRID_W
    kr = min(WIN_ROWS, rows)
    bias = _nbr_bias_table(rpb, rows)
    past = k_ctx.shape[1]

    def bias_idx(b, r):
        return (r - jnp.clip(r - kr // 2, 0, rows - kr), 0, 0, 0)

    seq_spec = pl.BlockSpec((seq_len, ATT_WIDTH), lambda b, r: (b, 0))
    ctx_spec = pl.BlockSpec((1, past, ATT_WIDTH), lambda b, r: (b, 0, 0))
    row_spec = pl.BlockSpec((GRID_W, ATT_WIDTH), lambda b, r: (b * rows + r, 0))
    return pl.pallas_call(
        functools.partial(_nbr_attn_kernel, rows=rows, kr=kr),
        grid=(n_batch, rows),
        in_specs=[row_spec, seq_spec, seq_spec, ctx_spec, ctx_spec,
                  pl.BlockSpec((1, N_ATT_HEADS, GRID_W, kr * GRID_W), bias_idx)],
        out_specs=row_spec,
        out_shape=jax.ShapeDtypeStruct((n_batch * seq_len, ATT_WIDTH), BF16),
        compiler_params=_params(2),
        name="nbr_attn",
    )(q, k, v, k_ctx, v_ctx, bias)


def _mlstm_kernel(*refs, chunk, n_chunks, has_state):
    dh = ML_HEAD_DIM
    if has_state:
        (q_ref, k_ref, v_ref, mo_ref, gate_ref, g_ref, c0_ref, n0_ref, m0_ref,
         o_ref, c_ref, n_ref, m_ref, hf_ref, hb_ref, cn_ref, mrun_ref) = refs
        for d in range(2):
            for hh in range(N_ML_HEADS):
                cn_ref[d, hh, :, 0:dh] = c0_ref[0, d, hh]
                cn_ref[d, hh, :, dh:2 * dh] = jnp.broadcast_to(n0_ref[0, d, hh], (dh, dh)).T
        mrun_ref[...] = m0_ref[0]
    else:
        (q_ref, k_ref, v_ref, mo_ref, gate_ref, g_ref,
         o_ref, c_ref, n_ref, m_ref, hf_ref, hb_ref, cn_ref, mrun_ref) = refs
        cn_ref[...] = jnp.zeros_like(cn_ref)
        mrun_ref[...] = jnp.zeros_like(mrun_ref)
    use_state = has_state or n_chunks > 1

    t_idx = lax.broadcasted_iota(jnp.int32, (chunk, chunk), 0)
    s_idx = lax.broadcasted_iota(jnp.int32, (chunk, chunk), 1)
    lane = lax.broadcasted_iota(jnp.int32, (1, LANES), 1)
    instances = [(hh, d) for d in range(2) for hh in range(N_ML_HEADS)]

    def body(i, _):
        rows = [pl.ds(pl.multiple_of(c * chunk, chunk), chunk) for c in (i, n_chunks - 1 - i)]
        valid = [s_idx <= t_idx, s_idx >= t_idx]
        m_prev = mrun_ref[...]

        def qkv(hh, d):
            sl = slice(hh * dh, (hh + 1) * dh)
            return q_ref[rows[d], sl], k_ref[rows[d], sl], v_ref[rows[d], sl]

        qk = [_dot_nt(*qkv(hh, d)[0:2]) for hh, d in instances]
        if use_state:
            cn = [cn_ref[d, hh] for hh, d in instances]
            qc = [_dot(qkv(hh, d)[0], cn[n].astype(BF16)) for n, (hh, d) in enumerate(instances)]

        cs, a_rows, w, decay, m_new = [], [], [], [], []
        for d in range(2):
            tri = jnp.where(valid[d], 1.0, 0.0).astype(BF16)
            gi = gate_ref[rows[d], 0:LANES]
            lf = _log_sigmoid(gate_ref[rows[d], LANES:2 * LANES])
            lf_hi = lf.astype(BF16)
            r1 = lf - lf_hi.astype(F32)
            lf_mid = r1.astype(BF16)
            lf_lo = (r1 - lf_mid.astype(F32)).astype(BF16)
            cs_d = _dot(tri, lf_hi) + _dot(tri, lf_mid) + _dot(tri, lf_lo)
            b_last = jnp.sum(lf, axis=0, keepdims=True)
            w_end = b_last + gi - cs_d
            m_d = jnp.maximum(b_last + m_prev, jnp.max(w_end, axis=0, keepdims=True))
            cs.append(cs_d)
            a_rows.append((gi - cs_d).T[0:8])
            decay.append(jnp.exp(b_last + m_prev - m_d))
            w.append(jnp.exp(w_end - m_d))
            m_new.append(m_d)
        mrun_ref[...] = jnp.where((lane & 1) == 0, m_new[0], m_new[1])

        s_all, g_all = [], []
        for n, (hh, d) in enumerate(instances):
            j = 2 * hh + d
            a = jnp.where(valid[d], a_rows[d][j:j + 1, :], -jnp.inf)
            g = jnp.maximum(jnp.max(a, axis=1, keepdims=True), m_prev[:, j:j + 1])
            s_all.append(qk[n] * jnp.exp(a - g))
            g_all.append(g)

        for n, (hh, d) in enumerate(instances):
            j = 2 * hh + d
            sl = slice(hh * dh, (hh + 1) * dh)
            _, k, v = qkv(hh, d)
            s, g = s_all[n], g_all[n]
            num = _dot(s.astype(BF16), v)
            den = jnp.sum(s, axis=1, keepdims=True)
            if use_state:
                w_inter = jnp.exp(m_prev[:, j:j + 1] - g)
                num = num + w_inter * qc[n][:, 0:dh]
                den = den + w_inter * qc[n][:, dh:2 * dh]
            h = num / jnp.maximum(jnp.abs(den), jnp.exp(-(cs[d][:, j:j + 1] + g)))
            (hb_ref if d else hf_ref)[rows[d], sl] = h
            kw = (k.astype(F32) * w[d][:, j:j + 1]).astype(BF16)
            v1 = jnp.concatenate([v, jnp.ones_like(v)], axis=1)
            upd = _dot_tn(kw, v1)
            if use_state:
                upd = decay[d][:, j:j + 1] * cn[n] + upd
            cn_ref[d, hh] = upd
        return 0

    def finish(c, _):
        rows = pl.ds(pl.multiple_of(c * chunk, chunk), chunk)
        for hh in range(N_ML_HEADS):
            sl = slice(hh * dh, (hh + 1) * dh)
            hs = hf_ref[rows, sl] + hb_ref[rows, sl]
            y = hs * lax.rsqrt(jnp.mean(hs * hs, axis=-1, keepdims=True) + EPS) * g_ref[:, sl]
            o_ref[rows, sl] = (y * jax.nn.sigmoid(mo_ref[rows, sl])).astype(o_ref.dtype)
        return 0

    if n_chunks == 1:
        body(0, 0)
        finish(0, 0)
    else:
        lax.fori_loop(0, n_chunks, body, 0)
        lax.fori_loop(0, n_chunks, finish, 0)
    for d in range(2):
        for hh in range(N_ML_HEADS):
            cn = cn_ref[d, hh]
            c_ref[0, d, hh] = cn[:, 0:dh]
            n_ref[0, d, hh] = cn[:, dh:2 * dh].T[0:1, :]
    m_ref[0] = mrun_ref[...]


def _mlstm(mq, mk, mv, mo, gates, ml_g, state, n_batch, seq_len, chunk):
    dh = ML_HEAD_DIM
    nh = N_ML_HEADS
    has_state = state is not None
    seq = pl.BlockSpec((seq_len, nh * dh), lambda b: (b, 0))
    st_c = pl.BlockSpec((1, 2, nh, dh, dh), lambda b: (b, 0, 0, 0, 0))
    st_n = pl.BlockSpec((1, 2, nh, 1, dh), lambda b: (b, 0, 0, 0, 0))
    st_m = pl.BlockSpec((1, 1, LANES), lambda b: (b, 0, 0))
    in_specs = [seq, seq, seq, seq, pl.BlockSpec((seq_len, 2 * LANES), lambda b: (b, 0)), _const_spec((1, nh * dh))]
    args = [mq, mk, mv, mo, gates, ml_g]
    if has_state:
        in_specs += [st_c, st_n, st_m]
        args += list(state)
    out_shape = [jax.ShapeDtypeStruct((n_batch * seq_len, nh * dh), BF16),
                 jax.ShapeDtypeStruct((n_batch, 2, nh, dh, dh), F32),
                 jax.ShapeDtypeStruct((n_batch, 2, nh, 1, dh), F32),
                 jax.ShapeDtypeStruct((n_batch, 1, LANES), F32)]
    return pl.pallas_call(
        functools.partial(_mlstm_kernel, chunk=chunk, n_chunks=seq_len // chunk, has_state=has_state),
        grid=(n_batch,),
        in_specs=in_specs,
        out_specs=[seq, st_c, st_n, st_m],
        out_shape=out_shape,
        scratch_shapes=[pltpu.VMEM((seq_len, nh * dh), F32), pltpu.VMEM((seq_len, nh * dh), F32),
                        pltpu.VMEM((2, nh, dh, 2 * dh), F32), pltpu.VMEM((1, LANES), F32)],
        compiler_params=_params(1),
        name="mlstm_latent" if has_state else "mlstm_ctx",
    )(*args)


def _pack_gate_lanes(m):
    b = m.shape[0]
    packed = m.transpose(0, 2, 1).reshape(b, 2 * N_ML_HEADS)
    return jnp.pad(packed, ((0, 0), (0, LANES - 2 * N_ML_HEADS))).reshape(b, 1, LANES)


def _unpack_gate_lanes(m):
    b = m.shape[0]
    return m[:, 0, :2 * N_ML_HEADS].reshape(b, N_ML_HEADS, 2).transpose(0, 2, 1)


def _outproj_kernel(oa_ref, om_ref, x_ref, mod_ref, g2_ref, wa_ref, wm_ref, x1_ref, h2_ref):
    d = x_ref.shape[-1]
    mod = mod_ref[0]
    g1 = mod[:, 2 * d:3 * d]
    sh2, sc2 = mod[:, 3 * d:4 * d], mod[:, 4 * d:5 * d]
    out = _dot(oa_ref[...], wa_ref[...]) + _dot(om_ref[...], wm_ref[...])
    x1 = x_ref[...] + g1 * out
    x1_ref[...] = x1
    y = x1 * lax.rsqrt(jnp.mean(x1 * x1, axis=-1, keepdims=True) + EPS)
    h2_ref[...] = ((y * g2_ref[...]) * (1.0 + sc2) + sh2).astype(h2_ref.dtype)


def _out_projection(o_att, o_ml, x, mod, mod_row, seq_len, wts):
    n_tok, d = x.shape
    tm = _token_tile(n_tok, seq_len)

    def tok(width):
        return pl.BlockSpec((tm, width), lambda i: (i, 0))

    return pl.pallas_call(
        _outproj_kernel,
        grid=(n_tok // tm,),
        in_specs=[tok(ATT_WIDTH), tok(ML_WIDTH), tok(d),
                  pl.BlockSpec((1, 1, mod.shape[-1]), lambda i: (mod_row(i, tm), 0, 0)),
                  _const_spec((1, d)), _const_spec((ATT_WIDTH, d)), _const_spec((ML_WIDTH, d))],
        out_specs=[tok(d), tok(d)],
        out_shape=[jax.ShapeDtypeStruct((n_tok, d), F32), jax.ShapeDtypeStruct((n_tok, d), BF16)],
        compiler_params=_params(1),
        name="out_proj",
    )(o_att, o_ml, x, mod, wts["norm2_g"], wts["w_out_att"], wts["w_out_ml"])


def _ffn_kernel(h_ref, hprev_ref, hnext_ref, x1_ref, mod_ref, wup_ref, cw_ref, cb_ref, wd_ref,
                y_ref, lhs_ref, ug_ref, uv_ref, acc_ref, *, seq_len, n_col_tiles):
    tm, d = x1_ref.shape
    lhs_ref[0:tm, :] = h_ref[...]
    halo_row = lax.broadcasted_iota(jnp.int32, (HALO, d), 0)
    lhs_ref[tm:tm + HALO, :] = jnp.where(halo_row < HALO // 2, hnext_ref[...], hprev_ref[...])
    acc_ref[...] = jnp.zeros_like(acc_ref)
    pos = (pl.program_id(0) * tm + lax.broadcasted_iota(jnp.int32, (tm, 1), 0)) % seq_len
    first = pos == 0
    last = pos == seq_len - 1

    def conv(u, cw, cb):
        prev = jnp.where(first, 0.0, pltpu.roll(u, 1, 0)[0:tm])
        nxt = jnp.where(last, 0.0, pltpu.roll(u, tm + HALO - 1, 0)[0:tm])
        return prev * cw[0:1] + u[0:tm] * cw[1:2] + nxt * cw[2:3] + cb

    def up(j, slot):
        lhs = lhs_ref[...]
        ug_ref[slot] = _dot(lhs, wup_ref[j])
        uv_ref[slot] = _dot(lhs, wup_ref[n_col_tiles + j])

    def down(j, slot):
        gate = conv(ug_ref[slot], cw_ref[j], cb_ref[j])
        val = conv(uv_ref[slot], cw_ref[n_col_tiles + j], cb_ref[n_col_tiles + j])
        act = (gate * jax.nn.sigmoid(gate)) * val
        return _dot(act.astype(BF16), wd_ref[j])

    def pair(i, _):
        up(2 * i + 1, 1)
        acc_ref[...] += down(2 * i, 0)
        up(2 * i + 2, 0)
        acc_ref[...] += down(2 * i + 1, 1)
        return 0

    up(0, 0)
    n_pairs = (n_col_tiles - 1) // 2
    if n_pairs:
        lax.fori_loop(0, n_pairs, pair, 0)
    j0 = 2 * n_pairs
    if n_col_tiles - j0 == 2:
        up(j0 + 1, 1)
        acc_ref[...] += down(j0, 0)
        tail = down(j0 + 1, 1)
    else:
        tail = down(j0, 0)
    g2 = mod_ref[0][:, 5 * d:6 * d]
    y_ref[...] = x1_ref[...] + g2 * (acc_ref[...] + tail)


def _conv_ffn(h2, x1, mod, mod_row, seq_len, wts):
    n_tok, d = x1.shape
    tm = _token_tile(n_tok, seq_len)
    hpt = tm // HALO
    n_halo = n_tok // HALO
    nct, tn, _ = wts["w_down"].shape

    def tok(width):
        return pl.BlockSpec((tm, width), lambda i: (i, 0))

    in_specs = [tok(d),
                pl.BlockSpec((HALO, d), lambda i: (jnp.maximum(i * hpt - 1, 0), 0)),
                pl.BlockSpec((HALO, d), lambda i: (jnp.minimum((i + 1) * hpt, n_halo - 1), 0)),
                tok(d),
                pl.BlockSpec((1, 1, mod.shape[-1]), lambda i: (mod_row(i, tm), 0, 0)),
                _const_spec((2 * nct, d, tn)), _const_spec((2 * nct, 3, tn)), _const_spec((2 * nct, 1, tn)),
                _const_spec((nct, tn, d))]
    return pl.pallas_call(
        functools.partial(_ffn_kernel, seq_len=seq_len, n_col_tiles=nct),
        grid=(n_tok // tm,),
        in_specs=in_specs,
        out_specs=tok(d),
        out_shape=jax.ShapeDtypeStruct((n_tok, d), F32),
        scratch_shapes=[pltpu.VMEM((tm + HALO, d), BF16),
                        pltpu.VMEM((2, tm + HALO, tn), F32), pltpu.VMEM((2, tm + HALO, tn), F32),
                        pltpu.VMEM((tm, d), F32)],
        compiler_params=_params(1),
        name="conv_ffn",
    )(h2, h2, h2, x1, mod, wts["w_up"], wts["conv_w"], wts["conv_b"], wts["w_down"])


def _retile_cast_kernel(w_ref, o_ref):
    o_ref[0] = w_ref[...].astype(o_ref.dtype)


def _retile_cast(w, tn):
    r, c = w.shape
    return pl.pallas_call(
        _retile_cast_kernel,
        grid=(c // tn,),
        in_specs=[pl.BlockSpec((r, tn), lambda j: (0, j))],
        out_specs=pl.BlockSpec((1, r, tn), lambda j: (j, 0, 0)),
        out_shape=jax.ShapeDtypeStruct((c // tn, r, tn), BF16),
        compiler_params=_params(1),
        name="retile_cast",
    )(w)


def _layer_weights(norm1_g, norm2_g, w_in, b_gate, q_g, k_g, ml_g, w_out, w_up, conv_w, conv_b, w_down):
    d = w_in.shape[0]
    a, m, nh = ATT_WIDTH, ML_WIDTH, N_ML_HEADS
    def gate_lanes(g):
        g = g.reshape(g.shape[0], 2, 2, nh).transpose(0, 2, 3, 1).reshape(g.shape[0], 2, 2 * nh)
        return jnp.pad(g, ((0, 0), (0, 0), (0, LANES - 2 * nh))).reshape(g.shape[0], 2 * LANES)

    w_g = gate_lanes(w_in[:, 3 * a + 4 * m:])
    b_g = gate_lanes(b_gate.astype(F32).reshape(1, 4 * nh))
    head_id = jnp.arange(a) // ATT_HEAD_DIM
    d_ff = w_down.shape[0]
    tn = MXU_WIDTH
    nct = d_ff // tn

    def col_tiles(w):
        return w.reshape(w.shape[0], 2 * nct, tn).transpose(1, 0, 2)

    return {
        "norm1_g": norm1_g.reshape(1, d), "norm2_g": norm2_g.reshape(1, d),
        "w_att": w_in[:, :3 * a].astype(BF16), "w_ml": w_in[:, 3 * a:3 * a + 4 * m].astype(BF16),
        "w_g": w_g.astype(BF16), "b_g": b_g,
        "q_g": jnp.tile(q_g, N_ATT_HEADS).reshape(1, a), "k_g": jnp.tile(k_g, N_ATT_HEADS).reshape(1, a),
        "blockdiag": (head_id[:, None] == head_id[None, :]).astype(BF16),
        "ml_g": ml_g.reshape(1, m),
        "w_out_att": w_out[:a].astype(BF16), "w_out_ml": w_out[a:].astype(BF16),
        "w_up": _retile_cast(w_up, tn),
        "conv_w": col_tiles(conv_w.astype(F32)), "conv_b": col_tiles(conv_b.astype(F32).reshape(1, -1)),
        "w_down": w_down.astype(BF16).reshape(nct, tn, w_down.shape[1]),
    }


def _rope_tables(seq_len):
    quarter = ML_HEAD_DIM // 4
    pos = jnp.arange(seq_len)
    inv_freq = ROPE_THETA ** (-jnp.arange(quarter, dtype=F32) / quarter)
    ang_r = (pos // GRID_W).astype(F32)[:, None] * inv_freq[None, :]
    ang_c = (pos % GRID_W).astype(F32)[:, None] * inv_freq[None, :]
    cos = jnp.concatenate([jnp.cos(ang_r)] * 2 + [jnp.cos(ang_c)] * 2, axis=-1)
    sin = jnp.concatenate([-jnp.sin(ang_r), jnp.sin(ang_r), -jnp.sin(ang_c), jnp.sin(ang_c)], axis=-1)
    return cos, sin


def _layer(x, mod, mod_row, n_batch, seq_len, wts, *, latent, ctx_kv=None, rpb=None, state=None, rope=None):
    chunk = min(ML_CHUNK, seq_len)
    kv_dtype = BF16 if latent else F32
    aq, ak, av, mq, mk, mv, mo, gt = _in_projection(
        x, mod, mod_row, seq_len, wts, rope, latent=latent, kv_dtype=kv_dtype)
    if latent:
        o_att = _neighborhood_attention(aq, ak, av, ctx_kv[0], ctx_kv[1], rpb, n_batch, seq_len)
    else:
        o_att = _context_attention(aq, ak, av, seq_len)
    o_ml, c_f, n_f, m_f = _mlstm(mq, mk, mv, mo, gt, wts["ml_g"], state, n_batch, seq_len, chunk)
    x1, h2 = _out_projection(o_att, o_ml, x, mod, mod_row, seq_len, wts)
    y = _conv_ffn(h2, x1, mod, mod_row, seq_len, wts)
    return y, ak, av, (c_f, n_f[:, :, :, 0, :], _unpack_gate_lanes(m_f))


def kernel(x_prompt, x_sample, cache_k, cache_v, state_C, state_n, state_m, c, c_ctx, w_mod, b_mod, norm1_g,
           norm2_g, w_in, b_gate, q_norm_g, k_norm_g, rpb, ml_norm_g, w_out, w_up, conv_w, conv_b, w_down):
    batch, seq, d = x_prompt.shape
    dec_batch, dec_seq, _ = x_sample.shape
    depth = w_mod.shape[0]
    past = cache_k.shape[2]
    cvecs = jnp.concatenate([c_ctx[None, :], c], axis=0).astype(F32)
    rope = _rope_tables(dec_seq)

    def ctx_row(i, tm):
        return 0 * i

    def lat_row(i, tm):
        return 1 + (i * tm) // dec_seq

    xp = x_prompt.reshape(batch * seq, d)
    xs = x_sample.reshape(dec_batch * dec_seq, d)
    ks, vs, cs, ns, ms = [], [], [], [], []
    for l in range(depth):
        wts = _layer_weights(norm1_g[l], norm2_g[l], w_in[l], b_gate[l], q_norm_g[l], k_norm_g[l],
                             ml_norm_g[l], w_out[l], w_up[l], conv_w[l], conv_b[l], w_down[l])
        mod = _modulation(cvecs, w_mod[l], b_mod[l])

        xp, ak, av, st = _layer(xp, mod, ctx_row, batch, seq, wts, latent=False)
        ks.append(ak.reshape(batch, seq, N_ATT_HEADS, ATT_HEAD_DIM))
        vs.append(av.reshape(batch, seq, N_ATT_HEADS, ATT_HEAD_DIM))
        cs.append(st[0])
        ns.append(st[1])
        ms.append(st[2])

        state = (state_C[:, l].astype(F32),
                 state_n[:, l].astype(F32)[:, :, :, None, :],
                 _pack_gate_lanes(state_m[:, l].astype(F32)))
        ctx_kv = (cache_k[:, l].reshape(dec_batch, past, ATT_WIDTH), cache_v[:, l].reshape(dec_batch, past, ATT_WIDTH))
        xs, _, _, _ = _layer(xs, mod, lat_row, dec_batch, dec_seq, wts, latent=True, ctx_kv=ctx_kv,
                             rpb=rpb[l], state=state, rope=rope)
    return (xp.reshape(batch, seq, d), xs.reshape(dec_batch, dec_seq, d),
            jnp.stack(ks, axis=1), jnp.stack(vs, axis=1),
            jnp.stack(cs, axis=1), jnp.stack(ns, axis=1), jnp.stack(ms, axis=1))
```

```python
import functools

import jax
import jax.numpy as jnp
from jax import lax
from jax.experimental import pallas as pl
from jax.experimental.pallas import tpu as pltpu

F32 = jnp.float32
BF16 = jnp.bfloat16

GRID_W = 64
N_ATT_HEADS = 8
ATT_HEAD_DIM = 64
ATT_WIDTH = N_ATT_HEADS * ATT_HEAD_DIM
WIN_ROWS = 8
WIN_COLS = 16
HEAD_GROUP = 4
N_ML_HEADS = 4
ML_HEAD_DIM = 128
ML_WIDTH = N_ML_HEADS * ML_HEAD_DIM
ROPE_THETA = 10000.0
EPS = 1e-6

LANES = 128
MXU_WIDTH = 256
ML_CHUNK = 256
HALO = 16
TOKEN_TILE = 512
VMEM_LIMIT = 56 * 1024 * 1024


def _dot(a, b):
    return jnp.dot(a, b, preferred_element_type=F32)


def _dot_nt(a, b):
    return lax.dot_general(a, b, (((1,), (1,)), ((), ())), preferred_element_type=F32)


def _dot_tn(a, b):
    return lax.dot_general(a, b, (((0,), (0,)), ((), ())), preferred_element_type=F32)


def _const_spec(shape):
    nd = len(shape)
    return pl.BlockSpec(shape, lambda *_: (0,) * nd, pipeline_mode=pl.Buffered(1))


def _params(n_axes):
    return pltpu.CompilerParams(dimension_semantics=("arbitrary",) * n_axes,
                                vmem_limit_bytes=VMEM_LIMIT)


def _token_tile(n_tok, seq_len):
    tm = min(TOKEN_TILE, n_tok)
    while n_tok % tm or (seq_len % tm and tm % seq_len):
        tm //= 2
    return tm


def _log_sigmoid(x):
    return jnp.minimum(x, 0.0) - jnp.log1p(jnp.exp(-jnp.abs(x)))


def _mod_kernel(c_ref, w_ref, b_ref, o_ref):
    c = c_ref[...]
    s = c * jax.nn.sigmoid(c)
    o_ref[...] = _dot(s.astype(BF16), w_ref[...].astype(BF16)) + b_ref[...]


def _modulation(cvecs, w_mod, b_mod):
    r, d = cvecs.shape
    n = w_mod.shape[1]
    tn = d
    out = pl.pallas_call(
        _mod_kernel,
        grid=(n // tn,),
        in_specs=[pl.BlockSpec((r, d), lambda j: (0, 0)),
                  pl.BlockSpec((d, tn), lambda j: (0, j)),
                  pl.BlockSpec((1, tn), lambda j: (0, j))],
        out_specs=pl.BlockSpec((r, tn), lambda j: (0, j)),
        out_shape=jax.ShapeDtypeStruct((r, n), F32),
        compiler_params=_params(1),
        name="adaln_mod",
    )(cvecs, w_mod, b_mod.reshape(1, n))
    return out.reshape(r, 1, n)


def _rope(x, cos, sin_signed):
    lane = lax.broadcasted_iota(jnp.int32, x.shape, 1)
    partner = jnp.where((lane & 32) == 0, pltpu.roll(x, LANES - 32, 1), pltpu.roll(x, 32, 1))
    return x * cos + partner * sin_signed


def _inproj_kernel(*refs, latent):
    if latent:
        (x_ref, mod_ref, g1_ref, watt_ref, wml_ref, wg_ref, bg_ref, qg_ref, kg_ref, bd_ref, cos_ref, sin_ref,
         aq_ref, ak_ref, av_ref, mq_ref, mk_ref, mv_ref, mo_ref, gt_ref) = refs
    else:
        (x_ref, mod_ref, g1_ref, watt_ref, wml_ref, wg_ref, bg_ref, qg_ref, kg_ref, bd_ref,
         aq_ref, ak_ref, av_ref, mq_ref, mk_ref, mv_ref, mo_ref, gt_ref) = refs
    x = x_ref[...]
    d = x.shape[-1]
    mod = mod_ref[0]
    sh1, sc1 = mod[:, 0:d], mod[:, d:2 * d]
    y = x * lax.rsqrt(jnp.mean(x * x, axis=-1, keepdims=True) + EPS)
    h = (y * g1_ref[...]) * (1.0 + sc1) + sh1
    hb = h.astype(BF16)

    def head_norm(a, g):
        ss = _dot((a * a).astype(BF16), bd_ref[...])
        return a * lax.rsqrt(ss * (1.0 / ATT_HEAD_DIM) + EPS) * g

    att = _dot(hb, watt_ref[...])
    w = ATT_WIDTH
    aq_ref[...] = (head_norm(att[:, 0:w], qg_ref[...]) * ATT_HEAD_DIM ** -0.5).astype(aq_ref.dtype)
    ak_ref[...] = head_norm(att[:, w:2 * w], kg_ref[...]).astype(ak_ref.dtype)
    if latent:
        av_ref[...] = att[:, 2 * w:3 * w].T.astype(av_ref.dtype)
    else:
        av_ref[...] = att[:, 2 * w:3 * w].astype(av_ref.dtype)

    w = ML_WIDTH
    mq = _dot(hb, wml_ref[:, 0:w])
    mk = _dot(hb, wml_ref[:, w:2 * w]) * ML_HEAD_DIM ** -0.5
    if latent:
        cos, sin = cos_ref[...], sin_ref[...]
        for hh in range(N_ML_HEADS):
            sl = slice(hh * ML_HEAD_DIM, (hh + 1) * ML_HEAD_DIM)
            mq_ref[:, sl] = _rope(mq[:, sl], cos, sin).astype(BF16)
            mk_ref[:, sl] = _rope(mk[:, sl], cos, sin).astype(BF16)
    else:
        mq_ref[...] = mq.astype(BF16)
        mk_ref[...] = mk.astype(BF16)
    mv_ref[...] = _dot(hb, wml_ref[:, 2 * w:3 * w]).astype(BF16)
    mo_ref[...] = _dot(hb, wml_ref[:, 3 * w:4 * w])
    gt_ref[...] = _dot(hb, wg_ref[...]) + bg_ref[...]


def _in_projection(x, mod, mod_row, seq_len, wts, rope, *, latent, kv_dtype):
    n_tok, d = x.shape
    tm = _token_tile(n_tok, seq_len)
    tiles_per_seq = max(seq_len // tm, 1)

    def tok(width):
        return pl.BlockSpec((tm, width), lambda i: (i, 0))

    in_specs = [tok(d),
                pl.BlockSpec((1, 1, mod.shape[-1]), lambda i: (mod_row(i, tm), 0, 0)),
                _const_spec((1, d)),
                _const_spec(wts["w_att"].shape), _const_spec(wts["w_ml"].shape),
                _const_spec(wts["w_g"].shape), _const_spec((1, 2 * LANES)),
                _const_spec((1, ATT_WIDTH)), _const_spec((1, ATT_WIDTH)),
                _const_spec((ATT_WIDTH, ATT_WIDTH))]
    args = [x, mod, wts["norm1_g"], wts["w_att"], wts["w_ml"], wts["w_g"], wts["b_g"],
            wts["q_g"], wts["k_g"], wts["blockdiag"]]
    if latent:
        in_specs += [pl.BlockSpec((tm, LANES), lambda i: (i % tiles_per_seq, 0))] * 2
        args += [rope[0], rope[1]]
    out_shape = [jax.ShapeDtypeStruct((n_tok, ATT_WIDTH), BF16),
                 jax.ShapeDtypeStruct((n_tok, ATT_WIDTH), kv_dtype),
                 jax.ShapeDtypeStruct((ATT_WIDTH, n_tok) if latent else (n_tok, ATT_WIDTH), kv_dtype),
                 jax.ShapeDtypeStruct((n_tok, ML_WIDTH), BF16),
                 jax.ShapeDtypeStruct((n_tok, ML_WIDTH), BF16),
                 jax.ShapeDtypeStruct((n_tok, ML_WIDTH), BF16),
                 jax.ShapeDtypeStruct((n_tok, ML_WIDTH), F32),
                 jax.ShapeDtypeStruct((n_tok, 2 * LANES), F32)]
    av_spec = pl.BlockSpec((ATT_WIDTH, tm), lambda i: (0, i)) if latent else tok(ATT_WIDTH)
    out_specs = [tok(ATT_WIDTH)] * 2 + [av_spec] + [tok(ML_WIDTH)] * 4 + [tok(2 * LANES)]
    return pl.pallas_call(
        functools.partial(_inproj_kernel, latent=latent),
        grid=(n_tok // tm,),
        in_specs=in_specs,
        out_specs=out_specs,
        out_shape=out_shape,
        compiler_params=_params(1),
        name="in_proj_latent" if latent else "in_proj_ctx",
    )(*args)


def _ctx_attn_kernel(q_ref, k_ref, v_ref, o_ref):
    for hh in range(N_ATT_HEADS):
        sl = slice(hh * ATT_HEAD_DIM, (hh + 1) * ATT_HEAD_DIM)
        q = q_ref[:, sl]
        k = k_ref[:, sl].astype(BF16)
        v = v_ref[:, sl].astype(BF16)
        s = _dot_nt(q, k)
        p = jnp.exp(s - jnp.max(s, axis=-1, keepdims=True))
        l = jnp.sum(p, axis=-1, keepdims=True)
        o_ref[:, sl] = (_dot(p.astype(BF16), v) / l).astype(o_ref.dtype)


def _context_attention(q, k, v, seq_len):
    n_tok = q.shape[0]
    spec = pl.BlockSpec((seq_len, ATT_WIDTH), lambda b: (b, 0))
    return pl.pallas_call(
        _ctx_attn_kernel,
        grid=(n_tok // seq_len,),
        in_specs=[spec, spec, spec],
        out_specs=spec,
        out_shape=jax.ShapeDtypeStruct((n_tok, ATT_WIDTH), BF16),
        compiler_params=_params(1),
        name="ctx_attn",
    )(q, k, v)


def _nbr_attn_kernel(q_ref, k_ref, vt_ref, kc_ref, vct_ref, bias_ref, o_ref, pt_ref, *, rows, kr):
    r = pl.program_id(1)
    n_win = kr + 2
    rs = jnp.clip(r - kr // 2, 0, rows - kr)
    start = jnp.minimum(rs - (rs & 1), rows - n_win)
    delta = rs - start
    bias0 = pl.multiple_of((WIN_ROWS - 1 - (r - rs)) * GRID_W, GRID_W)
    n_loc = kr * GRID_W
    gw = HEAD_GROUP * ATT_HEAD_DIM
    lane_head = lax.broadcasted_iota(jnp.int32, (GRID_W, gw), 1) // ATT_HEAD_DIM
    zeros2 = jnp.zeros((2 * GRID_W, gw), BF16)
    for g in range(N_ATT_HEADS // HEAD_GROUP):
        sl = slice(g * gw, (g + 1) * gw)
        q4 = q_ref[:, sl]
        qbd = jnp.concatenate([jnp.where(lane_head == hl, q4, jnp.zeros_like(q4))
                               for hl in range(HEAD_GROUP)], axis=0)
        s_loc = _dot_nt(k_ref[pl.ds(pl.multiple_of(rs * GRID_W, GRID_W), n_loc), sl], qbd)
        s_loc = s_loc + bias_ref[g, pl.ds(bias0, n_loc), :]
        s_ctx = _dot_nt(kc_ref[0, :, sl], qbd)
        m = jnp.maximum(jnp.max(s_loc, axis=0, keepdims=True), jnp.max(s_ctx, axis=0, keepdims=True))
        p_loc = jnp.exp(s_loc - m)
        p_ctx = jnp.exp(s_ctx - m)
        l = jnp.sum(p_loc, axis=0, keepdims=True) + jnp.sum(p_ctx, axis=0, keepdims=True)
        pt_ref[g, 0:2 * GRID_W, :] = zeros2
        pt_ref[g, n_loc:n_loc + 2 * GRID_W, :] = zeros2
        pt_ref[g, pl.ds(pl.multiple_of(delta * GRID_W, GRID_W), n_loc), :] = p_loc.astype(BF16)
        vt_win = vt_ref[sl, pl.ds(pl.multiple_of(start * GRID_W, 2 * GRID_W), n_win * GRID_W)]
        ot = _dot(vt_win, pt_ref[g]) + _dot(vct_ref[0, sl, :], p_ctx.astype(BF16))
        o4 = (ot / l).T
        out = jnp.where(lane_head == 0, o4[0:GRID_W], 0.0)
        for hl in range(1, HEAD_GROUP):
            out = out + jnp.where(lane_head == hl, o4[hl * GRID_W:(hl + 1) * GRID_W], 0.0)
        o_ref[:, sl] = out.astype(o_ref.dtype)


def _nbr_bias_table(rpb, rows):
    col = jnp.arange(GRID_W)
    cs = jnp.clip(col - WIN_COLS // 2, 0, GRID_W - WIN_COLS)
    in_win = (col[None, :] >= cs[:, None]) & (col[None, :] < cs[:, None] + WIN_COLS)
    dc_idx = jnp.clip(col[None, :] - col[:, None] + (WIN_COLS - 1), 0, 2 * WIN_COLS - 2)
    n_dr, n_dc = rpb.shape[1], rpb.shape[2]
    onehot = (dc_idx[None] == jnp.arange(n_dc)[:, None, None]).astype(F32)
    t = jnp.einsum("hrc,cqk->hrqk", rpb.astype(F32), onehot, precision=lax.Precision.HIGHEST)
    t = jnp.where(in_win[None, None], t, -jnp.inf)
    n_groups = N_ATT_HEADS // HEAD_GROUP
    t = t.reshape(n_groups, HEAD_GROUP, n_dr, GRID_W, GRID_W).transpose(0, 2, 4, 1, 3)
    return t.reshape(n_groups, n_dr * GRID_W, HEAD_GROUP * GRID_W)


def _neighborhood_attention(q, k, vt, k_ctx, vt_ctx, rpb, n_batch, seq_len):
    rows = seq_len // GRID_W
    kr = min(WIN_ROWS, rows)
    n_win = kr + 2
    assert rows >= n_win and (rows - n_win) % 2 == 0
    bias = _nbr_bias_table(rpb, rows)
    past = k_ctx.shape[1]
    gw = HEAD_GROUP * ATT_HEAD_DIM
    row_spec = pl.BlockSpec((GRID_W, ATT_WIDTH), lambda b, r: (b * rows + r, 0))
    return pl.pallas_call(
        functools.partial(_nbr_attn_kernel, rows=rows, kr=kr),
        grid=(n_batch, rows),
        in_specs=[row_spec,
                  pl.BlockSpec((seq_len, ATT_WIDTH), lambda b, r: (b, 0)),
                  pl.BlockSpec((ATT_WIDTH, seq_len), lambda b, r: (0, b)),
                  pl.BlockSpec((1, past, ATT_WIDTH), lambda b, r: (b, 0, 0)),
                  pl.BlockSpec((1, ATT_WIDTH, past), lambda b, r: (b, 0, 0)),
                  _const_spec(bias.shape)],
        out_specs=row_spec,
        out_shape=jax.ShapeDtypeStruct((n_batch * seq_len, ATT_WIDTH), BF16),
        scratch_shapes=[pltpu.VMEM((N_ATT_HEADS // HEAD_GROUP, n_win * GRID_W, gw), BF16)],
        compiler_params=_params(2),
        name="nbr_attn",
    )(q, k, vt, k_ctx, vt_ctx, bias)


def _mlstm_kernel(*refs, chunk, n_chunks, has_state):
    dh = ML_HEAD_DIM
    if has_state:
        (q_ref, k_ref, v_ref, mo_ref, gate_ref, g_ref, c0_ref, n0_ref, m0_ref,
         o_ref, c_ref, n_ref, m_ref, hf_ref, hb_ref, cn_ref, mrun_ref) = refs
        for d in range(2):
            for hh in range(N_ML_HEADS):
                cn_ref[d, hh, :, 0:dh] = c0_ref[0, d, hh]
                cn_ref[d, hh, :, dh:2 * dh] = jnp.broadcast_to(n0_ref[0, d, hh], (dh, dh)).T
        mrun_ref[...] = m0_ref[0]
    else:
        (q_ref, k_ref, v_ref, mo_ref, gate_ref, g_ref,
         o_ref, c_ref, n_ref, m_ref, hf_ref, hb_ref, cn_ref, mrun_ref) = refs
        cn_ref[...] = jnp.zeros_like(cn_ref)
        mrun_ref[...] = jnp.zeros_like(mrun_ref)
    use_state = has_state or n_chunks > 1

    t_idx = lax.broadcasted_iota(jnp.int32, (chunk, chunk), 0)
    s_idx = lax.broadcasted_iota(jnp.int32, (chunk, chunk), 1)
    lane = lax.broadcasted_iota(jnp.int32, (1, LANES), 1)
    instances = [(hh, d) for d in range(2) for hh in range(N_ML_HEADS)]

    def body(i, _):
        rows = [pl.ds(pl.multiple_of(c * chunk, chunk), chunk) for c in (i, n_chunks - 1 - i)]
        valid = [s_idx <= t_idx, s_idx >= t_idx]
        m_prev = mrun_ref[...]

        def qkv(hh, d):
            sl = slice(hh * dh, (hh + 1) * dh)
            return q_ref[rows[d], sl], k_ref[rows[d], sl], v_ref[rows[d], sl]

        qk = [_dot_nt(*qkv(hh, d)[0:2]) for hh, d in instances]
        if use_state:
            cn = [cn_ref[d, hh] for hh, d in instances]
            qc = [_dot(qkv(hh, d)[0], cn[n].astype(BF16)) for n, (hh, d) in enumerate(instances)]

        cs, a_rows, w, decay, m_new = [], [], [], [], []
        for d in range(2):
            tri = jnp.where(valid[d], 1.0, 0.0).astype(BF16)
            gi = gate_ref[rows[d], 0:LANES]
            lf = _log_sigmoid(gate_ref[rows[d], LANES:2 * LANES])
            lf_hi = lf.astype(BF16)
            r1 = lf - lf_hi.astype(F32)
            lf_mid = r1.astype(BF16)
            lf_lo = (r1 - lf_mid.astype(F32)).astype(BF16)
            cs_d = _dot(tri, lf_hi) + _dot(tri, lf_mid) + _dot(tri, lf_lo)
            b_last = jnp.sum(lf, axis=0, keepdims=True)
            w_end = b_last + gi - cs_d
            m_d = jnp.maximum(b_last + m_prev, jnp.max(w_end, axis=0, keepdims=True))
            cs.append(cs_d)
            a_rows.append((gi - cs_d).T[0:8])
            decay.append(jnp.exp(b_last + m_prev - m_d))
            w.append(jnp.exp(w_end - m_d))
            m_new.append(m_d)
        mrun_ref[...] = jnp.where((lane & 1) == 0, m_new[0], m_new[1])

        s_all, g_all = [], []
        for n, (hh, d) in enumerate(instances):
            j = 2 * hh + d
            a = jnp.where(valid[d], a_rows[d][j:j + 1, :], -jnp.inf)
            g = jnp.maximum(jnp.max(a, axis=1, keepdims=True), m_prev[:, j:j + 1])
            s_all.append(qk[n] * jnp.exp(a - g))
            g_all.append(g)

        for n, (hh, d) in enumerate(instances):
            j = 2 * hh + d
            sl = slice(hh * dh, (hh + 1) * dh)
            _, k, v = qkv(hh, d)
            s, g = s_all[n], g_all[n]
            num = _dot(s.astype(BF16), v)
            den = jnp.sum(s, axis=1, keepdims=True)
            if use_state:
                w_inter = jnp.exp(m_prev[:, j:j + 1] - g)
                num = num + w_inter * qc[n][:, 0:dh]
                den = den + w_inter * qc[n][:, dh:2 * dh]
            h = num / jnp.maximum(jnp.abs(den), jnp.exp(-(cs[d][:, j:j + 1] + g)))
            (hb_ref if d else hf_ref)[rows[d], sl] = h
            kw = (k.astype(F32) * w[d][:, j:j + 1]).astype(BF16)
            v1 = jnp.concatenate([v, jnp.ones_like(v)], axis=1)
            upd = _dot_tn(kw, v1)
            if use_state:
                upd = decay[d][:, j:j + 1] * cn[n] + upd
            cn_ref[d, hh] = upd
        return 0

    def finish(c, _):
        rows = pl.ds(pl.multiple_of(c * chunk, chunk), chunk)
        for hh in range(N_ML_HEADS):
            sl = slice(hh * dh, (hh + 1) * dh)
            hs = hf_ref[rows, sl] + hb_ref[rows, sl]
            y = hs * lax.rsqrt(jnp.mean(hs * hs, axis=-1, keepdims=True) + EPS) * g_ref[:, sl]
            o_ref[rows, sl] = (y * jax.nn.sigmoid(mo_ref[rows, sl])).astype(o_ref.dtype)
        return 0

    if n_chunks == 1:
        body(0, 0)
        finish(0, 0)
    else:
        lax.fori_loop(0, n_chunks, body, 0)
        lax.fori_loop(0, n_chunks, finish, 0)
    for d in range(2):
        for hh in range(N_ML_HEADS):
            cn = cn_ref[d, hh]
            c_ref[0, d, hh] = cn[:, 0:dh]
            n_ref[0, d, hh] = cn[:, dh:2 * dh].T[0:1, :]
    m_ref[0] = mrun_ref[...]


def _mlstm(mq, mk, mv, mo, gates, ml_g, state, n_batch, seq_len, chunk):
    dh = ML_HEAD_DIM
    nh = N_ML_HEADS
    has_state = state is not None
    seq = pl.BlockSpec((seq_len, nh * dh), lambda b: (b, 0))
    st_c = pl.BlockSpec((1, 2, nh, dh, dh), lambda b: (b, 0, 0, 0, 0))
    st_n = pl.BlockSpec((1, 2, nh, 1, dh), lambda b: (b, 0, 0, 0, 0))
    st_m = pl.BlockSpec((1, 1, LANES), lambda b: (b, 0, 0))
    in_specs = [seq, seq, seq, seq, pl.BlockSpec((seq_len, 2 * LANES), lambda b: (b, 0)), _const_spec((1, nh * dh))]
    args = [mq, mk, mv, mo, gates, ml_g]
    if has_state:
        in_specs += [st_c, st_n, st_m]
        args += list(state)
    out_shape = [jax.ShapeDtypeStruct((n_batch * seq_len, nh * dh), BF16),
                 jax.ShapeDtypeStruct((n_batch, 2, nh, dh, dh), F32),
                 jax.ShapeDtypeStruct((n_batch, 2, nh, 1, dh), F32),
                 jax.ShapeDtypeStruct((n_batch, 1, LANES), F32)]
    return pl.pallas_call(
        functools.partial(_mlstm_kernel, chunk=chunk, n_chunks=seq_len // chunk, has_state=has_state),
        grid=(n_batch,),
        in_specs=in_specs,
        out_specs=[seq, st_c, st_n, st_m],
        out_shape=out_shape,
        scratch_shapes=[pltpu.VMEM((seq_len, nh * dh), F32), pltpu.VMEM((seq_len, nh * dh), F32),
                        pltpu.VMEM((2, nh, dh, 2 * dh), F32), pltpu.VMEM((1, LANES), F32)],
        compiler_params=_params(1),
        name="mlstm_latent" if has_state else "mlstm_ctx",
    )(*args)


def _pack_gate_lanes(m):
    b = m.shape[0]
    packed = m.transpose(0, 2, 1).reshape(b, 2 * N_ML_HEADS)
    return jnp.pad(packed, ((0, 0), (0, LANES - 2 * N_ML_HEADS))).reshape(b, 1, LANES)


def _unpack_gate_lanes(m):
    b = m.shape[0]
    return m[:, 0, :2 * N_ML_HEADS].reshape(b, N_ML_HEADS, 2).transpose(0, 2, 1)


def _outproj_kernel(oa_ref, om_ref, x_ref, mod_ref, g2_ref, wa_ref, wm_ref, x1_ref, h2_ref):
    d = x_ref.shape[-1]
    mod = mod_ref[0]
    g1 = mod[:, 2 * d:3 * d]
    sh2, sc2 = mod[:, 3 * d:4 * d], mod[:, 4 * d:5 * d]
    out = _dot(oa_ref[...], wa_ref[...]) + _dot(om_ref[...], wm_ref[...])
    x1 = x_ref[...] + g1 * out
    x1_ref[...] = x1
    y = x1 * lax.rsqrt(jnp.mean(x1 * x1, axis=-1, keepdims=True) + EPS)
    h2_ref[...] = ((y * g2_ref[...]) * (1.0 + sc2) + sh2).astype(h2_ref.dtype)


def _out_projection(o_att, o_ml, x, mod, mod_row, seq_len, wts):
    n_tok, d = x.shape
    tm = _token_tile(n_tok, seq_len)

    def tok(width):
        return pl.BlockSpec((tm, width), lambda i: (i, 0))

    return pl.pallas_call(
        _outproj_kernel,
        grid=(n_tok // tm,),
        in_specs=[tok(ATT_WIDTH), tok(ML_WIDTH), tok(d),
                  pl.BlockSpec((1, 1, mod.shape[-1]), lambda i: (mod_row(i, tm), 0, 0)),
                  _const_spec((1, d)), _const_spec((ATT_WIDTH, d)), _const_spec((ML_WIDTH, d))],
        out_specs=[tok(d), tok(d)],
        out_shape=[jax.ShapeDtypeStruct((n_tok, d), F32), jax.ShapeDtypeStruct((n_tok, d), BF16)],
        compiler_params=_params(1),
        name="out_proj",
    )(o_att, o_ml, x, mod, wts["norm2_g"], wts["w_out_att"], wts["w_out_ml"])


def _ffn_kernel(h_ref, hprev_ref, hnext_ref, x1_ref, mod_ref, wup_ref, cw_ref, cb_ref, wd_ref,
                y_ref, lhs_ref, ug_ref, uv_ref, acc_ref, *, seq_len, n_col_tiles):
    tm, d = x1_ref.shape
    lhs_ref[0:tm, :] = h_ref[...]
    halo_row = lax.broadcasted_iota(jnp.int32, (HALO, d), 0)
    lhs_ref[tm:tm + HALO, :] = jnp.where(halo_row < HALO // 2, hnext_ref[...], hprev_ref[...])
    acc_ref[...] = jnp.zeros_like(acc_ref)
    pos = (pl.program_id(0) * tm + lax.broadcasted_iota(jnp.int32, (tm, 1), 0)) % seq_len
    first = pos == 0
    last = pos == seq_len - 1

    def conv(u, cw, cb):
        prev = jnp.where(first, 0.0, pltpu.roll(u, 1, 0)[0:tm])
        nxt = jnp.where(last, 0.0, pltpu.roll(u, tm + HALO - 1, 0)[0:tm])
        return prev * cw[0:1] + u[0:tm] * cw[1:2] + nxt * cw[2:3] + cb

    def up(j, slot):
        lhs = lhs_ref[...]
        ug_ref[slot] = _dot(lhs, wup_ref[j])
        uv_ref[slot] = _dot(lhs, wup_ref[n_col_tiles + j])

    def down(j, slot):
        gate = conv(ug_ref[slot], cw_ref[j], cb_ref[j])
        val = conv(uv_ref[slot], cw_ref[n_col_tiles + j], cb_ref[n_col_tiles + j])
        act = (gate * jax.nn.sigmoid(gate)) * val
        return _dot(act.astype(BF16), wd_ref[j])

    def pair(i, _):
        up(2 * i + 1, 1)
        acc_ref[...] += down(2 * i, 0)
        up(2 * i + 2, 0)
        acc_ref[...] += down(2 * i + 1, 1)
        return 0

    up(0, 0)
    n_pairs = (n_col_tiles - 1) // 2
    if n_pairs:
        lax.fori_loop(0, n_pairs, pair, 0)
    j0 = 2 * n_pairs
    if n_col_tiles - j0 == 2:
        up(j0 + 1, 1)
        acc_ref[...] += down(j0, 0)
        tail = down(j0 + 1, 1)
    else:
        tail = down(j0, 0)
    g2 = mod_ref[0][:, 5 * d:6 * d]
    y_ref[...] = x1_ref[...] + g2 * (acc_ref[...] + tail)


def _conv_ffn(h2, x1, mod, mod_row, seq_len, wts):
    n_tok, d = x1.shape
    tm = _token_tile(n_tok, seq_len)
    hpt = tm // HALO
    n_halo = n_tok // HALO
    nct, tn, _ = wts["w_down"].shape

    def tok(width):
        return pl.BlockSpec((tm, width), lambda i: (i, 0))

    in_specs = [tok(d),
                pl.BlockSpec((HALO, d), lambda i: (jnp.maximum(i * hpt - 1, 0), 0)),
                pl.BlockSpec((HALO, d), lambda i: (jnp.minimum((i + 1) * hpt, n_halo - 1), 0)),
                tok(d),
                pl.BlockSpec((1, 1, mod.shape[-1]), lambda i: (mod_row(i, tm), 0, 0)),
                _const_spec((2 * nct, d, tn)), _const_spec((2 * nct, 3, tn)), _const_spec((2 * nct, 1, tn)),
                _const_spec((nct, tn, d))]
    return pl.pallas_call(
        functools.partial(_ffn_kernel, seq_len=seq_len, n_col_tiles=nct),
        grid=(n_tok // tm,),
        in_specs=in_specs,
        out_specs=tok(d),
        out_shape=jax.ShapeDtypeStruct((n_tok, d), F32),
        scratch_shapes=[pltpu.VMEM((tm + HALO, d), BF16),
                        pltpu.VMEM((2, tm + HALO, tn), F32), pltpu.VMEM((2, tm + HALO, tn), F32),
                        pltpu.VMEM((tm, d), F32)],
        compiler_params=_params(1),
        name="conv_ffn",
    )(h2, h2, h2, x1, mod, wts["w_up"], wts["conv_w"], wts["conv_b"], wts["w_down"])


def _retile_cast_kernel(w_ref, o_ref):
    o_ref[0] = w_ref[...].astype(o_ref.dtype)


def _retile_cast(w, tn):
    r, c = w.shape
    return pl.pallas_call(
        _retile_cast_kernel,
        grid=(c // tn,),
        in_specs=[pl.BlockSpec((r, tn), lambda j: (0, j))],
        out_specs=pl.BlockSpec((1, r, tn), lambda j: (j, 0, 0)),
        out_shape=jax.ShapeDtypeStruct((c // tn, r, tn), BF16),
        compiler_params=_params(1),
        name="retile_cast",
    )(w)


def _layer_weights(norm1_g, norm2_g, w_in, b_gate, q_g, k_g, ml_g, w_out, w_up, conv_w, conv_b, w_down):
    d = w_in.shape[0]
    a, m, nh = ATT_WIDTH, ML_WIDTH, N_ML_HEADS
    def gate_lanes(g):
        g = g.reshape(g.shape[0], 2, 2, nh).transpose(0, 2, 3, 1).reshape(g.shape[0], 2, 2 * nh)
        return jnp.pad(g, ((0, 0), (0, 0), (0, LANES - 2 * nh))).reshape(g.shape[0], 2 * LANES)

    w_g = gate_lanes(w_in[:, 3 * a + 4 * m:])
    b_g = gate_lanes(b_gate.astype(F32).reshape(1, 4 * nh))
    head_id = jnp.arange(a) // ATT_HEAD_DIM
    d_ff = w_down.shape[0]
    tn = MXU_WIDTH
    nct = d_ff // tn

    def col_tiles(w):
        return w.reshape(w.shape[0], 2 * nct, tn).transpose(1, 0, 2)

    return {
        "norm1_g": norm1_g.reshape(1, d), "norm2_g": norm2_g.reshape(1, d),
        "w_att": w_in[:, :3 * a].astype(BF16), "w_ml": w_in[:, 3 * a:3 * a + 4 * m].astype(BF16),
        "w_g": w_g.astype(BF16), "b_g": b_g,
        "q_g": jnp.tile(q_g, N_ATT_HEADS).reshape(1, a), "k_g": jnp.tile(k_g, N_ATT_HEADS).reshape(1, a),
        "blockdiag": (head_id[:, None] == head_id[None, :]).astype(BF16),
        "ml_g": ml_g.reshape(1, m),
        "w_out_att": w_out[:a].astype(BF16), "w_out_ml": w_out[a:].astype(BF16),
        "w_up": _retile_cast(w_up, tn),
        "conv_w": col_tiles(conv_w.astype(F32)), "conv_b": col_tiles(conv_b.astype(F32).reshape(1, -1)),
        "w_down": w_down.astype(BF16).reshape(nct, tn, w_down.shape[1]),
    }


def _rope_tables(seq_len):
    quarter = ML_HEAD_DIM // 4
    pos = jnp.arange(seq_len)
    inv_freq = ROPE_THETA ** (-jnp.arange(quarter, dtype=F32) / quarter)
    ang_r = (pos // GRID_W).astype(F32)[:, None] * inv_freq[None, :]
    ang_c = (pos % GRID_W).astype(F32)[:, None] * inv_freq[None, :]
    cos = jnp.concatenate([jnp.cos(ang_r)] * 2 + [jnp.cos(ang_c)] * 2, axis=-1)
    sin = jnp.concatenate([-jnp.sin(ang_r), jnp.sin(ang_r), -jnp.sin(ang_c), jnp.sin(ang_c)], axis=-1)
    return cos, sin


def _layer(x, mod, mod_row, n_batch, seq_len, wts, *, latent, ctx_kv=None, rpb=None, state=None, rope=None):
    chunk = min(ML_CHUNK, seq_len)
    kv_dtype = BF16 if latent else F32
    aq, ak, av, mq, mk, mv, mo, gt = _in_projection(
        x, mod, mod_row, seq_len, wts, rope, latent=latent, kv_dtype=kv_dtype)
    if latent:
        o_att = _neighborhood_attention(aq, ak, av, ctx_kv[0], ctx_kv[1], rpb, n_batch, seq_len)
    else:
        o_att = _context_attention(aq, ak, av, seq_len)
    o_ml, c_f, n_f, m_f = _mlstm(mq, mk, mv, mo, gt, wts["ml_g"], state, n_batch, seq_len, chunk)
    x1, h2 = _out_projection(o_att, o_ml, x, mod, mod_row, seq_len, wts)
    y = _conv_ffn(h2, x1, mod, mod_row, seq_len, wts)
    return y, ak, av, (c_f, n_f[:, :, :, 0, :], _unpack_gate_lanes(m_f))


def kernel(x_prompt, x_sample, cache_k, cache_v, state_C, state_n, state_m, c, c_ctx, w_mod, b_mod, norm1_g,
           norm2_g, w_in, b_gate, q_norm_g, k_norm_g, rpb, ml_norm_g, w_out, w_up, conv_w, conv_b, w_down):
    batch, seq, d = x_prompt.shape
    dec_batch, dec_seq, _ = x_sample.shape
    depth = w_mod.shape[0]
    past = cache_k.shape[2]
    cvecs = jnp.concatenate([c_ctx[None, :], c], axis=0).astype(F32)
    rope = _rope_tables(dec_seq)

    def ctx_row(i, tm):
        return 0 * i

    def lat_row(i, tm):
        return 1 + (i * tm) // dec_seq

    xp = x_prompt.reshape(batch * seq, d)
    xs = x_sample.reshape(dec_batch * dec_seq, d)
    ks, vs, cs, ns, ms = [], [], [], [], []
    for l in range(depth):
        wts = _layer_weights(norm1_g[l], norm2_g[l], w_in[l], b_gate[l], q_norm_g[l], k_norm_g[l],
                             ml_norm_g[l], w_out[l], w_up[l], conv_w[l], conv_b[l], w_down[l])
        mod = _modulation(cvecs, w_mod[l], b_mod[l])

        xp, ak, av, st = _layer(xp, mod, ctx_row, batch, seq, wts, latent=False)
        ks.append(ak.reshape(batch, seq, N_ATT_HEADS, ATT_HEAD_DIM))
        vs.append(av.reshape(batch, seq, N_ATT_HEADS, ATT_HEAD_DIM))
        cs.append(st[0])
        ns.append(st[1])
        ms.append(st[2])

        state = (state_C[:, l].astype(F32),
                 state_n[:, l].astype(F32)[:, :, :, None, :],
                 _pack_gate_lanes(state_m[:, l].astype(F32)))
        ctx_kv = (cache_k[:, l].reshape(dec_batch, past, ATT_WIDTH).astype(BF16),
                  cache_v[:, l].reshape(dec_batch, past, ATT_WIDTH).transpose(0, 2, 1).astype(BF16))
        xs, _, _, _ = _layer(xs, mod, lat_row, dec_batch, dec_seq, wts, latent=True, ctx_kv=ctx_kv,
                             rpb=rpb[l], state=state, rope=rope)
    return (xp.reshape(batch, seq, d), xs.reshape(dec_batch, dec_seq, d),
            jnp.stack(ks, axis=1), jnp.stack(vs, axis=1),
            jnp.stack(cs, axis=1), jnp.stack(ns, axis=1), jnp.stack(ms, axis=1))
```

```python
import functools

import jax
import jax.numpy as jnp
from jax import lax
from jax.experimental import pallas as pl
from jax.experimental.pallas import tpu as pltpu

F32 = jnp.float32
BF16 = jnp.bfloat16

GRID_W = 64
N_ATT_HEADS = 8
ATT_HEAD_DIM = 64
ATT_WIDTH = N_ATT_HEADS * ATT_HEAD_DIM
WIN_ROWS = 8
WIN_COLS = 16
HEAD_GROUP = 4
N_ML_HEADS = 4
ML_HEAD_DIM = 128
ML_WIDTH = N_ML_HEADS * ML_HEAD_DIM
ROPE_THETA = 10000.0
EPS = 1e-6

LANES = 128
MXU_WIDTH = 256
ML_CHUNK = 256
GATE_ROWS = 8
HALO = 16
TOKEN_TILE = 512
VMEM_LIMIT = 56 * 1024 * 1024


def _dot(a, b):
    return jnp.dot(a, b, preferred_element_type=F32)


def _dot_nt(a, b):
    return lax.dot_general(a, b, (((1,), (1,)), ((), ())), preferred_element_type=F32)


def _dot_tn(a, b):
    return lax.dot_general(a, b, (((0,), (0,)), ((), ())), preferred_element_type=F32)


def _const_spec(shape):
    nd = len(shape)
    return pl.BlockSpec(shape, lambda *_: (0,) * nd, pipeline_mode=pl.Buffered(1))


def _params(n_axes):
    return pltpu.CompilerParams(dimension_semantics=("arbitrary",) * n_axes,
                                vmem_limit_bytes=VMEM_LIMIT)


def _token_tile(n_tok, seq_len):
    tm = min(TOKEN_TILE, n_tok)
    while n_tok % tm or (seq_len % tm and tm % seq_len):
        tm //= 2
    return tm


def _log_sigmoid(x):
    return jnp.minimum(x, 0.0) - jnp.log1p(jnp.exp(-jnp.abs(x)))


def _mod_kernel(c_ref, w_ref, b_ref, o_ref):
    c = c_ref[...]
    s = c * jax.nn.sigmoid(c)
    o_ref[...] = _dot(s.astype(BF16), w_ref[...].astype(BF16)) + b_ref[...]


def _modulation(cvecs, w_mod, b_mod):
    r, d = cvecs.shape
    n = w_mod.shape[1]
    tn = d
    out = pl.pallas_call(
        _mod_kernel,
        grid=(n // tn,),
        in_specs=[pl.BlockSpec((r, d), lambda j: (0, 0)),
                  pl.BlockSpec((d, tn), lambda j: (0, j)),
                  pl.BlockSpec((1, tn), lambda j: (0, j))],
        out_specs=pl.BlockSpec((r, tn), lambda j: (0, j)),
        out_shape=jax.ShapeDtypeStruct((r, n), F32),
        compiler_params=_params(1),
        name="adaln_mod",
    )(cvecs, w_mod, b_mod.reshape(1, n))
    return out.reshape(r, 1, n)


def _rope(x, cos, sin_signed):
    lane = lax.broadcasted_iota(jnp.int32, x.shape, 1)
    partner = jnp.where((lane & 32) == 0, pltpu.roll(x, LANES - 32, 1), pltpu.roll(x, 32, 1))
    return x * cos + partner * sin_signed


def _inproj_kernel(*refs, latent):
    if latent:
        (x_ref, mod_ref, g1_ref, watt_ref, wml_ref, wg_ref, bg_ref, qg_ref, kg_ref, bd_ref, cos_ref, sin_ref,
         aq_ref, ak_ref, av_ref, mq_ref, mk_ref, mv_ref, mo_ref, gt_ref, gtt_ref) = refs
    else:
        (x_ref, mod_ref, g1_ref, watt_ref, wml_ref, wg_ref, bg_ref, qg_ref, kg_ref, bd_ref,
         aq_ref, ak_ref, av_ref, mq_ref, mk_ref, mv_ref, mo_ref, gt_ref, gtt_ref) = refs
    x = x_ref[...]
    d = x.shape[-1]
    mod = mod_ref[0]
    sh1, sc1 = mod[:, 0:d], mod[:, d:2 * d]
    y = x * lax.rsqrt(jnp.mean(x * x, axis=-1, keepdims=True) + EPS)
    h = (y * g1_ref[...]) * (1.0 + sc1) + sh1
    hb = h.astype(BF16)

    def head_norm(a, g):
        ss = _dot((a * a).astype(BF16), bd_ref[...])
        return a * lax.rsqrt(ss * (1.0 / ATT_HEAD_DIM) + EPS) * g

    att = _dot(hb, watt_ref[...])
    w = ATT_WIDTH
    aq_ref[...] = (head_norm(att[:, 0:w], qg_ref[...]) * ATT_HEAD_DIM ** -0.5).astype(aq_ref.dtype)
    ak_ref[...] = head_norm(att[:, w:2 * w], kg_ref[...]).astype(ak_ref.dtype)
    if latent:
        av_ref[...] = att[:, 2 * w:3 * w].T.astype(av_ref.dtype)
    else:
        av_ref[...] = att[:, 2 * w:3 * w].astype(av_ref.dtype)

    w = ML_WIDTH
    mq = _dot(hb, wml_ref[:, 0:w])
    mk = _dot(hb, wml_ref[:, w:2 * w]) * ML_HEAD_DIM ** -0.5
    if latent:
        cos, sin = cos_ref[...], sin_ref[...]
        for hh in range(N_ML_HEADS):
            sl = slice(hh * ML_HEAD_DIM, (hh + 1) * ML_HEAD_DIM)
            mq_ref[:, sl] = _rope(mq[:, sl], cos, sin).astype(BF16)
            mk_ref[:, sl] = _rope(mk[:, sl], cos, sin).astype(BF16)
    else:
        mq_ref[...] = mq.astype(BF16)
        mk_ref[...] = mk.astype(BF16)
    mv_ref[...] = _dot(hb, wml_ref[:, 2 * w:3 * w]).T.astype(BF16)
    mo_ref[...] = _dot(hb, wml_ref[:, 3 * w:4 * w]).T
    gates = _dot(hb, wg_ref[...]) + bg_ref[...]
    gt_ref[...] = gates
    gtt_ref[0:GATE_ROWS, :] = gates[:, 0:LANES].T[0:GATE_ROWS]
    gtt_ref[GATE_ROWS:2 * GATE_ROWS, :] = gates[:, LANES:2 * LANES].T[0:GATE_ROWS]


def _in_projection(x, mod, mod_row, seq_len, wts, rope, *, latent, kv_dtype):
    n_tok, d = x.shape
    tm = _token_tile(n_tok, seq_len)
    tiles_per_seq = max(seq_len // tm, 1)

    def tok(width):
        return pl.BlockSpec((tm, width), lambda i: (i, 0))

    in_specs = [tok(d),
                pl.BlockSpec((1, 1, mod.shape[-1]), lambda i: (mod_row(i, tm), 0, 0)),
                _const_spec((1, d)),
                _const_spec(wts["w_att"].shape), _const_spec(wts["w_ml"].shape),
                _const_spec(wts["w_g"].shape), _const_spec((1, 2 * LANES)),
                _const_spec((1, ATT_WIDTH)), _const_spec((1, ATT_WIDTH)),
                _const_spec((ATT_WIDTH, ATT_WIDTH))]
    args = [x, mod, wts["norm1_g"], wts["w_att"], wts["w_ml"], wts["w_g"], wts["b_g"],
            wts["q_g"], wts["k_g"], wts["blockdiag"]]
    if latent:
        in_specs += [pl.BlockSpec((tm, LANES), lambda i: (i % tiles_per_seq, 0))] * 2
        args += [rope[0], rope[1]]
    out_shape = [jax.ShapeDtypeStruct((n_tok, ATT_WIDTH), BF16),
                 jax.ShapeDtypeStruct((n_tok, ATT_WIDTH), kv_dtype),
                 jax.ShapeDtypeStruct((ATT_WIDTH, n_tok) if latent else (n_tok, ATT_WIDTH), kv_dtype),
                 jax.ShapeDtypeStruct((n_tok, ML_WIDTH), BF16),
                 jax.ShapeDtypeStruct((n_tok, ML_WIDTH), BF16),
                 jax.ShapeDtypeStruct((ML_WIDTH, n_tok), BF16),
                 jax.ShapeDtypeStruct((ML_WIDTH, n_tok), F32),
                 jax.ShapeDtypeStruct((n_tok, 2 * LANES), F32),
                 jax.ShapeDtypeStruct((2 * GATE_ROWS, n_tok), F32)]

    def tok_t(height):
        return pl.BlockSpec((height, tm), lambda i: (0, i))

    av_spec = tok_t(ATT_WIDTH) if latent else tok(ATT_WIDTH)
    out_specs = ([tok(ATT_WIDTH)] * 2 + [av_spec] + [tok(ML_WIDTH)] * 2 + [tok_t(ML_WIDTH)] * 2
                 + [tok(2 * LANES), tok_t(2 * GATE_ROWS)])
    return pl.pallas_call(
        functools.partial(_inproj_kernel, latent=latent),
        grid=(n_tok // tm,),
        in_specs=in_specs,
        out_specs=out_specs,
        out_shape=out_shape,
        compiler_params=_params(1),
        name="in_proj_latent" if latent else "in_proj_ctx",
    )(*args)


def _ctx_attn_kernel(q_ref, k_ref, v_ref, o_ref):
    for hh in range(N_ATT_HEADS):
        sl = slice(hh * ATT_HEAD_DIM, (hh + 1) * ATT_HEAD_DIM)
        q = q_ref[:, sl]
        k = k_ref[:, sl].astype(BF16)
        v = v_ref[:, sl].astype(BF16)
        s = _dot_nt(q, k)
        p = jnp.exp(s - jnp.max(s, axis=-1, keepdims=True))
        l = jnp.sum(p, axis=-1, keepdims=True)
        o_ref[:, sl] = (_dot(p.astype(BF16), v) / l).astype(o_ref.dtype)


def _context_attention(q, k, v, seq_len):
    n_tok = q.shape[0]
    spec = pl.BlockSpec((seq_len, ATT_WIDTH), lambda b: (b, 0))
    return pl.pallas_call(
        _ctx_attn_kernel,
        grid=(n_tok // seq_len,),
        in_specs=[spec, spec, spec],
        out_specs=spec,
        out_shape=jax.ShapeDtypeStruct((n_tok, ATT_WIDTH), BF16),
        compiler_params=_params(1),
        name="ctx_attn",
    )(q, k, v)


def _nbr_attn_kernel(q_ref, k_ref, vt_ref, kc_ref, vct_ref, bias_ref, o_ref, pt_ref, *, rows, kr):
    r = pl.program_id(1)
    n_win = kr + 2
    rs = jnp.clip(r - kr // 2, 0, rows - kr)
    start = jnp.minimum(rs - (rs & 1), rows - n_win)
    delta = rs - start
    bias0 = pl.multiple_of((WIN_ROWS - 1 - (r - rs)) * GRID_W, GRID_W)
    n_loc = kr * GRID_W
    gw = HEAD_GROUP * ATT_HEAD_DIM
    lane_head = lax.broadcasted_iota(jnp.int32, (GRID_W, gw), 1) // ATT_HEAD_DIM
    zeros2 = jnp.zeros((2 * GRID_W, gw), BF16)
    for g in range(N_ATT_HEADS // HEAD_GROUP):
        sl = slice(g * gw, (g + 1) * gw)
        q4 = q_ref[:, sl]
        qbd = jnp.concatenate([jnp.where(lane_head == hl, q4, jnp.zeros_like(q4))
                               for hl in range(HEAD_GROUP)], axis=0)
        s_loc = _dot_nt(k_ref[pl.ds(pl.multiple_of(rs * GRID_W, GRID_W), n_loc), sl], qbd)
        s_loc = s_loc + bias_ref[g, pl.ds(bias0, n_loc), :]
        s_ctx = _dot_nt(kc_ref[0, :, sl], qbd)
        m = jnp.maximum(jnp.max(s_loc, axis=0, keepdims=True), jnp.max(s_ctx, axis=0, keepdims=True))
        p_loc = jnp.exp(s_loc - m)
        p_ctx = jnp.exp(s_ctx - m)
        l = jnp.sum(p_loc, axis=0, keepdims=True) + jnp.sum(p_ctx, axis=0, keepdims=True)
        pt_ref[g, 0:2 * GRID_W, :] = zeros2
        pt_ref[g, n_loc:n_loc + 2 * GRID_W, :] = zeros2
        pt_ref[g, pl.ds(pl.multiple_of(delta * GRID_W, GRID_W), n_loc), :] = p_loc.astype(BF16)
        vt_win = vt_ref[sl, pl.ds(pl.multiple_of(start * GRID_W, 2 * GRID_W), n_win * GRID_W)]
        ot = _dot(vt_win, pt_ref[g]) + _dot(vct_ref[0, sl, :], p_ctx.astype(BF16))
        o4 = (ot / l).T
        out = jnp.where(lane_head == 0, o4[0:GRID_W], 0.0)
        for hl in range(1, HEAD_GROUP):
            out = out + jnp.where(lane_head == hl, o4[hl * GRID_W:(hl + 1) * GRID_W], 0.0)
        o_ref[:, sl] = out.astype(o_ref.dtype)


def _nbr_bias_table(rpb, rows):
    col = jnp.arange(GRID_W)
    cs = jnp.clip(col - WIN_COLS // 2, 0, GRID_W - WIN_COLS)
    in_win = (col[None, :] >= cs[:, None]) & (col[None, :] < cs[:, None] + WIN_COLS)
    dc_idx = jnp.clip(col[None, :] - col[:, None] + (WIN_COLS - 1), 0, 2 * WIN_COLS - 2)
    n_dr, n_dc = rpb.shape[1], rpb.shape[2]
    onehot = (dc_idx[None] == jnp.arange(n_dc)[:, None, None]).astype(F32)
    t = jnp.einsum("hrc,cqk->hrqk", rpb.astype(F32), onehot, precision=lax.Precision.HIGHEST)
    t = jnp.where(in_win[None, None], t, -jnp.inf)
    n_groups = N_ATT_HEADS // HEAD_GROUP
    t = t.reshape(n_groups, HEAD_GROUP, n_dr, GRID_W, GRID_W).transpose(0, 2, 4, 1, 3)
    return t.reshape(n_groups, n_dr * GRID_W, HEAD_GROUP * GRID_W)


def _neighborhood_attention(q, k, vt, k_ctx, vt_ctx, rpb, n_batch, seq_len):
    rows = seq_len // GRID_W
    kr = min(WIN_ROWS, rows)
    n_win = kr + 2
    assert rows >= n_win and (rows - n_win) % 2 == 0
    bias = _nbr_bias_table(rpb, rows)
    past = k_ctx.shape[1]
    gw = HEAD_GROUP * ATT_HEAD_DIM
    row_spec = pl.BlockSpec((GRID_W, ATT_WIDTH), lambda b, r: (b * rows + r, 0))
    return pl.pallas_call(
        functools.partial(_nbr_attn_kernel, rows=rows, kr=kr),
        grid=(n_batch, rows),
        in_specs=[row_spec,
                  pl.BlockSpec((seq_len, ATT_WIDTH), lambda b, r: (b, 0)),
                  pl.BlockSpec((ATT_WIDTH, seq_len), lambda b, r: (0, b)),
                  pl.BlockSpec((1, past, ATT_WIDTH), lambda b, r: (b, 0, 0)),
                  pl.BlockSpec((1, ATT_WIDTH, past), lambda b, r: (b, 0, 0)),
                  _const_spec(bias.shape)],
        out_specs=row_spec,
        out_shape=jax.ShapeDtypeStruct((n_batch * seq_len, ATT_WIDTH), BF16),
        scratch_shapes=[pltpu.VMEM((N_ATT_HEADS // HEAD_GROUP, n_win * GRID_W, gw), BF16)],
        compiler_params=_params(2),
        name="nbr_attn",
    )(q, k, vt, k_ctx, vt_ctx, bias)


def _mlstm_kernel(*refs, chunk, n_chunks, has_state):
    dh = ML_HEAD_DIM
    nrep = 2 * GATE_ROWS
    if has_state:
        (q_ref, k_ref, vt_ref, mot_ref, gate_ref, gtt_ref, g_ref, c0_ref, n0_ref, m0_ref,
         o_ref, c_ref, n_ref, m_ref, hf_ref, hb_ref, cn_ref, mrun_ref) = refs
        for d in range(2):
            for hh in range(N_ML_HEADS):
                cn_ref[d, hh, 0:dh, :] = c0_ref[0, d, hh].T
                cn_ref[d, hh, dh:dh + nrep, :] = jnp.broadcast_to(n0_ref[0, d, hh], (nrep, dh))
        mrun_ref[...] = m0_ref[0]
    else:
        (q_ref, k_ref, vt_ref, mot_ref, gate_ref, gtt_ref, g_ref,
         o_ref, c_ref, n_ref, m_ref, hf_ref, hb_ref, cn_ref, mrun_ref) = refs
        cn_ref[...] = jnp.zeros_like(cn_ref)
        mrun_ref[...] = jnp.zeros_like(mrun_ref)
    use_state = has_state or n_chunks > 1

    i0 = lax.broadcasted_iota(jnp.int32, (chunk, chunk), 0)
    i1 = lax.broadcasted_iota(jnp.int32, (chunk, chunk), 1)
    row_id = lax.broadcasted_iota(jnp.int32, (GATE_ROWS, 1), 0)
    instances = [(hh, d) for d in range(2) for hh in range(N_ML_HEADS)]

    def split3(x):
        hi = x.astype(BF16)
        r1 = x - hi.astype(F32)
        mid = r1.astype(BF16)
        return hi, mid, (r1 - mid.astype(F32)).astype(BF16)

    def body(i, _):
        chunks = (i, n_chunks - 1 - i)
        rows = [pl.ds(pl.multiple_of(c * chunk, chunk), chunk) for c in chunks]
        le = [i0 <= i1, i0 >= i1]
        ge = [i0 >= i1, i0 <= i1]
        m_prev = mrun_ref[...][:, 0:1]

        def qkv(hh, d):
            sl = slice(hh * dh, (hh + 1) * dh)
            return q_ref[rows[d], sl], k_ref[rows[d], sl], vt_ref[sl, rows[d]]

        st = [_dot_nt(qkv(hh, d)[1], qkv(hh, d)[0]) for hh, d in instances]
        if use_state:
            cn = [cn_ref[d, hh] for hh, d in instances]
            qct = [_dot_nt(cn[n].astype(BF16), qkv(hh, d)[0]) for n, (hh, d) in enumerate(instances)]

        a_col, a8, cs8, w8, decay8, m_new = [], [], [], [], [], []
        for d in range(2):
            gi_r = gtt_ref[0:GATE_ROWS, rows[d]]
            lf_r = _log_sigmoid(gtt_ref[GATE_ROWS:2 * GATE_ROWS, rows[d]])
            tri_r = jnp.where(le[d], 1.0, 0.0).astype(BF16)
            cs_r = sum(_dot(t, tri_r) for t in split3(lf_r))
            b_last = jnp.sum(lf_r, axis=1, keepdims=True)
            w_end = b_last + gi_r - cs_r
            m_d = jnp.maximum(b_last + m_prev, jnp.max(w_end, axis=1, keepdims=True))
            a8.append(gi_r - cs_r)
            cs8.append(cs_r)
            decay8.append(jnp.exp(b_last + m_prev - m_d))
            w8.append(jnp.exp(w_end - m_d))
            m_new.append(m_d)
            lf_c = _log_sigmoid(gate_ref[rows[d], LANES:2 * LANES])
            tri_c = jnp.where(ge[d], 1.0, 0.0).astype(BF16)
            cs_c = sum(_dot(tri_c, t) for t in split3(lf_c))
            a_col.append(gate_ref[rows[d], 0:LANES] - cs_c)
        m_next = jnp.where((row_id & 1) == 0, m_new[0], m_new[1])
        mrun_ref[...] = jnp.broadcast_to(m_next, (GATE_ROWS, LANES))

        pt_all, g_all, den_all = [], [], []
        for n, (hh, d) in enumerate(instances):
            j = 2 * hh + d
            a = jnp.where(le[d], a_col[d][:, j:j + 1], -jnp.inf)
            g = jnp.maximum(jnp.max(a, axis=0, keepdims=True), m_prev[j:j + 1])
            pt = st[n] * jnp.exp(a - g)
            pt_all.append(pt)
            g_all.append(g)
            den_all.append(jnp.sum(pt, axis=0, keepdims=True))

        for n, (hh, d) in enumerate(instances):
            j = 2 * hh + d
            sl = slice(hh * dh, (hh + 1) * dh)
            _, k, vt = qkv(hh, d)
            g, den = g_all[n], den_all[n]
            num = _dot(vt, pt_all[n].astype(BF16))
            if use_state:
                w_inter = jnp.exp(m_prev[j:j + 1] - g)
                num = num + w_inter * qct[n][0:dh]
                den = den + w_inter * qct[n][dh:dh + 1]
            scale = 1.0 / jnp.maximum(jnp.abs(den), jnp.exp(-(cs8[d][j:j + 1] + g)))
            (hb_ref if d else hf_ref)[sl, rows[d]] = num * scale
            w_row = w8[d][j:j + 1]
            lhs = jnp.concatenate([(vt.astype(F32) * w_row).astype(BF16),
                                   jnp.broadcast_to(w_row, (nrep, chunk)).astype(BF16)], axis=0)
            upd = _dot(lhs, k)
            if use_state:
                upd = decay8[d][j:j + 1] * cn[n] + upd
            cn_ref[d, hh] = upd
        return 0

    def finish(c, _):
        rows = pl.ds(pl.multiple_of(c * chunk, chunk), chunk)
        for hh in range(N_ML_HEADS):
            sl = slice(hh * dh, (hh + 1) * dh)
            hs = hf_ref[sl, rows] + hb_ref[sl, rows]
            y = hs * lax.rsqrt(jnp.mean(hs * hs, axis=0, keepdims=True) + EPS) * g_ref[sl, :]
            y = y * jax.nn.sigmoid(mot_ref[sl, rows])
            o_ref[rows, sl] = y.T.astype(o_ref.dtype)
        return 0

    if n_chunks == 1:
        body(0, 0)
        finish(0, 0)
    else:
        lax.fori_loop(0, n_chunks, body, 0)
        lax.fori_loop(0, n_chunks, finish, 0)
    for d in range(2):
        for hh in range(N_ML_HEADS):
            cn = cn_ref[d, hh]
            c_ref[0, d, hh] = cn[0:dh].T
            n_ref[0, d, hh] = cn[dh:dh + 1]
    m_ref[0] = mrun_ref[...]


def _mlstm(mq, mk, mvt, mot, gates, gates_t, ml_g, state, n_batch, seq_len, chunk):
    dh = ML_HEAD_DIM
    nh = N_ML_HEADS
    width = nh * dh
    has_state = state is not None
    seq = pl.BlockSpec((seq_len, width), lambda b: (b, 0))
    seq_t = pl.BlockSpec((width, seq_len), lambda b: (0, b))
    st_c = pl.BlockSpec((1, 2, nh, dh, dh), lambda b: (b, 0, 0, 0, 0))
    st_n = pl.BlockSpec((1, 2, nh, 1, dh), lambda b: (b, 0, 0, 0, 0))
    st_m = pl.BlockSpec((1, GATE_ROWS, LANES), lambda b: (b, 0, 0))
    in_specs = [seq, seq, seq_t, seq_t,
                pl.BlockSpec((seq_len, 2 * LANES), lambda b: (b, 0)),
                pl.BlockSpec((2 * GATE_ROWS, seq_len), lambda b: (0, b)),
                _const_spec((width, chunk))]
    args = [mq, mk, mvt, mot, gates, gates_t, jnp.broadcast_to(ml_g.reshape(width, 1), (width, chunk))]
    if has_state:
        in_specs += [st_c, st_n, st_m]
        args += list(state)
    out_shape = [jax.ShapeDtypeStruct((n_batch * seq_len, width), BF16),
                 jax.ShapeDtypeStruct((n_batch, 2, nh, dh, dh), F32),
                 jax.ShapeDtypeStruct((n_batch, 2, nh, 1, dh), F32),
                 jax.ShapeDtypeStruct((n_batch, GATE_ROWS, LANES), F32)]
    return pl.pallas_call(
        functools.partial(_mlstm_kernel, chunk=chunk, n_chunks=seq_len // chunk, has_state=has_state),
        grid=(n_batch,),
        in_specs=in_specs,
        out_specs=[seq, st_c, st_n, st_m],
        out_shape=out_shape,
        scratch_shapes=[pltpu.VMEM((width, seq_len), F32), pltpu.VMEM((width, seq_len), F32),
                        pltpu.VMEM((2, nh, dh + 2 * GATE_ROWS, dh), F32), pltpu.VMEM((GATE_ROWS, LANES), F32)],
        compiler_params=_params(1),
        name="mlstm_latent" if has_state else "mlstm_ctx",
    )(*args)


def _pack_gate_rows(m):
    b = m.shape[0]
    packed = m.transpose(0, 2, 1).reshape(b, 2 * N_ML_HEADS, 1)
    return jnp.broadcast_to(packed, (b, GATE_ROWS, LANES))


def _unpack_gate_rows(m):
    b = m.shape[0]
    return m[:, :, 0].reshape(b, N_ML_HEADS, 2).transpose(0, 2, 1)


def _outproj_kernel(oa_ref, om_ref, x_ref, mod_ref, g2_ref, wa_ref, wm_ref, x1_ref, h2_ref):
    d = x_ref.shape[-1]
    mod = mod_ref[0]
    g1 = mod[:, 2 * d:3 * d]
    sh2, sc2 = mod[:, 3 * d:4 * d], mod[:, 4 * d:5 * d]
    out = _dot(oa_ref[...], wa_ref[...]) + _dot(om_ref[...], wm_ref[...])
    x1 = x_ref[...] + g1 * out
    x1_ref[...] = x1
    y = x1 * lax.rsqrt(jnp.mean(x1 * x1, axis=-1, keepdims=True) + EPS)
    h2_ref[...] = ((y * g2_ref[...]) * (1.0 + sc2) + sh2).astype(h2_ref.dtype)


def _out_projection(o_att, o_ml, x, mod, mod_row, seq_len, wts):
    n_tok, d = x.shape
    tm = _token_tile(n_tok, seq_len)

    def tok(width):
        return pl.BlockSpec((tm, width), lambda i: (i, 0))

    return pl.pallas_call(
        _outproj_kernel,
        grid=(n_tok // tm,),
        in_specs=[tok(ATT_WIDTH), tok(ML_WIDTH), tok(d),
                  pl.BlockSpec((1, 1, mod.shape[-1]), lambda i: (mod_row(i, tm), 0, 0)),
                  _const_spec((1, d)), _const_spec((ATT_WIDTH, d)), _const_spec((ML_WIDTH, d))],
        out_specs=[tok(d), tok(d)],
        out_shape=[jax.ShapeDtypeStruct((n_tok, d), F32), jax.ShapeDtypeStruct((n_tok, d), BF16)],
        compiler_params=_params(1),
        name="out_proj",
    )(o_att, o_ml, x, mod, wts["norm2_g"], wts["w_out_att"], wts["w_out_ml"])


def _ffn_kernel(h_ref, hprev_ref, hnext_ref, x1_ref, mod_ref, wup_ref, cw_ref, cb_ref, wd_ref,
                y_ref, lhs_ref, ug_ref, uv_ref, acc_ref, *, seq_len, n_col_tiles):
    tm, d = x1_ref.shape
    lhs_ref[0:tm, :] = h_ref[...]
    halo_row = lax.broadcasted_iota(jnp.int32, (HALO, d), 0)
    lhs_ref[tm:tm + HALO, :] = jnp.where(halo_row < HALO // 2, hnext_ref[...], hprev_ref[...])
    acc_ref[...] = jnp.zeros_like(acc_ref)
    pos = (pl.program_id(0) * tm + lax.broadcasted_iota(jnp.int32, (tm, 1), 0)) % seq_len
    first = pos == 0
    last = pos == seq_len - 1

    def conv(u, cw, cb):
        prev = jnp.where(first, 0.0, pltpu.roll(u, 1, 0)[0:tm])
        nxt = jnp.where(last, 0.0, pltpu.roll(u, tm + HALO - 1, 0)[0:tm])
        return prev * cw[0:1] + u[0:tm] * cw[1:2] + nxt * cw[2:3] + cb

    def up(j, slot):
        lhs = lhs_ref[...]
        ug_ref[slot] = _dot(lhs, wup_ref[j])
        uv_ref[slot] = _dot(lhs, wup_ref[n_col_tiles + j])

    def down(j, slot):
        gate = conv(ug_ref[slot], cw_ref[j], cb_ref[j])
        val = conv(uv_ref[slot], cw_ref[n_col_tiles + j], cb_ref[n_col_tiles + j])
        act = (gate * jax.nn.sigmoid(gate)) * val
        return _dot(act.astype(BF16), wd_ref[j])

    def pair(i, _):
        up(2 * i + 1, 1)
        acc_ref[...] += down(2 * i, 0)
        up(2 * i + 2, 0)
        acc_ref[...] += down(2 * i + 1, 1)
        return 0

    up(0, 0)
    n_pairs = (n_col_tiles - 1) // 2
    if n_pairs:
        lax.fori_loop(0, n_pairs, pair, 0)
    j0 = 2 * n_pairs
    if n_col_tiles - j0 == 2:
        up(j0 + 1, 1)
        acc_ref[...] += down(j0, 0)
        tail = down(j0 + 1, 1)
    else:
        tail = down(j0, 0)
    g2 = mod_ref[0][:, 5 * d:6 * d]
    y_ref[...] = x1_ref[...] + g2 * (acc_ref[...] + tail)


def _conv_ffn(h2, x1, mod, mod_row, seq_len, wts):
    n_tok, d = x1.shape
    tm = _token_tile(n_tok, seq_len)
    hpt = tm // HALO
    n_halo = n_tok // HALO
    nct, tn, _ = wts["w_down"].shape

    def tok(width):
        return pl.BlockSpec((tm, width), lambda i: (i, 0))

    in_specs = [tok(d),
                pl.BlockSpec((HALO, d), lambda i: (jnp.maximum(i * hpt - 1, 0), 0)),
                pl.BlockSpec((HALO, d), lambda i: (jnp.minimum((i + 1) * hpt, n_halo - 1), 0)),
                tok(d),
                pl.BlockSpec((1, 1, mod.shape[-1]), lambda i: (mod_row(i, tm), 0, 0)),
                _const_spec((2 * nct, d, tn)), _const_spec((2 * nct, 3, tn)), _const_spec((2 * nct, 1, tn)),
                _const_spec((nct, tn, d))]
    return pl.pallas_call(
        functools.partial(_ffn_kernel, seq_len=seq_len, n_col_tiles=nct),
        grid=(n_tok // tm,),
        in_specs=in_specs,
        out_specs=tok(d),
        out_shape=jax.ShapeDtypeStruct((n_tok, d), F32),
        scratch_shapes=[pltpu.VMEM((tm + HALO, d), BF16),
                        pltpu.VMEM((2, tm + HALO, tn), F32), pltpu.VMEM((2, tm + HALO, tn), F32),
                        pltpu.VMEM((tm, d), F32)],
        compiler_params=_params(1),
        name="conv_ffn",
    )(h2, h2, h2, x1, mod, wts["w_up"], wts["conv_w"], wts["conv_b"], wts["w_down"])


def _retile_cast_kernel(w_ref, o_ref):
    o_ref[0] = w_ref[...].astype(o_ref.dtype)


def _retile_cast(w, tn):
    r, c = w.shape
    return pl.pallas_call(
        _retile_cast_kernel,
        grid=(c // tn,),
        in_specs=[pl.BlockSpec((r, tn), lambda j: (0, j))],
        out_specs=pl.BlockSpec((1, r, tn), lambda j: (j, 0, 0)),
        out_shape=jax.ShapeDtypeStruct((c // tn, r, tn), BF16),
        compiler_params=_params(1),
        name="retile_cast",
    )(w)


def _layer_weights(norm1_g, norm2_g, w_in, b_gate, q_g, k_g, ml_g, w_out, w_up, conv_w, conv_b, w_down):
    d = w_in.shape[0]
    a, m, nh = ATT_WIDTH, ML_WIDTH, N_ML_HEADS
    def gate_lanes(g):
        g = g.reshape(g.shape[0], 2, 2, nh).transpose(0, 2, 3, 1).reshape(g.shape[0], 2, 2 * nh)
        return jnp.pad(g, ((0, 0), (0, 0), (0, LANES - 2 * nh))).reshape(g.shape[0], 2 * LANES)

    w_g = gate_lanes(w_in[:, 3 * a + 4 * m:])
    b_g = gate_lanes(b_gate.astype(F32).reshape(1, 4 * nh))
    head_id = jnp.arange(a) // ATT_HEAD_DIM
    d_ff = w_down.shape[0]
    tn = MXU_WIDTH
    nct = d_ff // tn

    def col_tiles(w):
        return w.reshape(w.shape[0], 2 * nct, tn).transpose(1, 0, 2)

    return {
        "norm1_g": norm1_g.reshape(1, d), "norm2_g": norm2_g.reshape(1, d),
        "w_att": w_in[:, :3 * a].astype(BF16), "w_ml": w_in[:, 3 * a:3 * a + 4 * m].astype(BF16),
        "w_g": w_g.astype(BF16), "b_g": b_g,
        "q_g": jnp.tile(q_g, N_ATT_HEADS).reshape(1, a), "k_g": jnp.tile(k_g, N_ATT_HEADS).reshape(1, a),
        "blockdiag": (head_id[:, None] == head_id[None, :]).astype(BF16),
        "ml_g": ml_g.reshape(1, m),
        "w_out_att": w_out[:a].astype(BF16), "w_out_ml": w_out[a:].astype(BF16),
        "w_up": _retile_cast(w_up, tn),
        "conv_w": col_tiles(conv_w.astype(F32)), "conv_b": col_tiles(conv_b.astype(F32).reshape(1, -1)),
        "w_down": w_down.astype(BF16).reshape(nct, tn, w_down.shape[1]),
    }


def _rope_tables(seq_len):
    quarter = ML_HEAD_DIM // 4
    pos = jnp.arange(seq_len)
    inv_freq = ROPE_THETA ** (-jnp.arange(quarter, dtype=F32) / quarter)
    ang_r = (pos // GRID_W).astype(F32)[:, None] * inv_freq[None, :]
    ang_c = (pos % GRID_W).astype(F32)[:, None] * inv_freq[None, :]
    cos = jnp.concatenate([jnp.cos(ang_r)] * 2 + [jnp.cos(ang_c)] * 2, axis=-1)
    sin = jnp.concatenate([-jnp.sin(ang_r), jnp.sin(ang_r), -jnp.sin(ang_c), jnp.sin(ang_c)], axis=-1)
    return cos, sin


def _layer(x, mod, mod_row, n_batch, seq_len, wts, *, latent, ctx_kv=None, rpb=None, state=None, rope=None):
    chunk = min(ML_CHUNK, seq_len)
    kv_dtype = BF16 if latent else F32
    aq, ak, av, mq, mk, mvt, mot, gt, gtt = _in_projection(
        x, mod, mod_row, seq_len, wts, rope, latent=latent, kv_dtype=kv_dtype)
    if latent:
        o_att = _neighborhood_attention(aq, ak, av, ctx_kv[0], ctx_kv[1], rpb, n_batch, seq_len)
    else:
        o_att = _context_attention(aq, ak, av, seq_len)
    o_ml, c_f, n_f, m_f = _mlstm(mq, mk, mvt, mot, gt, gtt, wts["ml_g"], state, n_batch, seq_len, chunk)
    x1, h2 = _out_projection(o_att, o_ml, x, mod, mod_row, seq_len, wts)
    y = _conv_ffn(h2, x1, mod, mod_row, seq_len, wts)
    return y, ak, av, (c_f, n_f[:, :, :, 0, :], _unpack_gate_rows(m_f))


def kernel(x_prompt, x_sample, cache_k, cache_v, state_C, state_n, state_m, c, c_ctx, w_mod, b_mod, norm1_g,
           norm2_g, w_in, b_gate, q_norm_g, k_norm_g, rpb, ml_norm_g, w_out, w_up, conv_w, conv_b, w_down):
    batch, seq, d = x_prompt.shape
    dec_batch, dec_seq, _ = x_sample.shape
    depth = w_mod.shape[0]
    past = cache_k.shape[2]
    cvecs = jnp.concatenate([c_ctx[None, :], c], axis=0).astype(F32)
    rope = _rope_tables(dec_seq)

    def ctx_row(i, tm):
        return 0 * i

    def lat_row(i, tm):
        return 1 + (i * tm) // dec_seq

    xp = x_prompt.reshape(batch * seq, d)
    xs = x_sample.reshape(dec_batch * dec_seq, d)
    ks, vs, cs, ns, ms = [], [], [], [], []
    for l in range(depth):
        wts = _layer_weights(norm1_g[l], norm2_g[l], w_in[l], b_gate[l], q_norm_g[l], k_norm_g[l],
                             ml_norm_g[l], w_out[l], w_up[l], conv_w[l], conv_b[l], w_down[l])
        mod = _modulation(cvecs, w_mod[l], b_mod[l])

        xp, ak, av, st = _layer(xp, mod, ctx_row, batch, seq, wts, latent=False)
        ks.append(ak.reshape(batch, seq, N_ATT_HEADS, ATT_HEAD_DIM))
        vs.append(av.reshape(batch, seq, N_ATT_HEADS, ATT_HEAD_DIM))
        cs.append(st[0])
        ns.append(st[1])
        ms.append(st[2])

        state = (state_C[:, l].astype(F32),
                 state_n[:, l].astype(F32)[:, :, :, None, :],
                 _pack_gate_rows(state_m[:, l].astype(F32)))
        ctx_kv = (cache_k[:, l].reshape(dec_batch, past, ATT_WIDTH).astype(BF16),
                  cache_v[:, l].reshape(dec_batch, past, ATT_WIDTH).transpose(0, 2, 1).astype(BF16))
        xs, _, _, _ = _layer(xs, mod, lat_row, dec_batch, dec_seq, wts, latent=True, ctx_kv=ctx_kv,
                             rpb=rpb[l], state=state, rope=rope)
    return (xp.reshape(batch, seq, d), xs.reshape(dec_batch, dec_seq, d),
            jnp.stack(ks, axis=1), jnp.stack(vs, axis=1),
            jnp.stack(cs, axis=1), jnp.stack(ns, axis=1), jnp.stack(ms, axis=1))
```

```python
import functools

import jax
import jax.numpy as jnp
from jax import lax
from jax.experimental import pallas as pl
from jax.experimental.pallas import tpu as pltpu

F32 = jnp.float32
BF16 = jnp.bfloat16

GRID_W = 64
N_ATT_HEADS = 8
ATT_HEAD_DIM = 64
ATT_WIDTH = N_ATT_HEADS * ATT_HEAD_DIM
WIN_ROWS = 8
WIN_COLS = 16
HEAD_GROUP = 4
N_ML_HEADS = 4
ML_HEAD_DIM = 128
ML_WIDTH = N_ML_HEADS * ML_HEAD_DIM
ROPE_THETA = 10000.0
EPS = 1e-6

LANES = 128
MXU_WIDTH = 256
ML_CHUNK = 256
GATE_ROWS = 8
HALO = 16
TOKEN_TILE = 512
VMEM_LIMIT = 56 * 1024 * 1024


def _dot(a, b):
    return jnp.dot(a, b, preferred_element_type=F32)


def _dot_nt(a, b):
    return lax.dot_general(a, b, (((1,), (1,)), ((), ())), preferred_element_type=F32)


def _dot_tn(a, b):
    return lax.dot_general(a, b, (((0,), (0,)), ((), ())), preferred_element_type=F32)


def _const_spec(shape):
    nd = len(shape)
    return pl.BlockSpec(shape, lambda *_: (0,) * nd, pipeline_mode=pl.Buffered(1))


def _params(n_axes):
    return pltpu.CompilerParams(dimension_semantics=("arbitrary",) * n_axes,
                                vmem_limit_bytes=VMEM_LIMIT)


def _token_tile(n_tok, seq_len):
    tm = min(TOKEN_TILE, n_tok)
    while n_tok % tm or (seq_len % tm and tm % seq_len):
        tm //= 2
    return tm


def _log_sigmoid(x):
    return jnp.minimum(x, 0.0) - jnp.log1p(jnp.exp(-jnp.abs(x)))


def _mod_kernel(c_ref, w_ref, b_ref, o_ref):
    c = c_ref[...]
    s = c * jax.nn.sigmoid(c)
    o_ref[...] = _dot(s.astype(BF16), w_ref[...].astype(BF16)) + b_ref[...]


def _modulation(cvecs, w_mod, b_mod):
    r, d = cvecs.shape
    n = w_mod.shape[1]
    tn = d
    out = pl.pallas_call(
        _mod_kernel,
        grid=(n // tn,),
        in_specs=[pl.BlockSpec((r, d), lambda j: (0, 0)),
                  pl.BlockSpec((d, tn), lambda j: (0, j)),
                  pl.BlockSpec((1, tn), lambda j: (0, j))],
        out_specs=pl.BlockSpec((r, tn), lambda j: (0, j)),
        out_shape=jax.ShapeDtypeStruct((r, n), F32),
        compiler_params=_params(1),
        name="adaln_mod",
    )(cvecs, w_mod, b_mod.reshape(1, n))
    return out.reshape(r, 1, n)


def _rope(x, cos, sin_signed):
    lane = lax.broadcasted_iota(jnp.int32, x.shape, 1)
    partner = jnp.where((lane & 32) == 0, pltpu.roll(x, LANES - 32, 1), pltpu.roll(x, 32, 1))
    return x * cos + partner * sin_signed


def _inproj_kernel(*refs, latent):
    if latent:
        (x_ref, mod_ref, g1_ref, watt_ref, wml_ref, wg_ref, bg_ref, qg_ref, kg_ref, bd_ref, cos_ref, sin_ref,
         aq_ref, ak_ref, av_ref, mq_ref, mk_ref, mv_ref, mo_ref, gt_ref, gtt_ref) = refs
    else:
        (x_ref, mod_ref, g1_ref, watt_ref, wml_ref, wg_ref, bg_ref, qg_ref, kg_ref, bd_ref,
         aq_ref, ak_ref, av_ref, mq_ref, mk_ref, mv_ref, mo_ref, gt_ref, gtt_ref, ck_ref, cv_ref) = refs
    x = x_ref[...]
    d = x.shape[-1]
    mod = mod_ref[0]
    sh1, sc1 = mod[:, 0:d], mod[:, d:2 * d]
    y = x * lax.rsqrt(jnp.mean(x * x, axis=-1, keepdims=True) + EPS)
    h = (y * g1_ref[...]) * (1.0 + sc1) + sh1
    hb = h.astype(BF16)

    def head_norm(a, g):
        ss = _dot((a * a).astype(BF16), bd_ref[...])
        return a * lax.rsqrt(ss * (1.0 / ATT_HEAD_DIM) + EPS) * g

    att = _dot(hb, watt_ref[...])
    w = ATT_WIDTH
    aq_ref[...] = (head_norm(att[:, 0:w], qg_ref[...]) * ATT_HEAD_DIM ** -0.5).astype(aq_ref.dtype)
    kn = head_norm(att[:, w:2 * w], kg_ref[...])
    av = att[:, 2 * w:3 * w]
    ak_ref[...] = kn.astype(BF16)
    if latent:
        av_ref[...] = av.T.astype(BF16)
    else:
        av_ref[...] = av.astype(BF16)
        for hh in range(N_ATT_HEADS):
            sl = slice(hh * ATT_HEAD_DIM, (hh + 1) * ATT_HEAD_DIM)
            ck_ref[:, hh, :] = kn[:, sl]
            cv_ref[:, hh, :] = av[:, sl]

    w = ML_WIDTH
    mq = _dot(hb, wml_ref[:, 0:w])
    mk = _dot(hb, wml_ref[:, w:2 * w]) * ML_HEAD_DIM ** -0.5
    if latent:
        cos, sin = cos_ref[...], sin_ref[...]
        for hh in range(N_ML_HEADS):
            sl = slice(hh * ML_HEAD_DIM, (hh + 1) * ML_HEAD_DIM)
            mq_ref[:, sl] = _rope(mq[:, sl], cos, sin).astype(BF16)
            mk_ref[:, sl] = _rope(mk[:, sl], cos, sin).astype(BF16)
    else:
        mq_ref[...] = mq.astype(BF16)
        mk_ref[...] = mk.astype(BF16)
    mv_ref[...] = _dot(hb, wml_ref[:, 2 * w:3 * w]).T.astype(BF16)
    mo_ref[...] = _dot(hb, wml_ref[:, 3 * w:4 * w]).T
    gates = _dot(hb, wg_ref[...]) + bg_ref[...]
    gt_ref[...] = gates
    gtt_ref[0:GATE_ROWS, :] = gates[:, 0:LANES].T[0:GATE_ROWS]
    gtt_ref[GATE_ROWS:2 * GATE_ROWS, :] = gates[:, LANES:2 * LANES].T[0:GATE_ROWS]


def _in_projection(x, mod, mod_row, seq_len, wts, rope, *, latent):
    n_tok, d = x.shape
    tm = _token_tile(n_tok, seq_len)
    tiles_per_seq = max(seq_len // tm, 1)

    def tok(width):
        return pl.BlockSpec((tm, width), lambda i: (i, 0))

    in_specs = [tok(d),
                pl.BlockSpec((1, 1, mod.shape[-1]), lambda i: (mod_row(i, tm), 0, 0)),
                _const_spec((1, d)),
                _const_spec(wts["w_att"].shape), _const_spec(wts["w_ml"].shape),
                _const_spec(wts["w_g"].shape), _const_spec((1, 2 * LANES)),
                _const_spec((1, ATT_WIDTH)), _const_spec((1, ATT_WIDTH)),
                _const_spec((ATT_WIDTH, ATT_WIDTH))]
    args = [x, mod, wts["norm1_g"], wts["w_att"], wts["w_ml"], wts["w_g"], wts["b_g"],
            wts["q_g"], wts["k_g"], wts["blockdiag"]]
    if latent:
        in_specs += [pl.BlockSpec((tm, LANES), lambda i: (i % tiles_per_seq, 0))] * 2
        args += [rope[0], rope[1]]
    out_shape = [jax.ShapeDtypeStruct((n_tok, ATT_WIDTH), BF16),
                 jax.ShapeDtypeStruct((n_tok, ATT_WIDTH), BF16),
                 jax.ShapeDtypeStruct((ATT_WIDTH, n_tok) if latent else (n_tok, ATT_WIDTH), BF16),
                 jax.ShapeDtypeStruct((n_tok, ML_WIDTH), BF16),
                 jax.ShapeDtypeStruct((n_tok, ML_WIDTH), BF16),
                 jax.ShapeDtypeStruct((ML_WIDTH, n_tok), BF16),
                 jax.ShapeDtypeStruct((ML_WIDTH, n_tok), F32),
                 jax.ShapeDtypeStruct((n_tok, 2 * LANES), F32),
                 jax.ShapeDtypeStruct((2 * GATE_ROWS, n_tok), F32)]

    def tok_t(height):
        return pl.BlockSpec((height, tm), lambda i: (0, i))

    av_spec = tok_t(ATT_WIDTH) if latent else tok(ATT_WIDTH)
    out_specs = ([tok(ATT_WIDTH)] * 2 + [av_spec] + [tok(ML_WIDTH)] * 2 + [tok_t(ML_WIDTH)] * 2
                 + [tok(2 * LANES), tok_t(2 * GATE_ROWS)])
    if not latent:
        cache = jax.ShapeDtypeStruct((n_tok, N_ATT_HEADS, ATT_HEAD_DIM), F32)
        out_shape += [cache, cache]
        out_specs += [pl.BlockSpec((tm, N_ATT_HEADS, ATT_HEAD_DIM), lambda i: (i, 0, 0))] * 2
    return pl.pallas_call(
        functools.partial(_inproj_kernel, latent=latent),
        grid=(n_tok // tm,),
        in_specs=in_specs,
        out_specs=out_specs,
        out_shape=out_shape,
        compiler_params=_params(1),
        name="in_proj_latent" if latent else "in_proj_ctx",
    )(*args)


def _ctx_attn_kernel(q_ref, k_ref, v_ref, o_ref):
    n = q_ref.shape[0]
    gw = HEAD_GROUP * ATT_HEAD_DIM
    lane_head = lax.broadcasted_iota(jnp.int32, (n, gw), 1) // ATT_HEAD_DIM
    for g in range(N_ATT_HEADS // HEAD_GROUP):
        sl = slice(g * gw, (g + 1) * gw)
        q4 = q_ref[:, sl]
        qbd = jnp.concatenate([jnp.where(lane_head == hl, q4, jnp.zeros_like(q4))
                               for hl in range(HEAD_GROUP)], axis=0)
        s = _dot_nt(k_ref[:, sl], qbd)
        p = jnp.exp(s - jnp.max(s, axis=0, keepdims=True))
        p = p / jnp.sum(p, axis=0, keepdims=True)
        o4 = _dot_tn(p.astype(BF16), v_ref[:, sl])
        out = jnp.where(lane_head == 0, o4[0:n], 0.0)
        for hl in range(1, HEAD_GROUP):
            out = out + jnp.where(lane_head == hl, o4[hl * n:(hl + 1) * n], 0.0)
        o_ref[:, sl] = out.astype(o_ref.dtype)


def _context_attention(q, k, v, seq_len):
    n_tok = q.shape[0]
    spec = pl.BlockSpec((seq_len, ATT_WIDTH), lambda b: (b, 0))
    return pl.pallas_call(
        _ctx_attn_kernel,
        grid=(n_tok // seq_len,),
        in_specs=[spec, spec, spec],
        out_specs=spec,
        out_shape=jax.ShapeDtypeStruct((n_tok, ATT_WIDTH), BF16),
        compiler_params=_params(1),
        name="ctx_attn",
    )(q, k, v)


def _nbr_attn_kernel(q_ref, k_ref, vt_ref, kc_ref, vct_ref, bias_ref, o_ref, pt_ref, *, rows, kr):
    r = pl.program_id(1)
    n_win = kr + 2
    rs = jnp.clip(r - kr // 2, 0, rows - kr)
    start = jnp.minimum(rs - (rs & 1), rows - n_win)
    delta = rs - start
    bias0 = pl.multiple_of((WIN_ROWS - 1 - (r - rs)) * GRID_W, GRID_W)
    n_loc = kr * GRID_W
    gw = HEAD_GROUP * ATT_HEAD_DIM
    lane_head = lax.broadcasted_iota(jnp.int32, (GRID_W, gw), 1) // ATT_HEAD_DIM
    zeros2 = jnp.zeros((2 * GRID_W, gw), BF16)
    for g in range(N_ATT_HEADS // HEAD_GROUP):
        sl = slice(g * gw, (g + 1) * gw)
        q4 = q_ref[:, sl]
        qbd = jnp.concatenate([jnp.where(lane_head == hl, q4, jnp.zeros_like(q4))
                               for hl in range(HEAD_GROUP)], axis=0)
        s_loc = _dot_nt(k_ref[pl.ds(pl.multiple_of(rs * GRID_W, GRID_W), n_loc), sl], qbd)
        s_loc = s_loc + bias_ref[g, pl.ds(bias0, n_loc), :]
        s_ctx = _dot_nt(kc_ref[0, :, sl], qbd)
        m = jnp.maximum(jnp.max(s_loc, axis=0, keepdims=True), jnp.max(s_ctx, axis=0, keepdims=True))
        p_loc = jnp.exp(s_loc - m)
        p_ctx = jnp.exp(s_ctx - m)
        l = jnp.sum(p_loc, axis=0, keepdims=True) + jnp.sum(p_ctx, axis=0, keepdims=True)
        pt_ref[g, 0:2 * GRID_W, :] = zeros2
        pt_ref[g, n_loc:n_loc + 2 * GRID_W, :] = zeros2
        pt_ref[g, pl.ds(pl.multiple_of(delta * GRID_W, GRID_W), n_loc), :] = p_loc.astype(BF16)
        vt_win = vt_ref[sl, pl.ds(pl.multiple_of(start * GRID_W, 2 * GRID_W), n_win * GRID_W)]
        ot = _dot(vt_win, pt_ref[g]) + _dot(vct_ref[0, sl, :], p_ctx.astype(BF16))
        o4 = (ot / l).T
        out = jnp.where(lane_head == 0, o4[0:GRID_W], 0.0)
        for hl in range(1, HEAD_GROUP):
            out = out + jnp.where(lane_head == hl, o4[hl * GRID_W:(hl + 1) * GRID_W], 0.0)
        o_ref[:, sl] = out.astype(o_ref.dtype)


def _nbr_bias_table(rpb, rows):
    col = jnp.arange(GRID_W)
    cs = jnp.clip(col - WIN_COLS // 2, 0, GRID_W - WIN_COLS)
    in_win = (col[None, :] >= cs[:, None]) & (col[None, :] < cs[:, None] + WIN_COLS)
    dc_idx = jnp.clip(col[None, :] - col[:, None] + (WIN_COLS - 1), 0, 2 * WIN_COLS - 2)
    n_dr, n_dc = rpb.shape[1], rpb.shape[2]
    onehot = (dc_idx[None] == jnp.arange(n_dc)[:, None, None]).astype(F32)
    t = jnp.einsum("hrc,cqk->hrqk", rpb.astype(F32), onehot, precision=lax.Precision.HIGHEST)
    t = jnp.where(in_win[None, None], t, -jnp.inf)
    n_groups = N_ATT_HEADS // HEAD_GROUP
    t = t.reshape(n_groups, HEAD_GROUP, n_dr, GRID_W, GRID_W).transpose(0, 2, 4, 1, 3)
    return t.reshape(n_groups, n_dr * GRID_W, HEAD_GROUP * GRID_W)


def _neighborhood_attention(q, k, vt, k_ctx, vt_ctx, rpb, n_batch, seq_len):
    rows = seq_len // GRID_W
    kr = min(WIN_ROWS, rows)
    n_win = kr + 2
    assert rows >= n_win and (rows - n_win) % 2 == 0
    bias = _nbr_bias_table(rpb, rows)
    past = k_ctx.shape[1]
    gw = HEAD_GROUP * ATT_HEAD_DIM
    row_spec = pl.BlockSpec((GRID_W, ATT_WIDTH), lambda b, r: (b * rows + r, 0))
    return pl.pallas_call(
        functools.partial(_nbr_attn_kernel, rows=rows, kr=kr),
        grid=(n_batch, rows),
        in_specs=[row_spec,
                  pl.BlockSpec((seq_len, ATT_WIDTH), lambda b, r: (b, 0)),
                  pl.BlockSpec((ATT_WIDTH, seq_len), lambda b, r: (0, b)),
                  pl.BlockSpec((1, past, ATT_WIDTH), lambda b, r: (b, 0, 0)),
                  pl.BlockSpec((1, ATT_WIDTH, past), lambda b, r: (b, 0, 0)),
                  _const_spec(bias.shape)],
        out_specs=row_spec,
        out_shape=jax.ShapeDtypeStruct((n_batch * seq_len, ATT_WIDTH), BF16),
        scratch_shapes=[pltpu.VMEM((N_ATT_HEADS // HEAD_GROUP, n_win * GRID_W, gw), BF16)],
        compiler_params=_params(2),
        name="nbr_attn",
    )(q, k, vt, k_ctx, vt_ctx, bias)


def _mlstm_kernel(*refs, chunk, n_chunks, has_state):
    dh = ML_HEAD_DIM
    nrep = 2 * GATE_ROWS
    if has_state:
        (q_ref, k_ref, vt_ref, mot_ref, gate_ref, gtt_ref, g_ref, c0_ref, n0_ref, m0_ref,
         o_ref, c_ref, n_ref, m_ref, hf_ref, hb_ref, cn_ref, mrun_ref) = refs
        for d in range(2):
            for hh in range(N_ML_HEADS):
                cn_ref[d, hh, 0:dh, :] = c0_ref[0, d, hh].T
                cn_ref[d, hh, dh:dh + nrep, :] = jnp.broadcast_to(n0_ref[0, d, hh], (nrep, dh))
        mrun_ref[...] = m0_ref[0]
    else:
        (q_ref, k_ref, vt_ref, mot_ref, gate_ref, gtt_ref, g_ref,
         o_ref, c_ref, n_ref, m_ref, hf_ref, hb_ref, cn_ref, mrun_ref) = refs
        cn_ref[...] = jnp.zeros_like(cn_ref)
        mrun_ref[...] = jnp.zeros_like(mrun_ref)
    use_state = has_state or n_chunks > 1

    i0 = lax.broadcasted_iota(jnp.int32, (chunk, chunk), 0)
    i1 = lax.broadcasted_iota(jnp.int32, (chunk, chunk), 1)
    row_id = lax.broadcasted_iota(jnp.int32, (GATE_ROWS, 1), 0)
    instances = [(hh, d) for d in range(2) for hh in range(N_ML_HEADS)]

    def split3(x):
        hi = x.astype(BF16)
        r1 = x - hi.astype(F32)
        mid = r1.astype(BF16)
        return hi, mid, (r1 - mid.astype(F32)).astype(BF16)

    def body(i, _):
        chunks = (i, n_chunks - 1 - i)
        rows = [pl.ds(pl.multiple_of(c * chunk, chunk), chunk) for c in chunks]
        le = [i0 <= i1, i0 >= i1]
        ge = [i0 >= i1, i0 <= i1]
        m_prev = mrun_ref[...][:, 0:1]

        def qkv(hh, d):
            sl = slice(hh * dh, (hh + 1) * dh)
            return q_ref[rows[d], sl], k_ref[rows[d], sl], vt_ref[sl, rows[d]]

        st = [_dot_nt(qkv(hh, d)[1], qkv(hh, d)[0]) for hh, d in instances]
        if use_state:
            cn = [cn_ref[d, hh] for hh, d in instances]
            qct = [_dot_nt(cn[n].astype(BF16), qkv(hh, d)[0]) for n, (hh, d) in enumerate(instances)]

        a_col, a8, cs8, w8, decay8, m_new = [], [], [], [], [], []
        for d in range(2):
            gi_r = gtt_ref[0:GATE_ROWS, rows[d]]
            lf_r = _log_sigmoid(gtt_ref[GATE_ROWS:2 * GATE_ROWS, rows[d]])
            tri_r = jnp.where(le[d], 1.0, 0.0).astype(BF16)
            cs_r = sum(_dot(t, tri_r) for t in split3(lf_r))
            b_last = jnp.sum(lf_r, axis=1, keepdims=True)
            w_end = b_last + gi_r - cs_r
            m_d = jnp.maximum(b_last + m_prev, jnp.max(w_end, axis=1, keepdims=True))
            a8.append(gi_r - cs_r)
            cs8.append(cs_r)
            decay8.append(jnp.exp(b_last + m_prev - m_d))
            w8.append(jnp.exp(w_end - m_d))
            m_new.append(m_d)
            lf_c = _log_sigmoid(gate_ref[rows[d], LANES:2 * LANES])
            tri_c = jnp.where(ge[d], 1.0, 0.0).astype(BF16)
            cs_c = sum(_dot(tri_c, t) for t in split3(lf_c))
            a_col.append(gate_ref[rows[d], 0:LANES] - cs_c)
        m_next = jnp.where((row_id & 1) == 0, m_new[0], m_new[1])
        mrun_ref[...] = jnp.broadcast_to(m_next, (GATE_ROWS, LANES))

        pt_all, g_all, den_all = [], [], []
        for n, (hh, d) in enumerate(instances):
            j = 2 * hh + d
            a = jnp.where(le[d], a_col[d][:, j:j + 1], -jnp.inf)
            g = jnp.maximum(jnp.max(a, axis=0, keepdims=True), m_prev[j:j + 1])
            pt = st[n] * jnp.exp(a - g)
            pt_all.append(pt)
            g_all.append(g)
            den_all.append(jnp.sum(pt, axis=0, keepdims=True))

        for n, (hh, d) in enumerate(instances):
            j = 2 * hh + d
            sl = slice(hh * dh, (hh + 1) * dh)
            _, k, vt = qkv(hh, d)
            g, den = g_all[n], den_all[n]
            num = _dot(vt, pt_all[n].astype(BF16))
            if use_state:
                w_inter = jnp.exp(m_prev[j:j + 1] - g)
                num = num + w_inter * qct[n][0:dh]
                den = den + w_inter * qct[n][dh:dh + 1]
            scale = 1.0 / jnp.maximum(jnp.abs(den), jnp.exp(-(cs8[d][j:j + 1] + g)))
            (hb_ref if d else hf_ref)[sl, rows[d]] = num * scale
            w_row = w8[d][j:j + 1]
            lhs = jnp.concatenate([(vt.astype(F32) * w_row).astype(BF16),
                                   jnp.broadcast_to(w_row, (nrep, chunk)).astype(BF16)], axis=0)
            upd = _dot(lhs, k)
            if use_state:
                upd = decay8[d][j:j + 1] * cn[n] + upd
            cn_ref[d, hh] = upd
        return 0

    def finish(c, _):
        rows = pl.ds(pl.multiple_of(c * chunk, chunk), chunk)
        for hh in range(N_ML_HEADS):
            sl = slice(hh * dh, (hh + 1) * dh)
            hs = hf_ref[sl, rows] + hb_ref[sl, rows]
            y = hs * lax.rsqrt(jnp.mean(hs * hs, axis=0, keepdims=True) + EPS) * g_ref[sl, :]
            y = y * jax.nn.sigmoid(mot_ref[sl, rows])
            o_ref[rows, sl] = y.T.astype(o_ref.dtype)
        return 0

    if n_chunks == 1:
        body(0, 0)
        finish(0, 0)
    else:
        lax.fori_loop(0, n_chunks, body, 0)
        lax.fori_loop(0, n_chunks, finish, 0)
    for d in range(2):
        for hh in range(N_ML_HEADS):
            cn = cn_ref[d, hh]
            c_ref[0, d, hh] = cn[0:dh].T
            n_ref[0, d, hh] = cn[dh:dh + 1]
    m_ref[0] = mrun_ref[...]


def _mlstm(mq, mk, mvt, mot, gates, gates_t, ml_g, state, n_batch, seq_len, chunk):
    dh = ML_HEAD_DIM
    nh = N_ML_HEADS
    width = nh * dh
    has_state = state is not None
    seq = pl.BlockSpec((seq_len, width), lambda b: (b, 0))
    seq_t = pl.BlockSpec((width, seq_len), lambda b: (0, b))
    st_c = pl.BlockSpec((1, 2, nh, dh, dh), lambda b: (b, 0, 0, 0, 0))
    st_n = pl.BlockSpec((1, 2, nh, 1, dh), lambda b: (b, 0, 0, 0, 0))
    st_m = pl.BlockSpec((1, GATE_ROWS, LANES), lambda b: (b, 0, 0))
    in_specs = [seq, seq, seq_t, seq_t,
                pl.BlockSpec((seq_len, 2 * LANES), lambda b: (b, 0)),
                pl.BlockSpec((2 * GATE_ROWS, seq_len), lambda b: (0, b)),
                _const_spec((width, chunk))]
    args = [mq, mk, mvt, mot, gates, gates_t, jnp.broadcast_to(ml_g.reshape(width, 1), (width, chunk))]
    if has_state:
        in_specs += [st_c, st_n, st_m]
        args += list(state)
    out_shape = [jax.ShapeDtypeStruct((n_batch * seq_len, width), BF16),
                 jax.ShapeDtypeStruct((n_batch, 2, nh, dh, dh), F32),
                 jax.ShapeDtypeStruct((n_batch, 2, nh, 1, dh), F32),
                 jax.ShapeDtypeStruct((n_batch, GATE_ROWS, LANES), F32)]
    return pl.pallas_call(
        functools.partial(_mlstm_kernel, chunk=chunk, n_chunks=seq_len // chunk, has_state=has_state),
        grid=(n_batch,),
        in_specs=in_specs,
        out_specs=[seq, st_c, st_n, st_m],
        out_shape=out_shape,
        scratch_shapes=[pltpu.VMEM((width, seq_len), F32), pltpu.VMEM((width, seq_len), F32),
                        pltpu.VMEM((2, nh, dh + 2 * GATE_ROWS, dh), F32), pltpu.VMEM((GATE_ROWS, LANES), F32)],
        compiler_params=_params(1),
        name="mlstm_latent" if has_state else "mlstm_ctx",
    )(*args)


def _pack_gate_rows(m):
    b = m.shape[0]
    packed = m.transpose(0, 2, 1).reshape(b, 2 * N_ML_HEADS, 1)
    return jnp.broadcast_to(packed, (b, GATE_ROWS, LANES))


def _unpack_gate_rows(m):
    b = m.shape[0]
    return m[:, :, 0].reshape(b, N_ML_HEADS, 2).transpose(0, 2, 1)


def _outproj_kernel(oa_ref, om_ref, x_ref, mod_ref, g2_ref, wa_ref, wm_ref, x1_ref, h2_ref):
    d = x_ref.shape[-1]
    mod = mod_ref[0]
    g1 = mod[:, 2 * d:3 * d]
    sh2, sc2 = mod[:, 3 * d:4 * d], mod[:, 4 * d:5 * d]
    out = _dot(oa_ref[...], wa_ref[...]) + _dot(om_ref[...], wm_ref[...])
    x1 = x_ref[...] + g1 * out
    x1_ref[...] = x1
    y = x1 * lax.rsqrt(jnp.mean(x1 * x1, axis=-1, keepdims=True) + EPS)
    h2_ref[...] = ((y * g2_ref[...]) * (1.0 + sc2) + sh2).astype(h2_ref.dtype)


def _out_projection(o_att, o_ml, x, mod, mod_row, seq_len, wts):
    n_tok, d = x.shape
    tm = _token_tile(n_tok, seq_len)

    def tok(width):
        return pl.BlockSpec((tm, width), lambda i: (i, 0))

    return pl.pallas_call(
        _outproj_kernel,
        grid=(n_tok // tm,),
        in_specs=[tok(ATT_WIDTH), tok(ML_WIDTH), tok(d),
                  pl.BlockSpec((1, 1, mod.shape[-1]), lambda i: (mod_row(i, tm), 0, 0)),
                  _const_spec((1, d)), _const_spec((ATT_WIDTH, d)), _const_spec((ML_WIDTH, d))],
        out_specs=[tok(d), tok(d)],
        out_shape=[jax.ShapeDtypeStruct((n_tok, d), F32), jax.ShapeDtypeStruct((n_tok, d), BF16)],
        compiler_params=_params(1),
        name="out_proj",
    )(o_att, o_ml, x, mod, wts["norm2_g"], wts["w_out_att"], wts["w_out_ml"])


def _ffn_kernel(h_ref, hprev_ref, hnext_ref, x1_ref, mod_ref, wup_ref, cw_ref, cb_ref, wd_ref,
                y_ref, lhs_ref, ug_ref, uv_ref, acc_ref, *, seq_len, n_col_tiles):
    tm, d = x1_ref.shape
    lhs_ref[0:tm, :] = h_ref[...]
    halo_row = lax.broadcasted_iota(jnp.int32, (HALO, d), 0)
    lhs_ref[tm:tm + HALO, :] = jnp.where(halo_row < HALO // 2, hnext_ref[...], hprev_ref[...])
    acc_ref[...] = jnp.zeros_like(acc_ref)
    sub = 8
    period = min(seq_len, tm)
    first_groups = sorted({r // sub for r in range(0, tm, period)})
    last_groups = sorted({(r + period - 1) // sub for r in range(0, tm, period)})
    tile0 = pl.program_id(0) * tm

    def zero_rows(x, groups, target):
        parts, at = [], 0
        for grp in groups:
            lo = grp * sub
            if lo > at:
                parts.append(x[at:lo])
            pos = (tile0 + lo + lax.broadcasted_iota(jnp.int32, (sub, 1), 0)) % seq_len
            parts.append(jnp.where(pos == target, 0.0, x[lo:lo + sub]))
            at = lo + sub
        if at < tm:
            parts.append(x[at:tm])
        return jnp.concatenate(parts, axis=0)

    def conv(u, cw, cb):
        prev = zero_rows(pltpu.roll(u, 1, 0)[0:tm], first_groups, 0)
        nxt = zero_rows(pltpu.roll(u, tm + HALO - 1, 0)[0:tm], last_groups, seq_len - 1)
        return prev * cw[0:1] + u[0:tm] * cw[1:2] + nxt * cw[2:3] + cb

    def up(j, slot):
        lhs = lhs_ref[...]
        ug_ref[slot] = _dot(lhs, wup_ref[j])
        uv_ref[slot] = _dot(lhs, wup_ref[n_col_tiles + j])

    def down(j, slot):
        gate = conv(ug_ref[slot], cw_ref[j], cb_ref[j])
        val = conv(uv_ref[slot], cw_ref[n_col_tiles + j], cb_ref[n_col_tiles + j])
        act = (gate * jax.nn.sigmoid(gate)) * val
        return _dot(act.astype(BF16), wd_ref[j])

    def pair(i, _):
        up(2 * i + 1, 1)
        acc_ref[...] += down(2 * i, 0)
        up(2 * i + 2, 0)
        acc_ref[...] += down(2 * i + 1, 1)
        return 0

    up(0, 0)
    n_pairs = (n_col_tiles - 1) // 2
    if n_pairs:
        lax.fori_loop(0, n_pairs, pair, 0)
    j0 = 2 * n_pairs
    if n_col_tiles - j0 == 2:
        up(j0 + 1, 1)
        acc_ref[...] += down(j0, 0)
        tail = down(j0 + 1, 1)
    else:
        tail = down(j0, 0)
    g2 = mod_ref[0][:, 5 * d:6 * d]
    y_ref[...] = x1_ref[...] + g2 * (acc_ref[...] + tail)


def _conv_ffn(h2, x1, mod, mod_row, seq_len, wts):
    n_tok, d = x1.shape
    tm = _token_tile(n_tok, seq_len)
    hpt = tm // HALO
    n_halo = n_tok // HALO
    nct, tn, _ = wts["w_down"].shape

    def tok(width):
        return pl.BlockSpec((tm, width), lambda i: (i, 0))

    in_specs = [tok(d),
                pl.BlockSpec((HALO, d), lambda i: (jnp.maximum(i * hpt - 1, 0), 0)),
                pl.BlockSpec((HALO, d), lambda i: (jnp.minimum((i + 1) * hpt, n_halo - 1), 0)),
                tok(d),
                pl.BlockSpec((1, 1, mod.shape[-1]), lambda i: (mod_row(i, tm), 0, 0)),
                _const_spec((2 * nct, d, tn)), _const_spec((2 * nct, 3, tn)), _const_spec((2 * nct, 1, tn)),
                _const_spec((nct, tn, d))]
    return pl.pallas_call(
        functools.partial(_ffn_kernel, seq_len=seq_len, n_col_tiles=nct),
        grid=(n_tok // tm,),
        in_specs=in_specs,
        out_specs=tok(d),
        out_shape=jax.ShapeDtypeStruct((n_tok, d), F32),
        scratch_shapes=[pltpu.VMEM((tm + HALO, d), BF16),
                        pltpu.VMEM((2, tm + HALO, tn), F32), pltpu.VMEM((2, tm + HALO, tn), F32),
                        pltpu.VMEM((tm, d), F32)],
        compiler_params=_params(1),
        name="conv_ffn",
    )(h2, h2, h2, x1, mod, wts["w_up"], wts["conv_w"], wts["conv_b"], wts["w_down"])


def _retile_cast_kernel(w_ref, o_ref):
    o_ref[0] = w_ref[...].astype(o_ref.dtype)


def _retile_cast(w, tn):
    r, c = w.shape
    return pl.pallas_call(
        _retile_cast_kernel,
        grid=(c // tn,),
        in_specs=[pl.BlockSpec((r, tn), lambda j: (0, j))],
        out_specs=pl.BlockSpec((1, r, tn), lambda j: (j, 0, 0)),
        out_shape=jax.ShapeDtypeStruct((c // tn, r, tn), BF16),
        compiler_params=_params(1),
        name="retile_cast",
    )(w)


def _layer_weights(norm1_g, norm2_g, w_in, b_gate, q_g, k_g, ml_g, w_out, w_up, conv_w, conv_b, w_down):
    d = w_in.shape[0]
    a, m, nh = ATT_WIDTH, ML_WIDTH, N_ML_HEADS
    def gate_lanes(g):
        g = g.reshape(g.shape[0], 2, 2, nh).transpose(0, 2, 3, 1).reshape(g.shape[0], 2, 2 * nh)
        return jnp.pad(g, ((0, 0), (0, 0), (0, LANES - 2 * nh))).reshape(g.shape[0], 2 * LANES)

    w_g = gate_lanes(w_in[:, 3 * a + 4 * m:])
    b_g = gate_lanes(b_gate.astype(F32).reshape(1, 4 * nh))
    head_id = jnp.arange(a) // ATT_HEAD_DIM
    d_ff = w_down.shape[0]
    tn = MXU_WIDTH
    nct = d_ff // tn

    def col_tiles(w):
        return w.reshape(w.shape[0], 2 * nct, tn).transpose(1, 0, 2)

    return {
        "norm1_g": norm1_g.reshape(1, d), "norm2_g": norm2_g.reshape(1, d),
        "w_att": w_in[:, :3 * a].astype(BF16), "w_ml": w_in[:, 3 * a:3 * a + 4 * m].astype(BF16),
        "w_g": w_g.astype(BF16), "b_g": b_g,
        "q_g": jnp.tile(q_g, N_ATT_HEADS).reshape(1, a), "k_g": jnp.tile(k_g, N_ATT_HEADS).reshape(1, a),
        "blockdiag": (head_id[:, None] == head_id[None, :]).astype(BF16),
        "ml_g": ml_g.reshape(1, m),
        "w_out_att": w_out[:a].astype(BF16), "w_out_ml": w_out[a:].astype(BF16),
        "w_up": _retile_cast(w_up, tn),
        "conv_w": col_tiles(conv_w.astype(F32)), "conv_b": col_tiles(conv_b.astype(F32).reshape(1, -1)),
        "w_down": w_down.astype(BF16).reshape(nct, tn, w_down.shape[1]),
    }


def _rope_tables(seq_len):
    quarter = ML_HEAD_DIM // 4
    pos = jnp.arange(seq_len)
    inv_freq = ROPE_THETA ** (-jnp.arange(quarter, dtype=F32) / quarter)
    ang_r = (pos // GRID_W).astype(F32)[:, None] * inv_freq[None, :]
    ang_c = (pos % GRID_W).astype(F32)[:, None] * inv_freq[None, :]
    cos = jnp.concatenate([jnp.cos(ang_r)] * 2 + [jnp.cos(ang_c)] * 2, axis=-1)
    sin = jnp.concatenate([-jnp.sin(ang_r), jnp.sin(ang_r), -jnp.sin(ang_c), jnp.sin(ang_c)], axis=-1)
    return cos, sin


def _layer(x, mod, mod_row, n_batch, seq_len, wts, *, latent, ctx_kv=None, rpb=None, state=None, rope=None):
    chunk = min(ML_CHUNK, seq_len)
    outs = _in_projection(x, mod, mod_row, seq_len, wts, rope, latent=latent)
    aq, ak, av, mq, mk, mvt, mot, gt, gtt = outs[:9]
    if latent:
        o_att = _neighborhood_attention(aq, ak, av, ctx_kv[0], ctx_kv[1], rpb, n_batch, seq_len)
        ak = av = None
    else:
        o_att = _context_attention(aq, ak, av, seq_len)
        ak, av = outs[9], outs[10]
    o_ml, c_f, n_f, m_f = _mlstm(mq, mk, mvt, mot, gt, gtt, wts["ml_g"], state, n_batch, seq_len, chunk)
    x1, h2 = _out_projection(o_att, o_ml, x, mod, mod_row, seq_len, wts)
    y = _conv_ffn(h2, x1, mod, mod_row, seq_len, wts)
    return y, ak, av, (c_f, n_f[:, :, :, 0, :], _unpack_gate_rows(m_f))


def kernel(x_prompt, x_sample, cache_k, cache_v, state_C, state_n, state_m, c, c_ctx, w_mod, b_mod, norm1_g,
           norm2_g, w_in, b_gate, q_norm_g, k_norm_g, rpb, ml_norm_g, w_out, w_up, conv_w, conv_b, w_down):
    batch, seq, d = x_prompt.shape
    dec_batch, dec_seq, _ = x_sample.shape
    depth = w_mod.shape[0]
    past = cache_k.shape[2]
    cvecs = jnp.concatenate([c_ctx[None, :], c], axis=0).astype(F32)
    rope = _rope_tables(dec_seq)

    def ctx_row(i, tm):
        return 0 * i

    def lat_row(i, tm):
        return 1 + (i * tm) // dec_seq

    xp = x_prompt.reshape(batch * seq, d)
    xs = x_sample.reshape(dec_batch * dec_seq, d)
    ks, vs, cs, ns, ms = [], [], [], [], []
    for l in range(depth):
        wts = _layer_weights(norm1_g[l], norm2_g[l], w_in[l], b_gate[l], q_norm_g[l], k_norm_g[l],
                             ml_norm_g[l], w_out[l], w_up[l], conv_w[l], conv_b[l], w_down[l])
        mod = _modulation(cvecs, w_mod[l], b_mod[l])

        xp, ak, av, st = _layer(xp, mod, ctx_row, batch, seq, wts, latent=False)
        ks.append(ak.reshape(batch, seq, N_ATT_HEADS, ATT_HEAD_DIM))
        vs.append(av.reshape(batch, seq, N_ATT_HEADS, ATT_HEAD_DIM))
        cs.append(st[0])
        ns.append(st[1])
        ms.append(st[2])

        state = (state_C[:, l].astype(F32),
                 state_n[:, l].astype(F32)[:, :, :, None, :],
                 _pack_gate_rows(state_m[:, l].astype(F32)))
        ctx_kv = (cache_k[:, l].reshape(dec_batch, past, ATT_WIDTH).astype(BF16),
                  cache_v[:, l].reshape(dec_batch, past, ATT_WIDTH).transpose(0, 2, 1).astype(BF16))
        xs, _, _, _ = _layer(xs, mod, lat_row, dec_batch, dec_seq, wts, latent=True, ctx_kv=ctx_kv,
                             rpb=rpb[l], state=state, rope=rope)
    return (xp.reshape(batch, seq, d), xs.reshape(dec_batch, dec_seq, d),
            jnp.stack(ks, axis=1), jnp.stack(vs, axis=1),
            jnp.stack(cs, axis=1), jnp.stack(ns, axis=1), jnp.stack(ms, axis=1))
```

```python
import functools

import jax
import jax.numpy as jnp
import numpy as np
from jax import lax
from jax.experimental import pallas as pl
from jax.experimental.pallas import tpu as pltpu

F32 = jnp.float32
BF16 = jnp.bfloat16

GRID_W = 64
N_ATT_HEADS = 8
ATT_HEAD_DIM = 64
ATT_WIDTH = N_ATT_HEADS * ATT_HEAD_DIM
WIN_ROWS = 8
WIN_COLS = 16
HEAD_GROUP = 4
N_ML_HEADS = 4
ML_HEAD_DIM = 128
ML_WIDTH = N_ML_HEADS * ML_HEAD_DIM
ROPE_THETA = 10000.0
EPS = 1e-6

LANES = 128
MXU_WIDTH = 256
ML_CHUNK = 256
GATE_ROWS = 8
HALO = 16
TOKEN_TILE = 512
VMEM_LIMIT = 56 * 1024 * 1024


def _dot(a, b):
    return jnp.dot(a, b, preferred_element_type=F32)


def _dot_nt(a, b):
    return lax.dot_general(a, b, (((1,), (1,)), ((), ())), preferred_element_type=F32)


def _dot_tn(a, b):
    return lax.dot_general(a, b, (((0,), (0,)), ((), ())), preferred_element_type=F32)


def _const_spec(shape):
    nd = len(shape)
    return pl.BlockSpec(shape, lambda *_: (0,) * nd, pipeline_mode=pl.Buffered(1))


def _params(n_axes):
    return pltpu.CompilerParams(dimension_semantics=("arbitrary",) * n_axes,
                                vmem_limit_bytes=VMEM_LIMIT)


def _token_tile(n_tok, seq_len):
    tm = min(TOKEN_TILE, n_tok)
    while n_tok % tm or (seq_len % tm and tm % seq_len):
        tm //= 2
    return tm


def _log_sigmoid(x):
    return jnp.minimum(x, 0.0) - jnp.log1p(jnp.exp(-jnp.abs(x)))


def _mod_kernel(c_ref, w_ref, b_ref, o_ref):
    c = c_ref[...]
    s = c * jax.nn.sigmoid(c)
    o_ref[...] = _dot(s.astype(BF16), w_ref[...].astype(BF16)) + b_ref[...]


def _modulation(cvecs, w_mod, b_mod):
    r, d = cvecs.shape
    n = w_mod.shape[1]
    tn = d
    out = pl.pallas_call(
        _mod_kernel,
        grid=(n // tn,),
        in_specs=[pl.BlockSpec((r, d), lambda j: (0, 0)),
                  pl.BlockSpec((d, tn), lambda j: (0, j)),
                  pl.BlockSpec((1, tn), lambda j: (0, j))],
        out_specs=pl.BlockSpec((r, tn), lambda j: (0, j)),
        out_shape=jax.ShapeDtypeStruct((r, n), F32),
        compiler_params=_params(1),
        name="adaln_mod",
    )(cvecs, w_mod, b_mod.reshape(1, n))
    return out.reshape(r, 1, n)


def _rope(x, cos, sin_signed):
    lane = lax.broadcasted_iota(jnp.int32, x.shape, 1)
    partner = jnp.where((lane & 32) == 0, pltpu.roll(x, LANES - 32, 1), pltpu.roll(x, 32, 1))
    return x * cos + partner * sin_signed


def _inproj_kernel(*refs, latent):
    if latent:
        (x_ref, mod_ref, g1_ref, win_ref, wg_ref, bg_ref, qg_ref, kg_ref, bd_ref, cos_ref, sin_ref,
         aq_ref, ak_ref, av_ref, mq_ref, mk_ref, mv_ref, mo_ref, gt_ref, gtt_ref) = refs
    else:
        (x_ref, mod_ref, g1_ref, win_ref, wg_ref, bg_ref, qg_ref, kg_ref, bd_ref,
         aq_ref, ak_ref, av_ref, mq_ref, mk_ref, mv_ref, mo_ref, gt_ref, gtt_ref, ck_ref, cv_ref,
         cks_ref, cvs_ref) = refs
    x = x_ref[...]
    d = x.shape[-1]
    mod = mod_ref[0]
    sh1, sc1 = mod[:, 0:d], mod[:, d:2 * d]
    y = x * lax.rsqrt(jnp.mean(x * x, axis=-1, keepdims=True) + EPS)
    h = (y * g1_ref[...]) * (1.0 + sc1) + sh1
    hb = h.astype(BF16)

    def head_norm(a, g):
        ss = _dot((a * a).astype(BF16), bd_ref[...])
        return a * lax.rsqrt(ss * (1.0 / ATT_HEAD_DIM) + EPS) * g

    w = ATT_WIDTH
    att = _dot(hb, win_ref[:, 0:3 * w])
    aq_ref[...] = (head_norm(att[:, 0:w], qg_ref[...]) * ATT_HEAD_DIM ** -0.5).astype(aq_ref.dtype)
    kn = head_norm(att[:, w:2 * w], kg_ref[...])
    av = att[:, 2 * w:3 * w]
    ak_ref[...] = kn.astype(BF16)
    if latent:
        av_ref[...] = av.T.astype(BF16)
    else:
        av_ref[...] = av.astype(BF16)
        tm = kn.shape[0]
        for src, dst, scr in ((kn, ck_ref, cks_ref), (av, cv_ref, cvs_ref)):
            for hh in range(N_ATT_HEADS):
                pair = src[:, (hh // 2) * LANES:(hh // 2 + 1) * LANES]
                if hh % 2:
                    pair = pltpu.roll(pair, ATT_HEAD_DIM, 1)
                scr[pl.ds(hh, tm, stride=N_ATT_HEADS), :] = pair
            dst[...] = scr[...].reshape(tm, N_ATT_HEADS, LANES)[:, :, 0:ATT_HEAD_DIM]

    w = ML_WIDTH
    ml0 = 3 * ATT_WIDTH
    mq = _dot(hb, win_ref[:, ml0:ml0 + w])
    mk = _dot(hb, win_ref[:, ml0 + w:ml0 + 2 * w]) * ML_HEAD_DIM ** -0.5
    if latent:
        cos, sin = cos_ref[...], sin_ref[...]
        for hh in range(N_ML_HEADS):
            sl = slice(hh * ML_HEAD_DIM, (hh + 1) * ML_HEAD_DIM)
            mq_ref[:, sl] = _rope(mq[:, sl], cos, sin).astype(BF16)
            mk_ref[:, sl] = _rope(mk[:, sl], cos, sin).astype(BF16)
    else:
        mq_ref[...] = mq.astype(BF16)
        mk_ref[...] = mk.astype(BF16)
    mv_ref[...] = _dot(hb, win_ref[:, ml0 + 2 * w:ml0 + 3 * w]).T.astype(BF16)
    mo_ref[...] = _dot(hb, win_ref[:, ml0 + 3 * w:ml0 + 4 * w]).T
    gates = _dot(hb, wg_ref[...]) + bg_ref[...]
    gt_ref[...] = gates
    gtt_ref[0:GATE_ROWS, :] = gates[:, 0:LANES].T[0:GATE_ROWS]
    gtt_ref[GATE_ROWS:2 * GATE_ROWS, :] = gates[:, LANES:2 * LANES].T[0:GATE_ROWS]


def _in_projection(x, mod, mod_row, seq_len, wts, rope, *, latent):
    n_tok, d = x.shape
    tm = _token_tile(n_tok, seq_len)
    tiles_per_seq = max(seq_len // tm, 1)

    def tok(width):
        return pl.BlockSpec((tm, width), lambda i: (i, 0))

    in_specs = [tok(d),
                pl.BlockSpec((1, 1, mod.shape[-1]), lambda i: (mod_row(i, tm), 0, 0)),
                _const_spec((1, d)),
                _const_spec(wts["w_in"].shape),
                _const_spec(wts["w_g"].shape), _const_spec((1, 2 * LANES)),
                _const_spec((1, ATT_WIDTH)), _const_spec((1, ATT_WIDTH)),
                _const_spec((ATT_WIDTH, ATT_WIDTH))]
    args = [x, mod, wts["norm1_g"], wts["w_in"], wts["w_g"], wts["b_g"],
            wts["q_g"], wts["k_g"], wts["blockdiag"]]
    if latent:
        in_specs += [pl.BlockSpec((tm, LANES), lambda i: (i % tiles_per_seq, 0))] * 2
        args += [rope[0], rope[1]]
    out_shape = [jax.ShapeDtypeStruct((n_tok, ATT_WIDTH), BF16),
                 jax.ShapeDtypeStruct((n_tok, ATT_WIDTH), BF16),
                 jax.ShapeDtypeStruct((ATT_WIDTH, n_tok) if latent else (n_tok, ATT_WIDTH), BF16),
                 jax.ShapeDtypeStruct((n_tok, ML_WIDTH), BF16),
                 jax.ShapeDtypeStruct((n_tok, ML_WIDTH), BF16),
                 jax.ShapeDtypeStruct((ML_WIDTH, n_tok), BF16),
                 jax.ShapeDtypeStruct((ML_WIDTH, n_tok), F32),
                 jax.ShapeDtypeStruct((n_tok, 2 * LANES), F32),
                 jax.ShapeDtypeStruct((2 * GATE_ROWS, n_tok), F32)]

    def tok_t(height):
        return pl.BlockSpec((height, tm), lambda i: (0, i))

    av_spec = tok_t(ATT_WIDTH) if latent else tok(ATT_WIDTH)
    out_specs = ([tok(ATT_WIDTH)] * 2 + [av_spec] + [tok(ML_WIDTH)] * 2 + [tok_t(ML_WIDTH)] * 2
                 + [tok(2 * LANES), tok_t(2 * GATE_ROWS)])
    if not latent:
        cache = jax.ShapeDtypeStruct((n_tok, N_ATT_HEADS, ATT_HEAD_DIM), F32)
        out_shape += [cache, cache]
        out_specs += [pl.BlockSpec((tm, N_ATT_HEADS, ATT_HEAD_DIM), lambda i: (i, 0, 0))] * 2
    scratch = [] if latent else [pltpu.VMEM((tm * N_ATT_HEADS, LANES), F32)] * 2
    return pl.pallas_call(
        functools.partial(_inproj_kernel, latent=latent),
        grid=(n_tok // tm,),
        in_specs=in_specs,
        out_specs=out_specs,
        out_shape=out_shape,
        scratch_shapes=scratch,
        compiler_params=_params(1),
        name="in_proj_latent" if latent else "in_proj_ctx",
    )(*args)


def _ctx_attn_kernel(q_ref, k_ref, v_ref, o_ref):
    n = q_ref.shape[0]
    gw = HEAD_GROUP * ATT_HEAD_DIM
    lane_head = lax.broadcasted_iota(jnp.int32, (n, gw), 1) // ATT_HEAD_DIM
    for g in range(N_ATT_HEADS // HEAD_GROUP):
        sl = slice(g * gw, (g + 1) * gw)
        q4 = q_ref[:, sl]
        qbd = jnp.concatenate([jnp.where(lane_head == hl, q4, jnp.zeros_like(q4))
                               for hl in range(HEAD_GROUP)], axis=0)
        s = _dot_nt(k_ref[:, sl], qbd)
        p = jnp.exp(s - jnp.max(s, axis=0, keepdims=True))
        p = p / jnp.sum(p, axis=0, keepdims=True)
        o4 = _dot_tn(p.astype(BF16), v_ref[:, sl])
        out = jnp.where(lane_head == 0, o4[0:n], 0.0)
        for hl in range(1, HEAD_GROUP):
            out = out + jnp.where(lane_head == hl, o4[hl * n:(hl + 1) * n], 0.0)
        o_ref[:, sl] = out.astype(o_ref.dtype)


def _context_attention(q, k, v, seq_len):
    n_tok = q.shape[0]
    spec = pl.BlockSpec((seq_len, ATT_WIDTH), lambda b: (b, 0))
    return pl.pallas_call(
        _ctx_attn_kernel,
        grid=(n_tok // seq_len,),
        in_specs=[spec, spec, spec],
        out_specs=spec,
        out_shape=jax.ShapeDtypeStruct((n_tok, ATT_WIDTH), BF16),
        compiler_params=_params(1),
        name="ctx_attn",
    )(q, k, v)


def _nbr_attn_kernel(q_ref, k_ref, vt_ref, kc_ref, vct_ref, bias_ref, o_ref, pt_ref, pc_ref, il_ref, *, rows, kr):
    r = pl.program_id(1)
    n_win = kr + 2
    n_loc = kr * GRID_W
    gw = HEAD_GROUP * ATT_HEAD_DIM
    groups = N_ATT_HEADS // HEAD_GROUP
    lane_head = lax.broadcasted_iota(jnp.int32, (GRID_W, gw), 1) // ATT_HEAD_DIM

    @pl.when((pl.program_id(0) == 0) & (r == 0))
    def _():
        pt_ref[...] = jnp.zeros_like(pt_ref)
        pc_ref[...] = jnp.zeros_like(pc_ref)
        il_ref[...] = jnp.zeros_like(il_ref)

    def window(row):
        rs = jnp.clip(row - kr // 2, 0, rows - kr)
        return rs, jnp.minimum(rs - (rs & 1), rows - n_win)

    _, start_b = window(jnp.maximum(r - 1, 0))
    for g in range(groups):
        sl = slice(g * gw, (g + 1) * gw)
        vt_win = vt_ref[sl, pl.ds(pl.multiple_of(start_b * GRID_W, 2 * GRID_W), n_win * GRID_W)]
        ot = _dot(vt_win, pt_ref[g]) + _dot(vct_ref[0, sl, :], pc_ref[g])
        o4 = (ot * il_ref[g]).T
        out = jnp.where(lane_head == 0, o4[0:GRID_W], 0.0)
        for hl in range(1, HEAD_GROUP):
            out = out + jnp.where(lane_head == hl, o4[hl * GRID_W:(hl + 1) * GRID_W], 0.0)
        o_ref[:, sl] = out.astype(o_ref.dtype)

    ra = jnp.minimum(r, rows - 1)
    rs, start = window(ra)
    delta = rs - start
    bias0 = pl.multiple_of((WIN_ROWS - 1 - (ra - rs)) * GRID_W, GRID_W)
    zeros2 = jnp.zeros((2 * GRID_W, gw), BF16)
    for g in range(groups):
        sl = slice(g * gw, (g + 1) * gw)
        q4 = q_ref[:, sl]
        qbd = jnp.concatenate([jnp.where(lane_head == hl, q4, jnp.zeros_like(q4))
                               for hl in range(HEAD_GROUP)], axis=0)
        s_loc = _dot_nt(k_ref[pl.ds(pl.multiple_of(rs * GRID_W, GRID_W), n_loc), sl], qbd)
        s_loc = s_loc + bias_ref[g, pl.ds(bias0, n_loc), :]
        s_ctx = _dot_nt(kc_ref[0, :, sl], qbd)
        m = jnp.maximum(jnp.max(s_loc, axis=0, keepdims=True), jnp.max(s_ctx, axis=0, keepdims=True))
        p_loc = jnp.exp(s_loc - m)
        p_ctx = jnp.exp(s_ctx - m)
        l = jnp.sum(p_loc, axis=0, keepdims=True) + jnp.sum(p_ctx, axis=0, keepdims=True)
        pt_ref[g, 0:2 * GRID_W, :] = zeros2
        pt_ref[g, n_loc:n_loc + 2 * GRID_W, :] = zeros2
        pt_ref[g, pl.ds(pl.multiple_of(delta * GRID_W, GRID_W), n_loc), :] = p_loc.astype(BF16)
        pc_ref[g] = p_ctx.astype(BF16)
        il_ref[g] = 1.0 / l


def _nbr_bias_table(rpb, rows):
    col = jnp.arange(GRID_W)
    cs = jnp.clip(col - WIN_COLS // 2, 0, GRID_W - WIN_COLS)
    in_win = (col[None, :] >= cs[:, None]) & (col[None, :] < cs[:, None] + WIN_COLS)
    dc_idx = jnp.clip(col[None, :] - col[:, None] + (WIN_COLS - 1), 0, 2 * WIN_COLS - 2)
    n_dr, n_dc = rpb.shape[1], rpb.shape[2]
    onehot = (dc_idx[None] == jnp.arange(n_dc)[:, None, None]).astype(F32)
    t = jnp.einsum("hrc,cqk->hrqk", rpb.astype(F32), onehot, precision=lax.Precision.HIGHEST)
    t = jnp.where(in_win[None, None], t, -jnp.inf)
    n_groups = N_ATT_HEADS // HEAD_GROUP
    t = t.reshape(n_groups, HEAD_GROUP, n_dr, GRID_W, GRID_W).transpose(0, 2, 4, 1, 3)
    return t.reshape(n_groups, n_dr * GRID_W, HEAD_GROUP * GRID_W)


def _neighborhood_attention(q, k, vt, k_ctx, vt_ctx, rpb, n_batch, seq_len):
    rows = seq_len // GRID_W
    kr = min(WIN_ROWS, rows)
    n_win = kr + 2
    assert rows >= n_win and (rows - n_win) % 2 == 0
    bias = _nbr_bias_table(rpb, rows)
    past = k_ctx.shape[1]
    gw = HEAD_GROUP * ATT_HEAD_DIM
    n_groups = N_ATT_HEADS // HEAD_GROUP
    q_spec = pl.BlockSpec((GRID_W, ATT_WIDTH), lambda b, r: (b * rows + jnp.minimum(r, rows - 1), 0))
    o_spec = pl.BlockSpec((GRID_W, ATT_WIDTH), lambda b, r: (b * rows + jnp.maximum(r - 1, 0), 0))
    return pl.pallas_call(
        functools.partial(_nbr_attn_kernel, rows=rows, kr=kr),
        grid=(n_batch, rows + 1),
        in_specs=[q_spec,
                  pl.BlockSpec((seq_len, ATT_WIDTH), lambda b, r: (b, 0)),
                  pl.BlockSpec((ATT_WIDTH, seq_len), lambda b, r: (0, b)),
                  pl.BlockSpec((1, past, ATT_WIDTH), lambda b, r: (b, 0, 0)),
                  pl.BlockSpec((1, ATT_WIDTH, past), lambda b, r: (b, 0, 0)),
                  _const_spec(bias.shape)],
        out_specs=o_spec,
        out_shape=jax.ShapeDtypeStruct((n_batch * seq_len, ATT_WIDTH), BF16),
        scratch_shapes=[pltpu.VMEM((n_groups, n_win * GRID_W, gw), BF16),
                        pltpu.VMEM((n_groups, past, gw), BF16),
                        pltpu.VMEM((n_groups, 1, gw), F32)],
        compiler_params=_params(2),
        name="nbr_attn",
    )(q, k, vt, k_ctx, vt_ctx, bias)


def _mlstm_kernel(*refs, chunk, n_chunks, has_state):
    dh = ML_HEAD_DIM
    nrep = 2 * GATE_ROWS
    if has_state:
        (q_ref, k_ref, vt_ref, mot_ref, gate_ref, gtt_ref, g_ref, c0_ref, n0_ref, m0_ref,
         o_ref, c_ref, n_ref, m_ref, hf_ref, hb_ref, cn_ref, mrun_ref) = refs
        for d in range(2):
            for hh in range(N_ML_HEADS):
                cn_ref[d, hh, 0:dh, :] = c0_ref[0, d, hh].T
                cn_ref[d, hh, dh:dh + nrep, :] = jnp.broadcast_to(n0_ref[0, d, hh], (nrep, dh))
        mrun_ref[...] = m0_ref[0]
    else:
        (q_ref, k_ref, vt_ref, mot_ref, gate_ref, gtt_ref, g_ref,
         o_ref, c_ref, n_ref, m_ref, hf_ref, hb_ref, cn_ref, mrun_ref) = refs
        cn_ref[...] = jnp.zeros_like(cn_ref)
        mrun_ref[...] = jnp.zeros_like(mrun_ref)
    use_state = has_state or n_chunks > 1

    i0 = lax.broadcasted_iota(jnp.int32, (chunk, chunk), 0)
    i1 = lax.broadcasted_iota(jnp.int32, (chunk, chunk), 1)
    row_id = lax.broadcasted_iota(jnp.int32, (GATE_ROWS, 1), 0)
    instances = [(hh, d) for d in range(2) for hh in range(N_ML_HEADS)]

    def split3(x):
        hi = x.astype(BF16)
        r1 = x - hi.astype(F32)
        mid = r1.astype(BF16)
        return hi, mid, (r1 - mid.astype(F32)).astype(BF16)

    def body(i, _):
        chunks = (i, n_chunks - 1 - i)
        rows = [pl.ds(pl.multiple_of(c * chunk, chunk), chunk) for c in chunks]
        le = [i0 <= i1, i0 >= i1]
        ge = [i0 >= i1, i0 <= i1]
        m_prev = mrun_ref[...][:, 0:1]

        def qkv(hh, d):
            sl = slice(hh * dh, (hh + 1) * dh)
            return q_ref[rows[d], sl], k_ref[rows[d], sl], vt_ref[sl, rows[d]]

        st = [_dot_nt(qkv(hh, d)[1], qkv(hh, d)[0]) for hh, d in instances]
        if use_state:
            cn = [cn_ref[d, hh] for hh, d in instances]
            qct = [_dot_nt(cn[n].astype(BF16), qkv(hh, d)[0]) for n, (hh, d) in enumerate(instances)]

        a_col, a8, cs8, w8, decay8, m_new = [], [], [], [], [], []
        for d in range(2):
            gi_r = gtt_ref[0:GATE_ROWS, rows[d]]
            lf_r = _log_sigmoid(gtt_ref[GATE_ROWS:2 * GATE_ROWS, rows[d]])
            tri_r = jnp.where(le[d], 1.0, 0.0).astype(BF16)
            cs_r = sum(_dot(t, tri_r) for t in split3(lf_r))
            b_last = jnp.sum(lf_r, axis=1, keepdims=True)
            w_end = b_last + gi_r - cs_r
            m_d = jnp.maximum(b_last + m_prev, jnp.max(w_end, axis=1, keepdims=True))
            a8.append(gi_r - cs_r)
            cs8.append(cs_r)
            decay8.append(jnp.exp(b_last + m_prev - m_d))
            w8.append(jnp.exp(w_end - m_d))
            m_new.append(m_d)
            lf_c = _log_sigmoid(gate_ref[rows[d], LANES:2 * LANES])
            tri_c = jnp.where(ge[d], 1.0, 0.0).astype(BF16)
            cs_c = sum(_dot(tri_c, t) for t in split3(lf_c))
            a_col.append(gate_ref[rows[d], 0:LANES] - cs_c)
        m_next = jnp.where((row_id & 1) == 0, m_new[0], m_new[1])
        mrun_ref[...] = jnp.broadcast_to(m_next, (GATE_ROWS, LANES))

        pt_all, g_all, den_all = [], [], []
        for n, (hh, d) in enumerate(instances):
            j = 2 * hh + d
            a = jnp.where(le[d], a_col[d][:, j:j + 1], -jnp.inf)
            g = jnp.maximum(jnp.max(a, axis=0, keepdims=True), m_prev[j:j + 1])
            pt = st[n] * jnp.exp(a - g)
            pt_all.append(pt)
            g_all.append(g)
            den_all.append(jnp.sum(pt, axis=0, keepdims=True))

        for n, (hh, d) in enumerate(instances):
            j = 2 * hh + d
            sl = slice(hh * dh, (hh + 1) * dh)
            _, k, vt = qkv(hh, d)
            g, den = g_all[n], den_all[n]
            num = _dot(vt, pt_all[n].astype(BF16))
            if use_state:
                w_inter = jnp.exp(m_prev[j:j + 1] - g)
                num = num + w_inter * qct[n][0:dh]
                den = den + w_inter * qct[n][dh:dh + 1]
            scale = 1.0 / jnp.maximum(jnp.abs(den), jnp.exp(-(cs8[d][j:j + 1] + g)))
            (hb_ref if d else hf_ref)[sl, rows[d]] = num * scale
            w_row = w8[d][j:j + 1]
            lhs = jnp.concatenate([(vt.astype(F32) * w_row).astype(BF16),
                                   jnp.broadcast_to(w_row, (nrep, chunk)).astype(BF16)], axis=0)
            upd = _dot(lhs, k)
            if use_state:
                upd = decay8[d][j:j + 1] * cn[n] + upd
            cn_ref[d, hh] = upd
        return 0

    def finish(c, _):
        rows = pl.ds(pl.multiple_of(c * chunk, chunk), chunk)
        for hh in range(N_ML_HEADS):
            sl = slice(hh * dh, (hh + 1) * dh)
            hs = hf_ref[sl, rows] + hb_ref[sl, rows]
            y = hs * lax.rsqrt(jnp.mean(hs * hs, axis=0, keepdims=True) + EPS) * g_ref[sl, :]
            y = y * jax.nn.sigmoid(mot_ref[sl, rows])
            o_ref[rows, sl] = y.T.astype(o_ref.dtype)
        return 0

    if n_chunks == 1:
        body(0, 0)
        finish(0, 0)
    else:
        lax.fori_loop(0, n_chunks, body, 0)
        lax.fori_loop(0, n_chunks, finish, 0)
    for d in range(2):
        for hh in range(N_ML_HEADS):
            cn = cn_ref[d, hh]
            c_ref[0, d, hh] = cn[0:dh].T
            n_ref[0, d, hh] = cn[dh:dh + 1]
    m_ref[0] = mrun_ref[...]


def _mlstm(mq, mk, mvt, mot, gates, gates_t, ml_g, state, n_batch, seq_len, chunk):
    dh = ML_HEAD_DIM
    nh = N_ML_HEADS
    width = nh * dh
    has_state = state is not None
    seq = pl.BlockSpec((seq_len, width), lambda b: (b, 0))
    seq_t = pl.BlockSpec((width, seq_len), lambda b: (0, b))
    st_c = pl.BlockSpec((1, 2, nh, dh, dh), lambda b: (b, 0, 0, 0, 0))
    st_n = pl.BlockSpec((1, 2, nh, 1, dh), lambda b: (b, 0, 0, 0, 0))
    st_m = pl.BlockSpec((1, GATE_ROWS, LANES), lambda b: (b, 0, 0))
    in_specs = [seq, seq, seq_t, seq_t,
                pl.BlockSpec((seq_len, 2 * LANES), lambda b: (b, 0)),
                pl.BlockSpec((2 * GATE_ROWS, seq_len), lambda b: (0, b)),
                _const_spec((width, chunk))]
    args = [mq, mk, mvt, mot, gates, gates_t, jnp.broadcast_to(ml_g.reshape(width, 1), (width, chunk))]
    if has_state:
        in_specs += [st_c, st_n, st_m]
        args += list(state)
    out_shape = [jax.ShapeDtypeStruct((n_batch * seq_len, width), BF16),
                 jax.ShapeDtypeStruct((n_batch, 2, nh, dh, dh), F32),
                 jax.ShapeDtypeStruct((n_batch, 2, nh, 1, dh), F32),
                 jax.ShapeDtypeStruct((n_batch, GATE_ROWS, LANES), F32)]
    return pl.pallas_call(
        functools.partial(_mlstm_kernel, chunk=chunk, n_chunks=seq_len // chunk, has_state=has_state),
        grid=(n_batch,),
        in_specs=in_specs,
        out_specs=[seq, st_c, st_n, st_m],
        out_shape=out_shape,
        scratch_shapes=[pltpu.VMEM((width, seq_len), F32), pltpu.VMEM((width, seq_len), F32),
                        pltpu.VMEM((2, nh, dh + 2 * GATE_ROWS, dh), F32), pltpu.VMEM((GATE_ROWS, LANES), F32)],
        compiler_params=_params(1),
        name="mlstm_latent" if has_state else "mlstm_ctx",
    )(*args)


def _pack_gate_rows(m):
    b = m.shape[0]
    packed = m.transpose(0, 2, 1).reshape(b, 2 * N_ML_HEADS, 1)
    return jnp.broadcast_to(packed, (b, GATE_ROWS, LANES))


def _unpack_gate_rows(m):
    b = m.shape[0]
    return m[:, :, 0].reshape(b, N_ML_HEADS, 2).transpose(0, 2, 1)


def _outproj_kernel(oa_ref, om_ref, x_ref, mod_ref, g2_ref, wa_ref, wm_ref, x1_ref, h2_ref):
    d = x_ref.shape[-1]
    mod = mod_ref[0]
    g1 = mod[:, 2 * d:3 * d]
    sh2, sc2 = mod[:, 3 * d:4 * d], mod[:, 4 * d:5 * d]
    out = _dot(oa_ref[...], wa_ref[...]) + _dot(om_ref[...], wm_ref[...])
    x1 = x_ref[...] + g1 * out
    x1_ref[...] = x1
    y = x1 * lax.rsqrt(jnp.mean(x1 * x1, axis=-1, keepdims=True) + EPS)
    h2_ref[...] = ((y * g2_ref[...]) * (1.0 + sc2) + sh2).astype(h2_ref.dtype)


def _out_projection(o_att, o_ml, x, mod, mod_row, seq_len, wts):
    n_tok, d = x.shape
    tm = _token_tile(n_tok, seq_len)

    def tok(width):
        return pl.BlockSpec((tm, width), lambda i: (i, 0))

    return pl.pallas_call(
        _outproj_kernel,
        grid=(n_tok // tm,),
        in_specs=[tok(ATT_WIDTH), tok(ML_WIDTH), tok(d),
                  pl.BlockSpec((1, 1, mod.shape[-1]), lambda i: (mod_row(i, tm), 0, 0)),
                  _const_spec((1, d)), _const_spec((ATT_WIDTH, d)), _const_spec((ML_WIDTH, d))],
        out_specs=[tok(d), tok(d)],
        out_shape=[jax.ShapeDtypeStruct((n_tok, d), F32), jax.ShapeDtypeStruct((n_tok, d), BF16)],
        compiler_params=_params(1),
        name="out_proj",
    )(o_att, o_ml, x, mod, wts["norm2_g"], wts["w_out_att"], wts["w_out_ml"])


def _ffn_kernel(h_ref, hprev_ref, hnext_ref, x1_ref, mod_ref, wup_ref, cw_ref, cb_ref, wd_ref,
                y_ref, lhs_ref, ug_ref, uv_ref, acc_ref, *, seq_len, n_col_tiles):
    tm, d = x1_ref.shape
    lhs_ref[0:tm, :] = h_ref[...]
    halo_row = lax.broadcasted_iota(jnp.int32, (HALO, d), 0)
    lhs_ref[tm:tm + HALO, :] = jnp.where(halo_row < HALO // 2, hnext_ref[...], hprev_ref[...])
    acc_ref[...] = jnp.zeros_like(acc_ref)
    sub = 8
    period = min(seq_len, tm)
    first_groups = sorted({r // sub for r in range(0, tm, period)})
    last_groups = sorted({(r + period - 1) // sub for r in range(0, tm, period)})
    tile0 = pl.program_id(0) * tm

    def zero_rows(x, groups, target):
        parts, at = [], 0
        for grp in groups:
            lo = grp * sub
            if lo > at:
                parts.append(x[at:lo])
            pos = (tile0 + lo + lax.broadcasted_iota(jnp.int32, (sub, 1), 0)) % seq_len
            parts.append(jnp.where(pos == target, 0.0, x[lo:lo + sub]))
            at = lo + sub
        if at < tm:
            parts.append(x[at:tm])
        return jnp.concatenate(parts, axis=0)

    def conv(u, cw, cb):
        prev = zero_rows(pltpu.roll(u, 1, 0)[0:tm], first_groups, 0)
        nxt = zero_rows(pltpu.roll(u, tm + HALO - 1, 0)[0:tm], last_groups, seq_len - 1)
        return prev * cw[0:1] + u[0:tm] * cw[1:2] + nxt * cw[2:3] + cb

    def up(j, slot):
        lhs = lhs_ref[...]
        ug_ref[slot] = _dot(lhs, wup_ref[j])
        uv_ref[slot] = _dot(lhs, wup_ref[n_col_tiles + j])

    def down(j, slot):
        gate = conv(ug_ref[slot], cw_ref[j], cb_ref[j])
        val = conv(uv_ref[slot], cw_ref[n_col_tiles + j], cb_ref[n_col_tiles + j])
        act = (gate * jax.nn.sigmoid(gate)) * val
        return _dot(act.astype(BF16), wd_ref[j])

    def pair(i, _):
        up(2 * i + 1, 1)
        acc_ref[...] += down(2 * i, 0)
        up(2 * i + 2, 0)
        acc_ref[...] += down(2 * i + 1, 1)
        return 0

    up(0, 0)
    n_pairs = (n_col_tiles - 1) // 2
    if n_pairs:
        lax.fori_loop(0, n_pairs, pair, 0)
    j0 = 2 * n_pairs
    if n_col_tiles - j0 == 2:
        up(j0 + 1, 1)
        acc_ref[...] += down(j0, 0)
        tail = down(j0 + 1, 1)
    else:
        tail = down(j0, 0)
    g2 = mod_ref[0][:, 5 * d:6 * d]
    y_ref[...] = x1_ref[...] + g2 * (acc_ref[...] + tail)


def _conv_ffn(h2, x1, mod, mod_row, seq_len, wts):
    n_tok, d = x1.shape
    tm = _token_tile(n_tok, seq_len)
    hpt = tm // HALO
    n_halo = n_tok // HALO
    nct, tn, _ = wts["w_down"].shape

    def tok(width):
        return pl.BlockSpec((tm, width), lambda i: (i, 0))

    in_specs = [tok(d),
                pl.BlockSpec((HALO, d), lambda i: (jnp.maximum(i * hpt - 1, 0), 0)),
                pl.BlockSpec((HALO, d), lambda i: (jnp.minimum((i + 1) * hpt, n_halo - 1), 0)),
                tok(d),
                pl.BlockSpec((1, 1, mod.shape[-1]), lambda i: (mod_row(i, tm), 0, 0)),
                _const_spec((2 * nct, d, tn)), _const_spec((2 * nct, 3, tn)), _const_spec((2 * nct, 1, tn)),
                _const_spec((nct, tn, d))]
    return pl.pallas_call(
        functools.partial(_ffn_kernel, seq_len=seq_len, n_col_tiles=nct),
        grid=(n_tok // tm,),
        in_specs=in_specs,
        out_specs=tok(d),
        out_shape=jax.ShapeDtypeStruct((n_tok, d), F32),
        scratch_shapes=[pltpu.VMEM((tm + HALO, d), BF16),
                        pltpu.VMEM((2, tm + HALO, tn), F32), pltpu.VMEM((2, tm + HALO, tn), F32),
                        pltpu.VMEM((tm, d), F32)],
        compiler_params=_params(1),
        name="conv_ffn",
    )(h2, h2, h2, x1, mod, wts["w_up"], wts["conv_w"], wts["conv_b"], wts["w_down"])


def _retile_cast_kernel(w_ref, o_ref):
    o_ref[0] = w_ref[...].astype(o_ref.dtype)


def _retile_cast(w, tn):
    r, c = w.shape
    return pl.pallas_call(
        _retile_cast_kernel,
        grid=(c // tn,),
        in_specs=[pl.BlockSpec((r, tn), lambda j: (0, j))],
        out_specs=pl.BlockSpec((1, r, tn), lambda j: (j, 0, 0)),
        out_shape=jax.ShapeDtypeStruct((c // tn, r, tn), BF16),
        compiler_params=_params(1),
        name="retile_cast",
    )(w)


def _layer_weights(norm1_g, norm2_g, w_in, b_gate, q_g, k_g, ml_g, w_out, w_up, conv_w, conv_b, w_down):
    d = w_in.shape[0]
    a, m, nh = ATT_WIDTH, ML_WIDTH, N_ML_HEADS
    def gate_lanes(g):
        g = g.reshape(g.shape[0], 2, 2, nh).transpose(0, 2, 3, 1).reshape(g.shape[0], 2, 2 * nh)
        return jnp.pad(g, ((0, 0), (0, 0), (0, LANES - 2 * nh))).reshape(g.shape[0], 2 * LANES)

    w_g = gate_lanes(w_in[:, 3 * a + 4 * m:])
    b_g = gate_lanes(b_gate.astype(F32).reshape(1, 4 * nh))
    head_id = jnp.arange(a) // ATT_HEAD_DIM
    d_ff = w_down.shape[0]
    tn = MXU_WIDTH
    nct = d_ff // tn

    def col_tiles(w):
        return w.reshape(w.shape[0], 2 * nct, tn).transpose(1, 0, 2)

    return {
        "norm1_g": norm1_g.reshape(1, d), "norm2_g": norm2_g.reshape(1, d),
        "w_in": w_in.astype(BF16),
        "w_g": w_g.astype(BF16), "b_g": b_g,
        "q_g": jnp.tile(q_g, N_ATT_HEADS).reshape(1, a), "k_g": jnp.tile(k_g, N_ATT_HEADS).reshape(1, a),
        "blockdiag": (head_id[:, None] == head_id[None, :]).astype(BF16),
        "ml_g": ml_g.reshape(1, m),
        "w_out_att": w_out[:a].astype(BF16), "w_out_ml": w_out[a:].astype(BF16),
        "w_up": _retile_cast(w_up, tn),
        "conv_w": col_tiles(conv_w.astype(F32)), "conv_b": col_tiles(conv_b.astype(F32).reshape(1, -1)),
        "w_down": w_down.astype(BF16).reshape(nct, tn, w_down.shape[1]),
    }


def _rope_tables(seq_len):
    quarter = ML_HEAD_DIM // 4
    pos = np.arange(seq_len)
    inv_freq = ROPE_THETA ** (-np.arange(quarter, dtype=np.float64) / quarter)
    ang_r = (pos // GRID_W).astype(np.float64)[:, None] * inv_freq[None, :]
    ang_c = (pos % GRID_W).astype(np.float64)[:, None] * inv_freq[None, :]
    cos = np.concatenate([np.cos(ang_r)] * 2 + [np.cos(ang_c)] * 2, axis=-1)
    sin = np.concatenate([-np.sin(ang_r), np.sin(ang_r), -np.sin(ang_c), np.sin(ang_c)], axis=-1)
    return jnp.asarray(cos, F32), jnp.asarray(sin, F32)


def _layer(x, mod, mod_row, n_batch, seq_len, wts, *, latent, ctx_kv=None, rpb=None, state=None, rope=None):
    chunk = min(ML_CHUNK, seq_len)
    outs = _in_projection(x, mod, mod_row, seq_len, wts, rope, latent=latent)
    aq, ak, av, mq, mk, mvt, mot, gt, gtt = outs[:9]
    if latent:
        o_att = _neighborhood_attention(aq, ak, av, ctx_kv[0], ctx_kv[1], rpb, n_batch, seq_len)
        ak = av = None
    else:
        o_att = _context_attention(aq, ak, av, seq_len)
        ak, av = outs[9], outs[10]
    o_ml, c_f, n_f, m_f = _mlstm(mq, mk, mvt, mot, gt, gtt, wts["ml_g"], state, n_batch, seq_len, chunk)
    x1, h2 = _out_projection(o_att, o_ml, x, mod, mod_row, seq_len, wts)
    y = _conv_ffn(h2, x1, mod, mod_row, seq_len, wts)
    return y, ak, av, (c_f, n_f[:, :, :, 0, :], _unpack_gate_rows(m_f))


def kernel(x_prompt, x_sample, cache_k, cache_v, state_C, state_n, state_m, c, c_ctx, w_mod, b_mod, norm1_g,
           norm2_g, w_in, b_gate, q_norm_g, k_norm_g, rpb, ml_norm_g, w_out, w_up, conv_w, conv_b, w_down):
    batch, seq, d = x_prompt.shape
    dec_batch, dec_seq, _ = x_sample.shape
    depth = w_mod.shape[0]
    past = cache_k.shape[2]
    cvecs = jnp.concatenate([c_ctx[None, :], c], axis=0).astype(F32)
    rope = _rope_tables(dec_seq)

    def ctx_row(i, tm):
        return 0 * i

    def lat_row(i, tm):
        return 1 + (i * tm) // dec_seq

    xp = x_prompt.reshape(batch * seq, d)
    xs = x_sample.reshape(dec_batch * dec_seq, d)
    ks, vs, cs, ns, ms = [], [], [], [], []
    for l in range(depth):
        wts = _layer_weights(norm1_g[l], norm2_g[l], w_in[l], b_gate[l], q_norm_g[l], k_norm_g[l],
                             ml_norm_g[l], w_out[l], w_up[l], conv_w[l], conv_b[l], w_down[l])
        mod = _modulation(cvecs, w_mod[l], b_mod[l])

        xp, ak, av, st = _layer(xp, mod, ctx_row, batch, seq, wts, latent=False)
        ks.append(ak.reshape(batch, seq, N_ATT_HEADS, ATT_HEAD_DIM))
        vs.append(av.reshape(batch, seq, N_ATT_HEADS, ATT_HEAD_DIM))
        cs.append(st[0])
        ns.append(st[1])
        ms.append(st[2])

        state = (state_C[:, l].astype(F32),
                 state_n[:, l].astype(F32)[:, :, :, None, :],
                 _pack_gate_rows(state_m[:, l].astype(F32)))
        ctx_kv = (cache_k[:, l].reshape(dec_batch, past, ATT_WIDTH).astype(BF16),
                  cache_v[:, l].reshape(dec_batch, past, ATT_WIDTH).transpose(0, 2, 1).astype(BF16))
        xs, _, _, _ = _layer(xs, mod, lat_row, dec_batch, dec_seq, wts, latent=True, ctx_kv=ctx_kv,
                             rpb=rpb[l], state=state, rope=rope)
    return (xp.reshape(batch, seq, d), xs.reshape(dec_batch, dec_seq, d),
            jnp.stack(ks, axis=1), jnp.stack(vs, axis=1),
            jnp.stack(cs, axis=1), jnp.stack(ns, axis=1), jnp.stack(ms, axis=1))
```

```python
import functools

import jax
import jax.numpy as jnp
import numpy as np
from jax import lax
from jax.experimental import pallas as pl
from jax.experimental.pallas import tpu as pltpu

F32 = jnp.float32
BF16 = jnp.bfloat16

GRID_W = 64
N_ATT_HEADS = 8
ATT_HEAD_DIM = 64
ATT_WIDTH = N_ATT_HEADS * ATT_HEAD_DIM
WIN_ROWS = 8
WIN_COLS = 16
HEAD_GROUP = 4
N_ML_HEADS = 4
ML_HEAD_DIM = 128
ML_WIDTH = N_ML_HEADS * ML_HEAD_DIM
ROPE_THETA = 10000.0
EPS = 1e-6

LANES = 128
MXU_WIDTH = 256
ML_CHUNK = 256
GATE_ROWS = 8
HALO = 16
TOKEN_TILE = 512
VMEM_LIMIT = 56 * 1024 * 1024


def _dot(a, b):
    return jnp.dot(a, b, preferred_element_type=F32)


def _dot_nt(a, b):
    return lax.dot_general(a, b, (((1,), (1,)), ((), ())), preferred_element_type=F32)


def _dot_tn(a, b):
    return lax.dot_general(a, b, (((0,), (0,)), ((), ())), preferred_element_type=F32)


def _const_spec(shape):
    nd = len(shape)
    return pl.BlockSpec(shape, lambda *_: (0,) * nd, pipeline_mode=pl.Buffered(1))


def _params(n_axes):
    return pltpu.CompilerParams(dimension_semantics=("arbitrary",) * n_axes,
                                vmem_limit_bytes=VMEM_LIMIT)


def _token_tile(n_tok, seq_len):
    tm = min(TOKEN_TILE, n_tok)
    while n_tok % tm or (seq_len % tm and tm % seq_len):
        tm //= 2
    return tm


def _log_sigmoid(x):
    return jnp.minimum(x, 0.0) - jnp.log1p(jnp.exp(-jnp.abs(x)))


def _mod_kernel(c_ref, w_ref, b_ref, o_ref):
    c = c_ref[...]
    s = c * jax.nn.sigmoid(c)
    o_ref[...] = _dot(s.astype(BF16), w_ref[...].astype(BF16)) + b_ref[...]


def _modulation(cvecs, w_mod, b_mod):
    r, d = cvecs.shape
    n = w_mod.shape[1]
    tn = d
    out = pl.pallas_call(
        _mod_kernel,
        grid=(n // tn,),
        in_specs=[pl.BlockSpec((r, d), lambda j: (0, 0)),
                  pl.BlockSpec((d, tn), lambda j: (0, j)),
                  pl.BlockSpec((1, tn), lambda j: (0, j))],
        out_specs=pl.BlockSpec((r, tn), lambda j: (0, j)),
        out_shape=jax.ShapeDtypeStruct((r, n), F32),
        compiler_params=_params(1),
        name="adaln_mod",
    )(cvecs, w_mod, b_mod.reshape(1, n))
    return out.reshape(r, 1, n)


def _rope(x, cos, sin_signed):
    lane = lax.broadcasted_iota(jnp.int32, x.shape, 1)
    partner = jnp.where((lane & 32) == 0, pltpu.roll(x, LANES - 32, 1), pltpu.roll(x, 32, 1))
    return x * cos + partner * sin_signed


def _inproj_kernel(*refs, latent):
    if latent:
        (x_ref, mod_ref, g1_ref, win_ref, wg_ref, bg_ref, qg_ref, kg_ref, bd_ref, cos_ref, sin_ref,
         aq_ref, ak_ref, av_ref, mq_ref, mk_ref, mv_ref, mo_ref, gt_ref, gtt_ref) = refs
    else:
        (x_ref, mod_ref, g1_ref, win_ref, wg_ref, bg_ref, qg_ref, kg_ref, bd_ref,
         aq_ref, ak_ref, av_ref, mq_ref, mk_ref, mv_ref, mo_ref, gt_ref, gtt_ref, ck_ref, cv_ref,
         cks_ref, cvs_ref) = refs
    x = x_ref[...]
    d = x.shape[-1]
    mod = mod_ref[0]
    sh1, sc1 = mod[:, 0:d], mod[:, d:2 * d]
    y = x * lax.rsqrt(jnp.mean(x * x, axis=-1, keepdims=True) + EPS)
    h = (y * g1_ref[...]) * (1.0 + sc1) + sh1
    hb = h.astype(BF16)

    def head_norm(a, g):
        ss = _dot((a * a).astype(BF16), bd_ref[...])
        return a * lax.rsqrt(ss * (1.0 / ATT_HEAD_DIM) + EPS) * g

    w = ATT_WIDTH
    att = _dot(hb, win_ref[:, 0:3 * w])
    aq_ref[...] = (head_norm(att[:, 0:w], qg_ref[...]) * ATT_HEAD_DIM ** -0.5).astype(aq_ref.dtype)
    kn = head_norm(att[:, w:2 * w], kg_ref[...])
    av = att[:, 2 * w:3 * w]
    ak_ref[...] = kn.astype(BF16)
    if latent:
        av_ref[...] = av.T.astype(BF16)
    else:
        av_ref[...] = av.astype(BF16)
        tm = kn.shape[0]
        for src, dst, scr in ((kn, ck_ref, cks_ref), (av, cv_ref, cvs_ref)):
            for hh in range(N_ATT_HEADS):
                pair = src[:, (hh // 2) * LANES:(hh // 2 + 1) * LANES]
                if hh % 2:
                    pair = pltpu.roll(pair, ATT_HEAD_DIM, 1)
                scr[pl.ds(hh, tm, stride=N_ATT_HEADS), :] = pair
            dst[...] = scr[...].reshape(tm, N_ATT_HEADS, LANES)[:, :, 0:ATT_HEAD_DIM]

    w = ML_WIDTH
    ml0 = 3 * ATT_WIDTH
    mq = _dot(hb, win_ref[:, ml0:ml0 + w])
    mk = _dot(hb, win_ref[:, ml0 + w:ml0 + 2 * w]) * ML_HEAD_DIM ** -0.5
    if latent:
        cos, sin = cos_ref[...], sin_ref[...]
        for hh in range(N_ML_HEADS):
            sl = slice(hh * ML_HEAD_DIM, (hh + 1) * ML_HEAD_DIM)
            mq_ref[:, sl] = _rope(mq[:, sl], cos, sin).astype(BF16)
            mk_ref[:, sl] = _rope(mk[:, sl], cos, sin).astype(BF16)
    else:
        mq_ref[...] = mq.astype(BF16)
        mk_ref[...] = mk.astype(BF16)
    mv_ref[...] = _dot(hb, win_ref[:, ml0 + 2 * w:ml0 + 3 * w]).T.astype(BF16)
    mo_ref[...] = _dot(hb, win_ref[:, ml0 + 3 * w:ml0 + 4 * w]).T
    gates = _dot(hb, wg_ref[...]) + bg_ref[...]
    gt_ref[...] = gates
    gtt_ref[0:GATE_ROWS, :] = gates[:, 0:LANES].T[0:GATE_ROWS]
    gtt_ref[GATE_ROWS:2 * GATE_ROWS, :] = gates[:, LANES:2 * LANES].T[0:GATE_ROWS]


def _in_projection(x, mod, mod_row, seq_len, wts, rope, *, latent):
    n_tok, d = x.shape
    tm = _token_tile(n_tok, seq_len)
    tiles_per_seq = max(seq_len // tm, 1)

    def tok(width):
        return pl.BlockSpec((tm, width), lambda i: (i, 0))

    in_specs = [tok(d),
                pl.BlockSpec((1, 1, mod.shape[-1]), lambda i: (mod_row(i, tm), 0, 0)),
                _const_spec((1, d)),
                _const_spec(wts["w_in"].shape),
                _const_spec(wts["w_g"].shape), _const_spec((1, 2 * LANES)),
                _const_spec((1, ATT_WIDTH)), _const_spec((1, ATT_WIDTH)),
                _const_spec((ATT_WIDTH, ATT_WIDTH))]
    args = [x, mod, wts["norm1_g"], wts["w_in"], wts["w_g"], wts["b_g"],
            wts["q_g"], wts["k_g"], wts["blockdiag"]]
    if latent:
        in_specs += [pl.BlockSpec((tm, LANES), lambda i: (i % tiles_per_seq, 0))] * 2
        args += [rope[0], rope[1]]
    out_shape = [jax.ShapeDtypeStruct((n_tok, ATT_WIDTH), BF16),
                 jax.ShapeDtypeStruct((n_tok, ATT_WIDTH), BF16),
                 jax.ShapeDtypeStruct((ATT_WIDTH, n_tok) if latent else (n_tok, ATT_WIDTH), BF16),
                 jax.ShapeDtypeStruct((n_tok, ML_WIDTH), BF16),
                 jax.ShapeDtypeStruct((n_tok, ML_WIDTH), BF16),
                 jax.ShapeDtypeStruct((ML_WIDTH, n_tok), BF16),
                 jax.ShapeDtypeStruct((ML_WIDTH, n_tok), F32),
                 jax.ShapeDtypeStruct((n_tok, 2 * LANES), F32),
                 jax.ShapeDtypeStruct((2 * GATE_ROWS, n_tok), F32)]

    def tok_t(height):
        return pl.BlockSpec((height, tm), lambda i: (0, i))

    av_spec = tok_t(ATT_WIDTH) if latent else tok(ATT_WIDTH)
    out_specs = ([tok(ATT_WIDTH)] * 2 + [av_spec] + [tok(ML_WIDTH)] * 2 + [tok_t(ML_WIDTH)] * 2
                 + [tok(2 * LANES), tok_t(2 * GATE_ROWS)])
    if not latent:
        cache = jax.ShapeDtypeStruct((n_tok, N_ATT_HEADS, ATT_HEAD_DIM), F32)
        out_shape += [cache, cache]
        out_specs += [pl.BlockSpec((tm, N_ATT_HEADS, ATT_HEAD_DIM), lambda i: (i, 0, 0))] * 2
    scratch = [] if latent else [pltpu.VMEM((tm * N_ATT_HEADS, LANES), F32)] * 2
    return pl.pallas_call(
        functools.partial(_inproj_kernel, latent=latent),
        grid=(n_tok // tm,),
        in_specs=in_specs,
        out_specs=out_specs,
        out_shape=out_shape,
        scratch_shapes=scratch,
        compiler_params=_params(1),
        name="in_proj_latent" if latent else "in_proj_ctx",
    )(*args)


def _ctx_attn_kernel(q_ref, k_ref, v_ref, o_ref):
    n = q_ref.shape[0]
    gw = HEAD_GROUP * ATT_HEAD_DIM
    lane_head = lax.broadcasted_iota(jnp.int32, (n, gw), 1) // ATT_HEAD_DIM
    for g in range(N_ATT_HEADS // HEAD_GROUP):
        sl = slice(g * gw, (g + 1) * gw)
        q4 = q_ref[:, sl]
        qbd = jnp.concatenate([jnp.where(lane_head == hl, q4, jnp.zeros_like(q4))
                               for hl in range(HEAD_GROUP)], axis=0)
        s = _dot_nt(k_ref[:, sl], qbd)
        p = jnp.exp(s - jnp.max(s, axis=0, keepdims=True))
        p = p / jnp.sum(p, axis=0, keepdims=True)
        o4 = _dot_tn(p.astype(BF16), v_ref[:, sl])
        out = jnp.where(lane_head == 0, o4[0:n], 0.0)
        for hl in range(1, HEAD_GROUP):
            out = out + jnp.where(lane_head == hl, o4[hl * n:(hl + 1) * n], 0.0)
        o_ref[:, sl] = out.astype(o_ref.dtype)


def _context_attention(q, k, v, seq_len):
    n_tok = q.shape[0]
    spec = pl.BlockSpec((seq_len, ATT_WIDTH), lambda b: (b, 0))
    return pl.pallas_call(
        _ctx_attn_kernel,
        grid=(n_tok // seq_len,),
        in_specs=[spec, spec, spec],
        out_specs=spec,
        out_shape=jax.ShapeDtypeStruct((n_tok, ATT_WIDTH), BF16),
        compiler_params=_params(1),
        name="ctx_attn",
    )(q, k, v)


def _nbr_attn_kernel(q_ref, k_ref, vt_ref, kc_ref, vct_ref, bias_ref, o_ref, pt_ref, *, rows, kr):
    r = pl.program_id(1)
    n_win = kr + 2
    n_loc = kr * GRID_W
    gw = HEAD_GROUP * ATT_HEAD_DIM
    groups = N_ATT_HEADS // HEAD_GROUP
    lane_head = lax.broadcasted_iota(jnp.int32, (GRID_W, gw), 1) // ATT_HEAD_DIM
    rs = jnp.clip(r - kr // 2, 0, rows - kr)
    start = jnp.minimum(rs - (rs & 1), rows - n_win)
    delta = rs - start
    bias0 = pl.multiple_of((WIN_ROWS - 1 - (r - rs)) * GRID_W, GRID_W)
    zeros2 = jnp.zeros((2 * GRID_W, gw), BF16)
    for g in range(groups):
        sl = slice(g * gw, (g + 1) * gw)
        q4 = q_ref[:, sl]
        qbd = jnp.concatenate([jnp.where(lane_head == hl, q4, jnp.zeros_like(q4))
                               for hl in range(HEAD_GROUP)], axis=0)
        s_loc = _dot_nt(k_ref[pl.ds(pl.multiple_of(rs * GRID_W, GRID_W), n_loc), sl], qbd)
        s_loc = s_loc + bias_ref[g, pl.ds(bias0, n_loc), :]
        s_ctx = _dot_nt(kc_ref[0, :, sl], qbd)
        m = jnp.maximum(jnp.max(s_loc, axis=0, keepdims=True), jnp.max(s_ctx, axis=0, keepdims=True))
        p_loc = jnp.exp(s_loc - m)
        p_ctx = jnp.exp(s_ctx - m)
        l = jnp.sum(p_loc, axis=0, keepdims=True) + jnp.sum(p_ctx, axis=0, keepdims=True)
        pt_ref[g, 0:2 * GRID_W, :] = zeros2
        pt_ref[g, n_loc:n_loc + 2 * GRID_W, :] = zeros2
        pt_ref[g, pl.ds(pl.multiple_of(delta * GRID_W, GRID_W), n_loc), :] = p_loc.astype(BF16)
        vt_win = vt_ref[sl, pl.ds(pl.multiple_of(start * GRID_W, 2 * GRID_W), n_win * GRID_W)]
        ot = _dot(vt_win, pt_ref[g]) + _dot(vct_ref[0, sl, :], p_ctx.astype(BF16))
        o4 = (ot / l).T
        out = jnp.where(lane_head == 0, o4[0:GRID_W], 0.0)
        for hl in range(1, HEAD_GROUP):
            out = out + jnp.where(lane_head == hl, o4[hl * GRID_W:(hl + 1) * GRID_W], 0.0)
        o_ref[:, sl] = out.astype(o_ref.dtype)


def _nbr_bias_table(rpb, rows):
    col = jnp.arange(GRID_W)
    cs = jnp.clip(col - WIN_COLS // 2, 0, GRID_W - WIN_COLS)
    in_win = (col[None, :] >= cs[:, None]) & (col[None, :] < cs[:, None] + WIN_COLS)
    dc_idx = jnp.clip(col[None, :] - col[:, None] + (WIN_COLS - 1), 0, 2 * WIN_COLS - 2)
    n_dr, n_dc = rpb.shape[1], rpb.shape[2]
    onehot = (dc_idx[None] == jnp.arange(n_dc)[:, None, None]).astype(F32)
    t = jnp.einsum("hrc,cqk->hrqk", rpb.astype(F32), onehot, precision=lax.Precision.HIGHEST)
    t = jnp.where(in_win[None, None], t, -jnp.inf)
    n_groups = N_ATT_HEADS // HEAD_GROUP
    t = t.reshape(n_groups, HEAD_GROUP, n_dr, GRID_W, GRID_W).transpose(0, 2, 4, 1, 3)
    return t.reshape(n_groups, n_dr * GRID_W, HEAD_GROUP * GRID_W)


def _neighborhood_attention(q, k, vt, k_ctx, vt_ctx, rpb, n_batch, seq_len):
    rows = seq_len // GRID_W
    kr = min(WIN_ROWS, rows)
    n_win = kr + 2
    assert rows >= n_win and (rows - n_win) % 2 == 0
    bias = _nbr_bias_table(rpb, rows)
    past = k_ctx.shape[1]
    gw = HEAD_GROUP * ATT_HEAD_DIM
    n_groups = N_ATT_HEADS // HEAD_GROUP
    row_spec = pl.BlockSpec((GRID_W, ATT_WIDTH), lambda b, r: (b * rows + r, 0))
    return pl.pallas_call(
        functools.partial(_nbr_attn_kernel, rows=rows, kr=kr),
        grid=(n_batch, rows),
        in_specs=[row_spec,
                  pl.BlockSpec((seq_len, ATT_WIDTH), lambda b, r: (b, 0)),
                  pl.BlockSpec((ATT_WIDTH, seq_len), lambda b, r: (0, b)),
                  pl.BlockSpec((1, past, ATT_WIDTH), lambda b, r: (b, 0, 0)),
                  pl.BlockSpec((1, ATT_WIDTH, past), lambda b, r: (b, 0, 0)),
                  _const_spec(bias.shape)],
        out_specs=row_spec,
        out_shape=jax.ShapeDtypeStruct((n_batch * seq_len, ATT_WIDTH), BF16),
        scratch_shapes=[pltpu.VMEM((n_groups, n_win * GRID_W, gw), BF16)],
        compiler_params=_params(2),
        name="nbr_attn",
    )(q, k, vt, k_ctx, vt_ctx, bias)


def _mlstm_kernel(*refs, chunk, n_chunks, has_state):
    dh = ML_HEAD_DIM
    nrep = 2 * GATE_ROWS
    if has_state:
        (q_ref, k_ref, vt_ref, mot_ref, gate_ref, gtt_ref, g_ref, c0_ref, n0_ref, m0_ref,
         o_ref, c_ref, n_ref, m_ref, hf_ref, hb_ref, cn_ref, mrun_ref) = refs
        for d in range(2):
            for hh in range(N_ML_HEADS):
                cn_ref[d, hh, 0:dh, :] = c0_ref[0, d, hh].T
                cn_ref[d, hh, dh:dh + nrep, :] = jnp.broadcast_to(n0_ref[0, d, hh], (nrep, dh))
        mrun_ref[...] = m0_ref[0]
    else:
        (q_ref, k_ref, vt_ref, mot_ref, gate_ref, gtt_ref, g_ref,
         o_ref, c_ref, n_ref, m_ref, hf_ref, hb_ref, cn_ref, mrun_ref) = refs
        cn_ref[...] = jnp.zeros_like(cn_ref)
        mrun_ref[...] = jnp.zeros_like(mrun_ref)
    use_state = has_state or n_chunks > 1

    i0 = lax.broadcasted_iota(jnp.int32, (chunk, chunk), 0)
    i1 = lax.broadcasted_iota(jnp.int32, (chunk, chunk), 1)
    row_id = lax.broadcasted_iota(jnp.int32, (GATE_ROWS, 1), 0)
    instances = [(hh, d) for d in range(2) for hh in range(N_ML_HEADS)]

    def split3(x):
        hi = x.astype(BF16)
        r1 = x - hi.astype(F32)
        mid = r1.astype(BF16)
        return hi, mid, (r1 - mid.astype(F32)).astype(BF16)

    def body(i, _):
        chunks = (i, n_chunks - 1 - i)
        rows = [pl.ds(pl.multiple_of(c * chunk, chunk), chunk) for c in chunks]
        le = [i0 <= i1, i0 >= i1]
        ge = [i0 >= i1, i0 <= i1]
        m_prev = mrun_ref[...][:, 0:1]

        def qkv(hh, d):
            sl = slice(hh * dh, (hh + 1) * dh)
            return q_ref[rows[d], sl], k_ref[rows[d], sl], vt_ref[sl, rows[d]]

        st = [_dot_nt(qkv(hh, d)[1], qkv(hh, d)[0]) for hh, d in instances]
        if use_state:
            cn = [cn_ref[d, hh] for hh, d in instances]
            qct = [_dot_nt(cn[n].astype(BF16), qkv(hh, d)[0]) for n, (hh, d) in enumerate(instances)]

        a_col, a8, cs8, w8, decay8, m_new = [], [], [], [], [], []
        for d in range(2):
            gi_r = gtt_ref[0:GATE_ROWS, rows[d]]
            lf_r = _log_sigmoid(gtt_ref[GATE_ROWS:2 * GATE_ROWS, rows[d]])
            tri_r = jnp.where(le[d], 1.0, 0.0).astype(BF16)
            cs_r = sum(_dot(t, tri_r) for t in split3(lf_r))
            b_last = jnp.sum(lf_r, axis=1, keepdims=True)
            w_end = b_last + gi_r - cs_r
            m_d = jnp.maximum(b_last + m_prev, jnp.max(w_end, axis=1, keepdims=True))
            a8.append(gi_r - cs_r)
            cs8.append(cs_r)
            decay8.append(jnp.exp(b_last + m_prev - m_d))
            w8.append(jnp.exp(w_end - m_d))
            m_new.append(m_d)
            lf_c = _log_sigmoid(gate_ref[rows[d], LANES:2 * LANES])
            tri_c = jnp.where(ge[d], 1.0, 0.0).astype(BF16)
            cs_c = sum(_dot(tri_c, t) for t in split3(lf_c))
            a_col.append(gate_ref[rows[d], 0:LANES] - cs_c)
        m_next = jnp.where((row_id & 1) == 0, m_new[0], m_new[1])
        mrun_ref[...] = jnp.broadcast_to(m_next, (GATE_ROWS, LANES))

        pt_all, g_all, den_all = [], [], []
        for n, (hh, d) in enumerate(instances):
            j = 2 * hh + d
            a = jnp.where(le[d], a_col[d][:, j:j + 1], -jnp.inf)
            g = jnp.maximum(jnp.max(a, axis=0, keepdims=True), m_prev[j:j + 1])
            pt = st[n] * jnp.exp(a - g)
            pt_all.append(pt)
            g_all.append(g)
            den_all.append(jnp.sum(pt, axis=0, keepdims=True))

        for n, (hh, d) in enumerate(instances):
            j = 2 * hh + d
            sl = slice(hh * dh, (hh + 1) * dh)
            _, k, vt = qkv(hh, d)
            g, den = g_all[n], den_all[n]
            num = _dot(vt, pt_all[n].astype(BF16))
            if use_state:
                w_inter = jnp.exp(m_prev[j:j + 1] - g)
                num = num + w_inter * qct[n][0:dh]
                den = den + w_inter * qct[n][dh:dh + 1]
            scale = 1.0 / jnp.maximum(jnp.abs(den), jnp.exp(-(cs8[d][j:j + 1] + g)))
            (hb_ref if d else hf_ref)[sl, rows[d]] = num * scale
            w_row = w8[d][j:j + 1]
            lhs = jnp.concatenate([(vt.astype(F32) * w_row).astype(BF16),
                                   jnp.broadcast_to(w_row, (nrep, chunk)).astype(BF16)], axis=0)
            upd = _dot(lhs, k)
            if use_state:
                upd = decay8[d][j:j + 1] * cn[n] + upd
            cn_ref[d, hh] = upd
        return 0

    def finish(c, _):
        rows = pl.ds(pl.multiple_of(c * chunk, chunk), chunk)
        for hh in range(N_ML_HEADS):
            sl = slice(hh * dh, (hh + 1) * dh)
            hs = hf_ref[sl, rows] + hb_ref[sl, rows]
            y = hs * lax.rsqrt(jnp.mean(hs * hs, axis=0, keepdims=True) + EPS) * g_ref[sl, :]
            y = y * jax.nn.sigmoid(mot_ref[sl, rows])
            o_ref[rows, sl] = y.T.astype(o_ref.dtype)
        return 0

    if n_chunks == 1:
        body(0, 0)
        finish(0, 0)
    else:
        lax.fori_loop(0, n_chunks, body, 0)
        lax.fori_loop(0, n_chunks, finish, 0)
    for d in range(2):
        for hh in range(N_ML_HEADS):
            cn = cn_ref[d, hh]
            c_ref[0, d, hh] = cn[0:dh].T
            n_ref[0, d, hh] = cn[dh:dh + 1]
    m_ref[0] = mrun_ref[...]


def _mlstm(mq, mk, mvt, mot, gates, gates_t, ml_g, state, n_batch, seq_len, chunk):
    dh = ML_HEAD_DIM
    nh = N_ML_HEADS
    width = nh * dh
    has_state = state is not None
    seq = pl.BlockSpec((seq_len, width), lambda b: (b, 0))
    seq_t = pl.BlockSpec((width, seq_len), lambda b: (0, b))
    st_c = pl.BlockSpec((1, 2, nh, dh, dh), lambda b: (b, 0, 0, 0, 0))
    st_n = pl.BlockSpec((1, 2, nh, 1, dh), lambda b: (b, 0, 0, 0, 0))
    st_m = pl.BlockSpec((1, GATE_ROWS, LANES), lambda b: (b, 0, 0))
    in_specs = [seq, seq, seq_t, seq_t,
                pl.BlockSpec((seq_len, 2 * LANES), lambda b: (b, 0)),
                pl.BlockSpec((2 * GATE_ROWS, seq_len), lambda b: (0, b)),
                _const_spec((width, chunk))]
    args = [mq, mk, mvt, mot, gates, gates_t, jnp.broadcast_to(ml_g.reshape(width, 1), (width, chunk))]
    if has_state:
        in_specs += [st_c, st_n, st_m]
        args += list(state)
    out_shape = [jax.ShapeDtypeStruct((n_batch * seq_len, width), BF16),
                 jax.ShapeDtypeStruct((n_batch, 2, nh, dh, dh), F32),
                 jax.ShapeDtypeStruct((n_batch, 2, nh, 1, dh), F32),
                 jax.ShapeDtypeStruct((n_batch, GATE_ROWS, LANES), F32)]
    return pl.pallas_call(
        functools.partial(_mlstm_kernel, chunk=chunk, n_chunks=seq_len // chunk, has_state=has_state),
        grid=(n_batch,),
        in_specs=in_specs,
        out_specs=[seq, st_c, st_n, st_m],
        out_shape=out_shape,
        scratch_shapes=[pltpu.VMEM((width, seq_len), F32), pltpu.VMEM((width, seq_len), F32),
                        pltpu.VMEM((2, nh, dh + 2 * GATE_ROWS, dh), F32), pltpu.VMEM((GATE_ROWS, LANES), F32)],
        compiler_params=_params(1),
        name="mlstm_latent" if has_state else "mlstm_ctx",
    )(*args)


def _pack_gate_rows(m):
    b = m.shape[0]
    packed = m.transpose(0, 2, 1).reshape(b, 2 * N_ML_HEADS, 1)
    return jnp.broadcast_to(packed, (b, GATE_ROWS, LANES))


def _unpack_gate_rows(m):
    b = m.shape[0]
    return m[:, :, 0].reshape(b, N_ML_HEADS, 2).transpose(0, 2, 1)


def _outproj_kernel(oa_ref, om_ref, x_ref, mod_ref, g2_ref, wa_ref, wm_ref, x1_ref, h2_ref):
    d = x_ref.shape[-1]
    mod = mod_ref[0]
    g1 = mod[:, 2 * d:3 * d]
    sh2, sc2 = mod[:, 3 * d:4 * d], mod[:, 4 * d:5 * d]
    out = _dot(oa_ref[...], wa_ref[...]) + _dot(om_ref[...], wm_ref[...])
    x1 = x_ref[...] + g1 * out
    x1_ref[...] = x1
    y = x1 * lax.rsqrt(jnp.mean(x1 * x1, axis=-1, keepdims=True) + EPS)
    h2_ref[...] = ((y * g2_ref[...]) * (1.0 + sc2) + sh2).astype(h2_ref.dtype)


def _out_projection(o_att, o_ml, x, mod, mod_row, seq_len, wts):
    n_tok, d = x.shape
    tm = _token_tile(n_tok, seq_len)

    def tok(width):
        return pl.BlockSpec((tm, width), lambda i: (i, 0))

    return pl.pallas_call(
        _outproj_kernel,
        grid=(n_tok // tm,),
        in_specs=[tok(ATT_WIDTH), tok(ML_WIDTH), tok(d),
                  pl.BlockSpec((1, 1, mod.shape[-1]), lambda i: (mod_row(i, tm), 0, 0)),
                  _const_spec((1, d)), _const_spec((ATT_WIDTH, d)), _const_spec((ML_WIDTH, d))],
        out_specs=[tok(d), tok(d)],
        out_shape=[jax.ShapeDtypeStruct((n_tok, d), F32), jax.ShapeDtypeStruct((n_tok, d), BF16)],
        compiler_params=_params(1),
        name="out_proj",
    )(o_att, o_ml, x, mod, wts["norm2_g"], wts["w_out_att"], wts["w_out_ml"])


def _ffn_kernel(h_ref, hprev_ref, hnext_ref, x1_ref, mod_ref, wup_ref, cw_ref, cb_ref, wd_ref,
                y_ref, lhs_ref, ug_ref, uv_ref, acc_ref, *, seq_len, n_col_tiles):
    tm, d = x1_ref.shape
    lhs_ref[0:tm, :] = h_ref[...]
    halo_row = lax.broadcasted_iota(jnp.int32, (HALO, d), 0)
    lhs_ref[tm:tm + HALO, :] = jnp.where(halo_row < HALO // 2, hnext_ref[...], hprev_ref[...])
    acc_ref[...] = jnp.zeros_like(acc_ref)
    sub = 8
    period = min(seq_len, tm)
    first_groups = sorted({r // sub for r in range(0, tm, period)})
    last_groups = sorted({(r + period - 1) // sub for r in range(0, tm, period)})
    tile0 = pl.program_id(0) * tm

    def zero_rows(x, groups, target):
        parts, at = [], 0
        for grp in groups:
            lo = grp * sub
            if lo > at:
                parts.append(x[at:lo])
            pos = (tile0 + lo + lax.broadcasted_iota(jnp.int32, (sub, 1), 0)) % seq_len
            parts.append(jnp.where(pos == target, 0.0, x[lo:lo + sub]))
            at = lo + sub
        if at < tm:
            parts.append(x[at:tm])
        return jnp.concatenate(parts, axis=0)

    def conv(u, cw, cb):
        prev = zero_rows(pltpu.roll(u, 1, 0)[0:tm], first_groups, 0)
        nxt = zero_rows(pltpu.roll(u, tm + HALO - 1, 0)[0:tm], last_groups, seq_len - 1)
        return prev * cw[0:1] + u[0:tm] * cw[1:2] + nxt * cw[2:3] + cb

    def up(j, slot):
        lhs = lhs_ref[...]
        ug_ref[slot] = _dot(lhs, wup_ref[j])
        uv_ref[slot] = _dot(lhs, wup_ref[n_col_tiles + j])

    def down(j, slot):
        gate = conv(ug_ref[slot], cw_ref[j], cb_ref[j])
        val = conv(uv_ref[slot], cw_ref[n_col_tiles + j], cb_ref[n_col_tiles + j])
        act = (gate * jax.nn.sigmoid(gate)) * val
        return _dot(act.astype(BF16), wd_ref[j])

    def pair(i, _):
        up(2 * i + 1, 1)
        acc_ref[...] += down(2 * i, 0)
        up(2 * i + 2, 0)
        acc_ref[...] += down(2 * i + 1, 1)
        return 0

    up(0, 0)
    n_pairs = (n_col_tiles - 1) // 2
    if n_pairs:
        lax.fori_loop(0, n_pairs, pair, 0)
    j0 = 2 * n_pairs
    if n_col_tiles - j0 == 2:
        up(j0 + 1, 1)
        acc_ref[...] += down(j0, 0)
        tail = down(j0 + 1, 1)
    else:
        tail = down(j0, 0)
    g2 = mod_ref[0][:, 5 * d:6 * d]
    y_ref[...] = x1_ref[...] + g2 * (acc_ref[...] + tail)


def _conv_ffn(h2, x1, mod, mod_row, seq_len, wts):
    n_tok, d = x1.shape
    tm = _token_tile(n_tok, seq_len)
    hpt = tm // HALO
    n_halo = n_tok // HALO
    nct, tn, _ = wts["w_down"].shape

    def tok(width):
        return pl.BlockSpec((tm, width), lambda i: (i, 0))

    in_specs = [tok(d),
                pl.BlockSpec((HALO, d), lambda i: (jnp.maximum(i * hpt - 1, 0), 0)),
                pl.BlockSpec((HALO, d), lambda i: (jnp.minimum((i + 1) * hpt, n_halo - 1), 0)),
                tok(d),
                pl.BlockSpec((1, 1, mod.shape[-1]), lambda i: (mod_row(i, tm), 0, 0)),
                _const_spec((2 * nct, d, tn)), _const_spec((2 * nct, 3, tn)), _const_spec((2 * nct, 1, tn)),
                _const_spec((nct, tn, d))]
    return pl.pallas_call(
        functools.partial(_ffn_kernel, seq_len=seq_len, n_col_tiles=nct),
        grid=(n_tok // tm,),
        in_specs=in_specs,
        out_specs=tok(d),
        out_shape=jax.ShapeDtypeStruct((n_tok, d), F32),
        scratch_shapes=[pltpu.VMEM((tm + HALO, d), BF16),
                        pltpu.VMEM((2, tm + HALO, tn), F32), pltpu.VMEM((2, tm + HALO, tn), F32),
                        pltpu.VMEM((tm, d), F32)],
        compiler_params=_params(1),
        name="conv_ffn",
    )(h2, h2, h2, x1, mod, wts["w_up"], wts["conv_w"], wts["conv_b"], wts["w_down"])


def _retile_cast_kernel(w_ref, o_ref):
    o_ref[0] = w_ref[...].astype(o_ref.dtype)


def _retile_cast(w, tn):
    r, c = w.shape
    return pl.pallas_call(
        _retile_cast_kernel,
        grid=(c // tn,),
        in_specs=[pl.BlockSpec((r, tn), lambda j: (0, j))],
        out_specs=pl.BlockSpec((1, r, tn), lambda j: (j, 0, 0)),
        out_shape=jax.ShapeDtypeStruct((c // tn, r, tn), BF16),
        compiler_params=_params(1),
        name="retile_cast",
    )(w)


def _layer_weights(norm1_g, norm2_g, w_in, b_gate, q_g, k_g, ml_g, w_out, w_up, conv_w, conv_b, w_down):
    d = w_in.shape[0]
    a, m, nh = ATT_WIDTH, ML_WIDTH, N_ML_HEADS
    def gate_lanes(g):
        g = g.reshape(g.shape[0], 2, 2, nh).transpose(0, 2, 3, 1).reshape(g.shape[0], 2, 2 * nh)
        return jnp.pad(g, ((0, 0), (0, 0), (0, LANES - 2 * nh))).reshape(g.shape[0], 2 * LANES)

    w_g = gate_lanes(w_in[:, 3 * a + 4 * m:])
    b_g = gate_lanes(b_gate.astype(F32).reshape(1, 4 * nh))
    head_id = jnp.arange(a) // ATT_HEAD_DIM
    d_ff = w_down.shape[0]
    tn = MXU_WIDTH
    nct = d_ff // tn

    def col_tiles(w):
        return w.reshape(w.shape[0], 2 * nct, tn).transpose(1, 0, 2)

    return {
        "norm1_g": norm1_g.reshape(1, d), "norm2_g": norm2_g.reshape(1, d),
        "w_in": w_in.astype(BF16),
        "w_g": w_g.astype(BF16), "b_g": b_g,
        "q_g": jnp.tile(q_g, N_ATT_HEADS).reshape(1, a), "k_g": jnp.tile(k_g, N_ATT_HEADS).reshape(1, a),
        "blockdiag": (head_id[:, None] == head_id[None, :]).astype(BF16),
        "ml_g": ml_g.reshape(1, m),
        "w_out_att": w_out[:a].astype(BF16), "w_out_ml": w_out[a:].astype(BF16),
        "w_up": _retile_cast(w_up, tn),
        "conv_w": col_tiles(conv_w.astype(F32)), "conv_b": col_tiles(conv_b.astype(F32).reshape(1, -1)),
        "w_down": w_down.astype(BF16).reshape(nct, tn, w_down.shape[1]),
    }


def _rope_tables(seq_len):
    quarter = ML_HEAD_DIM // 4
    pos = np.arange(seq_len)
    inv_freq = ROPE_THETA ** (-np.arange(quarter, dtype=np.float64) / quarter)
    ang_r = (pos // GRID_W).astype(np.float64)[:, None] * inv_freq[None, :]
    ang_c = (pos % GRID_W).astype(np.float64)[:, None] * inv_freq[None, :]
    cos = np.concatenate([np.cos(ang_r)] * 2 + [np.cos(ang_c)] * 2, axis=-1)
    sin = np.concatenate([-np.sin(ang_r), np.sin(ang_r), -np.sin(ang_c), np.sin(ang_c)], axis=-1)
    return jnp.asarray(cos, F32), jnp.asarray(sin, F32)


def _layer(x, mod, mod_row, n_batch, seq_len, wts, *, latent, ctx_kv=None, rpb=None, state=None, rope=None):
    chunk = min(ML_CHUNK, seq_len)
    outs = _in_projection(x, mod, mod_row, seq_len, wts, rope, latent=latent)
    aq, ak, av, mq, mk, mvt, mot, gt, gtt = outs[:9]
    if latent:
        o_att = _neighborhood_attention(aq, ak, av, ctx_kv[0], ctx_kv[1], rpb, n_batch, seq_len)
        ak = av = None
    else:
        o_att = _context_attention(aq, ak, av, seq_len)
        ak, av = outs[9], outs[10]
    o_ml, c_f, n_f, m_f = _mlstm(mq, mk, mvt, mot, gt, gtt, wts["ml_g"], state, n_batch, seq_len, chunk)
    x1, h2 = _out_projection(o_att, o_ml, x, mod, mod_row, seq_len, wts)
    y = _conv_ffn(h2, x1, mod, mod_row, seq_len, wts)
    return y, ak, av, (c_f, n_f[:, :, :, 0, :], _unpack_gate_rows(m_f))


def kernel(x_prompt, x_sample, cache_k, cache_v, state_C, state_n, state_m, c, c_ctx, w_mod, b_mod, norm1_g,
           norm2_g, w_in, b_gate, q_norm_g, k_norm_g, rpb, ml_norm_g, w_out, w_up, conv_w, conv_b, w_down):
    batch, seq, d = x_prompt.shape
    dec_batch, dec_seq, _ = x_sample.shape
    depth = w_mod.shape[0]
    past = cache_k.shape[2]
    cvecs = jnp.concatenate([c_ctx[None, :], c], axis=0).astype(F32)
    rope = _rope_tables(dec_seq)

    def ctx_row(i, tm):
        return 0 * i

    def lat_row(i, tm):
        return 1 + (i * tm) // dec_seq

    xp = x_prompt.reshape(batch * seq, d)
    xs = x_sample.reshape(dec_batch * dec_seq, d)
    ks, vs, cs, ns, ms = [], [], [], [], []
    for l in range(depth):
        wts = _layer_weights(norm1_g[l], norm2_g[l], w_in[l], b_gate[l], q_norm_g[l], k_norm_g[l],
                             ml_norm_g[l], w_out[l], w_up[l], conv_w[l], conv_b[l], w_down[l])
        mod = _modulation(cvecs, w_mod[l], b_mod[l])

        xp, ak, av, st = _layer(xp, mod, ctx_row, batch, seq, wts, latent=False)
        ks.append(ak.reshape(batch, seq, N_ATT_HEADS, ATT_HEAD_DIM))
        vs.append(av.reshape(batch, seq, N_ATT_HEADS, ATT_HEAD_DIM))
        cs.append(st[0])
        ns.append(st[1])
        ms.append(st[2])

        state = (state_C[:, l].astype(F32),
                 state_n[:, l].astype(F32)[:, :, :, None, :],
                 _pack_gate_rows(state_m[:, l].astype(F32)))
        ctx_kv = (cache_k[:, l].reshape(dec_batch, past, ATT_WIDTH).astype(BF16),
                  cache_v[:, l].reshape(dec_batch, past, ATT_WIDTH).transpose(0, 2, 1).astype(BF16))
        xs, _, _, _ = _layer(xs, mod, lat_row, dec_batch, dec_seq, wts, latent=True, ctx_kv=ctx_kv,
                             rpb=rpb[l], state=state, rope=rope)
    return (xp.reshape(batch, seq, d), xs.reshape(dec_batch, dec_seq, d),
            jnp.stack(ks, axis=1), jnp.stack(vs, axis=1),
            jnp.stack(cs, axis=1), jnp.stack(ns, axis=1), jnp.stack(ms, axis=1))
```

```python
import functools

import jax
import jax.numpy as jnp
import numpy as np
from jax import lax
from jax.experimental import pallas as pl
from jax.experimental.pallas import tpu as pltpu

F32 = jnp.float32
BF16 = jnp.bfloat16

GRID_W = 64
N_ATT_HEADS = 8
ATT_HEAD_DIM = 64
ATT_WIDTH = N_ATT_HEADS * ATT_HEAD_DIM
WIN_ROWS = 8
WIN_COLS = 16
HEAD_GROUP = 4
N_ML_HEADS = 4
ML_HEAD_DIM = 128
ML_WIDTH = N_ML_HEADS * ML_HEAD_DIM
ROPE_THETA = 10000.0
EPS = 1e-6

LANES = 128
MXU_WIDTH = 256
ML_CHUNK = 256
GATE_ROWS = 8
HALO = 16
TOKEN_TILE = 512
VMEM_LIMIT = 56 * 1024 * 1024


def _dot(a, b):
    return jnp.dot(a, b, preferred_element_type=F32)


def _dot_nt(a, b):
    return lax.dot_general(a, b, (((1,), (1,)), ((), ())), preferred_element_type=F32)


def _dot_tn(a, b):
    return lax.dot_general(a, b, (((0,), (0,)), ((), ())), preferred_element_type=F32)


def _const_spec(shape):
    nd = len(shape)
    return pl.BlockSpec(shape, lambda *_: (0,) * nd, pipeline_mode=pl.Buffered(1))


def _params(n_axes):
    return pltpu.CompilerParams(dimension_semantics=("arbitrary",) * n_axes,
                                vmem_limit_bytes=VMEM_LIMIT)


def _token_tile(n_tok, seq_len):
    tm = min(TOKEN_TILE, n_tok)
    while n_tok % tm or (seq_len % tm and tm % seq_len):
        tm //= 2
    return tm


def _log_sigmoid(x):
    return jnp.minimum(x, 0.0) - jnp.log1p(jnp.exp(-jnp.abs(x)))


def _mod_kernel(c_ref, w_ref, b_ref, o_ref):
    c = c_ref[...]
    s = c * jax.nn.sigmoid(c)
    o_ref[...] = _dot(s.astype(BF16), w_ref[...].astype(BF16)) + b_ref[...]


def _modulation(cvecs, w_mod, b_mod):
    r, d = cvecs.shape
    n = w_mod.shape[1]
    tn = d
    out = pl.pallas_call(
        _mod_kernel,
        grid=(n // tn,),
        in_specs=[pl.BlockSpec((r, d), lambda j: (0, 0)),
                  pl.BlockSpec((d, tn), lambda j: (0, j)),
                  pl.BlockSpec((1, tn), lambda j: (0, j))],
        out_specs=pl.BlockSpec((r, tn), lambda j: (0, j)),
        out_shape=jax.ShapeDtypeStruct((r, n), F32),
        compiler_params=_params(1),
        name="adaln_mod",
    )(cvecs, w_mod, b_mod.reshape(1, n))
    return out.reshape(r, 1, n)


def _rope(x, cos, sin_signed):
    lane = lax.broadcasted_iota(jnp.int32, x.shape, 1)
    partner = jnp.where((lane & 32) == 0, pltpu.roll(x, LANES - 32, 1), pltpu.roll(x, 32, 1))
    return x * cos + partner * sin_signed


def _inproj_kernel(*refs, latent):
    if latent:
        (x_ref, mod_ref, g1_ref, win_ref, wg_ref, bg_ref, qg_ref, kg_ref, bd_ref, cos_ref, sin_ref,
         aq_ref, ak_ref, av_ref, mq_ref, mk_ref, mv_ref, mo_ref, gtt_ref) = refs
    else:
        (x_ref, mod_ref, g1_ref, win_ref, wg_ref, bg_ref, qg_ref, kg_ref, bd_ref,
         aq_ref, ak_ref, av_ref, mq_ref, mk_ref, mv_ref, mo_ref, gtt_ref, ck_ref, cv_ref,
         cks_ref, cvs_ref) = refs
    x = x_ref[...]
    d = x.shape[-1]
    mod = mod_ref[0]
    sh1, sc1 = mod[:, 0:d], mod[:, d:2 * d]
    y = x * lax.rsqrt(jnp.mean(x * x, axis=-1, keepdims=True) + EPS)
    h = (y * g1_ref[...]) * (1.0 + sc1) + sh1
    hb = h.astype(BF16)

    def head_norm(a, g):
        ss = _dot((a * a).astype(BF16), bd_ref[...])
        return a * lax.rsqrt(ss * (1.0 / ATT_HEAD_DIM) + EPS) * g

    w = ATT_WIDTH
    att = _dot(hb, win_ref[:, 0:3 * w])
    aq_ref[...] = (head_norm(att[:, 0:w], qg_ref[...]) * ATT_HEAD_DIM ** -0.5).astype(aq_ref.dtype)
    kn = head_norm(att[:, w:2 * w], kg_ref[...])
    av = att[:, 2 * w:3 * w]
    ak_ref[...] = kn.astype(BF16)
    if latent:
        av_ref[...] = av.T.astype(BF16)
    else:
        av_ref[...] = av.astype(BF16)
        tm = kn.shape[0]
        for src, dst, scr in ((kn, ck_ref, cks_ref), (av, cv_ref, cvs_ref)):
            for hh in range(N_ATT_HEADS):
                pair = src[:, (hh // 2) * LANES:(hh // 2 + 1) * LANES]
                if hh % 2:
                    pair = pltpu.roll(pair, ATT_HEAD_DIM, 1)
                scr[pl.ds(hh, tm, stride=N_ATT_HEADS), :] = pair
            dst[...] = scr[...].reshape(tm, N_ATT_HEADS, LANES)[:, :, 0:ATT_HEAD_DIM]

    w = ML_WIDTH
    ml0 = 3 * ATT_WIDTH
    mq = _dot(hb, win_ref[:, ml0:ml0 + w])
    mk = _dot(hb, win_ref[:, ml0 + w:ml0 + 2 * w]) * ML_HEAD_DIM ** -0.5
    if latent:
        cos, sin = cos_ref[...], sin_ref[...]
        for hh in range(N_ML_HEADS):
            sl = slice(hh * ML_HEAD_DIM, (hh + 1) * ML_HEAD_DIM)
            mq_ref[:, sl] = _rope(mq[:, sl], cos, sin).astype(BF16)
            mk_ref[:, sl] = _rope(mk[:, sl], cos, sin).astype(BF16)
    else:
        mq_ref[...] = mq.astype(BF16)
        mk_ref[...] = mk.astype(BF16)
    mv_ref[...] = _dot(hb, win_ref[:, ml0 + 2 * w:ml0 + 3 * w]).T.astype(BF16)
    mo_ref[...] = _dot(hb, win_ref[:, ml0 + 3 * w:ml0 + 4 * w]).T
    gates = _dot(hb, wg_ref[...]) + bg_ref[...]
    gtt_ref[...] = gates.T[0:2 * GATE_ROWS]


def _in_projection(x, mod, mod_row, seq_len, wts, rope, *, latent):
    n_tok, d = x.shape
    tm = _token_tile(n_tok, seq_len)
    tiles_per_seq = max(seq_len // tm, 1)

    def tok(width):
        return pl.BlockSpec((tm, width), lambda i: (i, 0))

    in_specs = [tok(d),
                pl.BlockSpec((1, 1, mod.shape[-1]), lambda i: (mod_row(i, tm), 0, 0)),
                _const_spec((1, d)),
                _const_spec(wts["w_in"].shape),
                _const_spec(wts["w_g"].shape), _const_spec((1, LANES)),
                _const_spec((1, ATT_WIDTH)), _const_spec((1, ATT_WIDTH)),
                _const_spec((ATT_WIDTH, ATT_WIDTH))]
    args = [x, mod, wts["norm1_g"], wts["w_in"], wts["w_g"], wts["b_g"],
            wts["q_g"], wts["k_g"], wts["blockdiag"]]
    if latent:
        in_specs += [pl.BlockSpec((tm, LANES), lambda i: (i % tiles_per_seq, 0))] * 2
        args += [rope[0], rope[1]]
    out_shape = [jax.ShapeDtypeStruct((n_tok, ATT_WIDTH), BF16),
                 jax.ShapeDtypeStruct((n_tok, ATT_WIDTH), BF16),
                 jax.ShapeDtypeStruct((ATT_WIDTH, n_tok) if latent else (n_tok, ATT_WIDTH), BF16),
                 jax.ShapeDtypeStruct((n_tok, ML_WIDTH), BF16),
                 jax.ShapeDtypeStruct((n_tok, ML_WIDTH), BF16),
                 jax.ShapeDtypeStruct((ML_WIDTH, n_tok), BF16),
                 jax.ShapeDtypeStruct((ML_WIDTH, n_tok), F32),
                 jax.ShapeDtypeStruct((2 * GATE_ROWS, n_tok), F32)]

    def tok_t(height):
        return pl.BlockSpec((height, tm), lambda i: (0, i))

    av_spec = tok_t(ATT_WIDTH) if latent else tok(ATT_WIDTH)
    out_specs = ([tok(ATT_WIDTH)] * 2 + [av_spec] + [tok(ML_WIDTH)] * 2 + [tok_t(ML_WIDTH)] * 2
                 + [tok_t(2 * GATE_ROWS)])
    if not latent:
        cache = jax.ShapeDtypeStruct((n_tok, N_ATT_HEADS, ATT_HEAD_DIM), F32)
        out_shape += [cache, cache]
        out_specs += [pl.BlockSpec((tm, N_ATT_HEADS, ATT_HEAD_DIM), lambda i: (i, 0, 0))] * 2
    scratch = [] if latent else [pltpu.VMEM((tm * N_ATT_HEADS, LANES), F32)] * 2
    return pl.pallas_call(
        functools.partial(_inproj_kernel, latent=latent),
        grid=(n_tok // tm,),
        in_specs=in_specs,
        out_specs=out_specs,
        out_shape=out_shape,
        scratch_shapes=scratch,
        compiler_params=_params(1),
        name="in_proj_latent" if latent else "in_proj_ctx",
    )(*args)


def _ctx_attn_kernel(q_ref, k_ref, v_ref, o_ref):
    n = q_ref.shape[0]
    gw = HEAD_GROUP * ATT_HEAD_DIM
    lane_head = lax.broadcasted_iota(jnp.int32, (n, gw), 1) // ATT_HEAD_DIM
    for g in range(N_ATT_HEADS // HEAD_GROUP):
        sl = slice(g * gw, (g + 1) * gw)
        q4 = q_ref[:, sl]
        qbd = jnp.concatenate([jnp.where(lane_head == hl, q4, jnp.zeros_like(q4))
                               for hl in range(HEAD_GROUP)], axis=0)
        s = _dot_nt(k_ref[:, sl], qbd)
        p = jnp.exp(s - jnp.max(s, axis=0, keepdims=True))
        p = p / jnp.sum(p, axis=0, keepdims=True)
        o4 = _dot_tn(p.astype(BF16), v_ref[:, sl])
        out = jnp.where(lane_head == 0, o4[0:n], 0.0)
        for hl in range(1, HEAD_GROUP):
            out = out + jnp.where(lane_head == hl, o4[hl * n:(hl + 1) * n], 0.0)
        o_ref[:, sl] = out.astype(o_ref.dtype)


def _context_attention(q, k, v, seq_len):
    n_tok = q.shape[0]
    spec = pl.BlockSpec((seq_len, ATT_WIDTH), lambda b: (b, 0))
    return pl.pallas_call(
        _ctx_attn_kernel,
        grid=(n_tok // seq_len,),
        in_specs=[spec, spec, spec],
        out_specs=spec,
        out_shape=jax.ShapeDtypeStruct((n_tok, ATT_WIDTH), BF16),
        compiler_params=_params(1),
        name="ctx_attn",
    )(q, k, v)


def _nbr_attn_kernel(q_ref, k_ref, vt_ref, kc_ref, vct_ref, bias_ref, o_ref, pt_ref, *, rows, kr):
    r = pl.program_id(1)
    n_win = kr + 2
    n_loc = kr * GRID_W
    gw = HEAD_GROUP * ATT_HEAD_DIM
    groups = N_ATT_HEADS // HEAD_GROUP
    lane_head = lax.broadcasted_iota(jnp.int32, (GRID_W, gw), 1) // ATT_HEAD_DIM
    rs = jnp.clip(r - kr // 2, 0, rows - kr)
    start = jnp.minimum(rs - (rs & 1), rows - n_win)
    delta = rs - start
    bias0 = pl.multiple_of((WIN_ROWS - 1 - (r - rs)) * GRID_W, GRID_W)
    zeros2 = jnp.zeros((2 * GRID_W, gw), BF16)
    for g in range(groups):
        sl = slice(g * gw, (g + 1) * gw)
        q4 = q_ref[:, sl]
        qbd = jnp.concatenate([jnp.where(lane_head == hl, q4, jnp.zeros_like(q4))
                               for hl in range(HEAD_GROUP)], axis=0)
        s_loc = _dot_nt(k_ref[pl.ds(pl.multiple_of(rs * GRID_W, GRID_W), n_loc), sl], qbd)
        s_loc = s_loc + bias_ref[g, pl.ds(bias0, n_loc), :]
        s_ctx = _dot_nt(kc_ref[0, :, sl], qbd)
        m = jnp.maximum(jnp.max(s_loc, axis=0, keepdims=True), jnp.max(s_ctx, axis=0, keepdims=True))
        p_loc = jnp.exp(s_loc - m)
        p_ctx = jnp.exp(s_ctx - m)
        l = jnp.sum(p_loc, axis=0, keepdims=True) + jnp.sum(p_ctx, axis=0, keepdims=True)
        pt_ref[g, 0:2 * GRID_W, :] = zeros2
        pt_ref[g, n_loc:n_loc + 2 * GRID_W, :] = zeros2
        pt_ref[g, pl.ds(pl.multiple_of(delta * GRID_W, GRID_W), n_loc), :] = p_loc.astype(BF16)
        vt_win = vt_ref[sl, pl.ds(pl.multiple_of(start * GRID_W, 2 * GRID_W), n_win * GRID_W)]
        ot = _dot(vt_win, pt_ref[g]) + _dot(vct_ref[0, sl, :], p_ctx.astype(BF16))
        o4 = (ot / l).T
        out = jnp.where(lane_head == 0, o4[0:GRID_W], 0.0)
        for hl in range(1, HEAD_GROUP):
            out = out + jnp.where(lane_head == hl, o4[hl * GRID_W:(hl + 1) * GRID_W], 0.0)
        o_ref[:, sl] = out.astype(o_ref.dtype)


def _nbr_bias_table(rpb, rows):
    col = jnp.arange(GRID_W)
    cs = jnp.clip(col - WIN_COLS // 2, 0, GRID_W - WIN_COLS)
    in_win = (col[None, :] >= cs[:, None]) & (col[None, :] < cs[:, None] + WIN_COLS)
    dc_idx = jnp.clip(col[None, :] - col[:, None] + (WIN_COLS - 1), 0, 2 * WIN_COLS - 2)
    n_dr, n_dc = rpb.shape[1], rpb.shape[2]
    onehot = (dc_idx[None] == jnp.arange(n_dc)[:, None, None]).astype(F32)
    t = jnp.einsum("hrc,cqk->hrqk", rpb.astype(F32), onehot, precision=lax.Precision.HIGHEST)
    t = jnp.where(in_win[None, None], t, -jnp.inf)
    n_groups = N_ATT_HEADS // HEAD_GROUP
    t = t.reshape(n_groups, HEAD_GROUP, n_dr, GRID_W, GRID_W).transpose(0, 2, 4, 1, 3)
    return t.reshape(n_groups, n_dr * GRID_W, HEAD_GROUP * GRID_W)


def _neighborhood_attention(q, k, vt, k_ctx, vt_ctx, rpb, n_batch, seq_len):
    rows = seq_len // GRID_W
    kr = min(WIN_ROWS, rows)
    n_win = kr + 2
    assert rows >= n_win and (rows - n_win) % 2 == 0
    bias = _nbr_bias_table(rpb, rows)
    past = k_ctx.shape[1]
    gw = HEAD_GROUP * ATT_HEAD_DIM
    n_groups = N_ATT_HEADS // HEAD_GROUP
    row_spec = pl.BlockSpec((GRID_W, ATT_WIDTH), lambda b, r: (b * rows + r, 0))
    return pl.pallas_call(
        functools.partial(_nbr_attn_kernel, rows=rows, kr=kr),
        grid=(n_batch, rows),
        in_specs=[row_spec,
                  pl.BlockSpec((seq_len, ATT_WIDTH), lambda b, r: (b, 0)),
                  pl.BlockSpec((ATT_WIDTH, seq_len), lambda b, r: (0, b)),
                  pl.BlockSpec((1, past, ATT_WIDTH), lambda b, r: (b, 0, 0)),
                  pl.BlockSpec((1, ATT_WIDTH, past), lambda b, r: (b, 0, 0)),
                  _const_spec(bias.shape)],
        out_specs=row_spec,
        out_shape=jax.ShapeDtypeStruct((n_batch * seq_len, ATT_WIDTH), BF16),
        scratch_shapes=[pltpu.VMEM((n_groups, n_win * GRID_W, gw), BF16)],
        compiler_params=_params(2),
        name="nbr_attn",
    )(q, k, vt, k_ctx, vt_ctx, bias)


def _mlstm_kernel(*refs, chunk, n_chunks, has_state):
    dh = ML_HEAD_DIM
    nrep = 2 * GATE_ROWS
    if has_state:
        (q_ref, k_ref, vt_ref, mot_ref, gtt_ref, g_ref, c0_ref, n0_ref, m0_ref,
         o_ref, c_ref, n_ref, m_ref, hf_ref, hb_ref, cn_ref, mrun_ref) = refs
        for d in range(2):
            for hh in range(N_ML_HEADS):
                cn_ref[d, hh, 0:dh, :] = c0_ref[0, d, hh].T
                cn_ref[d, hh, dh:dh + nrep, :] = jnp.broadcast_to(n0_ref[0, d, hh], (nrep, dh))
        mrun_ref[...] = m0_ref[0]
    else:
        (q_ref, k_ref, vt_ref, mot_ref, gtt_ref, g_ref,
         o_ref, c_ref, n_ref, m_ref, hf_ref, hb_ref, cn_ref, mrun_ref) = refs
        cn_ref[...] = jnp.zeros_like(cn_ref)
        mrun_ref[...] = jnp.zeros_like(mrun_ref)
    use_state = has_state or n_chunks > 1

    i0 = lax.broadcasted_iota(jnp.int32, (chunk, chunk), 0)
    i1 = lax.broadcasted_iota(jnp.int32, (chunk, chunk), 1)
    row_id = lax.broadcasted_iota(jnp.int32, (GATE_ROWS, 1), 0)
    instances = [(hh, d) for d in range(2) for hh in range(N_ML_HEADS)]

    def split3(x):
        hi = x.astype(BF16)
        r1 = x - hi.astype(F32)
        mid = r1.astype(BF16)
        return hi, mid, (r1 - mid.astype(F32)).astype(BF16)

    def body(i, _):
        chunks = (i, n_chunks - 1 - i)
        rows = [pl.ds(pl.multiple_of(c * chunk, chunk), chunk) for c in chunks]
        le = [i0 <= i1, i0 >= i1]
        m_prev = mrun_ref[...][:, 0:1]

        def qkv(hh, d):
            sl = slice(hh * dh, (hh + 1) * dh)
            return q_ref[rows[d], sl], k_ref[rows[d], sl], vt_ref[sl, rows[d]]

        st = [_dot_nt(qkv(hh, d)[1], qkv(hh, d)[0]) for hh, d in instances]
        if use_state:
            cn = [cn_ref[d, hh] for hh, d in instances]
            qct = [_dot_nt(cn[n].astype(BF16), qkv(hh, d)[0]) for n, (hh, d) in enumerate(instances)]

        a_col, a8, cs8, w8, decay8, m_new = [], [], [], [], [], []
        for d in range(2):
            gi_r = gtt_ref[0:GATE_ROWS, rows[d]]
            lf_r = _log_sigmoid(gtt_ref[GATE_ROWS:2 * GATE_ROWS, rows[d]])
            tri_r = jnp.where(le[d], 1.0, 0.0).astype(BF16)
            cs_r = sum(_dot(t, tri_r) for t in split3(lf_r))
            b_last = jnp.sum(lf_r, axis=1, keepdims=True)
            w_end = b_last + gi_r - cs_r
            m_d = jnp.maximum(b_last + m_prev, jnp.max(w_end, axis=1, keepdims=True))
            a8.append(gi_r - cs_r)
            cs8.append(cs_r)
            decay8.append(jnp.exp(b_last + m_prev - m_d))
            w8.append(jnp.exp(w_end - m_d))
            m_new.append(m_d)
            pad = jnp.zeros((LANES - GATE_ROWS, chunk), F32)
            a_col.append(jnp.concatenate([gi_r - cs_r, pad], axis=0).T)
        m_next = jnp.where((row_id & 1) == 0, m_new[0], m_new[1])
        mrun_ref[...] = jnp.broadcast_to(m_next, (GATE_ROWS, LANES))

        pt_all, g_all, den_all = [], [], []
        for n, (hh, d) in enumerate(instances):
            j = 2 * hh + d
            a = jnp.where(le[d], a_col[d][:, j:j + 1], -jnp.inf)
            g = jnp.maximum(jnp.max(a, axis=0, keepdims=True), m_prev[j:j + 1])
            pt = st[n] * jnp.exp(a - g)
            pt_all.append(pt)
            g_all.append(g)
            den_all.append(jnp.sum(pt, axis=0, keepdims=True))

        for n, (hh, d) in enumerate(instances):
            j = 2 * hh + d
            sl = slice(hh * dh, (hh + 1) * dh)
            _, k, vt = qkv(hh, d)
            g, den = g_all[n], den_all[n]
            num = _dot(vt, pt_all[n].astype(BF16))
            if use_state:
                w_inter = jnp.exp(m_prev[j:j + 1] - g)
                num = num + w_inter * qct[n][0:dh]
                den = den + w_inter * qct[n][dh:dh + 1]
            scale = 1.0 / jnp.maximum(jnp.abs(den), jnp.exp(-(cs8[d][j:j + 1] + g)))
            (hb_ref if d else hf_ref)[sl, rows[d]] = num * scale
            w_row = w8[d][j:j + 1]
            lhs = jnp.concatenate([(vt.astype(F32) * w_row).astype(BF16),
                                   jnp.broadcast_to(w_row, (nrep, chunk)).astype(BF16)], axis=0)
            upd = _dot(lhs, k)
            if use_state:
                upd = decay8[d][j:j + 1] * cn[n] + upd
            cn_ref[d, hh] = upd
        return 0

    def finish(c, _):
        rows = pl.ds(pl.multiple_of(c * chunk, chunk), chunk)
        for hh in range(N_ML_HEADS):
            sl = slice(hh * dh, (hh + 1) * dh)
            hs = hf_ref[sl, rows] + hb_ref[sl, rows]
            y = hs * lax.rsqrt(jnp.mean(hs * hs, axis=0, keepdims=True) + EPS) * g_ref[sl, :]
            y = y * jax.nn.sigmoid(mot_ref[sl, rows])
            o_ref[rows, sl] = y.T.astype(o_ref.dtype)
        return 0

    if n_chunks == 1:
        body(0, 0)
        finish(0, 0)
    else:
        lax.fori_loop(0, n_chunks, body, 0)
        lax.fori_loop(0, n_chunks, finish, 0)
    for d in range(2):
        for hh in range(N_ML_HEADS):
            cn = cn_ref[d, hh]
            c_ref[0, d, hh] = cn[0:dh].T
            n_ref[0, d, hh] = cn[dh:dh + 1]
    m_ref[0] = mrun_ref[...]


def _mlstm(mq, mk, mvt, mot, gates_t, ml_g, state, n_batch, seq_len, chunk):
    dh = ML_HEAD_DIM
    nh = N_ML_HEADS
    width = nh * dh
    has_state = state is not None
    seq = pl.BlockSpec((seq_len, width), lambda b: (b, 0))
    seq_t = pl.BlockSpec((width, seq_len), lambda b: (0, b))
    st_c = pl.BlockSpec((1, 2, nh, dh, dh), lambda b: (b, 0, 0, 0, 0))
    st_n = pl.BlockSpec((1, 2, nh, 1, dh), lambda b: (b, 0, 0, 0, 0))
    st_m = pl.BlockSpec((1, GATE_ROWS, LANES), lambda b: (b, 0, 0))
    in_specs = [seq, seq, seq_t, seq_t,
                pl.BlockSpec((2 * GATE_ROWS, seq_len), lambda b: (0, b)),
                _const_spec((width, chunk))]
    args = [mq, mk, mvt, mot, gates_t, jnp.broadcast_to(ml_g.reshape(width, 1), (width, chunk))]
    if has_state:
        in_specs += [st_c, st_n, st_m]
        args += list(state)
    out_shape = [jax.ShapeDtypeStruct((n_batch * seq_len, width), BF16),
                 jax.ShapeDtypeStruct((n_batch, 2, nh, dh, dh), F32),
                 jax.ShapeDtypeStruct((n_batch, 2, nh, 1, dh), F32),
                 jax.ShapeDtypeStruct((n_batch, GATE_ROWS, LANES), F32)]
    return pl.pallas_call(
        functools.partial(_mlstm_kernel, chunk=chunk, n_chunks=seq_len // chunk, has_state=has_state),
        grid=(n_batch,),
        in_specs=in_specs,
        out_specs=[seq, st_c, st_n, st_m],
        out_shape=out_shape,
        scratch_shapes=[pltpu.VMEM((width, seq_len), F32), pltpu.VMEM((width, seq_len), F32),
                        pltpu.VMEM((2, nh, dh + 2 * GATE_ROWS, dh), F32), pltpu.VMEM((GATE_ROWS, LANES), F32)],
        compiler_params=_params(1),
        name="mlstm_latent" if has_state else "mlstm_ctx",
    )(*args)


def _pack_gate_rows(m):
    b = m.shape[0]
    packed = m.transpose(0, 2, 1).reshape(b, 2 * N_ML_HEADS, 1)
    return jnp.broadcast_to(packed, (b, GATE_ROWS, LANES))


def _unpack_gate_rows(m):
    b = m.shape[0]
    return m[:, :, 0].reshape(b, N_ML_HEADS, 2).transpose(0, 2, 1)


def _ffn_kernel(oa_ref, oap_ref, oan_ref, om_ref, omp_ref, omn_ref, x_ref, xp_ref, xn_ref, mod_ref, g2_ref,
                wout_ref, wup_ref, cw_ref, cb_ref, wd_ref,
                y_ref, oc_ref, lhs_ref, x1_ref, ug_ref, uv_ref, acc_ref, *, seq_len, n_col_tiles):
    tm, d = x_ref.shape
    aw = oa_ref.shape[1]

    def halo(next_ref, prev_ref):
        row = lax.broadcasted_iota(jnp.int32, next_ref.shape, 0)
        return jnp.where(row < HALO // 2, next_ref[...], prev_ref[...])

    oc_ref[0:tm, 0:aw] = oa_ref[...]
    oc_ref[0:tm, aw:] = om_ref[...]
    oc_ref[tm:tm + HALO, 0:aw] = halo(oan_ref, oap_ref)
    oc_ref[tm:tm + HALO, aw:] = halo(omn_ref, omp_ref)
    mod = mod_ref[0]
    g1 = mod[:, 2 * d:3 * d]
    sh2, sc2 = mod[:, 3 * d:4 * d], mod[:, 4 * d:5 * d]
    out = _dot(oc_ref[...], wout_ref[...])

    def norm2(x1):
        y = x1 * lax.rsqrt(jnp.mean(x1 * x1, axis=-1, keepdims=True) + EPS)
        return ((y * g2_ref[...]) * (1.0 + sc2) + sh2).astype(BF16)

    x1 = x_ref[...] + g1 * out[0:tm]
    x1_ref[...] = x1
    lhs_ref[0:tm, :] = norm2(x1)
    lhs_ref[tm:tm + HALO, :] = norm2(halo(xn_ref, xp_ref) + g1 * out[tm:tm + HALO])
    acc_ref[...] = jnp.zeros_like(acc_ref)
    sub = 8
    period = min(seq_len, tm)
    first_groups = sorted({r // sub for r in range(0, tm, period)})
    last_groups = sorted({(r + period - 1) // sub for r in range(0, tm, period)})
    tile0 = pl.program_id(0) * tm

    def zero_rows(x, groups, target):
        parts, at = [], 0
        for grp in groups:
            lo = grp * sub
            if lo > at:
                parts.append(x[at:lo])
            pos = (tile0 + lo + lax.broadcasted_iota(jnp.int32, (sub, 1), 0)) % seq_len
            parts.append(jnp.where(pos == target, 0.0, x[lo:lo + sub]))
            at = lo + sub
        if at < tm:
            parts.append(x[at:tm])
        return jnp.concatenate(parts, axis=0)

    def conv(u, cw, cb):
        prev = zero_rows(pltpu.roll(u, 1, 0)[0:tm], first_groups, 0)
        nxt = zero_rows(pltpu.roll(u, tm + HALO - 1, 0)[0:tm], last_groups, seq_len - 1)
        return prev * cw[0:1] + u[0:tm] * cw[1:2] + nxt * cw[2:3] + cb

    def up(j, slot):
        lhs = lhs_ref[...]
        ug_ref[slot] = _dot(lhs, wup_ref[j])
        uv_ref[slot] = _dot(lhs, wup_ref[n_col_tiles + j])

    def act(j, slot):
        gate = conv(ug_ref[slot], cw_ref[j], cb_ref[j])
        val = conv(uv_ref[slot], cw_ref[n_col_tiles + j], cb_ref[n_col_tiles + j])
        return ((gate * jax.nn.sigmoid(gate)) * val).astype(BF16)

    def stage(j, slot, next_up):
        a = act(j, slot)
        if next_up:
            up(j + 1, 1 - slot)
        return _dot(a, wd_ref[j])

    def pair(i, _):
        acc_ref[...] += stage(2 * i, 0, True)
        acc_ref[...] += stage(2 * i + 1, 1, True)
        return 0

    up(0, 0)
    n_pairs = (n_col_tiles - 1) // 2
    if n_pairs:
        lax.fori_loop(0, n_pairs, pair, 0)
    j0 = 2 * n_pairs
    if n_col_tiles - j0 == 2:
        acc_ref[...] += stage(j0, 0, True)
        tail = stage(j0 + 1, 1, False)
    else:
        tail = stage(j0, 0, False)
    g2 = mod_ref[0][:, 5 * d:6 * d]
    y_ref[...] = x1_ref[...] + g2 * (acc_ref[...] + tail)


def _mix_ffn(o_att, o_ml, x, mod, mod_row, seq_len, wts):
    n_tok, d = x.shape
    tm = _token_tile(n_tok, seq_len)
    hpt = tm // HALO
    n_halo = n_tok // HALO
    nct, tn, _ = wts["w_down"].shape
    mix = o_att.shape[1] + o_ml.shape[1]

    def tile_and_halos(width):
        return [pl.BlockSpec((tm, width), lambda i: (i, 0)),
                pl.BlockSpec((HALO, width), lambda i: (jnp.maximum(i * hpt - 1, 0), 0)),
                pl.BlockSpec((HALO, width), lambda i: (jnp.minimum((i + 1) * hpt, n_halo - 1), 0))]

    in_specs = (tile_and_halos(o_att.shape[1]) + tile_and_halos(o_ml.shape[1]) + tile_and_halos(d) + [
        pl.BlockSpec((1, 1, mod.shape[-1]), lambda i: (mod_row(i, tm), 0, 0)),
        _const_spec((1, d)), _const_spec((mix, d)),
        _const_spec((2 * nct, d, tn)), _const_spec((2 * nct, 3, tn)), _const_spec((2 * nct, 1, tn)),
        _const_spec((nct, tn, d))])
    return pl.pallas_call(
        functools.partial(_ffn_kernel, seq_len=seq_len, n_col_tiles=nct),
        grid=(n_tok // tm,),
        in_specs=in_specs,
        out_specs=pl.BlockSpec((tm, d), lambda i: (i, 0)),
        out_shape=jax.ShapeDtypeStruct((n_tok, d), F32),
        scratch_shapes=[pltpu.VMEM((tm + HALO, mix), BF16), pltpu.VMEM((tm + HALO, d), BF16),
                        pltpu.VMEM((tm, d), F32),
                        pltpu.VMEM((2, tm + HALO, tn), F32), pltpu.VMEM((2, tm + HALO, tn), F32),
                        pltpu.VMEM((tm, d), F32)],
        compiler_params=_params(1),
        name="mix_ffn",
    )(o_att, o_att, o_att, o_ml, o_ml, o_ml, x, x, x, mod, wts["norm2_g"], wts["w_out"],
      wts["w_up"], wts["conv_w"], wts["conv_b"], wts["w_down"])


def _retile_cast_kernel(w_ref, o_ref):
    tn = o_ref.shape[-1]
    for t in range(o_ref.shape[0]):
        o_ref[t] = w_ref[:, t * tn:(t + 1) * tn].astype(o_ref.dtype)


def _retile_cast(w, tn):
    r, c = w.shape
    group = 2 if (c // tn) % 2 == 0 else 1
    return pl.pallas_call(
        _retile_cast_kernel,
        grid=(c // (tn * group),),
        in_specs=[pl.BlockSpec((r, tn * group), lambda j: (0, j))],
        out_specs=pl.BlockSpec((group, r, tn), lambda j: (j, 0, 0)),
        out_shape=jax.ShapeDtypeStruct((c // tn, r, tn), BF16),
        compiler_params=_params(1),
        name="retile_cast",
    )(w)


def _cast_kernel(w_ref, o_ref):
    o_ref[...] = w_ref[...].astype(o_ref.dtype)


def _cast_rows(w, block_rows):
    r, c = w.shape
    spec = pl.BlockSpec((block_rows, c), lambda i: (i, 0))
    return pl.pallas_call(
        _cast_kernel,
        grid=(r // block_rows,),
        in_specs=[spec],
        out_specs=spec,
        out_shape=jax.ShapeDtypeStruct((r, c), BF16),
        compiler_params=_params(1),
        name="cast_rows",
    )(w)


def _layer_weights(norm1_g, norm2_g, w_in, b_gate, q_g, k_g, ml_g, w_out, w_up, conv_w, conv_b, w_down):
    d = w_in.shape[0]
    a, m, nh = ATT_WIDTH, ML_WIDTH, N_ML_HEADS
    def gate_lanes(g):
        g = g.reshape(g.shape[0], 2, 2, nh).transpose(0, 2, 3, 1).reshape(g.shape[0], 4 * nh)
        return jnp.pad(g, ((0, 0), (0, LANES - 4 * nh)))

    w_g = gate_lanes(w_in[:, 3 * a + 4 * m:])
    b_g = gate_lanes(b_gate.astype(F32).reshape(1, 4 * nh))
    head_id = jnp.arange(a) // ATT_HEAD_DIM
    d_ff = w_down.shape[0]
    tn = MXU_WIDTH
    nct = d_ff // tn

    def col_tiles(w):
        return w.reshape(w.shape[0], 2 * nct, tn).transpose(1, 0, 2)

    return {
        "norm1_g": norm1_g.reshape(1, d), "norm2_g": norm2_g.reshape(1, d),
        "w_in": w_in.astype(BF16),
        "w_g": w_g.astype(BF16), "b_g": b_g,
        "q_g": jnp.tile(q_g, N_ATT_HEADS).reshape(1, a), "k_g": jnp.tile(k_g, N_ATT_HEADS).reshape(1, a),
        "blockdiag": (head_id[:, None] == head_id[None, :]).astype(BF16),
        "ml_g": ml_g.reshape(1, m),
        "w_out": w_out.astype(BF16),
        "w_up": _retile_cast(w_up, tn),
        "conv_w": col_tiles(conv_w.astype(F32)), "conv_b": col_tiles(conv_b.astype(F32).reshape(1, -1)),
        "w_down": _cast_rows(w_down, tn).reshape(nct, tn, w_down.shape[1]),
    }


def _rope_tables(seq_len):
    quarter = ML_HEAD_DIM // 4
    pos = np.arange(seq_len)
    inv_freq = ROPE_THETA ** (-np.arange(quarter, dtype=np.float64) / quarter)
    ang_r = (pos // GRID_W).astype(np.float64)[:, None] * inv_freq[None, :]
    ang_c = (pos % GRID_W).astype(np.float64)[:, None] * inv_freq[None, :]
    cos = np.concatenate([np.cos(ang_r)] * 2 + [np.cos(ang_c)] * 2, axis=-1)
    sin = np.concatenate([-np.sin(ang_r), np.sin(ang_r), -np.sin(ang_c), np.sin(ang_c)], axis=-1)
    return jnp.asarray(cos, F32), jnp.asarray(sin, F32)


def _layer(x, mod, mod_row, n_batch, seq_len, wts, *, latent, ctx_kv=None, rpb=None, state=None, rope=None):
    chunk = min(ML_CHUNK, seq_len)
    outs = _in_projection(x, mod, mod_row, seq_len, wts, rope, latent=latent)
    aq, ak, av, mq, mk, mvt, mot, gtt = outs[:8]
    if latent:
        o_att = _neighborhood_attention(aq, ak, av, ctx_kv[0], ctx_kv[1], rpb, n_batch, seq_len)
        ak = av = None
    else:
        o_att = _context_attention(aq, ak, av, seq_len)
        ak, av = outs[8], outs[9]
    o_ml, c_f, n_f, m_f = _mlstm(mq, mk, mvt, mot, gtt, wts["ml_g"], state, n_batch, seq_len, chunk)
    y = _mix_ffn(o_att, o_ml, x, mod, mod_row, seq_len, wts)
    return y, ak, av, (c_f, n_f[:, :, :, 0, :], _unpack_gate_rows(m_f))


def kernel(x_prompt, x_sample, cache_k, cache_v, state_C, state_n, state_m, c, c_ctx, w_mod, b_mod, norm1_g,
           norm2_g, w_in, b_gate, q_norm_g, k_norm_g, rpb, ml_norm_g, w_out, w_up, conv_w, conv_b, w_down):
    batch, seq, d = x_prompt.shape
    dec_batch, dec_seq, _ = x_sample.shape
    depth = w_mod.shape[0]
    past = cache_k.shape[2]
    cvecs = jnp.concatenate([c_ctx[None, :], c], axis=0).astype(F32)
    rope = _rope_tables(dec_seq)

    def ctx_row(i, tm):
        return 0 * i

    def lat_row(i, tm):
        return 1 + (i * tm) // dec_seq

    xp = x_prompt.reshape(batch * seq, d)
    xs = x_sample.reshape(dec_batch * dec_seq, d)
    ks, vs, cs, ns, ms = [], [], [], [], []
    for l in range(depth):
        wts = _layer_weights(norm1_g[l], norm2_g[l], w_in[l], b_gate[l], q_norm_g[l], k_norm_g[l],
                             ml_norm_g[l], w_out[l], w_up[l], conv_w[l], conv_b[l], w_down[l])
        mod = _modulation(cvecs, w_mod[l], b_mod[l])

        xp, ak, av, st = _layer(xp, mod, ctx_row, batch, seq, wts, latent=False)
        ks.append(ak.reshape(batch, seq, N_ATT_HEADS, ATT_HEAD_DIM))
        vs.append(av.reshape(batch, seq, N_ATT_HEADS, ATT_HEAD_DIM))
        cs.append(st[0])
        ns.append(st[1])
        ms.append(st[2])

        state = (state_C[:, l].astype(F32),
                 state_n[:, l].astype(F32)[:, :, :, None, :],
                 _pack_gate_rows(state_m[:, l].astype(F32)))
        ctx_kv = (cache_k[:, l].reshape(dec_batch, past, ATT_WIDTH).astype(BF16),
                  cache_v[:, l].reshape(dec_batch, past, ATT_WIDTH).transpose(0, 2, 1).astype(BF16))
        xs, _, _, _ = _layer(xs, mod, lat_row, dec_batch, dec_seq, wts, latent=True, ctx_kv=ctx_kv,
                             rpb=rpb[l], state=state, rope=rope)
    return (xp.reshape(batch, seq, d), xs.reshape(dec_batch, dec_seq, d),
            jnp.stack(ks, axis=1), jnp.stack(vs, axis=1),
            jnp.stack(cs, axis=1), jnp.stack(ns, axis=1), jnp.stack(ms, axis=1))
```

```python
import functools

import jax
import jax.numpy as jnp
import numpy as np
from jax import lax
from jax.experimental import pallas as pl
from jax.experimental.pallas import tpu as pltpu

F32 = jnp.float32
BF16 = jnp.bfloat16

GRID_W = 64
N_ATT_HEADS = 8
ATT_HEAD_DIM = 64
ATT_WIDTH = N_ATT_HEADS * ATT_HEAD_DIM
WIN_ROWS = 8
WIN_COLS = 16
HEAD_GROUP = 4
N_ML_HEADS = 4
ML_HEAD_DIM = 128
ML_WIDTH = N_ML_HEADS * ML_HEAD_DIM
ROPE_THETA = 10000.0
EPS = 1e-6

LANES = 128
MXU_WIDTH = 256
ML_CHUNK = 256
GATE_ROWS = 8
HALO = 16
TOKEN_TILE = 512
VMEM_LIMIT = 56 * 1024 * 1024


def _dot(a, b):
    return jnp.dot(a, b, preferred_element_type=F32)


def _dot_nt(a, b):
    return lax.dot_general(a, b, (((1,), (1,)), ((), ())), preferred_element_type=F32)


def _dot_tn(a, b):
    return lax.dot_general(a, b, (((0,), (0,)), ((), ())), preferred_element_type=F32)


def _const_spec(shape):
    nd = len(shape)
    return pl.BlockSpec(shape, lambda *_: (0,) * nd, pipeline_mode=pl.Buffered(1))


def _params(n_axes):
    return pltpu.CompilerParams(dimension_semantics=("arbitrary",) * n_axes,
                                vmem_limit_bytes=VMEM_LIMIT)


def _token_tile(n_tok, seq_len):
    tm = min(TOKEN_TILE, n_tok)
    while n_tok % tm or (seq_len % tm and tm % seq_len):
        tm //= 2
    return tm


def _log_sigmoid(x):
    return jnp.minimum(x, 0.0) - jnp.log1p(jnp.exp(-jnp.abs(x)))


def _mod_kernel(c_ref, w_ref, b_ref, o_ref):
    c = c_ref[...]
    s = c * jax.nn.sigmoid(c)
    o_ref[...] = _dot(s.astype(BF16), w_ref[...].astype(BF16)) + b_ref[...]


def _modulation(cvecs, w_mod, b_mod):
    r, d = cvecs.shape
    n = w_mod.shape[1]
    tn = d
    out = pl.pallas_call(
        _mod_kernel,
        grid=(n // tn,),
        in_specs=[pl.BlockSpec((r, d), lambda j: (0, 0)),
                  pl.BlockSpec((d, tn), lambda j: (0, j)),
                  pl.BlockSpec((1, tn), lambda j: (0, j))],
        out_specs=pl.BlockSpec((r, tn), lambda j: (0, j)),
        out_shape=jax.ShapeDtypeStruct((r, n), F32),
        compiler_params=_params(1),
        name="adaln_mod",
    )(cvecs, w_mod, b_mod.reshape(1, n))
    return out.reshape(r, 1, n)


def _rope(x, cos, sin_signed):
    lane = lax.broadcasted_iota(jnp.int32, x.shape, 1)
    partner = jnp.where((lane & 32) == 0, pltpu.roll(x, LANES - 32, 1), pltpu.roll(x, 32, 1))
    return x * cos + partner * sin_signed


def _inproj_kernel(*refs, latent):
    if latent:
        (x_ref, mod_ref, g1_ref, win_ref, wg_ref, bg_ref, qg_ref, kg_ref, bd_ref, cos_ref, sin_ref,
         aq_ref, ak_ref, av_ref, mq_ref, mk_ref, mv_ref, mo_ref, gtt_ref) = refs
    else:
        (x_ref, mod_ref, g1_ref, win_ref, wg_ref, bg_ref, qg_ref, kg_ref, bd_ref,
         aq_ref, ak_ref, av_ref, mq_ref, mk_ref, mv_ref, mo_ref, gtt_ref, ck_ref, cv_ref,
         cks_ref, cvs_ref) = refs
    x = x_ref[...]
    d = x.shape[-1]
    mod = mod_ref[0]
    sh1, sc1 = mod[:, 0:d], mod[:, d:2 * d]
    y = x * lax.rsqrt(jnp.mean(x * x, axis=-1, keepdims=True) + EPS)
    h = (y * g1_ref[...]) * (1.0 + sc1) + sh1
    hb = h.astype(BF16)

    def head_norm(a, g):
        ss = _dot((a * a).astype(BF16), bd_ref[...])
        return a * lax.rsqrt(ss * (1.0 / ATT_HEAD_DIM) + EPS) * g

    w = ATT_WIDTH
    att = _dot(hb, win_ref[:, 0:3 * w])
    aq_ref[...] = (head_norm(att[:, 0:w], qg_ref[...]) * ATT_HEAD_DIM ** -0.5).astype(aq_ref.dtype)
    kn = head_norm(att[:, w:2 * w], kg_ref[...])
    av = att[:, 2 * w:3 * w]
    ak_ref[...] = kn.astype(BF16)
    if latent:
        av_ref[...] = av.T.astype(BF16)
    else:
        av_ref[...] = av.astype(BF16)
        tm = kn.shape[0]
        for src, dst, scr in ((kn, ck_ref, cks_ref), (av, cv_ref, cvs_ref)):
            for hh in range(N_ATT_HEADS):
                pair = src[:, (hh // 2) * LANES:(hh // 2 + 1) * LANES]
                if hh % 2:
                    pair = pltpu.roll(pair, ATT_HEAD_DIM, 1)
                scr[pl.ds(hh, tm, stride=N_ATT_HEADS), :] = pair
            dst[...] = scr[...].reshape(tm, N_ATT_HEADS, LANES)[:, :, 0:ATT_HEAD_DIM]

    w = ML_WIDTH
    ml0 = 3 * ATT_WIDTH
    mq = _dot(hb, win_ref[:, ml0:ml0 + w])
    mk = _dot(hb, win_ref[:, ml0 + w:ml0 + 2 * w]) * ML_HEAD_DIM ** -0.5
    if latent:
        cos, sin = cos_ref[...], sin_ref[...]
        for hh in range(N_ML_HEADS):
            sl = slice(hh * ML_HEAD_DIM, (hh + 1) * ML_HEAD_DIM)
            mq_ref[:, sl] = _rope(mq[:, sl], cos, sin).astype(BF16)
            mk_ref[:, sl] = _rope(mk[:, sl], cos, sin).astype(BF16)
    else:
        mq_ref[...] = mq.astype(BF16)
        mk_ref[...] = mk.astype(BF16)
    mv_ref[...] = _dot(hb, win_ref[:, ml0 + 2 * w:ml0 + 3 * w]).T.astype(BF16)
    mo_ref[...] = _dot(hb, win_ref[:, ml0 + 3 * w:ml0 + 4 * w]).T
    gates = _dot(hb, wg_ref[...]) + bg_ref[...]
    gtt_ref[...] = gates.T[0:2 * GATE_ROWS]


def _in_projection_parts(x, mod, mod_row, seq_len, wts, rope, *, latent):
    n_tok, d = x.shape
    tm = _token_tile(n_tok, seq_len)
    tiles_per_seq = max(seq_len // tm, 1)

    def tok(width):
        return pl.BlockSpec((tm, width), lambda i: (i, 0))

    in_specs = [tok(d),
                pl.BlockSpec((1, 1, mod.shape[-1]), lambda i: (mod_row(i, tm), 0, 0)),
                _const_spec((1, d)),
                _const_spec(wts["w_in"].shape),
                _const_spec(wts["w_g"].shape), _const_spec((1, LANES)),
                _const_spec((1, ATT_WIDTH)), _const_spec((1, ATT_WIDTH)),
                _const_spec((ATT_WIDTH, ATT_WIDTH))]
    args = [x, mod, wts["norm1_g"], wts["w_in"], wts["w_g"], wts["b_g"],
            wts["q_g"], wts["k_g"], wts["blockdiag"]]
    if latent:
        in_specs += [pl.BlockSpec((tm, LANES), lambda i: (i % tiles_per_seq, 0))] * 2
        args += [rope[0], rope[1]]
    out_shape = [jax.ShapeDtypeStruct((n_tok, ATT_WIDTH), BF16),
                 jax.ShapeDtypeStruct((n_tok, ATT_WIDTH), BF16),
                 jax.ShapeDtypeStruct((ATT_WIDTH, n_tok) if latent else (n_tok, ATT_WIDTH), BF16),
                 jax.ShapeDtypeStruct((n_tok, ML_WIDTH), BF16),
                 jax.ShapeDtypeStruct((n_tok, ML_WIDTH), BF16),
                 jax.ShapeDtypeStruct((ML_WIDTH, n_tok), BF16),
                 jax.ShapeDtypeStruct((ML_WIDTH, n_tok), F32),
                 jax.ShapeDtypeStruct((2 * GATE_ROWS, n_tok), F32)]

    def tok_t(height):
        return pl.BlockSpec((height, tm), lambda i: (0, i))

    av_spec = tok_t(ATT_WIDTH) if latent else tok(ATT_WIDTH)
    out_specs = ([tok(ATT_WIDTH)] * 2 + [av_spec] + [tok(ML_WIDTH)] * 2 + [tok_t(ML_WIDTH)] * 2
                 + [tok_t(2 * GATE_ROWS)])
    if not latent:
        cache = jax.ShapeDtypeStruct((n_tok, N_ATT_HEADS, ATT_HEAD_DIM), F32)
        out_shape += [cache, cache]
        out_specs += [pl.BlockSpec((tm, N_ATT_HEADS, ATT_HEAD_DIM), lambda i: (i, 0, 0))] * 2
    scratch = [] if latent else [pltpu.VMEM((tm * N_ATT_HEADS, LANES), F32)] * 2
    return n_tok // tm, in_specs, args, out_shape, out_specs, scratch


def _in_projection(x, mod, mod_row, seq_len, wts, rope, *, latent):
    n_tiles, in_specs, args, out_shape, out_specs, scratch = _in_projection_parts(
        x, mod, mod_row, seq_len, wts, rope, latent=latent)
    return pl.pallas_call(
        functools.partial(_inproj_kernel, latent=latent),
        grid=(n_tiles,),
        in_specs=in_specs,
        out_specs=out_specs,
        out_shape=out_shape,
        scratch_shapes=scratch,
        compiler_params=_params(1),
        name="in_proj_latent" if latent else "in_proj_ctx",
    )(*args)


def _ctx_attn_kernel(q_ref, k_ref, v_ref, o_ref):
    n = q_ref.shape[0]
    gw = HEAD_GROUP * ATT_HEAD_DIM
    lane_head = lax.broadcasted_iota(jnp.int32, (n, gw), 1) // ATT_HEAD_DIM
    for g in range(N_ATT_HEADS // HEAD_GROUP):
        sl = slice(g * gw, (g + 1) * gw)
        q4 = q_ref[:, sl]
        qbd = jnp.concatenate([jnp.where(lane_head == hl, q4, jnp.zeros_like(q4))
                               for hl in range(HEAD_GROUP)], axis=0)
        s = _dot_nt(k_ref[:, sl], qbd)
        p = jnp.exp(s - jnp.max(s, axis=0, keepdims=True))
        p = p / jnp.sum(p, axis=0, keepdims=True)
        o4 = _dot_tn(p.astype(BF16), v_ref[:, sl])
        out = jnp.where(lane_head == 0, o4[0:n], 0.0)
        for hl in range(1, HEAD_GROUP):
            out = out + jnp.where(lane_head == hl, o4[hl * n:(hl + 1) * n], 0.0)
        o_ref[:, sl] = out.astype(o_ref.dtype)


def _context_attention(q, k, v, seq_len):
    n_tok = q.shape[0]
    spec = pl.BlockSpec((seq_len, ATT_WIDTH), lambda b: (b, 0))
    return pl.pallas_call(
        _ctx_attn_kernel,
        grid=(n_tok // seq_len,),
        in_specs=[spec, spec, spec],
        out_specs=spec,
        out_shape=jax.ShapeDtypeStruct((n_tok, ATT_WIDTH), BF16),
        compiler_params=_params(1),
        name="ctx_attn",
    )(q, k, v)


def _nbr_attn_kernel(q_ref, k_ref, vt_ref, kc_ref, vct_ref, bias_ref, o_ref, pt_ref, *, rows, kr):
    r = pl.program_id(1)
    n_win = kr + 2
    n_loc = kr * GRID_W
    gw = HEAD_GROUP * ATT_HEAD_DIM
    groups = N_ATT_HEADS // HEAD_GROUP
    lane_head = lax.broadcasted_iota(jnp.int32, (GRID_W, gw), 1) // ATT_HEAD_DIM
    rs = jnp.clip(r - kr // 2, 0, rows - kr)
    start = jnp.minimum(rs - (rs & 1), rows - n_win)
    delta = rs - start
    bias0 = pl.multiple_of((WIN_ROWS - 1 - (r - rs)) * GRID_W, GRID_W)
    zeros2 = jnp.zeros((2 * GRID_W, gw), BF16)
    for g in range(groups):
        sl = slice(g * gw, (g + 1) * gw)
        q4 = q_ref[:, sl]
        qbd = jnp.concatenate([jnp.where(lane_head == hl, q4, jnp.zeros_like(q4))
                               for hl in range(HEAD_GROUP)], axis=0)
        s_loc = _dot_nt(k_ref[pl.ds(pl.multiple_of(rs * GRID_W, GRID_W), n_loc), sl], qbd)
        s_loc = s_loc + bias_ref[g, pl.ds(bias0, n_loc), :]
        s_ctx = _dot_nt(kc_ref[0, :, sl], qbd)
        m = jnp.maximum(jnp.max(s_loc, axis=0, keepdims=True), jnp.max(s_ctx, axis=0, keepdims=True))
        p_loc = jnp.exp(s_loc - m)
        p_ctx = jnp.exp(s_ctx - m)
        l = jnp.sum(p_loc, axis=0, keepdims=True) + jnp.sum(p_ctx, axis=0, keepdims=True)
        pt_ref[g, 0:2 * GRID_W, :] = zeros2
        pt_ref[g, n_loc:n_loc + 2 * GRID_W, :] = zeros2
        pt_ref[g, pl.ds(pl.multiple_of(delta * GRID_W, GRID_W), n_loc), :] = p_loc.astype(BF16)
        vt_win = vt_ref[sl, pl.ds(pl.multiple_of(start * GRID_W, 2 * GRID_W), n_win * GRID_W)]
        ot = _dot(vt_win, pt_ref[g]) + _dot(vct_ref[0, sl, :], p_ctx.astype(BF16))
        o4 = (ot / l).T
        out = jnp.where(lane_head == 0, o4[0:GRID_W], 0.0)
        for hl in range(1, HEAD_GROUP):
            out = out + jnp.where(lane_head == hl, o4[hl * GRID_W:(hl + 1) * GRID_W], 0.0)
        o_ref[:, sl] = out.astype(o_ref.dtype)


def _nbr_bias_table(rpb, rows):
    col = jnp.arange(GRID_W)
    cs = jnp.clip(col - WIN_COLS // 2, 0, GRID_W - WIN_COLS)
    in_win = (col[None, :] >= cs[:, None]) & (col[None, :] < cs[:, None] + WIN_COLS)
    dc_idx = jnp.clip(col[None, :] - col[:, None] + (WIN_COLS - 1), 0, 2 * WIN_COLS - 2)
    n_dr, n_dc = rpb.shape[1], rpb.shape[2]
    onehot = (dc_idx[None] == jnp.arange(n_dc)[:, None, None]).astype(F32)
    t = jnp.einsum("hrc,cqk->hrqk", rpb.astype(F32), onehot, precision=lax.Precision.HIGHEST)
    t = jnp.where(in_win[None, None], t, -jnp.inf)
    n_groups = N_ATT_HEADS // HEAD_GROUP
    t = t.reshape(n_groups, HEAD_GROUP, n_dr, GRID_W, GRID_W).transpose(0, 2, 4, 1, 3)
    return t.reshape(n_groups, n_dr * GRID_W, HEAD_GROUP * GRID_W)


def _neighborhood_attention(q, k, vt, k_ctx, vt_ctx, rpb, n_batch, seq_len):
    rows = seq_len // GRID_W
    kr = min(WIN_ROWS, rows)
    n_win = kr + 2
    assert rows >= n_win and (rows - n_win) % 2 == 0
    bias = _nbr_bias_table(rpb, rows)
    past = k_ctx.shape[1]
    gw = HEAD_GROUP * ATT_HEAD_DIM
    n_groups = N_ATT_HEADS // HEAD_GROUP
    row_spec = pl.BlockSpec((GRID_W, ATT_WIDTH), lambda b, r: (b * rows + r, 0))
    return pl.pallas_call(
        functools.partial(_nbr_attn_kernel, rows=rows, kr=kr),
        grid=(n_batch, rows),
        in_specs=[row_spec,
                  pl.BlockSpec((seq_len, ATT_WIDTH), lambda b, r: (b, 0)),
                  pl.BlockSpec((ATT_WIDTH, seq_len), lambda b, r: (0, b)),
                  pl.BlockSpec((1, past, ATT_WIDTH), lambda b, r: (b, 0, 0)),
                  pl.BlockSpec((1, ATT_WIDTH, past), lambda b, r: (b, 0, 0)),
                  _const_spec(bias.shape)],
        out_specs=row_spec,
        out_shape=jax.ShapeDtypeStruct((n_batch * seq_len, ATT_WIDTH), BF16),
        scratch_shapes=[pltpu.VMEM((n_groups, n_win * GRID_W, gw), BF16)],
        compiler_params=_params(2),
        name="nbr_attn",
    )(q, k, vt, k_ctx, vt_ctx, bias)


def _mlstm_kernel(*refs, chunk, n_chunks, has_state):
    dh = ML_HEAD_DIM
    nrep = 2 * GATE_ROWS
    if has_state:
        (q_ref, k_ref, vt_ref, mot_ref, gtt_ref, g_ref, c0_ref, n0_ref, m0_ref,
         o_ref, c_ref, n_ref, m_ref, hf_ref, hb_ref, cn_ref, mrun_ref) = refs
        for d in range(2):
            for hh in range(N_ML_HEADS):
                cn_ref[d, hh, 0:dh, :] = c0_ref[0, d, hh].T
                cn_ref[d, hh, dh:dh + nrep, :] = jnp.broadcast_to(n0_ref[0, d, hh], (nrep, dh))
        mrun_ref[...] = m0_ref[0]
    else:
        (q_ref, k_ref, vt_ref, mot_ref, gtt_ref, g_ref,
         o_ref, c_ref, n_ref, m_ref, hf_ref, hb_ref, cn_ref, mrun_ref) = refs
        cn_ref[...] = jnp.zeros_like(cn_ref)
        mrun_ref[...] = jnp.zeros_like(mrun_ref)
    use_state = has_state or n_chunks > 1

    i0 = lax.broadcasted_iota(jnp.int32, (chunk, chunk), 0)
    i1 = lax.broadcasted_iota(jnp.int32, (chunk, chunk), 1)
    row_id = lax.broadcasted_iota(jnp.int32, (GATE_ROWS, 1), 0)
    instances = [(hh, d) for d in range(2) for hh in range(N_ML_HEADS)]

    def split3(x):
        hi = x.astype(BF16)
        r1 = x - hi.astype(F32)
        mid = r1.astype(BF16)
        return hi, mid, (r1 - mid.astype(F32)).astype(BF16)

    def body(i, _):
        chunks = (i, n_chunks - 1 - i)
        rows = [pl.ds(pl.multiple_of(c * chunk, chunk), chunk) for c in chunks]
        le = [i0 <= i1, i0 >= i1]
        m_prev = mrun_ref[...][:, 0:1]

        def qkv(hh, d):
            sl = slice(hh * dh, (hh + 1) * dh)
            return q_ref[rows[d], sl], k_ref[rows[d], sl], vt_ref[sl, rows[d]]

        st = [_dot_nt(qkv(hh, d)[1], qkv(hh, d)[0]) for hh, d in instances]
        if use_state:
            cn = [cn_ref[d, hh] for hh, d in instances]
            qct = [_dot_nt(cn[n].astype(BF16), qkv(hh, d)[0]) for n, (hh, d) in enumerate(instances)]

        a_col, a8, cs8, w8, decay8, m_new = [], [], [], [], [], []
        for d in range(2):
            gi_r = gtt_ref[0:GATE_ROWS, rows[d]]
            lf_r = _log_sigmoid(gtt_ref[GATE_ROWS:2 * GATE_ROWS, rows[d]])
            tri_r = jnp.where(le[d], 1.0, 0.0).astype(BF16)
            cs_r = sum(_dot(t, tri_r) for t in split3(lf_r))
            b_last = jnp.sum(lf_r, axis=1, keepdims=True)
            w_end = b_last + gi_r - cs_r
            m_d = jnp.maximum(b_last + m_prev, jnp.max(w_end, axis=1, keepdims=True))
            a8.append(gi_r - cs_r)
            cs8.append(cs_r)
            decay8.append(jnp.exp(b_last + m_prev - m_d))
            w8.append(jnp.exp(w_end - m_d))
            m_new.append(m_d)
            pad = jnp.zeros((LANES - GATE_ROWS, chunk), F32)
            a_col.append(jnp.concatenate([gi_r - cs_r, pad], axis=0).T)
        m_next = jnp.where((row_id & 1) == 0, m_new[0], m_new[1])
        mrun_ref[...] = jnp.broadcast_to(m_next, (GATE_ROWS, LANES))

        pt_all, g_all, den_all = [], [], []
        for n, (hh, d) in enumerate(instances):
            j = 2 * hh + d
            a = jnp.where(le[d], a_col[d][:, j:j + 1], -jnp.inf)
            g = jnp.maximum(jnp.max(a, axis=0, keepdims=True), m_prev[j:j + 1])
            pt = st[n] * jnp.exp(a - g)
            pt_all.append(pt)
            g_all.append(g)
            den_all.append(jnp.sum(pt, axis=0, keepdims=True))

        for n, (hh, d) in enumerate(instances):
            j = 2 * hh + d
            sl = slice(hh * dh, (hh + 1) * dh)
            _, k, vt = qkv(hh, d)
            g, den = g_all[n], den_all[n]
            num = _dot(vt, pt_all[n].astype(BF16))
            if use_state:
                w_inter = jnp.exp(m_prev[j:j + 1] - g)
                num = num + w_inter * qct[n][0:dh]
                den = den + w_inter * qct[n][dh:dh + 1]
            scale = 1.0 / jnp.maximum(jnp.abs(den), jnp.exp(-(cs8[d][j:j + 1] + g)))
            (hb_ref if d else hf_ref)[sl, rows[d]] = num * scale
            w_row = w8[d][j:j + 1]
            lhs = jnp.concatenate([(vt.astype(F32) * w_row).astype(BF16),
                                   jnp.broadcast_to(w_row, (nrep, chunk)).astype(BF16)], axis=0)
            upd = _dot(lhs, k)
            if use_state:
                upd = decay8[d][j:j + 1] * cn[n] + upd
            cn_ref[d, hh] = upd
        return 0

    def finish(c, _):
        rows = pl.ds(pl.multiple_of(c * chunk, chunk), chunk)
        for hh in range(N_ML_HEADS):
            sl = slice(hh * dh, (hh + 1) * dh)
            hs = hf_ref[sl, rows] + hb_ref[sl, rows]
            y = hs * lax.rsqrt(jnp.mean(hs * hs, axis=0, keepdims=True) + EPS) * g_ref[sl, :]
            y = y * jax.nn.sigmoid(mot_ref[sl, rows])
            o_ref[rows, sl] = y.T.astype(o_ref.dtype)
        return 0

    if n_chunks == 1:
        body(0, 0)
        finish(0, 0)
    else:
        lax.fori_loop(0, n_chunks, body, 0)
        lax.fori_loop(0, n_chunks, finish, 0)
    for d in range(2):
        for hh in range(N_ML_HEADS):
            cn = cn_ref[d, hh]
            c_ref[0, d, hh] = cn[0:dh].T
            n_ref[0, d, hh] = cn[dh:dh + 1]
    m_ref[0] = mrun_ref[...]


def _mlstm(mq, mk, mvt, mot, gates_t, ml_g, state, n_batch, seq_len, chunk):
    dh = ML_HEAD_DIM
    nh = N_ML_HEADS
    width = nh * dh
    has_state = state is not None
    seq = pl.BlockSpec((seq_len, width), lambda b: (b, 0))
    seq_t = pl.BlockSpec((width, seq_len), lambda b: (0, b))
    st_c = pl.BlockSpec((1, 2, nh, dh, dh), lambda b: (b, 0, 0, 0, 0))
    st_n = pl.BlockSpec((1, 2, nh, 1, dh), lambda b: (b, 0, 0, 0, 0))
    st_m = pl.BlockSpec((1, GATE_ROWS, LANES), lambda b: (b, 0, 0))
    in_specs = [seq, seq, seq_t, seq_t,
                pl.BlockSpec((2 * GATE_ROWS, seq_len), lambda b: (0, b)),
                _const_spec((width, chunk))]
    args = [mq, mk, mvt, mot, gates_t, jnp.broadcast_to(ml_g.reshape(width, 1), (width, chunk))]
    if has_state:
        in_specs += [st_c, st_n, st_m]
        args += list(state)
    out_shape = [jax.ShapeDtypeStruct((n_batch * seq_len, width), BF16),
                 jax.ShapeDtypeStruct((n_batch, 2, nh, dh, dh), F32),
                 jax.ShapeDtypeStruct((n_batch, 2, nh, 1, dh), F32),
                 jax.ShapeDtypeStruct((n_batch, GATE_ROWS, LANES), F32)]
    return pl.pallas_call(
        functools.partial(_mlstm_kernel, chunk=chunk, n_chunks=seq_len // chunk, has_state=has_state),
        grid=(n_batch,),
        in_specs=in_specs,
        out_specs=[seq, st_c, st_n, st_m],
        out_shape=out_shape,
        scratch_shapes=[pltpu.VMEM((width, seq_len), F32), pltpu.VMEM((width, seq_len), F32),
                        pltpu.VMEM((2, nh, dh + 2 * GATE_ROWS, dh), F32), pltpu.VMEM((GATE_ROWS, LANES), F32)],
        compiler_params=_params(1),
        name="mlstm_latent" if has_state else "mlstm_ctx",
    )(*args)


def _pack_gate_rows(m):
    b = m.shape[0]
    packed = m.transpose(0, 2, 1).reshape(b, 2 * N_ML_HEADS, 1)
    return jnp.broadcast_to(packed, (b, GATE_ROWS, LANES))


def _unpack_gate_rows(m):
    b = m.shape[0]
    return m[:, :, 0].reshape(b, N_ML_HEADS, 2).transpose(0, 2, 1)


def _inproj_ctxmix_kernel(*refs, n_in, n_out, n_seq, seq_len, chunk):
    ip_in = refs[:n_in]
    q_ref, k_ref, v_ref, mq_ref, mk_ref, mvt_ref, mot_ref, gtt_ref, g_ref = refs[n_in:n_in + 9]
    outs = refs[n_in + 9:]
    ip_out = outs[:n_out]
    oatt_ref, oml_ref, c_ref, n_ref, m_ref = outs[n_out:n_out + 5]
    scratch = outs[n_out + 5:]
    _inproj_kernel(*ip_in, *ip_out, latent=True)
    for s in range(n_seq):
        rows = slice(s * seq_len, (s + 1) * seq_len)
        _ctx_attn_kernel(q_ref.at[rows, :], k_ref.at[rows, :], v_ref.at[rows, :], oatt_ref.at[rows, :])
        _mlstm_kernel(mq_ref.at[rows, :], mk_ref.at[rows, :], mvt_ref.at[:, rows], mot_ref.at[:, rows],
                      gtt_ref.at[:, rows], g_ref, oml_ref.at[rows, :], c_ref.at[s:s + 1], n_ref.at[s:s + 1],
                      m_ref.at[s:s + 1], *scratch[4 * s:4 * s + 4],
                      chunk=chunk, n_chunks=seq_len // chunk, has_state=False)


def _latent_in_projection_with_ctx_mixers(x, mod, mod_row, seq_len, wts, rope, ctx, n_ctx_batch, ctx_seq):
    n_tiles, in_specs, args, out_shape, out_specs, scratch = _in_projection_parts(
        x, mod, mod_row, seq_len, wts, rope, latent=True)
    n_seq = n_ctx_batch // n_tiles
    rows = n_seq * ctx_seq
    n_ctx_tok = n_ctx_batch * ctx_seq
    chunk = min(ML_CHUNK, ctx_seq)
    nh, dh = N_ML_HEADS, ML_HEAD_DIM

    def tok(width):
        return pl.BlockSpec((rows, width), lambda i: (i, 0))

    def tok_t(height):
        return pl.BlockSpec((height, rows), lambda i: (0, i))

    def per_seq(*tail):
        return pl.BlockSpec((n_seq,) + tail, lambda i: (i,) + (0,) * len(tail))

    aq, ak, av, mq, mk, mvt, mot, gtt = ctx
    mix_in_specs = ([tok(ATT_WIDTH)] * 3 + [tok(ML_WIDTH)] * 2 + [tok_t(ML_WIDTH)] * 2
                    + [tok_t(2 * GATE_ROWS), _const_spec((ML_WIDTH, chunk))])
    mix_args = [aq, ak, av, mq, mk, mvt, mot, gtt,
                jnp.broadcast_to(wts["ml_g"].reshape(ML_WIDTH, 1), (ML_WIDTH, chunk))]
    mix_out_shape = [jax.ShapeDtypeStruct((n_ctx_tok, ATT_WIDTH), BF16),
                     jax.ShapeDtypeStruct((n_ctx_tok, ML_WIDTH), BF16),
                     jax.ShapeDtypeStruct((n_ctx_batch, 2, nh, dh, dh), F32),
                     jax.ShapeDtypeStruct((n_ctx_batch, 2, nh, 1, dh), F32),
                     jax.ShapeDtypeStruct((n_ctx_batch, GATE_ROWS, LANES), F32)]
    mix_out_specs = [tok(ATT_WIDTH), tok(ML_WIDTH), per_seq(2, nh, dh, dh), per_seq(2, nh, 1, dh),
                     per_seq(GATE_ROWS, LANES)]
    mix_scratch = [pltpu.VMEM((ML_WIDTH, ctx_seq), F32), pltpu.VMEM((ML_WIDTH, ctx_seq), F32),
                   pltpu.VMEM((2, nh, dh + 2 * GATE_ROWS, dh), F32), pltpu.VMEM((GATE_ROWS, LANES), F32)] * n_seq
    outs = pl.pallas_call(
        functools.partial(_inproj_ctxmix_kernel, n_in=len(in_specs), n_out=len(out_specs), n_seq=n_seq,
                          seq_len=ctx_seq, chunk=chunk),
        grid=(n_tiles,),
        in_specs=in_specs + mix_in_specs,
        out_specs=out_specs + mix_out_specs,
        out_shape=out_shape + mix_out_shape,
        scratch_shapes=scratch + mix_scratch,
        compiler_params=_params(1),
        name="in_proj_latent_ctx_mixers",
    )(*args, *mix_args)
    return outs[:len(out_specs)], outs[len(out_specs):]


def _ffn_kernel(oa_ref, oap_ref, oan_ref, om_ref, omp_ref, omn_ref, x_ref, xp_ref, xn_ref, mod_ref, g2_ref,
                wout_ref, wup_ref, cw_ref, cb_ref, wd_ref,
                y_ref, oc_ref, lhs_ref, x1_ref, ug_ref, uv_ref, acc_ref, *, seq_len, n_col_tiles):
    tm, d = x_ref.shape
    aw = oa_ref.shape[1]

    def halo(next_ref, prev_ref):
        row = lax.broadcasted_iota(jnp.int32, next_ref.shape, 0)
        return jnp.where(row < HALO // 2, next_ref[...], prev_ref[...])

    oc_ref[0:tm, 0:aw] = oa_ref[...]
    oc_ref[0:tm, aw:] = om_ref[...]
    oc_ref[tm:tm + HALO, 0:aw] = halo(oan_ref, oap_ref)
    oc_ref[tm:tm + HALO, aw:] = halo(omn_ref, omp_ref)
    mod = mod_ref[0]
    g1 = mod[:, 2 * d:3 * d]
    sh2, sc2 = mod[:, 3 * d:4 * d], mod[:, 4 * d:5 * d]
    out = _dot(oc_ref[...], wout_ref[...])

    def norm2(x1):
        y = x1 * lax.rsqrt(jnp.mean(x1 * x1, axis=-1, keepdims=True) + EPS)
        return ((y * g2_ref[...]) * (1.0 + sc2) + sh2).astype(BF16)

    x1 = x_ref[...] + g1 * out[0:tm]
    x1_ref[...] = x1
    lhs_ref[0:tm, :] = norm2(x1)
    lhs_ref[tm:tm + HALO, :] = norm2(halo(xn_ref, xp_ref) + g1 * out[tm:tm + HALO])
    acc_ref[...] = jnp.zeros_like(acc_ref)
    sub = 8
    period = min(seq_len, tm)
    first_groups = sorted({r // sub for r in range(0, tm, period)})
    last_groups = sorted({(r + period - 1) // sub for r in range(0, tm, period)})
    tile0 = pl.program_id(0) * tm

    def zero_rows(x, groups, target):
        parts, at = [], 0
        for grp in groups:
            lo = grp * sub
            if lo > at:
                parts.append(x[at:lo])
            pos = (tile0 + lo + lax.broadcasted_iota(jnp.int32, (sub, 1), 0)) % seq_len
            parts.append(jnp.where(pos == target, 0.0, x[lo:lo + sub]))
            at = lo + sub
        if at < tm:
            parts.append(x[at:tm])
        return jnp.concatenate(parts, axis=0)

    def conv(u, cw, cb):
        prev = zero_rows(pltpu.roll(u, 1, 0)[0:tm], first_groups, 0)
        nxt = zero_rows(pltpu.roll(u, tm + HALO - 1, 0)[0:tm], last_groups, seq_len - 1)
        return prev * cw[0:1] + u[0:tm] * cw[1:2] + nxt * cw[2:3] + cb

    def up(j, slot):
        lhs = lhs_ref[...]
        ug_ref[slot] = _dot(lhs, wup_ref[j])
        uv_ref[slot] = _dot(lhs, wup_ref[n_col_tiles + j])

    def act(j, slot):
        gate = conv(ug_ref[slot], cw_ref[j], cb_ref[j])
        val = conv(uv_ref[slot], cw_ref[n_col_tiles + j], cb_ref[n_col_tiles + j])
        return ((gate * jax.nn.sigmoid(gate)) * val).astype(BF16)

    def stage(j, slot, next_up):
        a = act(j, slot)
        if next_up:
            up(j + 1, 1 - slot)
        return _dot(a, wd_ref[j])

    def pair(i, _):
        acc_ref[...] += stage(2 * i, 0, True)
        acc_ref[...] += stage(2 * i + 1, 1, True)
        return 0

    up(0, 0)
    n_pairs = (n_col_tiles - 1) // 2
    if n_pairs:
        lax.fori_loop(0, n_pairs, pair, 0)
    j0 = 2 * n_pairs
    if n_col_tiles - j0 == 2:
        acc_ref[...] += stage(j0, 0, True)
        tail = stage(j0 + 1, 1, False)
    else:
        tail = stage(j0, 0, False)
    g2 = mod_ref[0][:, 5 * d:6 * d]
    y_ref[...] = x1_ref[...] + g2 * (acc_ref[...] + tail)


def _mix_ffn(o_att, o_ml, x, mod, mod_row, seq_len, wts):
    n_tok, d = x.shape
    tm = _token_tile(n_tok, seq_len)
    hpt = tm // HALO
    n_halo = n_tok // HALO
    nct, tn, _ = wts["w_down"].shape
    mix = o_att.shape[1] + o_ml.shape[1]

    def tile_and_halos(width):
        return [pl.BlockSpec((tm, width), lambda i: (i, 0)),
                pl.BlockSpec((HALO, width), lambda i: (jnp.maximum(i * hpt - 1, 0), 0)),
                pl.BlockSpec((HALO, width), lambda i: (jnp.minimum((i + 1) * hpt, n_halo - 1), 0))]

    in_specs = (tile_and_halos(o_att.shape[1]) + tile_and_halos(o_ml.shape[1]) + tile_and_halos(d) + [
        pl.BlockSpec((1, 1, mod.shape[-1]), lambda i: (mod_row(i, tm), 0, 0)),
        _const_spec((1, d)), _const_spec((mix, d)),
        _const_spec((2 * nct, d, tn)), _const_spec((2 * nct, 3, tn)), _const_spec((2 * nct, 1, tn)),
        _const_spec((nct, tn, d))])
    return pl.pallas_call(
        functools.partial(_ffn_kernel, seq_len=seq_len, n_col_tiles=nct),
        grid=(n_tok // tm,),
        in_specs=in_specs,
        out_specs=pl.BlockSpec((tm, d), lambda i: (i, 0)),
        out_shape=jax.ShapeDtypeStruct((n_tok, d), F32),
        scratch_shapes=[pltpu.VMEM((tm + HALO, mix), BF16), pltpu.VMEM((tm + HALO, d), BF16),
                        pltpu.VMEM((tm, d), F32),
                        pltpu.VMEM((2, tm + HALO, tn), F32), pltpu.VMEM((2, tm + HALO, tn), F32),
                        pltpu.VMEM((tm, d), F32)],
        compiler_params=_params(1),
        name="mix_ffn",
    )(o_att, o_att, o_att, o_ml, o_ml, o_ml, x, x, x, mod, wts["norm2_g"], wts["w_out"],
      wts["w_up"], wts["conv_w"], wts["conv_b"], wts["w_down"])


def _retile_cast_kernel(w_ref, o_ref):
    tn = o_ref.shape[-1]
    for t in range(o_ref.shape[0]):
        o_ref[t] = w_ref[:, t * tn:(t + 1) * tn].astype(o_ref.dtype)


def _retile_cast(w, tn):
    r, c = w.shape
    group = 2 if (c // tn) % 2 == 0 else 1
    return pl.pallas_call(
        _retile_cast_kernel,
        grid=(c // (tn * group),),
        in_specs=[pl.BlockSpec((r, tn * group), lambda j: (0, j))],
        out_specs=pl.BlockSpec((group, r, tn), lambda j: (j, 0, 0)),
        out_shape=jax.ShapeDtypeStruct((c // tn, r, tn), BF16),
        compiler_params=_params(1),
        name="retile_cast",
    )(w)


def _cast_kernel(w_ref, o_ref):
    o_ref[...] = w_ref[...].astype(o_ref.dtype)


def _cast_rows(w, block_rows):
    r, c = w.shape
    spec = pl.BlockSpec((block_rows, c), lambda i: (i, 0))
    return pl.pallas_call(
        _cast_kernel,
        grid=(r // block_rows,),
        in_specs=[spec],
        out_specs=spec,
        out_shape=jax.ShapeDtypeStruct((r, c), BF16),
        compiler_params=_params(1),
        name="cast_rows",
    )(w)


def _layer_weights(norm1_g, norm2_g, w_in, b_gate, q_g, k_g, ml_g, w_out, w_up, conv_w, conv_b, w_down):
    d = w_in.shape[0]
    a, m, nh = ATT_WIDTH, ML_WIDTH, N_ML_HEADS
    def gate_lanes(g):
        g = g.reshape(g.shape[0], 2, 2, nh).transpose(0, 2, 3, 1).reshape(g.shape[0], 4 * nh)
        return jnp.pad(g, ((0, 0), (0, LANES - 4 * nh)))

    w_g = gate_lanes(w_in[:, 3 * a + 4 * m:])
    b_g = gate_lanes(b_gate.astype(F32).reshape(1, 4 * nh))
    head_id = jnp.arange(a) // ATT_HEAD_DIM
    d_ff = w_down.shape[0]
    tn = MXU_WIDTH
    nct = d_ff // tn

    def col_tiles(w):
        return w.reshape(w.shape[0], 2 * nct, tn).transpose(1, 0, 2)

    return {
        "norm1_g": norm1_g.reshape(1, d), "norm2_g": norm2_g.reshape(1, d),
        "w_in": w_in.astype(BF16),
        "w_g": w_g.astype(BF16), "b_g": b_g,
        "q_g": jnp.tile(q_g, N_ATT_HEADS).reshape(1, a), "k_g": jnp.tile(k_g, N_ATT_HEADS).reshape(1, a),
        "blockdiag": (head_id[:, None] == head_id[None, :]).astype(BF16),
        "ml_g": ml_g.reshape(1, m),
        "w_out": w_out.astype(BF16),
        "w_up": _retile_cast(w_up, tn),
        "conv_w": col_tiles(conv_w.astype(F32)), "conv_b": col_tiles(conv_b.astype(F32).reshape(1, -1)),
        "w_down": _cast_rows(w_down, tn).reshape(nct, tn, w_down.shape[1]),
    }


def _rope_tables(seq_len):
    quarter = ML_HEAD_DIM // 4
    pos = np.arange(seq_len)
    inv_freq = ROPE_THETA ** (-np.arange(quarter, dtype=np.float64) / quarter)
    ang_r = (pos // GRID_W).astype(np.float64)[:, None] * inv_freq[None, :]
    ang_c = (pos % GRID_W).astype(np.float64)[:, None] * inv_freq[None, :]
    cos = np.concatenate([np.cos(ang_r)] * 2 + [np.cos(ang_c)] * 2, axis=-1)
    sin = np.concatenate([-np.sin(ang_r), np.sin(ang_r), -np.sin(ang_c), np.sin(ang_c)], axis=-1)
    return jnp.asarray(cos, F32), jnp.asarray(sin, F32)


def _layer(xp, xs, mod, wts, rope, ctx_kv, rpb, state, batch, seq, dec_batch, dec_seq):
    def ctx_row(i, tm):
        return 0 * i

    def lat_row(i, tm):
        return 1 + (i * tm) // dec_seq

    ctx = _in_projection(xp, mod, ctx_row, seq, wts, None, latent=False)
    cache_k, cache_v = ctx[8], ctx[9]
    n_lat_tiles = (dec_batch * dec_seq) // _token_tile(dec_batch * dec_seq, dec_seq)
    if batch % n_lat_tiles == 0:
        lat, (o_att_c, o_ml_c, c_f, n_f, m_f) = _latent_in_projection_with_ctx_mixers(
            xs, mod, lat_row, dec_seq, wts, rope, ctx[:8], batch, seq)
    else:
        aq, ak, av, mq, mk, mvt, mot, gtt = ctx[:8]
        o_att_c = _context_attention(aq, ak, av, seq)
        o_ml_c, c_f, n_f, m_f = _mlstm(mq, mk, mvt, mot, gtt, wts["ml_g"], None, batch, seq, min(ML_CHUNK, seq))
        lat = _in_projection(xs, mod, lat_row, dec_seq, wts, rope, latent=True)
    xp = _mix_ffn(o_att_c, o_ml_c, xp, mod, ctx_row, seq, wts)

    aq, ak, avt, mq, mk, mvt, mot, gtt = lat[:8]
    o_att = _neighborhood_attention(aq, ak, avt, ctx_kv[0], ctx_kv[1], rpb, dec_batch, dec_seq)
    o_ml = _mlstm(mq, mk, mvt, mot, gtt, wts["ml_g"], state, dec_batch, dec_seq, min(ML_CHUNK, dec_seq))[0]
    xs = _mix_ffn(o_att, o_ml, xs, mod, lat_row, dec_seq, wts)
    return xp, xs, cache_k, cache_v, (c_f, n_f[:, :, :, 0, :], _unpack_gate_rows(m_f))


def kernel(x_prompt, x_sample, cache_k, cache_v, state_C, state_n, state_m, c, c_ctx, w_mod, b_mod, norm1_g,
           norm2_g, w_in, b_gate, q_norm_g, k_norm_g, rpb, ml_norm_g, w_out, w_up, conv_w, conv_b, w_down):
    batch, seq, d = x_prompt.shape
    dec_batch, dec_seq, _ = x_sample.shape
    depth = w_mod.shape[0]
    past = cache_k.shape[2]
    cvecs = jnp.concatenate([c_ctx[None, :], c], axis=0).astype(F32)
    rope = _rope_tables(dec_seq)

    xp = x_prompt.reshape(batch * seq, d)
    xs = x_sample.reshape(dec_batch * dec_seq, d)
    ks, vs, cs, ns, ms = [], [], [], [], []
    for l in range(depth):
        wts = _layer_weights(norm1_g[l], norm2_g[l], w_in[l], b_gate[l], q_norm_g[l], k_norm_g[l],
                             ml_norm_g[l], w_out[l], w_up[l], conv_w[l], conv_b[l], w_down[l])
        mod = _modulation(cvecs, w_mod[l], b_mod[l])

        state = (state_C[:, l].astype(F32),
                 state_n[:, l].astype(F32)[:, :, :, None, :],
                 _pack_gate_rows(state_m[:, l].astype(F32)))
        ctx_kv = (cache_k[:, l].reshape(dec_batch, past, ATT_WIDTH).astype(BF16),
                  cache_v[:, l].reshape(dec_batch, past, ATT_WIDTH).transpose(0, 2, 1).astype(BF16))
        xp, xs, ak, av, st = _layer(xp, xs, mod, wts, rope, ctx_kv, rpb[l], state, batch, seq, dec_batch, dec_seq)
        ks.append(ak.reshape(batch, seq, N_ATT_HEADS, ATT_HEAD_DIM))
        vs.append(av.reshape(batch, seq, N_ATT_HEADS, ATT_HEAD_DIM))
        cs.append(st[0])
        ns.append(st[1])
        ms.append(st[2])
    return (xp.reshape(batch, seq, d), xs.reshape(dec_batch, dec_seq, d),
            jnp.stack(ks, axis=1), jnp.stack(vs, axis=1),
            jnp.stack(cs, axis=1), jnp.stack(ns, axis=1), jnp.stack(ms, axis=1))
```

```python
import functools

import jax
import jax.numpy as jnp
import numpy as np
from jax import lax
from jax.experimental import pallas as pl
from jax.experimental.pallas import tpu as pltpu

F32 = jnp.float32
BF16 = jnp.bfloat16

GRID_W = 64
N_ATT_HEADS = 8
ATT_HEAD_DIM = 64
ATT_WIDTH = N_ATT_HEADS * ATT_HEAD_DIM
WIN_ROWS = 8
WIN_COLS = 16
HEAD_GROUP = 4
N_ML_HEADS = 4
ML_HEAD_DIM = 128
ML_WIDTH = N_ML_HEADS * ML_HEAD_DIM
ROPE_THETA = 10000.0
EPS = 1e-6

LANES = 128
MXU_WIDTH = 256
ML_CHUNK = 256
GATE_ROWS = 8
HALO = 16
TOKEN_TILE = 512
VMEM_LIMIT = 56 * 1024 * 1024


def _dot(a, b):
    return jnp.dot(a, b, preferred_element_type=F32)


def _dot_nt(a, b):
    return lax.dot_general(a, b, (((1,), (1,)), ((), ())), preferred_element_type=F32)


def _dot_tn(a, b):
    return lax.dot_general(a, b, (((0,), (0,)), ((), ())), preferred_element_type=F32)


def _const_spec(shape):
    nd = len(shape)
    return pl.BlockSpec(shape, lambda *_: (0,) * nd, pipeline_mode=pl.Buffered(1))


def _params(n_axes):
    return pltpu.CompilerParams(dimension_semantics=("arbitrary",) * n_axes,
                                vmem_limit_bytes=VMEM_LIMIT)


def _token_tile(n_tok, seq_len):
    tm = min(TOKEN_TILE, n_tok)
    while n_tok % tm or (seq_len % tm and tm % seq_len):
        tm //= 2
    return tm


def _log_sigmoid(x):
    return jnp.minimum(x, 0.0) - jnp.log1p(jnp.exp(-jnp.abs(x)))


def _mod_kernel(c_ref, w_ref, b_ref, o_ref):
    c = c_ref[...]
    s = c * jax.nn.sigmoid(c)
    o_ref[...] = _dot(s.astype(BF16), w_ref[...].astype(BF16)) + b_ref[...]


def _modulation(cvecs, w_mod, b_mod):
    r, d = cvecs.shape
    n = w_mod.shape[1]
    tn = d
    out = pl.pallas_call(
        _mod_kernel,
        grid=(n // tn,),
        in_specs=[pl.BlockSpec((r, d), lambda j: (0, 0)),
                  pl.BlockSpec((d, tn), lambda j: (0, j)),
                  pl.BlockSpec((1, tn), lambda j: (0, j))],
        out_specs=pl.BlockSpec((r, tn), lambda j: (0, j)),
        out_shape=jax.ShapeDtypeStruct((r, n), F32),
        compiler_params=_params(1),
        name="adaln_mod",
    )(cvecs, w_mod, b_mod.reshape(1, n))
    return out.reshape(r, 1, n)


def _rope(x, cos, sin_signed):
    lane = lax.broadcasted_iota(jnp.int32, x.shape, 1)
    partner = jnp.where((lane & 32) == 0, pltpu.roll(x, LANES - 32, 1), pltpu.roll(x, 32, 1))
    return x * cos + partner * sin_signed


def _inproj_kernel(*refs, latent):
    if latent:
        (x_ref, mod_ref, g1_ref, win_ref, wg_ref, bg_ref, qg_ref, kg_ref, bd_ref, cos_ref, sin_ref,
         aq_ref, ak_ref, av_ref, mq_ref, mk_ref, mv_ref, mo_ref, gtt_ref) = refs
    else:
        (x_ref, mod_ref, g1_ref, win_ref, wg_ref, bg_ref, qg_ref, kg_ref, bd_ref,
         aq_ref, ak_ref, av_ref, mq_ref, mk_ref, mv_ref, mo_ref, gtt_ref, ck_ref, cv_ref,
         cks_ref, cvs_ref) = refs
    x = x_ref[...]
    d = x.shape[-1]
    mod = mod_ref[0]
    sh1, sc1 = mod[:, 0:d], mod[:, d:2 * d]
    y = x * lax.rsqrt(jnp.mean(x * x, axis=-1, keepdims=True) + EPS)
    h = (y * g1_ref[...]) * (1.0 + sc1) + sh1
    hb = h.astype(BF16)

    def head_norm(a, g):
        ss = _dot((a * a).astype(BF16), bd_ref[...])
        return a * lax.rsqrt(ss * (1.0 / ATT_HEAD_DIM) + EPS) * g

    w = ATT_WIDTH
    att = _dot(hb, win_ref[:, 0:3 * w])
    aq_ref[...] = (head_norm(att[:, 0:w], qg_ref[...]) * ATT_HEAD_DIM ** -0.5).astype(aq_ref.dtype)
    kn = head_norm(att[:, w:2 * w], kg_ref[...])
    av = att[:, 2 * w:3 * w]
    ak_ref[...] = kn.astype(BF16)
    if latent:
        av_ref[...] = av.T.astype(BF16)
    else:
        av_ref[...] = av.astype(BF16)
        tm = kn.shape[0]
        for src, dst, scr in ((kn, ck_ref, cks_ref), (av, cv_ref, cvs_ref)):
            for hh in range(N_ATT_HEADS):
                pair = src[:, (hh // 2) * LANES:(hh // 2 + 1) * LANES]
                if hh % 2:
                    pair = pltpu.roll(pair, ATT_HEAD_DIM, 1)
                scr[pl.ds(hh, tm, stride=N_ATT_HEADS), :] = pair
            dst[...] = scr[...].reshape(tm, N_ATT_HEADS, LANES)[:, :, 0:ATT_HEAD_DIM]

    w = ML_WIDTH
    ml0 = 3 * ATT_WIDTH
    mq = _dot(hb, win_ref[:, ml0:ml0 + w])
    mk = _dot(hb, win_ref[:, ml0 + w:ml0 + 2 * w]) * ML_HEAD_DIM ** -0.5
    if latent:
        cos, sin = cos_ref[...], sin_ref[...]
        for hh in range(N_ML_HEADS):
            sl = slice(hh * ML_HEAD_DIM, (hh + 1) * ML_HEAD_DIM)
            mq_ref[:, sl] = _rope(mq[:, sl], cos, sin).astype(BF16)
            mk_ref[:, sl] = _rope(mk[:, sl], cos, sin).astype(BF16)
    else:
        mq_ref[...] = mq.astype(BF16)
        mk_ref[...] = mk.astype(BF16)
    mv_ref[...] = _dot(hb, win_ref[:, ml0 + 2 * w:ml0 + 3 * w]).T.astype(BF16)
    mo_ref[...] = _dot(hb, win_ref[:, ml0 + 3 * w:ml0 + 4 * w]).T
    gates = _dot(hb, wg_ref[...]) + bg_ref[...]
    gtt_ref[...] = gates.T[0:2 * GATE_ROWS]


def _in_projection_parts(x, mod, mod_row, seq_len, wts, rope, *, latent):
    n_tok, d = x.shape
    tm = _token_tile(n_tok, seq_len)
    tiles_per_seq = max(seq_len // tm, 1)

    def tok(width):
        return pl.BlockSpec((tm, width), lambda i: (i, 0))

    in_specs = [tok(d),
                pl.BlockSpec((1, 1, mod.shape[-1]), lambda i: (mod_row(i, tm), 0, 0)),
                _const_spec((1, d)),
                _const_spec(wts["w_in"].shape),
                _const_spec(wts["w_g"].shape), _const_spec((1, LANES)),
                _const_spec((1, ATT_WIDTH)), _const_spec((1, ATT_WIDTH)),
                _const_spec((ATT_WIDTH, ATT_WIDTH))]
    args = [x, mod, wts["norm1_g"], wts["w_in"], wts["w_g"], wts["b_g"],
            wts["q_g"], wts["k_g"], wts["blockdiag"]]
    if latent:
        in_specs += [pl.BlockSpec((tm, LANES), lambda i: (i % tiles_per_seq, 0))] * 2
        args += [rope[0], rope[1]]
    out_shape = [jax.ShapeDtypeStruct((n_tok, ATT_WIDTH), BF16),
                 jax.ShapeDtypeStruct((n_tok, ATT_WIDTH), BF16),
                 jax.ShapeDtypeStruct((ATT_WIDTH, n_tok) if latent else (n_tok, ATT_WIDTH), BF16),
                 jax.ShapeDtypeStruct((n_tok, ML_WIDTH), BF16),
                 jax.ShapeDtypeStruct((n_tok, ML_WIDTH), BF16),
                 jax.ShapeDtypeStruct((ML_WIDTH, n_tok), BF16),
                 jax.ShapeDtypeStruct((ML_WIDTH, n_tok), F32),
                 jax.ShapeDtypeStruct((2 * GATE_ROWS, n_tok), F32)]

    def tok_t(height):
        return pl.BlockSpec((height, tm), lambda i: (0, i))

    av_spec = tok_t(ATT_WIDTH) if latent else tok(ATT_WIDTH)
    out_specs = ([tok(ATT_WIDTH)] * 2 + [av_spec] + [tok(ML_WIDTH)] * 2 + [tok_t(ML_WIDTH)] * 2
                 + [tok_t(2 * GATE_ROWS)])
    if not latent:
        cache = jax.ShapeDtypeStruct((n_tok, N_ATT_HEADS, ATT_HEAD_DIM), F32)
        out_shape += [cache, cache]
        out_specs += [pl.BlockSpec((tm, N_ATT_HEADS, ATT_HEAD_DIM), lambda i: (i, 0, 0))] * 2
    scratch = [] if latent else [pltpu.VMEM((tm * N_ATT_HEADS, LANES), F32)] * 2
    return n_tok // tm, in_specs, args, out_shape, out_specs, scratch


def _in_projection(x, mod, mod_row, seq_len, wts, rope, *, latent):
    n_tiles, in_specs, args, out_shape, out_specs, scratch = _in_projection_parts(
        x, mod, mod_row, seq_len, wts, rope, latent=latent)
    return pl.pallas_call(
        functools.partial(_inproj_kernel, latent=latent),
        grid=(n_tiles,),
        in_specs=in_specs,
        out_specs=out_specs,
        out_shape=out_shape,
        scratch_shapes=scratch,
        compiler_params=_params(1),
        name="in_proj_latent" if latent else "in_proj_ctx",
    )(*args)


def _inproj_cast_kernel(*refs, n_in, n_out):
    ip_in, (wup_ref, wdn_ref) = refs[:n_in], refs[n_in:n_in + 2]
    outs = refs[n_in + 2:]
    ip_out, (wup_o, wdn_o), scratch = outs[:n_out], outs[n_out:n_out + 2], outs[n_out + 2:]
    _inproj_kernel(*ip_in, *ip_out, *scratch, latent=False)
    _retile_cast_kernel(wup_ref, wup_o)
    wdn_o[0] = wdn_ref[...].astype(wdn_o.dtype)


def _ctx_in_projection_with_weight_casts(x, mod, mod_row, seq_len, wts, w_up, w_down, tn):
    n_tiles, in_specs, args, out_shape, out_specs, scratch = _in_projection_parts(
        x, mod, mod_row, seq_len, wts, None, latent=False)
    d, two_dff = w_up.shape
    nct = w_down.shape[0] // tn
    n_grp = two_dff // (2 * tn)
    assert two_dff % (2 * tn) == 0 and max(n_grp, nct) <= n_tiles
    cast_in = [pl.BlockSpec((d, 2 * tn), lambda i: (0, jnp.minimum(i, n_grp - 1))),
               pl.BlockSpec((tn, d), lambda i: (jnp.minimum(i, nct - 1), 0))]
    cast_out = [pl.BlockSpec((2, d, tn), lambda i: (jnp.minimum(i, n_grp - 1), 0, 0)),
                pl.BlockSpec((1, tn, d), lambda i: (jnp.minimum(i, nct - 1), 0, 0))]
    cast_shape = [jax.ShapeDtypeStruct((two_dff // tn, d, tn), BF16), jax.ShapeDtypeStruct((nct, tn, d), BF16)]
    outs = pl.pallas_call(
        functools.partial(_inproj_cast_kernel, n_in=len(in_specs), n_out=len(out_specs)),
        grid=(n_tiles,),
        in_specs=in_specs + cast_in,
        out_specs=out_specs + cast_out,
        out_shape=out_shape + cast_shape,
        scratch_shapes=scratch,
        compiler_params=_params(1),
        name="in_proj_ctx_weight_casts",
    )(*args, w_up, w_down)
    return outs[:len(out_specs)], outs[len(out_specs)], outs[len(out_specs) + 1]


def _ctx_attn_kernel(q_ref, k_ref, v_ref, o_ref):
    n = q_ref.shape[0]
    gw = HEAD_GROUP * ATT_HEAD_DIM
    lane_head = lax.broadcasted_iota(jnp.int32, (n, gw), 1) // ATT_HEAD_DIM
    for g in range(N_ATT_HEADS // HEAD_GROUP):
        sl = slice(g * gw, (g + 1) * gw)
        q4 = q_ref[:, sl]
        qbd = jnp.concatenate([jnp.where(lane_head == hl, q4, jnp.zeros_like(q4))
                               for hl in range(HEAD_GROUP)], axis=0)
        s = _dot_nt(k_ref[:, sl], qbd)
        p = jnp.exp(s - jnp.max(s, axis=0, keepdims=True))
        p = p / jnp.sum(p, axis=0, keepdims=True)
        o4 = _dot_tn(p.astype(BF16), v_ref[:, sl])
        out = jnp.where(lane_head == 0, o4[0:n], 0.0)
        for hl in range(1, HEAD_GROUP):
            out = out + jnp.where(lane_head == hl, o4[hl * n:(hl + 1) * n], 0.0)
        o_ref[:, sl] = out.astype(o_ref.dtype)


def _context_attention(q, k, v, seq_len):
    n_tok = q.shape[0]
    spec = pl.BlockSpec((seq_len, ATT_WIDTH), lambda b: (b, 0))
    return pl.pallas_call(
        _ctx_attn_kernel,
        grid=(n_tok // seq_len,),
        in_specs=[spec, spec, spec],
        out_specs=spec,
        out_shape=jax.ShapeDtypeStruct((n_tok, ATT_WIDTH), BF16),
        compiler_params=_params(1),
        name="ctx_attn",
    )(q, k, v)


def _nbr_attn_kernel(q_ref, k_ref, vt_ref, kc_ref, vct_ref, bias_ref, o_ref, pt_ref, *, rows, kr):
    r = pl.program_id(1)
    n_win = kr + 2
    n_loc = kr * GRID_W
    gw = HEAD_GROUP * ATT_HEAD_DIM
    groups = N_ATT_HEADS // HEAD_GROUP
    lane_head = lax.broadcasted_iota(jnp.int32, (GRID_W, gw), 1) // ATT_HEAD_DIM
    rs = jnp.clip(r - kr // 2, 0, rows - kr)
    start = jnp.minimum(rs - (rs & 1), rows - n_win)
    delta = rs - start
    bias0 = pl.multiple_of((WIN_ROWS - 1 - (r - rs)) * GRID_W, GRID_W)
    zeros2 = jnp.zeros((2 * GRID_W, gw), BF16)
    for g in range(groups):
        sl = slice(g * gw, (g + 1) * gw)
        q4 = q_ref[:, sl]
        qbd = jnp.concatenate([jnp.where(lane_head == hl, q4, jnp.zeros_like(q4))
                               for hl in range(HEAD_GROUP)], axis=0)
        s_loc = _dot_nt(k_ref[pl.ds(pl.multiple_of(rs * GRID_W, GRID_W), n_loc), sl], qbd)
        s_loc = s_loc + bias_ref[g, pl.ds(bias0, n_loc), :]
        s_ctx = _dot_nt(kc_ref[0, :, sl], qbd)
        m = jnp.maximum(jnp.max(s_loc, axis=0, keepdims=True), jnp.max(s_ctx, axis=0, keepdims=True))
        p_loc = jnp.exp(s_loc - m)
        p_ctx = jnp.exp(s_ctx - m)
        l = jnp.sum(p_loc, axis=0, keepdims=True) + jnp.sum(p_ctx, axis=0, keepdims=True)
        pt_ref[g, 0:2 * GRID_W, :] = zeros2
        pt_ref[g, n_loc:n_loc + 2 * GRID_W, :] = zeros2
        pt_ref[g, pl.ds(pl.multiple_of(delta * GRID_W, GRID_W), n_loc), :] = p_loc.astype(BF16)
        vt_win = vt_ref[sl, pl.ds(pl.multiple_of(start * GRID_W, 2 * GRID_W), n_win * GRID_W)]
        ot = _dot(vt_win, pt_ref[g]) + _dot(vct_ref[0, sl, :], p_ctx.astype(BF16))
        o4 = (ot / l).T
        out = jnp.where(lane_head == 0, o4[0:GRID_W], 0.0)
        for hl in range(1, HEAD_GROUP):
            out = out + jnp.where(lane_head == hl, o4[hl * GRID_W:(hl + 1) * GRID_W], 0.0)
        o_ref[:, sl] = out.astype(o_ref.dtype)


def _nbr_bias_table(rpb, rows):
    col = jnp.arange(GRID_W)
    cs = jnp.clip(col - WIN_COLS // 2, 0, GRID_W - WIN_COLS)
    in_win = (col[None, :] >= cs[:, None]) & (col[None, :] < cs[:, None] + WIN_COLS)
    dc_idx = jnp.clip(col[None, :] - col[:, None] + (WIN_COLS - 1), 0, 2 * WIN_COLS - 2)
    n_dr, n_dc = rpb.shape[1], rpb.shape[2]
    onehot = (dc_idx[None] == jnp.arange(n_dc)[:, None, None]).astype(F32)
    t = jnp.einsum("hrc,cqk->hrqk", rpb.astype(F32), onehot, precision=lax.Precision.HIGHEST)
    t = jnp.where(in_win[None, None], t, -jnp.inf)
    n_groups = N_ATT_HEADS // HEAD_GROUP
    t = t.reshape(n_groups, HEAD_GROUP, n_dr, GRID_W, GRID_W).transpose(0, 2, 4, 1, 3)
    return t.reshape(n_groups, n_dr * GRID_W, HEAD_GROUP * GRID_W)


def _neighborhood_attention(q, k, vt, k_ctx, vt_ctx, rpb, n_batch, seq_len):
    rows = seq_len // GRID_W
    kr = min(WIN_ROWS, rows)
    n_win = kr + 2
    assert rows >= n_win and (rows - n_win) % 2 == 0
    bias = _nbr_bias_table(rpb, rows)
    past = k_ctx.shape[1]
    gw = HEAD_GROUP * ATT_HEAD_DIM
    n_groups = N_ATT_HEADS // HEAD_GROUP
    row_spec = pl.BlockSpec((GRID_W, ATT_WIDTH), lambda b, r: (b * rows + r, 0))
    return pl.pallas_call(
        functools.partial(_nbr_attn_kernel, rows=rows, kr=kr),
        grid=(n_batch, rows),
        in_specs=[row_spec,
                  pl.BlockSpec((seq_len, ATT_WIDTH), lambda b, r: (b, 0)),
                  pl.BlockSpec((ATT_WIDTH, seq_len), lambda b, r: (0, b)),
                  pl.BlockSpec((1, past, ATT_WIDTH), lambda b, r: (b, 0, 0)),
                  pl.BlockSpec((1, ATT_WIDTH, past), lambda b, r: (b, 0, 0)),
                  _const_spec(bias.shape)],
        out_specs=row_spec,
        out_shape=jax.ShapeDtypeStruct((n_batch * seq_len, ATT_WIDTH), BF16),
        scratch_shapes=[pltpu.VMEM((n_groups, n_win * GRID_W, gw), BF16)],
        compiler_params=_params(2),
        name="nbr_attn",
    )(q, k, vt, k_ctx, vt_ctx, bias)


def _mlstm_kernel(*refs, chunk, n_chunks, has_state):
    dh = ML_HEAD_DIM
    nrep = 2 * GATE_ROWS
    if has_state:
        (q_ref, k_ref, vt_ref, mot_ref, gtt_ref, g_ref, c0_ref, n0_ref, m0_ref,
         o_ref, c_ref, n_ref, m_ref, hf_ref, hb_ref, cn_ref, mrun_ref) = refs
        for d in range(2):
            for hh in range(N_ML_HEADS):
                cn_ref[d, hh, 0:dh, :] = c0_ref[0, d, hh].T
                cn_ref[d, hh, dh:dh + nrep, :] = jnp.broadcast_to(n0_ref[0, d, hh], (nrep, dh))
        mrun_ref[...] = m0_ref[0]
    else:
        (q_ref, k_ref, vt_ref, mot_ref, gtt_ref, g_ref,
         o_ref, c_ref, n_ref, m_ref, hf_ref, hb_ref, cn_ref, mrun_ref) = refs
        cn_ref[...] = jnp.zeros_like(cn_ref)
        mrun_ref[...] = jnp.zeros_like(mrun_ref)
    use_state = has_state or n_chunks > 1

    i0 = lax.broadcasted_iota(jnp.int32, (chunk, chunk), 0)
    i1 = lax.broadcasted_iota(jnp.int32, (chunk, chunk), 1)
    row_id = lax.broadcasted_iota(jnp.int32, (GATE_ROWS, 1), 0)
    instances = [(hh, d) for d in range(2) for hh in range(N_ML_HEADS)]

    def split3(x):
        hi = x.astype(BF16)
        r1 = x - hi.astype(F32)
        mid = r1.astype(BF16)
        return hi, mid, (r1 - mid.astype(F32)).astype(BF16)

    def body(i, _):
        chunks = (i, n_chunks - 1 - i)
        rows = [pl.ds(pl.multiple_of(c * chunk, chunk), chunk) for c in chunks]
        le = [i0 <= i1, i0 >= i1]
        m_prev = mrun_ref[...][:, 0:1]

        def qkv(hh, d):
            sl = slice(hh * dh, (hh + 1) * dh)
            return q_ref[rows[d], sl], k_ref[rows[d], sl], vt_ref[sl, rows[d]]

        st = [_dot_nt(qkv(hh, d)[1], qkv(hh, d)[0]) for hh, d in instances]
        if use_state:
            cn = [cn_ref[d, hh] for hh, d in instances]
            qct = [_dot_nt(cn[n].astype(BF16), qkv(hh, d)[0]) for n, (hh, d) in enumerate(instances)]

        a_col, a8, cs8, w8, decay8, m_new = [], [], [], [], [], []
        for d in range(2):
            gi_r = gtt_ref[0:GATE_ROWS, rows[d]]
            lf_r = _log_sigmoid(gtt_ref[GATE_ROWS:2 * GATE_ROWS, rows[d]])
            tri_r = jnp.where(le[d], 1.0, 0.0).astype(BF16)
            cs_r = sum(_dot(t, tri_r) for t in split3(lf_r))
            b_last = jnp.sum(lf_r, axis=1, keepdims=True)
            w_end = b_last + gi_r - cs_r
            m_d = jnp.maximum(b_last + m_prev, jnp.max(w_end, axis=1, keepdims=True))
            a8.append(gi_r - cs_r)
            cs8.append(cs_r)
            decay8.append(jnp.exp(b_last + m_prev - m_d))
            w8.append(jnp.exp(w_end - m_d))
            m_new.append(m_d)
            pad = jnp.zeros((LANES - GATE_ROWS, chunk), F32)
            a_col.append(jnp.concatenate([gi_r - cs_r, pad], axis=0).T)
        m_next = jnp.where((row_id & 1) == 0, m_new[0], m_new[1])
        mrun_ref[...] = jnp.broadcast_to(m_next, (GATE_ROWS, LANES))

        pt_all, g_all, den_all = [], [], []
        for n, (hh, d) in enumerate(instances):
            j = 2 * hh + d
            a = jnp.where(le[d], a_col[d][:, j:j + 1], -jnp.inf)
            g = jnp.maximum(jnp.max(a, axis=0, keepdims=True), m_prev[j:j + 1])
            pt = st[n] * jnp.exp(a - g)
            pt_all.append(pt)
            g_all.append(g)
            den_all.append(jnp.sum(pt, axis=0, keepdims=True))

        for n, (hh, d) in enumerate(instances):
            j = 2 * hh + d
            sl = slice(hh * dh, (hh + 1) * dh)
            _, k, vt = qkv(hh, d)
            g, den = g_all[n], den_all[n]
            num = _dot(vt, pt_all[n].astype(BF16))
            if use_state:
                w_inter = jnp.exp(m_prev[j:j + 1] - g)
                num = num + w_inter * qct[n][0:dh]
                den = den + w_inter * qct[n][dh:dh + 1]
            scale = 1.0 / jnp.maximum(jnp.abs(den), jnp.exp(-(cs8[d][j:j + 1] + g)))
            (hb_ref if d else hf_ref)[sl, rows[d]] = num * scale
            w_row = w8[d][j:j + 1]
            lhs = jnp.concatenate([(vt.astype(F32) * w_row).astype(BF16),
                                   jnp.broadcast_to(w_row, (nrep, chunk)).astype(BF16)], axis=0)
            upd = _dot(lhs, k)
            if use_state:
                upd = decay8[d][j:j + 1] * cn[n] + upd
            cn_ref[d, hh] = upd
        return 0

    def finish(c, _):
        rows = pl.ds(pl.multiple_of(c * chunk, chunk), chunk)
        for hh in range(N_ML_HEADS):
            sl = slice(hh * dh, (hh + 1) * dh)
            hs = hf_ref[sl, rows] + hb_ref[sl, rows]
            y = hs * lax.rsqrt(jnp.mean(hs * hs, axis=0, keepdims=True) + EPS) * g_ref[sl, :]
            y = y * jax.nn.sigmoid(mot_ref[sl, rows])
            o_ref[rows, sl] = y.T.astype(o_ref.dtype)
        return 0

    if n_chunks == 1:
        body(0, 0)
        finish(0, 0)
    else:
        lax.fori_loop(0, n_chunks, body, 0)
        lax.fori_loop(0, n_chunks, finish, 0)
    for d in range(2):
        for hh in range(N_ML_HEADS):
            cn = cn_ref[d, hh]
            c_ref[0, d, hh] = cn[0:dh].T
            n_ref[0, d, hh] = cn[dh:dh + 1]
    m_ref[0] = mrun_ref[...]


def _mlstm(mq, mk, mvt, mot, gates_t, ml_g, state, n_batch, seq_len, chunk):
    dh = ML_HEAD_DIM
    nh = N_ML_HEADS
    width = nh * dh
    has_state = state is not None
    seq = pl.BlockSpec((seq_len, width), lambda b: (b, 0))
    seq_t = pl.BlockSpec((width, seq_len), lambda b: (0, b))
    st_c = pl.BlockSpec((1, 2, nh, dh, dh), lambda b: (b, 0, 0, 0, 0))
    st_n = pl.BlockSpec((1, 2, nh, 1, dh), lambda b: (b, 0, 0, 0, 0))
    st_m = pl.BlockSpec((1, GATE_ROWS, LANES), lambda b: (b, 0, 0))
    in_specs = [seq, seq, seq_t, seq_t,
                pl.BlockSpec((2 * GATE_ROWS, seq_len), lambda b: (0, b)),
                _const_spec((width, chunk))]
    args = [mq, mk, mvt, mot, gates_t, jnp.broadcast_to(ml_g.reshape(width, 1), (width, chunk))]
    if has_state:
        in_specs += [st_c, st_n, st_m]
        args += list(state)
    out_shape = [jax.ShapeDtypeStruct((n_batch * seq_len, width), BF16),
                 jax.ShapeDtypeStruct((n_batch, 2, nh, dh, dh), F32),
                 jax.ShapeDtypeStruct((n_batch, 2, nh, 1, dh), F32),
                 jax.ShapeDtypeStruct((n_batch, GATE_ROWS, LANES), F32)]
    return pl.pallas_call(
        functools.partial(_mlstm_kernel, chunk=chunk, n_chunks=seq_len // chunk, has_state=has_state),
        grid=(n_batch,),
        in_specs=in_specs,
        out_specs=[seq, st_c, st_n, st_m],
        out_shape=out_shape,
        scratch_shapes=[pltpu.VMEM((width, seq_len), F32), pltpu.VMEM((width, seq_len), F32),
                        pltpu.VMEM((2, nh, dh + 2 * GATE_ROWS, dh), F32), pltpu.VMEM((GATE_ROWS, LANES), F32)],
        compiler_params=_params(1),
        name="mlstm_latent" if has_state else "mlstm_ctx",
    )(*args)


def _pack_gate_rows(m):
    b = m.shape[0]
    packed = m.transpose(0, 2, 1).reshape(b, 2 * N_ML_HEADS, 1)
    return jnp.broadcast_to(packed, (b, GATE_ROWS, LANES))


def _unpack_gate_rows(m):
    b = m.shape[0]
    return m[:, :, 0].reshape(b, N_ML_HEADS, 2).transpose(0, 2, 1)


def _inproj_ctxmix_kernel(*refs, n_in, n_out, n_seq, seq_len, chunk):
    ip_in = refs[:n_in]
    q_ref, k_ref, v_ref, mq_ref, mk_ref, mvt_ref, mot_ref, gtt_ref, g_ref = refs[n_in:n_in + 9]
    outs = refs[n_in + 9:]
    ip_out = outs[:n_out]
    oatt_ref, oml_ref, c_ref, n_ref, m_ref = outs[n_out:n_out + 5]
    scratch = outs[n_out + 5:]
    _inproj_kernel(*ip_in, *ip_out, latent=True)
    for s in range(n_seq):
        rows = slice(s * seq_len, (s + 1) * seq_len)
        _ctx_attn_kernel(q_ref.at[rows, :], k_ref.at[rows, :], v_ref.at[rows, :], oatt_ref.at[rows, :])
        _mlstm_kernel(mq_ref.at[rows, :], mk_ref.at[rows, :], mvt_ref.at[:, rows], mot_ref.at[:, rows],
                      gtt_ref.at[:, rows], g_ref, oml_ref.at[rows, :], c_ref.at[s:s + 1], n_ref.at[s:s + 1],
                      m_ref.at[s:s + 1], *scratch[4 * s:4 * s + 4],
                      chunk=chunk, n_chunks=seq_len // chunk, has_state=False)


def _latent_in_projection_with_ctx_mixers(x, mod, mod_row, seq_len, wts, rope, ctx, n_ctx_batch, ctx_seq):
    n_tiles, in_specs, args, out_shape, out_specs, scratch = _in_projection_parts(
        x, mod, mod_row, seq_len, wts, rope, latent=True)
    n_seq = n_ctx_batch // n_tiles
    rows = n_seq * ctx_seq
    n_ctx_tok = n_ctx_batch * ctx_seq
    chunk = min(ML_CHUNK, ctx_seq)
    nh, dh = N_ML_HEADS, ML_HEAD_DIM

    def tok(width):
        return pl.BlockSpec((rows, width), lambda i: (i, 0))

    def tok_t(height):
        return pl.BlockSpec((height, rows), lambda i: (0, i))

    def per_seq(*tail):
        return pl.BlockSpec((n_seq,) + tail, lambda i: (i,) + (0,) * len(tail))

    aq, ak, av, mq, mk, mvt, mot, gtt = ctx
    mix_in_specs = ([tok(ATT_WIDTH)] * 3 + [tok(ML_WIDTH)] * 2 + [tok_t(ML_WIDTH)] * 2
                    + [tok_t(2 * GATE_ROWS), _const_spec((ML_WIDTH, chunk))])
    mix_args = [aq, ak, av, mq, mk, mvt, mot, gtt,
                jnp.broadcast_to(wts["ml_g"].reshape(ML_WIDTH, 1), (ML_WIDTH, chunk))]
    mix_out_shape = [jax.ShapeDtypeStruct((n_ctx_tok, ATT_WIDTH), BF16),
                     jax.ShapeDtypeStruct((n_ctx_tok, ML_WIDTH), BF16),
                     jax.ShapeDtypeStruct((n_ctx_batch, 2, nh, dh, dh), F32),
                     jax.ShapeDtypeStruct((n_ctx_batch, 2, nh, 1, dh), F32),
                     jax.ShapeDtypeStruct((n_ctx_batch, GATE_ROWS, LANES), F32)]
    mix_out_specs = [tok(ATT_WIDTH), tok(ML_WIDTH), per_seq(2, nh, dh, dh), per_seq(2, nh, 1, dh),
                     per_seq(GATE_ROWS, LANES)]
    mix_scratch = [pltpu.VMEM((ML_WIDTH, ctx_seq), F32), pltpu.VMEM((ML_WIDTH, ctx_seq), F32),
                   pltpu.VMEM((2, nh, dh + 2 * GATE_ROWS, dh), F32), pltpu.VMEM((GATE_ROWS, LANES), F32)] * n_seq
    outs = pl.pallas_call(
        functools.partial(_inproj_ctxmix_kernel, n_in=len(in_specs), n_out=len(out_specs), n_seq=n_seq,
                          seq_len=ctx_seq, chunk=chunk),
        grid=(n_tiles,),
        in_specs=in_specs + mix_in_specs,
        out_specs=out_specs + mix_out_specs,
        out_shape=out_shape + mix_out_shape,
        scratch_shapes=scratch + mix_scratch,
        compiler_params=_params(1),
        name="in_proj_latent_ctx_mixers",
    )(*args, *mix_args)
    return outs[:len(out_specs)], outs[len(out_specs):]


def _ffn_kernel(oa_ref, oap_ref, oan_ref, om_ref, omp_ref, omn_ref, x_ref, xp_ref, xn_ref, mod_ref, g2_ref,
                wout_ref, wup_ref, cw_ref, cb_ref, wd_ref,
                y_ref, oc_ref, lhs_ref, x1_ref, ug_ref, uv_ref, acc_ref, *, seq_len, n_col_tiles):
    tm, d = x_ref.shape
    aw = oa_ref.shape[1]

    def halo(next_ref, prev_ref):
        row = lax.broadcasted_iota(jnp.int32, next_ref.shape, 0)
        return jnp.where(row < HALO // 2, next_ref[...], prev_ref[...])

    oc_ref[0:tm, 0:aw] = oa_ref[...]
    oc_ref[0:tm, aw:] = om_ref[...]
    oc_ref[tm:tm + HALO, 0:aw] = halo(oan_ref, oap_ref)
    oc_ref[tm:tm + HALO, aw:] = halo(omn_ref, omp_ref)
    mod = mod_ref[0]
    g1 = mod[:, 2 * d:3 * d]
    sh2, sc2 = mod[:, 3 * d:4 * d], mod[:, 4 * d:5 * d]
    out = _dot(oc_ref[...], wout_ref[...])

    def norm2(x1):
        y = x1 * lax.rsqrt(jnp.mean(x1 * x1, axis=-1, keepdims=True) + EPS)
        return ((y * g2_ref[...]) * (1.0 + sc2) + sh2).astype(BF16)

    x1 = x_ref[...] + g1 * out[0:tm]
    x1_ref[...] = x1
    lhs_ref[0:tm, :] = norm2(x1)
    lhs_ref[tm:tm + HALO, :] = norm2(halo(xn_ref, xp_ref) + g1 * out[tm:tm + HALO])
    acc_ref[...] = jnp.zeros_like(acc_ref)
    sub = 8
    period = min(seq_len, tm)
    first_groups = sorted({r // sub for r in range(0, tm, period)})
    last_groups = sorted({(r + period - 1) // sub for r in range(0, tm, period)})
    tile0 = pl.program_id(0) * tm

    def zero_rows(x, groups, target):
        parts, at = [], 0
        for grp in groups:
            lo = grp * sub
            if lo > at:
                parts.append(x[at:lo])
            pos = (tile0 + lo + lax.broadcasted_iota(jnp.int32, (sub, 1), 0)) % seq_len
            parts.append(jnp.where(pos == target, 0.0, x[lo:lo + sub]))
            at = lo + sub
        if at < tm:
            parts.append(x[at:tm])
        return jnp.concatenate(parts, axis=0)

    def conv(u, cw, cb):
        prev = zero_rows(pltpu.roll(u, 1, 0)[0:tm], first_groups, 0)
        nxt = zero_rows(pltpu.roll(u, tm + HALO - 1, 0)[0:tm], last_groups, seq_len - 1)
        return prev * cw[0:1] + u[0:tm] * cw[1:2] + nxt * cw[2:3] + cb

    def up(j, slot):
        lhs = lhs_ref[...]
        ug_ref[slot] = _dot(lhs, wup_ref[j])
        uv_ref[slot] = _dot(lhs, wup_ref[n_col_tiles + j])

    def act(j, slot):
        gate = conv(ug_ref[slot], cw_ref[j], cb_ref[j])
        val = conv(uv_ref[slot], cw_ref[n_col_tiles + j], cb_ref[n_col_tiles + j])
        return ((gate * jax.nn.sigmoid(gate)) * val).astype(BF16)

    def stage(j, slot, next_up):
        a = act(j, slot)
        if next_up:
            up(j + 1, 1 - slot)
        return _dot(a, wd_ref[j])

    def pair(i, _):
        acc_ref[...] += stage(2 * i, 0, True)
        acc_ref[...] += stage(2 * i + 1, 1, True)
        return 0

    up(0, 0)
    n_pairs = (n_col_tiles - 1) // 2
    if n_pairs:
        lax.fori_loop(0, n_pairs, pair, 0)
    j0 = 2 * n_pairs
    if n_col_tiles - j0 == 2:
        acc_ref[...] += stage(j0, 0, True)
        tail = stage(j0 + 1, 1, False)
    else:
        tail = stage(j0, 0, False)
    g2 = mod_ref[0][:, 5 * d:6 * d]
    y_ref[...] = x1_ref[...] + g2 * (acc_ref[...] + tail)


def _mix_ffn(o_att, o_ml, x, mod, mod_row, seq_len, wts):
    n_tok, d = x.shape
    tm = _token_tile(n_tok, seq_len)
    hpt = tm // HALO
    n_halo = n_tok // HALO
    nct, tn, _ = wts["w_down"].shape
    mix = o_att.shape[1] + o_ml.shape[1]

    def tile_and_halos(width):
        return [pl.BlockSpec((tm, width), lambda i: (i, 0)),
                pl.BlockSpec((HALO, width), lambda i: (jnp.maximum(i * hpt - 1, 0), 0)),
                pl.BlockSpec((HALO, width), lambda i: (jnp.minimum((i + 1) * hpt, n_halo - 1), 0))]

    in_specs = (tile_and_halos(o_att.shape[1]) + tile_and_halos(o_ml.shape[1]) + tile_and_halos(d) + [
        pl.BlockSpec((1, 1, mod.shape[-1]), lambda i: (mod_row(i, tm), 0, 0)),
        _const_spec((1, d)), _const_spec((mix, d)),
        _const_spec((2 * nct, d, tn)), _const_spec((2 * nct, 3, tn)), _const_spec((2 * nct, 1, tn)),
        _const_spec((nct, tn, d))])
    return pl.pallas_call(
        functools.partial(_ffn_kernel, seq_len=seq_len, n_col_tiles=nct),
        grid=(n_tok // tm,),
        in_specs=in_specs,
        out_specs=pl.BlockSpec((tm, d), lambda i: (i, 0)),
        out_shape=jax.ShapeDtypeStruct((n_tok, d), F32),
        scratch_shapes=[pltpu.VMEM((tm + HALO, mix), BF16), pltpu.VMEM((tm + HALO, d), BF16),
                        pltpu.VMEM((tm, d), F32),
                        pltpu.VMEM((2, tm + HALO, tn), F32), pltpu.VMEM((2, tm + HALO, tn), F32),
                        pltpu.VMEM((tm, d), F32)],
        compiler_params=_params(1),
        name="mix_ffn",
    )(o_att, o_att, o_att, o_ml, o_ml, o_ml, x, x, x, mod, wts["norm2_g"], wts["w_out"],
      wts["w_up"], wts["conv_w"], wts["conv_b"], wts["w_down"])


def _retile_cast_kernel(w_ref, o_ref):
    tn = o_ref.shape[-1]
    for t in range(o_ref.shape[0]):
        o_ref[t] = w_ref[:, t * tn:(t + 1) * tn].astype(o_ref.dtype)


def _retile_cast(w, tn):
    r, c = w.shape
    group = 2 if (c // tn) % 2 == 0 else 1
    return pl.pallas_call(
        _retile_cast_kernel,
        grid=(c // (tn * group),),
        in_specs=[pl.BlockSpec((r, tn * group), lambda j: (0, j))],
        out_specs=pl.BlockSpec((group, r, tn), lambda j: (j, 0, 0)),
        out_shape=jax.ShapeDtypeStruct((c // tn, r, tn), BF16),
        compiler_params=_params(1),
        name="retile_cast",
    )(w)


def _cast_kernel(w_ref, o_ref):
    o_ref[...] = w_ref[...].astype(o_ref.dtype)


def _cast_rows(w, block_rows):
    r, c = w.shape
    spec = pl.BlockSpec((block_rows, c), lambda i: (i, 0))
    return pl.pallas_call(
        _cast_kernel,
        grid=(r // block_rows,),
        in_specs=[spec],
        out_specs=spec,
        out_shape=jax.ShapeDtypeStruct((r, c), BF16),
        compiler_params=_params(1),
        name="cast_rows",
    )(w)


def _layer_weights(norm1_g, norm2_g, w_in, b_gate, q_g, k_g, ml_g, w_out, w_up, conv_w, conv_b, w_down):
    d = w_in.shape[0]
    a, m, nh = ATT_WIDTH, ML_WIDTH, N_ML_HEADS
    def gate_lanes(g):
        g = g.reshape(g.shape[0], 2, 2, nh).transpose(0, 2, 3, 1).reshape(g.shape[0], 4 * nh)
        return jnp.pad(g, ((0, 0), (0, LANES - 4 * nh)))

    w_g = gate_lanes(w_in[:, 3 * a + 4 * m:])
    b_g = gate_lanes(b_gate.astype(F32).reshape(1, 4 * nh))
    head_id = jnp.arange(a) // ATT_HEAD_DIM
    d_ff = w_down.shape[0]
    tn = MXU_WIDTH
    nct = d_ff // tn

    def col_tiles(w):
        return w.reshape(w.shape[0], 2 * nct, tn).transpose(1, 0, 2)

    return {
        "norm1_g": norm1_g.reshape(1, d), "norm2_g": norm2_g.reshape(1, d),
        "w_in": w_in.astype(BF16),
        "w_g": w_g.astype(BF16), "b_g": b_g,
        "q_g": jnp.tile(q_g, N_ATT_HEADS).reshape(1, a), "k_g": jnp.tile(k_g, N_ATT_HEADS).reshape(1, a),
        "blockdiag": (head_id[:, None] == head_id[None, :]).astype(BF16),
        "ml_g": ml_g.reshape(1, m),
        "w_out": w_out.astype(BF16),
        "conv_w": col_tiles(conv_w.astype(F32)), "conv_b": col_tiles(conv_b.astype(F32).reshape(1, -1)),
        "w_up_f32": w_up, "w_down_f32": w_down, "ffn_tile": tn,
    }


def _rope_tables(seq_len):
    quarter = ML_HEAD_DIM // 4
    pos = np.arange(seq_len)
    inv_freq = ROPE_THETA ** (-np.arange(quarter, dtype=np.float64) / quarter)
    ang_r = (pos // GRID_W).astype(np.float64)[:, None] * inv_freq[None, :]
    ang_c = (pos % GRID_W).astype(np.float64)[:, None] * inv_freq[None, :]
    cos = np.concatenate([np.cos(ang_r)] * 2 + [np.cos(ang_c)] * 2, axis=-1)
    sin = np.concatenate([-np.sin(ang_r), np.sin(ang_r), -np.sin(ang_c), np.sin(ang_c)], axis=-1)
    return jnp.asarray(cos, F32), jnp.asarray(sin, F32)


def _layer(xp, xs, mod, wts, rope, ctx_kv, rpb, state, batch, seq, dec_batch, dec_seq):
    def ctx_row(i, tm):
        return 0 * i

    def lat_row(i, tm):
        return 1 + (i * tm) // dec_seq

    w_up, w_down, tn = wts["w_up_f32"], wts["w_down_f32"], wts["ffn_tile"]
    n_ctx_tiles = (batch * seq) // _token_tile(batch * seq, seq)
    if n_ctx_tiles >= max(w_up.shape[1] // (2 * tn), w_down.shape[0] // tn) and (w_up.shape[1] // tn) % 2 == 0:
        ctx, w_up_t, w_down_t = _ctx_in_projection_with_weight_casts(xp, mod, ctx_row, seq, wts, w_up, w_down, tn)
    else:
        ctx = _in_projection(xp, mod, ctx_row, seq, wts, None, latent=False)
        w_up_t = _retile_cast(w_up, tn)
        w_down_t = _cast_rows(w_down, tn).reshape(w_down.shape[0] // tn, tn, w_down.shape[1])
    wts = dict(wts, w_up=w_up_t, w_down=w_down_t)
    cache_k, cache_v = ctx[8], ctx[9]
    n_lat_tiles = (dec_batch * dec_seq) // _token_tile(dec_batch * dec_seq, dec_seq)
    if batch % n_lat_tiles == 0:
        lat, (o_att_c, o_ml_c, c_f, n_f, m_f) = _latent_in_projection_with_ctx_mixers(
            xs, mod, lat_row, dec_seq, wts, rope, ctx[:8], batch, seq)
    else:
        aq, ak, av, mq, mk, mvt, mot, gtt = ctx[:8]
        o_att_c = _context_attention(aq, ak, av, seq)
        o_ml_c, c_f, n_f, m_f = _mlstm(mq, mk, mvt, mot, gtt, wts["ml_g"], None, batch, seq, min(ML_CHUNK, seq))
        lat = _in_projection(xs, mod, lat_row, dec_seq, wts, rope, latent=True)
    xp = _mix_ffn(o_att_c, o_ml_c, xp, mod, ctx_row, seq, wts)

    aq, ak, avt, mq, mk, mvt, mot, gtt = lat[:8]
    o_att = _neighborhood_attention(aq, ak, avt, ctx_kv[0], ctx_kv[1], rpb, dec_batch, dec_seq)
    o_ml = _mlstm(mq, mk, mvt, mot, gtt, wts["ml_g"], state, dec_batch, dec_seq, min(ML_CHUNK, dec_seq))[0]
    xs = _mix_ffn(o_att, o_ml, xs, mod, lat_row, dec_seq, wts)
    return xp, xs, cache_k, cache_v, (c_f, n_f[:, :, :, 0, :], _unpack_gate_rows(m_f))


def kernel(x_prompt, x_sample, cache_k, cache_v, state_C, state_n, state_m, c, c_ctx, w_mod, b_mod, norm1_g,
           norm2_g, w_in, b_gate, q_norm_g, k_norm_g, rpb, ml_norm_g, w_out, w_up, conv_w, conv_b, w_down):
    batch, seq, d = x_prompt.shape
    dec_batch, dec_seq, _ = x_sample.shape
    depth = w_mod.shape[0]
    past = cache_k.shape[2]
    cvecs = jnp.concatenate([c_ctx[None, :], c], axis=0).astype(F32)
    rope = _rope_tables(dec_seq)

    xp = x_prompt.reshape(batch * seq, d)
    xs = x_sample.reshape(dec_batch * dec_seq, d)
    ks, vs, cs, ns, ms = [], [], [], [], []
    for l in range(depth):
        wts = _layer_weights(norm1_g[l], norm2_g[l], w_in[l], b_gate[l], q_norm_g[l], k_norm_g[l],
                             ml_norm_g[l], w_out[l], w_up[l], conv_w[l], conv_b[l], w_down[l])
        mod = _modulation(cvecs, w_mod[l], b_mod[l])

        state = (state_C[:, l].astype(F32),
                 state_n[:, l].astype(F32)[:, :, :, None, :],
                 _pack_gate_rows(state_m[:, l].astype(F32)))
        ctx_kv = (cache_k[:, l].reshape(dec_batch, past, ATT_WIDTH).astype(BF16),
                  cache_v[:, l].reshape(dec_batch, past, ATT_WIDTH).transpose(0, 2, 1).astype(BF16))
        xp, xs, ak, av, st = _layer(xp, xs, mod, wts, rope, ctx_kv, rpb[l], state, batch, seq, dec_batch, dec_seq)
        ks.append(ak.reshape(batch, seq, N_ATT_HEADS, ATT_HEAD_DIM))
        vs.append(av.reshape(batch, seq, N_ATT_HEADS, ATT_HEAD_DIM))
        cs.append(st[0])
        ns.append(st[1])
        ms.append(st[2])
    return (xp.reshape(batch, seq, d), xs.reshape(dec_batch, dec_seq, d),
            jnp.stack(ks, axis=1), jnp.stack(vs, axis=1),
            jnp.stack(cs, axis=1), jnp.stack(ns, axis=1), jnp.stack(ms, axis=1))
```

```python
import functools

import jax
import jax.numpy as jnp
import numpy as np
from jax import lax
from jax.experimental import pallas as pl
from jax.experimental.pallas import tpu as pltpu

F32 = jnp.float32
BF16 = jnp.bfloat16

GRID_W = 64
N_ATT_HEADS = 8
ATT_HEAD_DIM = 64
ATT_WIDTH = N_ATT_HEADS * ATT_HEAD_DIM
WIN_ROWS = 8
WIN_COLS = 16
HEAD_GROUP = 4
N_ML_HEADS = 4
ML_HEAD_DIM = 128
ML_WIDTH = N_ML_HEADS * ML_HEAD_DIM
ROPE_THETA = 10000.0
EPS = 1e-6

LANES = 128
MXU_WIDTH = 256
ML_CHUNK = 256
GATE_ROWS = 8
HALO = 16
TOKEN_TILE = 512
VMEM_LIMIT = 56 * 1024 * 1024


def _dot(a, b):
    return jnp.dot(a, b, preferred_element_type=F32)


def _dot_nt(a, b):
    return lax.dot_general(a, b, (((1,), (1,)), ((), ())), preferred_element_type=F32)


def _dot_tn(a, b):
    return lax.dot_general(a, b, (((0,), (0,)), ((), ())), preferred_element_type=F32)


def _const_spec(shape):
    nd = len(shape)
    return pl.BlockSpec(shape, lambda *_: (0,) * nd, pipeline_mode=pl.Buffered(1))


def _params(n_axes):
    return pltpu.CompilerParams(dimension_semantics=("arbitrary",) * n_axes,
                                vmem_limit_bytes=VMEM_LIMIT)


def _token_tile(n_tok, seq_len):
    tm = min(TOKEN_TILE, n_tok)
    while n_tok % tm or (seq_len % tm and tm % seq_len):
        tm //= 2
    return tm


def _log_sigmoid(x):
    return jnp.minimum(x, 0.0) - jnp.log1p(jnp.exp(-jnp.abs(x)))


def _mod_kernel(c_ref, w_ref, b_ref, o_ref):
    c = c_ref[...]
    s = c * jax.nn.sigmoid(c)
    o_ref[...] = _dot(s.astype(BF16), w_ref[...].astype(BF16)) + b_ref[...]


def _modulation(cvecs, w_mod, b_mod):
    r, d = cvecs.shape
    n = w_mod.shape[1]
    tn = d
    out = pl.pallas_call(
        _mod_kernel,
        grid=(n // tn,),
        in_specs=[pl.BlockSpec((r, d), lambda j: (0, 0)),
                  pl.BlockSpec((d, tn), lambda j: (0, j)),
                  pl.BlockSpec((1, tn), lambda j: (0, j))],
        out_specs=pl.BlockSpec((r, tn), lambda j: (0, j)),
        out_shape=jax.ShapeDtypeStruct((r, n), F32),
        compiler_params=_params(1),
        name="adaln_mod",
    )(cvecs, w_mod, b_mod.reshape(1, n))
    return out.reshape(r, 1, n)


def _rope(x, cos, sin_signed):
    lane = lax.broadcasted_iota(jnp.int32, x.shape, 1)
    partner = jnp.where((lane & 32) == 0, pltpu.roll(x, LANES - 32, 1), pltpu.roll(x, 32, 1))
    return x * cos + partner * sin_signed


def _inproj_kernel(*refs, latent):
    if latent:
        (x_ref, mod_ref, g1_ref, win_ref, wg_ref, bg_ref, qg_ref, kg_ref, bd_ref, cos_ref, sin_ref,
         aq_ref, ak_ref, av_ref, mq_ref, mk_ref, mv_ref, mo_ref, gtt_ref) = refs
    else:
        (x_ref, mod_ref, g1_ref, win_ref, wg_ref, bg_ref, qg_ref, kg_ref, bd_ref,
         aq_ref, ak_ref, av_ref, mq_ref, mk_ref, mv_ref, mo_ref, gtt_ref, ck_ref, cv_ref,
         cks_ref, cvs_ref) = refs
    x = x_ref[...]
    d = x.shape[-1]
    mod = mod_ref[0]
    sh1, sc1 = mod[:, 0:d], mod[:, d:2 * d]
    y = x * lax.rsqrt(jnp.mean(x * x, axis=-1, keepdims=True) + EPS)
    h = (y * g1_ref[...]) * (1.0 + sc1) + sh1
    hb = h.astype(BF16)

    def head_norm(a, g):
        ss = _dot((a * a).astype(BF16), bd_ref[...])
        return a * lax.rsqrt(ss * (1.0 / ATT_HEAD_DIM) + EPS) * g

    w = ATT_WIDTH
    att = _dot(hb, win_ref[:, 0:3 * w])
    aq_ref[...] = (head_norm(att[:, 0:w], qg_ref[...]) * ATT_HEAD_DIM ** -0.5).astype(aq_ref.dtype)
    kn = head_norm(att[:, w:2 * w], kg_ref[...])
    av = att[:, 2 * w:3 * w]
    ak_ref[...] = kn.astype(BF16)
    if latent:
        av_ref[...] = av.T.astype(BF16)
    else:
        av_ref[...] = av.astype(BF16)
        tm = kn.shape[0]
        for src, dst, scr in ((kn, ck_ref, cks_ref), (av, cv_ref, cvs_ref)):
            for hh in range(N_ATT_HEADS):
                pair = src[:, (hh // 2) * LANES:(hh // 2 + 1) * LANES]
                if hh % 2:
                    pair = pltpu.roll(pair, ATT_HEAD_DIM, 1)
                scr[pl.ds(hh, tm, stride=N_ATT_HEADS), :] = pair
            dst[...] = scr[...].reshape(tm, N_ATT_HEADS, LANES)[:, :, 0:ATT_HEAD_DIM]

    w = ML_WIDTH
    ml0 = 3 * ATT_WIDTH
    mq = _dot(hb, win_ref[:, ml0:ml0 + w])
    mk = _dot(hb, win_ref[:, ml0 + w:ml0 + 2 * w]) * ML_HEAD_DIM ** -0.5
    if latent:
        cos, sin = cos_ref[...], sin_ref[...]
        for hh in range(N_ML_HEADS):
            sl = slice(hh * ML_HEAD_DIM, (hh + 1) * ML_HEAD_DIM)
            mq_ref[:, sl] = _rope(mq[:, sl], cos, sin).astype(BF16)
            mk_ref[:, sl] = _rope(mk[:, sl], cos, sin).astype(BF16)
    else:
        mq_ref[...] = mq.astype(BF16)
        mk_ref[...] = mk.astype(BF16)
    mv_ref[...] = _dot(hb, win_ref[:, ml0 + 2 * w:ml0 + 3 * w]).T.astype(BF16)
    mo_ref[...] = _dot(hb, win_ref[:, ml0 + 3 * w:ml0 + 4 * w]).T
    gates = _dot(hb, wg_ref[...]) + bg_ref[...]
    gtt_ref[...] = gates.T[0:2 * GATE_ROWS]


def _in_projection_parts(x, mod, mod_row, seq_len, wts, rope, *, latent):
    n_tok, d = x.shape
    tm = _token_tile(n_tok, seq_len)
    tiles_per_seq = max(seq_len // tm, 1)

    def tok(width):
        return pl.BlockSpec((tm, width), lambda i: (i, 0))

    in_specs = [tok(d),
                pl.BlockSpec((1, 1, mod.shape[-1]), lambda i: (mod_row(i, tm), 0, 0)),
                _const_spec((1, d)),
                _const_spec(wts["w_in"].shape),
                _const_spec(wts["w_g"].shape), _const_spec((1, LANES)),
                _const_spec((1, ATT_WIDTH)), _const_spec((1, ATT_WIDTH)),
                _const_spec((ATT_WIDTH, ATT_WIDTH))]
    args = [x, mod, wts["norm1_g"], wts["w_in"], wts["w_g"], wts["b_g"],
            wts["q_g"], wts["k_g"], wts["blockdiag"]]
    if latent:
        in_specs += [pl.BlockSpec((tm, LANES), lambda i: (i % tiles_per_seq, 0))] * 2
        args += [rope[0], rope[1]]
    out_shape = [jax.ShapeDtypeStruct((n_tok, ATT_WIDTH), BF16),
                 jax.ShapeDtypeStruct((n_tok, ATT_WIDTH), BF16),
                 jax.ShapeDtypeStruct((ATT_WIDTH, n_tok) if latent else (n_tok, ATT_WIDTH), BF16),
                 jax.ShapeDtypeStruct((n_tok, ML_WIDTH), BF16),
                 jax.ShapeDtypeStruct((n_tok, ML_WIDTH), BF16),
                 jax.ShapeDtypeStruct((ML_WIDTH, n_tok), BF16),
                 jax.ShapeDtypeStruct((ML_WIDTH, n_tok), F32),
                 jax.ShapeDtypeStruct((2 * GATE_ROWS, n_tok), F32)]

    def tok_t(height):
        return pl.BlockSpec((height, tm), lambda i: (0, i))

    av_spec = tok_t(ATT_WIDTH) if latent else tok(ATT_WIDTH)
    out_specs = ([tok(ATT_WIDTH)] * 2 + [av_spec] + [tok(ML_WIDTH)] * 2 + [tok_t(ML_WIDTH)] * 2
                 + [tok_t(2 * GATE_ROWS)])
    if not latent:
        cache = jax.ShapeDtypeStruct((n_tok, N_ATT_HEADS, ATT_HEAD_DIM), F32)
        out_shape += [cache, cache]
        out_specs += [pl.BlockSpec((tm, N_ATT_HEADS, ATT_HEAD_DIM), lambda i: (i, 0, 0))] * 2
    scratch = [] if latent else [pltpu.VMEM((tm * N_ATT_HEADS, LANES), F32)] * 2
    return n_tok // tm, in_specs, args, out_shape, out_specs, scratch


def _in_projection(x, mod, mod_row, seq_len, wts, rope, *, latent):
    n_tiles, in_specs, args, out_shape, out_specs, scratch = _in_projection_parts(
        x, mod, mod_row, seq_len, wts, rope, latent=latent)
    return pl.pallas_call(
        functools.partial(_inproj_kernel, latent=latent),
        grid=(n_tiles,),
        in_specs=in_specs,
        out_specs=out_specs,
        out_shape=out_shape,
        scratch_shapes=scratch,
        compiler_params=_params(1),
        name="in_proj_latent" if latent else "in_proj_ctx",
    )(*args)


def _inproj_cast_kernel(*refs, n_in, n_out):
    ip_in, (wup_ref, wdn_ref) = refs[:n_in], refs[n_in:n_in + 2]
    outs = refs[n_in + 2:]
    ip_out, (wup_o, wdn_o), scratch = outs[:n_out], outs[n_out:n_out + 2], outs[n_out + 2:]
    _inproj_kernel(*ip_in, *ip_out, *scratch, latent=False)
    _retile_cast_kernel(wup_ref, wup_o)
    wdn_o[0] = wdn_ref[...].astype(wdn_o.dtype)


def _ctx_in_projection_with_weight_casts(x, mod, mod_row, seq_len, wts, w_up, w_down, tn):
    n_tiles, in_specs, args, out_shape, out_specs, scratch = _in_projection_parts(
        x, mod, mod_row, seq_len, wts, None, latent=False)
    d, two_dff = w_up.shape
    nct = w_down.shape[0] // tn
    n_grp = two_dff // (2 * tn)
    assert two_dff % (2 * tn) == 0 and max(n_grp, nct) <= n_tiles
    cast_in = [pl.BlockSpec((d, 2 * tn), lambda i: (0, jnp.minimum(i, n_grp - 1))),
               pl.BlockSpec((tn, d), lambda i: (jnp.minimum(i, nct - 1), 0))]
    cast_out = [pl.BlockSpec((2, d, tn), lambda i: (jnp.minimum(i, n_grp - 1), 0, 0)),
                pl.BlockSpec((1, tn, d), lambda i: (jnp.minimum(i, nct - 1), 0, 0))]
    cast_shape = [jax.ShapeDtypeStruct((two_dff // tn, d, tn), BF16), jax.ShapeDtypeStruct((nct, tn, d), BF16)]
    outs = pl.pallas_call(
        functools.partial(_inproj_cast_kernel, n_in=len(in_specs), n_out=len(out_specs)),
        grid=(n_tiles,),
        in_specs=in_specs + cast_in,
        out_specs=out_specs + cast_out,
        out_shape=out_shape + cast_shape,
        scratch_shapes=scratch,
        compiler_params=_params(1),
        name="in_proj_ctx_weight_casts",
    )(*args, w_up, w_down)
    return outs[:len(out_specs)], outs[len(out_specs)], outs[len(out_specs) + 1]


def _ctx_attn_kernel(q_ref, k_ref, v_ref, o_ref):
    n = q_ref.shape[0]
    gw = HEAD_GROUP * ATT_HEAD_DIM
    lane_head = lax.broadcasted_iota(jnp.int32, (n, gw), 1) // ATT_HEAD_DIM
    for g in range(N_ATT_HEADS // HEAD_GROUP):
        sl = slice(g * gw, (g + 1) * gw)
        q4 = q_ref[:, sl]
        qbd = jnp.concatenate([jnp.where(lane_head == hl, q4, jnp.zeros_like(q4))
                               for hl in range(HEAD_GROUP)], axis=0)
        s = _dot_nt(k_ref[:, sl], qbd)
        p = jnp.exp(s - jnp.max(s, axis=0, keepdims=True))
        p = p / jnp.sum(p, axis=0, keepdims=True)
        o4 = _dot_tn(p.astype(BF16), v_ref[:, sl])
        out = jnp.where(lane_head == 0, o4[0:n], 0.0)
        for hl in range(1, HEAD_GROUP):
            out = out + jnp.where(lane_head == hl, o4[hl * n:(hl + 1) * n], 0.0)
        o_ref[:, sl] = out.astype(o_ref.dtype)


def _context_attention(q, k, v, seq_len):
    n_tok = q.shape[0]
    spec = pl.BlockSpec((seq_len, ATT_WIDTH), lambda b: (b, 0))
    return pl.pallas_call(
        _ctx_attn_kernel,
        grid=(n_tok // seq_len,),
        in_specs=[spec, spec, spec],
        out_specs=spec,
        out_shape=jax.ShapeDtypeStruct((n_tok, ATT_WIDTH), BF16),
        compiler_params=_params(1),
        name="ctx_attn",
    )(q, k, v)


def _nbr_attn_kernel(q_ref, k_ref, vt_ref, kc_ref, vct_ref, bias_ref, o_ref, pt_ref, *, rows, kr):
    _nbr_row(pl.program_id(1), q_ref, k_ref, vt_ref, kc_ref, vct_ref, bias_ref, o_ref, pt_ref, rows=rows, kr=kr)


def _nbr_row(r, q_ref, k_ref, vt_ref, kc_ref, vct_ref, bias_ref, o_ref, pt_ref, *, rows, kr):
    n_win = kr + 2
    n_loc = kr * GRID_W
    gw = HEAD_GROUP * ATT_HEAD_DIM
    groups = N_ATT_HEADS // HEAD_GROUP
    lane_head = lax.broadcasted_iota(jnp.int32, (GRID_W, gw), 1) // ATT_HEAD_DIM
    rs = jnp.clip(r - kr // 2, 0, rows - kr)
    start = jnp.minimum(rs - (rs & 1), rows - n_win)
    delta = rs - start
    bias0 = pl.multiple_of((WIN_ROWS - 1 - (r - rs)) * GRID_W, GRID_W)
    zeros2 = jnp.zeros((2 * GRID_W, gw), BF16)
    for g in range(groups):
        sl = slice(g * gw, (g + 1) * gw)
        q4 = q_ref[:, sl]
        qbd = jnp.concatenate([jnp.where(lane_head == hl, q4, jnp.zeros_like(q4))
                               for hl in range(HEAD_GROUP)], axis=0)
        s_loc = _dot_nt(k_ref[pl.ds(pl.multiple_of(rs * GRID_W, GRID_W), n_loc), sl], qbd)
        s_loc = s_loc + bias_ref[g, pl.ds(bias0, n_loc), :]
        s_ctx = _dot_nt(kc_ref[0, :, sl], qbd)
        m = jnp.maximum(jnp.max(s_loc, axis=0, keepdims=True), jnp.max(s_ctx, axis=0, keepdims=True))
        p_loc = jnp.exp(s_loc - m)
        p_ctx = jnp.exp(s_ctx - m)
        l = jnp.sum(p_loc, axis=0, keepdims=True) + jnp.sum(p_ctx, axis=0, keepdims=True)
        pt_ref[g, 0:2 * GRID_W, :] = zeros2
        pt_ref[g, n_loc:n_loc + 2 * GRID_W, :] = zeros2
        pt_ref[g, pl.ds(pl.multiple_of(delta * GRID_W, GRID_W), n_loc), :] = p_loc.astype(BF16)
        vt_win = vt_ref[sl, pl.ds(pl.multiple_of(start * GRID_W, 2 * GRID_W), n_win * GRID_W)]
        ot = _dot(vt_win, pt_ref[g]) + _dot(vct_ref[0, sl, :], p_ctx.astype(BF16))
        o4 = (ot / l).T
        out = jnp.where(lane_head == 0, o4[0:GRID_W], 0.0)
        for hl in range(1, HEAD_GROUP):
            out = out + jnp.where(lane_head == hl, o4[hl * GRID_W:(hl + 1) * GRID_W], 0.0)
        o_ref[:, sl] = out.astype(o_ref.dtype)


def _nbr_bias_table(rpb, rows):
    col = jnp.arange(GRID_W)
    cs = jnp.clip(col - WIN_COLS // 2, 0, GRID_W - WIN_COLS)
    in_win = (col[None, :] >= cs[:, None]) & (col[None, :] < cs[:, None] + WIN_COLS)
    dc_idx = jnp.clip(col[None, :] - col[:, None] + (WIN_COLS - 1), 0, 2 * WIN_COLS - 2)
    n_dr, n_dc = rpb.shape[1], rpb.shape[2]
    onehot = (dc_idx[None] == jnp.arange(n_dc)[:, None, None]).astype(F32)
    t = jnp.einsum("hrc,cqk->hrqk", rpb.astype(F32), onehot, precision=lax.Precision.HIGHEST)
    t = jnp.where(in_win[None, None], t, -jnp.inf)
    n_groups = N_ATT_HEADS // HEAD_GROUP
    t = t.reshape(n_groups, HEAD_GROUP, n_dr, GRID_W, GRID_W).transpose(0, 2, 4, 1, 3)
    return t.reshape(n_groups, n_dr * GRID_W, HEAD_GROUP * GRID_W)


def _neighborhood_attention(q, k, vt, k_ctx, vt_ctx, rpb, n_batch, seq_len):
    rows = seq_len // GRID_W
    kr = min(WIN_ROWS, rows)
    n_win = kr + 2
    assert rows >= n_win and (rows - n_win) % 2 == 0
    bias = _nbr_bias_table(rpb, rows)
    past = k_ctx.shape[1]
    gw = HEAD_GROUP * ATT_HEAD_DIM
    n_groups = N_ATT_HEADS // HEAD_GROUP
    row_spec = pl.BlockSpec((GRID_W, ATT_WIDTH), lambda b, r: (b * rows + r, 0))
    return pl.pallas_call(
        functools.partial(_nbr_attn_kernel, rows=rows, kr=kr),
        grid=(n_batch, rows),
        in_specs=[row_spec,
                  pl.BlockSpec((seq_len, ATT_WIDTH), lambda b, r: (b, 0)),
                  pl.BlockSpec((ATT_WIDTH, seq_len), lambda b, r: (0, b)),
                  pl.BlockSpec((1, past, ATT_WIDTH), lambda b, r: (b, 0, 0)),
                  pl.BlockSpec((1, ATT_WIDTH, past), lambda b, r: (b, 0, 0)),
                  _const_spec(bias.shape)],
        out_specs=row_spec,
        out_shape=jax.ShapeDtypeStruct((n_batch * seq_len, ATT_WIDTH), BF16),
        scratch_shapes=[pltpu.VMEM((n_groups, n_win * GRID_W, gw), BF16)],
        compiler_params=_params(2),
        name="nbr_attn",
    )(q, k, vt, k_ctx, vt_ctx, bias)


def _mlstm_kernel(*refs, chunk, n_chunks, has_state):
    dh = ML_HEAD_DIM
    nrep = 2 * GATE_ROWS
    if has_state:
        (q_ref, k_ref, vt_ref, mot_ref, gtt_ref, g_ref, c0_ref, n0_ref, m0_ref,
         o_ref, c_ref, n_ref, m_ref, hf_ref, hb_ref, cn_ref, mrun_ref) = refs
        for d in range(2):
            for hh in range(N_ML_HEADS):
                cn_ref[d, hh, 0:dh, :] = c0_ref[0, d, hh].T
                cn_ref[d, hh, dh:dh + nrep, :] = jnp.broadcast_to(n0_ref[0, d, hh], (nrep, dh))
        mrun_ref[...] = m0_ref[0]
    else:
        (q_ref, k_ref, vt_ref, mot_ref, gtt_ref, g_ref,
         o_ref, c_ref, n_ref, m_ref, hf_ref, hb_ref, cn_ref, mrun_ref) = refs
        cn_ref[...] = jnp.zeros_like(cn_ref)
        mrun_ref[...] = jnp.zeros_like(mrun_ref)
    use_state = has_state or n_chunks > 1

    i0 = lax.broadcasted_iota(jnp.int32, (chunk, chunk), 0)
    i1 = lax.broadcasted_iota(jnp.int32, (chunk, chunk), 1)
    row_id = lax.broadcasted_iota(jnp.int32, (GATE_ROWS, 1), 0)
    instances = [(hh, d) for d in range(2) for hh in range(N_ML_HEADS)]

    def split3(x):
        hi = x.astype(BF16)
        r1 = x - hi.astype(F32)
        mid = r1.astype(BF16)
        return hi, mid, (r1 - mid.astype(F32)).astype(BF16)

    def body(i, _):
        chunks = (i, n_chunks - 1 - i)
        rows = [pl.ds(pl.multiple_of(c * chunk, chunk), chunk) for c in chunks]
        le = [i0 <= i1, i0 >= i1]
        m_prev = mrun_ref[...][:, 0:1]

        def qkv(hh, d):
            sl = slice(hh * dh, (hh + 1) * dh)
            return q_ref[rows[d], sl], k_ref[rows[d], sl], vt_ref[sl, rows[d]]

        st = [_dot_nt(qkv(hh, d)[1], qkv(hh, d)[0]) for hh, d in instances]
        if use_state:
            cn = [cn_ref[d, hh] for hh, d in instances]
            qct = [_dot_nt(cn[n].astype(BF16), qkv(hh, d)[0]) for n, (hh, d) in enumerate(instances)]

        a_col, a8, cs8, w8, decay8, m_new = [], [], [], [], [], []
        for d in range(2):
            gi_r = gtt_ref[0:GATE_ROWS, rows[d]]
            lf_r = _log_sigmoid(gtt_ref[GATE_ROWS:2 * GATE_ROWS, rows[d]])
            tri_r = jnp.where(le[d], 1.0, 0.0).astype(BF16)
            cs_r = sum(_dot(t, tri_r) for t in split3(lf_r))
            b_last = jnp.sum(lf_r, axis=1, keepdims=True)
            w_end = b_last + gi_r - cs_r
            m_d = jnp.maximum(b_last + m_prev, jnp.max(w_end, axis=1, keepdims=True))
            a8.append(gi_r - cs_r)
            cs8.append(cs_r)
            decay8.append(jnp.exp(b_last + m_prev - m_d))
            w8.append(jnp.exp(w_end - m_d))
            m_new.append(m_d)
            pad = jnp.zeros((LANES - GATE_ROWS, chunk), F32)
            a_col.append(jnp.concatenate([gi_r - cs_r, pad], axis=0).T)
        m_next = jnp.where((row_id & 1) == 0, m_new[0], m_new[1])
        mrun_ref[...] = jnp.broadcast_to(m_next, (GATE_ROWS, LANES))

        pt_all, g_all, den_all = [], [], []
        for n, (hh, d) in enumerate(instances):
            j = 2 * hh + d
            a = jnp.where(le[d], a_col[d][:, j:j + 1], -jnp.inf)
            g = jnp.maximum(jnp.max(a, axis=0, keepdims=True), m_prev[j:j + 1])
            pt = st[n] * jnp.exp(a - g)
            pt_all.append(pt)
            g_all.append(g)
            den_all.append(jnp.sum(pt, axis=0, keepdims=True))

        for n, (hh, d) in enumerate(instances):
            j = 2 * hh + d
            sl = slice(hh * dh, (hh + 1) * dh)
            _, k, vt = qkv(hh, d)
            g, den = g_all[n], den_all[n]
            num = _dot(vt, pt_all[n].astype(BF16))
            if use_state:
                w_inter = jnp.exp(m_prev[j:j + 1] - g)
                num = num + w_inter * qct[n][0:dh]
                den = den + w_inter * qct[n][dh:dh + 1]
            scale = 1.0 / jnp.maximum(jnp.abs(den), jnp.exp(-(cs8[d][j:j + 1] + g)))
            (hb_ref if d else hf_ref)[sl, rows[d]] = num * scale
            w_row = w8[d][j:j + 1]
            lhs = jnp.concatenate([(vt.astype(F32) * w_row).astype(BF16),
                                   jnp.broadcast_to(w_row, (nrep, chunk)).astype(BF16)], axis=0)
            upd = _dot(lhs, k)
            if use_state:
                upd = decay8[d][j:j + 1] * cn[n] + upd
            cn_ref[d, hh] = upd
        return 0

    def finish(c, _):
        rows = pl.ds(pl.multiple_of(c * chunk, chunk), chunk)
        for hh in range(N_ML_HEADS):
            sl = slice(hh * dh, (hh + 1) * dh)
            hs = hf_ref[sl, rows] + hb_ref[sl, rows]
            y = hs * lax.rsqrt(jnp.mean(hs * hs, axis=0, keepdims=True) + EPS) * g_ref[sl, :]
            y = y * jax.nn.sigmoid(mot_ref[sl, rows])
            o_ref[rows, sl] = y.T.astype(o_ref.dtype)
        return 0

    if n_chunks == 1:
        body(0, 0)
        finish(0, 0)
    else:
        lax.fori_loop(0, n_chunks, body, 0)
        lax.fori_loop(0, n_chunks, finish, 0)
    for d in range(2):
        for hh in range(N_ML_HEADS):
            cn = cn_ref[d, hh]
            c_ref[0, d, hh] = cn[0:dh].T
            n_ref[0, d, hh] = cn[dh:dh + 1]
    m_ref[0] = mrun_ref[...]


def _mlstm(mq, mk, mvt, mot, gates_t, ml_g, state, n_batch, seq_len, chunk):
    dh = ML_HEAD_DIM
    nh = N_ML_HEADS
    width = nh * dh
    has_state = state is not None
    seq = pl.BlockSpec((seq_len, width), lambda b: (b, 0))
    seq_t = pl.BlockSpec((width, seq_len), lambda b: (0, b))
    st_c = pl.BlockSpec((1, 2, nh, dh, dh), lambda b: (b, 0, 0, 0, 0))
    st_n = pl.BlockSpec((1, 2, nh, 1, dh), lambda b: (b, 0, 0, 0, 0))
    st_m = pl.BlockSpec((1, GATE_ROWS, LANES), lambda b: (b, 0, 0))
    in_specs = [seq, seq, seq_t, seq_t,
                pl.BlockSpec((2 * GATE_ROWS, seq_len), lambda b: (0, b)),
                _const_spec((width, chunk))]
    args = [mq, mk, mvt, mot, gates_t, jnp.broadcast_to(ml_g.reshape(width, 1), (width, chunk))]
    if has_state:
        in_specs += [st_c, st_n, st_m]
        args += list(state)
    out_shape = [jax.ShapeDtypeStruct((n_batch * seq_len, width), BF16),
                 jax.ShapeDtypeStruct((n_batch, 2, nh, dh, dh), F32),
                 jax.ShapeDtypeStruct((n_batch, 2, nh, 1, dh), F32),
                 jax.ShapeDtypeStruct((n_batch, GATE_ROWS, LANES), F32)]
    return pl.pallas_call(
        functools.partial(_mlstm_kernel, chunk=chunk, n_chunks=seq_len // chunk, has_state=has_state),
        grid=(n_batch,),
        in_specs=in_specs,
        out_specs=[seq, st_c, st_n, st_m],
        out_shape=out_shape,
        scratch_shapes=[pltpu.VMEM((width, seq_len), F32), pltpu.VMEM((width, seq_len), F32),
                        pltpu.VMEM((2, nh, dh + 2 * GATE_ROWS, dh), F32), pltpu.VMEM((GATE_ROWS, LANES), F32)],
        compiler_params=_params(1),
        name="mlstm_latent" if has_state else "mlstm_ctx",
    )(*args)


def _pack_gate_rows(m):
    b = m.shape[0]
    packed = m.transpose(0, 2, 1).reshape(b, 2 * N_ML_HEADS, 1)
    return jnp.broadcast_to(packed, (b, GATE_ROWS, LANES))


def _unpack_gate_rows(m):
    b = m.shape[0]
    return m[:, :, 0].reshape(b, N_ML_HEADS, 2).transpose(0, 2, 1)


def _inproj_ctxmix_kernel(*refs, n_in, n_out, n_seq, seq_len, chunk):
    ip_in = refs[:n_in]
    q_ref, k_ref, v_ref, mq_ref, mk_ref, mvt_ref, mot_ref, gtt_ref, g_ref = refs[n_in:n_in + 9]
    outs = refs[n_in + 9:]
    ip_out = outs[:n_out]
    oatt_ref, oml_ref, c_ref, n_ref, m_ref = outs[n_out:n_out + 5]
    scratch = outs[n_out + 5:]
    _inproj_kernel(*ip_in, *ip_out, latent=True)
    for s in range(n_seq):
        rows = slice(s * seq_len, (s + 1) * seq_len)
        _ctx_attn_kernel(q_ref.at[rows, :], k_ref.at[rows, :], v_ref.at[rows, :], oatt_ref.at[rows, :])
        _mlstm_kernel(mq_ref.at[rows, :], mk_ref.at[rows, :], mvt_ref.at[:, rows], mot_ref.at[:, rows],
                      gtt_ref.at[:, rows], g_ref, oml_ref.at[rows, :], c_ref.at[s:s + 1], n_ref.at[s:s + 1],
                      m_ref.at[s:s + 1], *scratch[4 * s:4 * s + 4],
                      chunk=chunk, n_chunks=seq_len // chunk, has_state=False)


def _latent_in_projection_with_ctx_mixers(x, mod, mod_row, seq_len, wts, rope, ctx, n_ctx_batch, ctx_seq):
    n_tiles, in_specs, args, out_shape, out_specs, scratch = _in_projection_parts(
        x, mod, mod_row, seq_len, wts, rope, latent=True)
    n_seq = n_ctx_batch // n_tiles
    rows = n_seq * ctx_seq
    n_ctx_tok = n_ctx_batch * ctx_seq
    chunk = min(ML_CHUNK, ctx_seq)
    nh, dh = N_ML_HEADS, ML_HEAD_DIM

    def tok(width):
        return pl.BlockSpec((rows, width), lambda i: (i, 0))

    def tok_t(height):
        return pl.BlockSpec((height, rows), lambda i: (0, i))

    def per_seq(*tail):
        return pl.BlockSpec((n_seq,) + tail, lambda i: (i,) + (0,) * len(tail))

    aq, ak, av, mq, mk, mvt, mot, gtt = ctx
    mix_in_specs = ([tok(ATT_WIDTH)] * 3 + [tok(ML_WIDTH)] * 2 + [tok_t(ML_WIDTH)] * 2
                    + [tok_t(2 * GATE_ROWS), _const_spec((ML_WIDTH, chunk))])
    mix_args = [aq, ak, av, mq, mk, mvt, mot, gtt,
                jnp.broadcast_to(wts["ml_g"].reshape(ML_WIDTH, 1), (ML_WIDTH, chunk))]
    mix_out_shape = [jax.ShapeDtypeStruct((n_ctx_tok, ATT_WIDTH), BF16),
                     jax.ShapeDtypeStruct((n_ctx_tok, ML_WIDTH), BF16),
                     jax.ShapeDtypeStruct((n_ctx_batch, 2, nh, dh, dh), F32),
                     jax.ShapeDtypeStruct((n_ctx_batch, 2, nh, 1, dh), F32),
                     jax.ShapeDtypeStruct((n_ctx_batch, GATE_ROWS, LANES), F32)]
    mix_out_specs = [tok(ATT_WIDTH), tok(ML_WIDTH), per_seq(2, nh, dh, dh), per_seq(2, nh, 1, dh),
                     per_seq(GATE_ROWS, LANES)]
    mix_scratch = [pltpu.VMEM((ML_WIDTH, ctx_seq), F32), pltpu.VMEM((ML_WIDTH, ctx_seq), F32),
                   pltpu.VMEM((2, nh, dh + 2 * GATE_ROWS, dh), F32), pltpu.VMEM((GATE_ROWS, LANES), F32)] * n_seq
    outs = pl.pallas_call(
        functools.partial(_inproj_ctxmix_kernel, n_in=len(in_specs), n_out=len(out_specs), n_seq=n_seq,
                          seq_len=ctx_seq, chunk=chunk),
        grid=(n_tiles,),
        in_specs=in_specs + mix_in_specs,
        out_specs=out_specs + mix_out_specs,
        out_shape=out_shape + mix_out_shape,
        scratch_shapes=scratch + mix_scratch,
        compiler_params=_params(1),
        name="in_proj_latent_ctx_mixers",
    )(*args, *mix_args)
    return outs[:len(out_specs)], outs[len(out_specs):]


def _ffn_kernel(oa_ref, oap_ref, oan_ref, om_ref, omp_ref, omn_ref, x_ref, xp_ref, xn_ref, mod_ref, g2_ref,
                wout_ref, wup_ref, cw_ref, cb_ref, wd_ref,
                y_ref, oc_ref, lhs_ref, x1_ref, ug_ref, uv_ref, acc_ref, *, seq_len, n_col_tiles, side_work=None):
    tm, d = x_ref.shape
    aw = oa_ref.shape[1]

    def halo(next_ref, prev_ref):
        row = lax.broadcasted_iota(jnp.int32, next_ref.shape, 0)
        return jnp.where(row < HALO // 2, next_ref[...], prev_ref[...])

    oc_ref[0:tm, 0:aw] = oa_ref[...]
    oc_ref[0:tm, aw:] = om_ref[...]
    oc_ref[tm:tm + HALO, 0:aw] = halo(oan_ref, oap_ref)
    oc_ref[tm:tm + HALO, aw:] = halo(omn_ref, omp_ref)
    mod = mod_ref[0]
    g1 = mod[:, 2 * d:3 * d]
    sh2, sc2 = mod[:, 3 * d:4 * d], mod[:, 4 * d:5 * d]
    out = _dot(oc_ref[...], wout_ref[...])

    def norm2(x1):
        y = x1 * lax.rsqrt(jnp.mean(x1 * x1, axis=-1, keepdims=True) + EPS)
        return ((y * g2_ref[...]) * (1.0 + sc2) + sh2).astype(BF16)

    x1 = x_ref[...] + g1 * out[0:tm]
    x1_ref[...] = x1
    lhs_ref[0:tm, :] = norm2(x1)
    lhs_ref[tm:tm + HALO, :] = norm2(halo(xn_ref, xp_ref) + g1 * out[tm:tm + HALO])
    acc_ref[...] = jnp.zeros_like(acc_ref)
    sub = 8
    period = min(seq_len, tm)
    first_groups = sorted({r // sub for r in range(0, tm, period)})
    last_groups = sorted({(r + period - 1) // sub for r in range(0, tm, period)})
    tile0 = pl.program_id(0) * tm

    def zero_rows(x, groups, target):
        parts, at = [], 0
        for grp in groups:
            lo = grp * sub
            if lo > at:
                parts.append(x[at:lo])
            pos = (tile0 + lo + lax.broadcasted_iota(jnp.int32, (sub, 1), 0)) % seq_len
            parts.append(jnp.where(pos == target, 0.0, x[lo:lo + sub]))
            at = lo + sub
        if at < tm:
            parts.append(x[at:tm])
        return jnp.concatenate(parts, axis=0)

    def conv(u, cw, cb):
        prev = zero_rows(pltpu.roll(u, 1, 0)[0:tm], first_groups, 0)
        nxt = zero_rows(pltpu.roll(u, tm + HALO - 1, 0)[0:tm], last_groups, seq_len - 1)
        return prev * cw[0:1] + u[0:tm] * cw[1:2] + nxt * cw[2:3] + cb

    def up(j, slot):
        lhs = lhs_ref[...]
        ug_ref[slot] = _dot(lhs, wup_ref[j])
        uv_ref[slot] = _dot(lhs, wup_ref[n_col_tiles + j])

    def act(j, slot):
        gate = conv(ug_ref[slot], cw_ref[j], cb_ref[j])
        val = conv(uv_ref[slot], cw_ref[n_col_tiles + j], cb_ref[n_col_tiles + j])
        return ((gate * jax.nn.sigmoid(gate)) * val).astype(BF16)

    def stage(j, slot, next_up):
        a = act(j, slot)
        if next_up:
            up(j + 1, 1 - slot)
        return _dot(a, wd_ref[j])

    def pair(i, _):
        acc_ref[...] += stage(2 * i, 0, True)
        acc_ref[...] += stage(2 * i + 1, 1, True)
        if side_work is not None:
            side_work("loop", i)
        return 0

    up(0, 0)
    if side_work is not None:
        side_work("before", 0)
    n_pairs = (n_col_tiles - 1) // 2
    if n_pairs:
        lax.fori_loop(0, n_pairs, pair, 0)
    j0 = 2 * n_pairs
    if n_col_tiles - j0 == 2:
        acc_ref[...] += stage(j0, 0, True)
        tail = stage(j0 + 1, 1, False)
    else:
        tail = stage(j0, 0, False)
    g2 = mod_ref[0][:, 5 * d:6 * d]
    y_ref[...] = x1_ref[...] + g2 * (acc_ref[...] + tail)
    if side_work is not None:
        side_work("after", 0)


def _mix_ffn_parts(o_att, o_ml, x, mod, mod_row, seq_len, wts):
    n_tok, d = x.shape
    tm = _token_tile(n_tok, seq_len)
    hpt = tm // HALO
    n_halo = n_tok // HALO
    nct, tn, _ = wts["w_down"].shape
    mix = o_att.shape[1] + o_ml.shape[1]

    def tile_and_halos(width):
        return [pl.BlockSpec((tm, width), lambda i: (i, 0)),
                pl.BlockSpec((HALO, width), lambda i: (jnp.maximum(i * hpt - 1, 0), 0)),
                pl.BlockSpec((HALO, width), lambda i: (jnp.minimum((i + 1) * hpt, n_halo - 1), 0))]

    in_specs = (tile_and_halos(o_att.shape[1]) + tile_and_halos(o_ml.shape[1]) + tile_and_halos(d) + [
        pl.BlockSpec((1, 1, mod.shape[-1]), lambda i: (mod_row(i, tm), 0, 0)),
        _const_spec((1, d)), _const_spec((mix, d)),
        _const_spec((2 * nct, d, tn)), _const_spec((2 * nct, 3, tn)), _const_spec((2 * nct, 1, tn)),
        _const_spec((nct, tn, d))])
    args = [o_att, o_att, o_att, o_ml, o_ml, o_ml, x, x, x, mod, wts["norm2_g"], wts["w_out"],
            wts["w_up"], wts["conv_w"], wts["conv_b"], wts["w_down"]]
    scratch = [pltpu.VMEM((tm + HALO, mix), BF16), pltpu.VMEM((tm + HALO, d), BF16), pltpu.VMEM((tm, d), F32),
               pltpu.VMEM((2, tm + HALO, tn), F32), pltpu.VMEM((2, tm + HALO, tn), F32), pltpu.VMEM((tm, d), F32)]
    return (n_tok // tm, nct, in_specs, args, pl.BlockSpec((tm, d), lambda i: (i, 0)),
            jax.ShapeDtypeStruct((n_tok, d), F32), scratch)


def _mix_ffn(o_att, o_ml, x, mod, mod_row, seq_len, wts):
    n_tiles, nct, in_specs, args, out_spec, out_shape, scratch = _mix_ffn_parts(
        o_att, o_ml, x, mod, mod_row, seq_len, wts)
    return pl.pallas_call(
        functools.partial(_ffn_kernel, seq_len=seq_len, n_col_tiles=nct),
        grid=(n_tiles,),
        in_specs=in_specs,
        out_specs=out_spec,
        out_shape=out_shape,
        scratch_shapes=scratch,
        compiler_params=_params(1),
        name="mix_ffn",
    )(*args)


def _ffn_nbr_kernel(*refs, n_in, seq_len, n_col_tiles, rows, kr, rows_per_tile, tiles_per_batch):
    ffn_in = refs[:n_in]
    q_ref, k_ref, vt_ref, kc_ref, vct_ref, bias_ref = refs[n_in:n_in + 6]
    y_ref, o_ref = refs[n_in + 6:n_in + 8]
    scratch = refs[n_in + 8:]
    ffn_scratch, pt_refs = scratch[:-2], scratch[-2:]
    n_pairs = (n_col_tiles - 1) // 2
    base = (pl.program_id(0) % tiles_per_batch) * rows_per_tile

    def row(local, pt_ref):
        rr = pl.ds(pl.multiple_of(local * GRID_W, GRID_W), GRID_W)
        _nbr_row(base + local, q_ref.at[rr, :], k_ref, vt_ref, kc_ref, vct_ref, bias_ref, o_ref.at[rr, :], pt_ref,
                 rows=rows, kr=kr)

    n_before = (3 * rows_per_tile) // 4

    def side_work(where, i):
        if where == "before":
            for local in range(n_before):
                row(local, pt_refs[local % 2])
        elif where == "after":
            for local in range(n_before, rows_per_tile):
                row(local, pt_refs[local % 2])

    _ffn_kernel(*ffn_in, y_ref, *ffn_scratch, seq_len=seq_len, n_col_tiles=n_col_tiles, side_work=side_work)


def _mix_ffn_with_nbr_attention(o_att, o_ml, x, mod, mod_row, seq_len, wts, q, k, vt, k_ctx, vt_ctx, rpb,
                                n_batch, nbr_seq):
    n_tiles, nct, in_specs, args, out_spec, out_shape, scratch = _mix_ffn_parts(
        o_att, o_ml, x, mod, mod_row, seq_len, wts)
    rows = nbr_seq // GRID_W
    kr = min(WIN_ROWS, rows)
    n_win = kr + 2
    assert rows >= n_win and (rows - n_win) % 2 == 0
    rpt = (n_batch * rows) // n_tiles
    tpb = n_tiles // n_batch
    bias = _nbr_bias_table(rpb, rows)
    past = k_ctx.shape[1]
    gw = HEAD_GROUP * ATT_HEAD_DIM
    n_groups = N_ATT_HEADS // HEAD_GROUP
    row_blk = pl.BlockSpec((rpt * GRID_W, ATT_WIDTH), lambda i: (i, 0))
    once = pl.Buffered(1)
    nbr_specs = [row_blk,
                 pl.BlockSpec((nbr_seq, ATT_WIDTH), lambda i: (i // tpb, 0), pipeline_mode=once),
                 pl.BlockSpec((ATT_WIDTH, nbr_seq), lambda i: (0, i // tpb), pipeline_mode=once),
                 pl.BlockSpec((1, past, ATT_WIDTH), lambda i: (i // tpb, 0, 0), pipeline_mode=once),
                 pl.BlockSpec((1, ATT_WIDTH, past), lambda i: (i // tpb, 0, 0), pipeline_mode=once),
                 _const_spec(bias.shape)]
    pt = pltpu.VMEM((n_groups, n_win * GRID_W, gw), BF16)
    y, o_lat = pl.pallas_call(
        functools.partial(_ffn_nbr_kernel, n_in=len(in_specs), seq_len=seq_len, n_col_tiles=nct, rows=rows, kr=kr,
                          rows_per_tile=rpt, tiles_per_batch=tpb),
        grid=(n_tiles,),
        in_specs=in_specs + nbr_specs,
        out_specs=[out_spec, row_blk],
        out_shape=[out_shape, jax.ShapeDtypeStruct((n_batch * nbr_seq, ATT_WIDTH), BF16)],
        scratch_shapes=scratch + [pt, pt],
        compiler_params=_params(1),
        name="mix_ffn_ctx_nbr_attn",
    )(*args, q, k, vt, k_ctx, vt_ctx, bias)
    return y, o_lat


def _retile_cast_kernel(w_ref, o_ref):
    tn = o_ref.shape[-1]
    for t in range(o_ref.shape[0]):
        o_ref[t] = w_ref[:, t * tn:(t + 1) * tn].astype(o_ref.dtype)


def _retile_cast(w, tn):
    r, c = w.shape
    group = 2 if (c // tn) % 2 == 0 else 1
    return pl.pallas_call(
        _retile_cast_kernel,
        grid=(c // (tn * group),),
        in_specs=[pl.BlockSpec((r, tn * group), lambda j: (0, j))],
        out_specs=pl.BlockSpec((group, r, tn), lambda j: (j, 0, 0)),
        out_shape=jax.ShapeDtypeStruct((c // tn, r, tn), BF16),
        compiler_params=_params(1),
        name="retile_cast",
    )(w)


def _cast_kernel(w_ref, o_ref):
    o_ref[...] = w_ref[...].astype(o_ref.dtype)


def _cast_rows(w, block_rows):
    r, c = w.shape
    spec = pl.BlockSpec((block_rows, c), lambda i: (i, 0))
    return pl.pallas_call(
        _cast_kernel,
        grid=(r // block_rows,),
        in_specs=[spec],
        out_specs=spec,
        out_shape=jax.ShapeDtypeStruct((r, c), BF16),
        compiler_params=_params(1),
        name="cast_rows",
    )(w)


def _layer_weights(norm1_g, norm2_g, w_in, b_gate, q_g, k_g, ml_g, w_out, w_up, conv_w, conv_b, w_down):
    d = w_in.shape[0]
    a, m, nh = ATT_WIDTH, ML_WIDTH, N_ML_HEADS
    def gate_lanes(g):
        g = g.reshape(g.shape[0], 2, 2, nh).transpose(0, 2, 3, 1).reshape(g.shape[0], 4 * nh)
        return jnp.pad(g, ((0, 0), (0, LANES - 4 * nh)))

    w_g = gate_lanes(w_in[:, 3 * a + 4 * m:])
    b_g = gate_lanes(b_gate.astype(F32).reshape(1, 4 * nh))
    head_id = jnp.arange(a) // ATT_HEAD_DIM
    d_ff = w_down.shape[0]
    tn = MXU_WIDTH
    nct = d_ff // tn

    def col_tiles(w):
        return w.reshape(w.shape[0], 2 * nct, tn).transpose(1, 0, 2)

    return {
        "norm1_g": norm1_g.reshape(1, d), "norm2_g": norm2_g.reshape(1, d),
        "w_in": w_in.astype(BF16),
        "w_g": w_g.astype(BF16), "b_g": b_g,
        "q_g": jnp.tile(q_g, N_ATT_HEADS).reshape(1, a), "k_g": jnp.tile(k_g, N_ATT_HEADS).reshape(1, a),
        "blockdiag": (head_id[:, None] == head_id[None, :]).astype(BF16),
        "ml_g": ml_g.reshape(1, m),
        "w_out": w_out.astype(BF16),
        "conv_w": col_tiles(conv_w.astype(F32)), "conv_b": col_tiles(conv_b.astype(F32).reshape(1, -1)),
        "w_up_f32": w_up, "w_down_f32": w_down, "ffn_tile": tn,
    }


def _rope_tables(seq_len):
    quarter = ML_HEAD_DIM // 4
    pos = np.arange(seq_len)
    inv_freq = ROPE_THETA ** (-np.arange(quarter, dtype=np.float64) / quarter)
    ang_r = (pos // GRID_W).astype(np.float64)[:, None] * inv_freq[None, :]
    ang_c = (pos % GRID_W).astype(np.float64)[:, None] * inv_freq[None, :]
    cos = np.concatenate([np.cos(ang_r)] * 2 + [np.cos(ang_c)] * 2, axis=-1)
    sin = np.concatenate([-np.sin(ang_r), np.sin(ang_r), -np.sin(ang_c), np.sin(ang_c)], axis=-1)
    return jnp.asarray(cos, F32), jnp.asarray(sin, F32)


def _layer(xp, xs, mod, wts, rope, ctx_kv, rpb, state, batch, seq, dec_batch, dec_seq):
    def ctx_row(i, tm):
        return 0 * i

    def lat_row(i, tm):
        return 1 + (i * tm) // dec_seq

    w_up, w_down, tn = wts["w_up_f32"], wts["w_down_f32"], wts["ffn_tile"]
    n_ctx_tiles = (batch * seq) // _token_tile(batch * seq, seq)
    if n_ctx_tiles >= max(w_up.shape[1] // (2 * tn), w_down.shape[0] // tn) and (w_up.shape[1] // tn) % 2 == 0:
        ctx, w_up_t, w_down_t = _ctx_in_projection_with_weight_casts(xp, mod, ctx_row, seq, wts, w_up, w_down, tn)
    else:
        ctx = _in_projection(xp, mod, ctx_row, seq, wts, None, latent=False)
        w_up_t = _retile_cast(w_up, tn)
        w_down_t = _cast_rows(w_down, tn).reshape(w_down.shape[0] // tn, tn, w_down.shape[1])
    wts = dict(wts, w_up=w_up_t, w_down=w_down_t)
    cache_k, cache_v = ctx[8], ctx[9]
    n_lat_tiles = (dec_batch * dec_seq) // _token_tile(dec_batch * dec_seq, dec_seq)
    if batch % n_lat_tiles == 0:
        lat, (o_att_c, o_ml_c, c_f, n_f, m_f) = _latent_in_projection_with_ctx_mixers(
            xs, mod, lat_row, dec_seq, wts, rope, ctx[:8], batch, seq)
    else:
        aq, ak, av, mq, mk, mvt, mot, gtt = ctx[:8]
        o_att_c = _context_attention(aq, ak, av, seq)
        o_ml_c, c_f, n_f, m_f = _mlstm(mq, mk, mvt, mot, gtt, wts["ml_g"], None, batch, seq, min(ML_CHUNK, seq))
        lat = _in_projection(xs, mod, lat_row, dec_seq, wts, rope, latent=True)
    aq, ak, avt, mq, mk, mvt, mot, gtt = lat[:8]
    n_rows = dec_batch * (dec_seq // GRID_W)
    n_pairs = (wts["w_down"].shape[0] - 1) // 2
    if n_rows % n_ctx_tiles == 0 and n_ctx_tiles % dec_batch == 0 and n_rows // n_ctx_tiles > n_pairs:
        xp, o_att = _mix_ffn_with_nbr_attention(o_att_c, o_ml_c, xp, mod, ctx_row, seq, wts, aq, ak, avt,
                                                ctx_kv[0], ctx_kv[1], rpb, dec_batch, dec_seq)
    else:
        xp = _mix_ffn(o_att_c, o_ml_c, xp, mod, ctx_row, seq, wts)
        o_att = _neighborhood_attention(aq, ak, avt, ctx_kv[0], ctx_kv[1], rpb, dec_batch, dec_seq)
    o_ml = _mlstm(mq, mk, mvt, mot, gtt, wts["ml_g"], state, dec_batch, dec_seq, min(ML_CHUNK, dec_seq))[0]
    xs = _mix_ffn(o_att, o_ml, xs, mod, lat_row, dec_seq, wts)
    return xp, xs, cache_k, cache_v, (c_f, n_f[:, :, :, 0, :], _unpack_gate_rows(m_f))


def kernel(x_prompt, x_sample, cache_k, cache_v, state_C, state_n, state_m, c, c_ctx, w_mod, b_mod, norm1_g,
           norm2_g, w_in, b_gate, q_norm_g, k_norm_g, rpb, ml_norm_g, w_out, w_up, conv_w, conv_b, w_down):
    batch, seq, d = x_prompt.shape
    dec_batch, dec_seq, _ = x_sample.shape
    depth = w_mod.shape[0]
    past = cache_k.shape[2]
    cvecs = jnp.concatenate([c_ctx[None, :], c], axis=0).astype(F32)
    rope = _rope_tables(dec_seq)

    xp = x_prompt.reshape(batch * seq, d)
    xs = x_sample.reshape(dec_batch * dec_seq, d)
    ks, vs, cs, ns, ms = [], [], [], [], []
    for l in range(depth):
        wts = _layer_weights(norm1_g[l], norm2_g[l], w_in[l], b_gate[l], q_norm_g[l], k_norm_g[l],
                             ml_norm_g[l], w_out[l], w_up[l], conv_w[l], conv_b[l], w_down[l])
        mod = _modulation(cvecs, w_mod[l], b_mod[l])

        state = (state_C[:, l].astype(F32),
                 state_n[:, l].astype(F32)[:, :, :, None, :],
                 _pack_gate_rows(state_m[:, l].astype(F32)))
        ctx_kv = (cache_k[:, l].reshape(dec_batch, past, ATT_WIDTH).astype(BF16),
                  cache_v[:, l].reshape(dec_batch, past, ATT_WIDTH).transpose(0, 2, 1).astype(BF16))
        xp, xs, ak, av, st = _layer(xp, xs, mod, wts, rope, ctx_kv, rpb[l], state, batch, seq, dec_batch, dec_seq)
        ks.append(ak.reshape(batch, seq, N_ATT_HEADS, ATT_HEAD_DIM))
        vs.append(av.reshape(batch, seq, N_ATT_HEADS, ATT_HEAD_DIM))
        cs.append(st[0])
        ns.append(st[1])
        ms.append(st[2])
    return (xp.reshape(batch, seq, d), xs.reshape(dec_batch, dec_seq, d),
            jnp.stack(ks, axis=1), jnp.stack(vs, axis=1),
            jnp.stack(cs, axis=1), jnp.stack(ns, axis=1), jnp.stack(ms, axis=1))
```

```python
import functools

import jax
import jax.numpy as jnp
import numpy as np
from jax import lax
from jax.experimental import pallas as pl
from jax.experimental.pallas import tpu as pltpu

F32 = jnp.float32
BF16 = jnp.bfloat16

GRID_W = 64
N_ATT_HEADS = 8
ATT_HEAD_DIM = 64
ATT_WIDTH = N_ATT_HEADS * ATT_HEAD_DIM
WIN_ROWS = 8
WIN_COLS = 16
HEAD_GROUP = 4
N_ML_HEADS = 4
ML_HEAD_DIM = 128
ML_WIDTH = N_ML_HEADS * ML_HEAD_DIM
ROPE_THETA = 10000.0
EPS = 1e-6

LANES = 128
MXU_WIDTH = 256
ML_CHUNK = 256
GATE_ROWS = 8
HALO = 16
TOKEN_TILE = 512
VMEM_LIMIT = 56 * 1024 * 1024


def _dot(a, b):
    return jnp.dot(a, b, preferred_element_type=F32)


def _dot_nt(a, b):
    return lax.dot_general(a, b, (((1,), (1,)), ((), ())), preferred_element_type=F32)


def _dot_tn(a, b):
    return lax.dot_general(a, b, (((0,), (0,)), ((), ())), preferred_element_type=F32)


def _const_spec(shape):
    nd = len(shape)
    return pl.BlockSpec(shape, lambda *_: (0,) * nd, pipeline_mode=pl.Buffered(1))


def _params(n_axes):
    return pltpu.CompilerParams(dimension_semantics=("arbitrary",) * n_axes,
                                vmem_limit_bytes=VMEM_LIMIT)


def _token_tile(n_tok, seq_len):
    tm = min(TOKEN_TILE, n_tok)
    while n_tok % tm or (seq_len % tm and tm % seq_len):
        tm //= 2
    return tm


def _log_sigmoid(x):
    return jnp.minimum(x, 0.0) - jnp.log1p(jnp.exp(-jnp.abs(x)))


def _mod_kernel(c_ref, w_ref, b_ref, o_ref):
    c = c_ref[...]
    s = c * jax.nn.sigmoid(c)
    o_ref[...] = _dot(s.astype(BF16), w_ref[...].astype(BF16)) + b_ref[...]


def _modulation(cvecs, w_mod, b_mod):
    r, d = cvecs.shape
    n = w_mod.shape[1]
    tn = d
    out = pl.pallas_call(
        _mod_kernel,
        grid=(n // tn,),
        in_specs=[pl.BlockSpec((r, d), lambda j: (0, 0)),
                  pl.BlockSpec((d, tn), lambda j: (0, j)),
                  pl.BlockSpec((1, tn), lambda j: (0, j))],
        out_specs=pl.BlockSpec((r, tn), lambda j: (0, j)),
        out_shape=jax.ShapeDtypeStruct((r, n), F32),
        compiler_params=_params(1),
        name="adaln_mod",
    )(cvecs, w_mod, b_mod.reshape(1, n))
    return out.reshape(r, 1, n)


def _rope(x, cos, sin_signed):
    lane = lax.broadcasted_iota(jnp.int32, x.shape, 1)
    partner = jnp.where((lane & 32) == 0, pltpu.roll(x, LANES - 32, 1), pltpu.roll(x, 32, 1))
    return x * cos + partner * sin_signed


def _inproj_kernel(*refs, latent):
    if latent:
        (x_ref, mod_ref, g1_ref, win_ref, wg_ref, bg_ref, qg_ref, kg_ref, bd_ref, cos_ref, sin_ref,
         aq_ref, ak_ref, av_ref, mq_ref, mk_ref, mv_ref, mo_ref, gtt_ref) = refs
    else:
        (x_ref, mod_ref, g1_ref, win_ref, wg_ref, bg_ref, qg_ref, kg_ref, bd_ref,
         aq_ref, ak_ref, av_ref, mq_ref, mk_ref, mv_ref, mo_ref, gtt_ref, ck_ref, cv_ref,
         cks_ref, cvs_ref) = refs
    x = x_ref[...]
    d = x.shape[-1]
    mod = mod_ref[0]
    sh1, sc1 = mod[:, 0:d], mod[:, d:2 * d]
    y = x * lax.rsqrt(jnp.mean(x * x, axis=-1, keepdims=True) + EPS)
    h = (y * g1_ref[...]) * (1.0 + sc1) + sh1
    hb = h.astype(BF16)

    def head_norm(a, g):
        ss = _dot((a * a).astype(BF16), bd_ref[...])
        return a * lax.rsqrt(ss * (1.0 / ATT_HEAD_DIM) + EPS) * g

    w = ATT_WIDTH
    att = _dot(hb, win_ref[:, 0:3 * w])
    aq_ref[...] = (head_norm(att[:, 0:w], qg_ref[...]) * ATT_HEAD_DIM ** -0.5).astype(aq_ref.dtype)
    kn = head_norm(att[:, w:2 * w], kg_ref[...])
    av = att[:, 2 * w:3 * w]
    ak_ref[...] = kn.astype(BF16)
    if latent:
        av_ref[...] = av.T.astype(BF16)
    else:
        av_ref[...] = av.astype(BF16)
        tm = kn.shape[0]
        for src, dst, scr in ((kn, ck_ref, cks_ref), (av, cv_ref, cvs_ref)):
            for hh in range(N_ATT_HEADS):
                pair = src[:, (hh // 2) * LANES:(hh // 2 + 1) * LANES]
                if hh % 2:
                    pair = pltpu.roll(pair, ATT_HEAD_DIM, 1)
                scr[pl.ds(hh, tm, stride=N_ATT_HEADS), :] = pair
            dst[...] = scr[...].reshape(tm, N_ATT_HEADS, LANES)[:, :, 0:ATT_HEAD_DIM]

    w = ML_WIDTH
    ml0 = 3 * ATT_WIDTH
    mq = _dot(hb, win_ref[:, ml0:ml0 + w])
    mk = _dot(hb, win_ref[:, ml0 + w:ml0 + 2 * w]) * ML_HEAD_DIM ** -0.5
    if latent:
        cos, sin = cos_ref[...], sin_ref[...]
        for hh in range(N_ML_HEADS):
            sl = slice(hh * ML_HEAD_DIM, (hh + 1) * ML_HEAD_DIM)
            mq_ref[:, sl] = _rope(mq[:, sl], cos, sin).astype(BF16)
            mk_ref[:, sl] = _rope(mk[:, sl], cos, sin).astype(BF16)
    else:
        mq_ref[...] = mq.astype(BF16)
        mk_ref[...] = mk.astype(BF16)
    mv_ref[...] = _dot(hb, win_ref[:, ml0 + 2 * w:ml0 + 3 * w]).T.astype(BF16)
    mo_ref[...] = _dot(hb, win_ref[:, ml0 + 3 * w:ml0 + 4 * w]).T
    gates = _dot(hb, wg_ref[...]) + bg_ref[...]
    gtt_ref[...] = gates.T[0:2 * GATE_ROWS]


def _in_projection_parts(x, mod, mod_row, seq_len, wts, rope, *, latent):
    n_tok, d = x.shape
    tm = _token_tile(n_tok, seq_len)
    tiles_per_seq = max(seq_len // tm, 1)

    def tok(width):
        return pl.BlockSpec((tm, width), lambda i: (i, 0))

    in_specs = [tok(d),
                pl.BlockSpec((1, 1, mod.shape[-1]), lambda i: (mod_row(i, tm), 0, 0)),
                _const_spec((1, d)),
                _const_spec(wts["w_in"].shape),
                _const_spec(wts["w_g"].shape), _const_spec((1, LANES)),
                _const_spec((1, ATT_WIDTH)), _const_spec((1, ATT_WIDTH)),
                _const_spec((ATT_WIDTH, ATT_WIDTH))]
    args = [x, mod, wts["norm1_g"], wts["w_in"], wts["w_g"], wts["b_g"],
            wts["q_g"], wts["k_g"], wts["blockdiag"]]
    if latent:
        in_specs += [pl.BlockSpec((tm, LANES), lambda i: (i % tiles_per_seq, 0))] * 2
        args += [rope[0], rope[1]]
    out_shape = [jax.ShapeDtypeStruct((n_tok, ATT_WIDTH), BF16),
                 jax.ShapeDtypeStruct((n_tok, ATT_WIDTH), BF16),
                 jax.ShapeDtypeStruct((ATT_WIDTH, n_tok) if latent else (n_tok, ATT_WIDTH), BF16),
                 jax.ShapeDtypeStruct((n_tok, ML_WIDTH), BF16),
                 jax.ShapeDtypeStruct((n_tok, ML_WIDTH), BF16),
                 jax.ShapeDtypeStruct((ML_WIDTH, n_tok), BF16),
                 jax.ShapeDtypeStruct((ML_WIDTH, n_tok), F32),
                 jax.ShapeDtypeStruct((2 * GATE_ROWS, n_tok), F32)]

    def tok_t(height):
        return pl.BlockSpec((height, tm), lambda i: (0, i))

    av_spec = tok_t(ATT_WIDTH) if latent else tok(ATT_WIDTH)
    out_specs = ([tok(ATT_WIDTH)] * 2 + [av_spec] + [tok(ML_WIDTH)] * 2 + [tok_t(ML_WIDTH)] * 2
                 + [tok_t(2 * GATE_ROWS)])
    if not latent:
        cache = jax.ShapeDtypeStruct((n_tok, N_ATT_HEADS, ATT_HEAD_DIM), F32)
        out_shape += [cache, cache]
        out_specs += [pl.BlockSpec((tm, N_ATT_HEADS, ATT_HEAD_DIM), lambda i: (i, 0, 0))] * 2
    scratch = [] if latent else [pltpu.VMEM((tm * N_ATT_HEADS, LANES), F32)] * 2
    return n_tok // tm, in_specs, args, out_shape, out_specs, scratch


def _in_projection(x, mod, mod_row, seq_len, wts, rope, *, latent):
    n_tiles, in_specs, args, out_shape, out_specs, scratch = _in_projection_parts(
        x, mod, mod_row, seq_len, wts, rope, latent=latent)
    return pl.pallas_call(
        functools.partial(_inproj_kernel, latent=latent),
        grid=(n_tiles,),
        in_specs=in_specs,
        out_specs=out_specs,
        out_shape=out_shape,
        scratch_shapes=scratch,
        compiler_params=_params(1),
        name="in_proj_latent" if latent else "in_proj_ctx",
    )(*args)


def _inproj_cast_kernel(*refs, n_in, n_out):
    ip_in, (wup_ref, wdn_ref) = refs[:n_in], refs[n_in:n_in + 2]
    outs = refs[n_in + 2:]
    ip_out, (wup_o, wdn_o), scratch = outs[:n_out], outs[n_out:n_out + 2], outs[n_out + 2:]
    _inproj_kernel(*ip_in, *ip_out, *scratch, latent=False)
    _retile_cast_kernel(wup_ref, wup_o)
    wdn_o[0] = wdn_ref[...].astype(wdn_o.dtype)


def _ctx_in_projection_with_weight_casts(x, mod, mod_row, seq_len, wts, w_up, w_down, tn):
    n_tiles, in_specs, args, out_shape, out_specs, scratch = _in_projection_parts(
        x, mod, mod_row, seq_len, wts, None, latent=False)
    d, two_dff = w_up.shape
    nct = w_down.shape[0] // tn
    n_grp = two_dff // (2 * tn)
    assert two_dff % (2 * tn) == 0 and max(n_grp, nct) <= n_tiles
    cast_in = [pl.BlockSpec((d, 2 * tn), lambda i: (0, jnp.minimum(i, n_grp - 1))),
               pl.BlockSpec((tn, d), lambda i: (jnp.minimum(i, nct - 1), 0))]
    cast_out = [pl.BlockSpec((2, d, tn), lambda i: (jnp.minimum(i, n_grp - 1), 0, 0)),
                pl.BlockSpec((1, tn, d), lambda i: (jnp.minimum(i, nct - 1), 0, 0))]
    cast_shape = [jax.ShapeDtypeStruct((two_dff // tn, d, tn), BF16), jax.ShapeDtypeStruct((nct, tn, d), BF16)]
    outs = pl.pallas_call(
        functools.partial(_inproj_cast_kernel, n_in=len(in_specs), n_out=len(out_specs)),
        grid=(n_tiles,),
        in_specs=in_specs + cast_in,
        out_specs=out_specs + cast_out,
        out_shape=out_shape + cast_shape,
        scratch_shapes=scratch,
        compiler_params=_params(1),
        name="in_proj_ctx_weight_casts",
    )(*args, w_up, w_down)
    return outs[:len(out_specs)], outs[len(out_specs)], outs[len(out_specs) + 1]


def _ctx_attn_kernel(q_ref, k_ref, v_ref, o_ref):
    n = q_ref.shape[0]
    gw = HEAD_GROUP * ATT_HEAD_DIM
    lane_head = lax.broadcasted_iota(jnp.int32, (n, gw), 1) // ATT_HEAD_DIM
    for g in range(N_ATT_HEADS // HEAD_GROUP):
        sl = slice(g * gw, (g + 1) * gw)
        q4 = q_ref[:, sl]
        qbd = jnp.concatenate([jnp.where(lane_head == hl, q4, jnp.zeros_like(q4))
                               for hl in range(HEAD_GROUP)], axis=0)
        s = _dot_nt(k_ref[:, sl], qbd)
        p = jnp.exp(s - jnp.max(s, axis=0, keepdims=True))
        p = p / jnp.sum(p, axis=0, keepdims=True)
        o4 = _dot_tn(p.astype(BF16), v_ref[:, sl])
        out = jnp.where(lane_head == 0, o4[0:n], 0.0)
        for hl in range(1, HEAD_GROUP):
            out = out + jnp.where(lane_head == hl, o4[hl * n:(hl + 1) * n], 0.0)
        o_ref[:, sl] = out.astype(o_ref.dtype)


def _context_attention(q, k, v, seq_len):
    n_tok = q.shape[0]
    spec = pl.BlockSpec((seq_len, ATT_WIDTH), lambda b: (b, 0))
    return pl.pallas_call(
        _ctx_attn_kernel,
        grid=(n_tok // seq_len,),
        in_specs=[spec, spec, spec],
        out_specs=spec,
        out_shape=jax.ShapeDtypeStruct((n_tok, ATT_WIDTH), BF16),
        compiler_params=_params(1),
        name="ctx_attn",
    )(q, k, v)


def _nbr_attn_kernel(q_ref, k_ref, vt_ref, kc_ref, vct_ref, bias_ref, o_ref, pt_ref, *, rows, kr):
    _nbr_row(pl.program_id(1), q_ref, k_ref, vt_ref, kc_ref, vct_ref, bias_ref, o_ref, pt_ref, rows=rows, kr=kr)


def _nbr_row(r, q_ref, k_ref, vt_ref, kc_ref, vct_ref, bias_ref, o_ref, pt_ref, *, rows, kr):
    n_win = kr + 2
    n_loc = kr * GRID_W
    gw = HEAD_GROUP * ATT_HEAD_DIM
    groups = N_ATT_HEADS // HEAD_GROUP
    lane_head = lax.broadcasted_iota(jnp.int32, (GRID_W, gw), 1) // ATT_HEAD_DIM
    rs = jnp.clip(r - kr // 2, 0, rows - kr)
    start = jnp.minimum(rs - (rs & 1), rows - n_win)
    delta = rs - start
    bias0 = pl.multiple_of((WIN_ROWS - 1 - (r - rs)) * GRID_W, GRID_W)
    zeros2 = jnp.zeros((2 * GRID_W, gw), BF16)
    for g in range(groups):
        sl = slice(g * gw, (g + 1) * gw)
        q4 = q_ref[:, sl]
        qbd = jnp.concatenate([jnp.where(lane_head == hl, q4, jnp.zeros_like(q4))
                               for hl in range(HEAD_GROUP)], axis=0)
        s_loc = _dot_nt(k_ref[pl.ds(pl.multiple_of(rs * GRID_W, GRID_W), n_loc), sl], qbd)
        s_loc = s_loc + bias_ref[g, pl.ds(bias0, n_loc), :]
        s_ctx = _dot_nt(kc_ref[0, :, sl], qbd)
        m = jnp.maximum(jnp.max(s_loc, axis=0, keepdims=True), jnp.max(s_ctx, axis=0, keepdims=True))
        p_loc = jnp.exp(s_loc - m)
        p_ctx = jnp.exp(s_ctx - m)
        l = jnp.sum(p_loc, axis=0, keepdims=True) + jnp.sum(p_ctx, axis=0, keepdims=True)
        pt_ref[g, 0:2 * GRID_W, :] = zeros2
        pt_ref[g, n_loc:n_loc + 2 * GRID_W, :] = zeros2
        pt_ref[g, pl.ds(pl.multiple_of(delta * GRID_W, GRID_W), n_loc), :] = p_loc.astype(BF16)
        vt_win = vt_ref[sl, pl.ds(pl.multiple_of(start * GRID_W, 2 * GRID_W), n_win * GRID_W)]
        ot = _dot(vt_win, pt_ref[g]) + _dot(vct_ref[0, sl, :], p_ctx.astype(BF16))
        o4 = (ot / l).T
        out = jnp.where(lane_head == 0, o4[0:GRID_W], 0.0)
        for hl in range(1, HEAD_GROUP):
            out = out + jnp.where(lane_head == hl, o4[hl * GRID_W:(hl + 1) * GRID_W], 0.0)
        o_ref[:, sl] = out.astype(o_ref.dtype)


def _nbr_bias_table(rpb, rows):
    col = jnp.arange(GRID_W)
    cs = jnp.clip(col - WIN_COLS // 2, 0, GRID_W - WIN_COLS)
    in_win = (col[None, :] >= cs[:, None]) & (col[None, :] < cs[:, None] + WIN_COLS)
    dc_idx = jnp.clip(col[None, :] - col[:, None] + (WIN_COLS - 1), 0, 2 * WIN_COLS - 2)
    n_dr, n_dc = rpb.shape[1], rpb.shape[2]
    onehot = (dc_idx[None] == jnp.arange(n_dc)[:, None, None]).astype(F32)
    t = jnp.einsum("hrc,cqk->hrqk", rpb.astype(F32), onehot, precision=lax.Precision.HIGHEST)
    t = jnp.where(in_win[None, None], t, -jnp.inf)
    n_groups = N_ATT_HEADS // HEAD_GROUP
    t = t.reshape(n_groups, HEAD_GROUP, n_dr, GRID_W, GRID_W).transpose(0, 2, 4, 1, 3)
    return t.reshape(n_groups, n_dr * GRID_W, HEAD_GROUP * GRID_W)


def _neighborhood_attention(q, k, vt, k_ctx, vt_ctx, rpb, n_batch, seq_len):
    rows = seq_len // GRID_W
    kr = min(WIN_ROWS, rows)
    n_win = kr + 2
    assert rows >= n_win and (rows - n_win) % 2 == 0
    bias = _nbr_bias_table(rpb, rows)
    past = k_ctx.shape[1]
    gw = HEAD_GROUP * ATT_HEAD_DIM
    n_groups = N_ATT_HEADS // HEAD_GROUP
    row_spec = pl.BlockSpec((GRID_W, ATT_WIDTH), lambda b, r: (b * rows + r, 0))
    return pl.pallas_call(
        functools.partial(_nbr_attn_kernel, rows=rows, kr=kr),
        grid=(n_batch, rows),
        in_specs=[row_spec,
                  pl.BlockSpec((seq_len, ATT_WIDTH), lambda b, r: (b, 0)),
                  pl.BlockSpec((ATT_WIDTH, seq_len), lambda b, r: (0, b)),
                  pl.BlockSpec((1, past, ATT_WIDTH), lambda b, r: (b, 0, 0)),
                  pl.BlockSpec((1, ATT_WIDTH, past), lambda b, r: (b, 0, 0)),
                  _const_spec(bias.shape)],
        out_specs=row_spec,
        out_shape=jax.ShapeDtypeStruct((n_batch * seq_len, ATT_WIDTH), BF16),
        scratch_shapes=[pltpu.VMEM((n_groups, n_win * GRID_W, gw), BF16)],
        compiler_params=_params(2),
        name="nbr_attn",
    )(q, k, vt, k_ctx, vt_ctx, bias)


def _mlstm_kernel(*refs, chunk, n_chunks, has_state):
    dh = ML_HEAD_DIM
    nrep = 2 * GATE_ROWS
    if has_state:
        (q_ref, k_ref, vt_ref, mot_ref, gtt_ref, g_ref, c0_ref, n0_ref, m0_ref,
         o_ref, c_ref, n_ref, m_ref, hf_ref, hb_ref, cn_ref, mrun_ref) = refs
        for d in range(2):
            for hh in range(N_ML_HEADS):
                cn_ref[d, hh, 0:dh, :] = c0_ref[0, d, hh].T
                cn_ref[d, hh, dh:dh + nrep, :] = jnp.broadcast_to(n0_ref[0, d, hh], (nrep, dh))
        mrun_ref[...] = m0_ref[0]
    else:
        (q_ref, k_ref, vt_ref, mot_ref, gtt_ref, g_ref,
         o_ref, c_ref, n_ref, m_ref, hf_ref, hb_ref, cn_ref, mrun_ref) = refs
        cn_ref[...] = jnp.zeros_like(cn_ref)
        mrun_ref[...] = jnp.zeros_like(mrun_ref)
    use_state = has_state or n_chunks > 1

    i0 = lax.broadcasted_iota(jnp.int32, (chunk, chunk), 0)
    i1 = lax.broadcasted_iota(jnp.int32, (chunk, chunk), 1)
    row_id = lax.broadcasted_iota(jnp.int32, (GATE_ROWS, 1), 0)
    instances = [(hh, d) for d in range(2) for hh in range(N_ML_HEADS)]

    def split3(x):
        hi = x.astype(BF16)
        r1 = x - hi.astype(F32)
        mid = r1.astype(BF16)
        return hi, mid, (r1 - mid.astype(F32)).astype(BF16)

    def body(i, _):
        chunks = (i, n_chunks - 1 - i)
        rows = [pl.ds(pl.multiple_of(c * chunk, chunk), chunk) for c in chunks]
        le = [i0 <= i1, i0 >= i1]
        m_prev = mrun_ref[...][:, 0:1]

        def qkv(hh, d):
            sl = slice(hh * dh, (hh + 1) * dh)
            return q_ref[rows[d], sl], k_ref[rows[d], sl], vt_ref[sl, rows[d]]

        st = [_dot_nt(qkv(hh, d)[1], qkv(hh, d)[0]) for hh, d in instances]
        if use_state:
            cn = [cn_ref[d, hh] for hh, d in instances]
            qct = [_dot_nt(cn[n].astype(BF16), qkv(hh, d)[0]) for n, (hh, d) in enumerate(instances)]

        a_col, a8, cs8, w8, decay8, m_new = [], [], [], [], [], []
        for d in range(2):
            gi_r = gtt_ref[0:GATE_ROWS, rows[d]]
            lf_r = _log_sigmoid(gtt_ref[GATE_ROWS:2 * GATE_ROWS, rows[d]])
            tri_r = jnp.where(le[d], 1.0, 0.0).astype(BF16)
            cs_r = sum(_dot(t, tri_r) for t in split3(lf_r))
            b_last = jnp.sum(lf_r, axis=1, keepdims=True)
            w_end = b_last + gi_r - cs_r
            m_d = jnp.maximum(b_last + m_prev, jnp.max(w_end, axis=1, keepdims=True))
            a8.append(gi_r - cs_r)
            cs8.append(cs_r)
            decay8.append(jnp.exp(b_last + m_prev - m_d))
            w8.append(jnp.exp(w_end - m_d))
            m_new.append(m_d)
            pad = jnp.zeros((LANES - GATE_ROWS, chunk), F32)
            a_col.append(jnp.concatenate([gi_r - cs_r, pad], axis=0).T)
        m_next = jnp.where((row_id & 1) == 0, m_new[0], m_new[1])
        mrun_ref[...] = jnp.broadcast_to(m_next, (GATE_ROWS, LANES))

        pt_all, g_all, den_all = [], [], []
        for n, (hh, d) in enumerate(instances):
            j = 2 * hh + d
            a = jnp.where(le[d], a_col[d][:, j:j + 1], -jnp.inf)
            g = jnp.maximum(jnp.max(a, axis=0, keepdims=True), m_prev[j:j + 1])
            pt = st[n] * jnp.exp(a - g)
            pt_all.append(pt)
            g_all.append(g)
            den_all.append(jnp.sum(pt, axis=0, keepdims=True))

        for n, (hh, d) in enumerate(instances):
            j = 2 * hh + d
            sl = slice(hh * dh, (hh + 1) * dh)
            _, k, vt = qkv(hh, d)
            g, den = g_all[n], den_all[n]
            num = _dot(vt, pt_all[n].astype(BF16))
            if use_state:
                w_inter = jnp.exp(m_prev[j:j + 1] - g)
                num = num + w_inter * qct[n][0:dh]
                den = den + w_inter * qct[n][dh:dh + 1]
            scale = 1.0 / jnp.maximum(jnp.abs(den), jnp.exp(-(cs8[d][j:j + 1] + g)))
            (hb_ref if d else hf_ref)[sl, rows[d]] = num * scale
            w_row = w8[d][j:j + 1]
            lhs = jnp.concatenate([(vt.astype(F32) * w_row).astype(BF16),
                                   jnp.broadcast_to(w_row, (nrep, chunk)).astype(BF16)], axis=0)
            upd = _dot(lhs, k)
            if use_state:
                upd = decay8[d][j:j + 1] * cn[n] + upd
            cn_ref[d, hh] = upd
        return 0

    def finish(c, _):
        rows = pl.ds(pl.multiple_of(c * chunk, chunk), chunk)
        for hh in range(N_ML_HEADS):
            sl = slice(hh * dh, (hh + 1) * dh)
            hs = hf_ref[sl, rows] + hb_ref[sl, rows]
            y = hs * lax.rsqrt(jnp.mean(hs * hs, axis=0, keepdims=True) + EPS) * g_ref[sl, :]
            y = y * jax.nn.sigmoid(mot_ref[sl, rows])
            o_ref[rows, sl] = y.T.astype(o_ref.dtype)
        return 0

    if n_chunks == 1:
        body(0, 0)
        finish(0, 0)
    else:
        lax.fori_loop(0, n_chunks, body, 0)
        lax.fori_loop(0, n_chunks, finish, 0)
    for d in range(2):
        for hh in range(N_ML_HEADS):
            cn = cn_ref[d, hh]
            c_ref[0, d, hh] = cn[0:dh].T
            n_ref[0, d, hh] = cn[dh:dh + 1]
    m_ref[0] = mrun_ref[...]


def _mlstm(mq, mk, mvt, mot, gates_t, ml_g, state, n_batch, seq_len, chunk):
    dh = ML_HEAD_DIM
    nh = N_ML_HEADS
    width = nh * dh
    has_state = state is not None
    seq = pl.BlockSpec((seq_len, width), lambda b: (b, 0))
    seq_t = pl.BlockSpec((width, seq_len), lambda b: (0, b))
    st_c = pl.BlockSpec((1, 2, nh, dh, dh), lambda b: (b, 0, 0, 0, 0))
    st_n = pl.BlockSpec((1, 2, nh, 1, dh), lambda b: (b, 0, 0, 0, 0))
    st_m = pl.BlockSpec((1, GATE_ROWS, LANES), lambda b: (b, 0, 0))
    in_specs = [seq, seq, seq_t, seq_t,
                pl.BlockSpec((2 * GATE_ROWS, seq_len), lambda b: (0, b)),
                _const_spec((width, chunk))]
    args = [mq, mk, mvt, mot, gates_t, jnp.broadcast_to(ml_g.reshape(width, 1), (width, chunk))]
    if has_state:
        in_specs += [st_c, st_n, st_m]
        args += list(state)
    out_shape = [jax.ShapeDtypeStruct((n_batch * seq_len, width), BF16),
                 jax.ShapeDtypeStruct((n_batch, 2, nh, dh, dh), F32),
                 jax.ShapeDtypeStruct((n_batch, 2, nh, 1, dh), F32),
                 jax.ShapeDtypeStruct((n_batch, GATE_ROWS, LANES), F32)]
    return pl.pallas_call(
        functools.partial(_mlstm_kernel, chunk=chunk, n_chunks=seq_len // chunk, has_state=has_state),
        grid=(n_batch,),
        in_specs=in_specs,
        out_specs=[seq, st_c, st_n, st_m],
        out_shape=out_shape,
        scratch_shapes=[pltpu.VMEM((width, seq_len), F32), pltpu.VMEM((width, seq_len), F32),
                        pltpu.VMEM((2, nh, dh + 2 * GATE_ROWS, dh), F32), pltpu.VMEM((GATE_ROWS, LANES), F32)],
        compiler_params=_params(1),
        name="mlstm_latent" if has_state else "mlstm_ctx",
    )(*args)


def _pack_gate_rows(m):
    b = m.shape[0]
    packed = m.transpose(0, 2, 1).reshape(b, 2 * N_ML_HEADS, 1)
    return jnp.broadcast_to(packed, (b, GATE_ROWS, LANES))


def _unpack_gate_rows(m):
    b = m.shape[0]
    return m[:, :, 0].reshape(b, N_ML_HEADS, 2).transpose(0, 2, 1)


def _inproj_ctxmix_kernel(*refs, n_in, n_out, n_seq, seq_len, chunk):
    ip_in = refs[:n_in]
    q_ref, k_ref, v_ref, mq_ref, mk_ref, mvt_ref, mot_ref, gtt_ref, g_ref = refs[n_in:n_in + 9]
    outs = refs[n_in + 9:]
    ip_out = outs[:n_out]
    oatt_ref, oml_ref, c_ref, n_ref, m_ref = outs[n_out:n_out + 5]
    scratch = outs[n_out + 5:]
    _inproj_kernel(*ip_in, *ip_out, latent=True)
    for s in range(n_seq):
        rows = slice(s * seq_len, (s + 1) * seq_len)
        _ctx_attn_kernel(q_ref.at[rows, :], k_ref.at[rows, :], v_ref.at[rows, :], oatt_ref.at[rows, :])
        _mlstm_kernel(mq_ref.at[rows, :], mk_ref.at[rows, :], mvt_ref.at[:, rows], mot_ref.at[:, rows],
                      gtt_ref.at[:, rows], g_ref, oml_ref.at[rows, :], c_ref.at[s:s + 1], n_ref.at[s:s + 1],
                      m_ref.at[s:s + 1], *scratch[4 * s:4 * s + 4],
                      chunk=chunk, n_chunks=seq_len // chunk, has_state=False)


def _latent_in_projection_with_ctx_mixers(x, mod, mod_row, seq_len, wts, rope, ctx, n_ctx_batch, ctx_seq):
    n_tiles, in_specs, args, out_shape, out_specs, scratch = _in_projection_parts(
        x, mod, mod_row, seq_len, wts, rope, latent=True)
    n_seq = n_ctx_batch // n_tiles
    rows = n_seq * ctx_seq
    n_ctx_tok = n_ctx_batch * ctx_seq
    chunk = min(ML_CHUNK, ctx_seq)
    nh, dh = N_ML_HEADS, ML_HEAD_DIM

    def tok(width):
        return pl.BlockSpec((rows, width), lambda i: (i, 0))

    def tok_t(height):
        return pl.BlockSpec((height, rows), lambda i: (0, i))

    def per_seq(*tail):
        return pl.BlockSpec((n_seq,) + tail, lambda i: (i,) + (0,) * len(tail))

    aq, ak, av, mq, mk, mvt, mot, gtt = ctx
    mix_in_specs = ([tok(ATT_WIDTH)] * 3 + [tok(ML_WIDTH)] * 2 + [tok_t(ML_WIDTH)] * 2
                    + [tok_t(2 * GATE_ROWS), _const_spec((ML_WIDTH, chunk))])
    mix_args = [aq, ak, av, mq, mk, mvt, mot, gtt,
                jnp.broadcast_to(wts["ml_g"].reshape(ML_WIDTH, 1), (ML_WIDTH, chunk))]
    mix_out_shape = [jax.ShapeDtypeStruct((n_ctx_tok, ATT_WIDTH), BF16),
                     jax.ShapeDtypeStruct((n_ctx_tok, ML_WIDTH), BF16),
                     jax.ShapeDtypeStruct((n_ctx_batch, 2, nh, dh, dh), F32),
                     jax.ShapeDtypeStruct((n_ctx_batch, 2, nh, 1, dh), F32),
                     jax.ShapeDtypeStruct((n_ctx_batch, GATE_ROWS, LANES), F32)]
    mix_out_specs = [tok(ATT_WIDTH), tok(ML_WIDTH), per_seq(2, nh, dh, dh), per_seq(2, nh, 1, dh),
                     per_seq(GATE_ROWS, LANES)]
    mix_scratch = [pltpu.VMEM((ML_WIDTH, ctx_seq), F32), pltpu.VMEM((ML_WIDTH, ctx_seq), F32),
                   pltpu.VMEM((2, nh, dh + 2 * GATE_ROWS, dh), F32), pltpu.VMEM((GATE_ROWS, LANES), F32)] * n_seq
    outs = pl.pallas_call(
        functools.partial(_inproj_ctxmix_kernel, n_in=len(in_specs), n_out=len(out_specs), n_seq=n_seq,
                          seq_len=ctx_seq, chunk=chunk),
        grid=(n_tiles,),
        in_specs=in_specs + mix_in_specs,
        out_specs=out_specs + mix_out_specs,
        out_shape=out_shape + mix_out_shape,
        scratch_shapes=scratch + mix_scratch,
        compiler_params=_params(1),
        name="in_proj_latent_ctx_mixers",
    )(*args, *mix_args)
    return outs[:len(out_specs)], outs[len(out_specs):]


def _ffn_kernel(oa_ref, oap_ref, oan_ref, om_ref, omp_ref, omn_ref, x_ref, xp_ref, xn_ref, mod_ref, g2_ref,
                wout_ref, wup_ref, cw_ref, cb_ref, wd_ref,
                y_ref, oc_ref, lhs_ref, x1_ref, ug_ref, uv_ref, acc_ref, *, seq_len, n_col_tiles, side_work=None):
    tm, d = x_ref.shape
    aw = oa_ref.shape[1]

    def halo(next_ref, prev_ref):
        row = lax.broadcasted_iota(jnp.int32, next_ref.shape, 0)
        return jnp.where(row < HALO // 2, next_ref[...], prev_ref[...])

    oc_ref[0:tm, 0:aw] = oa_ref[...]
    oc_ref[0:tm, aw:] = om_ref[...]
    oc_ref[tm:tm + HALO, 0:aw] = halo(oan_ref, oap_ref)
    oc_ref[tm:tm + HALO, aw:] = halo(omn_ref, omp_ref)
    mod = mod_ref[0]
    g1 = mod[:, 2 * d:3 * d]
    sh2, sc2 = mod[:, 3 * d:4 * d], mod[:, 4 * d:5 * d]
    out = _dot(oc_ref[...], wout_ref[...])

    def norm2(x1):
        y = x1 * lax.rsqrt(jnp.mean(x1 * x1, axis=-1, keepdims=True) + EPS)
        return ((y * g2_ref[...]) * (1.0 + sc2) + sh2).astype(BF16)

    x1 = x_ref[...] + g1 * out[0:tm]
    x1_ref[...] = x1
    lhs_ref[0:tm, :] = norm2(x1)
    lhs_ref[tm:tm + HALO, :] = norm2(halo(xn_ref, xp_ref) + g1 * out[tm:tm + HALO])
    acc_ref[...] = jnp.zeros_like(acc_ref)
    sub = 8
    period = min(seq_len, tm)
    first_groups = sorted({r // sub for r in range(0, tm, period)})
    last_groups = sorted({(r + period - 1) // sub for r in range(0, tm, period)})
    tile0 = pl.program_id(0) * tm

    def zero_rows(x, groups, target):
        parts, at = [], 0
        for grp in groups:
            lo = grp * sub
            if lo > at:
                parts.append(x[at:lo])
            pos = (tile0 + lo + lax.broadcasted_iota(jnp.int32, (sub, 1), 0)) % seq_len
            parts.append(jnp.where(pos == target, 0.0, x[lo:lo + sub]))
            at = lo + sub
        if at < tm:
            parts.append(x[at:tm])
        return jnp.concatenate(parts, axis=0)

    def conv(u, cw, cb):
        prev = zero_rows(pltpu.roll(u, 1, 0)[0:tm], first_groups, 0)
        nxt = zero_rows(pltpu.roll(u, tm + HALO - 1, 0)[0:tm], last_groups, seq_len - 1)
        return prev * cw[0:1] + u[0:tm] * cw[1:2] + nxt * cw[2:3] + cb

    def up(j, slot):
        lhs = lhs_ref[...]
        ug_ref[slot] = _dot(lhs, wup_ref[j])
        uv_ref[slot] = _dot(lhs, wup_ref[n_col_tiles + j])

    def act(j, slot):
        gate = conv(ug_ref[slot], cw_ref[j], cb_ref[j])
        val = conv(uv_ref[slot], cw_ref[n_col_tiles + j], cb_ref[n_col_tiles + j])
        return ((gate * jax.nn.sigmoid(gate)) * val).astype(BF16)

    def stage(j, slot, next_up):
        a = act(j, slot)
        if next_up:
            up(j + 1, 1 - slot)
        return _dot(a, wd_ref[j])

    up(0, 0)
    if side_work is not None:
        side_work("before")
    n_pairs = (n_col_tiles - 1) // 2
    for i in range(n_pairs):
        acc_ref[...] += stage(2 * i, 0, True)
        acc_ref[...] += stage(2 * i + 1, 1, True)
    j0 = 2 * n_pairs
    if n_col_tiles - j0 == 2:
        acc_ref[...] += stage(j0, 0, True)
        tail = stage(j0 + 1, 1, False)
    else:
        tail = stage(j0, 0, False)
    g2 = mod_ref[0][:, 5 * d:6 * d]
    y_ref[...] = x1_ref[...] + g2 * (acc_ref[...] + tail)
    if side_work is not None:
        side_work("after")


def _mix_ffn_parts(o_att, o_ml, x, mod, mod_row, seq_len, wts):
    n_tok, d = x.shape
    tm = _token_tile(n_tok, seq_len)
    hpt = tm // HALO
    n_halo = n_tok // HALO
    nct, tn, _ = wts["w_down"].shape
    mix = o_att.shape[1] + o_ml.shape[1]

    def tile_and_halos(width):
        return [pl.BlockSpec((tm, width), lambda i: (i, 0)),
                pl.BlockSpec((HALO, width), lambda i: (jnp.maximum(i * hpt - 1, 0), 0)),
                pl.BlockSpec((HALO, width), lambda i: (jnp.minimum((i + 1) * hpt, n_halo - 1), 0))]

    in_specs = (tile_and_halos(o_att.shape[1]) + tile_and_halos(o_ml.shape[1]) + tile_and_halos(d) + [
        pl.BlockSpec((1, 1, mod.shape[-1]), lambda i: (mod_row(i, tm), 0, 0)),
        _const_spec((1, d)), _const_spec((mix, d)),
        _const_spec((2 * nct, d, tn)), _const_spec((2 * nct, 3, tn)), _const_spec((2 * nct, 1, tn)),
        _const_spec((nct, tn, d))])
    args = [o_att, o_att, o_att, o_ml, o_ml, o_ml, x, x, x, mod, wts["norm2_g"], wts["w_out"],
            wts["w_up"], wts["conv_w"], wts["conv_b"], wts["w_down"]]
    scratch = [pltpu.VMEM((tm + HALO, mix), BF16), pltpu.VMEM((tm + HALO, d), BF16), pltpu.VMEM((tm, d), F32),
               pltpu.VMEM((2, tm + HALO, tn), F32), pltpu.VMEM((2, tm + HALO, tn), F32), pltpu.VMEM((tm, d), F32)]
    return (n_tok // tm, nct, in_specs, args, pl.BlockSpec((tm, d), lambda i: (i, 0)),
            jax.ShapeDtypeStruct((n_tok, d), F32), scratch)


def _mix_ffn(o_att, o_ml, x, mod, mod_row, seq_len, wts):
    n_tiles, nct, in_specs, args, out_spec, out_shape, scratch = _mix_ffn_parts(
        o_att, o_ml, x, mod, mod_row, seq_len, wts)
    return pl.pallas_call(
        functools.partial(_ffn_kernel, seq_len=seq_len, n_col_tiles=nct),
        grid=(n_tiles,),
        in_specs=in_specs,
        out_specs=out_spec,
        out_shape=out_shape,
        scratch_shapes=scratch,
        compiler_params=_params(1),
        name="mix_ffn",
    )(*args)


def _ffn_nbr_kernel(*refs, n_in, seq_len, n_col_tiles, rows, kr, rows_per_tile, tiles_per_batch):
    ffn_in = refs[:n_in]
    q_ref, k_ref, vt_ref, kc_ref, vct_ref, bias_ref = refs[n_in:n_in + 6]
    y_ref, o_ref = refs[n_in + 6:n_in + 8]
    scratch = refs[n_in + 8:]
    ffn_scratch, pt_refs = scratch[:-2], scratch[-2:]
    base = (pl.program_id(0) % tiles_per_batch) * rows_per_tile

    def row(local, pt_ref):
        rr = pl.ds(pl.multiple_of(local * GRID_W, GRID_W), GRID_W)
        _nbr_row(base + local, q_ref.at[rr, :], k_ref, vt_ref, kc_ref, vct_ref, bias_ref, o_ref.at[rr, :], pt_ref,
                 rows=rows, kr=kr)

    n_before = rows_per_tile // 2

    def side_work(where):
        if where == "before":
            for local in range(n_before):
                row(local, pt_refs[local % 2])
        elif where == "after":
            for local in range(n_before, rows_per_tile):
                row(local, pt_refs[local % 2])

    _ffn_kernel(*ffn_in, y_ref, *ffn_scratch, seq_len=seq_len, n_col_tiles=n_col_tiles, side_work=side_work)


def _mix_ffn_with_nbr_attention(o_att, o_ml, x, mod, mod_row, seq_len, wts, q, k, vt, k_ctx, vt_ctx, rpb,
                                n_batch, nbr_seq):
    n_tiles, nct, in_specs, args, out_spec, out_shape, scratch = _mix_ffn_parts(
        o_att, o_ml, x, mod, mod_row, seq_len, wts)
    rows = nbr_seq // GRID_W
    kr = min(WIN_ROWS, rows)
    n_win = kr + 2
    assert rows >= n_win and (rows - n_win) % 2 == 0
    rpt = (n_batch * rows) // n_tiles
    tpb = n_tiles // n_batch
    bias = _nbr_bias_table(rpb, rows)
    past = k_ctx.shape[1]
    gw = HEAD_GROUP * ATT_HEAD_DIM
    n_groups = N_ATT_HEADS // HEAD_GROUP
    row_blk = pl.BlockSpec((rpt * GRID_W, ATT_WIDTH), lambda i: (i, 0))
    once = pl.Buffered(1)
    nbr_specs = [row_blk,
                 pl.BlockSpec((nbr_seq, ATT_WIDTH), lambda i: (i // tpb, 0), pipeline_mode=once),
                 pl.BlockSpec((ATT_WIDTH, nbr_seq), lambda i: (0, i // tpb), pipeline_mode=once),
                 pl.BlockSpec((1, past, ATT_WIDTH), lambda i: (i // tpb, 0, 0), pipeline_mode=once),
                 pl.BlockSpec((1, ATT_WIDTH, past), lambda i: (i // tpb, 0, 0), pipeline_mode=once),
                 _const_spec(bias.shape)]
    pt = pltpu.VMEM((n_groups, n_win * GRID_W, gw), BF16)
    y, o_lat = pl.pallas_call(
        functools.partial(_ffn_nbr_kernel, n_in=len(in_specs), seq_len=seq_len, n_col_tiles=nct, rows=rows, kr=kr,
                          rows_per_tile=rpt, tiles_per_batch=tpb),
        grid=(n_tiles,),
        in_specs=in_specs + nbr_specs,
        out_specs=[out_spec, row_blk],
        out_shape=[out_shape, jax.ShapeDtypeStruct((n_batch * nbr_seq, ATT_WIDTH), BF16)],
        scratch_shapes=scratch + [pt, pt],
        compiler_params=_params(1),
        name="mix_ffn_ctx_nbr_attn",
    )(*args, q, k, vt, k_ctx, vt_ctx, bias)
    return y, o_lat


def _retile_cast_kernel(w_ref, o_ref):
    tn = o_ref.shape[-1]
    for t in range(o_ref.shape[0]):
        o_ref[t] = w_ref[:, t * tn:(t + 1) * tn].astype(o_ref.dtype)


def _retile_cast(w, tn):
    r, c = w.shape
    group = 2 if (c // tn) % 2 == 0 else 1
    return pl.pallas_call(
        _retile_cast_kernel,
        grid=(c // (tn * group),),
        in_specs=[pl.BlockSpec((r, tn * group), lambda j: (0, j))],
        out_specs=pl.BlockSpec((group, r, tn), lambda j: (j, 0, 0)),
        out_shape=jax.ShapeDtypeStruct((c // tn, r, tn), BF16),
        compiler_params=_params(1),
        name="retile_cast",
    )(w)


def _cast_kernel(w_ref, o_ref):
    o_ref[...] = w_ref[...].astype(o_ref.dtype)


def _cast_rows(w, block_rows):
    r, c = w.shape
    spec = pl.BlockSpec((block_rows, c), lambda i: (i, 0))
    return pl.pallas_call(
        _cast_kernel,
        grid=(r // block_rows,),
        in_specs=[spec],
        out_specs=spec,
        out_shape=jax.ShapeDtypeStruct((r, c), BF16),
        compiler_params=_params(1),
        name="cast_rows",
    )(w)


def _layer_weights(norm1_g, norm2_g, w_in, b_gate, q_g, k_g, ml_g, w_out, w_up, conv_w, conv_b, w_down):
    d = w_in.shape[0]
    a, m, nh = ATT_WIDTH, ML_WIDTH, N_ML_HEADS
    def gate_lanes(g):
        g = g.reshape(g.shape[0], 2, 2, nh).transpose(0, 2, 3, 1).reshape(g.shape[0], 4 * nh)
        return jnp.pad(g, ((0, 0), (0, LANES - 4 * nh)))

    w_g = gate_lanes(w_in[:, 3 * a + 4 * m:])
    b_g = gate_lanes(b_gate.astype(F32).reshape(1, 4 * nh))
    head_id = jnp.arange(a) // ATT_HEAD_DIM
    d_ff = w_down.shape[0]
    tn = MXU_WIDTH
    nct = d_ff // tn

    def col_tiles(w):
        return w.reshape(w.shape[0], 2 * nct, tn).transpose(1, 0, 2)

    return {
        "norm1_g": norm1_g.reshape(1, d), "norm2_g": norm2_g.reshape(1, d),
        "w_in": w_in.astype(BF16),
        "w_g": w_g.astype(BF16), "b_g": b_g,
        "q_g": jnp.tile(q_g, N_ATT_HEADS).reshape(1, a), "k_g": jnp.tile(k_g, N_ATT_HEADS).reshape(1, a),
        "blockdiag": (head_id[:, None] == head_id[None, :]).astype(BF16),
        "ml_g": ml_g.reshape(1, m),
        "w_out": w_out.astype(BF16),
        "conv_w": col_tiles(conv_w.astype(F32)), "conv_b": col_tiles(conv_b.astype(F32).reshape(1, -1)),
        "w_up_f32": w_up, "w_down_f32": w_down, "ffn_tile": tn,
    }


def _rope_tables(seq_len):
    quarter = ML_HEAD_DIM // 4
    pos = np.arange(seq_len)
    inv_freq = ROPE_THETA ** (-np.arange(quarter, dtype=np.float64) / quarter)
    ang_r = (pos // GRID_W).astype(np.float64)[:, None] * inv_freq[None, :]
    ang_c = (pos % GRID_W).astype(np.float64)[:, None] * inv_freq[None, :]
    cos = np.concatenate([np.cos(ang_r)] * 2 + [np.cos(ang_c)] * 2, axis=-1)
    sin = np.concatenate([-np.sin(ang_r), np.sin(ang_r), -np.sin(ang_c), np.sin(ang_c)], axis=-1)
    return jnp.asarray(cos, F32), jnp.asarray(sin, F32)


def _layer(xp, xs, mod, wts, rope, ctx_kv, rpb, state, batch, seq, dec_batch, dec_seq):
    def ctx_row(i, tm):
        return 0 * i

    def lat_row(i, tm):
        return 1 + (i * tm) // dec_seq

    w_up, w_down, tn = wts["w_up_f32"], wts["w_down_f32"], wts["ffn_tile"]
    n_ctx_tiles = (batch * seq) // _token_tile(batch * seq, seq)
    if n_ctx_tiles >= max(w_up.shape[1] // (2 * tn), w_down.shape[0] // tn) and (w_up.shape[1] // tn) % 2 == 0:
        ctx, w_up_t, w_down_t = _ctx_in_projection_with_weight_casts(xp, mod, ctx_row, seq, wts, w_up, w_down, tn)
    else:
        ctx = _in_projection(xp, mod, ctx_row, seq, wts, None, latent=False)
        w_up_t = _retile_cast(w_up, tn)
        w_down_t = _cast_rows(w_down, tn).reshape(w_down.shape[0] // tn, tn, w_down.shape[1])
    wts = dict(wts, w_up=w_up_t, w_down=w_down_t)
    cache_k, cache_v = ctx[8], ctx[9]
    n_lat_tiles = (dec_batch * dec_seq) // _token_tile(dec_batch * dec_seq, dec_seq)
    if batch % n_lat_tiles == 0:
        lat, (o_att_c, o_ml_c, c_f, n_f, m_f) = _latent_in_projection_with_ctx_mixers(
            xs, mod, lat_row, dec_seq, wts, rope, ctx[:8], batch, seq)
    else:
        aq, ak, av, mq, mk, mvt, mot, gtt = ctx[:8]
        o_att_c = _context_attention(aq, ak, av, seq)
        o_ml_c, c_f, n_f, m_f = _mlstm(mq, mk, mvt, mot, gtt, wts["ml_g"], None, batch, seq, min(ML_CHUNK, seq))
        lat = _in_projection(xs, mod, lat_row, dec_seq, wts, rope, latent=True)
    aq, ak, avt, mq, mk, mvt, mot, gtt = lat[:8]
    n_rows = dec_batch * (dec_seq // GRID_W)
    if n_rows % n_ctx_tiles == 0 and n_ctx_tiles % dec_batch == 0:
        xp, o_att = _mix_ffn_with_nbr_attention(o_att_c, o_ml_c, xp, mod, ctx_row, seq, wts, aq, ak, avt,
                                                ctx_kv[0], ctx_kv[1], rpb, dec_batch, dec_seq)
    else:
        xp = _mix_ffn(o_att_c, o_ml_c, xp, mod, ctx_row, seq, wts)
        o_att = _neighborhood_attention(aq, ak, avt, ctx_kv[0], ctx_kv[1], rpb, dec_batch, dec_seq)
    o_ml = _mlstm(mq, mk, mvt, mot, gtt, wts["ml_g"], state, dec_batch, dec_seq, min(ML_CHUNK, dec_seq))[0]
    xs = _mix_ffn(o_att, o_ml, xs, mod, lat_row, dec_seq, wts)
    return xp, xs, cache_k, cache_v, (c_f, n_f[:, :, :, 0, :], _unpack_gate_rows(m_f))


def kernel(x_prompt, x_sample, cache_k, cache_v, state_C, state_n, state_m, c, c_ctx, w_mod, b_mod, norm1_g,
           norm2_g, w_in, b_gate, q_norm_g, k_norm_g, rpb, ml_norm_g, w_out, w_up, conv_w, conv_b, w_down):
    batch, seq, d = x_prompt.shape
    dec_batch, dec_seq, _ = x_sample.shape
    depth = w_mod.shape[0]
    past = cache_k.shape[2]
    cvecs = jnp.concatenate([c_ctx[None, :], c], axis=0).astype(F32)
    rope = _rope_tables(dec_seq)

    xp = x_prompt.reshape(batch * seq, d)
    xs = x_sample.reshape(dec_batch * dec_seq, d)
    ks, vs, cs, ns, ms = [], [], [], [], []
    for l in range(depth):
        wts = _layer_weights(norm1_g[l], norm2_g[l], w_in[l], b_gate[l], q_norm_g[l], k_norm_g[l],
                             ml_norm_g[l], w_out[l], w_up[l], conv_w[l], conv_b[l], w_down[l])
        mod = _modulation(cvecs, w_mod[l], b_mod[l])

        state = (state_C[:, l].astype(F32),
                 state_n[:, l].astype(F32)[:, :, :, None, :],
                 _pack_gate_rows(state_m[:, l].astype(F32)))
        ctx_kv = (cache_k[:, l].reshape(dec_batch, past, ATT_WIDTH).astype(BF16),
                  cache_v[:, l].reshape(dec_batch, past, ATT_WIDTH).transpose(0, 2, 1).astype(BF16))
        xp, xs, ak, av, st = _layer(xp, xs, mod, wts, rope, ctx_kv, rpb[l], state, batch, seq, dec_batch, dec_seq)
        ks.append(ak.reshape(batch, seq, N_ATT_HEADS, ATT_HEAD_DIM))
        vs.append(av.reshape(batch, seq, N_ATT_HEADS, ATT_HEAD_DIM))
        cs.append(st[0])
        ns.append(st[1])
        ms.append(st[2])
    return (xp.reshape(batch, seq, d), xs.reshape(dec_batch, dec_seq, d),
            jnp.stack(ks, axis=1), jnp.stack(vs, axis=1),
            jnp.stack(cs, axis=1), jnp.stack(ns, axis=1), jnp.stack(ms, axis=1))
```

```python
import functools

import jax
import jax.numpy as jnp
import numpy as np
from jax import lax
from jax.experimental import pallas as pl
from jax.experimental.pallas import tpu as pltpu

F32 = jnp.float32
BF16 = jnp.bfloat16

GRID_W = 64
N_ATT_HEADS = 8
ATT_HEAD_DIM = 64
ATT_WIDTH = N_ATT_HEADS * ATT_HEAD_DIM
WIN_ROWS = 8
WIN_COLS = 16
HEAD_GROUP = 4
N_ML_HEADS = 4
ML_HEAD_DIM = 128
ML_WIDTH = N_ML_HEADS * ML_HEAD_DIM
ROPE_THETA = 10000.0
EPS = 1e-6

LANES = 128
MXU_WIDTH = 256
ML_CHUNK = 256
MLSTM_UNROLL = 8
GATE_ROWS = 8
HALO = 16
TOKEN_TILE = 512
VMEM_LIMIT = 56 * 1024 * 1024


def _dot(a, b):
    return jnp.dot(a, b, preferred_element_type=F32)


def _dot_nt(a, b):
    return lax.dot_general(a, b, (((1,), (1,)), ((), ())), preferred_element_type=F32)


def _dot_tn(a, b):
    return lax.dot_general(a, b, (((0,), (0,)), ((), ())), preferred_element_type=F32)


def _const_spec(shape):
    nd = len(shape)
    return pl.BlockSpec(shape, lambda *_: (0,) * nd, pipeline_mode=pl.Buffered(1))


def _params(n_axes):
    return pltpu.CompilerParams(dimension_semantics=("arbitrary",) * n_axes,
                                vmem_limit_bytes=VMEM_LIMIT)


def _token_tile(n_tok, seq_len):
    tm = min(TOKEN_TILE, n_tok)
    while n_tok % tm or (seq_len % tm and tm % seq_len):
        tm //= 2
    return tm


def _log_sigmoid(x):
    return jnp.minimum(x, 0.0) - jnp.log1p(jnp.exp(-jnp.abs(x)))


def _mod_kernel(c_ref, w_ref, b_ref, o_ref):
    c = c_ref[...]
    s = c * jax.nn.sigmoid(c)
    o_ref[...] = _dot(s.astype(BF16), w_ref[...].astype(BF16)) + b_ref[...]


def _modulation(cvecs, w_mod, b_mod):
    r, d = cvecs.shape
    n = w_mod.shape[1]
    tn = d
    out = pl.pallas_call(
        _mod_kernel,
        grid=(n // tn,),
        in_specs=[pl.BlockSpec((r, d), lambda j: (0, 0)),
                  pl.BlockSpec((d, tn), lambda j: (0, j)),
                  pl.BlockSpec((1, tn), lambda j: (0, j))],
        out_specs=pl.BlockSpec((r, tn), lambda j: (0, j)),
        out_shape=jax.ShapeDtypeStruct((r, n), F32),
        compiler_params=_params(1),
        name="adaln_mod",
    )(cvecs, w_mod, b_mod.reshape(1, n))
    return out.reshape(r, 1, n)


def _rope(x, cos, sin_signed):
    lane = lax.broadcasted_iota(jnp.int32, x.shape, 1)
    partner = jnp.where((lane & 32) == 0, pltpu.roll(x, LANES - 32, 1), pltpu.roll(x, 32, 1))
    return x * cos + partner * sin_signed


def _inproj_kernel(*refs, latent):
    if latent:
        (x_ref, mod_ref, g1_ref, win_ref, wg_ref, bg_ref, qg_ref, kg_ref, bd_ref, cos_ref, sin_ref,
         aq_ref, ak_ref, av_ref, mq_ref, mk_ref, mv_ref, mo_ref, gtt_ref) = refs
    else:
        (x_ref, mod_ref, g1_ref, win_ref, wg_ref, bg_ref, qg_ref, kg_ref, bd_ref,
         aq_ref, ak_ref, av_ref, mq_ref, mk_ref, mv_ref, mo_ref, gtt_ref, ck_ref, cv_ref,
         cks_ref, cvs_ref) = refs
    x = x_ref[...]
    d = x.shape[-1]
    mod = mod_ref[0]
    sh1, sc1 = mod[:, 0:d], mod[:, d:2 * d]
    y = x * lax.rsqrt(jnp.mean(x * x, axis=-1, keepdims=True) + EPS)
    h = (y * g1_ref[...]) * (1.0 + sc1) + sh1
    hb = h.astype(BF16)

    def head_norm(a, g):
        ss = _dot((a * a).astype(BF16), bd_ref[...])
        return a * lax.rsqrt(ss * (1.0 / ATT_HEAD_DIM) + EPS) * g

    w = ATT_WIDTH
    att = _dot(hb, win_ref[:, 0:3 * w])
    aq_ref[...] = (head_norm(att[:, 0:w], qg_ref[...]) * ATT_HEAD_DIM ** -0.5).astype(aq_ref.dtype)
    kn = head_norm(att[:, w:2 * w], kg_ref[...])
    av = att[:, 2 * w:3 * w]
    ak_ref[...] = kn.astype(BF16)
    if latent:
        av_ref[...] = av.T.astype(BF16)
    else:
        av_ref[...] = av.astype(BF16)
        tm = kn.shape[0]
        for src, dst, scr in ((kn, ck_ref, cks_ref), (av, cv_ref, cvs_ref)):
            for hh in range(N_ATT_HEADS):
                pair = src[:, (hh // 2) * LANES:(hh // 2 + 1) * LANES]
                if hh % 2:
                    pair = pltpu.roll(pair, ATT_HEAD_DIM, 1)
                scr[pl.ds(hh, tm, stride=N_ATT_HEADS), :] = pair
            dst[...] = scr[...].reshape(tm, N_ATT_HEADS, LANES)[:, :, 0:ATT_HEAD_DIM]

    w = ML_WIDTH
    ml0 = 3 * ATT_WIDTH
    mq = _dot(hb, win_ref[:, ml0:ml0 + w])
    mk = _dot(hb, win_ref[:, ml0 + w:ml0 + 2 * w]) * ML_HEAD_DIM ** -0.5
    if latent:
        cos, sin = cos_ref[...], sin_ref[...]
        for hh in range(N_ML_HEADS):
            sl = slice(hh * ML_HEAD_DIM, (hh + 1) * ML_HEAD_DIM)
            mq_ref[:, sl] = _rope(mq[:, sl], cos, sin).astype(BF16)
            mk_ref[:, sl] = _rope(mk[:, sl], cos, sin).astype(BF16)
    else:
        mq_ref[...] = mq.astype(BF16)
        mk_ref[...] = mk.astype(BF16)
    mv_ref[...] = _dot(hb, win_ref[:, ml0 + 2 * w:ml0 + 3 * w]).T.astype(BF16)
    mo_ref[...] = _dot(hb, win_ref[:, ml0 + 3 * w:ml0 + 4 * w]).T
    gates = _dot(hb, wg_ref[...]) + bg_ref[...]
    gtt_ref[...] = gates.T[0:2 * GATE_ROWS]


def _in_projection_parts(x, mod, mod_row, seq_len, wts, rope, *, latent):
    n_tok, d = x.shape
    tm = _token_tile(n_tok, seq_len)
    tiles_per_seq = max(seq_len // tm, 1)

    def tok(width):
        return pl.BlockSpec((tm, width), lambda i: (i, 0))

    in_specs = [tok(d),
                pl.BlockSpec((1, 1, mod.shape[-1]), lambda i: (mod_row(i, tm), 0, 0)),
                _const_spec((1, d)),
                _const_spec(wts["w_in"].shape),
                _const_spec(wts["w_g"].shape), _const_spec((1, LANES)),
                _const_spec((1, ATT_WIDTH)), _const_spec((1, ATT_WIDTH)),
                _const_spec((ATT_WIDTH, ATT_WIDTH))]
    args = [x, mod, wts["norm1_g"], wts["w_in"], wts["w_g"], wts["b_g"],
            wts["q_g"], wts["k_g"], wts["blockdiag"]]
    if latent:
        in_specs += [pl.BlockSpec((tm, LANES), lambda i: (i % tiles_per_seq, 0))] * 2
        args += [rope[0], rope[1]]
    out_shape = [jax.ShapeDtypeStruct((n_tok, ATT_WIDTH), BF16),
                 jax.ShapeDtypeStruct((n_tok, ATT_WIDTH), BF16),
                 jax.ShapeDtypeStruct((ATT_WIDTH, n_tok) if latent else (n_tok, ATT_WIDTH), BF16),
                 jax.ShapeDtypeStruct((n_tok, ML_WIDTH), BF16),
                 jax.ShapeDtypeStruct((n_tok, ML_WIDTH), BF16),
                 jax.ShapeDtypeStruct((ML_WIDTH, n_tok), BF16),
                 jax.ShapeDtypeStruct((ML_WIDTH, n_tok), F32),
                 jax.ShapeDtypeStruct((2 * GATE_ROWS, n_tok), F32)]

    def tok_t(height):
        return pl.BlockSpec((height, tm), lambda i: (0, i))

    av_spec = tok_t(ATT_WIDTH) if latent else tok(ATT_WIDTH)
    out_specs = ([tok(ATT_WIDTH)] * 2 + [av_spec] + [tok(ML_WIDTH)] * 2 + [tok_t(ML_WIDTH)] * 2
                 + [tok_t(2 * GATE_ROWS)])
    if not latent:
        cache = jax.ShapeDtypeStruct((n_tok, N_ATT_HEADS, ATT_HEAD_DIM), F32)
        out_shape += [cache, cache]
        out_specs += [pl.BlockSpec((tm, N_ATT_HEADS, ATT_HEAD_DIM), lambda i: (i, 0, 0))] * 2
    scratch = [] if latent else [pltpu.VMEM((tm * N_ATT_HEADS, LANES), F32)] * 2
    return n_tok // tm, in_specs, args, out_shape, out_specs, scratch


def _in_projection(x, mod, mod_row, seq_len, wts, rope, *, latent):
    n_tiles, in_specs, args, out_shape, out_specs, scratch = _in_projection_parts(
        x, mod, mod_row, seq_len, wts, rope, latent=latent)
    return pl.pallas_call(
        functools.partial(_inproj_kernel, latent=latent),
        grid=(n_tiles,),
        in_specs=in_specs,
        out_specs=out_specs,
        out_shape=out_shape,
        scratch_shapes=scratch,
        compiler_params=_params(1),
        name="in_proj_latent" if latent else "in_proj_ctx",
    )(*args)


def _inproj_cast_kernel(*refs, n_in, n_out):
    ip_in, (wup_ref, wdn_ref) = refs[:n_in], refs[n_in:n_in + 2]
    outs = refs[n_in + 2:]
    ip_out, (wup_o, wdn_o), scratch = outs[:n_out], outs[n_out:n_out + 2], outs[n_out + 2:]
    _inproj_kernel(*ip_in, *ip_out, *scratch, latent=False)
    _retile_cast_kernel(wup_ref, wup_o)
    wdn_o[0] = wdn_ref[...].astype(wdn_o.dtype)


def _ctx_in_projection_with_weight_casts(x, mod, mod_row, seq_len, wts, w_up, w_down, tn):
    n_tiles, in_specs, args, out_shape, out_specs, scratch = _in_projection_parts(
        x, mod, mod_row, seq_len, wts, None, latent=False)
    d, two_dff = w_up.shape
    nct = w_down.shape[0] // tn
    n_grp = two_dff // (2 * tn)
    assert two_dff % (2 * tn) == 0 and max(n_grp, nct) <= n_tiles
    cast_in = [pl.BlockSpec((d, 2 * tn), lambda i: (0, jnp.minimum(i, n_grp - 1))),
               pl.BlockSpec((tn, d), lambda i: (jnp.minimum(i, nct - 1), 0))]
    cast_out = [pl.BlockSpec((2, d, tn), lambda i: (jnp.minimum(i, n_grp - 1), 0, 0)),
                pl.BlockSpec((1, tn, d), lambda i: (jnp.minimum(i, nct - 1), 0, 0))]
    cast_shape = [jax.ShapeDtypeStruct((two_dff // tn, d, tn), BF16), jax.ShapeDtypeStruct((nct, tn, d), BF16)]
    outs = pl.pallas_call(
        functools.partial(_inproj_cast_kernel, n_in=len(in_specs), n_out=len(out_specs)),
        grid=(n_tiles,),
        in_specs=in_specs + cast_in,
        out_specs=out_specs + cast_out,
        out_shape=out_shape + cast_shape,
        scratch_shapes=scratch,
        compiler_params=_params(1),
        name="in_proj_ctx_weight_casts",
    )(*args, w_up, w_down)
    return outs[:len(out_specs)], outs[len(out_specs)], outs[len(out_specs) + 1]


def _ctx_attn_kernel(q_ref, k_ref, v_ref, o_ref):
    n = q_ref.shape[0]
    gw = HEAD_GROUP * ATT_HEAD_DIM
    lane_head = lax.broadcasted_iota(jnp.int32, (n, gw), 1) // ATT_HEAD_DIM
    for g in range(N_ATT_HEADS // HEAD_GROUP):
        sl = slice(g * gw, (g + 1) * gw)
        q4 = q_ref[:, sl]
        qbd = jnp.concatenate([jnp.where(lane_head == hl, q4, jnp.zeros_like(q4))
                               for hl in range(HEAD_GROUP)], axis=0)
        s = _dot_nt(k_ref[:, sl], qbd)
        p = jnp.exp(s - jnp.max(s, axis=0, keepdims=True))
        p = p / jnp.sum(p, axis=0, keepdims=True)
        o4 = _dot_tn(p.astype(BF16), v_ref[:, sl])
        out = jnp.where(lane_head == 0, o4[0:n], 0.0)
        for hl in range(1, HEAD_GROUP):
            out = out + jnp.where(lane_head == hl, o4[hl * n:(hl + 1) * n], 0.0)
        o_ref[:, sl] = out.astype(o_ref.dtype)


def _context_attention(q, k, v, seq_len):
    n_tok = q.shape[0]
    spec = pl.BlockSpec((seq_len, ATT_WIDTH), lambda b: (b, 0))
    return pl.pallas_call(
        _ctx_attn_kernel,
        grid=(n_tok // seq_len,),
        in_specs=[spec, spec, spec],
        out_specs=spec,
        out_shape=jax.ShapeDtypeStruct((n_tok, ATT_WIDTH), BF16),
        compiler_params=_params(1),
        name="ctx_attn",
    )(q, k, v)


def _nbr_attn_kernel(q_ref, k_ref, vt_ref, kc_ref, vct_ref, bias_ref, o_ref, pt_ref, *, rows, kr):
    _nbr_row(pl.program_id(1), q_ref, k_ref, vt_ref, kc_ref, vct_ref, bias_ref, o_ref, pt_ref, rows=rows, kr=kr)


def _nbr_row(r, q_ref, k_ref, vt_ref, kc_ref, vct_ref, bias_ref, o_ref, pt_ref, *, rows, kr):
    n_win = kr + 2
    n_loc = kr * GRID_W
    gw = HEAD_GROUP * ATT_HEAD_DIM
    groups = N_ATT_HEADS // HEAD_GROUP
    lane_head = lax.broadcasted_iota(jnp.int32, (GRID_W, gw), 1) // ATT_HEAD_DIM
    rs = jnp.clip(r - kr // 2, 0, rows - kr)
    start = jnp.minimum(rs - (rs & 1), rows - n_win)
    delta = rs - start
    bias0 = pl.multiple_of((WIN_ROWS - 1 - (r - rs)) * GRID_W, GRID_W)
    zeros2 = jnp.zeros((2 * GRID_W, gw), BF16)
    for g in range(groups):
        sl = slice(g * gw, (g + 1) * gw)
        q4 = q_ref[:, sl]
        qbd = jnp.concatenate([jnp.where(lane_head == hl, q4, jnp.zeros_like(q4))
                               for hl in range(HEAD_GROUP)], axis=0)
        s_loc = _dot_nt(k_ref[pl.ds(pl.multiple_of(rs * GRID_W, GRID_W), n_loc), sl], qbd)
        s_loc = s_loc + bias_ref[g, pl.ds(bias0, n_loc), :]
        s_ctx = _dot_nt(kc_ref[0, :, sl], qbd)
        m = jnp.maximum(jnp.max(s_loc, axis=0, keepdims=True), jnp.max(s_ctx, axis=0, keepdims=True))
        p_loc = jnp.exp(s_loc - m)
        p_ctx = jnp.exp(s_ctx - m)
        l = jnp.sum(p_loc, axis=0, keepdims=True) + jnp.sum(p_ctx, axis=0, keepdims=True)
        pt_ref[g, 0:2 * GRID_W, :] = zeros2
        pt_ref[g, n_loc:n_loc + 2 * GRID_W, :] = zeros2
        pt_ref[g, pl.ds(pl.multiple_of(delta * GRID_W, GRID_W), n_loc), :] = p_loc.astype(BF16)
        vt_win = vt_ref[sl, pl.ds(pl.multiple_of(start * GRID_W, 2 * GRID_W), n_win * GRID_W)]
        ot = _dot(vt_win, pt_ref[g]) + _dot(vct_ref[0, sl, :], p_ctx.astype(BF16))
        o4 = (ot / l).T
        out = jnp.where(lane_head == 0, o4[0:GRID_W], 0.0)
        for hl in range(1, HEAD_GROUP):
            out = out + jnp.where(lane_head == hl, o4[hl * GRID_W:(hl + 1) * GRID_W], 0.0)
        o_ref[:, sl] = out.astype(o_ref.dtype)


def _nbr_bias_table(rpb, rows):
    col = jnp.arange(GRID_W)
    cs = jnp.clip(col - WIN_COLS // 2, 0, GRID_W - WIN_COLS)
    in_win = (col[None, :] >= cs[:, None]) & (col[None, :] < cs[:, None] + WIN_COLS)
    dc_idx = jnp.clip(col[None, :] - col[:, None] + (WIN_COLS - 1), 0, 2 * WIN_COLS - 2)
    n_dr, n_dc = rpb.shape[1], rpb.shape[2]
    onehot = (dc_idx[None] == jnp.arange(n_dc)[:, None, None]).astype(F32)
    t = jnp.einsum("hrc,cqk->hrqk", rpb.astype(F32), onehot, precision=lax.Precision.HIGHEST)
    t = jnp.where(in_win[None, None], t, -jnp.inf)
    n_groups = N_ATT_HEADS // HEAD_GROUP
    t = t.reshape(n_groups, HEAD_GROUP, n_dr, GRID_W, GRID_W).transpose(0, 2, 4, 1, 3)
    return t.reshape(n_groups, n_dr * GRID_W, HEAD_GROUP * GRID_W)


def _neighborhood_attention(q, k, vt, k_ctx, vt_ctx, rpb, n_batch, seq_len):
    rows = seq_len // GRID_W
    kr = min(WIN_ROWS, rows)
    n_win = kr + 2
    assert rows >= n_win and (rows - n_win) % 2 == 0
    bias = _nbr_bias_table(rpb, rows)
    past = k_ctx.shape[1]
    gw = HEAD_GROUP * ATT_HEAD_DIM
    n_groups = N_ATT_HEADS // HEAD_GROUP
    row_spec = pl.BlockSpec((GRID_W, ATT_WIDTH), lambda b, r: (b * rows + r, 0))
    return pl.pallas_call(
        functools.partial(_nbr_attn_kernel, rows=rows, kr=kr),
        grid=(n_batch, rows),
        in_specs=[row_spec,
                  pl.BlockSpec((seq_len, ATT_WIDTH), lambda b, r: (b, 0)),
                  pl.BlockSpec((ATT_WIDTH, seq_len), lambda b, r: (0, b)),
                  pl.BlockSpec((1, past, ATT_WIDTH), lambda b, r: (b, 0, 0)),
                  pl.BlockSpec((1, ATT_WIDTH, past), lambda b, r: (b, 0, 0)),
                  _const_spec(bias.shape)],
        out_specs=row_spec,
        out_shape=jax.ShapeDtypeStruct((n_batch * seq_len, ATT_WIDTH), BF16),
        scratch_shapes=[pltpu.VMEM((n_groups, n_win * GRID_W, gw), BF16)],
        compiler_params=_params(2),
        name="nbr_attn",
    )(q, k, vt, k_ctx, vt_ctx, bias)


def _mlstm_kernel(*refs, chunk, n_chunks, has_state):
    dh = ML_HEAD_DIM
    nrep = 2 * GATE_ROWS
    if has_state:
        (q_ref, k_ref, vt_ref, mot_ref, gtt_ref, g_ref, c0_ref, n0_ref, m0_ref,
         o_ref, c_ref, n_ref, m_ref, hf_ref, hb_ref, cn_ref, mrun_ref) = refs
        for d in range(2):
            for hh in range(N_ML_HEADS):
                cn_ref[d, hh, 0:dh, :] = c0_ref[0, d, hh].T
                cn_ref[d, hh, dh:dh + nrep, :] = jnp.broadcast_to(n0_ref[0, d, hh], (nrep, dh))
        mrun_ref[...] = m0_ref[0]
    else:
        (q_ref, k_ref, vt_ref, mot_ref, gtt_ref, g_ref,
         o_ref, c_ref, n_ref, m_ref, hf_ref, hb_ref, cn_ref, mrun_ref) = refs
        cn_ref[...] = jnp.zeros_like(cn_ref)
        mrun_ref[...] = jnp.zeros_like(mrun_ref)
    use_state = has_state or n_chunks > 1

    i0 = lax.broadcasted_iota(jnp.int32, (chunk, chunk), 0)
    i1 = lax.broadcasted_iota(jnp.int32, (chunk, chunk), 1)
    row_id = lax.broadcasted_iota(jnp.int32, (GATE_ROWS, 1), 0)
    instances = [(hh, d) for d in range(2) for hh in range(N_ML_HEADS)]

    def split3(x):
        hi = x.astype(BF16)
        r1 = x - hi.astype(F32)
        mid = r1.astype(BF16)
        return hi, mid, (r1 - mid.astype(F32)).astype(BF16)

    def body(i, _):
        chunks = (i, n_chunks - 1 - i)
        rows = [pl.ds(pl.multiple_of(c * chunk, chunk), chunk) for c in chunks]
        le = [i0 <= i1, i0 >= i1]
        m_prev = mrun_ref[...][:, 0:1]

        def qkv(hh, d):
            sl = slice(hh * dh, (hh + 1) * dh)
            return q_ref[rows[d], sl], k_ref[rows[d], sl], vt_ref[sl, rows[d]]

        st = [_dot_nt(qkv(hh, d)[1], qkv(hh, d)[0]) for hh, d in instances]
        if use_state:
            cn = [cn_ref[d, hh] for hh, d in instances]
            qct = [_dot_nt(cn[n].astype(BF16), qkv(hh, d)[0]) for n, (hh, d) in enumerate(instances)]

        a_col, a8, cs8, w8, decay8, m_new = [], [], [], [], [], []
        for d in range(2):
            gi_r = gtt_ref[0:GATE_ROWS, rows[d]]
            lf_r = _log_sigmoid(gtt_ref[GATE_ROWS:2 * GATE_ROWS, rows[d]])
            tri_r = jnp.where(le[d], 1.0, 0.0).astype(BF16)
            cs_r = sum(_dot(t, tri_r) for t in split3(lf_r))
            b_last = jnp.sum(lf_r, axis=1, keepdims=True)
            w_end = b_last + gi_r - cs_r
            m_d = jnp.maximum(b_last + m_prev, jnp.max(w_end, axis=1, keepdims=True))
            a8.append(gi_r - cs_r)
            cs8.append(cs_r)
            decay8.append(jnp.exp(b_last + m_prev - m_d))
            w8.append(jnp.exp(w_end - m_d))
            m_new.append(m_d)
            pad = jnp.zeros((LANES - GATE_ROWS, chunk), F32)
            a_col.append(jnp.concatenate([gi_r - cs_r, pad], axis=0).T)
        m_next = jnp.where((row_id & 1) == 0, m_new[0], m_new[1])
        mrun_ref[...] = jnp.broadcast_to(m_next, (GATE_ROWS, LANES))

        pt_all, g_all, den_all = [], [], []
        for n, (hh, d) in enumerate(instances):
            j = 2 * hh + d
            a = jnp.where(le[d], a_col[d][:, j:j + 1], -jnp.inf)
            g = jnp.maximum(jnp.max(a, axis=0, keepdims=True), m_prev[j:j + 1])
            pt = st[n] * jnp.exp(a - g)
            pt_all.append(pt)
            g_all.append(g)
            den_all.append(jnp.sum(pt, axis=0, keepdims=True))

        for n, (hh, d) in enumerate(instances):
            j = 2 * hh + d
            sl = slice(hh * dh, (hh + 1) * dh)
            _, k, vt = qkv(hh, d)
            g, den = g_all[n], den_all[n]
            num = _dot(vt, pt_all[n].astype(BF16))
            if use_state:
                w_inter = jnp.exp(m_prev[j:j + 1] - g)
                num = num + w_inter * qct[n][0:dh]
                den = den + w_inter * qct[n][dh:dh + 1]
            scale = 1.0 / jnp.maximum(jnp.abs(den), jnp.exp(-(cs8[d][j:j + 1] + g)))
            (hb_ref if d else hf_ref)[sl, rows[d]] = num * scale
            w_row = w8[d][j:j + 1]
            lhs = jnp.concatenate([(vt.astype(F32) * w_row).astype(BF16),
                                   jnp.broadcast_to(w_row, (nrep, chunk)).astype(BF16)], axis=0)
            upd = _dot(lhs, k)
            if use_state:
                upd = decay8[d][j:j + 1] * cn[n] + upd
            cn_ref[d, hh] = upd
        return 0

    def finish(c, _):
        rows = pl.ds(pl.multiple_of(c * chunk, chunk), chunk)
        for hh in range(N_ML_HEADS):
            sl = slice(hh * dh, (hh + 1) * dh)
            hs = hf_ref[sl, rows] + hb_ref[sl, rows]
            y = hs * lax.rsqrt(jnp.mean(hs * hs, axis=0, keepdims=True) + EPS) * g_ref[sl, :]
            y = y * jax.nn.sigmoid(mot_ref[sl, rows])
            o_ref[rows, sl] = y.T.astype(o_ref.dtype)
        return 0

    if n_chunks <= MLSTM_UNROLL:
        for c in range(n_chunks):
            body(c, 0)
        for c in range(n_chunks):
            finish(c, 0)
    else:
        lax.fori_loop(0, n_chunks, body, 0)
        lax.fori_loop(0, n_chunks, finish, 0)
    for d in range(2):
        for hh in range(N_ML_HEADS):
            cn = cn_ref[d, hh]
            c_ref[0, d, hh] = cn[0:dh].T
            n_ref[0, d, hh] = cn[dh:dh + 1]
    m_ref[0] = mrun_ref[...]


def _mlstm(mq, mk, mvt, mot, gates_t, ml_g, state, n_batch, seq_len, chunk):
    dh = ML_HEAD_DIM
    nh = N_ML_HEADS
    width = nh * dh
    has_state = state is not None
    seq = pl.BlockSpec((seq_len, width), lambda b: (b, 0))
    seq_t = pl.BlockSpec((width, seq_len), lambda b: (0, b))
    st_c = pl.BlockSpec((1, 2, nh, dh, dh), lambda b: (b, 0, 0, 0, 0))
    st_n = pl.BlockSpec((1, 2, nh, 1, dh), lambda b: (b, 0, 0, 0, 0))
    st_m = pl.BlockSpec((1, GATE_ROWS, LANES), lambda b: (b, 0, 0))
    in_specs = [seq, seq, seq_t, seq_t,
                pl.BlockSpec((2 * GATE_ROWS, seq_len), lambda b: (0, b)),
                _const_spec((width, chunk))]
    args = [mq, mk, mvt, mot, gates_t, jnp.broadcast_to(ml_g.reshape(width, 1), (width, chunk))]
    if has_state:
        in_specs += [st_c, st_n, st_m]
        args += list(state)
    out_shape = [jax.ShapeDtypeStruct((n_batch * seq_len, width), BF16),
                 jax.ShapeDtypeStruct((n_batch, 2, nh, dh, dh), F32),
                 jax.ShapeDtypeStruct((n_batch, 2, nh, 1, dh), F32),
                 jax.ShapeDtypeStruct((n_batch, GATE_ROWS, LANES), F32)]
    return pl.pallas_call(
        functools.partial(_mlstm_kernel, chunk=chunk, n_chunks=seq_len // chunk, has_state=has_state),
        grid=(n_batch,),
        in_specs=in_specs,
        out_specs=[seq, st_c, st_n, st_m],
        out_shape=out_shape,
        scratch_shapes=[pltpu.VMEM((width, seq_len), F32), pltpu.VMEM((width, seq_len), F32),
                        pltpu.VMEM((2, nh, dh + 2 * GATE_ROWS, dh), F32), pltpu.VMEM((GATE_ROWS, LANES), F32)],
        compiler_params=_params(1),
        name="mlstm_latent" if has_state else "mlstm_ctx",
    )(*args)


def _pack_gate_rows(m):
    b = m.shape[0]
    packed = m.transpose(0, 2, 1).reshape(b, 2 * N_ML_HEADS, 1)
    return jnp.broadcast_to(packed, (b, GATE_ROWS, LANES))


def _unpack_gate_rows(m):
    b = m.shape[0]
    return m[:, :, 0].reshape(b, N_ML_HEADS, 2).transpose(0, 2, 1)


def _inproj_ctxmix_kernel(*refs, n_in, n_out, n_seq, seq_len, chunk):
    ip_in = refs[:n_in]
    q_ref, k_ref, v_ref, mq_ref, mk_ref, mvt_ref, mot_ref, gtt_ref, g_ref = refs[n_in:n_in + 9]
    outs = refs[n_in + 9:]
    ip_out = outs[:n_out]
    oatt_ref, oml_ref, c_ref, n_ref, m_ref = outs[n_out:n_out + 5]
    scratch = outs[n_out + 5:]
    _inproj_kernel(*ip_in, *ip_out, latent=True)
    for s in range(n_seq):
        rows = slice(s * seq_len, (s + 1) * seq_len)
        _ctx_attn_kernel(q_ref.at[rows, :], k_ref.at[rows, :], v_ref.at[rows, :], oatt_ref.at[rows, :])
        _mlstm_kernel(mq_ref.at[rows, :], mk_ref.at[rows, :], mvt_ref.at[:, rows], mot_ref.at[:, rows],
                      gtt_ref.at[:, rows], g_ref, oml_ref.at[rows, :], c_ref.at[s:s + 1], n_ref.at[s:s + 1],
                      m_ref.at[s:s + 1], *scratch[4 * s:4 * s + 4],
                      chunk=chunk, n_chunks=seq_len // chunk, has_state=False)


def _latent_in_projection_with_ctx_mixers(x, mod, mod_row, seq_len, wts, rope, ctx, n_ctx_batch, ctx_seq):
    n_tiles, in_specs, args, out_shape, out_specs, scratch = _in_projection_parts(
        x, mod, mod_row, seq_len, wts, rope, latent=True)
    n_seq = n_ctx_batch // n_tiles
    rows = n_seq * ctx_seq
    n_ctx_tok = n_ctx_batch * ctx_seq
    chunk = min(ML_CHUNK, ctx_seq)
    nh, dh = N_ML_HEADS, ML_HEAD_DIM

    def tok(width):
        return pl.BlockSpec((rows, width), lambda i: (i, 0))

    def tok_t(height):
        return pl.BlockSpec((height, rows), lambda i: (0, i))

    def per_seq(*tail):
        return pl.BlockSpec((n_seq,) + tail, lambda i: (i,) + (0,) * len(tail))

    aq, ak, av, mq, mk, mvt, mot, gtt = ctx
    mix_in_specs = ([tok(ATT_WIDTH)] * 3 + [tok(ML_WIDTH)] * 2 + [tok_t(ML_WIDTH)] * 2
                    + [tok_t(2 * GATE_ROWS), _const_spec((ML_WIDTH, chunk))])
    mix_args = [aq, ak, av, mq, mk, mvt, mot, gtt,
                jnp.broadcast_to(wts["ml_g"].reshape(ML_WIDTH, 1), (ML_WIDTH, chunk))]
    mix_out_shape = [jax.ShapeDtypeStruct((n_ctx_tok, ATT_WIDTH), BF16),
                     jax.ShapeDtypeStruct((n_ctx_tok, ML_WIDTH), BF16),
                     jax.ShapeDtypeStruct((n_ctx_batch, 2, nh, dh, dh), F32),
                     jax.ShapeDtypeStruct((n_ctx_batch, 2, nh, 1, dh), F32),
                     jax.ShapeDtypeStruct((n_ctx_batch, GATE_ROWS, LANES), F32)]
    mix_out_specs = [tok(ATT_WIDTH), tok(ML_WIDTH), per_seq(2, nh, dh, dh), per_seq(2, nh, 1, dh),
                     per_seq(GATE_ROWS, LANES)]
    mix_scratch = [pltpu.VMEM((ML_WIDTH, ctx_seq), F32), pltpu.VMEM((ML_WIDTH, ctx_seq), F32),
                   pltpu.VMEM((2, nh, dh + 2 * GATE_ROWS, dh), F32), pltpu.VMEM((GATE_ROWS, LANES), F32)] * n_seq
    outs = pl.pallas_call(
        functools.partial(_inproj_ctxmix_kernel, n_in=len(in_specs), n_out=len(out_specs), n_seq=n_seq,
                          seq_len=ctx_seq, chunk=chunk),
        grid=(n_tiles,),
        in_specs=in_specs + mix_in_specs,
        out_specs=out_specs + mix_out_specs,
        out_shape=out_shape + mix_out_shape,
        scratch_shapes=scratch + mix_scratch,
        compiler_params=_params(1),
        name="in_proj_latent_ctx_mixers",
    )(*args, *mix_args)
    return outs[:len(out_specs)], outs[len(out_specs):]


def _ffn_kernel(oa_ref, oap_ref, oan_ref, om_ref, omp_ref, omn_ref, x_ref, xp_ref, xn_ref, mod_ref, g2_ref,
                wout_ref, wup_ref, cw_ref, cb_ref, wd_ref,
                y_ref, oc_ref, lhs_ref, x1_ref, ug_ref, uv_ref, acc_ref, *, seq_len, n_col_tiles, side_work=None):
    tm, d = x_ref.shape
    aw = oa_ref.shape[1]

    def halo(next_ref, prev_ref):
        row = lax.broadcasted_iota(jnp.int32, next_ref.shape, 0)
        return jnp.where(row < HALO // 2, next_ref[...], prev_ref[...])

    oc_ref[0:tm, 0:aw] = oa_ref[...]
    oc_ref[0:tm, aw:] = om_ref[...]
    oc_ref[tm:tm + HALO, 0:aw] = halo(oan_ref, oap_ref)
    oc_ref[tm:tm + HALO, aw:] = halo(omn_ref, omp_ref)
    mod = mod_ref[0]
    g1 = mod[:, 2 * d:3 * d]
    sh2, sc2 = mod[:, 3 * d:4 * d], mod[:, 4 * d:5 * d]
    out = _dot(oc_ref[...], wout_ref[...])

    def norm2(x1):
        y = x1 * lax.rsqrt(jnp.mean(x1 * x1, axis=-1, keepdims=True) + EPS)
        return ((y * g2_ref[...]) * (1.0 + sc2) + sh2).astype(BF16)

    x1 = x_ref[...] + g1 * out[0:tm]
    x1_ref[...] = x1
    lhs_ref[0:tm, :] = norm2(x1)
    lhs_ref[tm:tm + HALO, :] = norm2(halo(xn_ref, xp_ref) + g1 * out[tm:tm + HALO])
    acc_ref[...] = jnp.zeros_like(acc_ref)
    sub = 8
    period = min(seq_len, tm)
    first_groups = sorted({r // sub for r in range(0, tm, period)})
    last_groups = sorted({(r + period - 1) // sub for r in range(0, tm, period)})
    tile0 = pl.program_id(0) * tm

    def zero_rows(x, groups, target):
        parts, at = [], 0
        for grp in groups:
            lo = grp * sub
            if lo > at:
                parts.append(x[at:lo])
            pos = (tile0 + lo + lax.broadcasted_iota(jnp.int32, (sub, 1), 0)) % seq_len
            parts.append(jnp.where(pos == target, 0.0, x[lo:lo + sub]))
            at = lo + sub
        if at < tm:
            parts.append(x[at:tm])
        return jnp.concatenate(parts, axis=0)

    def conv(u, cw, cb):
        prev = zero_rows(pltpu.roll(u, 1, 0)[0:tm], first_groups, 0)
        nxt = zero_rows(pltpu.roll(u, tm + HALO - 1, 0)[0:tm], last_groups, seq_len - 1)
        return prev * cw[0:1] + u[0:tm] * cw[1:2] + nxt * cw[2:3] + cb

    def up(j, slot):
        lhs = lhs_ref[...]
        ug_ref[slot] = _dot(lhs, wup_ref[j])
        uv_ref[slot] = _dot(lhs, wup_ref[n_col_tiles + j])

    def act(j, slot):
        gate = conv(ug_ref[slot], cw_ref[j], cb_ref[j])
        val = conv(uv_ref[slot], cw_ref[n_col_tiles + j], cb_ref[n_col_tiles + j])
        return ((gate * jax.nn.sigmoid(gate)) * val).astype(BF16)

    def stage(j, slot, next_up):
        a = act(j, slot)
        if next_up:
            up(j + 1, 1 - slot)
        return _dot(a, wd_ref[j])

    def pair(i, _):
        acc_ref[...] += stage(2 * i, 0, True)
        acc_ref[...] += stage(2 * i + 1, 1, True)
        return 0

    up(0, 0)
    n_pairs = (n_col_tiles - 1) // 2
    if side_work is not None:
        side_work("before")
        for i in range(n_pairs):
            pair(i, 0)
    elif n_pairs:
        lax.fori_loop(0, n_pairs, pair, 0)
    j0 = 2 * n_pairs
    if n_col_tiles - j0 == 2:
        acc_ref[...] += stage(j0, 0, True)
        tail = stage(j0 + 1, 1, False)
    else:
        tail = stage(j0, 0, False)
    g2 = mod_ref[0][:, 5 * d:6 * d]
    y_ref[...] = x1_ref[...] + g2 * (acc_ref[...] + tail)
    if side_work is not None:
        side_work("after")


def _mix_ffn_parts(o_att, o_ml, x, mod, mod_row, seq_len, wts):
    n_tok, d = x.shape
    tm = _token_tile(n_tok, seq_len)
    hpt = tm // HALO
    n_halo = n_tok // HALO
    nct, tn, _ = wts["w_down"].shape
    mix = o_att.shape[1] + o_ml.shape[1]

    def tile_and_halos(width):
        return [pl.BlockSpec((tm, width), lambda i: (i, 0)),
                pl.BlockSpec((HALO, width), lambda i: (jnp.maximum(i * hpt - 1, 0), 0)),
                pl.BlockSpec((HALO, width), lambda i: (jnp.minimum((i + 1) * hpt, n_halo - 1), 0))]

    in_specs = (tile_and_halos(o_att.shape[1]) + tile_and_halos(o_ml.shape[1]) + tile_and_halos(d) + [
        pl.BlockSpec((1, 1, mod.shape[-1]), lambda i: (mod_row(i, tm), 0, 0)),
        _const_spec((1, d)), _const_spec((mix, d)),
        _const_spec((2 * nct, d, tn)), _const_spec((2 * nct, 3, tn)), _const_spec((2 * nct, 1, tn)),
        _const_spec((nct, tn, d))])
    args = [o_att, o_att, o_att, o_ml, o_ml, o_ml, x, x, x, mod, wts["norm2_g"], wts["w_out"],
            wts["w_up"], wts["conv_w"], wts["conv_b"], wts["w_down"]]
    scratch = [pltpu.VMEM((tm + HALO, mix), BF16), pltpu.VMEM((tm + HALO, d), BF16), pltpu.VMEM((tm, d), F32),
               pltpu.VMEM((2, tm + HALO, tn), F32), pltpu.VMEM((2, tm + HALO, tn), F32), pltpu.VMEM((tm, d), F32)]
    return (n_tok // tm, nct, in_specs, args, pl.BlockSpec((tm, d), lambda i: (i, 0)),
            jax.ShapeDtypeStruct((n_tok, d), F32), scratch)


def _mix_ffn(o_att, o_ml, x, mod, mod_row, seq_len, wts):
    n_tiles, nct, in_specs, args, out_spec, out_shape, scratch = _mix_ffn_parts(
        o_att, o_ml, x, mod, mod_row, seq_len, wts)
    return pl.pallas_call(
        functools.partial(_ffn_kernel, seq_len=seq_len, n_col_tiles=nct),
        grid=(n_tiles,),
        in_specs=in_specs,
        out_specs=out_spec,
        out_shape=out_shape,
        scratch_shapes=scratch,
        compiler_params=_params(1),
        name="mix_ffn",
    )(*args)


def _ffn_nbr_kernel(*refs, n_in, seq_len, n_col_tiles, rows, kr, rows_per_tile, tiles_per_batch):
    ffn_in = refs[:n_in]
    q_ref, k_ref, vt_ref, kc_ref, vct_ref, bias_ref = refs[n_in:n_in + 6]
    y_ref, o_ref = refs[n_in + 6:n_in + 8]
    scratch = refs[n_in + 8:]
    ffn_scratch, pt_refs = scratch[:-2], scratch[-2:]
    base = (pl.program_id(0) % tiles_per_batch) * rows_per_tile

    def row(local, pt_ref):
        rr = pl.ds(pl.multiple_of(local * GRID_W, GRID_W), GRID_W)
        _nbr_row(base + local, q_ref.at[rr, :], k_ref, vt_ref, kc_ref, vct_ref, bias_ref, o_ref.at[rr, :], pt_ref,
                 rows=rows, kr=kr)

    n_before = rows_per_tile // 2

    def side_work(where):
        if where == "before":
            for local in range(n_before):
                row(local, pt_refs[local % 2])
        elif where == "after":
            for local in range(n_before, rows_per_tile):
                row(local, pt_refs[local % 2])

    _ffn_kernel(*ffn_in, y_ref, *ffn_scratch, seq_len=seq_len, n_col_tiles=n_col_tiles, side_work=side_work)


def _mix_ffn_with_nbr_attention(o_att, o_ml, x, mod, mod_row, seq_len, wts, q, k, vt, k_ctx, vt_ctx, rpb,
                                n_batch, nbr_seq):
    n_tiles, nct, in_specs, args, out_spec, out_shape, scratch = _mix_ffn_parts(
        o_att, o_ml, x, mod, mod_row, seq_len, wts)
    rows = nbr_seq // GRID_W
    kr = min(WIN_ROWS, rows)
    n_win = kr + 2
    assert rows >= n_win and (rows - n_win) % 2 == 0
    rpt = (n_batch * rows) // n_tiles
    tpb = n_tiles // n_batch
    bias = _nbr_bias_table(rpb, rows)
    past = k_ctx.shape[1]
    gw = HEAD_GROUP * ATT_HEAD_DIM
    n_groups = N_ATT_HEADS // HEAD_GROUP
    row_blk = pl.BlockSpec((rpt * GRID_W, ATT_WIDTH), lambda i: (i, 0))
    once = pl.Buffered(1)
    nbr_specs = [row_blk,
                 pl.BlockSpec((nbr_seq, ATT_WIDTH), lambda i: (i // tpb, 0), pipeline_mode=once),
                 pl.BlockSpec((ATT_WIDTH, nbr_seq), lambda i: (0, i // tpb), pipeline_mode=once),
                 pl.BlockSpec((1, past, ATT_WIDTH), lambda i: (i // tpb, 0, 0), pipeline_mode=once),
                 pl.BlockSpec((1, ATT_WIDTH, past), lambda i: (i // tpb, 0, 0), pipeline_mode=once),
                 _const_spec(bias.shape)]
    pt = pltpu.VMEM((n_groups, n_win * GRID_W, gw), BF16)
    y, o_lat = pl.pallas_call(
        functools.partial(_ffn_nbr_kernel, n_in=len(in_specs), seq_len=seq_len, n_col_tiles=nct, rows=rows, kr=kr,
                          rows_per_tile=rpt, tiles_per_batch=tpb),
        grid=(n_tiles,),
        in_specs=in_specs + nbr_specs,
        out_specs=[out_spec, row_blk],
        out_shape=[out_shape, jax.ShapeDtypeStruct((n_batch * nbr_seq, ATT_WIDTH), BF16)],
        scratch_shapes=scratch + [pt, pt],
        compiler_params=_params(1),
        name="mix_ffn_ctx_nbr_attn",
    )(*args, q, k, vt, k_ctx, vt_ctx, bias)
    return y, o_lat


def _retile_cast_kernel(w_ref, o_ref):
    tn = o_ref.shape[-1]
    for t in range(o_ref.shape[0]):
        o_ref[t] = w_ref[:, t * tn:(t + 1) * tn].astype(o_ref.dtype)


def _retile_cast(w, tn):
    r, c = w.shape
    group = 2 if (c // tn) % 2 == 0 else 1
    return pl.pallas_call(
        _retile_cast_kernel,
        grid=(c // (tn * group),),
        in_specs=[pl.BlockSpec((r, tn * group), lambda j: (0, j))],
        out_specs=pl.BlockSpec((group, r, tn), lambda j: (j, 0, 0)),
        out_shape=jax.ShapeDtypeStruct((c // tn, r, tn), BF16),
        compiler_params=_params(1),
        name="retile_cast",
    )(w)


def _cast_kernel(w_ref, o_ref):
    o_ref[...] = w_ref[...].astype(o_ref.dtype)


def _cast_rows(w, block_rows):
    r, c = w.shape
    spec = pl.BlockSpec((block_rows, c), lambda i: (i, 0))
    return pl.pallas_call(
        _cast_kernel,
        grid=(r // block_rows,),
        in_specs=[spec],
        out_specs=spec,
        out_shape=jax.ShapeDtypeStruct((r, c), BF16),
        compiler_params=_params(1),
        name="cast_rows",
    )(w)


def _layer_weights(norm1_g, norm2_g, w_in, b_gate, q_g, k_g, ml_g, w_out, w_up, conv_w, conv_b, w_down):
    d = w_in.shape[0]
    a, m, nh = ATT_WIDTH, ML_WIDTH, N_ML_HEADS
    def gate_lanes(g):
        g = g.reshape(g.shape[0], 2, 2, nh).transpose(0, 2, 3, 1).reshape(g.shape[0], 4 * nh)
        return jnp.pad(g, ((0, 0), (0, LANES - 4 * nh)))

    w_g = gate_lanes(w_in[:, 3 * a + 4 * m:])
    b_g = gate_lanes(b_gate.astype(F32).reshape(1, 4 * nh))
    head_id = jnp.arange(a) // ATT_HEAD_DIM
    d_ff = w_down.shape[0]
    tn = MXU_WIDTH
    nct = d_ff // tn

    def col_tiles(w):
        return w.reshape(w.shape[0], 2 * nct, tn).transpose(1, 0, 2)

    return {
        "norm1_g": norm1_g.reshape(1, d), "norm2_g": norm2_g.reshape(1, d),
        "w_in": w_in.astype(BF16),
        "w_g": w_g.astype(BF16), "b_g": b_g,
        "q_g": jnp.tile(q_g, N_ATT_HEADS).reshape(1, a), "k_g": jnp.tile(k_g, N_ATT_HEADS).reshape(1, a),
        "blockdiag": (head_id[:, None] == head_id[None, :]).astype(BF16),
        "ml_g": ml_g.reshape(1, m),
        "w_out": w_out.astype(BF16),
        "conv_w": col_tiles(conv_w.astype(F32)), "conv_b": col_tiles(conv_b.astype(F32).reshape(1, -1)),
        "w_up_f32": w_up, "w_down_f32": w_down, "ffn_tile": tn,
    }


def _rope_tables(seq_len):
    quarter = ML_HEAD_DIM // 4
    pos = np.arange(seq_len)
    inv_freq = ROPE_THETA ** (-np.arange(quarter, dtype=np.float64) / quarter)
    ang_r = (pos // GRID_W).astype(np.float64)[:, None] * inv_freq[None, :]
    ang_c = (pos % GRID_W).astype(np.float64)[:, None] * inv_freq[None, :]
    cos = np.concatenate([np.cos(ang_r)] * 2 + [np.cos(ang_c)] * 2, axis=-1)
    sin = np.concatenate([-np.sin(ang_r), np.sin(ang_r), -np.sin(ang_c), np.sin(ang_c)], axis=-1)
    return jnp.asarray(cos, F32), jnp.asarray(sin, F32)


def _layer(xp, xs, mod, wts, rope, ctx_kv, rpb, state, batch, seq, dec_batch, dec_seq):
    def ctx_row(i, tm):
        return 0 * i

    def lat_row(i, tm):
        return 1 + (i * tm) // dec_seq

    w_up, w_down, tn = wts["w_up_f32"], wts["w_down_f32"], wts["ffn_tile"]
    n_ctx_tiles = (batch * seq) // _token_tile(batch * seq, seq)
    if n_ctx_tiles >= max(w_up.shape[1] // (2 * tn), w_down.shape[0] // tn) and (w_up.shape[1] // tn) % 2 == 0:
        ctx, w_up_t, w_down_t = _ctx_in_projection_with_weight_casts(xp, mod, ctx_row, seq, wts, w_up, w_down, tn)
    else:
        ctx = _in_projection(xp, mod, ctx_row, seq, wts, None, latent=False)
        w_up_t = _retile_cast(w_up, tn)
        w_down_t = _cast_rows(w_down, tn).reshape(w_down.shape[0] // tn, tn, w_down.shape[1])
    wts = dict(wts, w_up=w_up_t, w_down=w_down_t)
    cache_k, cache_v = ctx[8], ctx[9]
    n_lat_tiles = (dec_batch * dec_seq) // _token_tile(dec_batch * dec_seq, dec_seq)
    if batch % n_lat_tiles == 0:
        lat, (o_att_c, o_ml_c, c_f, n_f, m_f) = _latent_in_projection_with_ctx_mixers(
            xs, mod, lat_row, dec_seq, wts, rope, ctx[:8], batch, seq)
    else:
        aq, ak, av, mq, mk, mvt, mot, gtt = ctx[:8]
        o_att_c = _context_attention(aq, ak, av, seq)
        o_ml_c, c_f, n_f, m_f = _mlstm(mq, mk, mvt, mot, gtt, wts["ml_g"], None, batch, seq, min(ML_CHUNK, seq))
        lat = _in_projection(xs, mod, lat_row, dec_seq, wts, rope, latent=True)
    aq, ak, avt, mq, mk, mvt, mot, gtt = lat[:8]
    n_rows = dec_batch * (dec_seq // GRID_W)
    if n_rows % n_ctx_tiles == 0 and n_ctx_tiles % dec_batch == 0:
        xp, o_att = _mix_ffn_with_nbr_attention(o_att_c, o_ml_c, xp, mod, ctx_row, seq, wts, aq, ak, avt,
                                                ctx_kv[0], ctx_kv[1], rpb, dec_batch, dec_seq)
    else:
        xp = _mix_ffn(o_att_c, o_ml_c, xp, mod, ctx_row, seq, wts)
        o_att = _neighborhood_attention(aq, ak, avt, ctx_kv[0], ctx_kv[1], rpb, dec_batch, dec_seq)
    o_ml = _mlstm(mq, mk, mvt, mot, gtt, wts["ml_g"], state, dec_batch, dec_seq, min(ML_CHUNK, dec_seq))[0]
    xs = _mix_ffn(o_att, o_ml, xs, mod, lat_row, dec_seq, wts)
    return xp, xs, cache_k, cache_v, (c_f, n_f[:, :, :, 0, :], _unpack_gate_rows(m_f))


def kernel(x_prompt, x_sample, cache_k, cache_v, state_C, state_n, state_m, c, c_ctx, w_mod, b_mod, norm1_g,
           norm2_g, w_in, b_gate, q_norm_g, k_norm_g, rpb, ml_norm_g, w_out, w_up, conv_w, conv_b, w_down):
    batch, seq, d = x_prompt.shape
    dec_batch, dec_seq, _ = x_sample.shape
    depth = w_mod.shape[0]
    past = cache_k.shape[2]
    cvecs = jnp.concatenate([c_ctx[None, :], c], axis=0).astype(F32)
    rope = _rope_tables(dec_seq)

    xp = x_prompt.reshape(batch * seq, d)
    xs = x_sample.reshape(dec_batch * dec_seq, d)
    ks, vs, cs, ns, ms = [], [], [], [], []
    for l in range(depth):
        wts = _layer_weights(norm1_g[l], norm2_g[l], w_in[l], b_gate[l], q_norm_g[l], k_norm_g[l],
                             ml_norm_g[l], w_out[l], w_up[l], conv_w[l], conv_b[l], w_down[l])
        mod = _modulation(cvecs, w_mod[l], b_mod[l])

        state = (state_C[:, l].astype(F32),
                 state_n[:, l].astype(F32)[:, :, :, None, :],
                 _pack_gate_rows(state_m[:, l].astype(F32)))
        ctx_kv = (cache_k[:, l].reshape(dec_batch, past, ATT_WIDTH).astype(BF16),
                  cache_v[:, l].reshape(dec_batch, past, ATT_WIDTH).transpose(0, 2, 1).astype(BF16))
        xp, xs, ak, av, st = _layer(xp, xs, mod, wts, rope, ctx_kv, rpb[l], state, batch, seq, dec_batch, dec_seq)
        ks.append(ak.reshape(batch, seq, N_ATT_HEADS, ATT_HEAD_DIM))
        vs.append(av.reshape(batch, seq, N_ATT_HEADS, ATT_HEAD_DIM))
        cs.append(st[0])
        ns.append(st[1])
        ms.append(st[2])
    return (xp.reshape(batch, seq, d), xs.reshape(dec_batch, dec_seq, d),
            jnp.stack(ks, axis=1), jnp.stack(vs, axis=1),
            jnp.stack(cs, axis=1), jnp.stack(ns, axis=1), jnp.stack(ms, axis=1))
```

```python
import functools

import jax
import jax.numpy as jnp
import numpy as np
from jax import lax
from jax.experimental import pallas as pl
from jax.experimental.pallas import tpu as pltpu

F32 = jnp.float32
BF16 = jnp.bfloat16

GRID_W = 64
N_ATT_HEADS = 8
ATT_HEAD_DIM = 64
ATT_WIDTH = N_ATT_HEADS * ATT_HEAD_DIM
WIN_ROWS = 8
WIN_COLS = 16
HEAD_GROUP = 4
N_ML_HEADS = 4
ML_HEAD_DIM = 128
ML_WIDTH = N_ML_HEADS * ML_HEAD_DIM
ROPE_THETA = 10000.0
EPS = 1e-6

LANES = 128
MXU_WIDTH = 256
ML_CHUNK = 256
MLSTM_UNROLL = 8
GATE_ROWS = 8
HALO = 16
TOKEN_TILE = 512
VMEM_LIMIT = 56 * 1024 * 1024


def _dot(a, b):
    return jnp.dot(a, b, preferred_element_type=F32)


def _dot_nt(a, b):
    return lax.dot_general(a, b, (((1,), (1,)), ((), ())), preferred_element_type=F32)


def _dot_tn(a, b):
    return lax.dot_general(a, b, (((0,), (0,)), ((), ())), preferred_element_type=F32)


def _const_spec(shape):
    nd = len(shape)
    return pl.BlockSpec(shape, lambda *_: (0,) * nd, pipeline_mode=pl.Buffered(1))


def _params(n_axes):
    return pltpu.CompilerParams(dimension_semantics=("arbitrary",) * n_axes,
                                vmem_limit_bytes=VMEM_LIMIT)


def _token_tile(n_tok, seq_len):
    tm = min(TOKEN_TILE, n_tok)
    while n_tok % tm or (seq_len % tm and tm % seq_len):
        tm //= 2
    return tm


def _log_sigmoid(x):
    return jnp.minimum(x, 0.0) - jnp.log1p(jnp.exp(-jnp.abs(x)))


def _mod_kernel(c_ref, w_ref, b_ref, o_ref):
    c = c_ref[...]
    s = c * jax.nn.sigmoid(c)
    o_ref[...] = _dot(s.astype(BF16), w_ref[...].astype(BF16)) + b_ref[...]


def _modulation(cvecs, w_mod, b_mod):
    r, d = cvecs.shape
    n = w_mod.shape[1]
    tn = d
    out = pl.pallas_call(
        _mod_kernel,
        grid=(n // tn,),
        in_specs=[pl.BlockSpec((r, d), lambda j: (0, 0)),
                  pl.BlockSpec((d, tn), lambda j: (0, j)),
                  pl.BlockSpec((1, tn), lambda j: (0, j))],
        out_specs=pl.BlockSpec((r, tn), lambda j: (0, j)),
        out_shape=jax.ShapeDtypeStruct((r, n), F32),
        compiler_params=_params(1),
        name="adaln_mod",
    )(cvecs, w_mod, b_mod.reshape(1, n))
    return out.reshape(r, 1, n)


def _rope(x, cos, sin_signed):
    lane = lax.broadcasted_iota(jnp.int32, x.shape, 1)
    partner = jnp.where((lane & 32) == 0, pltpu.roll(x, LANES - 32, 1), pltpu.roll(x, 32, 1))
    return x * cos + partner * sin_signed


def _inproj_kernel(*refs, latent):
    if latent:
        (x_ref, mod_ref, g1_ref, win_ref, wg_ref, bg_ref, qg_ref, kg_ref, bd_ref, cos_ref, sin_ref,
         aq_ref, ak_ref, av_ref, mq_ref, mk_ref, mv_ref, mo_ref, gtt_ref) = refs
    else:
        (x_ref, mod_ref, g1_ref, win_ref, wg_ref, bg_ref, qg_ref, kg_ref, bd_ref,
         aq_ref, ak_ref, av_ref, mq_ref, mk_ref, mv_ref, mo_ref, gtt_ref, ck_ref, cv_ref,
         cks_ref, cvs_ref) = refs
    x = x_ref[...]
    d = x.shape[-1]
    mod = mod_ref[0]
    sh1, sc1 = mod[:, 0:d], mod[:, d:2 * d]
    y = x * lax.rsqrt(jnp.mean(x * x, axis=-1, keepdims=True) + EPS)
    h = (y * g1_ref[...]) * (1.0 + sc1) + sh1
    hb = h.astype(BF16)

    def head_norm(a, g):
        ss = _dot((a * a).astype(BF16), bd_ref[...])
        return a * lax.rsqrt(ss * (1.0 / ATT_HEAD_DIM) + EPS) * g

    w = ATT_WIDTH
    att = _dot(hb, win_ref[:, 0:3 * w])
    aq_ref[...] = (head_norm(att[:, 0:w], qg_ref[...]) * ATT_HEAD_DIM ** -0.5).astype(aq_ref.dtype)
    kn = head_norm(att[:, w:2 * w], kg_ref[...])
    av = att[:, 2 * w:3 * w]
    ak_ref[...] = kn.astype(BF16)
    if latent:
        av_ref[...] = av.T.astype(BF16)
    else:
        av_ref[...] = av.astype(BF16)
        tm = kn.shape[0]
        for src, dst, scr in ((kn, ck_ref, cks_ref), (av, cv_ref, cvs_ref)):
            for hh in range(N_ATT_HEADS):
                pair = src[:, (hh // 2) * LANES:(hh // 2 + 1) * LANES]
                if hh % 2:
                    pair = pltpu.roll(pair, ATT_HEAD_DIM, 1)
                scr[pl.ds(hh, tm, stride=N_ATT_HEADS), :] = pair
            dst[...] = scr[...].reshape(tm, N_ATT_HEADS, LANES)[:, :, 0:ATT_HEAD_DIM]

    w = ML_WIDTH
    ml0 = 3 * ATT_WIDTH
    mq = _dot(hb, win_ref[:, ml0:ml0 + w])
    mk = _dot(hb, win_ref[:, ml0 + w:ml0 + 2 * w]) * ML_HEAD_DIM ** -0.5
    if latent:
        cos, sin = cos_ref[...], sin_ref[...]
        for hh in range(N_ML_HEADS):
            sl = slice(hh * ML_HEAD_DIM, (hh + 1) * ML_HEAD_DIM)
            mq_ref[:, sl] = _rope(mq[:, sl], cos, sin).astype(BF16)
            mk_ref[:, sl] = _rope(mk[:, sl], cos, sin).astype(BF16)
    else:
        mq_ref[...] = mq.astype(BF16)
        mk_ref[...] = mk.astype(BF16)
    mv_ref[...] = _dot(hb, win_ref[:, ml0 + 2 * w:ml0 + 3 * w]).T.astype(BF16)
    mo_ref[...] = _dot(hb, win_ref[:, ml0 + 3 * w:ml0 + 4 * w]).T
    gates = _dot(hb, wg_ref[...]) + bg_ref[...]
    gtt_ref[...] = gates.T[0:2 * GATE_ROWS]


def _in_projection_parts(x, mod, mod_row, seq_len, wts, rope, *, latent):
    n_tok, d = x.shape
    tm = _token_tile(n_tok, seq_len)
    tiles_per_seq = max(seq_len // tm, 1)

    def tok(width):
        return pl.BlockSpec((tm, width), lambda i: (i, 0))

    in_specs = [tok(d),
                pl.BlockSpec((1, 1, mod.shape[-1]), lambda i: (mod_row(i, tm), 0, 0)),
                _const_spec((1, d)),
                _const_spec(wts["w_in"].shape),
                _const_spec(wts["w_g"].shape), _const_spec((1, LANES)),
                _const_spec((1, ATT_WIDTH)), _const_spec((1, ATT_WIDTH)),
                _const_spec((ATT_WIDTH, ATT_WIDTH))]
    args = [x, mod, wts["norm1_g"], wts["w_in"], wts["w_g"], wts["b_g"],
            wts["q_g"], wts["k_g"], wts["blockdiag"]]
    if latent:
        in_specs += [pl.BlockSpec((tm, LANES), lambda i: (i % tiles_per_seq, 0))] * 2
        args += [rope[0], rope[1]]
    out_shape = [jax.ShapeDtypeStruct((n_tok, ATT_WIDTH), BF16),
                 jax.ShapeDtypeStruct((n_tok, ATT_WIDTH), BF16),
                 jax.ShapeDtypeStruct((ATT_WIDTH, n_tok) if latent else (n_tok, ATT_WIDTH), BF16),
                 jax.ShapeDtypeStruct((n_tok, ML_WIDTH), BF16),
                 jax.ShapeDtypeStruct((n_tok, ML_WIDTH), BF16),
                 jax.ShapeDtypeStruct((ML_WIDTH, n_tok), BF16),
                 jax.ShapeDtypeStruct((ML_WIDTH, n_tok), F32),
                 jax.ShapeDtypeStruct((2 * GATE_ROWS, n_tok), F32)]

    def tok_t(height):
        return pl.BlockSpec((height, tm), lambda i: (0, i))

    av_spec = tok_t(ATT_WIDTH) if latent else tok(ATT_WIDTH)
    out_specs = ([tok(ATT_WIDTH)] * 2 + [av_spec] + [tok(ML_WIDTH)] * 2 + [tok_t(ML_WIDTH)] * 2
                 + [tok_t(2 * GATE_ROWS)])
    if not latent:
        cache = jax.ShapeDtypeStruct((n_tok, N_ATT_HEADS, ATT_HEAD_DIM), F32)
        out_shape += [cache, cache]
        out_specs += [pl.BlockSpec((tm, N_ATT_HEADS, ATT_HEAD_DIM), lambda i: (i, 0, 0))] * 2
    scratch = [] if latent else [pltpu.VMEM((tm * N_ATT_HEADS, LANES), F32)] * 2
    return n_tok // tm, in_specs, args, out_shape, out_specs, scratch


def _in_projection(x, mod, mod_row, seq_len, wts, rope, *, latent):
    n_tiles, in_specs, args, out_shape, out_specs, scratch = _in_projection_parts(
        x, mod, mod_row, seq_len, wts, rope, latent=latent)
    return pl.pallas_call(
        functools.partial(_inproj_kernel, latent=latent),
        grid=(n_tiles,),
        in_specs=in_specs,
        out_specs=out_specs,
        out_shape=out_shape,
        scratch_shapes=scratch,
        compiler_params=_params(1),
        name="in_proj_latent" if latent else "in_proj_ctx",
    )(*args)


def _inproj_cast_kernel(*refs, n_in, n_out):
    ip_in, (wup_ref, wdn_ref) = refs[:n_in], refs[n_in:n_in + 2]
    outs = refs[n_in + 2:]
    ip_out, (wup_o, wdn_o), scratch = outs[:n_out], outs[n_out:n_out + 2], outs[n_out + 2:]
    _inproj_kernel(*ip_in, *ip_out, *scratch, latent=False)
    _retile_cast_kernel(wup_ref, wup_o)
    wdn_o[0] = wdn_ref[...].astype(wdn_o.dtype)


def _ctx_in_projection_with_weight_casts(x, mod, mod_row, seq_len, wts, w_up, w_down, tn):
    n_tiles, in_specs, args, out_shape, out_specs, scratch = _in_projection_parts(
        x, mod, mod_row, seq_len, wts, None, latent=False)
    d, two_dff = w_up.shape
    nct = w_down.shape[0] // tn
    n_grp = two_dff // (2 * tn)
    assert two_dff % (2 * tn) == 0 and max(n_grp, nct) <= n_tiles
    cast_in = [pl.BlockSpec((d, 2 * tn), lambda i: (0, jnp.minimum(i, n_grp - 1))),
               pl.BlockSpec((tn, d), lambda i: (jnp.minimum(i, nct - 1), 0))]
    cast_out = [pl.BlockSpec((2, d, tn), lambda i: (jnp.minimum(i, n_grp - 1), 0, 0)),
                pl.BlockSpec((1, tn, d), lambda i: (jnp.minimum(i, nct - 1), 0, 0))]
    cast_shape = [jax.ShapeDtypeStruct((two_dff // tn, d, tn), BF16), jax.ShapeDtypeStruct((nct, tn, d), BF16)]
    outs = pl.pallas_call(
        functools.partial(_inproj_cast_kernel, n_in=len(in_specs), n_out=len(out_specs)),
        grid=(n_tiles,),
        in_specs=in_specs + cast_in,
        out_specs=out_specs + cast_out,
        out_shape=out_shape + cast_shape,
        scratch_shapes=scratch,
        compiler_params=_params(1),
        name="in_proj_ctx_weight_casts",
    )(*args, w_up, w_down)
    return outs[:len(out_specs)], outs[len(out_specs)], outs[len(out_specs) + 1]


def _ctx_attn_kernel(q_ref, k_ref, v_ref, o_ref):
    n = q_ref.shape[0]
    gw = HEAD_GROUP * ATT_HEAD_DIM
    lane_head = lax.broadcasted_iota(jnp.int32, (n, gw), 1) // ATT_HEAD_DIM
    for g in range(N_ATT_HEADS // HEAD_GROUP):
        sl = slice(g * gw, (g + 1) * gw)
        q4 = q_ref[:, sl]
        qbd = jnp.concatenate([jnp.where(lane_head == hl, q4, jnp.zeros_like(q4))
                               for hl in range(HEAD_GROUP)], axis=0)
        s = _dot_nt(k_ref[:, sl], qbd)
        p = jnp.exp(s - jnp.max(s, axis=0, keepdims=True))
        p = p / jnp.sum(p, axis=0, keepdims=True)
        o4 = _dot_tn(p.astype(BF16), v_ref[:, sl])
        out = jnp.where(lane_head == 0, o4[0:n], 0.0)
        for hl in range(1, HEAD_GROUP):
            out = out + jnp.where(lane_head == hl, o4[hl * n:(hl + 1) * n], 0.0)
        o_ref[:, sl] = out.astype(o_ref.dtype)


def _context_attention(q, k, v, seq_len):
    n_tok = q.shape[0]
    spec = pl.BlockSpec((seq_len, ATT_WIDTH), lambda b: (b, 0))
    return pl.pallas_call(
        _ctx_attn_kernel,
        grid=(n_tok // seq_len,),
        in_specs=[spec, spec, spec],
        out_specs=spec,
        out_shape=jax.ShapeDtypeStruct((n_tok, ATT_WIDTH), BF16),
        compiler_params=_params(1),
        name="ctx_attn",
    )(q, k, v)


def _nbr_attn_kernel(q_ref, k_ref, vt_ref, kc_ref, vct_ref, bias_ref, o_ref, pt_ref, *, rows, kr):
    _nbr_row(pl.program_id(1), q_ref, k_ref, vt_ref, kc_ref, vct_ref, bias_ref, o_ref, pt_ref, rows=rows, kr=kr)


def _nbr_row(r, q_ref, k_ref, vt_ref, kc_ref, vct_ref, bias_ref, o_ref, pt_ref, *, rows, kr, group_ids=None):
    n_win = kr + 2
    n_loc = kr * GRID_W
    gw = HEAD_GROUP * ATT_HEAD_DIM
    groups = N_ATT_HEADS // HEAD_GROUP
    lane_head = lax.broadcasted_iota(jnp.int32, (GRID_W, gw), 1) // ATT_HEAD_DIM
    rs = jnp.clip(r - kr // 2, 0, rows - kr)
    start = jnp.minimum(rs - (rs & 1), rows - n_win)
    delta = rs - start
    bias0 = pl.multiple_of((WIN_ROWS - 1 - (r - rs)) * GRID_W, GRID_W)
    zeros2 = jnp.zeros((2 * GRID_W, gw), BF16)
    for g in (range(groups) if group_ids is None else group_ids):
        sl = slice(g * gw, (g + 1) * gw)
        q4 = q_ref[:, sl]
        qbd = jnp.concatenate([jnp.where(lane_head == hl, q4, jnp.zeros_like(q4))
                               for hl in range(HEAD_GROUP)], axis=0)
        s_loc = _dot_nt(k_ref[pl.ds(pl.multiple_of(rs * GRID_W, GRID_W), n_loc), sl], qbd)
        s_loc = s_loc + bias_ref[g, pl.ds(bias0, n_loc), :]
        s_ctx = _dot_nt(kc_ref[0, :, sl], qbd)
        m = jnp.maximum(jnp.max(s_loc, axis=0, keepdims=True), jnp.max(s_ctx, axis=0, keepdims=True))
        p_loc = jnp.exp(s_loc - m)
        p_ctx = jnp.exp(s_ctx - m)
        l = jnp.sum(p_loc, axis=0, keepdims=True) + jnp.sum(p_ctx, axis=0, keepdims=True)
        pt_ref[g, 0:2 * GRID_W, :] = zeros2
        pt_ref[g, n_loc:n_loc + 2 * GRID_W, :] = zeros2
        pt_ref[g, pl.ds(pl.multiple_of(delta * GRID_W, GRID_W), n_loc), :] = p_loc.astype(BF16)
        vt_win = vt_ref[sl, pl.ds(pl.multiple_of(start * GRID_W, 2 * GRID_W), n_win * GRID_W)]
        ot = _dot(vt_win, pt_ref[g]) + _dot(vct_ref[0, sl, :], p_ctx.astype(BF16))
        o4 = (ot / l).T
        out = jnp.where(lane_head == 0, o4[0:GRID_W], 0.0)
        for hl in range(1, HEAD_GROUP):
            out = out + jnp.where(lane_head == hl, o4[hl * GRID_W:(hl + 1) * GRID_W], 0.0)
        o_ref[:, sl] = out.astype(o_ref.dtype)


def _nbr_bias_table(rpb, rows):
    col = jnp.arange(GRID_W)
    cs = jnp.clip(col - WIN_COLS // 2, 0, GRID_W - WIN_COLS)
    in_win = (col[None, :] >= cs[:, None]) & (col[None, :] < cs[:, None] + WIN_COLS)
    dc_idx = jnp.clip(col[None, :] - col[:, None] + (WIN_COLS - 1), 0, 2 * WIN_COLS - 2)
    n_dr, n_dc = rpb.shape[1], rpb.shape[2]
    onehot = (dc_idx[None] == jnp.arange(n_dc)[:, None, None]).astype(F32)
    t = jnp.einsum("hrc,cqk->hrqk", rpb.astype(F32), onehot, precision=lax.Precision.HIGHEST)
    t = jnp.where(in_win[None, None], t, -jnp.inf)
    n_groups = N_ATT_HEADS // HEAD_GROUP
    t = t.reshape(n_groups, HEAD_GROUP, n_dr, GRID_W, GRID_W).transpose(0, 2, 4, 1, 3)
    return t.reshape(n_groups, n_dr * GRID_W, HEAD_GROUP * GRID_W)


def _neighborhood_attention(q, k, vt, k_ctx, vt_ctx, rpb, n_batch, seq_len):
    rows = seq_len // GRID_W
    kr = min(WIN_ROWS, rows)
    n_win = kr + 2
    assert rows >= n_win and (rows - n_win) % 2 == 0
    bias = _nbr_bias_table(rpb, rows)
    past = k_ctx.shape[1]
    gw = HEAD_GROUP * ATT_HEAD_DIM
    n_groups = N_ATT_HEADS // HEAD_GROUP
    row_spec = pl.BlockSpec((GRID_W, ATT_WIDTH), lambda b, r: (b * rows + r, 0))
    return pl.pallas_call(
        functools.partial(_nbr_attn_kernel, rows=rows, kr=kr),
        grid=(n_batch, rows),
        in_specs=[row_spec,
                  pl.BlockSpec((seq_len, ATT_WIDTH), lambda b, r: (b, 0)),
                  pl.BlockSpec((ATT_WIDTH, seq_len), lambda b, r: (0, b)),
                  pl.BlockSpec((1, past, ATT_WIDTH), lambda b, r: (b, 0, 0)),
                  pl.BlockSpec((1, ATT_WIDTH, past), lambda b, r: (b, 0, 0)),
                  _const_spec(bias.shape)],
        out_specs=row_spec,
        out_shape=jax.ShapeDtypeStruct((n_batch * seq_len, ATT_WIDTH), BF16),
        scratch_shapes=[pltpu.VMEM((n_groups, n_win * GRID_W, gw), BF16)],
        compiler_params=_params(2),
        name="nbr_attn",
    )(q, k, vt, k_ctx, vt_ctx, bias)


def _mlstm_kernel(*refs, chunk, n_chunks, has_state):
    dh = ML_HEAD_DIM
    nrep = 2 * GATE_ROWS
    if has_state:
        (q_ref, k_ref, vt_ref, mot_ref, gtt_ref, g_ref, c0_ref, n0_ref, m0_ref,
         o_ref, c_ref, n_ref, m_ref, hf_ref, hb_ref, cn_ref, mrun_ref) = refs
        for d in range(2):
            for hh in range(N_ML_HEADS):
                cn_ref[d, hh, 0:dh, :] = c0_ref[0, d, hh].T
                cn_ref[d, hh, dh:dh + nrep, :] = jnp.broadcast_to(n0_ref[0, d, hh], (nrep, dh))
        mrun_ref[...] = m0_ref[0]
    else:
        (q_ref, k_ref, vt_ref, mot_ref, gtt_ref, g_ref,
         o_ref, c_ref, n_ref, m_ref, hf_ref, hb_ref, cn_ref, mrun_ref) = refs
        cn_ref[...] = jnp.zeros_like(cn_ref)
        mrun_ref[...] = jnp.zeros_like(mrun_ref)
    use_state = has_state or n_chunks > 1

    i0 = lax.broadcasted_iota(jnp.int32, (chunk, chunk), 0)
    i1 = lax.broadcasted_iota(jnp.int32, (chunk, chunk), 1)
    row_id = lax.broadcasted_iota(jnp.int32, (GATE_ROWS, 1), 0)
    instances = [(hh, d) for d in range(2) for hh in range(N_ML_HEADS)]

    def split3(x):
        hi = x.astype(BF16)
        r1 = x - hi.astype(F32)
        mid = r1.astype(BF16)
        return hi, mid, (r1 - mid.astype(F32)).astype(BF16)

    def body(i, _):
        chunks = (i, n_chunks - 1 - i)
        rows = [pl.ds(pl.multiple_of(c * chunk, chunk), chunk) for c in chunks]
        le = [i0 <= i1, i0 >= i1]
        m_prev = mrun_ref[...][:, 0:1]

        def qkv(hh, d):
            sl = slice(hh * dh, (hh + 1) * dh)
            return q_ref[rows[d], sl], k_ref[rows[d], sl], vt_ref[sl, rows[d]]

        st = [_dot_nt(qkv(hh, d)[1], qkv(hh, d)[0]) for hh, d in instances]
        if use_state:
            cn = [cn_ref[d, hh] for hh, d in instances]
            qct = [_dot_nt(cn[n].astype(BF16), qkv(hh, d)[0]) for n, (hh, d) in enumerate(instances)]

        a_col, a8, cs8, w8, decay8, m_new = [], [], [], [], [], []
        for d in range(2):
            gi_r = gtt_ref[0:GATE_ROWS, rows[d]]
            lf_r = _log_sigmoid(gtt_ref[GATE_ROWS:2 * GATE_ROWS, rows[d]])
            tri_r = jnp.where(le[d], 1.0, 0.0).astype(BF16)
            cs_r = sum(_dot(t, tri_r) for t in split3(lf_r))
            b_last = jnp.sum(lf_r, axis=1, keepdims=True)
            w_end = b_last + gi_r - cs_r
            m_d = jnp.maximum(b_last + m_prev, jnp.max(w_end, axis=1, keepdims=True))
            a8.append(gi_r - cs_r)
            cs8.append(cs_r)
            decay8.append(jnp.exp(b_last + m_prev - m_d))
            w8.append(jnp.exp(w_end - m_d))
            m_new.append(m_d)
            pad = jnp.zeros((LANES - GATE_ROWS, chunk), F32)
            a_col.append(jnp.concatenate([gi_r - cs_r, pad], axis=0).T)
        m_next = jnp.where((row_id & 1) == 0, m_new[0], m_new[1])
        mrun_ref[...] = jnp.broadcast_to(m_next, (GATE_ROWS, LANES))

        pt_all, g_all, den_all = [], [], []
        for n, (hh, d) in enumerate(instances):
            j = 2 * hh + d
            a = jnp.where(le[d], a_col[d][:, j:j + 1], -jnp.inf)
            g = jnp.maximum(jnp.max(a, axis=0, keepdims=True), m_prev[j:j + 1])
            pt = st[n] * jnp.exp(a - g)
            pt_all.append(pt)
            g_all.append(g)
            den_all.append(jnp.sum(pt, axis=0, keepdims=True))

        for n, (hh, d) in enumerate(instances):
            j = 2 * hh + d
            sl = slice(hh * dh, (hh + 1) * dh)
            _, k, vt = qkv(hh, d)
            g, den = g_all[n], den_all[n]
            num = _dot(vt, pt_all[n].astype(BF16))
            if use_state:
                w_inter = jnp.exp(m_prev[j:j + 1] - g)
                num = num + w_inter * qct[n][0:dh]
                den = den + w_inter * qct[n][dh:dh + 1]
            scale = 1.0 / jnp.maximum(jnp.abs(den), jnp.exp(-(cs8[d][j:j + 1] + g)))
            (hb_ref if d else hf_ref)[sl, rows[d]] = num * scale
            w_row = w8[d][j:j + 1]
            lhs = jnp.concatenate([(vt.astype(F32) * w_row).astype(BF16),
                                   jnp.broadcast_to(w_row, (nrep, chunk)).astype(BF16)], axis=0)
            upd = _dot(lhs, k)
            if use_state:
                upd = decay8[d][j:j + 1] * cn[n] + upd
            cn_ref[d, hh] = upd
        return 0

    def finish(c, _):
        rows = pl.ds(pl.multiple_of(c * chunk, chunk), chunk)
        for hh in range(N_ML_HEADS):
            sl = slice(hh * dh, (hh + 1) * dh)
            hs = hf_ref[sl, rows] + hb_ref[sl, rows]
            y = hs * lax.rsqrt(jnp.mean(hs * hs, axis=0, keepdims=True) + EPS) * g_ref[sl, :]
            y = y * jax.nn.sigmoid(mot_ref[sl, rows])
            o_ref[rows, sl] = y.T.astype(o_ref.dtype)
        return 0

    if n_chunks <= MLSTM_UNROLL:
        for c in range(n_chunks):
            body(c, 0)
        for c in range(n_chunks):
            finish(c, 0)
    else:
        lax.fori_loop(0, n_chunks, body, 0)
        lax.fori_loop(0, n_chunks, finish, 0)
    for d in range(2):
        for hh in range(N_ML_HEADS):
            cn = cn_ref[d, hh]
            c_ref[0, d, hh] = cn[0:dh].T
            n_ref[0, d, hh] = cn[dh:dh + 1]
    m_ref[0] = mrun_ref[...]


def _mlstm(mq, mk, mvt, mot, gates_t, ml_g, state, n_batch, seq_len, chunk):
    dh = ML_HEAD_DIM
    nh = N_ML_HEADS
    width = nh * dh
    has_state = state is not None
    seq = pl.BlockSpec((seq_len, width), lambda b: (b, 0))
    seq_t = pl.BlockSpec((width, seq_len), lambda b: (0, b))
    st_c = pl.BlockSpec((1, 2, nh, dh, dh), lambda b: (b, 0, 0, 0, 0))
    st_n = pl.BlockSpec((1, 2, nh, 1, dh), lambda b: (b, 0, 0, 0, 0))
    st_m = pl.BlockSpec((1, GATE_ROWS, LANES), lambda b: (b, 0, 0))
    in_specs = [seq, seq, seq_t, seq_t,
                pl.BlockSpec((2 * GATE_ROWS, seq_len), lambda b: (0, b)),
                _const_spec((width, chunk))]
    args = [mq, mk, mvt, mot, gates_t, jnp.broadcast_to(ml_g.reshape(width, 1), (width, chunk))]
    if has_state:
        in_specs += [st_c, st_n, st_m]
        args += list(state)
    out_shape = [jax.ShapeDtypeStruct((n_batch * seq_len, width), BF16),
                 jax.ShapeDtypeStruct((n_batch, 2, nh, dh, dh), F32),
                 jax.ShapeDtypeStruct((n_batch, 2, nh, 1, dh), F32),
                 jax.ShapeDtypeStruct((n_batch, GATE_ROWS, LANES), F32)]
    return pl.pallas_call(
        functools.partial(_mlstm_kernel, chunk=chunk, n_chunks=seq_len // chunk, has_state=has_state),
        grid=(n_batch,),
        in_specs=in_specs,
        out_specs=[seq, st_c, st_n, st_m],
        out_shape=out_shape,
        scratch_shapes=[pltpu.VMEM((width, seq_len), F32), pltpu.VMEM((width, seq_len), F32),
                        pltpu.VMEM((2, nh, dh + 2 * GATE_ROWS, dh), F32), pltpu.VMEM((GATE_ROWS, LANES), F32)],
        compiler_params=_params(1),
        name="mlstm_latent" if has_state else "mlstm_ctx",
    )(*args)


def _pack_gate_rows(m):
    b = m.shape[0]
    packed = m.transpose(0, 2, 1).reshape(b, 2 * N_ML_HEADS, 1)
    return jnp.broadcast_to(packed, (b, GATE_ROWS, LANES))


def _unpack_gate_rows(m):
    b = m.shape[0]
    return m[:, :, 0].reshape(b, N_ML_HEADS, 2).transpose(0, 2, 1)


def _inproj_ctxmix_kernel(*refs, n_in, n_out, n_seq, seq_len, chunk):
    ip_in = refs[:n_in]
    q_ref, k_ref, v_ref, mq_ref, mk_ref, mvt_ref, mot_ref, gtt_ref, g_ref = refs[n_in:n_in + 9]
    outs = refs[n_in + 9:]
    ip_out = outs[:n_out]
    oatt_ref, oml_ref, c_ref, n_ref, m_ref = outs[n_out:n_out + 5]
    scratch = outs[n_out + 5:]
    _inproj_kernel(*ip_in, *ip_out, latent=True)
    for s in range(n_seq):
        rows = slice(s * seq_len, (s + 1) * seq_len)
        _ctx_attn_kernel(q_ref.at[rows, :], k_ref.at[rows, :], v_ref.at[rows, :], oatt_ref.at[rows, :])
        _mlstm_kernel(mq_ref.at[rows, :], mk_ref.at[rows, :], mvt_ref.at[:, rows], mot_ref.at[:, rows],
                      gtt_ref.at[:, rows], g_ref, oml_ref.at[rows, :], c_ref.at[s:s + 1], n_ref.at[s:s + 1],
                      m_ref.at[s:s + 1], *scratch[4 * s:4 * s + 4],
                      chunk=chunk, n_chunks=seq_len // chunk, has_state=False)


def _latent_in_projection_with_ctx_mixers(x, mod, mod_row, seq_len, wts, rope, ctx, n_ctx_batch, ctx_seq):
    n_tiles, in_specs, args, out_shape, out_specs, scratch = _in_projection_parts(
        x, mod, mod_row, seq_len, wts, rope, latent=True)
    n_seq = n_ctx_batch // n_tiles
    rows = n_seq * ctx_seq
    n_ctx_tok = n_ctx_batch * ctx_seq
    chunk = min(ML_CHUNK, ctx_seq)
    nh, dh = N_ML_HEADS, ML_HEAD_DIM

    def tok(width):
        return pl.BlockSpec((rows, width), lambda i: (i, 0))

    def tok_t(height):
        return pl.BlockSpec((height, rows), lambda i: (0, i))

    def per_seq(*tail):
        return pl.BlockSpec((n_seq,) + tail, lambda i: (i,) + (0,) * len(tail))

    aq, ak, av, mq, mk, mvt, mot, gtt = ctx
    mix_in_specs = ([tok(ATT_WIDTH)] * 3 + [tok(ML_WIDTH)] * 2 + [tok_t(ML_WIDTH)] * 2
                    + [tok_t(2 * GATE_ROWS), _const_spec((ML_WIDTH, chunk))])
    mix_args = [aq, ak, av, mq, mk, mvt, mot, gtt,
                jnp.broadcast_to(wts["ml_g"].reshape(ML_WIDTH, 1), (ML_WIDTH, chunk))]
    mix_out_shape = [jax.ShapeDtypeStruct((n_ctx_tok, ATT_WIDTH), BF16),
                     jax.ShapeDtypeStruct((n_ctx_tok, ML_WIDTH), BF16),
                     jax.ShapeDtypeStruct((n_ctx_batch, 2, nh, dh, dh), F32),
                     jax.ShapeDtypeStruct((n_ctx_batch, 2, nh, 1, dh), F32),
                     jax.ShapeDtypeStruct((n_ctx_batch, GATE_ROWS, LANES), F32)]
    mix_out_specs = [tok(ATT_WIDTH), tok(ML_WIDTH), per_seq(2, nh, dh, dh), per_seq(2, nh, 1, dh),
                     per_seq(GATE_ROWS, LANES)]
    mix_scratch = [pltpu.VMEM((ML_WIDTH, ctx_seq), F32), pltpu.VMEM((ML_WIDTH, ctx_seq), F32),
                   pltpu.VMEM((2, nh, dh + 2 * GATE_ROWS, dh), F32), pltpu.VMEM((GATE_ROWS, LANES), F32)] * n_seq
    outs = pl.pallas_call(
        functools.partial(_inproj_ctxmix_kernel, n_in=len(in_specs), n_out=len(out_specs), n_seq=n_seq,
                          seq_len=ctx_seq, chunk=chunk),
        grid=(n_tiles,),
        in_specs=in_specs + mix_in_specs,
        out_specs=out_specs + mix_out_specs,
        out_shape=out_shape + mix_out_shape,
        scratch_shapes=scratch + mix_scratch,
        compiler_params=_params(1),
        name="in_proj_latent_ctx_mixers",
    )(*args, *mix_args)
    return outs[:len(out_specs)], outs[len(out_specs):]


def _ffn_kernel(oa_ref, oap_ref, oan_ref, om_ref, omp_ref, omn_ref, x_ref, xp_ref, xn_ref, mod_ref, g2_ref,
                wout_ref, wup_ref, cw_ref, cb_ref, wd_ref,
                y_ref, oc_ref, lhs_ref, x1_ref, ug_ref, uv_ref, act_ref, *, seq_len, n_col_tiles, side_work=None):
    tm, d = x_ref.shape
    aw = oa_ref.shape[1]

    def halo(next_ref, prev_ref, cols=slice(None)):
        row = lax.broadcasted_iota(jnp.int32, (HALO, 1), 0)
        return jnp.where(row < HALO // 2, next_ref[:, cols], prev_ref[:, cols])

    oc_ref[0:tm, 0:aw] = oa_ref[...]
    oc_ref[0:tm, aw:] = om_ref[...]
    oc_ref[tm:tm + HALO, 0:aw] = halo(oan_ref, oap_ref)
    oc_ref[tm:tm + HALO, aw:] = halo(omn_ref, omp_ref)
    n_blocks = d // MXU_WIDTH

    def out_proj(n):
        cols = slice(n * MXU_WIDTH, (n + 1) * MXU_WIDTH)
        g1 = mod_ref[0][:, 2 * d:3 * d][:, cols]
        out = _dot(oc_ref[...], wout_ref[:, cols])
        x1_ref[0:tm, cols] = x_ref[:, cols] + g1 * out[0:tm]
        x1_ref[tm:tm + HALO, cols] = halo(xn_ref, xp_ref, cols) + g1 * out[tm:tm + HALO]

    def norm2():
        mod = mod_ref[0]
        sh2, sc2 = mod[:, 3 * d:4 * d], mod[:, 4 * d:5 * d]
        x1 = x1_ref[...]
        y = x1 * lax.rsqrt(jnp.mean(x1 * x1, axis=-1, keepdims=True) + EPS)
        lhs_ref[...] = ((y * g2_ref[...]) * (1.0 + sc2) + sh2).astype(BF16)

    sub = 8
    period = min(seq_len, tm)
    first_groups = sorted({r // sub for r in range(0, tm, period)})
    last_groups = sorted({(r + period - 1) // sub for r in range(0, tm, period)})
    tile0 = pl.program_id(0) * tm

    def zero_rows(x, groups, target):
        parts, at = [], 0
        for grp in groups:
            lo = grp * sub
            if lo > at:
                parts.append(x[at:lo])
            pos = (tile0 + lo + lax.broadcasted_iota(jnp.int32, (sub, 1), 0)) % seq_len
            parts.append(jnp.where(pos == target, 0.0, x[lo:lo + sub]))
            at = lo + sub
        if at < tm:
            parts.append(x[at:tm])
        return jnp.concatenate(parts, axis=0)

    def conv(u, cw, cb):
        prev = zero_rows(pltpu.roll(u, 1, 0)[0:tm], first_groups, 0)
        nxt = zero_rows(pltpu.roll(u, tm + HALO - 1, 0)[0:tm], last_groups, seq_len - 1)
        return prev * cw[0:1] + u[0:tm] * cw[1:2] + nxt * cw[2:3] + cb

    def up_gate(j, slot):
        ug_ref[slot] = _dot(lhs_ref[...], wup_ref[j])

    def up_val(j, slot):
        uv_ref[slot] = _dot(lhs_ref[...], wup_ref[n_col_tiles + j])

    def act(j, slot):
        gate = conv(ug_ref[slot], cw_ref[j], cb_ref[j])
        val = conv(uv_ref[slot], cw_ref[n_col_tiles + j], cb_ref[n_col_tiles + j])
        return ((gate * jax.nn.sigmoid(gate)) * val).astype(BF16)

    def stage(j, slot, next_up):
        a = act(j, slot)
        if next_up:
            up_gate(j + 1, 1 - slot)
            up_val(j + 1, 1 - slot)
        tn = a.shape[1]
        cols = slice(j * tn, (j + 1) * tn) if isinstance(j, int) else pl.ds(pl.multiple_of(j * tn, tn), tn)
        act_ref[:, cols] = a

    def down(n):
        cols = slice(n * MXU_WIDTH, (n + 1) * MXU_WIDTH)
        g2 = mod_ref[0][:, 5 * d:6 * d][:, cols]
        y_ref[:, cols] = x1_ref[0:tm, cols] + g2 * _dot(act_ref[...], wd_ref[:, cols])

    def first_up():
        norm2()
        up_gate(0, 0)

    head = [(functools.partial(out_proj, n), True) for n in range(n_blocks)]
    head += [(first_up, True), (functools.partial(up_val, 0, 0), True)]
    tail = [(functools.partial(down, n), True) for n in range(n_blocks)]

    if side_work:
        stages = [(functools.partial(stage, j, j % 2, j + 1 < n_col_tiles), False) for j in range(n_col_tiles)]
        steps = head + stages + tail
        counts = [0] * len(steps)
        idle = [i for i, (_, mxu_only) in enumerate(steps) if mxu_only][:len(side_work)]
        for i in idle:
            counts[i] = 1
        busy = [i for i, (_, mxu_only) in enumerate(steps) if not mxu_only]
        n_rest = len(side_work) - len(idle)
        for k in range(n_rest):
            counts[busy[(k * len(busy)) // n_rest]] += 1
        work = iter(side_work)
        for (thunk, _), count in zip(steps, counts):
            for _ in range(count):
                next(work)()
            thunk()
        return

    def pair(i, _):
        stage(2 * i, 0, True)
        stage(2 * i + 1, 1, True)
        return 0

    for thunk, _ in head:
        thunk()
    n_pairs = (n_col_tiles - 1) // 2
    if n_pairs:
        lax.fori_loop(0, n_pairs, pair, 0)
    j0 = 2 * n_pairs
    if n_col_tiles - j0 == 2:
        stage(j0, 0, True)
        stage(j0 + 1, 1, False)
    else:
        stage(j0, 0, False)
    for thunk, _ in tail:
        thunk()


def _mix_ffn_parts(o_att, o_ml, x, mod, mod_row, seq_len, wts):
    n_tok, d = x.shape
    tm = _token_tile(n_tok, seq_len)
    hpt = tm // HALO
    n_halo = n_tok // HALO
    nct, tn, _ = wts["w_down"].shape
    mix = o_att.shape[1] + o_ml.shape[1]

    def tile_and_halos(width):
        return [pl.BlockSpec((tm, width), lambda i: (i, 0)),
                pl.BlockSpec((HALO, width), lambda i: (jnp.maximum(i * hpt - 1, 0), 0)),
                pl.BlockSpec((HALO, width), lambda i: (jnp.minimum((i + 1) * hpt, n_halo - 1), 0))]

    in_specs = (tile_and_halos(o_att.shape[1]) + tile_and_halos(o_ml.shape[1]) + tile_and_halos(d) + [
        pl.BlockSpec((1, 1, mod.shape[-1]), lambda i: (mod_row(i, tm), 0, 0)),
        _const_spec((1, d)), _const_spec((mix, d)),
        _const_spec((2 * nct, d, tn)), _const_spec((2 * nct, 3, tn)), _const_spec((2 * nct, 1, tn)),
        _const_spec((nct * tn, d))])
    args = [o_att, o_att, o_att, o_ml, o_ml, o_ml, x, x, x, mod, wts["norm2_g"], wts["w_out"],
            wts["w_up"], wts["conv_w"], wts["conv_b"], wts["w_down"].reshape(nct * tn, d)]
    scratch = [pltpu.VMEM((tm + HALO, mix), BF16), pltpu.VMEM((tm + HALO, d), BF16), pltpu.VMEM((tm + HALO, d), F32),
               pltpu.VMEM((2, tm + HALO, tn), F32), pltpu.VMEM((2, tm + HALO, tn), F32),
               pltpu.VMEM((tm, nct * tn), BF16)]
    return (n_tok // tm, nct, in_specs, args, pl.BlockSpec((tm, d), lambda i: (i, 0)),
            jax.ShapeDtypeStruct((n_tok, d), F32), scratch)


def _mix_ffn(o_att, o_ml, x, mod, mod_row, seq_len, wts):
    n_tiles, nct, in_specs, args, out_spec, out_shape, scratch = _mix_ffn_parts(
        o_att, o_ml, x, mod, mod_row, seq_len, wts)
    return pl.pallas_call(
        functools.partial(_ffn_kernel, seq_len=seq_len, n_col_tiles=nct),
        grid=(n_tiles,),
        in_specs=in_specs,
        out_specs=out_spec,
        out_shape=out_shape,
        scratch_shapes=scratch,
        compiler_params=_params(1),
        name="mix_ffn",
    )(*args)


def _ffn_nbr_kernel(*refs, n_in, seq_len, n_col_tiles, rows, kr, rows_per_tile, tiles_per_batch):
    ffn_in = refs[:n_in]
    q_ref, k_ref, vt_ref, kc_ref, vct_ref, bias_ref = refs[n_in:n_in + 6]
    y_ref, o_ref = refs[n_in + 6:n_in + 8]
    scratch = refs[n_in + 8:]
    ffn_scratch, pt_refs = scratch[:-2], scratch[-2:]
    base = (pl.program_id(0) % tiles_per_batch) * rows_per_tile

    def head_group(local, g):
        rr = pl.ds(pl.multiple_of(local * GRID_W, GRID_W), GRID_W)
        _nbr_row(base + local, q_ref.at[rr, :], k_ref, vt_ref, kc_ref, vct_ref, bias_ref, o_ref.at[rr, :],
                 pt_refs[local % 2], rows=rows, kr=kr, group_ids=(g,))

    side_work = [functools.partial(head_group, local, g)
                 for local in range(rows_per_tile) for g in range(N_ATT_HEADS // HEAD_GROUP)]
    _ffn_kernel(*ffn_in, y_ref, *ffn_scratch, seq_len=seq_len, n_col_tiles=n_col_tiles, side_work=side_work)


def _mix_ffn_with_nbr_attention(o_att, o_ml, x, mod, mod_row, seq_len, wts, q, k, vt, k_ctx, vt_ctx, rpb,
                                n_batch, nbr_seq):
    n_tiles, nct, in_specs, args, out_spec, out_shape, scratch = _mix_ffn_parts(
        o_att, o_ml, x, mod, mod_row, seq_len, wts)
    rows = nbr_seq // GRID_W
    kr = min(WIN_ROWS, rows)
    n_win = kr + 2
    assert rows >= n_win and (rows - n_win) % 2 == 0
    rpt = (n_batch * rows) // n_tiles
    tpb = n_tiles // n_batch
    bias = _nbr_bias_table(rpb, rows)
    past = k_ctx.shape[1]
    gw = HEAD_GROUP * ATT_HEAD_DIM
    n_groups = N_ATT_HEADS // HEAD_GROUP
    row_blk = pl.BlockSpec((rpt * GRID_W, ATT_WIDTH), lambda i: (i, 0))
    once = pl.Buffered(1)
    nbr_specs = [row_blk,
                 pl.BlockSpec((nbr_seq, ATT_WIDTH), lambda i: (i // tpb, 0), pipeline_mode=once),
                 pl.BlockSpec((ATT_WIDTH, nbr_seq), lambda i: (0, i // tpb), pipeline_mode=once),
                 pl.BlockSpec((1, past, ATT_WIDTH), lambda i: (i // tpb, 0, 0), pipeline_mode=once),
                 pl.BlockSpec((1, ATT_WIDTH, past), lambda i: (i // tpb, 0, 0), pipeline_mode=once),
                 _const_spec(bias.shape)]
    pt = pltpu.VMEM((n_groups, n_win * GRID_W, gw), BF16)
    y, o_lat = pl.pallas_call(
        functools.partial(_ffn_nbr_kernel, n_in=len(in_specs), seq_len=seq_len, n_col_tiles=nct, rows=rows, kr=kr,
                          rows_per_tile=rpt, tiles_per_batch=tpb),
        grid=(n_tiles,),
        in_specs=in_specs + nbr_specs,
        out_specs=[out_spec, row_blk],
        out_shape=[out_shape, jax.ShapeDtypeStruct((n_batch * nbr_seq, ATT_WIDTH), BF16)],
        scratch_shapes=scratch + [pt, pt],
        compiler_params=_params(1),
        name="mix_ffn_ctx_nbr_attn",
    )(*args, q, k, vt, k_ctx, vt_ctx, bias)
    return y, o_lat


def _retile_cast_kernel(w_ref, o_ref):
    tn = o_ref.shape[-1]
    for t in range(o_ref.shape[0]):
        o_ref[t] = w_ref[:, t * tn:(t + 1) * tn].astype(o_ref.dtype)


def _retile_cast(w, tn):
    r, c = w.shape
    group = 2 if (c // tn) % 2 == 0 else 1
    return pl.pallas_call(
        _retile_cast_kernel,
        grid=(c // (tn * group),),
        in_specs=[pl.BlockSpec((r, tn * group), lambda j: (0, j))],
        out_specs=pl.BlockSpec((group, r, tn), lambda j: (j, 0, 0)),
        out_shape=jax.ShapeDtypeStruct((c // tn, r, tn), BF16),
        compiler_params=_params(1),
        name="retile_cast",
    )(w)


def _cast_kernel(w_ref, o_ref):
    o_ref[...] = w_ref[...].astype(o_ref.dtype)


def _cast_rows(w, block_rows):
    r, c = w.shape
    spec = pl.BlockSpec((block_rows, c), lambda i: (i, 0))
    return pl.pallas_call(
        _cast_kernel,
        grid=(r // block_rows,),
        in_specs=[spec],
        out_specs=spec,
        out_shape=jax.ShapeDtypeStruct((r, c), BF16),
        compiler_params=_params(1),
        name="cast_rows",
    )(w)


def _layer_weights(norm1_g, norm2_g, w_in, b_gate, q_g, k_g, ml_g, w_out, w_up, conv_w, conv_b, w_down):
    d = w_in.shape[0]
    a, m, nh = ATT_WIDTH, ML_WIDTH, N_ML_HEADS
    def gate_lanes(g):
        g = g.reshape(g.shape[0], 2, 2, nh).transpose(0, 2, 3, 1).reshape(g.shape[0], 4 * nh)
        return jnp.pad(g, ((0, 0), (0, LANES - 4 * nh)))

    w_g = gate_lanes(w_in[:, 3 * a + 4 * m:])
    b_g = gate_lanes(b_gate.astype(F32).reshape(1, 4 * nh))
    head_id = jnp.arange(a) // ATT_HEAD_DIM
    d_ff = w_down.shape[0]
    tn = MXU_WIDTH
    nct = d_ff // tn

    def col_tiles(w):
        return w.reshape(w.shape[0], 2 * nct, tn).transpose(1, 0, 2)

    return {
        "norm1_g": norm1_g.reshape(1, d), "norm2_g": norm2_g.reshape(1, d),
        "w_in": w_in.astype(BF16),
        "w_g": w_g.astype(BF16), "b_g": b_g,
        "q_g": jnp.tile(q_g, N_ATT_HEADS).reshape(1, a), "k_g": jnp.tile(k_g, N_ATT_HEADS).reshape(1, a),
        "blockdiag": (head_id[:, None] == head_id[None, :]).astype(BF16),
        "ml_g": ml_g.reshape(1, m),
        "w_out": w_out.astype(BF16),
        "conv_w": col_tiles(conv_w.astype(F32)), "conv_b": col_tiles(conv_b.astype(F32).reshape(1, -1)),
        "w_up_f32": w_up, "w_down_f32": w_down, "ffn_tile": tn,
    }


def _rope_tables(seq_len):
    quarter = ML_HEAD_DIM // 4
    pos = np.arange(seq_len)
    inv_freq = ROPE_THETA ** (-np.arange(quarter, dtype=np.float64) / quarter)
    ang_r = (pos // GRID_W).astype(np.float64)[:, None] * inv_freq[None, :]
    ang_c = (pos % GRID_W).astype(np.float64)[:, None] * inv_freq[None, :]
    cos = np.concatenate([np.cos(ang_r)] * 2 + [np.cos(ang_c)] * 2, axis=-1)
    sin = np.concatenate([-np.sin(ang_r), np.sin(ang_r), -np.sin(ang_c), np.sin(ang_c)], axis=-1)
    return jnp.asarray(cos, F32), jnp.asarray(sin, F32)


def _layer(xp, xs, mod, wts, rope, ctx_kv, rpb, state, batch, seq, dec_batch, dec_seq):
    def ctx_row(i, tm):
        return 0 * i

    def lat_row(i, tm):
        return 1 + (i * tm) // dec_seq

    w_up, w_down, tn = wts["w_up_f32"], wts["w_down_f32"], wts["ffn_tile"]
    n_ctx_tiles = (batch * seq) // _token_tile(batch * seq, seq)
    if n_ctx_tiles >= max(w_up.shape[1] // (2 * tn), w_down.shape[0] // tn) and (w_up.shape[1] // tn) % 2 == 0:
        ctx, w_up_t, w_down_t = _ctx_in_projection_with_weight_casts(xp, mod, ctx_row, seq, wts, w_up, w_down, tn)
    else:
        ctx = _in_projection(xp, mod, ctx_row, seq, wts, None, latent=False)
        w_up_t = _retile_cast(w_up, tn)
        w_down_t = _cast_rows(w_down, tn).reshape(w_down.shape[0] // tn, tn, w_down.shape[1])
    wts = dict(wts, w_up=w_up_t, w_down=w_down_t)
    cache_k, cache_v = ctx[8], ctx[9]
    n_lat_tiles = (dec_batch * dec_seq) // _token_tile(dec_batch * dec_seq, dec_seq)
    if batch % n_lat_tiles == 0:
        lat, (o_att_c, o_ml_c, c_f, n_f, m_f) = _latent_in_projection_with_ctx_mixers(
            xs, mod, lat_row, dec_seq, wts, rope, ctx[:8], batch, seq)
    else:
        aq, ak, av, mq, mk, mvt, mot, gtt = ctx[:8]
        o_att_c = _context_attention(aq, ak, av, seq)
        o_ml_c, c_f, n_f, m_f = _mlstm(mq, mk, mvt, mot, gtt, wts["ml_g"], None, batch, seq, min(ML_CHUNK, seq))
        lat = _in_projection(xs, mod, lat_row, dec_seq, wts, rope, latent=True)
    aq, ak, avt, mq, mk, mvt, mot, gtt = lat[:8]
    n_rows = dec_batch * (dec_seq // GRID_W)
    if n_rows % n_ctx_tiles == 0 and n_ctx_tiles % dec_batch == 0:
        xp, o_att = _mix_ffn_with_nbr_attention(o_att_c, o_ml_c, xp, mod, ctx_row, seq, wts, aq, ak, avt,
                                                ctx_kv[0], ctx_kv[1], rpb, dec_batch, dec_seq)
    else:
        xp = _mix_ffn(o_att_c, o_ml_c, xp, mod, ctx_row, seq, wts)
        o_att = _neighborhood_attention(aq, ak, avt, ctx_kv[0], ctx_kv[1], rpb, dec_batch, dec_seq)
    o_ml = _mlstm(mq, mk, mvt, mot, gtt, wts["ml_g"], state, dec_batch, dec_seq, min(ML_CHUNK, dec_seq))[0]
    xs = _mix_ffn(o_att, o_ml, xs, mod, lat_row, dec_seq, wts)
    return xp, xs, cache_k, cache_v, (c_f, n_f[:, :, :, 0, :], _unpack_gate_rows(m_f))


def kernel(x_prompt, x_sample, cache_k, cache_v, state_C, state_n, state_m, c, c_ctx, w_mod, b_mod, norm1_g,
           norm2_g, w_in, b_gate, q_norm_g, k_norm_g, rpb, ml_norm_g, w_out, w_up, conv_w, conv_b, w_down):
    batch, seq, d = x_prompt.shape
    dec_batch, dec_seq, _ = x_sample.shape
    depth = w_mod.shape[0]
    past = cache_k.shape[2]
    cvecs = jnp.concatenate([c_ctx[None, :], c], axis=0).astype(F32)
    rope = _rope_tables(dec_seq)

    xp = x_prompt.reshape(batch * seq, d)
    xs = x_sample.reshape(dec_batch * dec_seq, d)
    ks, vs, cs, ns, ms = [], [], [], [], []
    for l in range(depth):
        wts = _layer_weights(norm1_g[l], norm2_g[l], w_in[l], b_gate[l], q_norm_g[l], k_norm_g[l],
                             ml_norm_g[l], w_out[l], w_up[l], conv_w[l], conv_b[l], w_down[l])
        mod = _modulation(cvecs, w_mod[l], b_mod[l])

        state = (state_C[:, l].astype(F32),
                 state_n[:, l].astype(F32)[:, :, :, None, :],
                 _pack_gate_rows(state_m[:, l].astype(F32)))
        ctx_kv = (cache_k[:, l].reshape(dec_batch, past, ATT_WIDTH).astype(BF16),
                  cache_v[:, l].reshape(dec_batch, past, ATT_WIDTH).transpose(0, 2, 1).astype(BF16))
        xp, xs, ak, av, st = _layer(xp, xs, mod, wts, rope, ctx_kv, rpb[l], state, batch, seq, dec_batch, dec_seq)
        ks.append(ak.reshape(batch, seq, N_ATT_HEADS, ATT_HEAD_DIM))
        vs.append(av.reshape(batch, seq, N_ATT_HEADS, ATT_HEAD_DIM))
        cs.append(st[0])
        ns.append(st[1])
        ms.append(st[2])
    return (xp.reshape(batch, seq, d), xs.reshape(dec_batch, dec_seq, d),
            jnp.stack(ks, axis=1), jnp.stack(vs, axis=1),
            jnp.stack(cs, axis=1), jnp.stack(ns, axis=1), jnp.stack(ms, axis=1))
```

```python
import functools

import jax
import jax.numpy as jnp
import numpy as np
from jax import lax
from jax.experimental import pallas as pl
from jax.experimental.pallas import tpu as pltpu

F32 = jnp.float32
BF16 = jnp.bfloat16

GRID_W = 64
N_ATT_HEADS = 8
ATT_HEAD_DIM = 64
ATT_WIDTH = N_ATT_HEADS * ATT_HEAD_DIM
WIN_ROWS = 8
WIN_COLS = 16
HEAD_GROUP = 4
N_ML_HEADS = 4
ML_HEAD_DIM = 128
ML_WIDTH = N_ML_HEADS * ML_HEAD_DIM
ROPE_THETA = 10000.0
EPS = 1e-6

LANES = 128
MXU_WIDTH = 256
ML_CHUNK = 256
MLSTM_UNROLL = 8
GATE_ROWS = 8
HALO = 16
TOKEN_TILE = 512
VMEM_LIMIT = 56 * 1024 * 1024


def _dot(a, b):
    return jnp.dot(a, b, preferred_element_type=F32)


def _dot_nt(a, b):
    return lax.dot_general(a, b, (((1,), (1,)), ((), ())), preferred_element_type=F32)


def _dot_tn(a, b):
    return lax.dot_general(a, b, (((0,), (0,)), ((), ())), preferred_element_type=F32)


def _const_spec(shape):
    nd = len(shape)
    return pl.BlockSpec(shape, lambda *_: (0,) * nd, pipeline_mode=pl.Buffered(1))


def _params(n_axes):
    return pltpu.CompilerParams(dimension_semantics=("arbitrary",) * n_axes,
                                vmem_limit_bytes=VMEM_LIMIT)


def _token_tile(n_tok, seq_len):
    tm = min(TOKEN_TILE, n_tok)
    while n_tok % tm or (seq_len % tm and tm % seq_len):
        tm //= 2
    return tm


def _log_sigmoid(x):
    return jnp.minimum(x, 0.0) - jnp.log1p(jnp.exp(-jnp.abs(x)))


def _mod_kernel(c_ref, w_ref, b_ref, o_ref):
    c = c_ref[...]
    s = c * jax.nn.sigmoid(c)
    o_ref[...] = _dot(s.astype(BF16), w_ref[...].astype(BF16)) + b_ref[...]


def _modulation(cvecs, w_mod, b_mod):
    r, d = cvecs.shape
    n = w_mod.shape[1]
    tn = d
    out = pl.pallas_call(
        _mod_kernel,
        grid=(n // tn,),
        in_specs=[pl.BlockSpec((r, d), lambda j: (0, 0)),
                  pl.BlockSpec((d, tn), lambda j: (0, j)),
                  pl.BlockSpec((1, tn), lambda j: (0, j))],
        out_specs=pl.BlockSpec((r, tn), lambda j: (0, j)),
        out_shape=jax.ShapeDtypeStruct((r, n), F32),
        compiler_params=_params(1),
        name="adaln_mod",
    )(cvecs, w_mod, b_mod.reshape(1, n))
    return out.reshape(r, 1, n)


def _rope(x, cos, sin_signed):
    lane = lax.broadcasted_iota(jnp.int32, x.shape, 1)
    partner = jnp.where((lane & 32) == 0, pltpu.roll(x, LANES - 32, 1), pltpu.roll(x, 32, 1))
    return x * cos + partner * sin_signed


def _inproj_kernel(*refs, latent):
    if latent:
        (x_ref, mod_ref, g1_ref, win_ref, wg_ref, bg_ref, qg_ref, kg_ref, bd_ref, cos_ref, sin_ref,
         aq_ref, ak_ref, av_ref, mq_ref, mk_ref, mv_ref, mo_ref, gtt_ref) = refs
    else:
        (x_ref, mod_ref, g1_ref, win_ref, wg_ref, bg_ref, qg_ref, kg_ref, bd_ref,
         aq_ref, ak_ref, av_ref, mq_ref, mk_ref, mv_ref, mo_ref, gtt_ref, ck_ref, cv_ref,
         cks_ref, cvs_ref) = refs
    x = x_ref[...]
    d = x.shape[-1]
    mod = mod_ref[0]
    sh1, sc1 = mod[:, 0:d], mod[:, d:2 * d]
    y = x * lax.rsqrt(jnp.mean(x * x, axis=-1, keepdims=True) + EPS)
    h = (y * g1_ref[...]) * (1.0 + sc1) + sh1
    hb = h.astype(BF16)

    def head_norm(a, g):
        ss = _dot((a * a).astype(BF16), bd_ref[...])
        return a * lax.rsqrt(ss * (1.0 / ATT_HEAD_DIM) + EPS) * g

    w = ATT_WIDTH
    att = _dot(hb, win_ref[:, 0:3 * w])
    aq_ref[...] = (head_norm(att[:, 0:w], qg_ref[...]) * ATT_HEAD_DIM ** -0.5).astype(aq_ref.dtype)
    kn = head_norm(att[:, w:2 * w], kg_ref[...])
    av = att[:, 2 * w:3 * w]
    ak_ref[...] = kn.astype(BF16)
    if latent:
        av_ref[...] = av.T.astype(BF16)
    else:
        av_ref[...] = av.astype(BF16)
        tm = kn.shape[0]
        for src, dst, scr in ((kn, ck_ref, cks_ref), (av, cv_ref, cvs_ref)):
            for hh in range(N_ATT_HEADS):
                pair = src[:, (hh // 2) * LANES:(hh // 2 + 1) * LANES]
                if hh % 2:
                    pair = pltpu.roll(pair, ATT_HEAD_DIM, 1)
                scr[pl.ds(hh, tm, stride=N_ATT_HEADS), :] = pair
            dst[...] = scr[...].reshape(tm, N_ATT_HEADS, LANES)[:, :, 0:ATT_HEAD_DIM]

    w = ML_WIDTH
    ml0 = 3 * ATT_WIDTH
    mq = _dot(hb, win_ref[:, ml0:ml0 + w])
    mk = _dot(hb, win_ref[:, ml0 + w:ml0 + 2 * w]) * ML_HEAD_DIM ** -0.5
    if latent:
        cos, sin = cos_ref[...], sin_ref[...]
        for hh in range(N_ML_HEADS):
            sl = slice(hh * ML_HEAD_DIM, (hh + 1) * ML_HEAD_DIM)
            mq_ref[:, sl] = _rope(mq[:, sl], cos, sin).astype(BF16)
            mk_ref[:, sl] = _rope(mk[:, sl], cos, sin).astype(BF16)
    else:
        mq_ref[...] = mq.astype(BF16)
        mk_ref[...] = mk.astype(BF16)
    mv_ref[...] = _dot(hb, win_ref[:, ml0 + 2 * w:ml0 + 3 * w]).T.astype(BF16)
    mo_ref[...] = _dot(hb, win_ref[:, ml0 + 3 * w:ml0 + 4 * w]).T
    gates = _dot(hb, wg_ref[...]) + bg_ref[...]
    gtt_ref[...] = gates.T[0:2 * GATE_ROWS]


def _in_projection_parts(x, mod, mod_row, seq_len, wts, rope, *, latent):
    n_tok, d = x.shape
    tm = _token_tile(n_tok, seq_len)
    tiles_per_seq = max(seq_len // tm, 1)

    def tok(width):
        return pl.BlockSpec((tm, width), lambda i: (i, 0))

    in_specs = [tok(d),
                pl.BlockSpec((1, 1, mod.shape[-1]), lambda i: (mod_row(i, tm), 0, 0)),
                _const_spec((1, d)),
                _const_spec(wts["w_in"].shape),
                _const_spec(wts["w_g"].shape), _const_spec((1, LANES)),
                _const_spec((1, ATT_WIDTH)), _const_spec((1, ATT_WIDTH)),
                _const_spec((ATT_WIDTH, ATT_WIDTH))]
    args = [x, mod, wts["norm1_g"], wts["w_in"], wts["w_g"], wts["b_g"],
            wts["q_g"], wts["k_g"], wts["blockdiag"]]
    if latent:
        in_specs += [pl.BlockSpec((tm, LANES), lambda i: (i % tiles_per_seq, 0))] * 2
        args += [rope[0], rope[1]]
    out_shape = [jax.ShapeDtypeStruct((n_tok, ATT_WIDTH), BF16),
                 jax.ShapeDtypeStruct((n_tok, ATT_WIDTH), BF16),
                 jax.ShapeDtypeStruct((ATT_WIDTH, n_tok) if latent else (n_tok, ATT_WIDTH), BF16),
                 jax.ShapeDtypeStruct((n_tok, ML_WIDTH), BF16),
                 jax.ShapeDtypeStruct((n_tok, ML_WIDTH), BF16),
                 jax.ShapeDtypeStruct((ML_WIDTH, n_tok), BF16),
                 jax.ShapeDtypeStruct((ML_WIDTH, n_tok), F32),
                 jax.ShapeDtypeStruct((2 * GATE_ROWS, n_tok), F32)]

    def tok_t(height):
        return pl.BlockSpec((height, tm), lambda i: (0, i))

    av_spec = tok_t(ATT_WIDTH) if latent else tok(ATT_WIDTH)
    out_specs = ([tok(ATT_WIDTH)] * 2 + [av_spec] + [tok(ML_WIDTH)] * 2 + [tok_t(ML_WIDTH)] * 2
                 + [tok_t(2 * GATE_ROWS)])
    if not latent:
        cache = jax.ShapeDtypeStruct((n_tok, N_ATT_HEADS, ATT_HEAD_DIM), F32)
        out_shape += [cache, cache]
        out_specs += [pl.BlockSpec((tm, N_ATT_HEADS, ATT_HEAD_DIM), lambda i: (i, 0, 0))] * 2
    scratch = [] if latent else [pltpu.VMEM((tm * N_ATT_HEADS, LANES), F32)] * 2
    return n_tok // tm, in_specs, args, out_shape, out_specs, scratch


def _in_projection(x, mod, mod_row, seq_len, wts, rope, *, latent):
    n_tiles, in_specs, args, out_shape, out_specs, scratch = _in_projection_parts(
        x, mod, mod_row, seq_len, wts, rope, latent=latent)
    return pl.pallas_call(
        functools.partial(_inproj_kernel, latent=latent),
        grid=(n_tiles,),
        in_specs=in_specs,
        out_specs=out_specs,
        out_shape=out_shape,
        scratch_shapes=scratch,
        compiler_params=_params(1),
        name="in_proj_latent" if latent else "in_proj_ctx",
    )(*args)


def _inproj_cast_kernel(*refs, n_in, n_out):
    ip_in, (wup_ref, wdn_ref) = refs[:n_in], refs[n_in:n_in + 2]
    outs = refs[n_in + 2:]
    ip_out, (wup_o, wdn_o), scratch = outs[:n_out], outs[n_out:n_out + 2], outs[n_out + 2:]
    _inproj_kernel(*ip_in, *ip_out, *scratch, latent=False)
    _retile_cast_kernel(wup_ref, wup_o)
    wdn_o[0] = wdn_ref[...].astype(wdn_o.dtype)


def _ctx_in_projection_with_weight_casts(x, mod, mod_row, seq_len, wts, w_up, w_down, tn):
    n_tiles, in_specs, args, out_shape, out_specs, scratch = _in_projection_parts(
        x, mod, mod_row, seq_len, wts, None, latent=False)
    d, two_dff = w_up.shape
    nct = w_down.shape[0] // tn
    n_grp = two_dff // (2 * tn)
    assert two_dff % (2 * tn) == 0 and max(n_grp, nct) <= n_tiles
    cast_in = [pl.BlockSpec((d, 2 * tn), lambda i: (0, jnp.minimum(i, n_grp - 1))),
               pl.BlockSpec((tn, d), lambda i: (jnp.minimum(i, nct - 1), 0))]
    cast_out = [pl.BlockSpec((2, d, tn), lambda i: (jnp.minimum(i, n_grp - 1), 0, 0)),
                pl.BlockSpec((1, tn, d), lambda i: (jnp.minimum(i, nct - 1), 0, 0))]
    cast_shape = [jax.ShapeDtypeStruct((two_dff // tn, d, tn), BF16), jax.ShapeDtypeStruct((nct, tn, d), BF16)]
    outs = pl.pallas_call(
        functools.partial(_inproj_cast_kernel, n_in=len(in_specs), n_out=len(out_specs)),
        grid=(n_tiles,),
        in_specs=in_specs + cast_in,
        out_specs=out_specs + cast_out,
        out_shape=out_shape + cast_shape,
        scratch_shapes=scratch,
        compiler_params=_params(1),
        name="in_proj_ctx_weight_casts",
    )(*args, w_up, w_down)
    return outs[:len(out_specs)], outs[len(out_specs)], outs[len(out_specs) + 1]


def _ctx_attn_kernel(q_ref, k_ref, v_ref, o_ref):
    n = q_ref.shape[0]
    gw = HEAD_GROUP * ATT_HEAD_DIM
    lane_head = lax.broadcasted_iota(jnp.int32, (n, gw), 1) // ATT_HEAD_DIM
    for g in range(N_ATT_HEADS // HEAD_GROUP):
        sl = slice(g * gw, (g + 1) * gw)
        q4 = q_ref[:, sl]
        qbd = jnp.concatenate([jnp.where(lane_head == hl, q4, jnp.zeros_like(q4))
                               for hl in range(HEAD_GROUP)], axis=0)
        s = _dot_nt(k_ref[:, sl], qbd)
        p = jnp.exp(s - jnp.max(s, axis=0, keepdims=True))
        p = p / jnp.sum(p, axis=0, keepdims=True)
        o4 = _dot_tn(p.astype(BF16), v_ref[:, sl])
        out = jnp.where(lane_head == 0, o4[0:n], 0.0)
        for hl in range(1, HEAD_GROUP):
            out = out + jnp.where(lane_head == hl, o4[hl * n:(hl + 1) * n], 0.0)
        o_ref[:, sl] = out.astype(o_ref.dtype)


def _context_attention(q, k, v, seq_len):
    n_tok = q.shape[0]
    spec = pl.BlockSpec((seq_len, ATT_WIDTH), lambda b: (b, 0))
    return pl.pallas_call(
        _ctx_attn_kernel,
        grid=(n_tok // seq_len,),
        in_specs=[spec, spec, spec],
        out_specs=spec,
        out_shape=jax.ShapeDtypeStruct((n_tok, ATT_WIDTH), BF16),
        compiler_params=_params(1),
        name="ctx_attn",
    )(q, k, v)


def _nbr_attn_kernel(q_ref, k_ref, vt_ref, kc_ref, vct_ref, bias_ref, o_ref, pt_ref, *, rows, kr):
    _nbr_row(pl.program_id(1), q_ref, k_ref, vt_ref, kc_ref, vct_ref, bias_ref, o_ref, pt_ref, rows=rows, kr=kr)


def _nbr_row(r, q_ref, k_ref, vt_ref, kc_ref, vct_ref, bias_ref, o_ref, pt_ref, *, rows, kr):
    n_win = kr + 2
    n_loc = kr * GRID_W
    gw = HEAD_GROUP * ATT_HEAD_DIM
    groups = N_ATT_HEADS // HEAD_GROUP
    lane_head = lax.broadcasted_iota(jnp.int32, (GRID_W, gw), 1) // ATT_HEAD_DIM
    rs = jnp.clip(r - kr // 2, 0, rows - kr)
    start = jnp.minimum(rs - (rs & 1), rows - n_win)
    delta = rs - start
    bias0 = pl.multiple_of((WIN_ROWS - 1 - (r - rs)) * GRID_W, GRID_W)
    zeros2 = jnp.zeros((2 * GRID_W, gw), BF16)
    for g in range(groups):
        sl = slice(g * gw, (g + 1) * gw)
        q4 = q_ref[:, sl]
        qbd = jnp.concatenate([jnp.where(lane_head == hl, q4, jnp.zeros_like(q4))
                               for hl in range(HEAD_GROUP)], axis=0)
        s_loc = _dot_nt(k_ref[pl.ds(pl.multiple_of(rs * GRID_W, GRID_W), n_loc), sl], qbd)
        s_loc = s_loc + bias_ref[g, pl.ds(bias0, n_loc), :]
        s_ctx = _dot_nt(kc_ref[0, :, sl], qbd)
        m = jnp.maximum(jnp.max(s_loc, axis=0, keepdims=True), jnp.max(s_ctx, axis=0, keepdims=True))
        p_loc = jnp.exp(s_loc - m)
        p_ctx = jnp.exp(s_ctx - m)
        l = jnp.sum(p_loc, axis=0, keepdims=True) + jnp.sum(p_ctx, axis=0, keepdims=True)
        pt_ref[g, 0:2 * GRID_W, :] = zeros2
        pt_ref[g, n_loc:n_loc + 2 * GRID_W, :] = zeros2
        pt_ref[g, pl.ds(pl.multiple_of(delta * GRID_W, GRID_W), n_loc), :] = p_loc.astype(BF16)
        vt_win = vt_ref[sl, pl.ds(pl.multiple_of(start * GRID_W, 2 * GRID_W), n_win * GRID_W)]
        ot = _dot(vt_win, pt_ref[g]) + _dot(vct_ref[0, sl, :], p_ctx.astype(BF16))
        o4 = (ot / l).T
        out = jnp.where(lane_head == 0, o4[0:GRID_W], 0.0)
        for hl in range(1, HEAD_GROUP):
            out = out + jnp.where(lane_head == hl, o4[hl * GRID_W:(hl + 1) * GRID_W], 0.0)
        o_ref[:, sl] = out.astype(o_ref.dtype)


def _nbr_bias_table(rpb):
    n_heads, n_dr, n_dc = rpb.shape
    c = WIN_COLS - 1
    assert n_heads == N_ATT_HEADS and n_dc == 2 * c + 1 and 2 * GRID_W == LANES
    f = rpb.astype(F32)
    packed = jnp.concatenate([f[..., c::-1], jnp.zeros((n_heads, n_dr, LANES - n_dc), F32), f[..., :c:-1]], axis=-1)
    n_groups = N_ATT_HEADS // HEAD_GROUP

    def sublane_groups(heads):
        return jnp.repeat(packed[heads], 8, axis=1).reshape(-1, LANES)

    return pl.pallas_call(
        functools.partial(_nbr_bias_kernel, n_dr=n_dr),
        out_shape=jax.ShapeDtypeStruct((n_groups, n_dr * GRID_W, HEAD_GROUP * GRID_W), F32),
        name="nbr_bias_table",
    )(sublane_groups(slice(0, None, 2)), sublane_groups(slice(1, None, 2)))


def _nbr_bias_kernel(even_ref, odd_ref, o_ref, *, n_dr):
    n_rows = even_ref.shape[0]
    s = lax.broadcasted_iota(jnp.int32, (n_rows, LANES), 0) % 8
    lane = lax.broadcasted_iota(jnp.int32, (n_rows, LANES), 1)
    cs = jnp.clip(lane % GRID_W - WIN_COLS // 2, 0, GRID_W - WIN_COLS)

    def sheared(ref):
        x = ref[...]
        for bit in (1, 2, 4):
            x = jnp.where((s & bit) != 0, pltpu.roll(x, bit, 1), x)
        return x

    even, odd = sheared(even_ref), sheared(odd_ref)
    for i in range(GRID_W // 8):
        key = 8 * i + s
        t = jnp.where(lane < GRID_W, pltpu.roll(even, 8 * i, 1) if i else even,
                      pltpu.roll(odd, (8 * i + GRID_W) % LANES, 1))
        t = jnp.where((key >= cs) & (key < cs + WIN_COLS), t, -jnp.inf)
        for p in range(n_rows // (8 * n_dr)):
            g, pair = divmod(p, HEAD_GROUP // 2)
            for dr in range(n_dr):
                r0 = (p * n_dr + dr) * 8
                o_ref[g, dr * GRID_W + 8 * i:dr * GRID_W + 8 * i + 8, pair * LANES:(pair + 1) * LANES] = t[r0:r0 + 8]


def _neighborhood_attention(q, k, vt, k_ctx, vt_ctx, rpb, n_batch, seq_len):
    rows = seq_len // GRID_W
    kr = min(WIN_ROWS, rows)
    n_win = kr + 2
    assert rows >= n_win and (rows - n_win) % 2 == 0
    bias = _nbr_bias_table(rpb)
    past = k_ctx.shape[1]
    gw = HEAD_GROUP * ATT_HEAD_DIM
    n_groups = N_ATT_HEADS // HEAD_GROUP
    row_spec = pl.BlockSpec((GRID_W, ATT_WIDTH), lambda b, r: (b * rows + r, 0))
    return pl.pallas_call(
        functools.partial(_nbr_attn_kernel, rows=rows, kr=kr),
        grid=(n_batch, rows),
        in_specs=[row_spec,
                  pl.BlockSpec((seq_len, ATT_WIDTH), lambda b, r: (b, 0)),
                  pl.BlockSpec((ATT_WIDTH, seq_len), lambda b, r: (0, b)),
                  pl.BlockSpec((1, past, ATT_WIDTH), lambda b, r: (b, 0, 0)),
                  pl.BlockSpec((1, ATT_WIDTH, past), lambda b, r: (b, 0, 0)),
                  _const_spec(bias.shape)],
        out_specs=row_spec,
        out_shape=jax.ShapeDtypeStruct((n_batch * seq_len, ATT_WIDTH), BF16),
        scratch_shapes=[pltpu.VMEM((n_groups, n_win * GRID_W, gw), BF16)],
        compiler_params=_params(2),
        name="nbr_attn",
    )(q, k, vt, k_ctx, vt_ctx, bias)


def _mlstm_kernel(*refs, chunk, n_chunks, has_state):
    dh = ML_HEAD_DIM
    nrep = 2 * GATE_ROWS
    if has_state:
        (q_ref, k_ref, vt_ref, mot_ref, gtt_ref, g_ref, c0_ref, n0_ref, m0_ref,
         o_ref, c_ref, n_ref, m_ref, hf_ref, hb_ref, cn_ref, mrun_ref) = refs
        for d in range(2):
            for hh in range(N_ML_HEADS):
                cn_ref[d, hh, 0:dh, :] = c0_ref[0, d, hh].T
                cn_ref[d, hh, dh:dh + nrep, :] = jnp.broadcast_to(n0_ref[0, d, hh], (nrep, dh))
        mrun_ref[...] = m0_ref[0]
    else:
        (q_ref, k_ref, vt_ref, mot_ref, gtt_ref, g_ref,
         o_ref, c_ref, n_ref, m_ref, hf_ref, hb_ref, cn_ref, mrun_ref) = refs
        cn_ref[...] = jnp.zeros_like(cn_ref)
        mrun_ref[...] = jnp.zeros_like(mrun_ref)
    use_state = has_state or n_chunks > 1

    i0 = lax.broadcasted_iota(jnp.int32, (chunk, chunk), 0)
    i1 = lax.broadcasted_iota(jnp.int32, (chunk, chunk), 1)
    row_id = lax.broadcasted_iota(jnp.int32, (GATE_ROWS, 1), 0)
    instances = [(hh, d) for d in range(2) for hh in range(N_ML_HEADS)]

    def split3(x):
        hi = x.astype(BF16)
        r1 = x - hi.astype(F32)
        mid = r1.astype(BF16)
        return hi, mid, (r1 - mid.astype(F32)).astype(BF16)

    def body(i, _):
        chunks = (i, n_chunks - 1 - i)
        rows = [pl.ds(pl.multiple_of(c * chunk, chunk), chunk) for c in chunks]
        le = [i0 <= i1, i0 >= i1]
        m_prev = mrun_ref[...][:, 0:1]

        def qkv(hh, d):
            sl = slice(hh * dh, (hh + 1) * dh)
            return q_ref[rows[d], sl], k_ref[rows[d], sl], vt_ref[sl, rows[d]]

        st = [_dot_nt(qkv(hh, d)[1], qkv(hh, d)[0]) for hh, d in instances]
        if use_state:
            cn = [cn_ref[d, hh] for hh, d in instances]
            qct = [_dot_nt(cn[n].astype(BF16), qkv(hh, d)[0]) for n, (hh, d) in enumerate(instances)]

        a_col, a8, cs8, w8, decay8, m_new = [], [], [], [], [], []
        for d in range(2):
            gi_r = gtt_ref[0:GATE_ROWS, rows[d]]
            lf_r = _log_sigmoid(gtt_ref[GATE_ROWS:2 * GATE_ROWS, rows[d]])
            tri_r = jnp.where(le[d], 1.0, 0.0).astype(BF16)
            cs_r = sum(_dot(t, tri_r) for t in split3(lf_r))
            b_last = jnp.sum(lf_r, axis=1, keepdims=True)
            w_end = b_last + gi_r - cs_r
            m_d = jnp.maximum(b_last + m_prev, jnp.max(w_end, axis=1, keepdims=True))
            a8.append(gi_r - cs_r)
            cs8.append(cs_r)
            decay8.append(jnp.exp(b_last + m_prev - m_d))
            w8.append(jnp.exp(w_end - m_d))
            m_new.append(m_d)
            pad = jnp.zeros((LANES - GATE_ROWS, chunk), F32)
            a_col.append(jnp.concatenate([gi_r - cs_r, pad], axis=0).T)
        m_next = jnp.where((row_id & 1) == 0, m_new[0], m_new[1])
        mrun_ref[...] = jnp.broadcast_to(m_next, (GATE_ROWS, LANES))

        pt_all, g_all, den_all = [], [], []
        for n, (hh, d) in enumerate(instances):
            j = 2 * hh + d
            a = jnp.where(le[d], a_col[d][:, j:j + 1], -jnp.inf)
            g = jnp.maximum(jnp.max(a, axis=0, keepdims=True), m_prev[j:j + 1])
            pt = st[n] * jnp.exp(a - g)
            pt_all.append(pt)
            g_all.append(g)
            den_all.append(jnp.sum(pt, axis=0, keepdims=True))

        for n, (hh, d) in enumerate(instances):
            j = 2 * hh + d
            sl = slice(hh * dh, (hh + 1) * dh)
            _, k, vt = qkv(hh, d)
            g, den = g_all[n], den_all[n]
            num = _dot(vt, pt_all[n].astype(BF16))
            if use_state:
                w_inter = jnp.exp(m_prev[j:j + 1] - g)
                num = num + w_inter * qct[n][0:dh]
                den = den + w_inter * qct[n][dh:dh + 1]
            scale = 1.0 / jnp.maximum(jnp.abs(den), jnp.exp(-(cs8[d][j:j + 1] + g)))
            (hb_ref if d else hf_ref)[sl, rows[d]] = num * scale
            w_row = w8[d][j:j + 1]
            lhs = jnp.concatenate([(vt.astype(F32) * w_row).astype(BF16),
                                   jnp.broadcast_to(w_row, (nrep, chunk)).astype(BF16)], axis=0)
            upd = _dot(lhs, k)
            if use_state:
                upd = decay8[d][j:j + 1] * cn[n] + upd
            cn_ref[d, hh] = upd
        return 0

    def finish(c, _):
        rows = pl.ds(pl.multiple_of(c * chunk, chunk), chunk)
        for hh in range(N_ML_HEADS):
            sl = slice(hh * dh, (hh + 1) * dh)
            hs = hf_ref[sl, rows] + hb_ref[sl, rows]
            y = hs * lax.rsqrt(jnp.mean(hs * hs, axis=0, keepdims=True) + EPS) * g_ref[sl, :]
            y = y * jax.nn.sigmoid(mot_ref[sl, rows])
            o_ref[rows, sl] = y.T.astype(o_ref.dtype)
        return 0

    if n_chunks <= MLSTM_UNROLL:
        for c in range(n_chunks):
            body(c, 0)
        for c in range(n_chunks):
            finish(c, 0)
    else:
        lax.fori_loop(0, n_chunks, body, 0)
        lax.fori_loop(0, n_chunks, finish, 0)
    for d in range(2):
        for hh in range(N_ML_HEADS):
            cn = cn_ref[d, hh]
            c_ref[0, d, hh] = cn[0:dh].T
            n_ref[0, d, hh] = cn[dh:dh + 1]
    m_ref[0] = mrun_ref[...]


def _mlstm(mq, mk, mvt, mot, gates_t, ml_g, state, n_batch, seq_len, chunk):
    dh = ML_HEAD_DIM
    nh = N_ML_HEADS
    width = nh * dh
    has_state = state is not None
    seq = pl.BlockSpec((seq_len, width), lambda b: (b, 0))
    seq_t = pl.BlockSpec((width, seq_len), lambda b: (0, b))
    st_c = pl.BlockSpec((1, 2, nh, dh, dh), lambda b: (b, 0, 0, 0, 0))
    st_n = pl.BlockSpec((1, 2, nh, 1, dh), lambda b: (b, 0, 0, 0, 0))
    st_m = pl.BlockSpec((1, GATE_ROWS, LANES), lambda b: (b, 0, 0))
    in_specs = [seq, seq, seq_t, seq_t,
                pl.BlockSpec((2 * GATE_ROWS, seq_len), lambda b: (0, b)),
                _const_spec((width, chunk))]
    args = [mq, mk, mvt, mot, gates_t, jnp.broadcast_to(ml_g.reshape(width, 1), (width, chunk))]
    if has_state:
        in_specs += [st_c, st_n, st_m]
        args += list(state)
    out_shape = [jax.ShapeDtypeStruct((n_batch * seq_len, width), BF16),
                 jax.ShapeDtypeStruct((n_batch, 2, nh, dh, dh), F32),
                 jax.ShapeDtypeStruct((n_batch, 2, nh, 1, dh), F32),
                 jax.ShapeDtypeStruct((n_batch, GATE_ROWS, LANES), F32)]
    return pl.pallas_call(
        functools.partial(_mlstm_kernel, chunk=chunk, n_chunks=seq_len // chunk, has_state=has_state),
        grid=(n_batch,),
        in_specs=in_specs,
        out_specs=[seq, st_c, st_n, st_m],
        out_shape=out_shape,
        scratch_shapes=[pltpu.VMEM((width, seq_len), F32), pltpu.VMEM((width, seq_len), F32),
                        pltpu.VMEM((2, nh, dh + 2 * GATE_ROWS, dh), F32), pltpu.VMEM((GATE_ROWS, LANES), F32)],
        compiler_params=_params(1),
        name="mlstm_latent" if has_state else "mlstm_ctx",
    )(*args)


def _pack_gate_rows(m):
    b = m.shape[0]
    packed = m.transpose(0, 2, 1).reshape(b, 2 * N_ML_HEADS, 1)
    return jnp.broadcast_to(packed, (b, GATE_ROWS, LANES))


def _unpack_gate_rows(m):
    b = m.shape[0]
    return m[:, :, 0].reshape(b, N_ML_HEADS, 2).transpose(0, 2, 1)


def _inproj_ctxmix_kernel(*refs, n_in, n_out, n_seq, seq_len, chunk):
    ip_in = refs[:n_in]
    q_ref, k_ref, v_ref, mq_ref, mk_ref, mvt_ref, mot_ref, gtt_ref, g_ref = refs[n_in:n_in + 9]
    outs = refs[n_in + 9:]
    ip_out = outs[:n_out]
    oatt_ref, oml_ref, c_ref, n_ref, m_ref = outs[n_out:n_out + 5]
    scratch = outs[n_out + 5:]
    _inproj_kernel(*ip_in, *ip_out, latent=True)
    for s in range(n_seq):
        rows = slice(s * seq_len, (s + 1) * seq_len)
        _ctx_attn_kernel(q_ref.at[rows, :], k_ref.at[rows, :], v_ref.at[rows, :], oatt_ref.at[rows, :])
        _mlstm_kernel(mq_ref.at[rows, :], mk_ref.at[rows, :], mvt_ref.at[:, rows], mot_ref.at[:, rows],
                      gtt_ref.at[:, rows], g_ref, oml_ref.at[rows, :], c_ref.at[s:s + 1], n_ref.at[s:s + 1],
                      m_ref.at[s:s + 1], *scratch[4 * s:4 * s + 4],
                      chunk=chunk, n_chunks=seq_len // chunk, has_state=False)


def _latent_in_projection_with_ctx_mixers(x, mod, mod_row, seq_len, wts, rope, ctx, n_ctx_batch, ctx_seq):
    n_tiles, in_specs, args, out_shape, out_specs, scratch = _in_projection_parts(
        x, mod, mod_row, seq_len, wts, rope, latent=True)
    n_seq = n_ctx_batch // n_tiles
    rows = n_seq * ctx_seq
    n_ctx_tok = n_ctx_batch * ctx_seq
    chunk = min(ML_CHUNK, ctx_seq)
    nh, dh = N_ML_HEADS, ML_HEAD_DIM

    def tok(width):
        return pl.BlockSpec((rows, width), lambda i: (i, 0))

    def tok_t(height):
        return pl.BlockSpec((height, rows), lambda i: (0, i))

    def per_seq(*tail):
        return pl.BlockSpec((n_seq,) + tail, lambda i: (i,) + (0,) * len(tail))

    aq, ak, av, mq, mk, mvt, mot, gtt = ctx
    mix_in_specs = ([tok(ATT_WIDTH)] * 3 + [tok(ML_WIDTH)] * 2 + [tok_t(ML_WIDTH)] * 2
                    + [tok_t(2 * GATE_ROWS), _const_spec((ML_WIDTH, chunk))])
    mix_args = [aq, ak, av, mq, mk, mvt, mot, gtt,
                jnp.broadcast_to(wts["ml_g"].reshape(ML_WIDTH, 1), (ML_WIDTH, chunk))]
    mix_out_shape = [jax.ShapeDtypeStruct((n_ctx_tok, ATT_WIDTH), BF16),
                     jax.ShapeDtypeStruct((n_ctx_tok, ML_WIDTH), BF16),
                     jax.ShapeDtypeStruct((n_ctx_batch, 2, nh, dh, dh), F32),
                     jax.ShapeDtypeStruct((n_ctx_batch, 2, nh, 1, dh), F32),
                     jax.ShapeDtypeStruct((n_ctx_batch, GATE_ROWS, LANES), F32)]
    mix_out_specs = [tok(ATT_WIDTH), tok(ML_WIDTH), per_seq(2, nh, dh, dh), per_seq(2, nh, 1, dh),
                     per_seq(GATE_ROWS, LANES)]
    mix_scratch = [pltpu.VMEM((ML_WIDTH, ctx_seq), F32), pltpu.VMEM((ML_WIDTH, ctx_seq), F32),
                   pltpu.VMEM((2, nh, dh + 2 * GATE_ROWS, dh), F32), pltpu.VMEM((GATE_ROWS, LANES), F32)] * n_seq
    outs = pl.pallas_call(
        functools.partial(_inproj_ctxmix_kernel, n_in=len(in_specs), n_out=len(out_specs), n_seq=n_seq,
                          seq_len=ctx_seq, chunk=chunk),
        grid=(n_tiles,),
        in_specs=in_specs + mix_in_specs,
        out_specs=out_specs + mix_out_specs,
        out_shape=out_shape + mix_out_shape,
        scratch_shapes=scratch + mix_scratch,
        compiler_params=_params(1),
        name="in_proj_latent_ctx_mixers",
    )(*args, *mix_args)
    return outs[:len(out_specs)], outs[len(out_specs):]


def _ffn_kernel(oa_ref, oap_ref, oan_ref, om_ref, omp_ref, omn_ref, x_ref, xp_ref, xn_ref, mod_ref, g2_ref,
                wout_ref, wup_ref, cw_ref, cb_ref, wd_ref,
                y_ref, oc_ref, lhs_ref, x1_ref, ug_ref, uv_ref, acc_ref, *, seq_len, n_col_tiles, side_work=None):
    tm, d = x_ref.shape
    aw = oa_ref.shape[1]

    def halo(next_ref, prev_ref):
        row = lax.broadcasted_iota(jnp.int32, next_ref.shape, 0)
        return jnp.where(row < HALO // 2, next_ref[...], prev_ref[...])

    oc_ref[0:tm, 0:aw] = oa_ref[...]
    oc_ref[0:tm, aw:] = om_ref[...]
    oc_ref[tm:tm + HALO, 0:aw] = halo(oan_ref, oap_ref)
    oc_ref[tm:tm + HALO, aw:] = halo(omn_ref, omp_ref)
    mod = mod_ref[0]
    g1 = mod[:, 2 * d:3 * d]
    sh2, sc2 = mod[:, 3 * d:4 * d], mod[:, 4 * d:5 * d]
    out = _dot(oc_ref[...], wout_ref[...])

    def norm2(x1):
        y = x1 * lax.rsqrt(jnp.mean(x1 * x1, axis=-1, keepdims=True) + EPS)
        return ((y * g2_ref[...]) * (1.0 + sc2) + sh2).astype(BF16)

    x1 = x_ref[...] + g1 * out[0:tm]
    x1_ref[...] = x1
    lhs_ref[0:tm, :] = norm2(x1)
    lhs_ref[tm:tm + HALO, :] = norm2(halo(xn_ref, xp_ref) + g1 * out[tm:tm + HALO])
    acc_ref[...] = jnp.zeros_like(acc_ref)
    sub = 8
    period = min(seq_len, tm)
    first_groups = sorted({r // sub for r in range(0, tm, period)})
    last_groups = sorted({(r + period - 1) // sub for r in range(0, tm, period)})
    tile0 = pl.program_id(0) * tm

    def zero_rows(x, groups, target):
        parts, at = [], 0
        for grp in groups:
            lo = grp * sub
            if lo > at:
                parts.append(x[at:lo])
            pos = (tile0 + lo + lax.broadcasted_iota(jnp.int32, (sub, 1), 0)) % seq_len
            parts.append(jnp.where(pos == target, 0.0, x[lo:lo + sub]))
            at = lo + sub
        if at < tm:
            parts.append(x[at:tm])
        return jnp.concatenate(parts, axis=0)

    def conv(u, cw, cb):
        prev = zero_rows(pltpu.roll(u, 1, 0)[0:tm], first_groups, 0)
        nxt = zero_rows(pltpu.roll(u, tm + HALO - 1, 0)[0:tm], last_groups, seq_len - 1)
        return prev * cw[0:1] + u[0:tm] * cw[1:2] + nxt * cw[2:3] + cb

    def up(j, slot):
        lhs = lhs_ref[...]
        ug_ref[slot] = _dot(lhs, wup_ref[j])
        uv_ref[slot] = _dot(lhs, wup_ref[n_col_tiles + j])

    def act(j, slot):
        gate = conv(ug_ref[slot], cw_ref[j], cb_ref[j])
        val = conv(uv_ref[slot], cw_ref[n_col_tiles + j], cb_ref[n_col_tiles + j])
        return ((gate * jax.nn.sigmoid(gate)) * val).astype(BF16)

    def stage(j, slot, next_up):
        a = act(j, slot)
        if next_up:
            up(j + 1, 1 - slot)
        return _dot(a, wd_ref[j])

    def pair(i, _):
        acc_ref[...] += stage(2 * i, 0, True)
        acc_ref[...] += stage(2 * i + 1, 1, True)
        return 0

    def finish(tail):
        g2 = mod_ref[0][:, 5 * d:6 * d]
        y_ref[...] = x1_ref[...] + g2 * (acc_ref[...] + tail)

    up(0, 0)
    if side_work:
        n_points = n_col_tiles + 1

        def hosted(point):
            for i, work in enumerate(side_work):
                if (i * n_points) // len(side_work) == point:
                    work()

        hosted(0)
        for j in range(n_col_tiles - 1):
            acc_ref[...] += stage(j, j % 2, True)
            hosted(j + 1)
        finish(stage(n_col_tiles - 1, (n_col_tiles - 1) % 2, False))
        hosted(n_col_tiles)
        return
    n_pairs = (n_col_tiles - 1) // 2
    if n_pairs:
        lax.fori_loop(0, n_pairs, pair, 0)
    j0 = 2 * n_pairs
    if n_col_tiles - j0 == 2:
        acc_ref[...] += stage(j0, 0, True)
        finish(stage(j0 + 1, 1, False))
    else:
        finish(stage(j0, 0, False))


def _mix_ffn_parts(o_att, o_ml, x, mod, mod_row, seq_len, wts):
    n_tok, d = x.shape
    tm = _token_tile(n_tok, seq_len)
    hpt = tm // HALO
    n_halo = n_tok // HALO
    nct, tn, _ = wts["w_down"].shape
    mix = o_att.shape[1] + o_ml.shape[1]

    def tile_and_halos(width):
        return [pl.BlockSpec((tm, width), lambda i: (i, 0)),
                pl.BlockSpec((HALO, width), lambda i: (jnp.maximum(i * hpt - 1, 0), 0)),
                pl.BlockSpec((HALO, width), lambda i: (jnp.minimum((i + 1) * hpt, n_halo - 1), 0))]

    in_specs = (tile_and_halos(o_att.shape[1]) + tile_and_halos(o_ml.shape[1]) + tile_and_halos(d) + [
        pl.BlockSpec((1, 1, mod.shape[-1]), lambda i: (mod_row(i, tm), 0, 0)),
        _const_spec((1, d)), _const_spec((mix, d)),
        _const_spec((2 * nct, d, tn)), _const_spec((2 * nct, 3, tn)), _const_spec((2 * nct, 1, tn)),
        _const_spec((nct, tn, d))])
    args = [o_att, o_att, o_att, o_ml, o_ml, o_ml, x, x, x, mod, wts["norm2_g"], wts["w_out"],
            wts["w_up"], wts["conv_w"], wts["conv_b"], wts["w_down"]]
    scratch = [pltpu.VMEM((tm + HALO, mix), BF16), pltpu.VMEM((tm + HALO, d), BF16), pltpu.VMEM((tm, d), F32),
               pltpu.VMEM((2, tm + HALO, tn), F32), pltpu.VMEM((2, tm + HALO, tn), F32), pltpu.VMEM((tm, d), F32)]
    return (n_tok // tm, nct, in_specs, args, pl.BlockSpec((tm, d), lambda i: (i, 0)),
            jax.ShapeDtypeStruct((n_tok, d), F32), scratch)


def _mix_ffn(o_att, o_ml, x, mod, mod_row, seq_len, wts):
    n_tiles, nct, in_specs, args, out_spec, out_shape, scratch = _mix_ffn_parts(
        o_att, o_ml, x, mod, mod_row, seq_len, wts)
    return pl.pallas_call(
        functools.partial(_ffn_kernel, seq_len=seq_len, n_col_tiles=nct),
        grid=(n_tiles,),
        in_specs=in_specs,
        out_specs=out_spec,
        out_shape=out_shape,
        scratch_shapes=scratch,
        compiler_params=_params(1),
        name="mix_ffn",
    )(*args)


def _ffn_nbr_kernel(*refs, n_in, seq_len, n_col_tiles, rows, kr, rows_per_tile, tiles_per_batch):
    ffn_in = refs[:n_in]
    q_ref, k_ref, vt_ref, kc_ref, vct_ref, bias_ref = refs[n_in:n_in + 6]
    y_ref, o_ref = refs[n_in + 6:n_in + 8]
    scratch = refs[n_in + 8:]
    ffn_scratch, pt_refs = scratch[:-2], scratch[-2:]
    base = (pl.program_id(0) % tiles_per_batch) * rows_per_tile

    def row(local, pt_ref):
        rr = pl.ds(pl.multiple_of(local * GRID_W, GRID_W), GRID_W)
        _nbr_row(base + local, q_ref.at[rr, :], k_ref, vt_ref, kc_ref, vct_ref, bias_ref, o_ref.at[rr, :], pt_ref,
                 rows=rows, kr=kr)

    side_work = [functools.partial(row, local, pt_refs[local % 2]) for local in range(rows_per_tile)]
    _ffn_kernel(*ffn_in, y_ref, *ffn_scratch, seq_len=seq_len, n_col_tiles=n_col_tiles, side_work=side_work)


def _mix_ffn_with_nbr_attention(o_att, o_ml, x, mod, mod_row, seq_len, wts, q, k, vt, k_ctx, vt_ctx, rpb,
                                n_batch, nbr_seq):
    n_tiles, nct, in_specs, args, out_spec, out_shape, scratch = _mix_ffn_parts(
        o_att, o_ml, x, mod, mod_row, seq_len, wts)
    rows = nbr_seq // GRID_W
    kr = min(WIN_ROWS, rows)
    n_win = kr + 2
    assert rows >= n_win and (rows - n_win) % 2 == 0
    rpt = (n_batch * rows) // n_tiles
    tpb = n_tiles // n_batch
    bias = _nbr_bias_table(rpb)
    past = k_ctx.shape[1]
    gw = HEAD_GROUP * ATT_HEAD_DIM
    n_groups = N_ATT_HEADS // HEAD_GROUP
    row_blk = pl.BlockSpec((rpt * GRID_W, ATT_WIDTH), lambda i: (i, 0))
    once = pl.Buffered(1)
    nbr_specs = [row_blk,
                 pl.BlockSpec((nbr_seq, ATT_WIDTH), lambda i: (i // tpb, 0), pipeline_mode=once),
                 pl.BlockSpec((ATT_WIDTH, nbr_seq), lambda i: (0, i // tpb), pipeline_mode=once),
                 pl.BlockSpec((1, past, ATT_WIDTH), lambda i: (i // tpb, 0, 0), pipeline_mode=once),
                 pl.BlockSpec((1, ATT_WIDTH, past), lambda i: (i // tpb, 0, 0), pipeline_mode=once),
                 _const_spec(bias.shape)]
    pt = pltpu.VMEM((n_groups, n_win * GRID_W, gw), BF16)
    y, o_lat = pl.pallas_call(
        functools.partial(_ffn_nbr_kernel, n_in=len(in_specs), seq_len=seq_len, n_col_tiles=nct, rows=rows, kr=kr,
                          rows_per_tile=rpt, tiles_per_batch=tpb),
        grid=(n_tiles,),
        in_specs=in_specs + nbr_specs,
        out_specs=[out_spec, row_blk],
        out_shape=[out_shape, jax.ShapeDtypeStruct((n_batch * nbr_seq, ATT_WIDTH), BF16)],
        scratch_shapes=scratch + [pt, pt],
        compiler_params=_params(1),
        name="mix_ffn_ctx_nbr_attn",
    )(*args, q, k, vt, k_ctx, vt_ctx, bias)
    return y, o_lat


def _retile_cast_kernel(w_ref, o_ref):
    tn = o_ref.shape[-1]
    for t in range(o_ref.shape[0]):
        o_ref[t] = w_ref[:, t * tn:(t + 1) * tn].astype(o_ref.dtype)


def _retile_cast(w, tn):
    r, c = w.shape
    group = 2 if (c // tn) % 2 == 0 else 1
    return pl.pallas_call(
        _retile_cast_kernel,
        grid=(c // (tn * group),),
        in_specs=[pl.BlockSpec((r, tn * group), lambda j: (0, j))],
        out_specs=pl.BlockSpec((group, r, tn), lambda j: (j, 0, 0)),
        out_shape=jax.ShapeDtypeStruct((c // tn, r, tn), BF16),
        compiler_params=_params(1),
        name="retile_cast",
    )(w)


def _cast_kernel(w_ref, o_ref):
    o_ref[...] = w_ref[...].astype(o_ref.dtype)


def _cast_rows(w, block_rows):
    r, c = w.shape
    spec = pl.BlockSpec((block_rows, c), lambda i: (i, 0))
    return pl.pallas_call(
        _cast_kernel,
        grid=(r // block_rows,),
        in_specs=[spec],
        out_specs=spec,
        out_shape=jax.ShapeDtypeStruct((r, c), BF16),
        compiler_params=_params(1),
        name="cast_rows",
    )(w)


def _layer_weights(norm1_g, norm2_g, w_in, b_gate, q_g, k_g, ml_g, w_out, w_up, conv_w, conv_b, w_down):
    d = w_in.shape[0]
    a, m, nh = ATT_WIDTH, ML_WIDTH, N_ML_HEADS
    def gate_lanes(g):
        g = g.reshape(g.shape[0], 2, 2, nh).transpose(0, 2, 3, 1).reshape(g.shape[0], 4 * nh)
        return jnp.pad(g, ((0, 0), (0, LANES - 4 * nh)))

    w_g = gate_lanes(w_in[:, 3 * a + 4 * m:])
    b_g = gate_lanes(b_gate.astype(F32).reshape(1, 4 * nh))
    head_id = jnp.arange(a) // ATT_HEAD_DIM
    d_ff = w_down.shape[0]
    tn = MXU_WIDTH
    nct = d_ff // tn

    def col_tiles(w):
        return w.reshape(w.shape[0], 2 * nct, tn).transpose(1, 0, 2)

    return {
        "norm1_g": norm1_g.reshape(1, d), "norm2_g": norm2_g.reshape(1, d),
        "w_in": w_in.astype(BF16),
        "w_g": w_g.astype(BF16), "b_g": b_g,
        "q_g": jnp.tile(q_g, N_ATT_HEADS).reshape(1, a), "k_g": jnp.tile(k_g, N_ATT_HEADS).reshape(1, a),
        "blockdiag": (head_id[:, None] == head_id[None, :]).astype(BF16),
        "ml_g": ml_g.reshape(1, m),
        "w_out": w_out.astype(BF16),
        "conv_w": col_tiles(conv_w.astype(F32)), "conv_b": col_tiles(conv_b.astype(F32).reshape(1, -1)),
        "w_up_f32": w_up, "w_down_f32": w_down, "ffn_tile": tn,
    }


def _rope_tables(seq_len):
    quarter = ML_HEAD_DIM // 4
    pos = np.arange(seq_len)
    inv_freq = ROPE_THETA ** (-np.arange(quarter, dtype=np.float64) / quarter)
    ang_r = (pos // GRID_W).astype(np.float64)[:, None] * inv_freq[None, :]
    ang_c = (pos % GRID_W).astype(np.float64)[:, None] * inv_freq[None, :]
    cos = np.concatenate([np.cos(ang_r)] * 2 + [np.cos(ang_c)] * 2, axis=-1)
    sin = np.concatenate([-np.sin(ang_r), np.sin(ang_r), -np.sin(ang_c), np.sin(ang_c)], axis=-1)
    return jnp.asarray(cos, F32), jnp.asarray(sin, F32)


def _layer(xp, xs, mod, wts, rope, ctx_kv, rpb, state, batch, seq, dec_batch, dec_seq):
    def ctx_row(i, tm):
        return 0 * i

    def lat_row(i, tm):
        return 1 + (i * tm) // dec_seq

    w_up, w_down, tn = wts["w_up_f32"], wts["w_down_f32"], wts["ffn_tile"]
    n_ctx_tiles = (batch * seq) // _token_tile(batch * seq, seq)
    if n_ctx_tiles >= max(w_up.shape[1] // (2 * tn), w_down.shape[0] // tn) and (w_up.shape[1] // tn) % 2 == 0:
        ctx, w_up_t, w_down_t = _ctx_in_projection_with_weight_casts(xp, mod, ctx_row, seq, wts, w_up, w_down, tn)
    else:
        ctx = _in_projection(xp, mod, ctx_row, seq, wts, None, latent=False)
        w_up_t = _retile_cast(w_up, tn)
        w_down_t = _cast_rows(w_down, tn).reshape(w_down.shape[0] // tn, tn, w_down.shape[1])
    wts = dict(wts, w_up=w_up_t, w_down=w_down_t)
    cache_k, cache_v = ctx[8], ctx[9]
    n_lat_tiles = (dec_batch * dec_seq) // _token_tile(dec_batch * dec_seq, dec_seq)
    if batch % n_lat_tiles == 0:
        lat, (o_att_c, o_ml_c, c_f, n_f, m_f) = _latent_in_projection_with_ctx_mixers(
            xs, mod, lat_row, dec_seq, wts, rope, ctx[:8], batch, seq)
    else:
        aq, ak, av, mq, mk, mvt, mot, gtt = ctx[:8]
        o_att_c = _context_attention(aq, ak, av, seq)
        o_ml_c, c_f, n_f, m_f = _mlstm(mq, mk, mvt, mot, gtt, wts["ml_g"], None, batch, seq, min(ML_CHUNK, seq))
        lat = _in_projection(xs, mod, lat_row, dec_seq, wts, rope, latent=True)
    aq, ak, avt, mq, mk, mvt, mot, gtt = lat[:8]
    n_rows = dec_batch * (dec_seq // GRID_W)
    if n_rows % n_ctx_tiles == 0 and n_ctx_tiles % dec_batch == 0:
        xp, o_att = _mix_ffn_with_nbr_attention(o_att_c, o_ml_c, xp, mod, ctx_row, seq, wts, aq, ak, avt,
                                                ctx_kv[0], ctx_kv[1], rpb, dec_batch, dec_seq)
    else:
        xp = _mix_ffn(o_att_c, o_ml_c, xp, mod, ctx_row, seq, wts)
        o_att = _neighborhood_attention(aq, ak, avt, ctx_kv[0], ctx_kv[1], rpb, dec_batch, dec_seq)
    o_ml = _mlstm(mq, mk, mvt, mot, gtt, wts["ml_g"], state, dec_batch, dec_seq, min(ML_CHUNK, dec_seq))[0]
    xs = _mix_ffn(o_att, o_ml, xs, mod, lat_row, dec_seq, wts)
    return xp, xs, cache_k, cache_v, (c_f, n_f[:, :, :, 0, :], _unpack_gate_rows(m_f))


def kernel(x_prompt, x_sample, cache_k, cache_v, state_C, state_n, state_m, c, c_ctx, w_mod, b_mod, norm1_g,
           norm2_g, w_in, b_gate, q_norm_g, k_norm_g, rpb, ml_norm_g, w_out, w_up, conv_w, conv_b, w_down):
    batch, seq, d = x_prompt.shape
    dec_batch, dec_seq, _ = x_sample.shape
    depth = w_mod.shape[0]
    past = cache_k.shape[2]
    cvecs = jnp.concatenate([c_ctx[None, :], c], axis=0).astype(F32)
    rope = _rope_tables(dec_seq)

    xp = x_prompt.reshape(batch * seq, d)
    xs = x_sample.reshape(dec_batch * dec_seq, d)
    ks, vs, cs, ns, ms = [], [], [], [], []
    for l in range(depth):
        wts = _layer_weights(norm1_g[l], norm2_g[l], w_in[l], b_gate[l], q_norm_g[l], k_norm_g[l],
                             ml_norm_g[l], w_out[l], w_up[l], conv_w[l], conv_b[l], w_down[l])
        mod = _modulation(cvecs, w_mod[l], b_mod[l])

        state = (state_C[:, l].astype(F32),
                 state_n[:, l].astype(F32)[:, :, :, None, :],
                 _pack_gate_rows(state_m[:, l].astype(F32)))
        ctx_kv = (cache_k[:, l].reshape(dec_batch, past, ATT_WIDTH).astype(BF16),
                  cache_v[:, l].reshape(dec_batch, past, ATT_WIDTH).transpose(0, 2, 1).astype(BF16))
        xp, xs, ak, av, st = _layer(xp, xs, mod, wts, rope, ctx_kv, rpb[l], state, batch, seq, dec_batch, dec_seq)
        ks.append(ak.reshape(batch, seq, N_ATT_HEADS, ATT_HEAD_DIM))
        vs.append(av.reshape(batch, seq, N_ATT_HEADS, ATT_HEAD_DIM))
        cs.append(st[0])
        ns.append(st[1])
        ms.append(st[2])
    return (xp.reshape(batch, seq, d), xs.reshape(dec_batch, dec_seq, d),
            jnp.stack(ks, axis=1), jnp.stack(vs, axis=1),
            jnp.stack(cs, axis=1), jnp.stack(ns, axis=1), jnp.stack(ms, axis=1))
```

```python
import functools

import jax
import jax.numpy as jnp
import numpy as np
from jax import lax
from jax.experimental import pallas as pl
from jax.experimental.pallas import tpu as pltpu

F32 = jnp.float32
BF16 = jnp.bfloat16

GRID_W = 64
N_ATT_HEADS = 8
ATT_HEAD_DIM = 64
ATT_WIDTH = N_ATT_HEADS * ATT_HEAD_DIM
WIN_ROWS = 8
WIN_COLS = 16
HEAD_GROUP = 4
N_ML_HEADS = 4
ML_HEAD_DIM = 128
ML_WIDTH = N_ML_HEADS * ML_HEAD_DIM
ROPE_THETA = 10000.0
EPS = 1e-6

LANES = 128
MXU_WIDTH = 256
ML_CHUNK = 256
MLSTM_UNROLL = 8
GATE_ROWS = 8
HALO = 16
TOKEN_TILE = 512
VMEM_LIMIT = 56 * 1024 * 1024


def _dot(a, b):
    return jnp.dot(a, b, preferred_element_type=F32)


def _dot_nt(a, b):
    return lax.dot_general(a, b, (((1,), (1,)), ((), ())), preferred_element_type=F32)


def _dot_tn(a, b):
    return lax.dot_general(a, b, (((0,), (0,)), ((), ())), preferred_element_type=F32)


def _const_spec(shape):
    nd = len(shape)
    return pl.BlockSpec(shape, lambda *_: (0,) * nd, pipeline_mode=pl.Buffered(1))


def _params(n_axes):
    return pltpu.CompilerParams(dimension_semantics=("arbitrary",) * n_axes,
                                vmem_limit_bytes=VMEM_LIMIT)


def _token_tile(n_tok, seq_len):
    tm = min(TOKEN_TILE, n_tok)
    while n_tok % tm or (seq_len % tm and tm % seq_len):
        tm //= 2
    return tm


def _log_sigmoid(x):
    return jnp.minimum(x, 0.0) - jnp.log1p(jnp.exp(-jnp.abs(x)))


def _mod_kernel(c_ref, w_ref, b_ref, o_ref):
    c = c_ref[...]
    s = c * jax.nn.sigmoid(c)
    o_ref[...] = _dot(s.astype(BF16), w_ref[...].astype(BF16)) + b_ref[...]


def _modulation(cvecs, w_mod, b_mod):
    r, d = cvecs.shape
    n = w_mod.shape[1]
    tn = d
    out = pl.pallas_call(
        _mod_kernel,
        grid=(n // tn,),
        in_specs=[pl.BlockSpec((r, d), lambda j: (0, 0)),
                  pl.BlockSpec((d, tn), lambda j: (0, j)),
                  pl.BlockSpec((1, tn), lambda j: (0, j))],
        out_specs=pl.BlockSpec((r, tn), lambda j: (0, j)),
        out_shape=jax.ShapeDtypeStruct((r, n), F32),
        compiler_params=_params(1),
        name="adaln_mod",
    )(cvecs, w_mod, b_mod.reshape(1, n))
    return out.reshape(r, 1, n)


def _rope(x, cos, sin_signed):
    lane = lax.broadcasted_iota(jnp.int32, x.shape, 1)
    partner = jnp.where((lane & 32) == 0, pltpu.roll(x, LANES - 32, 1), pltpu.roll(x, 32, 1))
    return x * cos + partner * sin_signed


def _inproj_kernel(*refs, latent):
    if latent:
        (x_ref, mod_ref, g1_ref, win_ref, wg_ref, bg_ref, qg_ref, kg_ref, bd_ref, cos_ref, sin_ref,
         aq_ref, ak_ref, av_ref, mq_ref, mk_ref, mv_ref, mo_ref, gtt_ref) = refs
    else:
        (x_ref, mod_ref, g1_ref, win_ref, wg_ref, bg_ref, qg_ref, kg_ref, bd_ref,
         aq_ref, ak_ref, av_ref, mq_ref, mk_ref, mv_ref, mo_ref, gtt_ref, ck_ref, cv_ref,
         cks_ref, cvs_ref) = refs
    x = x_ref[...]
    d = x.shape[-1]
    mod = mod_ref[0]
    sh1, sc1 = mod[:, 0:d], mod[:, d:2 * d]
    y = x * lax.rsqrt(jnp.mean(x * x, axis=-1, keepdims=True) + EPS)
    h = (y * g1_ref[...]) * (1.0 + sc1) + sh1
    hb = h.astype(BF16)

    def head_norm(a, g):
        ss = _dot((a * a).astype(BF16), bd_ref[...])
        return a * lax.rsqrt(ss * (1.0 / ATT_HEAD_DIM) + EPS) * g

    w = ATT_WIDTH
    att = _dot(hb, win_ref[:, 0:3 * w])
    aq_ref[...] = (head_norm(att[:, 0:w], qg_ref[...]) * ATT_HEAD_DIM ** -0.5).astype(aq_ref.dtype)
    kn = head_norm(att[:, w:2 * w], kg_ref[...])
    av = att[:, 2 * w:3 * w]
    ak_ref[...] = kn.astype(BF16)
    if latent:
        av_ref[...] = av.T.astype(BF16)
    else:
        av_ref[...] = av.astype(BF16)
        tm = kn.shape[0]
        for src, dst, scr in ((kn, ck_ref, cks_ref), (av, cv_ref, cvs_ref)):
            for hh in range(N_ATT_HEADS):
                pair = src[:, (hh // 2) * LANES:(hh // 2 + 1) * LANES]
                if hh % 2:
                    pair = pltpu.roll(pair, ATT_HEAD_DIM, 1)
                scr[pl.ds(hh, tm, stride=N_ATT_HEADS), :] = pair
            dst[...] = scr[...].reshape(tm, N_ATT_HEADS, LANES)[:, :, 0:ATT_HEAD_DIM]

    w = ML_WIDTH
    ml0 = 3 * ATT_WIDTH
    mq = _dot(hb, win_ref[:, ml0:ml0 + w])
    mk = _dot(hb, win_ref[:, ml0 + w:ml0 + 2 * w]) * ML_HEAD_DIM ** -0.5
    if latent:
        cos, sin = cos_ref[...], sin_ref[...]
        for hh in range(N_ML_HEADS):
            sl = slice(hh * ML_HEAD_DIM, (hh + 1) * ML_HEAD_DIM)
            mq_ref[:, sl] = _rope(mq[:, sl], cos, sin).astype(BF16)
            mk_ref[:, sl] = _rope(mk[:, sl], cos, sin).astype(BF16)
    else:
        mq_ref[...] = mq.astype(BF16)
        mk_ref[...] = mk.astype(BF16)
    mv_ref[...] = _dot(hb, win_ref[:, ml0 + 2 * w:ml0 + 3 * w]).T.astype(BF16)
    mo_ref[...] = _dot(hb, win_ref[:, ml0 + 3 * w:ml0 + 4 * w]).T
    gates = _dot(hb, wg_ref[...]) + bg_ref[...]
    gtt_ref[...] = gates.T[0:2 * GATE_ROWS]


def _in_projection_parts(x, mod, mod_row, seq_len, wts, rope, *, latent):
    n_tok, d = x.shape
    tm = _token_tile(n_tok, seq_len)
    tiles_per_seq = max(seq_len // tm, 1)

    def tok(width):
        return pl.BlockSpec((tm, width), lambda i: (i, 0))

    in_specs = [tok(d),
                pl.BlockSpec((1, 1, mod.shape[-1]), lambda i: (mod_row(i, tm), 0, 0)),
                _const_spec((1, d)),
                _const_spec(wts["w_in"].shape),
                _const_spec(wts["w_g"].shape), _const_spec((1, LANES)),
                _const_spec((1, ATT_WIDTH)), _const_spec((1, ATT_WIDTH)),
                _const_spec((ATT_WIDTH, ATT_WIDTH))]
    args = [x, mod, wts["norm1_g"], wts["w_in"], wts["w_g"], wts["b_g"],
            wts["q_g"], wts["k_g"], wts["blockdiag"]]
    if latent:
        in_specs += [pl.BlockSpec((tm, LANES), lambda i: (i % tiles_per_seq, 0))] * 2
        args += [rope[0], rope[1]]
    out_shape = [jax.ShapeDtypeStruct((n_tok, ATT_WIDTH), BF16),
                 jax.ShapeDtypeStruct((n_tok, ATT_WIDTH), BF16),
                 jax.ShapeDtypeStruct((ATT_WIDTH, n_tok) if latent else (n_tok, ATT_WIDTH), BF16),
                 jax.ShapeDtypeStruct((n_tok, ML_WIDTH), BF16),
                 jax.ShapeDtypeStruct((n_tok, ML_WIDTH), BF16),
                 jax.ShapeDtypeStruct((ML_WIDTH, n_tok), BF16),
                 jax.ShapeDtypeStruct((ML_WIDTH, n_tok), F32),
                 jax.ShapeDtypeStruct((2 * GATE_ROWS, n_tok), F32)]

    def tok_t(height):
        return pl.BlockSpec((height, tm), lambda i: (0, i))

    av_spec = tok_t(ATT_WIDTH) if latent else tok(ATT_WIDTH)
    out_specs = ([tok(ATT_WIDTH)] * 2 + [av_spec] + [tok(ML_WIDTH)] * 2 + [tok_t(ML_WIDTH)] * 2
                 + [tok_t(2 * GATE_ROWS)])
    if not latent:
        cache = jax.ShapeDtypeStruct((n_tok, N_ATT_HEADS, ATT_HEAD_DIM), F32)
        out_shape += [cache, cache]
        out_specs += [pl.BlockSpec((tm, N_ATT_HEADS, ATT_HEAD_DIM), lambda i: (i, 0, 0))] * 2
    scratch = [] if latent else [pltpu.VMEM((tm * N_ATT_HEADS, LANES), F32)] * 2
    return n_tok // tm, in_specs, args, out_shape, out_specs, scratch


def _in_projection(x, mod, mod_row, seq_len, wts, rope, *, latent):
    n_tiles, in_specs, args, out_shape, out_specs, scratch = _in_projection_parts(
        x, mod, mod_row, seq_len, wts, rope, latent=latent)
    return pl.pallas_call(
        functools.partial(_inproj_kernel, latent=latent),
        grid=(n_tiles,),
        in_specs=in_specs,
        out_specs=out_specs,
        out_shape=out_shape,
        scratch_shapes=scratch,
        compiler_params=_params(1),
        name="in_proj_latent" if latent else "in_proj_ctx",
    )(*args)


def _inproj_cast_kernel(*refs, n_in, n_out):
    ip_in, (wup_ref, wdn_ref) = refs[:n_in], refs[n_in:n_in + 2]
    outs = refs[n_in + 2:]
    ip_out, (wup_o, wdn_o), scratch = outs[:n_out], outs[n_out:n_out + 2], outs[n_out + 2:]
    _inproj_kernel(*ip_in, *ip_out, *scratch, latent=False)
    _retile_cast_kernel(wup_ref, wup_o)
    wdn_o[0] = wdn_ref[...].astype(wdn_o.dtype)


def _ctx_in_projection_with_weight_casts(x, mod, mod_row, seq_len, wts, w_up, w_down, tn):
    n_tiles, in_specs, args, out_shape, out_specs, scratch = _in_projection_parts(
        x, mod, mod_row, seq_len, wts, None, latent=False)
    d, two_dff = w_up.shape
    nct = w_down.shape[0] // tn
    n_grp = two_dff // (2 * tn)
    assert two_dff % (2 * tn) == 0 and max(n_grp, nct) <= n_tiles
    cast_in = [pl.BlockSpec((d, 2 * tn), lambda i: (0, jnp.minimum(i, n_grp - 1))),
               pl.BlockSpec((tn, d), lambda i: (jnp.minimum(i, nct - 1), 0))]
    cast_out = [pl.BlockSpec((2, d, tn), lambda i: (jnp.minimum(i, n_grp - 1), 0, 0)),
                pl.BlockSpec((1, tn, d), lambda i: (jnp.minimum(i, nct - 1), 0, 0))]
    cast_shape = [jax.ShapeDtypeStruct((two_dff // tn, d, tn), BF16), jax.ShapeDtypeStruct((nct, tn, d), BF16)]
    outs = pl.pallas_call(
        functools.partial(_inproj_cast_kernel, n_in=len(in_specs), n_out=len(out_specs)),
        grid=(n_tiles,),
        in_specs=in_specs + cast_in,
        out_specs=out_specs + cast_out,
        out_shape=out_shape + cast_shape,
        scratch_shapes=scratch,
        compiler_params=_params(1),
        name="in_proj_ctx_weight_casts",
    )(*args, w_up, w_down)
    return outs[:len(out_specs)], outs[len(out_specs)], outs[len(out_specs) + 1]


def _ctx_attn_kernel(q_ref, k_ref, v_ref, o_ref):
    n = q_ref.shape[0]
    gw = HEAD_GROUP * ATT_HEAD_DIM
    lane_head = lax.broadcasted_iota(jnp.int32, (n, gw), 1) // ATT_HEAD_DIM
    for g in range(N_ATT_HEADS // HEAD_GROUP):
        sl = slice(g * gw, (g + 1) * gw)
        q4 = q_ref[:, sl]
        qbd = jnp.concatenate([jnp.where(lane_head == hl, q4, jnp.zeros_like(q4))
                               for hl in range(HEAD_GROUP)], axis=0)
        s = _dot_nt(k_ref[:, sl], qbd)
        p = jnp.exp(s - jnp.max(s, axis=0, keepdims=True))
        p = p / jnp.sum(p, axis=0, keepdims=True)
        o4 = _dot_tn(p.astype(BF16), v_ref[:, sl])
        out = jnp.where(lane_head == 0, o4[0:n], 0.0)
        for hl in range(1, HEAD_GROUP):
            out = out + jnp.where(lane_head == hl, o4[hl * n:(hl + 1) * n], 0.0)
        o_ref[:, sl] = out.astype(o_ref.dtype)


def _context_attention(q, k, v, seq_len):
    n_tok = q.shape[0]
    spec = pl.BlockSpec((seq_len, ATT_WIDTH), lambda b: (b, 0))
    return pl.pallas_call(
        _ctx_attn_kernel,
        grid=(n_tok // seq_len,),
        in_specs=[spec, spec, spec],
        out_specs=spec,
        out_shape=jax.ShapeDtypeStruct((n_tok, ATT_WIDTH), BF16),
        compiler_params=_params(1),
        name="ctx_attn",
    )(q, k, v)


def _nbr_attn_kernel(q_ref, k_ref, vt_ref, kc_ref, vct_ref, bias_ref, o_ref, pt_ref, *, rows, kr):
    _nbr_row(pl.program_id(1), q_ref, k_ref, vt_ref, kc_ref, vct_ref, bias_ref, o_ref, pt_ref, rows=rows, kr=kr)


def _nbr_row(r, q_ref, k_ref, vt_ref, kc_ref, vct_ref, bias_ref, o_ref, pt_ref, *, rows, kr):
    n_win = kr + 2
    n_loc = kr * GRID_W
    gw = HEAD_GROUP * ATT_HEAD_DIM
    groups = N_ATT_HEADS // HEAD_GROUP
    lane_head = lax.broadcasted_iota(jnp.int32, (GRID_W, gw), 1) // ATT_HEAD_DIM
    rs = jnp.clip(r - kr // 2, 0, rows - kr)
    start = jnp.minimum(rs - (rs & 1), rows - n_win)
    delta = rs - start
    bias0 = pl.multiple_of((WIN_ROWS - 1 - (r - rs)) * GRID_W, GRID_W)
    zeros2 = jnp.zeros((2 * GRID_W, gw), BF16)
    for g in range(groups):
        sl = slice(g * gw, (g + 1) * gw)
        q4 = q_ref[:, sl]
        qbd = jnp.concatenate([jnp.where(lane_head == hl, q4, jnp.zeros_like(q4))
                               for hl in range(HEAD_GROUP)], axis=0)
        s_loc = _dot_nt(k_ref[pl.ds(pl.multiple_of(rs * GRID_W, GRID_W), n_loc), sl], qbd)
        s_loc = s_loc + bias_ref[g, pl.ds(bias0, n_loc), :]
        s_ctx = _dot_nt(kc_ref[0, :, sl], qbd)
        m = jnp.maximum(jnp.max(s_loc, axis=0, keepdims=True), jnp.max(s_ctx, axis=0, keepdims=True))
        p_loc = jnp.exp(s_loc - m)
        p_ctx = jnp.exp(s_ctx - m)
        l = jnp.sum(p_loc, axis=0, keepdims=True) + jnp.sum(p_ctx, axis=0, keepdims=True)
        pt_ref[g, 0:2 * GRID_W, :] = zeros2
        pt_ref[g, n_loc:n_loc + 2 * GRID_W, :] = zeros2
        pt_ref[g, pl.ds(pl.multiple_of(delta * GRID_W, GRID_W), n_loc), :] = p_loc.astype(BF16)
        vt_win = vt_ref[sl, pl.ds(pl.multiple_of(start * GRID_W, 2 * GRID_W), n_win * GRID_W)]
        ot = _dot(vt_win, pt_ref[g]) + _dot(vct_ref[0, sl, :], p_ctx.astype(BF16))
        o4 = (ot / l).T
        out = jnp.where(lane_head == 0, o4[0:GRID_W], 0.0)
        for hl in range(1, HEAD_GROUP):
            out = out + jnp.where(lane_head == hl, o4[hl * GRID_W:(hl + 1) * GRID_W], 0.0)
        o_ref[:, sl] = out.astype(o_ref.dtype)


def _nbr_bias_table(rpb):
    n_heads, n_dr, n_dc = rpb.shape
    c = WIN_COLS - 1
    assert n_heads == N_ATT_HEADS and n_dc == 2 * c + 1 and 2 * GRID_W == LANES
    f = rpb.astype(F32)
    packed = jnp.concatenate([f[..., c::-1], jnp.zeros((n_heads, n_dr, LANES - n_dc), F32), f[..., :c:-1]], axis=-1)
    n_groups = N_ATT_HEADS // HEAD_GROUP

    def sublane_groups(heads):
        return jnp.repeat(packed[heads], 8, axis=1).reshape(-1, LANES)

    return pl.pallas_call(
        functools.partial(_nbr_bias_kernel, n_dr=n_dr),
        out_shape=jax.ShapeDtypeStruct((n_groups, n_dr * GRID_W, HEAD_GROUP * GRID_W), F32),
        name="nbr_bias_table",
    )(sublane_groups(slice(0, None, 2)), sublane_groups(slice(1, None, 2)))


def _nbr_bias_kernel(even_ref, odd_ref, o_ref, *, n_dr):
    n_rows = even_ref.shape[0]
    s = lax.broadcasted_iota(jnp.int32, (n_rows, LANES), 0) % 8
    lane = lax.broadcasted_iota(jnp.int32, (n_rows, LANES), 1)
    cs = jnp.clip(lane % GRID_W - WIN_COLS // 2, 0, GRID_W - WIN_COLS)

    def sheared(ref):
        x = ref[...]
        for bit in (1, 2, 4):
            x = jnp.where((s & bit) != 0, pltpu.roll(x, bit, 1), x)
        return x

    even, odd = sheared(even_ref), sheared(odd_ref)
    for i in range(GRID_W // 8):
        key = 8 * i + s
        t = jnp.where(lane < GRID_W, pltpu.roll(even, 8 * i, 1) if i else even,
                      pltpu.roll(odd, (8 * i + GRID_W) % LANES, 1))
        t = jnp.where((key >= cs) & (key < cs + WIN_COLS), t, -jnp.inf)
        for p in range(n_rows // (8 * n_dr)):
            g, pair = divmod(p, HEAD_GROUP // 2)
            for dr in range(n_dr):
                r0 = (p * n_dr + dr) * 8
                o_ref[g, dr * GRID_W + 8 * i:dr * GRID_W + 8 * i + 8, pair * LANES:(pair + 1) * LANES] = t[r0:r0 + 8]


def _neighborhood_attention(q, k, vt, k_ctx, vt_ctx, rpb, n_batch, seq_len):
    rows = seq_len // GRID_W
    kr = min(WIN_ROWS, rows)
    n_win = kr + 2
    assert rows >= n_win and (rows - n_win) % 2 == 0
    bias = _nbr_bias_table(rpb)
    past = k_ctx.shape[1]
    gw = HEAD_GROUP * ATT_HEAD_DIM
    n_groups = N_ATT_HEADS // HEAD_GROUP
    row_spec = pl.BlockSpec((GRID_W, ATT_WIDTH), lambda b, r: (b * rows + r, 0))
    return pl.pallas_call(
        functools.partial(_nbr_attn_kernel, rows=rows, kr=kr),
        grid=(n_batch, rows),
        in_specs=[row_spec,
                  pl.BlockSpec((seq_len, ATT_WIDTH), lambda b, r: (b, 0)),
                  pl.BlockSpec((ATT_WIDTH, seq_len), lambda b, r: (0, b)),
                  pl.BlockSpec((1, past, ATT_WIDTH), lambda b, r: (b, 0, 0)),
                  pl.BlockSpec((1, ATT_WIDTH, past), lambda b, r: (b, 0, 0)),
                  _const_spec(bias.shape)],
        out_specs=row_spec,
        out_shape=jax.ShapeDtypeStruct((n_batch * seq_len, ATT_WIDTH), BF16),
        scratch_shapes=[pltpu.VMEM((n_groups, n_win * GRID_W, gw), BF16)],
        compiler_params=_params(2),
        name="nbr_attn",
    )(q, k, vt, k_ctx, vt_ctx, bias)


def _mlstm_kernel(*refs, chunk, n_chunks, has_state):
    dh = ML_HEAD_DIM
    nrep = 2 * GATE_ROWS
    if has_state:
        (q_ref, k_ref, vt_ref, mot_ref, gtt_ref, g_ref, c0_ref, n0_ref, m0_ref,
         o_ref, c_ref, n_ref, m_ref, hf_ref, hb_ref, cn_ref, mrun_ref) = refs
        for d in range(2):
            for hh in range(N_ML_HEADS):
                cn_ref[d, hh, 0:dh, :] = c0_ref[0, d, hh].T
                cn_ref[d, hh, dh:dh + nrep, :] = jnp.broadcast_to(n0_ref[0, d, hh], (nrep, dh))
        mrun_ref[...] = m0_ref[0]
    else:
        (q_ref, k_ref, vt_ref, mot_ref, gtt_ref, g_ref,
         o_ref, c_ref, n_ref, m_ref, hf_ref, hb_ref, cn_ref, mrun_ref) = refs
        cn_ref[...] = jnp.zeros_like(cn_ref)
        mrun_ref[...] = jnp.zeros_like(mrun_ref)
    use_state = has_state or n_chunks > 1

    i0 = lax.broadcasted_iota(jnp.int32, (chunk, chunk), 0)
    i1 = lax.broadcasted_iota(jnp.int32, (chunk, chunk), 1)
    row_id = lax.broadcasted_iota(jnp.int32, (GATE_ROWS, 1), 0)
    instances = [(hh, d) for d in range(2) for hh in range(N_ML_HEADS)]

    le = [i0 <= i1, i0 >= i1]

    def split3(x):
        hi = x.astype(BF16).astype(F32)
        r1 = x - hi
        mid = r1.astype(BF16).astype(F32)
        return jnp.concatenate([hi, mid, r1 - mid], axis=0).astype(BF16)

    def gate_rows(gi, lf_pre):
        n_rows = gi.shape[0]
        lf = _log_sigmoid(lf_pre)
        b_last = jnp.sum(lf, axis=1, keepdims=True)
        terms = split3(lf)
        per_dir = []
        for d in range(2):
            tri = jnp.where(le[d], 1.0, 0.0).astype(BF16)
            parts = _dot(terms, tri)
            cs = parts[0:n_rows] + parts[n_rows:2 * n_rows] + parts[2 * n_rows:3 * n_rows]
            w_end = b_last + gi - cs
            pad = jnp.zeros((LANES - n_rows, chunk), F32)
            per_dir.append((cs, w_end, jnp.max(w_end, axis=1, keepdims=True),
                            jnp.concatenate([gi - cs, pad], axis=0).T))
        return b_last, per_dir

    def chunk_rows(lo):
        return jnp.concatenate([gtt_ref[lo:lo + GATE_ROWS, c * chunk:(c + 1) * chunk] for c in range(n_chunks)],
                               axis=0)

    def body(i, gates):
        chunks = (i, n_chunks - 1 - i)
        rows = [pl.ds(pl.multiple_of(c * chunk, chunk), chunk) for c in chunks]
        m_prev = mrun_ref[...][:, 0:1]

        def qkv(hh, d):
            sl = slice(hh * dh, (hh + 1) * dh)
            return q_ref[rows[d], sl], k_ref[rows[d], sl], vt_ref[sl, rows[d]]

        st = [_dot_nt(qkv(hh, d)[1], qkv(hh, d)[0]) for hh, d in instances]
        if use_state:
            cn = [cn_ref[d, hh] for hh, d in instances]
            qct = [_dot_nt(cn[n].astype(BF16), qkv(hh, d)[0]) for n, (hh, d) in enumerate(instances)]

        if gates is None:
            gates = gate_rows(*[jnp.concatenate([gtt_ref[lo:lo + GATE_ROWS, r] for r in rows], axis=0)
                                for lo in (0, GATE_ROWS)])
            row0 = [0, GATE_ROWS]
        else:
            row0 = [GATE_ROWS * c for c in chunks]
        b_last_all, per_dir = gates
        a_col, cs8, w8, decay8, m_new = [], [], [], [], []
        for d in range(2):
            r = slice(row0[d], row0[d] + GATE_ROWS)
            cs_all, w_end_all, w_max_all, a_col_d = per_dir[d]
            b_last = b_last_all[r]
            m_d = jnp.maximum(b_last + m_prev, w_max_all[r])
            cs8.append(cs_all[r])
            decay8.append(jnp.exp(b_last + m_prev - m_d))
            w8.append(jnp.exp(w_end_all[r] - m_d))
            m_new.append(m_d)
            a_col.append(a_col_d)
        m_next = jnp.where((row_id & 1) == 0, m_new[0], m_new[1])
        mrun_ref[...] = jnp.broadcast_to(m_next, (GATE_ROWS, LANES))

        pt_all, g_all, den_all = [], [], []
        for n, (hh, d) in enumerate(instances):
            j = 2 * hh + d
            lane = row0[d] + j
            a = jnp.where(le[d], a_col[d][:, lane:lane + 1], -jnp.inf)
            g = jnp.maximum(jnp.max(a, axis=0, keepdims=True), m_prev[j:j + 1])
            pt = st[n] * jnp.exp(a - g)
            pt_all.append(pt)
            g_all.append(g)
            den_all.append(jnp.sum(pt, axis=0, keepdims=True))

        for n, (hh, d) in enumerate(instances):
            j = 2 * hh + d
            sl = slice(hh * dh, (hh + 1) * dh)
            _, k, vt = qkv(hh, d)
            g, den = g_all[n], den_all[n]
            num = _dot(vt, pt_all[n].astype(BF16))
            if use_state:
                w_inter = jnp.exp(m_prev[j:j + 1] - g)
                num = num + w_inter * qct[n][0:dh]
                den = den + w_inter * qct[n][dh:dh + 1]
            scale = 1.0 / jnp.maximum(jnp.abs(den), jnp.exp(-(cs8[d][j:j + 1] + g)))
            (hb_ref if d else hf_ref)[sl, rows[d]] = num * scale
            w_row = w8[d][j:j + 1]
            lhs = jnp.concatenate([(vt.astype(F32) * w_row).astype(BF16),
                                   jnp.broadcast_to(w_row, (nrep, chunk)).astype(BF16)], axis=0)
            upd = _dot(lhs, k)
            if use_state:
                upd = decay8[d][j:j + 1] * cn[n] + upd
            cn_ref[d, hh] = upd
        return 0

    def finish(c, _):
        rows = pl.ds(pl.multiple_of(c * chunk, chunk), chunk)
        for hh in range(N_ML_HEADS):
            sl = slice(hh * dh, (hh + 1) * dh)
            hs = hf_ref[sl, rows] + hb_ref[sl, rows]
            y = hs * lax.rsqrt(jnp.mean(hs * hs, axis=0, keepdims=True) + EPS) * g_ref[sl, :]
            y = y * jax.nn.sigmoid(mot_ref[sl, rows])
            o_ref[rows, sl] = y.T.astype(o_ref.dtype)
        return 0

    if n_chunks <= MLSTM_UNROLL:
        gates = gate_rows(chunk_rows(0), chunk_rows(GATE_ROWS))
        for c in range(n_chunks):
            body(c, gates)
        for c in range(n_chunks):
            finish(c, 0)
    else:
        lax.fori_loop(0, n_chunks, lambda i, _: body(i, None), 0)
        lax.fori_loop(0, n_chunks, finish, 0)
    for d in range(2):
        for hh in range(N_ML_HEADS):
            cn = cn_ref[d, hh]
            c_ref[0, d, hh] = cn[0:dh].T
            n_ref[0, d, hh] = cn[dh:dh + 1]
    m_ref[0] = mrun_ref[...]


def _mlstm(mq, mk, mvt, mot, gates_t, ml_g, state, n_batch, seq_len, chunk):
    dh = ML_HEAD_DIM
    nh = N_ML_HEADS
    width = nh * dh
    has_state = state is not None
    seq = pl.BlockSpec((seq_len, width), lambda b: (b, 0))
    seq_t = pl.BlockSpec((width, seq_len), lambda b: (0, b))
    st_c = pl.BlockSpec((1, 2, nh, dh, dh), lambda b: (b, 0, 0, 0, 0))
    st_n = pl.BlockSpec((1, 2, nh, 1, dh), lambda b: (b, 0, 0, 0, 0))
    st_m = pl.BlockSpec((1, GATE_ROWS, LANES), lambda b: (b, 0, 0))
    in_specs = [seq, seq, seq_t, seq_t,
                pl.BlockSpec((2 * GATE_ROWS, seq_len), lambda b: (0, b)),
                _const_spec((width, chunk))]
    args = [mq, mk, mvt, mot, gates_t, jnp.broadcast_to(ml_g.reshape(width, 1), (width, chunk))]
    if has_state:
        in_specs += [st_c, st_n, st_m]
        args += list(state)
    out_shape = [jax.ShapeDtypeStruct((n_batch * seq_len, width), BF16),
                 jax.ShapeDtypeStruct((n_batch, 2, nh, dh, dh), F32),
                 jax.ShapeDtypeStruct((n_batch, 2, nh, 1, dh), F32),
                 jax.ShapeDtypeStruct((n_batch, GATE_ROWS, LANES), F32)]
    return pl.pallas_call(
        functools.partial(_mlstm_kernel, chunk=chunk, n_chunks=seq_len // chunk, has_state=has_state),
        grid=(n_batch,),
        in_specs=in_specs,
        out_specs=[seq, st_c, st_n, st_m],
        out_shape=out_shape,
        scratch_shapes=[pltpu.VMEM((width, seq_len), F32), pltpu.VMEM((width, seq_len), F32),
                        pltpu.VMEM((2, nh, dh + 2 * GATE_ROWS, dh), F32), pltpu.VMEM((GATE_ROWS, LANES), F32)],
        compiler_params=_params(1),
        name="mlstm_latent" if has_state else "mlstm_ctx",
    )(*args)


def _pack_gate_rows(m):
    b = m.shape[0]
    packed = m.transpose(0, 2, 1).reshape(b, 2 * N_ML_HEADS, 1)
    return jnp.broadcast_to(packed, (b, GATE_ROWS, LANES))


def _unpack_gate_rows(m):
    b = m.shape[0]
    return m[:, :, 0].reshape(b, N_ML_HEADS, 2).transpose(0, 2, 1)


def _inproj_ctxmix_kernel(*refs, n_in, n_out, n_seq, seq_len, chunk):
    ip_in = refs[:n_in]
    q_ref, k_ref, v_ref, mq_ref, mk_ref, mvt_ref, mot_ref, gtt_ref, g_ref = refs[n_in:n_in + 9]
    outs = refs[n_in + 9:]
    ip_out = outs[:n_out]
    oatt_ref, oml_ref, c_ref, n_ref, m_ref = outs[n_out:n_out + 5]
    scratch = outs[n_out + 5:]
    _inproj_kernel(*ip_in, *ip_out, latent=True)
    for s in range(n_seq):
        rows = slice(s * seq_len, (s + 1) * seq_len)
        _ctx_attn_kernel(q_ref.at[rows, :], k_ref.at[rows, :], v_ref.at[rows, :], oatt_ref.at[rows, :])
        _mlstm_kernel(mq_ref.at[rows, :], mk_ref.at[rows, :], mvt_ref.at[:, rows], mot_ref.at[:, rows],
                      gtt_ref.at[:, rows], g_ref, oml_ref.at[rows, :], c_ref.at[s:s + 1], n_ref.at[s:s + 1],
                      m_ref.at[s:s + 1], *scratch[4 * s:4 * s + 4],
                      chunk=chunk, n_chunks=seq_len // chunk, has_state=False)


def _latent_in_projection_with_ctx_mixers(x, mod, mod_row, seq_len, wts, rope, ctx, n_ctx_batch, ctx_seq):
    n_tiles, in_specs, args, out_shape, out_specs, scratch = _in_projection_parts(
        x, mod, mod_row, seq_len, wts, rope, latent=True)
    n_seq = n_ctx_batch // n_tiles
    rows = n_seq * ctx_seq
    n_ctx_tok = n_ctx_batch * ctx_seq
    chunk = min(ML_CHUNK, ctx_seq)
    nh, dh = N_ML_HEADS, ML_HEAD_DIM

    def tok(width):
        return pl.BlockSpec((rows, width), lambda i: (i, 0))

    def tok_t(height):
        return pl.BlockSpec((height, rows), lambda i: (0, i))

    def per_seq(*tail):
        return pl.BlockSpec((n_seq,) + tail, lambda i: (i,) + (0,) * len(tail))

    aq, ak, av, mq, mk, mvt, mot, gtt = ctx
    mix_in_specs = ([tok(ATT_WIDTH)] * 3 + [tok(ML_WIDTH)] * 2 + [tok_t(ML_WIDTH)] * 2
                    + [tok_t(2 * GATE_ROWS), _const_spec((ML_WIDTH, chunk))])
    mix_args = [aq, ak, av, mq, mk, mvt, mot, gtt,
                jnp.broadcast_to(wts["ml_g"].reshape(ML_WIDTH, 1), (ML_WIDTH, chunk))]
    mix_out_shape = [jax.ShapeDtypeStruct((n_ctx_tok, ATT_WIDTH), BF16),
                     jax.ShapeDtypeStruct((n_ctx_tok, ML_WIDTH), BF16),
                     jax.ShapeDtypeStruct((n_ctx_batch, 2, nh, dh, dh), F32),
                     jax.ShapeDtypeStruct((n_ctx_batch, 2, nh, 1, dh), F32),
                     jax.ShapeDtypeStruct((n_ctx_batch, GATE_ROWS, LANES), F32)]
    mix_out_specs = [tok(ATT_WIDTH), tok(ML_WIDTH), per_seq(2, nh, dh, dh), per_seq(2, nh, 1, dh),
                     per_seq(GATE_ROWS, LANES)]
    mix_scratch = [pltpu.VMEM((ML_WIDTH, ctx_seq), F32), pltpu.VMEM((ML_WIDTH, ctx_seq), F32),
                   pltpu.VMEM((2, nh, dh + 2 * GATE_ROWS, dh), F32), pltpu.VMEM((GATE_ROWS, LANES), F32)] * n_seq
    outs = pl.pallas_call(
        functools.partial(_inproj_ctxmix_kernel, n_in=len(in_specs), n_out=len(out_specs), n_seq=n_seq,
                          seq_len=ctx_seq, chunk=chunk),
        grid=(n_tiles,),
        in_specs=in_specs + mix_in_specs,
        out_specs=out_specs + mix_out_specs,
        out_shape=out_shape + mix_out_shape,
        scratch_shapes=scratch + mix_scratch,
        compiler_params=_params(1),
        name="in_proj_latent_ctx_mixers",
    )(*args, *mix_args)
    return outs[:len(out_specs)], outs[len(out_specs):]


def _ffn_kernel(oa_ref, oap_ref, oan_ref, om_ref, omp_ref, omn_ref, x_ref, xp_ref, xn_ref, mod_ref, g2_ref,
                wout_ref, wup_ref, cw_ref, cb_ref, wd_ref,
                y_ref, oc_ref, lhs_ref, x1_ref, ug_ref, uv_ref, acc_ref, *, seq_len, n_col_tiles, side_work=None):
    tm, d = x_ref.shape
    aw = oa_ref.shape[1]

    def halo(next_ref, prev_ref):
        row = lax.broadcasted_iota(jnp.int32, next_ref.shape, 0)
        return jnp.where(row < HALO // 2, next_ref[...], prev_ref[...])

    oc_ref[0:tm, 0:aw] = oa_ref[...]
    oc_ref[0:tm, aw:] = om_ref[...]
    oc_ref[tm:tm + HALO, 0:aw] = halo(oan_ref, oap_ref)
    oc_ref[tm:tm + HALO, aw:] = halo(omn_ref, omp_ref)
    mod = mod_ref[0]
    g1 = mod[:, 2 * d:3 * d]
    sh2, sc2 = mod[:, 3 * d:4 * d], mod[:, 4 * d:5 * d]
    out = _dot(oc_ref[...], wout_ref[...])

    def norm2(x1):
        y = x1 * lax.rsqrt(jnp.mean(x1 * x1, axis=-1, keepdims=True) + EPS)
        return ((y * g2_ref[...]) * (1.0 + sc2) + sh2).astype(BF16)

    x1 = x_ref[...] + g1 * out[0:tm]
    x1_ref[...] = x1
    lhs_ref[0:tm, :] = norm2(x1)
    lhs_ref[tm:tm + HALO, :] = norm2(halo(xn_ref, xp_ref) + g1 * out[tm:tm + HALO])
    acc_ref[...] = jnp.zeros_like(acc_ref)
    sub = 8
    period = min(seq_len, tm)
    first_groups = sorted({r // sub for r in range(0, tm, period)})
    last_groups = sorted({(r + period - 1) // sub for r in range(0, tm, period)})
    tile0 = pl.program_id(0) * tm

    def zero_rows(x, groups, target):
        parts, at = [], 0
        for grp in groups:
            lo = grp * sub
            if lo > at:
                parts.append(x[at:lo])
            pos = (tile0 + lo + lax.broadcasted_iota(jnp.int32, (sub, 1), 0)) % seq_len
            parts.append(jnp.where(pos == target, 0.0, x[lo:lo + sub]))
            at = lo + sub
        if at < tm:
            parts.append(x[at:tm])
        return jnp.concatenate(parts, axis=0)

    def conv(u, cw, cb):
        prev = zero_rows(pltpu.roll(u, 1, 0)[0:tm], first_groups, 0)
        nxt = zero_rows(pltpu.roll(u, tm + HALO - 1, 0)[0:tm], last_groups, seq_len - 1)
        return prev * cw[0:1] + u[0:tm] * cw[1:2] + nxt * cw[2:3] + cb

    def up(j, slot):
        lhs = lhs_ref[...]
        ug_ref[slot] = _dot(lhs, wup_ref[j])
        uv_ref[slot] = _dot(lhs, wup_ref[n_col_tiles + j])

    def act(j, slot):
        gate = conv(ug_ref[slot], cw_ref[j], cb_ref[j])
        val = conv(uv_ref[slot], cw_ref[n_col_tiles + j], cb_ref[n_col_tiles + j])
        return ((gate * jax.nn.sigmoid(gate)) * val).astype(BF16)

    def stage(j, slot, next_up):
        a = act(j, slot)
        if next_up:
            up(j + 1, 1 - slot)
        return _dot(a, wd_ref[j])

    def pair(i, _):
        acc_ref[...] += stage(2 * i, 0, True)
        acc_ref[...] += stage(2 * i + 1, 1, True)
        return 0

    def finish(tail):
        g2 = mod_ref[0][:, 5 * d:6 * d]
        y_ref[...] = x1_ref[...] + g2 * (acc_ref[...] + tail)

    up(0, 0)
    if side_work:
        n_points = n_col_tiles + 1

        def hosted(point):
            for i, work in enumerate(side_work):
                if (i * n_points) // len(side_work) == point:
                    work()

        hosted(0)
        for j in range(n_col_tiles - 1):
            acc_ref[...] += stage(j, j % 2, True)
            hosted(j + 1)
        finish(stage(n_col_tiles - 1, (n_col_tiles - 1) % 2, False))
        hosted(n_col_tiles)
        return
    n_pairs = (n_col_tiles - 1) // 2
    if n_pairs:
        lax.fori_loop(0, n_pairs, pair, 0)
    j0 = 2 * n_pairs
    if n_col_tiles - j0 == 2:
        acc_ref[...] += stage(j0, 0, True)
        finish(stage(j0 + 1, 1, False))
    else:
        finish(stage(j0, 0, False))


def _mix_ffn_parts(o_att, o_ml, x, mod, mod_row, seq_len, wts):
    n_tok, d = x.shape
    tm = _token_tile(n_tok, seq_len)
    hpt = tm // HALO
    n_halo = n_tok // HALO
    nct, tn, _ = wts["w_down"].shape
    mix = o_att.shape[1] + o_ml.shape[1]

    def tile_and_halos(width):
        return [pl.BlockSpec((tm, width), lambda i: (i, 0)),
                pl.BlockSpec((HALO, width), lambda i: (jnp.maximum(i * hpt - 1, 0), 0)),
                pl.BlockSpec((HALO, width), lambda i: (jnp.minimum((i + 1) * hpt, n_halo - 1), 0))]

    in_specs = (tile_and_halos(o_att.shape[1]) + tile_and_halos(o_ml.shape[1]) + tile_and_halos(d) + [
        pl.BlockSpec((1, 1, mod.shape[-1]), lambda i: (mod_row(i, tm), 0, 0)),
        _const_spec((1, d)), _const_spec((mix, d)),
        _const_spec((2 * nct, d, tn)), _const_spec((2 * nct, 3, tn)), _const_spec((2 * nct, 1, tn)),
        _const_spec((nct, tn, d))])
    args = [o_att, o_att, o_att, o_ml, o_ml, o_ml, x, x, x, mod, wts["norm2_g"], wts["w_out"],
            wts["w_up"], wts["conv_w"], wts["conv_b"], wts["w_down"]]
    scratch = [pltpu.VMEM((tm + HALO, mix), BF16), pltpu.VMEM((tm + HALO, d), BF16), pltpu.VMEM((tm, d), F32),
               pltpu.VMEM((2, tm + HALO, tn), F32), pltpu.VMEM((2, tm + HALO, tn), F32), pltpu.VMEM((tm, d), F32)]
    return (n_tok // tm, nct, in_specs, args, pl.BlockSpec((tm, d), lambda i: (i, 0)),
            jax.ShapeDtypeStruct((n_tok, d), F32), scratch)


def _mix_ffn(o_att, o_ml, x, mod, mod_row, seq_len, wts):
    n_tiles, nct, in_specs, args, out_spec, out_shape, scratch = _mix_ffn_parts(
        o_att, o_ml, x, mod, mod_row, seq_len, wts)
    return pl.pallas_call(
        functools.partial(_ffn_kernel, seq_len=seq_len, n_col_tiles=nct),
        grid=(n_tiles,),
        in_specs=in_specs,
        out_specs=out_spec,
        out_shape=out_shape,
        scratch_shapes=scratch,
        compiler_params=_params(1),
        name="mix_ffn",
    )(*args)


def _ffn_nbr_kernel(*refs, n_in, seq_len, n_col_tiles, rows, kr, rows_per_tile, tiles_per_batch):
    ffn_in = refs[:n_in]
    q_ref, k_ref, vt_ref, kc_ref, vct_ref, bias_ref = refs[n_in:n_in + 6]
    y_ref, o_ref = refs[n_in + 6:n_in + 8]
    scratch = refs[n_in + 8:]
    ffn_scratch, pt_refs = scratch[:-2], scratch[-2:]
    base = (pl.program_id(0) % tiles_per_batch) * rows_per_tile

    def row(local, pt_ref):
        rr = pl.ds(pl.multiple_of(local * GRID_W, GRID_W), GRID_W)
        _nbr_row(base + local, q_ref.at[rr, :], k_ref, vt_ref, kc_ref, vct_ref, bias_ref, o_ref.at[rr, :], pt_ref,
                 rows=rows, kr=kr)

    side_work = [functools.partial(row, local, pt_refs[local % 2]) for local in range(rows_per_tile)]
    _ffn_kernel(*ffn_in, y_ref, *ffn_scratch, seq_len=seq_len, n_col_tiles=n_col_tiles, side_work=side_work)


def _mix_ffn_with_nbr_attention(o_att, o_ml, x, mod, mod_row, seq_len, wts, q, k, vt, k_ctx, vt_ctx, rpb,
                                n_batch, nbr_seq):
    n_tiles, nct, in_specs, args, out_spec, out_shape, scratch = _mix_ffn_parts(
        o_att, o_ml, x, mod, mod_row, seq_len, wts)
    rows = nbr_seq // GRID_W
    kr = min(WIN_ROWS, rows)
    n_win = kr + 2
    assert rows >= n_win and (rows - n_win) % 2 == 0
    rpt = (n_batch * rows) // n_tiles
    tpb = n_tiles // n_batch
    bias = _nbr_bias_table(rpb)
    past = k_ctx.shape[1]
    gw = HEAD_GROUP * ATT_HEAD_DIM
    n_groups = N_ATT_HEADS // HEAD_GROUP
    row_blk = pl.BlockSpec((rpt * GRID_W, ATT_WIDTH), lambda i: (i, 0))
    once = pl.Buffered(1)
    nbr_specs = [row_blk,
                 pl.BlockSpec((nbr_seq, ATT_WIDTH), lambda i: (i // tpb, 0), pipeline_mode=once),
                 pl.BlockSpec((ATT_WIDTH, nbr_seq), lambda i: (0, i // tpb), pipeline_mode=once),
                 pl.BlockSpec((1, past, ATT_WIDTH), lambda i: (i // tpb, 0, 0), pipeline_mode=once),
                 pl.BlockSpec((1, ATT_WIDTH, past), lambda i: (i // tpb, 0, 0), pipeline_mode=once),
                 _const_spec(bias.shape)]
    pt = pltpu.VMEM((n_groups, n_win * GRID_W, gw), BF16)
    y, o_lat = pl.pallas_call(
        functools.partial(_ffn_nbr_kernel, n_in=len(in_specs), seq_len=seq_len, n_col_tiles=nct, rows=rows, kr=kr,
                          rows_per_tile=rpt, tiles_per_batch=tpb),
        grid=(n_tiles,),
        in_specs=in_specs + nbr_specs,
        out_specs=[out_spec, row_blk],
        out_shape=[out_shape, jax.ShapeDtypeStruct((n_batch * nbr_seq, ATT_WIDTH), BF16)],
        scratch_shapes=scratch + [pt, pt],
        compiler_params=_params(1),
        name="mix_ffn_ctx_nbr_attn",
    )(*args, q, k, vt, k_ctx, vt_ctx, bias)
    return y, o_lat


def _retile_cast_kernel(w_ref, o_ref):
    tn = o_ref.shape[-1]
    for t in range(o_ref.shape[0]):
        o_ref[t] = w_ref[:, t * tn:(t + 1) * tn].astype(o_ref.dtype)


def _retile_cast(w, tn):
    r, c = w.shape
    group = 2 if (c // tn) % 2 == 0 else 1
    return pl.pallas_call(
        _retile_cast_kernel,
        grid=(c // (tn * group),),
        in_specs=[pl.BlockSpec((r, tn * group), lambda j: (0, j))],
        out_specs=pl.BlockSpec((group, r, tn), lambda j: (j, 0, 0)),
        out_shape=jax.ShapeDtypeStruct((c // tn, r, tn), BF16),
        compiler_params=_params(1),
        name="retile_cast",
    )(w)


def _cast_kernel(w_ref, o_ref):
    o_ref[...] = w_ref[...].astype(o_ref.dtype)


def _cast_rows(w, block_rows):
    r, c = w.shape
    spec = pl.BlockSpec((block_rows, c), lambda i: (i, 0))
    return pl.pallas_call(
        _cast_kernel,
        grid=(r // block_rows,),
        in_specs=[spec],
        out_specs=spec,
        out_shape=jax.ShapeDtypeStruct((r, c), BF16),
        compiler_params=_params(1),
        name="cast_rows",
    )(w)


def _layer_weights(norm1_g, norm2_g, w_in, b_gate, q_g, k_g, ml_g, w_out, w_up, conv_w, conv_b, w_down):
    d = w_in.shape[0]
    a, m, nh = ATT_WIDTH, ML_WIDTH, N_ML_HEADS
    def gate_lanes(g):
        g = g.reshape(g.shape[0], 2, 2, nh).transpose(0, 2, 3, 1).reshape(g.shape[0], 4 * nh)
        return jnp.pad(g, ((0, 0), (0, LANES - 4 * nh)))

    w_g = gate_lanes(w_in[:, 3 * a + 4 * m:])
    b_g = gate_lanes(b_gate.astype(F32).reshape(1, 4 * nh))
    head_id = jnp.arange(a) // ATT_HEAD_DIM
    d_ff = w_down.shape[0]
    tn = MXU_WIDTH
    nct = d_ff // tn

    def col_tiles(w):
        return w.reshape(w.shape[0], 2 * nct, tn).transpose(1, 0, 2)

    return {
        "norm1_g": norm1_g.reshape(1, d), "norm2_g": norm2_g.reshape(1, d),
        "w_in": w_in.astype(BF16),
        "w_g": w_g.astype(BF16), "b_g": b_g,
        "q_g": jnp.tile(q_g, N_ATT_HEADS).reshape(1, a), "k_g": jnp.tile(k_g, N_ATT_HEADS).reshape(1, a),
        "blockdiag": (head_id[:, None] == head_id[None, :]).astype(BF16),
        "ml_g": ml_g.reshape(1, m),
        "w_out": w_out.astype(BF16),
        "conv_w": col_tiles(conv_w.astype(F32)), "conv_b": col_tiles(conv_b.astype(F32).reshape(1, -1)),
        "w_up_f32": w_up, "w_down_f32": w_down, "ffn_tile": tn,
    }


def _rope_tables(seq_len):
    quarter = ML_HEAD_DIM // 4
    pos = np.arange(seq_len)
    inv_freq = ROPE_THETA ** (-np.arange(quarter, dtype=np.float64) / quarter)
    ang_r = (pos // GRID_W).astype(np.float64)[:, None] * inv_freq[None, :]
    ang_c = (pos % GRID_W).astype(np.float64)[:, None] * inv_freq[None, :]
    cos = np.concatenate([np.cos(ang_r)] * 2 + [np.cos(ang_c)] * 2, axis=-1)
    sin = np.concatenate([-np.sin(ang_r), np.sin(ang_r), -np.sin(ang_c), np.sin(ang_c)], axis=-1)
    return jnp.asarray(cos, F32), jnp.asarray(sin, F32)


def _layer(xp, xs, mod, wts, rope, ctx_kv, rpb, state, batch, seq, dec_batch, dec_seq):
    def ctx_row(i, tm):
        return 0 * i

    def lat_row(i, tm):
        return 1 + (i * tm) // dec_seq

    w_up, w_down, tn = wts["w_up_f32"], wts["w_down_f32"], wts["ffn_tile"]
    n_ctx_tiles = (batch * seq) // _token_tile(batch * seq, seq)
    if n_ctx_tiles >= max(w_up.shape[1] // (2 * tn), w_down.shape[0] // tn) and (w_up.shape[1] // tn) % 2 == 0:
        ctx, w_up_t, w_down_t = _ctx_in_projection_with_weight_casts(xp, mod, ctx_row, seq, wts, w_up, w_down, tn)
    else:
        ctx = _in_projection(xp, mod, ctx_row, seq, wts, None, latent=False)
        w_up_t = _retile_cast(w_up, tn)
        w_down_t = _cast_rows(w_down, tn).reshape(w_down.shape[0] // tn, tn, w_down.shape[1])
    wts = dict(wts, w_up=w_up_t, w_down=w_down_t)
    cache_k, cache_v = ctx[8], ctx[9]
    n_lat_tiles = (dec_batch * dec_seq) // _token_tile(dec_batch * dec_seq, dec_seq)
    if batch % n_lat_tiles == 0:
        lat, (o_att_c, o_ml_c, c_f, n_f, m_f) = _latent_in_projection_with_ctx_mixers(
            xs, mod, lat_row, dec_seq, wts, rope, ctx[:8], batch, seq)
    else:
        aq, ak, av, mq, mk, mvt, mot, gtt = ctx[:8]
        o_att_c = _context_attention(aq, ak, av, seq)
        o_ml_c, c_f, n_f, m_f = _mlstm(mq, mk, mvt, mot, gtt, wts["ml_g"], None, batch, seq, min(ML_CHUNK, seq))
        lat = _in_projection(xs, mod, lat_row, dec_seq, wts, rope, latent=True)
    aq, ak, avt, mq, mk, mvt, mot, gtt = lat[:8]
    n_rows = dec_batch * (dec_seq // GRID_W)
    if n_rows % n_ctx_tiles == 0 and n_ctx_tiles % dec_batch == 0:
        xp, o_att = _mix_ffn_with_nbr_attention(o_att_c, o_ml_c, xp, mod, ctx_row, seq, wts, aq, ak, avt,
                                                ctx_kv[0], ctx_kv[1], rpb, dec_batch, dec_seq)
    else:
        xp = _mix_ffn(o_att_c, o_ml_c, xp, mod, ctx_row, seq, wts)
        o_att = _neighborhood_attention(aq, ak, avt, ctx_kv[0], ctx_kv[1], rpb, dec_batch, dec_seq)
    o_ml = _mlstm(mq, mk, mvt, mot, gtt, wts["ml_g"], state, dec_batch, dec_seq, min(ML_CHUNK, dec_seq))[0]
    xs = _mix_ffn(o_att, o_ml, xs, mod, lat_row, dec_seq, wts)
    return xp, xs, cache_k, cache_v, (c_f, n_f[:, :, :, 0, :], _unpack_gate_rows(m_f))


def kernel(x_prompt, x_sample, cache_k, cache_v, state_C, state_n, state_m, c, c_ctx, w_mod, b_mod, norm1_g,
           norm2_g, w_in, b_gate, q_norm_g, k_norm_g, rpb, ml_norm_g, w_out, w_up, conv_w, conv_b, w_down):
    batch, seq, d = x_prompt.shape
    dec_batch, dec_seq, _ = x_sample.shape
    depth = w_mod.shape[0]
    past = cache_k.shape[2]
    cvecs = jnp.concatenate([c_ctx[None, :], c], axis=0).astype(F32)
    rope = _rope_tables(dec_seq)

    xp = x_prompt.reshape(batch * seq, d)
    xs = x_sample.reshape(dec_batch * dec_seq, d)
    ks, vs, cs, ns, ms = [], [], [], [], []
    for l in range(depth):
        wts = _layer_weights(norm1_g[l], norm2_g[l], w_in[l], b_gate[l], q_norm_g[l], k_norm_g[l],
                             ml_norm_g[l], w_out[l], w_up[l], conv_w[l], conv_b[l], w_down[l])
        mod = _modulation(cvecs, w_mod[l], b_mod[l])

        state = (state_C[:, l].astype(F32),
                 state_n[:, l].astype(F32)[:, :, :, None, :],
                 _pack_gate_rows(state_m[:, l].astype(F32)))
        ctx_kv = (cache_k[:, l].reshape(dec_batch, past, ATT_WIDTH).astype(BF16),
                  cache_v[:, l].reshape(dec_batch, past, ATT_WIDTH).transpose(0, 2, 1).astype(BF16))
        xp, xs, ak, av, st = _layer(xp, xs, mod, wts, rope, ctx_kv, rpb[l], state, batch, seq, dec_batch, dec_seq)
        ks.append(ak.reshape(batch, seq, N_ATT_HEADS, ATT_HEAD_DIM))
        vs.append(av.reshape(batch, seq, N_ATT_HEADS, ATT_HEAD_DIM))
        cs.append(st[0])
        ns.append(st[1])
        ms.append(st[2])
    return (xp.reshape(batch, seq, d), xs.reshape(dec_batch, dec_seq, d),
            jnp.stack(ks, axis=1), jnp.stack(vs, axis=1),
            jnp.stack(cs, axis=1), jnp.stack(ns, axis=1), jnp.stack(ms, axis=1))
```

```python
import functools

import jax
import jax.numpy as jnp
import numpy as np
from jax import lax
from jax.experimental import pallas as pl
from jax.experimental.pallas import tpu as pltpu

F32 = jnp.float32
BF16 = jnp.bfloat16

GRID_W = 64
N_ATT_HEADS = 8
ATT_HEAD_DIM = 64
ATT_WIDTH = N_ATT_HEADS * ATT_HEAD_DIM
WIN_ROWS = 8
WIN_COLS = 16
HEAD_GROUP = 4
N_ML_HEADS = 4
ML_HEAD_DIM = 128
ML_WIDTH = N_ML_HEADS * ML_HEAD_DIM
ROPE_THETA = 10000.0
EPS = 1e-6

LANES = 128
MXU_WIDTH = 256
ML_CHUNK = 256
MLSTM_UNROLL = 8
GATE_ROWS = 8
HALO = 16
TOKEN_TILE = 512
VMEM_LIMIT = 56 * 1024 * 1024


def _dot(a, b):
    return jnp.dot(a, b, preferred_element_type=F32)


def _dot_nt(a, b):
    return lax.dot_general(a, b, (((1,), (1,)), ((), ())), preferred_element_type=F32)


def _dot_tn(a, b):
    return lax.dot_general(a, b, (((0,), (0,)), ((), ())), preferred_element_type=F32)


def _const_spec(shape):
    nd = len(shape)
    return pl.BlockSpec(shape, lambda *_: (0,) * nd, pipeline_mode=pl.Buffered(1))


def _params(n_axes):
    return pltpu.CompilerParams(dimension_semantics=("arbitrary",) * n_axes,
                                vmem_limit_bytes=VMEM_LIMIT)


def _token_tile(n_tok, seq_len):
    tm = min(TOKEN_TILE, n_tok)
    while n_tok % tm or (seq_len % tm and tm % seq_len):
        tm //= 2
    return tm


def _log_sigmoid(x):
    return jnp.minimum(x, 0.0) - jnp.log1p(jnp.exp(-jnp.abs(x)))


def _mod_kernel(c_ref, w_ref, b_ref, o_ref):
    c = c_ref[...]
    s = c * jax.nn.sigmoid(c)
    o_ref[...] = _dot(s.astype(BF16), w_ref[...].astype(BF16)) + b_ref[...]


def _modulation(cvecs, w_mod, b_mod):
    r, d = cvecs.shape
    n = w_mod.shape[1]
    tn = d
    out = pl.pallas_call(
        _mod_kernel,
        grid=(n // tn,),
        in_specs=[pl.BlockSpec((r, d), lambda j: (0, 0)),
                  pl.BlockSpec((d, tn), lambda j: (0, j)),
                  pl.BlockSpec((1, tn), lambda j: (0, j))],
        out_specs=pl.BlockSpec((r, tn), lambda j: (0, j)),
        out_shape=jax.ShapeDtypeStruct((r, n), F32),
        compiler_params=_params(1),
        name="adaln_mod",
    )(cvecs, w_mod, b_mod.reshape(1, n))
    return out.reshape(r, 1, n)


def _rope(x, cos, sin_signed):
    lane = lax.broadcasted_iota(jnp.int32, x.shape, 1)
    partner = jnp.where((lane & 32) == 0, pltpu.roll(x, LANES - 32, 1), pltpu.roll(x, 32, 1))
    return x * cos + partner * sin_signed


def _inproj_kernel(*refs, latent):
    if latent:
        (x_ref, mod_ref, g1_ref, win_ref, wg_ref, bg_ref, qg_ref, kg_ref, bd_ref, cos_ref, sin_ref,
         aq_ref, ak_ref, av_ref, mq_ref, mk_ref, mv_ref, mo_ref, gtt_ref) = refs
    else:
        (x_ref, mod_ref, g1_ref, win_ref, wg_ref, bg_ref, qg_ref, kg_ref, bd_ref,
         aq_ref, ak_ref, av_ref, mq_ref, mk_ref, mv_ref, mo_ref, gtt_ref, ck_ref, cv_ref,
         cks_ref, cvs_ref) = refs
    x = x_ref[...]
    d = x.shape[-1]
    mod = mod_ref[0]
    sh1, sc1 = mod[:, 0:d], mod[:, d:2 * d]
    y = x * lax.rsqrt(jnp.mean(x * x, axis=-1, keepdims=True) + EPS)
    h = (y * g1_ref[...]) * (1.0 + sc1) + sh1
    hb = h.astype(BF16)

    def head_norm(a, g):
        ss = _dot((a * a).astype(BF16), bd_ref[...])
        return a * lax.rsqrt(ss * (1.0 / ATT_HEAD_DIM) + EPS) * g

    w = ATT_WIDTH
    att = _dot(hb, win_ref[:, 0:3 * w])
    aq_ref[...] = (head_norm(att[:, 0:w], qg_ref[...]) * ATT_HEAD_DIM ** -0.5).astype(aq_ref.dtype)
    kn = head_norm(att[:, w:2 * w], kg_ref[...])
    av = att[:, 2 * w:3 * w]
    ak_ref[...] = kn.astype(BF16)
    if latent:
        av_ref[...] = av.T.astype(BF16)
    else:
        av_ref[...] = av.astype(BF16)
        tm = kn.shape[0]
        for src, dst, scr in ((kn, ck_ref, cks_ref), (av, cv_ref, cvs_ref)):
            for hh in range(N_ATT_HEADS):
                pair = src[:, (hh // 2) * LANES:(hh // 2 + 1) * LANES]
                if hh % 2:
                    pair = pltpu.roll(pair, ATT_HEAD_DIM, 1)
                scr[pl.ds(hh, tm, stride=N_ATT_HEADS), :] = pair
            dst[...] = scr[...].reshape(tm, N_ATT_HEADS, LANES)[:, :, 0:ATT_HEAD_DIM]

    w = ML_WIDTH
    ml0 = 3 * ATT_WIDTH
    mq = _dot(hb, win_ref[:, ml0:ml0 + w])
    mk = _dot(hb, win_ref[:, ml0 + w:ml0 + 2 * w]) * ML_HEAD_DIM ** -0.5
    if latent:
        cos, sin = cos_ref[...], sin_ref[...]
        for hh in range(N_ML_HEADS):
            sl = slice(hh * ML_HEAD_DIM, (hh + 1) * ML_HEAD_DIM)
            mq_ref[:, sl] = _rope(mq[:, sl], cos, sin).astype(BF16)
            mk_ref[:, sl] = _rope(mk[:, sl], cos, sin).astype(BF16)
    else:
        mq_ref[...] = mq.astype(BF16)
        mk_ref[...] = mk.astype(BF16)
    mv_ref[...] = _dot(hb, win_ref[:, ml0 + 2 * w:ml0 + 3 * w]).T.astype(BF16)
    mo_ref[...] = _dot(hb, win_ref[:, ml0 + 3 * w:ml0 + 4 * w]).T
    gates = _dot(hb, wg_ref[...]) + bg_ref[...]
    gtt_ref[...] = gates.T[0:2 * GATE_ROWS]


def _in_projection_parts(x, mod, mod_row, seq_len, wts, rope, *, latent):
    n_tok, d = x.shape
    tm = _token_tile(n_tok, seq_len)
    tiles_per_seq = max(seq_len // tm, 1)

    def tok(width):
        return pl.BlockSpec((tm, width), lambda i: (i, 0))

    in_specs = [tok(d),
                pl.BlockSpec((1, 1, mod.shape[-1]), lambda i: (mod_row(i, tm), 0, 0)),
                _const_spec((1, d)),
                _const_spec(wts["w_in"].shape),
                _const_spec(wts["w_g"].shape), _const_spec((1, LANES)),
                _const_spec((1, ATT_WIDTH)), _const_spec((1, ATT_WIDTH)),
                _const_spec((ATT_WIDTH, ATT_WIDTH))]
    args = [x, mod, wts["norm1_g"], wts["w_in"], wts["w_g"], wts["b_g"],
            wts["q_g"], wts["k_g"], wts["blockdiag"]]
    if latent:
        in_specs += [pl.BlockSpec((tm, LANES), lambda i: (i % tiles_per_seq, 0))] * 2
        args += [rope[0], rope[1]]
    out_shape = [jax.ShapeDtypeStruct((n_tok, ATT_WIDTH), BF16),
                 jax.ShapeDtypeStruct((n_tok, ATT_WIDTH), BF16),
                 jax.ShapeDtypeStruct((ATT_WIDTH, n_tok) if latent else (n_tok, ATT_WIDTH), BF16),
                 jax.ShapeDtypeStruct((n_tok, ML_WIDTH), BF16),
                 jax.ShapeDtypeStruct((n_tok, ML_WIDTH), BF16),
                 jax.ShapeDtypeStruct((ML_WIDTH, n_tok), BF16),
                 jax.ShapeDtypeStruct((ML_WIDTH, n_tok), F32),
                 jax.ShapeDtypeStruct((2 * GATE_ROWS, n_tok), F32)]

    def tok_t(height):
        return pl.BlockSpec((height, tm), lambda i: (0, i))

    av_spec = tok_t(ATT_WIDTH) if latent else tok(ATT_WIDTH)
    out_specs = ([tok(ATT_WIDTH)] * 2 + [av_spec] + [tok(ML_WIDTH)] * 2 + [tok_t(ML_WIDTH)] * 2
                 + [tok_t(2 * GATE_ROWS)])
    if not latent:
        cache = jax.ShapeDtypeStruct((n_tok, N_ATT_HEADS, ATT_HEAD_DIM), F32)
        out_shape += [cache, cache]
        out_specs += [pl.BlockSpec((tm, N_ATT_HEADS, ATT_HEAD_DIM), lambda i: (i, 0, 0))] * 2
    scratch = [] if latent else [pltpu.VMEM((tm * N_ATT_HEADS, LANES), F32)] * 2
    return n_tok // tm, in_specs, args, out_shape, out_specs, scratch


def _in_projection(x, mod, mod_row, seq_len, wts, rope, *, latent):
    n_tiles, in_specs, args, out_shape, out_specs, scratch = _in_projection_parts(
        x, mod, mod_row, seq_len, wts, rope, latent=latent)
    return pl.pallas_call(
        functools.partial(_inproj_kernel, latent=latent),
        grid=(n_tiles,),
        in_specs=in_specs,
        out_specs=out_specs,
        out_shape=out_shape,
        scratch_shapes=scratch,
        compiler_params=_params(1),
        name="in_proj_latent" if latent else "in_proj_ctx",
    )(*args)


def _inproj_cast_kernel(*refs, n_in, n_out):
    ip_in, (wup_ref, wdn_ref) = refs[:n_in], refs[n_in:n_in + 2]
    outs = refs[n_in + 2:]
    ip_out, (wup_o, wdn_o), scratch = outs[:n_out], outs[n_out:n_out + 2], outs[n_out + 2:]
    _inproj_kernel(*ip_in, *ip_out, *scratch, latent=False)
    _retile_cast_kernel(wup_ref, wup_o)
    wdn_o[0] = wdn_ref[...].astype(wdn_o.dtype)


def _ctx_in_projection_with_weight_casts(x, mod, mod_row, seq_len, wts, w_up, w_down, tn):
    n_tiles, in_specs, args, out_shape, out_specs, scratch = _in_projection_parts(
        x, mod, mod_row, seq_len, wts, None, latent=False)
    d, two_dff = w_up.shape
    nct = w_down.shape[0] // tn
    n_grp = two_dff // (2 * tn)
    assert two_dff % (2 * tn) == 0 and max(n_grp, nct) <= n_tiles
    cast_in = [pl.BlockSpec((d, 2 * tn), lambda i: (0, jnp.minimum(i, n_grp - 1))),
               pl.BlockSpec((tn, d), lambda i: (jnp.minimum(i, nct - 1), 0))]
    cast_out = [pl.BlockSpec((2, d, tn), lambda i: (jnp.minimum(i, n_grp - 1), 0, 0)),
                pl.BlockSpec((1, tn, d), lambda i: (jnp.minimum(i, nct - 1), 0, 0))]
    cast_shape = [jax.ShapeDtypeStruct((two_dff // tn, d, tn), BF16), jax.ShapeDtypeStruct((nct, tn, d), BF16)]
    outs = pl.pallas_call(
        functools.partial(_inproj_cast_kernel, n_in=len(in_specs), n_out=len(out_specs)),
        grid=(n_tiles,),
        in_specs=in_specs + cast_in,
        out_specs=out_specs + cast_out,
        out_shape=out_shape + cast_shape,
        scratch_shapes=scratch,
        compiler_params=_params(1),
        name="in_proj_ctx_weight_casts",
    )(*args, w_up, w_down)
    return outs[:len(out_specs)], outs[len(out_specs)], outs[len(out_specs) + 1]


def _ctx_attn_kernel(q_ref, k_ref, v_ref, o_ref):
    n = q_ref.shape[0]
    gw = HEAD_GROUP * ATT_HEAD_DIM
    lane_head = lax.broadcasted_iota(jnp.int32, (n, gw), 1) // ATT_HEAD_DIM
    for g in range(N_ATT_HEADS // HEAD_GROUP):
        sl = slice(g * gw, (g + 1) * gw)
        q4 = q_ref[:, sl]
        qbd = jnp.concatenate([jnp.where(lane_head == hl, q4, jnp.zeros_like(q4))
                               for hl in range(HEAD_GROUP)], axis=0)
        s = _dot_nt(k_ref[:, sl], qbd)
        p = jnp.exp(s - jnp.max(s, axis=0, keepdims=True))
        p = p / jnp.sum(p, axis=0, keepdims=True)
        o4 = _dot_tn(p.astype(BF16), v_ref[:, sl])
        out = jnp.where(lane_head == 0, o4[0:n], 0.0)
        for hl in range(1, HEAD_GROUP):
            out = out + jnp.where(lane_head == hl, o4[hl * n:(hl + 1) * n], 0.0)
        o_ref[:, sl] = out.astype(o_ref.dtype)


def _context_attention(q, k, v, seq_len):
    n_tok = q.shape[0]
    spec = pl.BlockSpec((seq_len, ATT_WIDTH), lambda b: (b, 0))
    return pl.pallas_call(
        _ctx_attn_kernel,
        grid=(n_tok // seq_len,),
        in_specs=[spec, spec, spec],
        out_specs=spec,
        out_shape=jax.ShapeDtypeStruct((n_tok, ATT_WIDTH), BF16),
        compiler_params=_params(1),
        name="ctx_attn",
    )(q, k, v)


def _nbr_attn_kernel(q_ref, k_ref, vt_ref, kc_ref, vct_ref, bias_ref, o_ref, pt_ref, *, rows, kr):
    _nbr_row(pl.program_id(1), q_ref, k_ref, vt_ref, kc_ref, vct_ref, bias_ref, o_ref, pt_ref, rows=rows, kr=kr)


def _nbr_row(r, q_ref, k_ref, vt_ref, kc_ref, vct_ref, bias_ref, o_ref, pt_ref, *, rows, kr):
    n_win = kr + 2
    n_loc = kr * GRID_W
    gw = HEAD_GROUP * ATT_HEAD_DIM
    groups = N_ATT_HEADS // HEAD_GROUP
    lane_head = lax.broadcasted_iota(jnp.int32, (GRID_W, gw), 1) // ATT_HEAD_DIM
    rs = jnp.clip(r - kr // 2, 0, rows - kr)
    start = jnp.minimum(rs - (rs & 1), rows - n_win)
    delta = rs - start
    bias0 = pl.multiple_of((WIN_ROWS - 1 - (r - rs)) * GRID_W, GRID_W)
    zeros2 = jnp.zeros((2 * GRID_W, gw), BF16)
    for g in range(groups):
        sl = slice(g * gw, (g + 1) * gw)
        q4 = q_ref[:, sl]
        qbd = jnp.concatenate([jnp.where(lane_head == hl, q4, jnp.zeros_like(q4))
                               for hl in range(HEAD_GROUP)], axis=0)
        s_loc = _dot_nt(k_ref[pl.ds(pl.multiple_of(rs * GRID_W, GRID_W), n_loc), sl], qbd)
        s_loc = s_loc + bias_ref[g, pl.ds(bias0, n_loc), :]
        s_ctx = _dot_nt(kc_ref[0, :, sl], qbd)
        m = jnp.maximum(jnp.max(s_loc, axis=0, keepdims=True), jnp.max(s_ctx, axis=0, keepdims=True))
        p_loc = jnp.exp(s_loc - m)
        p_ctx = jnp.exp(s_ctx - m)
        l = jnp.sum(p_loc, axis=0, keepdims=True) + jnp.sum(p_ctx, axis=0, keepdims=True)
        pt_ref[g, 0:2 * GRID_W, :] = zeros2
        pt_ref[g, n_loc:n_loc + 2 * GRID_W, :] = zeros2
        pt_ref[g, pl.ds(pl.multiple_of(delta * GRID_W, GRID_W), n_loc), :] = p_loc.astype(BF16)
        vt_win = vt_ref[sl, pl.ds(pl.multiple_of(start * GRID_W, 2 * GRID_W), n_win * GRID_W)]
        ot = _dot(vt_win, pt_ref[g]) + _dot(vct_ref[0, sl, :], p_ctx.astype(BF16))
        o4 = (ot / l).T
        out = jnp.where(lane_head == 0, o4[0:GRID_W], 0.0)
        for hl in range(1, HEAD_GROUP):
            out = out + jnp.where(lane_head == hl, o4[hl * GRID_W:(hl + 1) * GRID_W], 0.0)
        o_ref[:, sl] = out.astype(o_ref.dtype)


def _nbr_bias_table(rpb):
    n_heads, n_dr, n_dc = rpb.shape
    c = WIN_COLS - 1
    assert n_heads == N_ATT_HEADS and n_dc == 2 * c + 1 and 2 * GRID_W == LANES
    f = rpb.astype(F32)
    packed = jnp.concatenate([f[..., c::-1], jnp.zeros((n_heads, n_dr, LANES - n_dc), F32), f[..., :c:-1]], axis=-1)
    n_groups = N_ATT_HEADS // HEAD_GROUP

    def sublane_groups(heads):
        return jnp.repeat(packed[heads], 8, axis=1).reshape(-1, LANES)

    return pl.pallas_call(
        functools.partial(_nbr_bias_kernel, n_dr=n_dr),
        out_shape=jax.ShapeDtypeStruct((n_groups, n_dr * GRID_W, HEAD_GROUP * GRID_W), F32),
        name="nbr_bias_table",
    )(sublane_groups(slice(0, None, 2)), sublane_groups(slice(1, None, 2)))


def _nbr_bias_kernel(even_ref, odd_ref, o_ref, *, n_dr):
    n_rows = even_ref.shape[0]
    s = lax.broadcasted_iota(jnp.int32, (n_rows, LANES), 0) % 8
    lane = lax.broadcasted_iota(jnp.int32, (n_rows, LANES), 1)
    cs = jnp.clip(lane % GRID_W - WIN_COLS // 2, 0, GRID_W - WIN_COLS)

    def sheared(ref):
        x = ref[...]
        for bit in (1, 2, 4):
            x = jnp.where((s & bit) != 0, pltpu.roll(x, bit, 1), x)
        return x

    even, odd = sheared(even_ref), sheared(odd_ref)
    for i in range(GRID_W // 8):
        key = 8 * i + s
        t = jnp.where(lane < GRID_W, pltpu.roll(even, 8 * i, 1) if i else even,
                      pltpu.roll(odd, (8 * i + GRID_W) % LANES, 1))
        t = jnp.where((key >= cs) & (key < cs + WIN_COLS), t, -jnp.inf)
        for p in range(n_rows // (8 * n_dr)):
            g, pair = divmod(p, HEAD_GROUP // 2)
            for dr in range(n_dr):
                r0 = (p * n_dr + dr) * 8
                o_ref[g, dr * GRID_W + 8 * i:dr * GRID_W + 8 * i + 8, pair * LANES:(pair + 1) * LANES] = t[r0:r0 + 8]


def _neighborhood_attention(q, k, vt, k_ctx, vt_ctx, rpb, n_batch, seq_len):
    rows = seq_len // GRID_W
    kr = min(WIN_ROWS, rows)
    n_win = kr + 2
    assert rows >= n_win and (rows - n_win) % 2 == 0
    bias = _nbr_bias_table(rpb)
    past = k_ctx.shape[1]
    gw = HEAD_GROUP * ATT_HEAD_DIM
    n_groups = N_ATT_HEADS // HEAD_GROUP
    row_spec = pl.BlockSpec((GRID_W, ATT_WIDTH), lambda b, r: (b * rows + r, 0))
    return pl.pallas_call(
        functools.partial(_nbr_attn_kernel, rows=rows, kr=kr),
        grid=(n_batch, rows),
        in_specs=[row_spec,
                  pl.BlockSpec((seq_len, ATT_WIDTH), lambda b, r: (b, 0)),
                  pl.BlockSpec((ATT_WIDTH, seq_len), lambda b, r: (0, b)),
                  pl.BlockSpec((1, past, ATT_WIDTH), lambda b, r: (b, 0, 0)),
                  pl.BlockSpec((1, ATT_WIDTH, past), lambda b, r: (b, 0, 0)),
                  _const_spec(bias.shape)],
        out_specs=row_spec,
        out_shape=jax.ShapeDtypeStruct((n_batch * seq_len, ATT_WIDTH), BF16),
        scratch_shapes=[pltpu.VMEM((n_groups, n_win * GRID_W, gw), BF16)],
        compiler_params=_params(2),
        name="nbr_attn",
    )(q, k, vt, k_ctx, vt_ctx, bias)


def _mlstm_kernel(*refs, chunk, n_chunks, has_state):
    dh = ML_HEAD_DIM
    nrep = 2 * GATE_ROWS
    if has_state:
        (q_ref, k_ref, vt_ref, mot_ref, gtt_ref, g_ref, c0_ref, n0_ref, m0_ref,
         o_ref, c_ref, n_ref, m_ref, hf_ref, hb_ref, cn_ref, mrun_ref) = refs
        for d in range(2):
            for hh in range(N_ML_HEADS):
                cn_ref[d, hh, 0:dh, :] = c0_ref[0, d, hh].T
                cn_ref[d, hh, dh:dh + nrep, :] = jnp.broadcast_to(n0_ref[0, d, hh], (nrep, dh))
        mrun_ref[...] = m0_ref[0]
    else:
        (q_ref, k_ref, vt_ref, mot_ref, gtt_ref, g_ref,
         o_ref, c_ref, n_ref, m_ref, hf_ref, hb_ref, cn_ref, mrun_ref) = refs
        cn_ref[...] = jnp.zeros_like(cn_ref)
        mrun_ref[...] = jnp.zeros_like(mrun_ref)
    use_state = has_state or n_chunks > 1

    i0 = lax.broadcasted_iota(jnp.int32, (chunk, chunk), 0)
    i1 = lax.broadcasted_iota(jnp.int32, (chunk, chunk), 1)
    row_id = lax.broadcasted_iota(jnp.int32, (GATE_ROWS, 1), 0)
    instances = [(hh, d) for d in range(2) for hh in range(N_ML_HEADS)]

    le = [i0 <= i1, i0 >= i1]

    def split3(x):
        hi = x.astype(BF16).astype(F32)
        r1 = x - hi
        mid = r1.astype(BF16).astype(F32)
        return jnp.concatenate([hi, mid, r1 - mid], axis=0).astype(BF16)

    def gate_rows(gi, lf_pre):
        n_rows = gi.shape[0]
        lf = _log_sigmoid(lf_pre)
        b_last = jnp.sum(lf, axis=1, keepdims=True)
        terms = split3(lf)
        per_dir = []
        for d in range(2):
            tri = jnp.where(le[d], 1.0, 0.0).astype(BF16)
            parts = _dot(terms, tri)
            cs = parts[0:n_rows] + parts[n_rows:2 * n_rows] + parts[2 * n_rows:3 * n_rows]
            w_end = b_last + gi - cs
            pad = jnp.zeros((LANES - n_rows, chunk), F32)
            per_dir.append((cs, w_end, jnp.max(w_end, axis=1, keepdims=True),
                            jnp.concatenate([gi - cs, pad], axis=0).T))
        return b_last, per_dir

    def chunk_rows(lo):
        return jnp.concatenate([gtt_ref[lo:lo + GATE_ROWS, c * chunk:(c + 1) * chunk] for c in range(n_chunks)],
                               axis=0)

    def body(i, gates):
        chunks = (i, n_chunks - 1 - i)
        rows = [pl.ds(pl.multiple_of(c * chunk, chunk), chunk) for c in chunks]
        m_prev = mrun_ref[...][:, 0:1]

        def qkv(hh, d):
            sl = slice(hh * dh, (hh + 1) * dh)
            return q_ref[rows[d], sl], k_ref[rows[d], sl], vt_ref[sl, rows[d]]

        st = [_dot_nt(qkv(hh, d)[1], qkv(hh, d)[0]) for hh, d in instances]
        if use_state:
            cn = [cn_ref[d, hh] for hh, d in instances]
            qct = [_dot_nt(cn[n].astype(BF16), qkv(hh, d)[0]) for n, (hh, d) in enumerate(instances)]

        if gates is None:
            gates = gate_rows(*[jnp.concatenate([gtt_ref[lo:lo + GATE_ROWS, r] for r in rows], axis=0)
                                for lo in (0, GATE_ROWS)])
            row0 = [0, GATE_ROWS]
        else:
            row0 = [GATE_ROWS * c for c in chunks]
        b_last_all, per_dir = gates
        a_col, cs8, w8, decay8, m_new = [], [], [], [], []
        for d in range(2):
            r = slice(row0[d], row0[d] + GATE_ROWS)
            cs_all, w_end_all, w_max_all, a_col_d = per_dir[d]
            b_last = b_last_all[r]
            m_d = jnp.maximum(b_last + m_prev, w_max_all[r])
            cs8.append(cs_all[r])
            decay8.append(jnp.exp(b_last + m_prev - m_d))
            w8.append(jnp.exp(w_end_all[r] - m_d))
            m_new.append(m_d)
            a_col.append(a_col_d)
        m_next = jnp.where((row_id & 1) == 0, m_new[0], m_new[1])
        mrun_ref[...] = jnp.broadcast_to(m_next, (GATE_ROWS, LANES))

        pt_all, g_all, den_all = [], [], []
        for n, (hh, d) in enumerate(instances):
            j = 2 * hh + d
            lane = row0[d] + j
            a = jnp.where(le[d], a_col[d][:, lane:lane + 1], -jnp.inf)
            g = jnp.maximum(jnp.max(a, axis=0, keepdims=True), m_prev[j:j + 1])
            pt = st[n] * jnp.exp(a - g)
            pt_all.append(pt)
            g_all.append(g)
            den_all.append(jnp.sum(pt, axis=0, keepdims=True))

        for n, (hh, d) in enumerate(instances):
            j = 2 * hh + d
            sl = slice(hh * dh, (hh + 1) * dh)
            _, k, vt = qkv(hh, d)
            g, den = g_all[n], den_all[n]
            num = _dot(vt, pt_all[n].astype(BF16))
            if use_state:
                w_inter = jnp.exp(m_prev[j:j + 1] - g)
                num = num + w_inter * qct[n][0:dh]
                den = den + w_inter * qct[n][dh:dh + 1]
            scale = 1.0 / jnp.maximum(jnp.abs(den), jnp.exp(-(cs8[d][j:j + 1] + g)))
            (hb_ref if d else hf_ref)[sl, rows[d]] = num * scale
            w_row = w8[d][j:j + 1]
            lhs = jnp.concatenate([(vt.astype(F32) * w_row).astype(BF16),
                                   jnp.broadcast_to(w_row, (nrep, chunk)).astype(BF16)], axis=0)
            upd = _dot(lhs, k)
            if use_state:
                upd = decay8[d][j:j + 1] * cn[n] + upd
            cn_ref[d, hh] = upd
        return 0

    def finish(c, _):
        rows = pl.ds(pl.multiple_of(c * chunk, chunk), chunk)
        for hh in range(N_ML_HEADS):
            sl = slice(hh * dh, (hh + 1) * dh)
            hs = hf_ref[sl, rows] + hb_ref[sl, rows]
            y = hs * lax.rsqrt(jnp.mean(hs * hs, axis=0, keepdims=True) + EPS) * g_ref[sl, :]
            y = y * jax.nn.sigmoid(mot_ref[sl, rows])
            o_ref[rows, sl] = y.T.astype(o_ref.dtype)
        return 0

    if n_chunks <= MLSTM_UNROLL:
        gates = gate_rows(chunk_rows(0), chunk_rows(GATE_ROWS))
        for c in range(n_chunks):
            body(c, gates)
        for c in range(n_chunks):
            finish(c, 0)
    else:
        lax.fori_loop(0, n_chunks, lambda i, _: body(i, None), 0)
        lax.fori_loop(0, n_chunks, finish, 0)
    for d in range(2):
        for hh in range(N_ML_HEADS):
            cn = cn_ref[d, hh]
            c_ref[0, d, hh] = cn[0:dh].T
            n_ref[0, d, hh] = cn[dh:dh + 1]
    m_ref[0] = mrun_ref[...]


def _mlstm(mq, mk, mvt, mot, gates_t, ml_g, state, n_batch, seq_len, chunk):
    dh = ML_HEAD_DIM
    nh = N_ML_HEADS
    width = nh * dh
    has_state = state is not None
    seq = pl.BlockSpec((seq_len, width), lambda b: (b, 0))
    seq_t = pl.BlockSpec((width, seq_len), lambda b: (0, b))
    st_c = pl.BlockSpec((1, 2, nh, dh, dh), lambda b: (b, 0, 0, 0, 0))
    st_n = pl.BlockSpec((1, 2, nh, 1, dh), lambda b: (b, 0, 0, 0, 0))
    st_m = pl.BlockSpec((1, GATE_ROWS, LANES), lambda b: (b, 0, 0))
    in_specs = [seq, seq, seq_t, seq_t,
                pl.BlockSpec((2 * GATE_ROWS, seq_len), lambda b: (0, b)),
                _const_spec((width, chunk))]
    args = [mq, mk, mvt, mot, gates_t, jnp.broadcast_to(ml_g.reshape(width, 1), (width, chunk))]
    if has_state:
        in_specs += [st_c, st_n, st_m]
        args += list(state)
    out_shape = [jax.ShapeDtypeStruct((n_batch * seq_len, width), BF16),
                 jax.ShapeDtypeStruct((n_batch, 2, nh, dh, dh), F32),
                 jax.ShapeDtypeStruct((n_batch, 2, nh, 1, dh), F32),
                 jax.ShapeDtypeStruct((n_batch, GATE_ROWS, LANES), F32)]
    return pl.pallas_call(
        functools.partial(_mlstm_kernel, chunk=chunk, n_chunks=seq_len // chunk, has_state=has_state),
        grid=(n_batch,),
        in_specs=in_specs,
        out_specs=[seq, st_c, st_n, st_m],
        out_shape=out_shape,
        scratch_shapes=[pltpu.VMEM((width, seq_len), F32), pltpu.VMEM((width, seq_len), F32),
                        pltpu.VMEM((2, nh, dh + 2 * GATE_ROWS, dh), F32), pltpu.VMEM((GATE_ROWS, LANES), F32)],
        compiler_params=_params(1),
        name="mlstm_latent" if has_state else "mlstm_ctx",
    )(*args)


def _pack_gate_rows(m):
    b = m.shape[0]
    packed = m.transpose(0, 2, 1).reshape(b, 2 * N_ML_HEADS, 1)
    return jnp.broadcast_to(packed, (b, GATE_ROWS, LANES))


def _unpack_gate_rows(m):
    b = m.shape[0]
    return m[:, :, 0].reshape(b, N_ML_HEADS, 2).transpose(0, 2, 1)


def _inproj_ctxmix_kernel(*refs, n_in, n_out, n_seq, seq_len, chunk):
    ip_in = refs[:n_in]
    q_ref, k_ref, v_ref, mq_ref, mk_ref, mvt_ref, mot_ref, gtt_ref, g_ref = refs[n_in:n_in + 9]
    outs = refs[n_in + 9:]
    ip_out = outs[:n_out]
    oatt_ref, oml_ref, c_ref, n_ref, m_ref = outs[n_out:n_out + 5]
    scratch = outs[n_out + 5:]
    _inproj_kernel(*ip_in, *ip_out, latent=True)
    for s in range(n_seq):
        rows = slice(s * seq_len, (s + 1) * seq_len)
        _ctx_attn_kernel(q_ref.at[rows, :], k_ref.at[rows, :], v_ref.at[rows, :], oatt_ref.at[rows, :])
        _mlstm_kernel(mq_ref.at[rows, :], mk_ref.at[rows, :], mvt_ref.at[:, rows], mot_ref.at[:, rows],
                      gtt_ref.at[:, rows], g_ref, oml_ref.at[rows, :], c_ref.at[s:s + 1], n_ref.at[s:s + 1],
                      m_ref.at[s:s + 1], *scratch[4 * s:4 * s + 4],
                      chunk=chunk, n_chunks=seq_len // chunk, has_state=False)


def _latent_in_projection_with_ctx_mixers(x, mod, mod_row, seq_len, wts, rope, ctx, n_ctx_batch, ctx_seq):
    n_tiles, in_specs, args, out_shape, out_specs, scratch = _in_projection_parts(
        x, mod, mod_row, seq_len, wts, rope, latent=True)
    n_seq = n_ctx_batch // n_tiles
    rows = n_seq * ctx_seq
    n_ctx_tok = n_ctx_batch * ctx_seq
    chunk = min(ML_CHUNK, ctx_seq)
    nh, dh = N_ML_HEADS, ML_HEAD_DIM

    def tok(width):
        return pl.BlockSpec((rows, width), lambda i: (i, 0))

    def tok_t(height):
        return pl.BlockSpec((height, rows), lambda i: (0, i))

    def per_seq(*tail):
        return pl.BlockSpec((n_seq,) + tail, lambda i: (i,) + (0,) * len(tail))

    aq, ak, av, mq, mk, mvt, mot, gtt = ctx
    mix_in_specs = ([tok(ATT_WIDTH)] * 3 + [tok(ML_WIDTH)] * 2 + [tok_t(ML_WIDTH)] * 2
                    + [tok_t(2 * GATE_ROWS), _const_spec((ML_WIDTH, chunk))])
    mix_args = [aq, ak, av, mq, mk, mvt, mot, gtt,
                jnp.broadcast_to(wts["ml_g"].reshape(ML_WIDTH, 1), (ML_WIDTH, chunk))]
    mix_out_shape = [jax.ShapeDtypeStruct((n_ctx_tok, ATT_WIDTH), BF16),
                     jax.ShapeDtypeStruct((n_ctx_tok, ML_WIDTH), BF16),
                     jax.ShapeDtypeStruct((n_ctx_batch, 2, nh, dh, dh), F32),
                     jax.ShapeDtypeStruct((n_ctx_batch, 2, nh, 1, dh), F32),
                     jax.ShapeDtypeStruct((n_ctx_batch, GATE_ROWS, LANES), F32)]
    mix_out_specs = [tok(ATT_WIDTH), tok(ML_WIDTH), per_seq(2, nh, dh, dh), per_seq(2, nh, 1, dh),
                     per_seq(GATE_ROWS, LANES)]
    mix_scratch = [pltpu.VMEM((ML_WIDTH, ctx_seq), F32), pltpu.VMEM((ML_WIDTH, ctx_seq), F32),
                   pltpu.VMEM((2, nh, dh + 2 * GATE_ROWS, dh), F32), pltpu.VMEM((GATE_ROWS, LANES), F32)] * n_seq
    outs = pl.pallas_call(
        functools.partial(_inproj_ctxmix_kernel, n_in=len(in_specs), n_out=len(out_specs), n_seq=n_seq,
                          seq_len=ctx_seq, chunk=chunk),
        grid=(n_tiles,),
        in_specs=in_specs + mix_in_specs,
        out_specs=out_specs + mix_out_specs,
        out_shape=out_shape + mix_out_shape,
        scratch_shapes=scratch + mix_scratch,
        compiler_params=_params(1),
        name="in_proj_latent_ctx_mixers",
    )(*args, *mix_args)
    return outs[:len(out_specs)], outs[len(out_specs):]


def _ffn_kernel(oa_ref, oap_ref, oan_ref, om_ref, omp_ref, omn_ref, x_ref, xp_ref, xn_ref, mod_ref, g2_ref,
                wout_ref, wup_ref, cw_ref, cb_ref, wd_ref,
                y_ref, oc_ref, lhs_ref, x1_ref, ug_ref, uv_ref, acc_ref, *, seq_len, n_col_tiles, side_work=None):
    tm, d = x_ref.shape
    aw = oa_ref.shape[1]

    def halo(next_ref, prev_ref):
        row = lax.broadcasted_iota(jnp.int32, next_ref.shape, 0)
        return jnp.where(row < HALO // 2, next_ref[...], prev_ref[...])

    oc_ref[0:tm, 0:aw] = oa_ref[...]
    oc_ref[0:tm, aw:] = om_ref[...]
    oc_ref[tm:tm + HALO, 0:aw] = halo(oan_ref, oap_ref)
    oc_ref[tm:tm + HALO, aw:] = halo(omn_ref, omp_ref)
    mod = mod_ref[0]
    g1 = mod[:, 2 * d:3 * d]
    sh2, sc2 = mod[:, 3 * d:4 * d], mod[:, 4 * d:5 * d]
    out = _dot(oc_ref[...], wout_ref[...])

    def norm2(x1):
        y = x1 * lax.rsqrt(jnp.mean(x1 * x1, axis=-1, keepdims=True) + EPS)
        return ((y * g2_ref[...]) * (1.0 + sc2) + sh2).astype(BF16)

    x1 = x_ref[...] + g1 * out[0:tm]
    x1_ref[...] = x1
    lhs_ref[0:tm, :] = norm2(x1)
    lhs_ref[tm:tm + HALO, :] = norm2(halo(xn_ref, xp_ref) + g1 * out[tm:tm + HALO])
    acc_ref[...] = jnp.zeros_like(acc_ref)
    sub = 8
    period = min(seq_len, tm)
    first_groups = sorted({r // sub for r in range(0, tm, period)})
    last_groups = sorted({(r + period - 1) // sub for r in range(0, tm, period)})
    tile0 = pl.program_id(0) * tm

    def zero_rows(x, groups, target):
        parts, at = [], 0
        for grp in groups:
            lo = grp * sub
            if lo > at:
                parts.append(x[at:lo])
            pos = (tile0 + lo + lax.broadcasted_iota(jnp.int32, (sub, 1), 0)) % seq_len
            parts.append(jnp.where(pos == target, 0.0, x[lo:lo + sub]))
            at = lo + sub
        if at < tm:
            parts.append(x[at:tm])
        return jnp.concatenate(parts, axis=0)

    def conv(u, cw, cb):
        prev = zero_rows(pltpu.roll(u, 1, 0)[0:tm], first_groups, 0)
        nxt = zero_rows(pltpu.roll(u, tm + HALO - 1, 0)[0:tm], last_groups, seq_len - 1)
        return prev * cw[0:1] + u[0:tm] * cw[1:2] + nxt * cw[2:3] + cb

    def up(j, slot):
        lhs = lhs_ref[...]
        ug_ref[slot] = _dot(lhs, wup_ref[j])
        uv_ref[slot] = _dot(lhs, wup_ref[n_col_tiles + j])

    def act(j, slot):
        gate = conv(ug_ref[slot], cw_ref[j], cb_ref[j])
        val = conv(uv_ref[slot], cw_ref[n_col_tiles + j], cb_ref[n_col_tiles + j])
        return ((gate * jax.nn.sigmoid(gate)) * val).astype(BF16)

    def stage(j, slot, next_up):
        a = act(j, slot)
        if next_up:
            up(j + 1, 1 - slot)
        return _dot(a, wd_ref[j])

    def pair(i, _):
        acc_ref[...] += stage(2 * i, 0, True)
        acc_ref[...] += stage(2 * i + 1, 1, True)
        return 0

    def finish(tail):
        g2 = mod_ref[0][:, 5 * d:6 * d]
        y_ref[...] = x1_ref[...] + g2 * (acc_ref[...] + tail)

    up(0, 0)
    if side_work:
        n_points = n_col_tiles + 1

        def hosted(point):
            for i, work in enumerate(side_work):
                if (i * n_points) // len(side_work) == point:
                    work()

        hosted(0)
        for j in range(n_col_tiles - 1):
            acc_ref[...] += stage(j, j % 2, True)
            hosted(j + 1)
        finish(stage(n_col_tiles - 1, (n_col_tiles - 1) % 2, False))
        hosted(n_col_tiles)
        return
    n_pairs = (n_col_tiles - 1) // 2
    if n_pairs:
        lax.fori_loop(0, n_pairs, pair, 0)
    j0 = 2 * n_pairs
    if n_col_tiles - j0 == 2:
        acc_ref[...] += stage(j0, 0, True)
        finish(stage(j0 + 1, 1, False))
    else:
        finish(stage(j0, 0, False))


def _mix_ffn_parts(o_att, o_ml, x, mod, mod_row, seq_len, wts):
    n_tok, d = x.shape
    tm = _token_tile(n_tok, seq_len)
    hpt = tm // HALO
    n_halo = n_tok // HALO
    nct, tn, _ = wts["w_down"].shape
    mix = o_att.shape[1] + o_ml.shape[1]

    def tile_and_halos(width):
        return [pl.BlockSpec((tm, width), lambda i: (i, 0)),
                pl.BlockSpec((HALO, width), lambda i: (jnp.maximum(i * hpt - 1, 0), 0)),
                pl.BlockSpec((HALO, width), lambda i: (jnp.minimum((i + 1) * hpt, n_halo - 1), 0))]

    in_specs = (tile_and_halos(o_att.shape[1]) + tile_and_halos(o_ml.shape[1]) + tile_and_halos(d) + [
        pl.BlockSpec((1, 1, mod.shape[-1]), lambda i: (mod_row(i, tm), 0, 0)),
        _const_spec((1, d)), _const_spec((mix, d)),
        _const_spec((2 * nct, d, tn)), _const_spec((2 * nct, 3, tn)), _const_spec((2 * nct, 1, tn)),
        _const_spec((nct, tn, d))])
    args = [o_att, o_att, o_att, o_ml, o_ml, o_ml, x, x, x, mod, wts["norm2_g"], wts["w_out"],
            wts["w_up"], wts["conv_w"], wts["conv_b"], wts["w_down"]]
    scratch = [pltpu.VMEM((tm + HALO, mix), BF16), pltpu.VMEM((tm + HALO, d), BF16), pltpu.VMEM((tm, d), F32),
               pltpu.VMEM((2, tm + HALO, tn), F32), pltpu.VMEM((2, tm + HALO, tn), F32), pltpu.VMEM((tm, d), F32)]
    return (n_tok // tm, nct, in_specs, args, pl.BlockSpec((tm, d), lambda i: (i, 0)),
            jax.ShapeDtypeStruct((n_tok, d), F32), scratch)


def _mix_ffn(o_att, o_ml, x, mod, mod_row, seq_len, wts):
    n_tiles, nct, in_specs, args, out_spec, out_shape, scratch = _mix_ffn_parts(
        o_att, o_ml, x, mod, mod_row, seq_len, wts)
    return pl.pallas_call(
        functools.partial(_ffn_kernel, seq_len=seq_len, n_col_tiles=nct),
        grid=(n_tiles,),
        in_specs=in_specs,
        out_specs=out_spec,
        out_shape=out_shape,
        scratch_shapes=scratch,
        compiler_params=_params(1),
        name="mix_ffn",
    )(*args)


def _ffn_nbr_kernel(*refs, n_in, seq_len, n_col_tiles, rows, kr, rows_per_tile, tiles_per_batch):
    ffn_in = refs[:n_in]
    q_ref, k_ref, vt_ref, kc_ref, vct_ref, bias_ref = refs[n_in:n_in + 6]
    y_ref, o_ref = refs[n_in + 6:n_in + 8]
    scratch = refs[n_in + 8:]
    ffn_scratch, pt_refs = scratch[:-2], scratch[-2:]
    base = (pl.program_id(0) % tiles_per_batch) * rows_per_tile

    def row(local, pt_ref):
        rr = pl.ds(pl.multiple_of(local * GRID_W, GRID_W), GRID_W)
        _nbr_row(base + local, q_ref.at[rr, :], k_ref, vt_ref, kc_ref, vct_ref, bias_ref, o_ref.at[rr, :], pt_ref,
                 rows=rows, kr=kr)

    side_work = [functools.partial(row, local, pt_refs[local % 2]) for local in range(rows_per_tile)]
    _ffn_kernel(*ffn_in, y_ref, *ffn_scratch, seq_len=seq_len, n_col_tiles=n_col_tiles, side_work=side_work)


def _mix_ffn_with_nbr_attention(o_att, o_ml, x, mod, mod_row, seq_len, wts, q, k, vt, k_ctx, vt_ctx, rpb,
                                n_batch, nbr_seq):
    n_tiles, nct, in_specs, args, out_spec, out_shape, scratch = _mix_ffn_parts(
        o_att, o_ml, x, mod, mod_row, seq_len, wts)
    rows = nbr_seq // GRID_W
    kr = min(WIN_ROWS, rows)
    n_win = kr + 2
    assert rows >= n_win and (rows - n_win) % 2 == 0
    rpt = (n_batch * rows) // n_tiles
    tpb = n_tiles // n_batch
    bias = _nbr_bias_table(rpb)
    past = k_ctx.shape[1]
    gw = HEAD_GROUP * ATT_HEAD_DIM
    n_groups = N_ATT_HEADS // HEAD_GROUP
    row_blk = pl.BlockSpec((rpt * GRID_W, ATT_WIDTH), lambda i: (i, 0))
    once = pl.Buffered(1)
    nbr_specs = [row_blk,
                 pl.BlockSpec((nbr_seq, ATT_WIDTH), lambda i: (i // tpb, 0), pipeline_mode=once),
                 pl.BlockSpec((ATT_WIDTH, nbr_seq), lambda i: (0, i // tpb), pipeline_mode=once),
                 pl.BlockSpec((1, past, ATT_WIDTH), lambda i: (i // tpb, 0, 0), pipeline_mode=once),
                 pl.BlockSpec((1, ATT_WIDTH, past), lambda i: (i // tpb, 0, 0), pipeline_mode=once),
                 _const_spec(bias.shape)]
    pt = pltpu.VMEM((n_groups, n_win * GRID_W, gw), BF16)
    y, o_lat = pl.pallas_call(
        functools.partial(_ffn_nbr_kernel, n_in=len(in_specs), seq_len=seq_len, n_col_tiles=nct, rows=rows, kr=kr,
                          rows_per_tile=rpt, tiles_per_batch=tpb),
        grid=(n_tiles,),
        in_specs=in_specs + nbr_specs,
        out_specs=[out_spec, row_blk],
        out_shape=[out_shape, jax.ShapeDtypeStruct((n_batch * nbr_seq, ATT_WIDTH), BF16)],
        scratch_shapes=scratch + [pt, pt],
        compiler_params=_params(1),
        name="mix_ffn_ctx_nbr_attn",
    )(*args, q, k, vt, k_ctx, vt_ctx, bias)
    return y, o_lat


def _retile_cast_kernel(w_ref, o_ref):
    tn = o_ref.shape[-1]
    for t in range(o_ref.shape[0]):
        o_ref[t] = w_ref[:, t * tn:(t + 1) * tn].astype(o_ref.dtype)


def _retile_cast(w, tn):
    r, c = w.shape
    group = 2 if (c // tn) % 2 == 0 else 1
    return pl.pallas_call(
        _retile_cast_kernel,
        grid=(c // (tn * group),),
        in_specs=[pl.BlockSpec((r, tn * group), lambda j: (0, j))],
        out_specs=pl.BlockSpec((group, r, tn), lambda j: (j, 0, 0)),
        out_shape=jax.ShapeDtypeStruct((c // tn, r, tn), BF16),
        compiler_params=_params(1),
        name="retile_cast",
    )(w)


def _cast_kernel(w_ref, o_ref):
    o_ref[...] = w_ref[...].astype(o_ref.dtype)


def _cast_rows(w, block_rows):
    r, c = w.shape
    spec = pl.BlockSpec((block_rows, c), lambda i: (i, 0))
    return pl.pallas_call(
        _cast_kernel,
        grid=(r // block_rows,),
        in_specs=[spec],
        out_specs=spec,
        out_shape=jax.ShapeDtypeStruct((r, c), BF16),
        compiler_params=_params(1),
        name="cast_rows",
    )(w)


def _layer_weights(norm1_g, norm2_g, w_in, b_gate, q_g, k_g, ml_g, w_out, w_up, conv_w, conv_b, w_down):
    d = w_in.shape[0]
    a, m, nh = ATT_WIDTH, ML_WIDTH, N_ML_HEADS
    def gate_lanes(g):
        g = g.reshape(g.shape[0], 2, 2, nh).transpose(0, 2, 3, 1).reshape(g.shape[0], 4 * nh)
        return jnp.pad(g, ((0, 0), (0, LANES - 4 * nh)))

    w_g = gate_lanes(w_in[:, 3 * a + 4 * m:])
    b_g = gate_lanes(b_gate.astype(F32).reshape(1, 4 * nh))
    head_id = np.arange(a) // ATT_HEAD_DIM
    d_ff = w_down.shape[0]
    tn = MXU_WIDTH
    nct = d_ff // tn

    def col_tiles(w):
        return w.reshape(w.shape[0], 2 * nct, tn).transpose(1, 0, 2)

    return {
        "norm1_g": norm1_g.reshape(1, d), "norm2_g": norm2_g.reshape(1, d),
        "w_in": w_in[:, :3 * a + 4 * m].astype(BF16),
        "w_g": w_g.astype(BF16), "b_g": b_g,
        "q_g": jnp.tile(q_g, N_ATT_HEADS).reshape(1, a), "k_g": jnp.tile(k_g, N_ATT_HEADS).reshape(1, a),
        "blockdiag": jnp.asarray(head_id[:, None] == head_id[None, :], BF16),
        "ml_g": ml_g.reshape(1, m),
        "w_out": w_out.astype(BF16),
        "conv_w": col_tiles(conv_w.astype(F32)), "conv_b": col_tiles(conv_b.astype(F32).reshape(1, -1)),
        "w_up_f32": w_up, "w_down_f32": w_down, "ffn_tile": tn,
    }


def _rope_tables(seq_len):
    quarter = ML_HEAD_DIM // 4
    pos = np.arange(seq_len)
    inv_freq = ROPE_THETA ** (-np.arange(quarter, dtype=np.float64) / quarter)
    ang_r = (pos // GRID_W).astype(np.float64)[:, None] * inv_freq[None, :]
    ang_c = (pos % GRID_W).astype(np.float64)[:, None] * inv_freq[None, :]
    cos = np.concatenate([np.cos(ang_r)] * 2 + [np.cos(ang_c)] * 2, axis=-1)
    sin = np.concatenate([-np.sin(ang_r), np.sin(ang_r), -np.sin(ang_c), np.sin(ang_c)], axis=-1)
    return jnp.asarray(cos, F32), jnp.asarray(sin, F32)


def _layer(xp, xs, mod, wts, rope, ctx_kv, rpb, state, batch, seq, dec_batch, dec_seq):
    def ctx_row(i, tm):
        return 0 * i

    def lat_row(i, tm):
        return 1 + (i * tm) // dec_seq

    w_up, w_down, tn = wts["w_up_f32"], wts["w_down_f32"], wts["ffn_tile"]
    n_ctx_tiles = (batch * seq) // _token_tile(batch * seq, seq)
    if n_ctx_tiles >= max(w_up.shape[1] // (2 * tn), w_down.shape[0] // tn) and (w_up.shape[1] // tn) % 2 == 0:
        ctx, w_up_t, w_down_t = _ctx_in_projection_with_weight_casts(xp, mod, ctx_row, seq, wts, w_up, w_down, tn)
    else:
        ctx = _in_projection(xp, mod, ctx_row, seq, wts, None, latent=False)
        w_up_t = _retile_cast(w_up, tn)
        w_down_t = _cast_rows(w_down, tn).reshape(w_down.shape[0] // tn, tn, w_down.shape[1])
    wts = dict(wts, w_up=w_up_t, w_down=w_down_t)
    cache_k, cache_v = ctx[8], ctx[9]
    n_lat_tiles = (dec_batch * dec_seq) // _token_tile(dec_batch * dec_seq, dec_seq)
    if batch % n_lat_tiles == 0:
        lat, (o_att_c, o_ml_c, c_f, n_f, m_f) = _latent_in_projection_with_ctx_mixers(
            xs, mod, lat_row, dec_seq, wts, rope, ctx[:8], batch, seq)
    else:
        aq, ak, av, mq, mk, mvt, mot, gtt = ctx[:8]
        o_att_c = _context_attention(aq, ak, av, seq)
        o_ml_c, c_f, n_f, m_f = _mlstm(mq, mk, mvt, mot, gtt, wts["ml_g"], None, batch, seq, min(ML_CHUNK, seq))
        lat = _in_projection(xs, mod, lat_row, dec_seq, wts, rope, latent=True)
    aq, ak, avt, mq, mk, mvt, mot, gtt = lat[:8]
    n_rows = dec_batch * (dec_seq // GRID_W)
    if n_rows % n_ctx_tiles == 0 and n_ctx_tiles % dec_batch == 0:
        xp, o_att = _mix_ffn_with_nbr_attention(o_att_c, o_ml_c, xp, mod, ctx_row, seq, wts, aq, ak, avt,
                                                ctx_kv[0], ctx_kv[1], rpb, dec_batch, dec_seq)
    else:
        xp = _mix_ffn(o_att_c, o_ml_c, xp, mod, ctx_row, seq, wts)
        o_att = _neighborhood_attention(aq, ak, avt, ctx_kv[0], ctx_kv[1], rpb, dec_batch, dec_seq)
    o_ml = _mlstm(mq, mk, mvt, mot, gtt, wts["ml_g"], state, dec_batch, dec_seq, min(ML_CHUNK, dec_seq))[0]
    xs = _mix_ffn(o_att, o_ml, xs, mod, lat_row, dec_seq, wts)
    return xp, xs, cache_k, cache_v, (c_f, n_f[:, :, :, 0, :], _unpack_gate_rows(m_f))


def kernel(x_prompt, x_sample, cache_k, cache_v, state_C, state_n, state_m, c, c_ctx, w_mod, b_mod, norm1_g,
           norm2_g, w_in, b_gate, q_norm_g, k_norm_g, rpb, ml_norm_g, w_out, w_up, conv_w, conv_b, w_down):
    batch, seq, d = x_prompt.shape
    dec_batch, dec_seq, _ = x_sample.shape
    depth = w_mod.shape[0]
    past = cache_k.shape[2]
    cvecs = jnp.concatenate([c_ctx[None, :], c], axis=0).astype(F32)
    rope = _rope_tables(dec_seq)

    xp = x_prompt.reshape(batch * seq, d)
    xs = x_sample.reshape(dec_batch * dec_seq, d)
    ks, vs, cs, ns, ms = [], [], [], [], []
    for l in range(depth):
        wts = _layer_weights(norm1_g[l], norm2_g[l], w_in[l], b_gate[l], q_norm_g[l], k_norm_g[l],
                             ml_norm_g[l], w_out[l], w_up[l], conv_w[l], conv_b[l], w_down[l])
        mod = _modulation(cvecs, w_mod[l], b_mod[l])

        state = (state_C[:, l].astype(F32),
                 state_n[:, l].astype(F32)[:, :, :, None, :],
                 _pack_gate_rows(state_m[:, l].astype(F32)))
        ctx_kv = (cache_k[:, l].reshape(dec_batch, past, ATT_WIDTH).astype(BF16),
                  cache_v[:, l].reshape(dec_batch, past, ATT_WIDTH).transpose(0, 2, 1).astype(BF16))
        xp, xs, ak, av, st = _layer(xp, xs, mod, wts, rope, ctx_kv, rpb[l], state, batch, seq, dec_batch, dec_seq)
        ks.append(ak.reshape(batch, seq, N_ATT_HEADS, ATT_HEAD_DIM))
        vs.append(av.reshape(batch, seq, N_ATT_HEADS, ATT_HEAD_DIM))
        cs.append(st[0])
        ns.append(st[1])
        ms.append(st[2])
    return (xp.reshape(batch, seq, d), xs.reshape(dec_batch, dec_seq, d),
            jnp.stack(ks, axis=1), jnp.stack(vs, axis=1),
            jnp.stack(cs, axis=1), jnp.stack(ns, axis=1), jnp.stack(ms, axis=1))
```

```python
import functools

import jax
import jax.numpy as jnp
import numpy as np
from jax import lax
from jax.experimental import pallas as pl
from jax.experimental.pallas import tpu as pltpu

F32 = jnp.float32
BF16 = jnp.bfloat16

GRID_W = 64
N_ATT_HEADS = 8
ATT_HEAD_DIM = 64
ATT_WIDTH = N_ATT_HEADS * ATT_HEAD_DIM
WIN_ROWS = 8
WIN_COLS = 16
HEAD_GROUP = 4
N_ML_HEADS = 4
ML_HEAD_DIM = 128
ML_WIDTH = N_ML_HEADS * ML_HEAD_DIM
ROPE_THETA = 10000.0
EPS = 1e-6

LANES = 128
MXU_WIDTH = 256
ML_CHUNK = 256
MLSTM_UNROLL = 8
GATE_ROWS = 8
HALO = 16
TOKEN_TILE = 512
VMEM_LIMIT = 56 * 1024 * 1024


def _dot(a, b):
    return jnp.dot(a, b, preferred_element_type=F32)


def _dot_nt(a, b):
    return lax.dot_general(a, b, (((1,), (1,)), ((), ())), preferred_element_type=F32)


def _dot_tn(a, b):
    return lax.dot_general(a, b, (((0,), (0,)), ((), ())), preferred_element_type=F32)


def _const_spec(shape):
    nd = len(shape)
    return pl.BlockSpec(shape, lambda *_: (0,) * nd, pipeline_mode=pl.Buffered(1))


def _params(n_axes):
    return pltpu.CompilerParams(dimension_semantics=("arbitrary",) * n_axes,
                                vmem_limit_bytes=VMEM_LIMIT)


def _token_tile(n_tok, seq_len):
    tm = min(TOKEN_TILE, n_tok)
    while n_tok % tm or (seq_len % tm and tm % seq_len):
        tm //= 2
    return tm


def _log_sigmoid(x):
    return jnp.minimum(x, 0.0) - jnp.log1p(jnp.exp(-jnp.abs(x)))


def _mod_kernel(c_ref, w_ref, b_ref, o_ref):
    c = c_ref[...]
    s = c * jax.nn.sigmoid(c)
    o_ref[...] = _dot(s.astype(BF16), w_ref[...].astype(BF16)) + b_ref[...]


def _modulation(cvecs, w_mod, b_mod):
    r, d = cvecs.shape
    n = w_mod.shape[1]
    tn = d
    out = pl.pallas_call(
        _mod_kernel,
        grid=(n // tn,),
        in_specs=[pl.BlockSpec((r, d), lambda j: (0, 0)),
                  pl.BlockSpec((d, tn), lambda j: (0, j)),
                  pl.BlockSpec((1, tn), lambda j: (0, j))],
        out_specs=pl.BlockSpec((r, tn), lambda j: (0, j)),
        out_shape=jax.ShapeDtypeStruct((r, n), F32),
        compiler_params=_params(1),
        name="adaln_mod",
    )(cvecs, w_mod, b_mod.reshape(1, n))
    return out.reshape(r, 1, n)


def _rope(x, cos, sin_signed):
    lane = lax.broadcasted_iota(jnp.int32, x.shape, 1)
    partner = jnp.where((lane & 32) == 0, pltpu.roll(x, LANES - 32, 1), pltpu.roll(x, 32, 1))
    return x * cos + partner * sin_signed


def _inproj_kernel(*refs, latent):
    if latent:
        (x_ref, mod_ref, g1_ref, win_ref, wg_ref, bg_ref, qg_ref, kg_ref, bd_ref, cos_ref, sin_ref,
         aq_ref, ak_ref, av_ref, mq_ref, mk_ref, mv_ref, mo_ref, gtt_ref) = refs
    else:
        (x_ref, mod_ref, g1_ref, win_ref, wg_ref, bg_ref, qg_ref, kg_ref, bd_ref,
         aq_ref, ak_ref, av_ref, mq_ref, mk_ref, mv_ref, mo_ref, gtt_ref, ck_ref, cv_ref,
         cks_ref, cvs_ref) = refs
    x = x_ref[...]
    d = x.shape[-1]
    mod = mod_ref[0]
    sh1, sc1 = mod[:, 0:d], mod[:, d:2 * d]
    y = x * lax.rsqrt(jnp.mean(x * x, axis=-1, keepdims=True) + EPS)
    h = (y * g1_ref[...]) * (1.0 + sc1) + sh1
    hb = h.astype(BF16)

    def head_norm(a, g):
        ss = _dot((a * a).astype(BF16), bd_ref[...])
        return a * lax.rsqrt(ss * (1.0 / ATT_HEAD_DIM) + EPS) * g

    w = ATT_WIDTH
    att = _dot_nt(hb, win_ref[0:3 * w, :])
    aq_ref[...] = (head_norm(att[:, 0:w], qg_ref[...]) * ATT_HEAD_DIM ** -0.5).astype(aq_ref.dtype)
    kn = head_norm(att[:, w:2 * w], kg_ref[...])
    av = att[:, 2 * w:3 * w]
    ak_ref[...] = kn.astype(BF16)
    if latent:
        av_ref[...] = av.T.astype(BF16)
    else:
        av_ref[...] = av.astype(BF16)
        tm = kn.shape[0]
        for src, dst, scr in ((kn, ck_ref, cks_ref), (av, cv_ref, cvs_ref)):
            for hh in range(N_ATT_HEADS):
                pair = src[:, (hh // 2) * LANES:(hh // 2 + 1) * LANES]
                if hh % 2:
                    pair = pltpu.roll(pair, ATT_HEAD_DIM, 1)
                scr[pl.ds(hh, tm, stride=N_ATT_HEADS), :] = pair
            dst[...] = scr[...].reshape(tm, N_ATT_HEADS, LANES)[:, :, 0:ATT_HEAD_DIM]

    w = ML_WIDTH
    ml0 = 3 * ATT_WIDTH
    mq = _dot_nt(hb, win_ref[ml0:ml0 + w, :])
    mk = _dot_nt(hb, win_ref[ml0 + w:ml0 + 2 * w, :]) * ML_HEAD_DIM ** -0.5
    if latent:
        cos, sin = cos_ref[...], sin_ref[...]
        for hh in range(N_ML_HEADS):
            sl = slice(hh * ML_HEAD_DIM, (hh + 1) * ML_HEAD_DIM)
            mq_ref[:, sl] = _rope(mq[:, sl], cos, sin).astype(BF16)
            mk_ref[:, sl] = _rope(mk[:, sl], cos, sin).astype(BF16)
    else:
        mq_ref[...] = mq.astype(BF16)
        mk_ref[...] = mk.astype(BF16)
    mv_ref[...] = _dot_nt(hb, win_ref[ml0 + 2 * w:ml0 + 3 * w, :]).T.astype(BF16)
    mo_ref[...] = _dot_nt(hb, win_ref[ml0 + 3 * w:ml0 + 4 * w, :]).T
    gates = _dot(hb, wg_ref[...]) + bg_ref[...]
    gtt_ref[...] = gates.T[0:2 * GATE_ROWS]


def _in_projection_parts(x, mod, mod_row, seq_len, wts, rope, *, latent):
    n_tok, d = x.shape
    tm = _token_tile(n_tok, seq_len)
    tiles_per_seq = max(seq_len // tm, 1)

    def tok(width):
        return pl.BlockSpec((tm, width), lambda i: (i, 0))

    in_specs = [tok(d),
                pl.BlockSpec((1, 1, mod.shape[-1]), lambda i: (mod_row(i, tm), 0, 0)),
                _const_spec((1, d)),
                _const_spec(wts["w_in"].shape),
                _const_spec(wts["w_g"].shape), _const_spec((1, LANES)),
                _const_spec((1, ATT_WIDTH)), _const_spec((1, ATT_WIDTH)),
                _const_spec((ATT_WIDTH, ATT_WIDTH))]
    args = [x, mod, wts["norm1_g"], wts["w_in"], wts["w_g"], wts["b_g"],
            wts["q_g"], wts["k_g"], wts["blockdiag"]]
    if latent:
        in_specs += [pl.BlockSpec((tm, LANES), lambda i: (i % tiles_per_seq, 0))] * 2
        args += [rope[0], rope[1]]
    out_shape = [jax.ShapeDtypeStruct((n_tok, ATT_WIDTH), BF16),
                 jax.ShapeDtypeStruct((n_tok, ATT_WIDTH), BF16),
                 jax.ShapeDtypeStruct((ATT_WIDTH, n_tok) if latent else (n_tok, ATT_WIDTH), BF16),
                 jax.ShapeDtypeStruct((n_tok, ML_WIDTH), BF16),
                 jax.ShapeDtypeStruct((n_tok, ML_WIDTH), BF16),
                 jax.ShapeDtypeStruct((ML_WIDTH, n_tok), BF16),
                 jax.ShapeDtypeStruct((ML_WIDTH, n_tok), F32),
                 jax.ShapeDtypeStruct((2 * GATE_ROWS, n_tok), F32)]

    def tok_t(height):
        return pl.BlockSpec((height, tm), lambda i: (0, i))

    av_spec = tok_t(ATT_WIDTH) if latent else tok(ATT_WIDTH)
    out_specs = ([tok(ATT_WIDTH)] * 2 + [av_spec] + [tok(ML_WIDTH)] * 2 + [tok_t(ML_WIDTH)] * 2
                 + [tok_t(2 * GATE_ROWS)])
    if not latent:
        cache = jax.ShapeDtypeStruct((n_tok, N_ATT_HEADS, ATT_HEAD_DIM), F32)
        out_shape += [cache, cache]
        out_specs += [pl.BlockSpec((tm, N_ATT_HEADS, ATT_HEAD_DIM), lambda i: (i, 0, 0))] * 2
    scratch = [] if latent else [pltpu.VMEM((tm * N_ATT_HEADS, LANES), F32)] * 2
    return n_tok // tm, in_specs, args, out_shape, out_specs, scratch


def _in_projection(x, mod, mod_row, seq_len, wts, rope, *, latent):
    n_tiles, in_specs, args, out_shape, out_specs, scratch = _in_projection_parts(
        x, mod, mod_row, seq_len, wts, rope, latent=latent)
    return pl.pallas_call(
        functools.partial(_inproj_kernel, latent=latent),
        grid=(n_tiles,),
        in_specs=in_specs,
        out_specs=out_specs,
        out_shape=out_shape,
        scratch_shapes=scratch,
        compiler_params=_params(1),
        name="in_proj_latent" if latent else "in_proj_ctx",
    )(*args)


def _inproj_cast_kernel(*refs, n_in, n_out):
    ip_in, (wup_ref, wdn_ref) = refs[:n_in], refs[n_in:n_in + 2]
    outs = refs[n_in + 2:]
    ip_out, (wup_o, wdn_o), scratch = outs[:n_out], outs[n_out:n_out + 2], outs[n_out + 2:]
    _inproj_kernel(*ip_in, *ip_out, *scratch, latent=False)
    _retile_cast_kernel(wup_ref, wup_o)
    wdn_o[0] = wdn_ref[...].astype(wdn_o.dtype)


def _ctx_in_projection_with_weight_casts(x, mod, mod_row, seq_len, wts, w_up, w_down, tn):
    n_tiles, in_specs, args, out_shape, out_specs, scratch = _in_projection_parts(
        x, mod, mod_row, seq_len, wts, None, latent=False)
    d, two_dff = w_up.shape
    nct = w_down.shape[0] // tn
    n_grp = two_dff // (2 * tn)
    assert two_dff % (2 * tn) == 0 and max(n_grp, nct) <= n_tiles
    cast_in = [pl.BlockSpec((d, 2 * tn), lambda i: (0, jnp.minimum(i, n_grp - 1))),
               pl.BlockSpec((tn, d), lambda i: (jnp.minimum(i, nct - 1), 0))]
    cast_out = [pl.BlockSpec((2, d, tn), lambda i: (jnp.minimum(i, n_grp - 1), 0, 0)),
                pl.BlockSpec((1, tn, d), lambda i: (jnp.minimum(i, nct - 1), 0, 0))]
    cast_shape = [jax.ShapeDtypeStruct((two_dff // tn, d, tn), BF16), jax.ShapeDtypeStruct((nct, tn, d), BF16)]
    outs = pl.pallas_call(
        functools.partial(_inproj_cast_kernel, n_in=len(in_specs), n_out=len(out_specs)),
        grid=(n_tiles,),
        in_specs=in_specs + cast_in,
        out_specs=out_specs + cast_out,
        out_shape=out_shape + cast_shape,
        scratch_shapes=scratch,
        compiler_params=_params(1),
        name="in_proj_ctx_weight_casts",
    )(*args, w_up, w_down)
    return outs[:len(out_specs)], outs[len(out_specs)], outs[len(out_specs) + 1]


def _ctx_attn_kernel(q_ref, k_ref, v_ref, o_ref):
    n = q_ref.shape[0]
    gw = HEAD_GROUP * ATT_HEAD_DIM
    lane_head = lax.broadcasted_iota(jnp.int32, (n, gw), 1) // ATT_HEAD_DIM
    for g in range(N_ATT_HEADS // HEAD_GROUP):
        sl = slice(g * gw, (g + 1) * gw)
        q4 = q_ref[:, sl]
        qbd = jnp.concatenate([jnp.where(lane_head == hl, q4, jnp.zeros_like(q4))
                               for hl in range(HEAD_GROUP)], axis=0)
        s = _dot_nt(k_ref[:, sl], qbd)
        p = jnp.exp(s - jnp.max(s, axis=0, keepdims=True))
        p = p / jnp.sum(p, axis=0, keepdims=True)
        o4 = _dot_tn(p.astype(BF16), v_ref[:, sl])
        out = jnp.where(lane_head == 0, o4[0:n], 0.0)
        for hl in range(1, HEAD_GROUP):
            out = out + jnp.where(lane_head == hl, o4[hl * n:(hl + 1) * n], 0.0)
        o_ref[:, sl] = out.astype(o_ref.dtype)


def _context_attention(q, k, v, seq_len):
    n_tok = q.shape[0]
    spec = pl.BlockSpec((seq_len, ATT_WIDTH), lambda b: (b, 0))
    return pl.pallas_call(
        _ctx_attn_kernel,
        grid=(n_tok // seq_len,),
        in_specs=[spec, spec, spec],
        out_specs=spec,
        out_shape=jax.ShapeDtypeStruct((n_tok, ATT_WIDTH), BF16),
        compiler_params=_params(1),
        name="ctx_attn",
    )(q, k, v)


def _nbr_attn_kernel(q_ref, k_ref, vt_ref, kc_ref, vct_ref, bias_ref, o_ref, pt_ref, *, rows, kr):
    _nbr_row(pl.program_id(1), q_ref, k_ref, vt_ref, kc_ref, vct_ref, bias_ref, o_ref, pt_ref, rows=rows, kr=kr)


def _nbr_row(r, q_ref, k_ref, vt_ref, kc_ref, vct_ref, bias_ref, o_ref, pt_ref, *, rows, kr):
    n_win = kr + 2
    n_loc = kr * GRID_W
    gw = HEAD_GROUP * ATT_HEAD_DIM
    groups = N_ATT_HEADS // HEAD_GROUP
    lane_head = lax.broadcasted_iota(jnp.int32, (GRID_W, gw), 1) // ATT_HEAD_DIM
    rs = jnp.clip(r - kr // 2, 0, rows - kr)
    start = jnp.minimum(rs - (rs & 1), rows - n_win)
    delta = rs - start
    bias0 = pl.multiple_of((WIN_ROWS - 1 - (r - rs)) * GRID_W, GRID_W)
    zeros2 = jnp.zeros((2 * GRID_W, gw), BF16)
    for g in range(groups):
        sl = slice(g * gw, (g + 1) * gw)
        q4 = q_ref[:, sl]
        qbd = jnp.concatenate([jnp.where(lane_head == hl, q4, jnp.zeros_like(q4))
                               for hl in range(HEAD_GROUP)], axis=0)
        s_loc = _dot_nt(k_ref[pl.ds(pl.multiple_of(rs * GRID_W, GRID_W), n_loc), sl], qbd)
        s_loc = s_loc + bias_ref[g, pl.ds(bias0, n_loc), :]
        s_ctx = _dot_nt(kc_ref[0, :, sl], qbd)
        m = jnp.maximum(jnp.max(s_loc, axis=0, keepdims=True), jnp.max(s_ctx, axis=0, keepdims=True))
        p_loc = jnp.exp(s_loc - m)
        p_ctx = jnp.exp(s_ctx - m)
        l = jnp.sum(p_loc, axis=0, keepdims=True) + jnp.sum(p_ctx, axis=0, keepdims=True)
        pt_ref[g, 0:2 * GRID_W, :] = zeros2
        pt_ref[g, n_loc:n_loc + 2 * GRID_W, :] = zeros2
        pt_ref[g, pl.ds(pl.multiple_of(delta * GRID_W, GRID_W), n_loc), :] = p_loc.astype(BF16)
        vt_win = vt_ref[sl, pl.ds(pl.multiple_of(start * GRID_W, 2 * GRID_W), n_win * GRID_W)]
        ot = _dot(vt_win, pt_ref[g]) + _dot(vct_ref[0, sl, :], p_ctx.astype(BF16))
        o4 = (ot / l).T
        out = jnp.where(lane_head == 0, o4[0:GRID_W], 0.0)
        for hl in range(1, HEAD_GROUP):
            out = out + jnp.where(lane_head == hl, o4[hl * GRID_W:(hl + 1) * GRID_W], 0.0)
        o_ref[:, sl] = out.astype(o_ref.dtype)


def _nbr_bias_table(rpb):
    n_heads, n_dr, n_dc = rpb.shape
    c = WIN_COLS - 1
    assert n_heads == N_ATT_HEADS and n_dc == 2 * c + 1 and 2 * GRID_W == LANES
    f = rpb.astype(F32)
    packed = jnp.concatenate([f[..., c::-1], jnp.zeros((n_heads, n_dr, LANES - n_dc), F32), f[..., :c:-1]], axis=-1)
    n_groups = N_ATT_HEADS // HEAD_GROUP

    def sublane_groups(heads):
        return jnp.repeat(packed[heads], 8, axis=1).reshape(-1, LANES)

    return pl.pallas_call(
        functools.partial(_nbr_bias_kernel, n_dr=n_dr),
        out_shape=jax.ShapeDtypeStruct((n_groups, n_dr * GRID_W, HEAD_GROUP * GRID_W), F32),
        name="nbr_bias_table",
    )(sublane_groups(slice(0, None, 2)), sublane_groups(slice(1, None, 2)))


def _nbr_bias_kernel(even_ref, odd_ref, o_ref, *, n_dr):
    n_rows = even_ref.shape[0]
    s = lax.broadcasted_iota(jnp.int32, (n_rows, LANES), 0) % 8
    lane = lax.broadcasted_iota(jnp.int32, (n_rows, LANES), 1)
    cs = jnp.clip(lane % GRID_W - WIN_COLS // 2, 0, GRID_W - WIN_COLS)

    def sheared(ref):
        x = ref[...]
        for bit in (1, 2, 4):
            x = jnp.where((s & bit) != 0, pltpu.roll(x, bit, 1), x)
        return x

    even, odd = sheared(even_ref), sheared(odd_ref)
    for i in range(GRID_W // 8):
        key = 8 * i + s
        t = jnp.where(lane < GRID_W, pltpu.roll(even, 8 * i, 1) if i else even,
                      pltpu.roll(odd, (8 * i + GRID_W) % LANES, 1))
        t = jnp.where((key >= cs) & (key < cs + WIN_COLS), t, -jnp.inf)
        for p in range(n_rows // (8 * n_dr)):
            g, pair = divmod(p, HEAD_GROUP // 2)
            for dr in range(n_dr):
                r0 = (p * n_dr + dr) * 8
                o_ref[g, dr * GRID_W + 8 * i:dr * GRID_W + 8 * i + 8, pair * LANES:(pair + 1) * LANES] = t[r0:r0 + 8]


def _neighborhood_attention(q, k, vt, k_ctx, vt_ctx, rpb, n_batch, seq_len):
    rows = seq_len // GRID_W
    kr = min(WIN_ROWS, rows)
    n_win = kr + 2
    assert rows >= n_win and (rows - n_win) % 2 == 0
    bias = _nbr_bias_table(rpb)
    past = k_ctx.shape[1]
    gw = HEAD_GROUP * ATT_HEAD_DIM
    n_groups = N_ATT_HEADS // HEAD_GROUP
    row_spec = pl.BlockSpec((GRID_W, ATT_WIDTH), lambda b, r: (b * rows + r, 0))
    return pl.pallas_call(
        functools.partial(_nbr_attn_kernel, rows=rows, kr=kr),
        grid=(n_batch, rows),
        in_specs=[row_spec,
                  pl.BlockSpec((seq_len, ATT_WIDTH), lambda b, r: (b, 0)),
                  pl.BlockSpec((ATT_WIDTH, seq_len), lambda b, r: (0, b)),
                  pl.BlockSpec((1, past, ATT_WIDTH), lambda b, r: (b, 0, 0)),
                  pl.BlockSpec((1, ATT_WIDTH, past), lambda b, r: (b, 0, 0)),
                  _const_spec(bias.shape)],
        out_specs=row_spec,
        out_shape=jax.ShapeDtypeStruct((n_batch * seq_len, ATT_WIDTH), BF16),
        scratch_shapes=[pltpu.VMEM((n_groups, n_win * GRID_W, gw), BF16)],
        compiler_params=_params(2),
        name="nbr_attn",
    )(q, k, vt, k_ctx, vt_ctx, bias)


def _mlstm_kernel(*refs, chunk, n_chunks, has_state):
    dh = ML_HEAD_DIM
    nrep = 2 * GATE_ROWS
    if has_state:
        (q_ref, k_ref, vt_ref, mot_ref, gtt_ref, g_ref, c0_ref, n0_ref, m0_ref,
         o_ref, c_ref, n_ref, m_ref, hf_ref, hb_ref, cn_ref, mrun_ref) = refs
        for d in range(2):
            for hh in range(N_ML_HEADS):
                cn_ref[d, hh, 0:dh, :] = c0_ref[0, d, hh].T
                cn_ref[d, hh, dh:dh + nrep, :] = jnp.broadcast_to(n0_ref[0, d, hh], (nrep, dh))
        mrun_ref[...] = m0_ref[0]
    else:
        (q_ref, k_ref, vt_ref, mot_ref, gtt_ref, g_ref,
         o_ref, c_ref, n_ref, m_ref, hf_ref, hb_ref, cn_ref, mrun_ref) = refs
        cn_ref[...] = jnp.zeros_like(cn_ref)
        mrun_ref[...] = jnp.zeros_like(mrun_ref)
    use_state = has_state or n_chunks > 1

    i0 = lax.broadcasted_iota(jnp.int32, (chunk, chunk), 0)
    i1 = lax.broadcasted_iota(jnp.int32, (chunk, chunk), 1)
    row_id = lax.broadcasted_iota(jnp.int32, (GATE_ROWS, 1), 0)
    instances = [(hh, d) for d in range(2) for hh in range(N_ML_HEADS)]

    le = [i0 <= i1, i0 >= i1]

    def split3(x):
        hi = x.astype(BF16).astype(F32)
        r1 = x - hi
        mid = r1.astype(BF16).astype(F32)
        return jnp.concatenate([hi, mid, r1 - mid], axis=0).astype(BF16)

    def gate_rows(gi, lf_pre):
        n_rows = gi.shape[0]
        lf = _log_sigmoid(lf_pre)
        b_last = jnp.sum(lf, axis=1, keepdims=True)
        terms = split3(lf)
        per_dir = []
        for d in range(2):
            tri = jnp.where(le[d], 1.0, 0.0).astype(BF16)
            parts = _dot(terms, tri)
            cs = parts[0:n_rows] + parts[n_rows:2 * n_rows] + parts[2 * n_rows:3 * n_rows]
            w_end = b_last + gi - cs
            pad = jnp.zeros((LANES - n_rows, chunk), F32)
            per_dir.append((cs, w_end, jnp.max(w_end, axis=1, keepdims=True),
                            jnp.concatenate([gi - cs, pad], axis=0).T))
        return b_last, per_dir

    def chunk_rows(lo):
        return jnp.concatenate([gtt_ref[lo:lo + GATE_ROWS, c * chunk:(c + 1) * chunk] for c in range(n_chunks)],
                               axis=0)

    def body(i, gates):
        chunks = (i, n_chunks - 1 - i)
        rows = [pl.ds(pl.multiple_of(c * chunk, chunk), chunk) for c in chunks]
        m_prev = mrun_ref[...][:, 0:1]

        def qkv(hh, d):
            sl = slice(hh * dh, (hh + 1) * dh)
            return q_ref[rows[d], sl], k_ref[rows[d], sl], vt_ref[sl, rows[d]]

        st = [_dot_nt(qkv(hh, d)[1], qkv(hh, d)[0]) for hh, d in instances]
        if use_state:
            cn = [cn_ref[d, hh] for hh, d in instances]
            qct = [_dot_nt(cn[n].astype(BF16), qkv(hh, d)[0]) for n, (hh, d) in enumerate(instances)]

        if gates is None:
            gates = gate_rows(*[jnp.concatenate([gtt_ref[lo:lo + GATE_ROWS, r] for r in rows], axis=0)
                                for lo in (0, GATE_ROWS)])
            row0 = [0, GATE_ROWS]
        else:
            row0 = [GATE_ROWS * c for c in chunks]
        b_last_all, per_dir = gates
        a_col, cs8, w8, decay8, m_new = [], [], [], [], []
        for d in range(2):
            r = slice(row0[d], row0[d] + GATE_ROWS)
            cs_all, w_end_all, w_max_all, a_col_d = per_dir[d]
            b_last = b_last_all[r]
            m_d = jnp.maximum(b_last + m_prev, w_max_all[r])
            cs8.append(cs_all[r])
            decay8.append(jnp.exp(b_last + m_prev - m_d))
            w8.append(jnp.exp(w_end_all[r] - m_d))
            m_new.append(m_d)
            a_col.append(a_col_d)
        m_next = jnp.where((row_id & 1) == 0, m_new[0], m_new[1])
        mrun_ref[...] = jnp.broadcast_to(m_next, (GATE_ROWS, LANES))

        pt_all, g_all, den_all = [], [], []
        for n, (hh, d) in enumerate(instances):
            j = 2 * hh + d
            lane = row0[d] + j
            a = jnp.where(le[d], a_col[d][:, lane:lane + 1], -jnp.inf)
            g = jnp.maximum(jnp.max(a, axis=0, keepdims=True), m_prev[j:j + 1])
            pt = st[n] * jnp.exp(a - g)
            pt_all.append(pt)
            g_all.append(g)
            den_all.append(jnp.sum(pt, axis=0, keepdims=True))

        for n, (hh, d) in enumerate(instances):
            j = 2 * hh + d
            sl = slice(hh * dh, (hh + 1) * dh)
            _, k, vt = qkv(hh, d)
            g, den = g_all[n], den_all[n]
            num = _dot(vt, pt_all[n].astype(BF16))
            if use_state:
                w_inter = jnp.exp(m_prev[j:j + 1] - g)
                num = num + w_inter * qct[n][0:dh]
                den = den + w_inter * qct[n][dh:dh + 1]
            scale = 1.0 / jnp.maximum(jnp.abs(den), jnp.exp(-(cs8[d][j:j + 1] + g)))
            (hb_ref if d else hf_ref)[sl, rows[d]] = num * scale
            w_row = w8[d][j:j + 1]
            lhs = jnp.concatenate([(vt.astype(F32) * w_row).astype(BF16),
                                   jnp.broadcast_to(w_row, (nrep, chunk)).astype(BF16)], axis=0)
            upd = _dot(lhs, k)
            if use_state:
                upd = decay8[d][j:j + 1] * cn[n] + upd
            cn_ref[d, hh] = upd
        return 0

    def finish(c, _):
        rows = pl.ds(pl.multiple_of(c * chunk, chunk), chunk)
        for hh in range(N_ML_HEADS):
            sl = slice(hh * dh, (hh + 1) * dh)
            hs = hf_ref[sl, rows] + hb_ref[sl, rows]
            y = hs * lax.rsqrt(jnp.mean(hs * hs, axis=0, keepdims=True) + EPS) * g_ref[sl, :]
            y = y * jax.nn.sigmoid(mot_ref[sl, rows])
            o_ref[rows, sl] = y.T.astype(o_ref.dtype)
        return 0

    if n_chunks <= MLSTM_UNROLL:
        gates = gate_rows(chunk_rows(0), chunk_rows(GATE_ROWS))
        for c in range(n_chunks):
            body(c, gates)
        for c in range(n_chunks):
            finish(c, 0)
    else:
        lax.fori_loop(0, n_chunks, lambda i, _: body(i, None), 0)
        lax.fori_loop(0, n_chunks, finish, 0)
    for d in range(2):
        for hh in range(N_ML_HEADS):
            cn = cn_ref[d, hh]
            c_ref[0, d, hh] = cn[0:dh].T
            n_ref[0, d, hh] = cn[dh:dh + 1]
    m_ref[0] = mrun_ref[...]


def _mlstm(mq, mk, mvt, mot, gates_t, ml_g, state, n_batch, seq_len, chunk):
    dh = ML_HEAD_DIM
    nh = N_ML_HEADS
    width = nh * dh
    has_state = state is not None
    seq = pl.BlockSpec((seq_len, width), lambda b: (b, 0))
    seq_t = pl.BlockSpec((width, seq_len), lambda b: (0, b))
    st_c = pl.BlockSpec((1, 2, nh, dh, dh), lambda b: (b, 0, 0, 0, 0))
    st_n = pl.BlockSpec((1, 2, nh, 1, dh), lambda b: (b, 0, 0, 0, 0))
    st_m = pl.BlockSpec((1, GATE_ROWS, LANES), lambda b: (b, 0, 0))
    in_specs = [seq, seq, seq_t, seq_t,
                pl.BlockSpec((2 * GATE_ROWS, seq_len), lambda b: (0, b)),
                _const_spec((width, chunk))]
    args = [mq, mk, mvt, mot, gates_t, jnp.broadcast_to(ml_g.reshape(width, 1), (width, chunk))]
    if has_state:
        in_specs += [st_c, st_n, st_m]
        args += list(state)
    out_shape = [jax.ShapeDtypeStruct((n_batch * seq_len, width), BF16),
                 jax.ShapeDtypeStruct((n_batch, 2, nh, dh, dh), F32),
                 jax.ShapeDtypeStruct((n_batch, 2, nh, 1, dh), F32),
                 jax.ShapeDtypeStruct((n_batch, GATE_ROWS, LANES), F32)]
    return pl.pallas_call(
        functools.partial(_mlstm_kernel, chunk=chunk, n_chunks=seq_len // chunk, has_state=has_state),
        grid=(n_batch,),
        in_specs=in_specs,
        out_specs=[seq, st_c, st_n, st_m],
        out_shape=out_shape,
        scratch_shapes=[pltpu.VMEM((width, seq_len), F32), pltpu.VMEM((width, seq_len), F32),
                        pltpu.VMEM((2, nh, dh + 2 * GATE_ROWS, dh), F32), pltpu.VMEM((GATE_ROWS, LANES), F32)],
        compiler_params=_params(1),
        name="mlstm_latent" if has_state else "mlstm_ctx",
    )(*args)


def _pack_gate_rows(m):
    b = m.shape[0]
    packed = m.transpose(0, 2, 1).reshape(b, 2 * N_ML_HEADS, 1)
    return jnp.broadcast_to(packed, (b, GATE_ROWS, LANES))


def _unpack_gate_rows(m):
    b = m.shape[0]
    return m[:, :, 0].reshape(b, N_ML_HEADS, 2).transpose(0, 2, 1)


def _inproj_ctxmix_kernel(*refs, n_in, n_out, n_seq, seq_len, chunk):
    ip_in = refs[:n_in]
    q_ref, k_ref, v_ref, mq_ref, mk_ref, mvt_ref, mot_ref, gtt_ref, g_ref = refs[n_in:n_in + 9]
    outs = refs[n_in + 9:]
    ip_out = outs[:n_out]
    oatt_ref, oml_ref, c_ref, n_ref, m_ref = outs[n_out:n_out + 5]
    scratch = outs[n_out + 5:]
    _inproj_kernel(*ip_in, *ip_out, latent=True)
    for s in range(n_seq):
        rows = slice(s * seq_len, (s + 1) * seq_len)
        _ctx_attn_kernel(q_ref.at[rows, :], k_ref.at[rows, :], v_ref.at[rows, :], oatt_ref.at[rows, :])
        _mlstm_kernel(mq_ref.at[rows, :], mk_ref.at[rows, :], mvt_ref.at[:, rows], mot_ref.at[:, rows],
                      gtt_ref.at[:, rows], g_ref, oml_ref.at[rows, :], c_ref.at[s:s + 1], n_ref.at[s:s + 1],
                      m_ref.at[s:s + 1], *scratch[4 * s:4 * s + 4],
                      chunk=chunk, n_chunks=seq_len // chunk, has_state=False)


def _latent_in_projection_with_ctx_mixers(x, mod, mod_row, seq_len, wts, rope, ctx, n_ctx_batch, ctx_seq):
    n_tiles, in_specs, args, out_shape, out_specs, scratch = _in_projection_parts(
        x, mod, mod_row, seq_len, wts, rope, latent=True)
    n_seq = n_ctx_batch // n_tiles
    rows = n_seq * ctx_seq
    n_ctx_tok = n_ctx_batch * ctx_seq
    chunk = min(ML_CHUNK, ctx_seq)
    nh, dh = N_ML_HEADS, ML_HEAD_DIM

    def tok(width):
        return pl.BlockSpec((rows, width), lambda i: (i, 0))

    def tok_t(height):
        return pl.BlockSpec((height, rows), lambda i: (0, i))

    def per_seq(*tail):
        return pl.BlockSpec((n_seq,) + tail, lambda i: (i,) + (0,) * len(tail))

    aq, ak, av, mq, mk, mvt, mot, gtt = ctx
    mix_in_specs = ([tok(ATT_WIDTH)] * 3 + [tok(ML_WIDTH)] * 2 + [tok_t(ML_WIDTH)] * 2
                    + [tok_t(2 * GATE_ROWS), _const_spec((ML_WIDTH, chunk))])
    mix_args = [aq, ak, av, mq, mk, mvt, mot, gtt,
                jnp.broadcast_to(wts["ml_g"].reshape(ML_WIDTH, 1), (ML_WIDTH, chunk))]
    mix_out_shape = [jax.ShapeDtypeStruct((n_ctx_tok, ATT_WIDTH), BF16),
                     jax.ShapeDtypeStruct((n_ctx_tok, ML_WIDTH), BF16),
                     jax.ShapeDtypeStruct((n_ctx_batch, 2, nh, dh, dh), F32),
                     jax.ShapeDtypeStruct((n_ctx_batch, 2, nh, 1, dh), F32),
                     jax.ShapeDtypeStruct((n_ctx_batch, GATE_ROWS, LANES), F32)]
    mix_out_specs = [tok(ATT_WIDTH), tok(ML_WIDTH), per_seq(2, nh, dh, dh), per_seq(2, nh, 1, dh),
                     per_seq(GATE_ROWS, LANES)]
    mix_scratch = [pltpu.VMEM((ML_WIDTH, ctx_seq), F32), pltpu.VMEM((ML_WIDTH, ctx_seq), F32),
                   pltpu.VMEM((2, nh, dh + 2 * GATE_ROWS, dh), F32), pltpu.VMEM((GATE_ROWS, LANES), F32)] * n_seq
    outs = pl.pallas_call(
        functools.partial(_inproj_ctxmix_kernel, n_in=len(in_specs), n_out=len(out_specs), n_seq=n_seq,
                          seq_len=ctx_seq, chunk=chunk),
        grid=(n_tiles,),
        in_specs=in_specs + mix_in_specs,
        out_specs=out_specs + mix_out_specs,
        out_shape=out_shape + mix_out_shape,
        scratch_shapes=scratch + mix_scratch,
        compiler_params=_params(1),
        name="in_proj_latent_ctx_mixers",
    )(*args, *mix_args)
    return outs[:len(out_specs)], outs[len(out_specs):]


def _ffn_kernel(oa_ref, oap_ref, oan_ref, om_ref, omp_ref, omn_ref, x_ref, xp_ref, xn_ref, mod_ref, g2_ref,
                wout_ref, wup_ref, cw_ref, cb_ref, wd_ref,
                y_ref, oc_ref, lhs_ref, x1_ref, ug_ref, uv_ref, acc_ref, *, seq_len, n_col_tiles, side_work=None):
    tm, d = x_ref.shape
    aw = oa_ref.shape[1]

    def halo(next_ref, prev_ref):
        row = lax.broadcasted_iota(jnp.int32, next_ref.shape, 0)
        return jnp.where(row < HALO // 2, next_ref[...], prev_ref[...])

    oc_ref[0:tm, 0:aw] = oa_ref[...]
    oc_ref[0:tm, aw:] = om_ref[...]
    oc_ref[tm:tm + HALO, 0:aw] = halo(oan_ref, oap_ref)
    oc_ref[tm:tm + HALO, aw:] = halo(omn_ref, omp_ref)
    mod = mod_ref[0]
    g1 = mod[:, 2 * d:3 * d]
    sh2, sc2 = mod[:, 3 * d:4 * d], mod[:, 4 * d:5 * d]
    out = _dot(oc_ref[...], wout_ref[...])

    def norm2(x1):
        y = x1 * lax.rsqrt(jnp.mean(x1 * x1, axis=-1, keepdims=True) + EPS)
        return ((y * g2_ref[...]) * (1.0 + sc2) + sh2).astype(BF16)

    x1 = x_ref[...] + g1 * out[0:tm]
    x1_ref[...] = x1
    lhs_ref[0:tm, :] = norm2(x1)
    lhs_ref[tm:tm + HALO, :] = norm2(halo(xn_ref, xp_ref) + g1 * out[tm:tm + HALO])
    acc_ref[...] = jnp.zeros_like(acc_ref)
    sub = 8
    period = min(seq_len, tm)
    first_groups = sorted({r // sub for r in range(0, tm, period)})
    last_groups = sorted({(r + period - 1) // sub for r in range(0, tm, period)})
    tile0 = pl.program_id(0) * tm

    def zero_rows(x, groups, target):
        parts, at = [], 0
        for grp in groups:
            lo = grp * sub
            if lo > at:
                parts.append(x[at:lo])
            pos = (tile0 + lo + lax.broadcasted_iota(jnp.int32, (sub, 1), 0)) % seq_len
            parts.append(jnp.where(pos == target, 0.0, x[lo:lo + sub]))
            at = lo + sub
        if at < tm:
            parts.append(x[at:tm])
        return jnp.concatenate(parts, axis=0)

    def conv(u, cw, cb):
        prev = zero_rows(pltpu.roll(u, 1, 0)[0:tm], first_groups, 0)
        nxt = zero_rows(pltpu.roll(u, tm + HALO - 1, 0)[0:tm], last_groups, seq_len - 1)
        return prev * cw[0:1] + u[0:tm] * cw[1:2] + nxt * cw[2:3] + cb

    def up(j, slot):
        lhs = lhs_ref[...]
        ug_ref[slot] = _dot(lhs, wup_ref[j])
        uv_ref[slot] = _dot(lhs, wup_ref[n_col_tiles + j])

    def act(j, slot):
        gate = conv(ug_ref[slot], cw_ref[j], cb_ref[j])
        val = conv(uv_ref[slot], cw_ref[n_col_tiles + j], cb_ref[n_col_tiles + j])
        return ((gate * jax.nn.sigmoid(gate)) * val).astype(BF16)

    def stage(j, slot, next_up):
        a = act(j, slot)
        if next_up:
            up(j + 1, 1 - slot)
        return _dot(a, wd_ref[j])

    def pair(i, _):
        acc_ref[...] += stage(2 * i, 0, True)
        acc_ref[...] += stage(2 * i + 1, 1, True)
        return 0

    def finish(tail):
        g2 = mod_ref[0][:, 5 * d:6 * d]
        y_ref[...] = x1_ref[...] + g2 * (acc_ref[...] + tail)

    up(0, 0)
    if side_work:
        n_points = n_col_tiles + 1

        def hosted(point):
            for i, work in enumerate(side_work):
                if (i * n_points) // len(side_work) == point:
                    work()

        hosted(0)
        for j in range(n_col_tiles - 1):
            acc_ref[...] += stage(j, j % 2, True)
            hosted(j + 1)
        finish(stage(n_col_tiles - 1, (n_col_tiles - 1) % 2, False))
        hosted(n_col_tiles)
        return
    n_pairs = (n_col_tiles - 1) // 2
    if n_pairs:
        lax.fori_loop(0, n_pairs, pair, 0)
    j0 = 2 * n_pairs
    if n_col_tiles - j0 == 2:
        acc_ref[...] += stage(j0, 0, True)
        finish(stage(j0 + 1, 1, False))
    else:
        finish(stage(j0, 0, False))


def _mix_ffn_parts(o_att, o_ml, x, mod, mod_row, seq_len, wts):
    n_tok, d = x.shape
    tm = _token_tile(n_tok, seq_len)
    hpt = tm // HALO
    n_halo = n_tok // HALO
    nct, tn, _ = wts["w_down"].shape
    mix = o_att.shape[1] + o_ml.shape[1]

    def tile_and_halos(width):
        return [pl.BlockSpec((tm, width), lambda i: (i, 0)),
                pl.BlockSpec((HALO, width), lambda i: (jnp.maximum(i * hpt - 1, 0), 0)),
                pl.BlockSpec((HALO, width), lambda i: (jnp.minimum((i + 1) * hpt, n_halo - 1), 0))]

    in_specs = (tile_and_halos(o_att.shape[1]) + tile_and_halos(o_ml.shape[1]) + tile_and_halos(d) + [
        pl.BlockSpec((1, 1, mod.shape[-1]), lambda i: (mod_row(i, tm), 0, 0)),
        _const_spec((1, d)), _const_spec((mix, d)),
        _const_spec((2 * nct, d, tn)), _const_spec((2 * nct, 3, tn)), _const_spec((2 * nct, 1, tn)),
        _const_spec((nct, tn, d))])
    args = [o_att, o_att, o_att, o_ml, o_ml, o_ml, x, x, x, mod, wts["norm2_g"], wts["w_out"],
            wts["w_up"], wts["conv_w"], wts["conv_b"], wts["w_down"]]
    scratch = [pltpu.VMEM((tm + HALO, mix), BF16), pltpu.VMEM((tm + HALO, d), BF16), pltpu.VMEM((tm, d), F32),
               pltpu.VMEM((2, tm + HALO, tn), F32), pltpu.VMEM((2, tm + HALO, tn), F32), pltpu.VMEM((tm, d), F32)]
    return (n_tok // tm, nct, in_specs, args, pl.BlockSpec((tm, d), lambda i: (i, 0)),
            jax.ShapeDtypeStruct((n_tok, d), F32), scratch)


def _mix_ffn(o_att, o_ml, x, mod, mod_row, seq_len, wts):
    n_tiles, nct, in_specs, args, out_spec, out_shape, scratch = _mix_ffn_parts(
        o_att, o_ml, x, mod, mod_row, seq_len, wts)
    return pl.pallas_call(
        functools.partial(_ffn_kernel, seq_len=seq_len, n_col_tiles=nct),
        grid=(n_tiles,),
        in_specs=in_specs,
        out_specs=out_spec,
        out_shape=out_shape,
        scratch_shapes=scratch,
        compiler_params=_params(1),
        name="mix_ffn",
    )(*args)


def _ffn_nbr_kernel(*refs, n_in, seq_len, n_col_tiles, rows, kr, rows_per_tile, tiles_per_batch):
    ffn_in = refs[:n_in]
    q_ref, k_ref, vt_ref, kc_ref, vct_ref, bias_ref = refs[n_in:n_in + 6]
    y_ref, o_ref = refs[n_in + 6:n_in + 8]
    scratch = refs[n_in + 8:]
    ffn_scratch, pt_refs = scratch[:-2], scratch[-2:]
    base = (pl.program_id(0) % tiles_per_batch) * rows_per_tile

    def row(local, pt_ref):
        rr = pl.ds(pl.multiple_of(local * GRID_W, GRID_W), GRID_W)
        _nbr_row(base + local, q_ref.at[rr, :], k_ref, vt_ref, kc_ref, vct_ref, bias_ref, o_ref.at[rr, :], pt_ref,
                 rows=rows, kr=kr)

    side_work = [functools.partial(row, local, pt_refs[local % 2]) for local in range(rows_per_tile)]
    _ffn_kernel(*ffn_in, y_ref, *ffn_scratch, seq_len=seq_len, n_col_tiles=n_col_tiles, side_work=side_work)


def _mix_ffn_with_nbr_attention(o_att, o_ml, x, mod, mod_row, seq_len, wts, q, k, vt, k_ctx, vt_ctx, rpb,
                                n_batch, nbr_seq):
    n_tiles, nct, in_specs, args, out_spec, out_shape, scratch = _mix_ffn_parts(
        o_att, o_ml, x, mod, mod_row, seq_len, wts)
    rows = nbr_seq // GRID_W
    kr = min(WIN_ROWS, rows)
    n_win = kr + 2
    assert rows >= n_win and (rows - n_win) % 2 == 0
    rpt = (n_batch * rows) // n_tiles
    tpb = n_tiles // n_batch
    bias = _nbr_bias_table(rpb)
    past = k_ctx.shape[1]
    gw = HEAD_GROUP * ATT_HEAD_DIM
    n_groups = N_ATT_HEADS // HEAD_GROUP
    row_blk = pl.BlockSpec((rpt * GRID_W, ATT_WIDTH), lambda i: (i, 0))
    once = pl.Buffered(1)
    nbr_specs = [row_blk,
                 pl.BlockSpec((nbr_seq, ATT_WIDTH), lambda i: (i // tpb, 0), pipeline_mode=once),
                 pl.BlockSpec((ATT_WIDTH, nbr_seq), lambda i: (0, i // tpb), pipeline_mode=once),
                 pl.BlockSpec((1, past, ATT_WIDTH), lambda i: (i // tpb, 0, 0), pipeline_mode=once),
                 pl.BlockSpec((1, ATT_WIDTH, past), lambda i: (i // tpb, 0, 0), pipeline_mode=once),
                 _const_spec(bias.shape)]
    pt = pltpu.VMEM((n_groups, n_win * GRID_W, gw), BF16)
    y, o_lat = pl.pallas_call(
        functools.partial(_ffn_nbr_kernel, n_in=len(in_specs), seq_len=seq_len, n_col_tiles=nct, rows=rows, kr=kr,
                          rows_per_tile=rpt, tiles_per_batch=tpb),
        grid=(n_tiles,),
        in_specs=in_specs + nbr_specs,
        out_specs=[out_spec, row_blk],
        out_shape=[out_shape, jax.ShapeDtypeStruct((n_batch * nbr_seq, ATT_WIDTH), BF16)],
        scratch_shapes=scratch + [pt, pt],
        compiler_params=_params(1),
        name="mix_ffn_ctx_nbr_attn",
    )(*args, q, k, vt, k_ctx, vt_ctx, bias)
    return y, o_lat


def _retile_cast_kernel(w_ref, o_ref):
    tn = o_ref.shape[-1]
    for t in range(o_ref.shape[0]):
        o_ref[t] = w_ref[:, t * tn:(t + 1) * tn].astype(o_ref.dtype)


def _retile_cast(w, tn):
    r, c = w.shape
    group = 2 if (c // tn) % 2 == 0 else 1
    return pl.pallas_call(
        _retile_cast_kernel,
        grid=(c // (tn * group),),
        in_specs=[pl.BlockSpec((r, tn * group), lambda j: (0, j))],
        out_specs=pl.BlockSpec((group, r, tn), lambda j: (j, 0, 0)),
        out_shape=jax.ShapeDtypeStruct((c // tn, r, tn), BF16),
        compiler_params=_params(1),
        name="retile_cast",
    )(w)


def _cast_kernel(w_ref, o_ref):
    o_ref[...] = w_ref[...].astype(o_ref.dtype)


def _cast_rows(w, block_rows):
    r, c = w.shape
    spec = pl.BlockSpec((block_rows, c), lambda i: (i, 0))
    return pl.pallas_call(
        _cast_kernel,
        grid=(r // block_rows,),
        in_specs=[spec],
        out_specs=spec,
        out_shape=jax.ShapeDtypeStruct((r, c), BF16),
        compiler_params=_params(1),
        name="cast_rows",
    )(w)


def _layer_weights(norm1_g, norm2_g, w_in, b_gate, q_g, k_g, ml_g, w_out, w_up, conv_w, conv_b, w_down):
    d = w_in.shape[0]
    a, m, nh = ATT_WIDTH, ML_WIDTH, N_ML_HEADS
    def gate_lanes(g):
        g = g.reshape(g.shape[0], 2, 2, nh).transpose(0, 2, 3, 1).reshape(g.shape[0], 4 * nh)
        return jnp.pad(g, ((0, 0), (0, LANES - 4 * nh)))

    w_g = gate_lanes(w_in[:, 3 * a + 4 * m:])
    b_g = gate_lanes(b_gate.astype(F32).reshape(1, 4 * nh))
    head_id = np.arange(a) // ATT_HEAD_DIM
    d_ff = w_down.shape[0]
    tn = MXU_WIDTH
    nct = d_ff // tn

    def col_tiles(w):
        return w.reshape(w.shape[0], 2 * nct, tn).transpose(1, 0, 2)

    return {
        "norm1_g": norm1_g.reshape(1, d), "norm2_g": norm2_g.reshape(1, d),
        "w_in": w_in.T[:3 * a + 4 * m].astype(BF16),
        "w_g": w_g.astype(BF16), "b_g": b_g,
        "q_g": jnp.tile(q_g, N_ATT_HEADS).reshape(1, a), "k_g": jnp.tile(k_g, N_ATT_HEADS).reshape(1, a),
        "blockdiag": jnp.asarray(head_id[:, None] == head_id[None, :], BF16),
        "ml_g": ml_g.reshape(1, m),
        "w_out": w_out.astype(BF16),
        "conv_w": col_tiles(conv_w.astype(F32)), "conv_b": col_tiles(conv_b.astype(F32).reshape(1, -1)),
        "w_up_f32": w_up, "w_down_f32": w_down, "ffn_tile": tn,
    }


def _rope_tables(seq_len):
    quarter = ML_HEAD_DIM // 4
    pos = np.arange(seq_len)
    inv_freq = ROPE_THETA ** (-np.arange(quarter, dtype=np.float64) / quarter)
    ang_r = (pos // GRID_W).astype(np.float64)[:, None] * inv_freq[None, :]
    ang_c = (pos % GRID_W).astype(np.float64)[:, None] * inv_freq[None, :]
    cos = np.concatenate([np.cos(ang_r)] * 2 + [np.cos(ang_c)] * 2, axis=-1)
    sin = np.concatenate([-np.sin(ang_r), np.sin(ang_r), -np.sin(ang_c), np.sin(ang_c)], axis=-1)
    return jnp.asarray(cos, F32), jnp.asarray(sin, F32)


def _layer(xp, xs, mod, wts, rope, ctx_kv, rpb, state, batch, seq, dec_batch, dec_seq):
    def ctx_row(i, tm):
        return 0 * i

    def lat_row(i, tm):
        return 1 + (i * tm) // dec_seq

    w_up, w_down, tn = wts["w_up_f32"], wts["w_down_f32"], wts["ffn_tile"]
    n_ctx_tiles = (batch * seq) // _token_tile(batch * seq, seq)
    if n_ctx_tiles >= max(w_up.shape[1] // (2 * tn), w_down.shape[0] // tn) and (w_up.shape[1] // tn) % 2 == 0:
        ctx, w_up_t, w_down_t = _ctx_in_projection_with_weight_casts(xp, mod, ctx_row, seq, wts, w_up, w_down, tn)
    else:
        ctx = _in_projection(xp, mod, ctx_row, seq, wts, None, latent=False)
        w_up_t = _retile_cast(w_up, tn)
        w_down_t = _cast_rows(w_down, tn).reshape(w_down.shape[0] // tn, tn, w_down.shape[1])
    wts = dict(wts, w_up=w_up_t, w_down=w_down_t)
    cache_k, cache_v = ctx[8], ctx[9]
    n_lat_tiles = (dec_batch * dec_seq) // _token_tile(dec_batch * dec_seq, dec_seq)
    if batch % n_lat_tiles == 0:
        lat, (o_att_c, o_ml_c, c_f, n_f, m_f) = _latent_in_projection_with_ctx_mixers(
            xs, mod, lat_row, dec_seq, wts, rope, ctx[:8], batch, seq)
    else:
        aq, ak, av, mq, mk, mvt, mot, gtt = ctx[:8]
        o_att_c = _context_attention(aq, ak, av, seq)
        o_ml_c, c_f, n_f, m_f = _mlstm(mq, mk, mvt, mot, gtt, wts["ml_g"], None, batch, seq, min(ML_CHUNK, seq))
        lat = _in_projection(xs, mod, lat_row, dec_seq, wts, rope, latent=True)
    aq, ak, avt, mq, mk, mvt, mot, gtt = lat[:8]
    n_rows = dec_batch * (dec_seq // GRID_W)
    if n_rows % n_ctx_tiles == 0 and n_ctx_tiles % dec_batch == 0:
        xp, o_att = _mix_ffn_with_nbr_attention(o_att_c, o_ml_c, xp, mod, ctx_row, seq, wts, aq, ak, avt,
                                                ctx_kv[0], ctx_kv[1], rpb, dec_batch, dec_seq)
    else:
        xp = _mix_ffn(o_att_c, o_ml_c, xp, mod, ctx_row, seq, wts)
        o_att = _neighborhood_attention(aq, ak, avt, ctx_kv[0], ctx_kv[1], rpb, dec_batch, dec_seq)
    o_ml = _mlstm(mq, mk, mvt, mot, gtt, wts["ml_g"], state, dec_batch, dec_seq, min(ML_CHUNK, dec_seq))[0]
    xs = _mix_ffn(o_att, o_ml, xs, mod, lat_row, dec_seq, wts)
    return xp, xs, cache_k, cache_v, (c_f, n_f[:, :, :, 0, :], _unpack_gate_rows(m_f))


def kernel(x_prompt, x_sample, cache_k, cache_v, state_C, state_n, state_m, c, c_ctx, w_mod, b_mod, norm1_g,
           norm2_g, w_in, b_gate, q_norm_g, k_norm_g, rpb, ml_norm_g, w_out, w_up, conv_w, conv_b, w_down):
    batch, seq, d = x_prompt.shape
    dec_batch, dec_seq, _ = x_sample.shape
    depth = w_mod.shape[0]
    past = cache_k.shape[2]
    cvecs = jnp.concatenate([c_ctx[None, :], c], axis=0).astype(F32)
    rope = _rope_tables(dec_seq)

    xp = x_prompt.reshape(batch * seq, d)
    xs = x_sample.reshape(dec_batch * dec_seq, d)
    ks, vs, cs, ns, ms = [], [], [], [], []
    for l in range(depth):
        wts = _layer_weights(norm1_g[l], norm2_g[l], w_in[l], b_gate[l], q_norm_g[l], k_norm_g[l],
                             ml_norm_g[l], w_out[l], w_up[l], conv_w[l], conv_b[l], w_down[l])
        mod = _modulation(cvecs, w_mod[l], b_mod[l])

        state = (state_C[:, l].astype(F32),
                 state_n[:, l].astype(F32)[:, :, :, None, :],
                 _pack_gate_rows(state_m[:, l].astype(F32)))
        ctx_kv = (cache_k[:, l].reshape(dec_batch, past, ATT_WIDTH).astype(BF16),
                  cache_v[:, l].reshape(dec_batch, past, ATT_WIDTH).transpose(0, 2, 1).astype(BF16))
        xp, xs, ak, av, st = _layer(xp, xs, mod, wts, rope, ctx_kv, rpb[l], state, batch, seq, dec_batch, dec_seq)
        ks.append(ak.reshape(batch, seq, N_ATT_HEADS, ATT_HEAD_DIM))
        vs.append(av.reshape(batch, seq, N_ATT_HEADS, ATT_HEAD_DIM))
        cs.append(st[0])
        ns.append(st[1])
        ms.append(st[2])
    return (xp.reshape(batch, seq, d), xs.reshape(dec_batch, dec_seq, d),
            jnp.stack(ks, axis=1), jnp.stack(vs, axis=1),
            jnp.stack(cs, axis=1), jnp.stack(ns, axis=1), jnp.stack(ms, axis=1))
```

```python
import functools

import jax
import jax.numpy as jnp
import numpy as np
from jax import lax
from jax.experimental import pallas as pl
from jax.experimental.pallas import tpu as pltpu

F32 = jnp.float32
BF16 = jnp.bfloat16

GRID_W = 64
N_ATT_HEADS = 8
ATT_HEAD_DIM = 64
ATT_WIDTH = N_ATT_HEADS * ATT_HEAD_DIM
WIN_ROWS = 8
WIN_COLS = 16
HEAD_GROUP = 4
N_ML_HEADS = 4
ML_HEAD_DIM = 128
ML_WIDTH = N_ML_HEADS * ML_HEAD_DIM
ROPE_THETA = 10000.0
EPS = 1e-6

LANES = 128
MXU_WIDTH = 256
ML_CHUNK = 256
MLSTM_UNROLL = 8
GATE_ROWS = 8
HALO = 16
TOKEN_TILE = 512
VMEM_LIMIT = 56 * 1024 * 1024
X_RING = 3


def _dot(a, b):
    return jnp.dot(a, b, preferred_element_type=F32)


def _dot_nt(a, b):
    return lax.dot_general(a, b, (((1,), (1,)), ((), ())), preferred_element_type=F32)


def _dot_tn(a, b):
    return lax.dot_general(a, b, (((0,), (0,)), ((), ())), preferred_element_type=F32)


def _const_spec(shape):
    nd = len(shape)
    return pl.BlockSpec(shape, lambda *_: (0,) * nd, pipeline_mode=pl.Buffered(1))


def _params(n_axes):
    return pltpu.CompilerParams(dimension_semantics=("arbitrary",) * n_axes,
                                vmem_limit_bytes=VMEM_LIMIT)


def _token_tile(n_tok, seq_len):
    tm = min(TOKEN_TILE, n_tok)
    while n_tok % tm or (seq_len % tm and tm % seq_len):
        tm //= 2
    return tm


def _log_sigmoid(x):
    return jnp.minimum(x, 0.0) - jnp.log1p(jnp.exp(-jnp.abs(x)))


def _mod_kernel(c_ref, w_ref, b_ref, o_ref):
    c = c_ref[...]
    s = c * jax.nn.sigmoid(c)
    o_ref[...] = _dot(s.astype(BF16), w_ref[...].astype(BF16)) + b_ref[...]


def _modulation(cvecs, w_mod, b_mod):
    r, d = cvecs.shape
    n = w_mod.shape[1]
    tn = d
    out = pl.pallas_call(
        _mod_kernel,
        grid=(n // tn,),
        in_specs=[pl.BlockSpec((r, d), lambda j: (0, 0)),
                  pl.BlockSpec((d, tn), lambda j: (0, j)),
                  pl.BlockSpec((1, tn), lambda j: (0, j))],
        out_specs=pl.BlockSpec((r, tn), lambda j: (0, j)),
        out_shape=jax.ShapeDtypeStruct((r, n), F32),
        compiler_params=_params(1),
        name="adaln_mod",
    )(cvecs, w_mod, b_mod.reshape(1, n))
    return out.reshape(r, 1, n)


def _rope(x, cos, sin_signed):
    lane = lax.broadcasted_iota(jnp.int32, x.shape, 1)
    partner = jnp.where((lane & 32) == 0, pltpu.roll(x, LANES - 32, 1), pltpu.roll(x, 32, 1))
    return x * cos + partner * sin_signed


def _inproj_kernel(*refs, latent):
    if latent:
        (x_ref, mod_ref, g1_ref, win_ref, wg_ref, bg_ref, qg_ref, kg_ref, bd_ref, cos_ref, sin_ref,
         aq_ref, ak_ref, av_ref, mq_ref, mk_ref, mv_ref, mo_ref, gtt_ref) = refs
    else:
        (x_ref, mod_ref, g1_ref, win_ref, wg_ref, bg_ref, qg_ref, kg_ref, bd_ref,
         aq_ref, ak_ref, av_ref, mq_ref, mk_ref, mv_ref, mo_ref, gtt_ref, ck_ref, cv_ref,
         cks_ref, cvs_ref) = refs
    x = x_ref[...]
    d = x.shape[-1]
    mod = mod_ref[0]
    sh1, sc1 = mod[:, 0:d], mod[:, d:2 * d]
    y = x * lax.rsqrt(jnp.mean(x * x, axis=-1, keepdims=True) + EPS)
    h = (y * g1_ref[...]) * (1.0 + sc1) + sh1
    hb = h.astype(BF16)

    def head_norm(a, g):
        ss = _dot((a * a).astype(BF16), bd_ref[...])
        return a * lax.rsqrt(ss * (1.0 / ATT_HEAD_DIM) + EPS) * g

    w = ATT_WIDTH
    att = _dot(hb, win_ref[:, 0:3 * w])
    aq_ref[...] = (head_norm(att[:, 0:w], qg_ref[...]) * ATT_HEAD_DIM ** -0.5).astype(aq_ref.dtype)
    kn = head_norm(att[:, w:2 * w], kg_ref[...])
    av = att[:, 2 * w:3 * w]
    ak_ref[...] = kn.astype(BF16)
    if latent:
        av_ref[...] = av.T.astype(BF16)
    else:
        av_ref[...] = av.astype(BF16)
        tm = kn.shape[0]
        for src, dst, scr in ((kn, ck_ref, cks_ref), (av, cv_ref, cvs_ref)):
            for hh in range(N_ATT_HEADS):
                pair = src[:, (hh // 2) * LANES:(hh // 2 + 1) * LANES]
                if hh % 2:
                    pair = pltpu.roll(pair, ATT_HEAD_DIM, 1)
                scr[pl.ds(hh, tm, stride=N_ATT_HEADS), :] = pair
            dst[...] = scr[...].reshape(tm, N_ATT_HEADS, LANES)[:, :, 0:ATT_HEAD_DIM]

    w = ML_WIDTH
    ml0 = 3 * ATT_WIDTH
    mq = _dot(hb, win_ref[:, ml0:ml0 + w])
    mk = _dot(hb, win_ref[:, ml0 + w:ml0 + 2 * w]) * ML_HEAD_DIM ** -0.5
    if latent:
        cos, sin = cos_ref[...], sin_ref[...]
        for hh in range(N_ML_HEADS):
            sl = slice(hh * ML_HEAD_DIM, (hh + 1) * ML_HEAD_DIM)
            mq_ref[:, sl] = _rope(mq[:, sl], cos, sin).astype(BF16)
            mk_ref[:, sl] = _rope(mk[:, sl], cos, sin).astype(BF16)
    else:
        mq_ref[...] = mq.astype(BF16)
        mk_ref[...] = mk.astype(BF16)
    mv_ref[...] = _dot(hb, win_ref[:, ml0 + 2 * w:ml0 + 3 * w]).T.astype(BF16)
    mo_ref[...] = _dot(hb, win_ref[:, ml0 + 3 * w:ml0 + 4 * w]).T
    gates = _dot(hb, wg_ref[...]) + bg_ref[...]
    gtt_ref[...] = gates.T[0:2 * GATE_ROWS]


def _in_projection_parts(x, mod, mod_row, seq_len, wts, rope, *, latent):
    n_tok, d = x.shape
    tm = _token_tile(n_tok, seq_len)
    tiles_per_seq = max(seq_len // tm, 1)

    def tok(width):
        return pl.BlockSpec((tm, width), lambda i: (i, 0))

    in_specs = [tok(d),
                pl.BlockSpec((1, 1, mod.shape[-1]), lambda i: (mod_row(i, tm), 0, 0)),
                _const_spec((1, d)),
                _const_spec(wts["w_in"].shape),
                _const_spec(wts["w_g"].shape), _const_spec((1, LANES)),
                _const_spec((1, ATT_WIDTH)), _const_spec((1, ATT_WIDTH)),
                _const_spec((ATT_WIDTH, ATT_WIDTH))]
    args = [x, mod, wts["norm1_g"], wts["w_in"], wts["w_g"], wts["b_g"],
            wts["q_g"], wts["k_g"], wts["blockdiag"]]
    if latent:
        in_specs += [pl.BlockSpec((tm, LANES), lambda i: (i % tiles_per_seq, 0))] * 2
        args += [rope[0], rope[1]]
    out_shape = [jax.ShapeDtypeStruct((n_tok, ATT_WIDTH), BF16),
                 jax.ShapeDtypeStruct((n_tok, ATT_WIDTH), BF16),
                 jax.ShapeDtypeStruct((ATT_WIDTH, n_tok) if latent else (n_tok, ATT_WIDTH), BF16),
                 jax.ShapeDtypeStruct((n_tok, ML_WIDTH), BF16),
                 jax.ShapeDtypeStruct((n_tok, ML_WIDTH), BF16),
                 jax.ShapeDtypeStruct((ML_WIDTH, n_tok), BF16),
                 jax.ShapeDtypeStruct((ML_WIDTH, n_tok), F32),
                 jax.ShapeDtypeStruct((2 * GATE_ROWS, n_tok), F32)]

    def tok_t(height):
        return pl.BlockSpec((height, tm), lambda i: (0, i))

    av_spec = tok_t(ATT_WIDTH) if latent else tok(ATT_WIDTH)
    out_specs = ([tok(ATT_WIDTH)] * 2 + [av_spec] + [tok(ML_WIDTH)] * 2 + [tok_t(ML_WIDTH)] * 2
                 + [tok_t(2 * GATE_ROWS)])
    if not latent:
        cache = jax.ShapeDtypeStruct((n_tok, N_ATT_HEADS, ATT_HEAD_DIM), F32)
        out_shape += [cache, cache]
        out_specs += [pl.BlockSpec((tm, N_ATT_HEADS, ATT_HEAD_DIM), lambda i: (i, 0, 0))] * 2
    scratch = [] if latent else [pltpu.VMEM((tm * N_ATT_HEADS, LANES), F32)] * 2
    return n_tok // tm, in_specs, args, out_shape, out_specs, scratch


def _in_projection(x, mod, mod_row, seq_len, wts, rope, *, latent):
    n_tiles, in_specs, args, out_shape, out_specs, scratch = _in_projection_parts(
        x, mod, mod_row, seq_len, wts, rope, latent=latent)
    return pl.pallas_call(
        functools.partial(_inproj_kernel, latent=latent),
        grid=(n_tiles,),
        in_specs=in_specs,
        out_specs=out_specs,
        out_shape=out_shape,
        scratch_shapes=scratch,
        compiler_params=_params(1),
        name="in_proj_latent" if latent else "in_proj_ctx",
    )(*args)


def _inproj_cast_kernel(*refs, n_in, n_out, n_tiles):
    (x_hbm, *ip_in), (wup_ref, wdn_ref) = refs[:n_in], refs[n_in:n_in + 2]
    outs = refs[n_in + 2:]
    ip_out, (wup_o, wdn_o), scratch = outs[:n_out], outs[n_out:n_out + 2], outs[n_out + 2:]
    *scratch, xbuf, sem = scratch
    i = pl.program_id(0)
    tm = xbuf.shape[1]

    def x_copy(step):
        slot = step % X_RING
        return pltpu.make_async_copy(x_hbm.at[pl.ds(pl.multiple_of(step * tm, tm), tm), :], xbuf.at[slot],
                                     sem.at[slot])

    @pl.when(i == 0)
    def _():
        for step in range(min(X_RING - 1, n_tiles)):
            x_copy(step).start()

    @pl.when(i + (X_RING - 1) < n_tiles)
    def _():
        x_copy(i + (X_RING - 1)).start()

    x_copy(i).wait()
    _inproj_kernel(xbuf.at[i % X_RING], *ip_in, *ip_out, *scratch, latent=False)
    _retile_cast_kernel(wup_ref, wup_o)
    wdn_o[0] = wdn_ref[...].astype(wdn_o.dtype)


def _ctx_in_projection_with_weight_casts(x, mod, mod_row, seq_len, wts, w_up, w_down, tn):
    n_tiles, in_specs, args, out_shape, out_specs, scratch = _in_projection_parts(
        x, mod, mod_row, seq_len, wts, None, latent=False)
    d, two_dff = w_up.shape
    nct = w_down.shape[0] // tn
    n_grp = two_dff // (2 * tn)
    assert two_dff % (2 * tn) == 0 and max(n_grp, nct) <= n_tiles
    cast_in = [pl.BlockSpec((d, 2 * tn), lambda i: (0, jnp.minimum(i, n_grp - 1))),
               pl.BlockSpec((tn, d), lambda i: (jnp.minimum(i, nct - 1), 0))]
    cast_out = [pl.BlockSpec((2, d, tn), lambda i: (jnp.minimum(i, n_grp - 1), 0, 0)),
                pl.BlockSpec((1, tn, d), lambda i: (jnp.minimum(i, nct - 1), 0, 0))]
    cast_shape = [jax.ShapeDtypeStruct((two_dff // tn, d, tn), BF16), jax.ShapeDtypeStruct((nct, tn, d), BF16)]
    tm = x.shape[0] // n_tiles
    in_specs = [pl.BlockSpec(memory_space=pl.ANY)] + in_specs[1:]
    scratch = scratch + [pltpu.VMEM((X_RING, tm, d), F32), pltpu.SemaphoreType.DMA((X_RING,))]
    outs = pl.pallas_call(
        functools.partial(_inproj_cast_kernel, n_in=len(in_specs), n_out=len(out_specs), n_tiles=n_tiles),
        grid=(n_tiles,),
        in_specs=in_specs + cast_in,
        out_specs=out_specs + cast_out,
        out_shape=out_shape + cast_shape,
        scratch_shapes=scratch,
        compiler_params=_params(1),
        name="in_proj_ctx_weight_casts",
    )(*args, w_up, w_down)
    return outs[:len(out_specs)], outs[len(out_specs)], outs[len(out_specs) + 1]


def _ctx_attn_kernel(q_ref, k_ref, v_ref, o_ref):
    n = q_ref.shape[0]
    gw = HEAD_GROUP * ATT_HEAD_DIM
    lane_head = lax.broadcasted_iota(jnp.int32, (n, gw), 1) // ATT_HEAD_DIM
    for g in range(N_ATT_HEADS // HEAD_GROUP):
        sl = slice(g * gw, (g + 1) * gw)
        q4 = q_ref[:, sl]
        qbd = jnp.concatenate([jnp.where(lane_head == hl, q4, jnp.zeros_like(q4))
                               for hl in range(HEAD_GROUP)], axis=0)
        s = _dot_nt(k_ref[:, sl], qbd)
        p = jnp.exp(s - jnp.max(s, axis=0, keepdims=True))
        p = p / jnp.sum(p, axis=0, keepdims=True)
        o4 = _dot_tn(p.astype(BF16), v_ref[:, sl])
        out = jnp.where(lane_head == 0, o4[0:n], 0.0)
        for hl in range(1, HEAD_GROUP):
            out = out + jnp.where(lane_head == hl, o4[hl * n:(hl + 1) * n], 0.0)
        o_ref[:, sl] = out.astype(o_ref.dtype)


def _context_attention(q, k, v, seq_len):
    n_tok = q.shape[0]
    spec = pl.BlockSpec((seq_len, ATT_WIDTH), lambda b: (b, 0))
    return pl.pallas_call(
        _ctx_attn_kernel,
        grid=(n_tok // seq_len,),
        in_specs=[spec, spec, spec],
        out_specs=spec,
        out_shape=jax.ShapeDtypeStruct((n_tok, ATT_WIDTH), BF16),
        compiler_params=_params(1),
        name="ctx_attn",
    )(q, k, v)


def _nbr_attn_kernel(q_ref, k_ref, vt_ref, kc_ref, vct_ref, bias_ref, o_ref, pt_ref, *, rows, kr):
    _nbr_row(pl.program_id(1), q_ref, k_ref, vt_ref, kc_ref, vct_ref, bias_ref, o_ref, pt_ref, rows=rows, kr=kr)


def _nbr_row(r, q_ref, k_ref, vt_ref, kc_ref, vct_ref, bias_ref, o_ref, pt_ref, *, rows, kr):
    n_win = kr + 2
    n_loc = kr * GRID_W
    gw = HEAD_GROUP * ATT_HEAD_DIM
    groups = N_ATT_HEADS // HEAD_GROUP
    lane_head = lax.broadcasted_iota(jnp.int32, (GRID_W, gw), 1) // ATT_HEAD_DIM
    rs = jnp.clip(r - kr // 2, 0, rows - kr)
    start = jnp.minimum(rs - (rs & 1), rows - n_win)
    delta = rs - start
    bias0 = pl.multiple_of((WIN_ROWS - 1 - (r - rs)) * GRID_W, GRID_W)
    zeros2 = jnp.zeros((2 * GRID_W, gw), BF16)
    for g in range(groups):
        sl = slice(g * gw, (g + 1) * gw)
        q4 = q_ref[:, sl]
        qbd = jnp.concatenate([jnp.where(lane_head == hl, q4, jnp.zeros_like(q4))
                               for hl in range(HEAD_GROUP)], axis=0)
        s_loc = _dot_nt(k_ref[pl.ds(pl.multiple_of(rs * GRID_W, GRID_W), n_loc), sl], qbd)
        s_loc = s_loc + bias_ref[g, pl.ds(bias0, n_loc), :]
        s_ctx = _dot_nt(kc_ref[0, :, sl], qbd)
        m = jnp.maximum(jnp.max(s_loc, axis=0, keepdims=True), jnp.max(s_ctx, axis=0, keepdims=True))
        p_loc = jnp.exp(s_loc - m)
        p_ctx = jnp.exp(s_ctx - m)
        l = jnp.sum(p_loc, axis=0, keepdims=True) + jnp.sum(p_ctx, axis=0, keepdims=True)
        pt_ref[g, 0:2 * GRID_W, :] = zeros2
        pt_ref[g, n_loc:n_loc + 2 * GRID_W, :] = zeros2
        pt_ref[g, pl.ds(pl.multiple_of(delta * GRID_W, GRID_W), n_loc), :] = p_loc.astype(BF16)
        vt_win = vt_ref[sl, pl.ds(pl.multiple_of(start * GRID_W, 2 * GRID_W), n_win * GRID_W)]
        ot = _dot(vt_win, pt_ref[g]) + _dot(vct_ref[0, sl, :], p_ctx.astype(BF16))
        o4 = (ot / l).T
        out = jnp.where(lane_head == 0, o4[0:GRID_W], 0.0)
        for hl in range(1, HEAD_GROUP):
            out = out + jnp.where(lane_head == hl, o4[hl * GRID_W:(hl + 1) * GRID_W], 0.0)
        o_ref[:, sl] = out.astype(o_ref.dtype)


def _nbr_bias_table(rpb):
    n_heads, n_dr, n_dc = rpb.shape
    c = WIN_COLS - 1
    assert n_heads == N_ATT_HEADS and n_dc == 2 * c + 1 and 2 * GRID_W == LANES
    f = rpb.astype(F32)
    packed = jnp.concatenate([f[..., c::-1], jnp.zeros((n_heads, n_dr, LANES - n_dc), F32), f[..., :c:-1]], axis=-1)
    n_groups = N_ATT_HEADS // HEAD_GROUP

    def sublane_groups(heads):
        return jnp.repeat(packed[heads], 8, axis=1).reshape(-1, LANES)

    return pl.pallas_call(
        functools.partial(_nbr_bias_kernel, n_dr=n_dr),
        out_shape=jax.ShapeDtypeStruct((n_groups, n_dr * GRID_W, HEAD_GROUP * GRID_W), F32),
        name="nbr_bias_table",
    )(sublane_groups(slice(0, None, 2)), sublane_groups(slice(1, None, 2)))


def _nbr_bias_kernel(even_ref, odd_ref, o_ref, *, n_dr):
    n_rows = even_ref.shape[0]
    s = lax.broadcasted_iota(jnp.int32, (n_rows, LANES), 0) % 8
    lane = lax.broadcasted_iota(jnp.int32, (n_rows, LANES), 1)
    cs = jnp.clip(lane % GRID_W - WIN_COLS // 2, 0, GRID_W - WIN_COLS)

    def sheared(ref):
        x = ref[...]
        for bit in (1, 2, 4):
            x = jnp.where((s & bit) != 0, pltpu.roll(x, bit, 1), x)
        return x

    even, odd = sheared(even_ref), sheared(odd_ref)
    for i in range(GRID_W // 8):
        key = 8 * i + s
        t = jnp.where(lane < GRID_W, pltpu.roll(even, 8 * i, 1) if i else even,
                      pltpu.roll(odd, (8 * i + GRID_W) % LANES, 1))
        t = jnp.where((key >= cs) & (key < cs + WIN_COLS), t, -jnp.inf)
        for p in range(n_rows // (8 * n_dr)):
            g, pair = divmod(p, HEAD_GROUP // 2)
            for dr in range(n_dr):
                r0 = (p * n_dr + dr) * 8
                o_ref[g, dr * GRID_W + 8 * i:dr * GRID_W + 8 * i + 8, pair * LANES:(pair + 1) * LANES] = t[r0:r0 + 8]


def _neighborhood_attention(q, k, vt, k_ctx, vt_ctx, rpb, n_batch, seq_len):
    rows = seq_len // GRID_W
    kr = min(WIN_ROWS, rows)
    n_win = kr + 2
    assert rows >= n_win and (rows - n_win) % 2 == 0
    bias = _nbr_bias_table(rpb)
    past = k_ctx.shape[1]
    gw = HEAD_GROUP * ATT_HEAD_DIM
    n_groups = N_ATT_HEADS // HEAD_GROUP
    row_spec = pl.BlockSpec((GRID_W, ATT_WIDTH), lambda b, r: (b * rows + r, 0))
    return pl.pallas_call(
        functools.partial(_nbr_attn_kernel, rows=rows, kr=kr),
        grid=(n_batch, rows),
        in_specs=[row_spec,
                  pl.BlockSpec((seq_len, ATT_WIDTH), lambda b, r: (b, 0)),
                  pl.BlockSpec((ATT_WIDTH, seq_len), lambda b, r: (0, b)),
                  pl.BlockSpec((1, past, ATT_WIDTH), lambda b, r: (b, 0, 0)),
                  pl.BlockSpec((1, ATT_WIDTH, past), lambda b, r: (b, 0, 0)),
                  _const_spec(bias.shape)],
        out_specs=row_spec,
        out_shape=jax.ShapeDtypeStruct((n_batch * seq_len, ATT_WIDTH), BF16),
        scratch_shapes=[pltpu.VMEM((n_groups, n_win * GRID_W, gw), BF16)],
        compiler_params=_params(2),
        name="nbr_attn",
    )(q, k, vt, k_ctx, vt_ctx, bias)


def _mlstm_kernel(*refs, chunk, n_chunks, has_state):
    dh = ML_HEAD_DIM
    nrep = 2 * GATE_ROWS
    if has_state:
        (q_ref, k_ref, vt_ref, mot_ref, gtt_ref, g_ref, c0_ref, n0_ref, m0_ref,
         o_ref, c_ref, n_ref, m_ref, hf_ref, hb_ref, cn_ref, mrun_ref) = refs
        for d in range(2):
            for hh in range(N_ML_HEADS):
                cn_ref[d, hh, 0:dh, :] = c0_ref[0, d, hh].T
                cn_ref[d, hh, dh:dh + nrep, :] = jnp.broadcast_to(n0_ref[0, d, hh], (nrep, dh))
        mrun_ref[...] = m0_ref[0]
    else:
        (q_ref, k_ref, vt_ref, mot_ref, gtt_ref, g_ref,
         o_ref, c_ref, n_ref, m_ref, hf_ref, hb_ref, cn_ref, mrun_ref) = refs
        cn_ref[...] = jnp.zeros_like(cn_ref)
        mrun_ref[...] = jnp.zeros_like(mrun_ref)
    use_state = has_state or n_chunks > 1

    i0 = lax.broadcasted_iota(jnp.int32, (chunk, chunk), 0)
    i1 = lax.broadcasted_iota(jnp.int32, (chunk, chunk), 1)
    row_id = lax.broadcasted_iota(jnp.int32, (GATE_ROWS, 1), 0)
    instances = [(hh, d) for d in range(2) for hh in range(N_ML_HEADS)]

    le = [i0 <= i1, i0 >= i1]

    def split3(x):
        hi = x.astype(BF16).astype(F32)
        r1 = x - hi
        mid = r1.astype(BF16).astype(F32)
        return jnp.concatenate([hi, mid, r1 - mid], axis=0).astype(BF16)

    def gate_rows(gi, lf_pre):
        n_rows = gi.shape[0]
        lf = _log_sigmoid(lf_pre)
        b_last = jnp.sum(lf, axis=1, keepdims=True)
        terms = split3(lf)
        per_dir = []
        for d in range(2):
            tri = jnp.where(le[d], 1.0, 0.0).astype(BF16)
            parts = _dot(terms, tri)
            cs = parts[0:n_rows] + parts[n_rows:2 * n_rows] + parts[2 * n_rows:3 * n_rows]
            w_end = b_last + gi - cs
            pad = jnp.zeros((LANES - n_rows, chunk), F32)
            per_dir.append((cs, w_end, jnp.max(w_end, axis=1, keepdims=True),
                            jnp.concatenate([gi - cs, pad], axis=0).T))
        return b_last, per_dir

    def chunk_rows(lo):
        return jnp.concatenate([gtt_ref[lo:lo + GATE_ROWS, c * chunk:(c + 1) * chunk] for c in range(n_chunks)],
                               axis=0)

    def body(i, gates):
        chunks = (i, n_chunks - 1 - i)
        rows = [pl.ds(pl.multiple_of(c * chunk, chunk), chunk) for c in chunks]
        m_prev = mrun_ref[...][:, 0:1]

        def qkv(hh, d):
            sl = slice(hh * dh, (hh + 1) * dh)
            return q_ref[rows[d], sl], k_ref[rows[d], sl], vt_ref[sl, rows[d]]

        st = [_dot_nt(qkv(hh, d)[1], qkv(hh, d)[0]) for hh, d in instances]
        if use_state:
            cn = [cn_ref[d, hh] for hh, d in instances]
            qct = [_dot_nt(cn[n].astype(BF16), qkv(hh, d)[0]) for n, (hh, d) in enumerate(instances)]

        if gates is None:
            gates = gate_rows(*[jnp.concatenate([gtt_ref[lo:lo + GATE_ROWS, r] for r in rows], axis=0)
                                for lo in (0, GATE_ROWS)])
            row0 = [0, GATE_ROWS]
        else:
            row0 = [GATE_ROWS * c for c in chunks]
        b_last_all, per_dir = gates
        a_col, cs8, w8, decay8, m_new = [], [], [], [], []
        for d in range(2):
            r = slice(row0[d], row0[d] + GATE_ROWS)
            cs_all, w_end_all, w_max_all, a_col_d = per_dir[d]
            b_last = b_last_all[r]
            m_d = jnp.maximum(b_last + m_prev, w_max_all[r])
            cs8.append(cs_all[r])
            decay8.append(jnp.exp(b_last + m_prev - m_d))
            w8.append(jnp.exp(w_end_all[r] - m_d))
            m_new.append(m_d)
            a_col.append(a_col_d)
        m_next = jnp.where((row_id & 1) == 0, m_new[0], m_new[1])
        mrun_ref[...] = jnp.broadcast_to(m_next, (GATE_ROWS, LANES))

        pt_all, g_all, den_all = [], [], []
        for n, (hh, d) in enumerate(instances):
            j = 2 * hh + d
            lane = row0[d] + j
            a = jnp.where(le[d], a_col[d][:, lane:lane + 1], -jnp.inf)
            g = jnp.maximum(jnp.max(a, axis=0, keepdims=True), m_prev[j:j + 1])
            pt = st[n] * jnp.exp(a - g)
            pt_all.append(pt)
            g_all.append(g)
            den_all.append(jnp.sum(pt, axis=0, keepdims=True))

        for n, (hh, d) in enumerate(instances):
            j = 2 * hh + d
            sl = slice(hh * dh, (hh + 1) * dh)
            _, k, vt = qkv(hh, d)
            g, den = g_all[n], den_all[n]
            num = _dot(vt, pt_all[n].astype(BF16))
            if use_state:
                w_inter = jnp.exp(m_prev[j:j + 1] - g)
                num = num + w_inter * qct[n][0:dh]
                den = den + w_inter * qct[n][dh:dh + 1]
            scale = 1.0 / jnp.maximum(jnp.abs(den), jnp.exp(-(cs8[d][j:j + 1] + g)))
            (hb_ref if d else hf_ref)[sl, rows[d]] = num * scale
            w_row = w8[d][j:j + 1]
            lhs = jnp.concatenate([(vt.astype(F32) * w_row).astype(BF16),
                                   jnp.broadcast_to(w_row, (nrep, chunk)).astype(BF16)], axis=0)
            upd = _dot(lhs, k)
            if use_state:
                upd = decay8[d][j:j + 1] * cn[n] + upd
            cn_ref[d, hh] = upd
        return 0

    def finish(c, _):
        rows = pl.ds(pl.multiple_of(c * chunk, chunk), chunk)
        for hh in range(N_ML_HEADS):
            sl = slice(hh * dh, (hh + 1) * dh)
            hs = hf_ref[sl, rows] + hb_ref[sl, rows]
            y = hs * lax.rsqrt(jnp.mean(hs * hs, axis=0, keepdims=True) + EPS) * g_ref[sl, :]
            y = y * jax.nn.sigmoid(mot_ref[sl, rows])
            o_ref[rows, sl] = y.T.astype(o_ref.dtype)
        return 0

    if n_chunks <= MLSTM_UNROLL:
        gates = gate_rows(chunk_rows(0), chunk_rows(GATE_ROWS))
        for c in range(n_chunks):
            body(c, gates)
        for c in range(n_chunks):
            finish(c, 0)
    else:
        lax.fori_loop(0, n_chunks, lambda i, _: body(i, None), 0)
        lax.fori_loop(0, n_chunks, finish, 0)
    for d in range(2):
        for hh in range(N_ML_HEADS):
            cn = cn_ref[d, hh]
            c_ref[0, d, hh] = cn[0:dh].T
            n_ref[0, d, hh] = cn[dh:dh + 1]
    m_ref[0] = mrun_ref[...]


def _mlstm(mq, mk, mvt, mot, gates_t, ml_g, state, n_batch, seq_len, chunk):
    dh = ML_HEAD_DIM
    nh = N_ML_HEADS
    width = nh * dh
    has_state = state is not None
    seq = pl.BlockSpec((seq_len, width), lambda b: (b, 0))
    seq_t = pl.BlockSpec((width, seq_len), lambda b: (0, b))
    st_c = pl.BlockSpec((1, 2, nh, dh, dh), lambda b: (b, 0, 0, 0, 0))
    st_n = pl.BlockSpec((1, 2, nh, 1, dh), lambda b: (b, 0, 0, 0, 0))
    st_m = pl.BlockSpec((1, GATE_ROWS, LANES), lambda b: (b, 0, 0))
    in_specs = [seq, seq, seq_t, seq_t,
                pl.BlockSpec((2 * GATE_ROWS, seq_len), lambda b: (0, b)),
                _const_spec((width, chunk))]
    args = [mq, mk, mvt, mot, gates_t, jnp.broadcast_to(ml_g.reshape(width, 1), (width, chunk))]
    if has_state:
        in_specs += [st_c, st_n, st_m]
        args += list(state)
    out_shape = [jax.ShapeDtypeStruct((n_batch * seq_len, width), BF16),
                 jax.ShapeDtypeStruct((n_batch, 2, nh, dh, dh), F32),
                 jax.ShapeDtypeStruct((n_batch, 2, nh, 1, dh), F32),
                 jax.ShapeDtypeStruct((n_batch, GATE_ROWS, LANES), F32)]
    return pl.pallas_call(
        functools.partial(_mlstm_kernel, chunk=chunk, n_chunks=seq_len // chunk, has_state=has_state),
        grid=(n_batch,),
        in_specs=in_specs,
        out_specs=[seq, st_c, st_n, st_m],
        out_shape=out_shape,
        scratch_shapes=[pltpu.VMEM((width, seq_len), F32), pltpu.VMEM((width, seq_len), F32),
                        pltpu.VMEM((2, nh, dh + 2 * GATE_ROWS, dh), F32), pltpu.VMEM((GATE_ROWS, LANES), F32)],
        compiler_params=_params(1),
        name="mlstm_latent" if has_state else "mlstm_ctx",
    )(*args)


def _pack_gate_rows(m):
    b = m.shape[0]
    packed = m.transpose(0, 2, 1).reshape(b, 2 * N_ML_HEADS, 1)
    return jnp.broadcast_to(packed, (b, GATE_ROWS, LANES))


def _unpack_gate_rows(m):
    b = m.shape[0]
    return m[:, :, 0].reshape(b, N_ML_HEADS, 2).transpose(0, 2, 1)


def _inproj_ctxmix_kernel(*refs, n_in, n_out, n_seq, seq_len, chunk):
    ip_in = refs[:n_in]
    q_ref, k_ref, v_ref, mq_ref, mk_ref, mvt_ref, mot_ref, gtt_ref, g_ref = refs[n_in:n_in + 9]
    outs = refs[n_in + 9:]
    ip_out = outs[:n_out]
    oatt_ref, oml_ref, c_ref, n_ref, m_ref = outs[n_out:n_out + 5]
    scratch = outs[n_out + 5:]
    _inproj_kernel(*ip_in, *ip_out, latent=True)
    for s in range(n_seq):
        rows = slice(s * seq_len, (s + 1) * seq_len)
        _ctx_attn_kernel(q_ref.at[rows, :], k_ref.at[rows, :], v_ref.at[rows, :], oatt_ref.at[rows, :])
        _mlstm_kernel(mq_ref.at[rows, :], mk_ref.at[rows, :], mvt_ref.at[:, rows], mot_ref.at[:, rows],
                      gtt_ref.at[:, rows], g_ref, oml_ref.at[rows, :], c_ref.at[s:s + 1], n_ref.at[s:s + 1],
                      m_ref.at[s:s + 1], *scratch[4 * s:4 * s + 4],
                      chunk=chunk, n_chunks=seq_len // chunk, has_state=False)


def _latent_in_projection_with_ctx_mixers(x, mod, mod_row, seq_len, wts, rope, ctx, n_ctx_batch, ctx_seq):
    n_tiles, in_specs, args, out_shape, out_specs, scratch = _in_projection_parts(
        x, mod, mod_row, seq_len, wts, rope, latent=True)
    n_seq = n_ctx_batch // n_tiles
    rows = n_seq * ctx_seq
    n_ctx_tok = n_ctx_batch * ctx_seq
    chunk = min(ML_CHUNK, ctx_seq)
    nh, dh = N_ML_HEADS, ML_HEAD_DIM

    def tok(width):
        return pl.BlockSpec((rows, width), lambda i: (i, 0))

    def tok_t(height):
        return pl.BlockSpec((height, rows), lambda i: (0, i))

    def per_seq(*tail):
        return pl.BlockSpec((n_seq,) + tail, lambda i: (i,) + (0,) * len(tail))

    aq, ak, av, mq, mk, mvt, mot, gtt = ctx
    mix_in_specs = ([tok(ATT_WIDTH)] * 3 + [tok(ML_WIDTH)] * 2 + [tok_t(ML_WIDTH)] * 2
                    + [tok_t(2 * GATE_ROWS), _const_spec((ML_WIDTH, chunk))])
    mix_args = [aq, ak, av, mq, mk, mvt, mot, gtt,
                jnp.broadcast_to(wts["ml_g"].reshape(ML_WIDTH, 1), (ML_WIDTH, chunk))]
    mix_out_shape = [jax.ShapeDtypeStruct((n_ctx_tok, ATT_WIDTH), BF16),
                     jax.ShapeDtypeStruct((n_ctx_tok, ML_WIDTH), BF16),
                     jax.ShapeDtypeStruct((n_ctx_batch, 2, nh, dh, dh), F32),
                     jax.ShapeDtypeStruct((n_ctx_batch, 2, nh, 1, dh), F32),
                     jax.ShapeDtypeStruct((n_ctx_batch, GATE_ROWS, LANES), F32)]
    mix_out_specs = [tok(ATT_WIDTH), tok(ML_WIDTH), per_seq(2, nh, dh, dh), per_seq(2, nh, 1, dh),
                     per_seq(GATE_ROWS, LANES)]
    mix_scratch = [pltpu.VMEM((ML_WIDTH, ctx_seq), F32), pltpu.VMEM((ML_WIDTH, ctx_seq), F32),
                   pltpu.VMEM((2, nh, dh + 2 * GATE_ROWS, dh), F32), pltpu.VMEM((GATE_ROWS, LANES), F32)] * n_seq
    outs = pl.pallas_call(
        functools.partial(_inproj_ctxmix_kernel, n_in=len(in_specs), n_out=len(out_specs), n_seq=n_seq,
                          seq_len=ctx_seq, chunk=chunk),
        grid=(n_tiles,),
        in_specs=in_specs + mix_in_specs,
        out_specs=out_specs + mix_out_specs,
        out_shape=out_shape + mix_out_shape,
        scratch_shapes=scratch + mix_scratch,
        compiler_params=_params(1),
        name="in_proj_latent_ctx_mixers",
    )(*args, *mix_args)
    return outs[:len(out_specs)], outs[len(out_specs):]


def _ffn_kernel(oa_ref, oap_ref, oan_ref, om_ref, omp_ref, omn_ref, x_ref, xp_ref, xn_ref, mod_ref, g2_ref,
                wout_ref, wup_ref, cw_ref, cb_ref, wd_ref,
                y_ref, oc_ref, lhs_ref, x1_ref, ug_ref, uv_ref, acc_ref, *, seq_len, n_col_tiles, side_work=None):
    tm, d = x_ref.shape
    aw = oa_ref.shape[1]

    def halo(next_ref, prev_ref):
        row = lax.broadcasted_iota(jnp.int32, next_ref.shape, 0)
        return jnp.where(row < HALO // 2, next_ref[...], prev_ref[...])

    oc_ref[0:tm, 0:aw] = oa_ref[...]
    oc_ref[0:tm, aw:] = om_ref[...]
    oc_ref[tm:tm + HALO, 0:aw] = halo(oan_ref, oap_ref)
    oc_ref[tm:tm + HALO, aw:] = halo(omn_ref, omp_ref)
    mod = mod_ref[0]
    g1 = mod[:, 2 * d:3 * d]
    sh2, sc2 = mod[:, 3 * d:4 * d], mod[:, 4 * d:5 * d]
    out = _dot(oc_ref[...], wout_ref[...])

    def norm2(x1):
        y = x1 * lax.rsqrt(jnp.mean(x1 * x1, axis=-1, keepdims=True) + EPS)
        return ((y * g2_ref[...]) * (1.0 + sc2) + sh2).astype(BF16)

    x1 = x_ref[...] + g1 * out[0:tm]
    x1_ref[...] = x1
    lhs_ref[0:tm, :] = norm2(x1)
    lhs_ref[tm:tm + HALO, :] = norm2(halo(xn_ref, xp_ref) + g1 * out[tm:tm + HALO])
    acc_ref[...] = jnp.zeros_like(acc_ref)
    sub = 8
    period = min(seq_len, tm)
    first_groups = sorted({r // sub for r in range(0, tm, period)})
    last_groups = sorted({(r + period - 1) // sub for r in range(0, tm, period)})
    tile0 = pl.program_id(0) * tm

    def zero_rows(x, groups, target):
        parts, at = [], 0
        for grp in groups:
            lo = grp * sub
            if lo > at:
                parts.append(x[at:lo])
            pos = (tile0 + lo + lax.broadcasted_iota(jnp.int32, (sub, 1), 0)) % seq_len
            parts.append(jnp.where(pos == target, 0.0, x[lo:lo + sub]))
            at = lo + sub
        if at < tm:
            parts.append(x[at:tm])
        return jnp.concatenate(parts, axis=0)

    def conv(u, cw, cb):
        prev = zero_rows(pltpu.roll(u, 1, 0)[0:tm], first_groups, 0)
        nxt = zero_rows(pltpu.roll(u, tm + HALO - 1, 0)[0:tm], last_groups, seq_len - 1)
        return prev * cw[0:1] + u[0:tm] * cw[1:2] + nxt * cw[2:3] + cb

    def up(j, slot):
        lhs = lhs_ref[...]
        ug_ref[slot] = _dot(lhs, wup_ref[j])
        uv_ref[slot] = _dot(lhs, wup_ref[n_col_tiles + j])

    def act(j, slot):
        gate = conv(ug_ref[slot], cw_ref[j], cb_ref[j])
        val = conv(uv_ref[slot], cw_ref[n_col_tiles + j], cb_ref[n_col_tiles + j])
        return ((gate * jax.nn.sigmoid(gate)) * val).astype(BF16)

    def stage(j, slot, next_up):
        a = act(j, slot)
        if next_up:
            up(j + 1, 1 - slot)
        return _dot(a, wd_ref[j])

    def pair(i, _):
        acc_ref[...] += stage(2 * i, 0, True)
        acc_ref[...] += stage(2 * i + 1, 1, True)
        return 0

    def finish(tail):
        g2 = mod_ref[0][:, 5 * d:6 * d]
        y_ref[...] = x1_ref[...] + g2 * (acc_ref[...] + tail)

    up(0, 0)
    if side_work:
        n_points = n_col_tiles + 1

        def hosted(point):
            for i, work in enumerate(side_work):
                if (i * n_points) // len(side_work) == point:
                    work()

        hosted(0)
        for j in range(n_col_tiles - 1):
            acc_ref[...] += stage(j, j % 2, True)
            hosted(j + 1)
        finish(stage(n_col_tiles - 1, (n_col_tiles - 1) % 2, False))
        hosted(n_col_tiles)
        return
    n_pairs = (n_col_tiles - 1) // 2
    if n_pairs:
        lax.fori_loop(0, n_pairs, pair, 0)
    j0 = 2 * n_pairs
    if n_col_tiles - j0 == 2:
        acc_ref[...] += stage(j0, 0, True)
        finish(stage(j0 + 1, 1, False))
    else:
        finish(stage(j0, 0, False))


def _mix_ffn_parts(o_att, o_ml, x, mod, mod_row, seq_len, wts):
    n_tok, d = x.shape
    tm = _token_tile(n_tok, seq_len)
    hpt = tm // HALO
    n_halo = n_tok // HALO
    nct, tn, _ = wts["w_down"].shape
    mix = o_att.shape[1] + o_ml.shape[1]

    def tile_and_halos(width):
        return [pl.BlockSpec((tm, width), lambda i: (i, 0)),
                pl.BlockSpec((HALO, width), lambda i: (jnp.maximum(i * hpt - 1, 0), 0)),
                pl.BlockSpec((HALO, width), lambda i: (jnp.minimum((i + 1) * hpt, n_halo - 1), 0))]

    in_specs = (tile_and_halos(o_att.shape[1]) + tile_and_halos(o_ml.shape[1]) + tile_and_halos(d) + [
        pl.BlockSpec((1, 1, mod.shape[-1]), lambda i: (mod_row(i, tm), 0, 0)),
        _const_spec((1, d)), _const_spec((mix, d)),
        _const_spec((2 * nct, d, tn)), _const_spec((2 * nct, 3, tn)), _const_spec((2 * nct, 1, tn)),
        _const_spec((nct, tn, d))])
    args = [o_att, o_att, o_att, o_ml, o_ml, o_ml, x, x, x, mod, wts["norm2_g"], wts["w_out"],
            wts["w_up"], wts["conv_w"], wts["conv_b"], wts["w_down"]]
    scratch = [pltpu.VMEM((tm + HALO, mix), BF16), pltpu.VMEM((tm + HALO, d), BF16), pltpu.VMEM((tm, d), F32),
               pltpu.VMEM((2, tm + HALO, tn), F32), pltpu.VMEM((2, tm + HALO, tn), F32), pltpu.VMEM((tm, d), F32)]
    return (n_tok // tm, nct, in_specs, args, pl.BlockSpec((tm, d), lambda i: (i, 0)),
            jax.ShapeDtypeStruct((n_tok, d), F32), scratch)


def _mix_ffn(o_att, o_ml, x, mod, mod_row, seq_len, wts):
    n_tiles, nct, in_specs, args, out_spec, out_shape, scratch = _mix_ffn_parts(
        o_att, o_ml, x, mod, mod_row, seq_len, wts)
    return pl.pallas_call(
        functools.partial(_ffn_kernel, seq_len=seq_len, n_col_tiles=nct),
        grid=(n_tiles,),
        in_specs=in_specs,
        out_specs=out_spec,
        out_shape=out_shape,
        scratch_shapes=scratch,
        compiler_params=_params(1),
        name="mix_ffn",
    )(*args)


def _ffn_nbr_kernel(*refs, n_in, seq_len, n_col_tiles, rows, kr, rows_per_tile, tiles_per_batch):
    ffn_in = refs[:n_in]
    q_ref, k_ref, vt_ref, kc_ref, vct_ref, bias_ref = refs[n_in:n_in + 6]
    y_ref, o_ref = refs[n_in + 6:n_in + 8]
    scratch = refs[n_in + 8:]
    ffn_scratch, pt_refs = scratch[:-2], scratch[-2:]
    base = (pl.program_id(0) % tiles_per_batch) * rows_per_tile

    def row(local, pt_ref):
        rr = pl.ds(pl.multiple_of(local * GRID_W, GRID_W), GRID_W)
        _nbr_row(base + local, q_ref.at[rr, :], k_ref, vt_ref, kc_ref, vct_ref, bias_ref, o_ref.at[rr, :], pt_ref,
                 rows=rows, kr=kr)

    side_work = [functools.partial(row, local, pt_refs[local % 2]) for local in range(rows_per_tile)]
    _ffn_kernel(*ffn_in, y_ref, *ffn_scratch, seq_len=seq_len, n_col_tiles=n_col_tiles, side_work=side_work)


def _mix_ffn_with_nbr_attention(o_att, o_ml, x, mod, mod_row, seq_len, wts, q, k, vt, k_ctx, vt_ctx, rpb,
                                n_batch, nbr_seq):
    n_tiles, nct, in_specs, args, out_spec, out_shape, scratch = _mix_ffn_parts(
        o_att, o_ml, x, mod, mod_row, seq_len, wts)
    rows = nbr_seq // GRID_W
    kr = min(WIN_ROWS, rows)
    n_win = kr + 2
    assert rows >= n_win and (rows - n_win) % 2 == 0
    rpt = (n_batch * rows) // n_tiles
    tpb = n_tiles // n_batch
    bias = _nbr_bias_table(rpb)
    past = k_ctx.shape[1]
    gw = HEAD_GROUP * ATT_HEAD_DIM
    n_groups = N_ATT_HEADS // HEAD_GROUP
    row_blk = pl.BlockSpec((rpt * GRID_W, ATT_WIDTH), lambda i: (i, 0))
    once = pl.Buffered(1)
    nbr_specs = [row_blk,
                 pl.BlockSpec((nbr_seq, ATT_WIDTH), lambda i: (i // tpb, 0), pipeline_mode=once),
                 pl.BlockSpec((ATT_WIDTH, nbr_seq), lambda i: (0, i // tpb), pipeline_mode=once),
                 pl.BlockSpec((1, past, ATT_WIDTH), lambda i: (i // tpb, 0, 0), pipeline_mode=once),
                 pl.BlockSpec((1, ATT_WIDTH, past), lambda i: (i // tpb, 0, 0), pipeline_mode=once),
                 _const_spec(bias.shape)]
    pt = pltpu.VMEM((n_groups, n_win * GRID_W, gw), BF16)
    y, o_lat = pl.pallas_call(
        functools.partial(_ffn_nbr_kernel, n_in=len(in_specs), seq_len=seq_len, n_col_tiles=nct, rows=rows, kr=kr,
                          rows_per_tile=rpt, tiles_per_batch=tpb),
        grid=(n_tiles,),
        in_specs=in_specs + nbr_specs,
        out_specs=[out_spec, row_blk],
        out_shape=[out_shape, jax.ShapeDtypeStruct((n_batch * nbr_seq, ATT_WIDTH), BF16)],
        scratch_shapes=scratch + [pt, pt],
        compiler_params=_params(1),
        name="mix_ffn_ctx_nbr_attn",
    )(*args, q, k, vt, k_ctx, vt_ctx, bias)
    return y, o_lat


def _retile_cast_kernel(w_ref, o_ref):
    tn = o_ref.shape[-1]
    for t in range(o_ref.shape[0]):
        o_ref[t] = w_ref[:, t * tn:(t + 1) * tn].astype(o_ref.dtype)


def _retile_cast(w, tn):
    r, c = w.shape
    group = 2 if (c // tn) % 2 == 0 else 1
    return pl.pallas_call(
        _retile_cast_kernel,
        grid=(c // (tn * group),),
        in_specs=[pl.BlockSpec((r, tn * group), lambda j: (0, j))],
        out_specs=pl.BlockSpec((group, r, tn), lambda j: (j, 0, 0)),
        out_shape=jax.ShapeDtypeStruct((c // tn, r, tn), BF16),
        compiler_params=_params(1),
        name="retile_cast",
    )(w)


def _cast_kernel(w_ref, o_ref):
    o_ref[...] = w_ref[...].astype(o_ref.dtype)


def _cast_rows(w, block_rows):
    r, c = w.shape
    spec = pl.BlockSpec((block_rows, c), lambda i: (i, 0))
    return pl.pallas_call(
        _cast_kernel,
        grid=(r // block_rows,),
        in_specs=[spec],
        out_specs=spec,
        out_shape=jax.ShapeDtypeStruct((r, c), BF16),
        compiler_params=_params(1),
        name="cast_rows",
    )(w)


def _layer_weights(norm1_g, norm2_g, w_in, b_gate, q_g, k_g, ml_g, w_out, w_up, conv_w, conv_b, w_down):
    d = w_in.shape[0]
    a, m, nh = ATT_WIDTH, ML_WIDTH, N_ML_HEADS
    def gate_lanes(g):
        g = g.reshape(g.shape[0], 2, 2, nh).transpose(0, 2, 3, 1).reshape(g.shape[0], 4 * nh)
        return jnp.pad(g, ((0, 0), (0, LANES - 4 * nh)))

    w_g = gate_lanes(w_in[:, 3 * a + 4 * m:])
    b_g = gate_lanes(b_gate.astype(F32).reshape(1, 4 * nh))
    head_id = jnp.arange(a) // ATT_HEAD_DIM
    d_ff = w_down.shape[0]
    tn = MXU_WIDTH
    nct = d_ff // tn

    def col_tiles(w):
        return w.reshape(w.shape[0], 2 * nct, tn).transpose(1, 0, 2)

    return {
        "norm1_g": norm1_g.reshape(1, d), "norm2_g": norm2_g.reshape(1, d),
        "w_in": w_in.astype(BF16),
        "w_g": w_g.astype(BF16), "b_g": b_g,
        "q_g": jnp.tile(q_g, N_ATT_HEADS).reshape(1, a), "k_g": jnp.tile(k_g, N_ATT_HEADS).reshape(1, a),
        "blockdiag": (head_id[:, None] == head_id[None, :]).astype(BF16),
        "ml_g": ml_g.reshape(1, m),
        "w_out": w_out.astype(BF16),
        "conv_w": col_tiles(conv_w.astype(F32)), "conv_b": col_tiles(conv_b.astype(F32).reshape(1, -1)),
        "w_up_f32": w_up, "w_down_f32": w_down, "ffn_tile": tn,
    }


def _rope_tables(seq_len):
    quarter = ML_HEAD_DIM // 4
    pos = np.arange(seq_len)
    inv_freq = ROPE_THETA ** (-np.arange(quarter, dtype=np.float64) / quarter)
    ang_r = (pos // GRID_W).astype(np.float64)[:, None] * inv_freq[None, :]
    ang_c = (pos % GRID_W).astype(np.float64)[:, None] * inv_freq[None, :]
    cos = np.concatenate([np.cos(ang_r)] * 2 + [np.cos(ang_c)] * 2, axis=-1)
    sin = np.concatenate([-np.sin(ang_r), np.sin(ang_r), -np.sin(ang_c), np.sin(ang_c)], axis=-1)
    return jnp.asarray(cos, F32), jnp.asarray(sin, F32)


def _layer(xp, xs, mod, wts, rope, ctx_kv, rpb, state, batch, seq, dec_batch, dec_seq):
    def ctx_row(i, tm):
        return 0 * i

    def lat_row(i, tm):
        return 1 + (i * tm) // dec_seq

    w_up, w_down, tn = wts["w_up_f32"], wts["w_down_f32"], wts["ffn_tile"]
    n_ctx_tiles = (batch * seq) // _token_tile(batch * seq, seq)
    if n_ctx_tiles >= max(w_up.shape[1] // (2 * tn), w_down.shape[0] // tn) and (w_up.shape[1] // tn) % 2 == 0:
        ctx, w_up_t, w_down_t = _ctx_in_projection_with_weight_casts(xp, mod, ctx_row, seq, wts, w_up, w_down, tn)
    else:
        ctx = _in_projection(xp, mod, ctx_row, seq, wts, None, latent=False)
        w_up_t = _retile_cast(w_up, tn)
        w_down_t = _cast_rows(w_down, tn).reshape(w_down.shape[0] // tn, tn, w_down.shape[1])
    wts = dict(wts, w_up=w_up_t, w_down=w_down_t)
    cache_k, cache_v = ctx[8], ctx[9]
    n_lat_tiles = (dec_batch * dec_seq) // _token_tile(dec_batch * dec_seq, dec_seq)
    if batch % n_lat_tiles == 0:
        lat, (o_att_c, o_ml_c, c_f, n_f, m_f) = _latent_in_projection_with_ctx_mixers(
            xs, mod, lat_row, dec_seq, wts, rope, ctx[:8], batch, seq)
    else:
        aq, ak, av, mq, mk, mvt, mot, gtt = ctx[:8]
        o_att_c = _context_attention(aq, ak, av, seq)
        o_ml_c, c_f, n_f, m_f = _mlstm(mq, mk, mvt, mot, gtt, wts["ml_g"], None, batch, seq, min(ML_CHUNK, seq))
        lat = _in_projection(xs, mod, lat_row, dec_seq, wts, rope, latent=True)
    aq, ak, avt, mq, mk, mvt, mot, gtt = lat[:8]
    n_rows = dec_batch * (dec_seq // GRID_W)
    if n_rows % n_ctx_tiles == 0 and n_ctx_tiles % dec_batch == 0:
        xp, o_att = _mix_ffn_with_nbr_attention(o_att_c, o_ml_c, xp, mod, ctx_row, seq, wts, aq, ak, avt,
                                                ctx_kv[0], ctx_kv[1], rpb, dec_batch, dec_seq)
    else:
        xp = _mix_ffn(o_att_c, o_ml_c, xp, mod, ctx_row, seq, wts)
        o_att = _neighborhood_attention(aq, ak, avt, ctx_kv[0], ctx_kv[1], rpb, dec_batch, dec_seq)
    o_ml = _mlstm(mq, mk, mvt, mot, gtt, wts["ml_g"], state, dec_batch, dec_seq, min(ML_CHUNK, dec_seq))[0]
    xs = _mix_ffn(o_att, o_ml, xs, mod, lat_row, dec_seq, wts)
    return xp, xs, cache_k, cache_v, (c_f, n_f[:, :, :, 0, :], _unpack_gate_rows(m_f))


def kernel(x_prompt, x_sample, cache_k, cache_v, state_C, state_n, state_m, c, c_ctx, w_mod, b_mod, norm1_g,
           norm2_g, w_in, b_gate, q_norm_g, k_norm_g, rpb, ml_norm_g, w_out, w_up, conv_w, conv_b, w_down):
    batch, seq, d = x_prompt.shape
    dec_batch, dec_seq, _ = x_sample.shape
    depth = w_mod.shape[0]
    past = cache_k.shape[2]
    cvecs = jnp.concatenate([c_ctx[None, :], c], axis=0).astype(F32)
    rope = _rope_tables(dec_seq)

    xp = x_prompt.reshape(batch * seq, d)
    xs = x_sample.reshape(dec_batch * dec_seq, d)
    ks, vs, cs, ns, ms = [], [], [], [], []
    for l in range(depth):
        wts = _layer_weights(norm1_g[l], norm2_g[l], w_in[l], b_gate[l], q_norm_g[l], k_norm_g[l],
                             ml_norm_g[l], w_out[l], w_up[l], conv_w[l], conv_b[l], w_down[l])
        mod = _modulation(cvecs, w_mod[l], b_mod[l])

        state = (state_C[:, l].astype(F32),
                 state_n[:, l].astype(F32)[:, :, :, None, :],
                 _pack_gate_rows(state_m[:, l].astype(F32)))
        ctx_kv = (cache_k[:, l].reshape(dec_batch, past, ATT_WIDTH).astype(BF16),
                  cache_v[:, l].reshape(dec_batch, past, ATT_WIDTH).transpose(0, 2, 1).astype(BF16))
        xp, xs, ak, av, st = _layer(xp, xs, mod, wts, rope, ctx_kv, rpb[l], state, batch, seq, dec_batch, dec_seq)
        ks.append(ak.reshape(batch, seq, N_ATT_HEADS, ATT_HEAD_DIM))
        vs.append(av.reshape(batch, seq, N_ATT_HEADS, ATT_HEAD_DIM))
        cs.append(st[0])
        ns.append(st[1])
        ms.append(st[2])
    return (xp.reshape(batch, seq, d), xs.reshape(dec_batch, dec_seq, d),
            jnp.stack(ks, axis=1), jnp.stack(vs, axis=1),
            jnp.stack(cs, axis=1), jnp.stack(ns, axis=1), jnp.stack(ms, axis=1))
```

```python
import functools

import jax
import jax.numpy as jnp
import numpy as np
from jax import lax
from jax.experimental import pallas as pl
from jax.experimental.pallas import tpu as pltpu

F32 = jnp.float32
BF16 = jnp.bfloat16

GRID_W = 64
N_ATT_HEADS = 8
ATT_HEAD_DIM = 64
ATT_WIDTH = N_ATT_HEADS * ATT_HEAD_DIM
WIN_ROWS = 8
WIN_COLS = 16
HEAD_GROUP = 4
N_ML_HEADS = 4
ML_HEAD_DIM = 128
ML_WIDTH = N_ML_HEADS * ML_HEAD_DIM
ROPE_THETA = 10000.0
EPS = 1e-6

LANES = 128
MXU_WIDTH = 256
ML_CHUNK = 256
MLSTM_UNROLL = 8
GATE_ROWS = 8
HALO = 16
TOKEN_TILE = 512
VMEM_LIMIT = 56 * 1024 * 1024


def _dot(a, b):
    return jnp.dot(a, b, preferred_element_type=F32)


def _dot_nt(a, b):
    return lax.dot_general(a, b, (((1,), (1,)), ((), ())), preferred_element_type=F32)


def _dot_tn(a, b):
    return lax.dot_general(a, b, (((0,), (0,)), ((), ())), preferred_element_type=F32)


def _const_spec(shape):
    nd = len(shape)
    return pl.BlockSpec(shape, lambda *_: (0,) * nd, pipeline_mode=pl.Buffered(1))


def _params(n_axes):
    return pltpu.CompilerParams(dimension_semantics=("arbitrary",) * n_axes,
                                vmem_limit_bytes=VMEM_LIMIT)


def _token_tile(n_tok, seq_len):
    tm = min(TOKEN_TILE, n_tok)
    while n_tok % tm or (seq_len % tm and tm % seq_len):
        tm //= 2
    return tm


def _log_sigmoid(x):
    return jnp.minimum(x, 0.0) - jnp.log1p(jnp.exp(-jnp.abs(x)))


def _mod_kernel(c_ref, w_ref, b_ref, o_ref):
    c = c_ref[...]
    s = c * jax.nn.sigmoid(c)
    o_ref[...] = _dot(s.astype(BF16), w_ref[...].astype(BF16)) + b_ref[...]


def _modulation(cvecs, w_mod, b_mod):
    r, d = cvecs.shape
    n = w_mod.shape[1]
    tn = d
    out = pl.pallas_call(
        _mod_kernel,
        grid=(n // tn,),
        in_specs=[pl.BlockSpec((r, d), lambda j: (0, 0)),
                  pl.BlockSpec((d, tn), lambda j: (0, j)),
                  pl.BlockSpec((1, tn), lambda j: (0, j))],
        out_specs=pl.BlockSpec((r, tn), lambda j: (0, j)),
        out_shape=jax.ShapeDtypeStruct((r, n), F32),
        compiler_params=_params(1),
        name="adaln_mod",
    )(cvecs, w_mod, b_mod.reshape(1, n))
    return out.reshape(r, 1, n)


def _rope(x, cos, sin_signed):
    lane = lax.broadcasted_iota(jnp.int32, x.shape, 1)
    partner = jnp.where((lane & 32) == 0, pltpu.roll(x, LANES - 32, 1), pltpu.roll(x, 32, 1))
    return x * cos + partner * sin_signed


def _inproj_kernel(*refs, latent):
    if latent:
        (x_ref, mod_ref, g1_ref, win_ref, wg_ref, bg_ref, qg_ref, kg_ref, bd_ref, cos_ref, sin_ref,
         aq_ref, ak_ref, av_ref, mq_ref, mk_ref, mv_ref, mo_ref, gtt_ref) = refs
    else:
        (x_ref, mod_ref, g1_ref, win_ref, wg_ref, bg_ref, qg_ref, kg_ref, bd_ref,
         aq_ref, ak_ref, av_ref, mq_ref, mk_ref, mv_ref, mo_ref, gtt_ref, ck_ref, cv_ref,
         cks_ref, cvs_ref) = refs
    x = x_ref[...]
    d = x.shape[-1]
    mod = mod_ref[0]
    sh1, sc1 = mod[:, 0:d], mod[:, d:2 * d]
    y = x * lax.rsqrt(jnp.mean(x * x, axis=-1, keepdims=True) + EPS)
    h = (y * g1_ref[...]) * (1.0 + sc1) + sh1
    hb = h.astype(BF16)

    def head_norm(a, g):
        ss = _dot((a * a).astype(BF16), bd_ref[...])
        return a * lax.rsqrt(ss * (1.0 / ATT_HEAD_DIM) + EPS) * g

    w = ATT_WIDTH
    att = _dot(hb, win_ref[:, 0:3 * w])
    aq_ref[...] = (head_norm(att[:, 0:w], qg_ref[...]) * ATT_HEAD_DIM ** -0.5).astype(aq_ref.dtype)
    kn = head_norm(att[:, w:2 * w], kg_ref[...])
    av = att[:, 2 * w:3 * w]
    ak_ref[...] = kn.astype(BF16)
    if latent:
        av_ref[...] = av.T.astype(BF16)
    else:
        av_ref[...] = av.astype(BF16)
        tm = kn.shape[0]
        for src, dst, scr in ((kn, ck_ref, cks_ref), (av, cv_ref, cvs_ref)):
            for hh in range(N_ATT_HEADS):
                pair = src[:, (hh // 2) * LANES:(hh // 2 + 1) * LANES]
                if hh % 2:
                    pair = pltpu.roll(pair, ATT_HEAD_DIM, 1)
                scr[pl.ds(hh, tm, stride=N_ATT_HEADS), :] = pair
            dst[...] = scr[...].reshape(tm, N_ATT_HEADS, LANES)[:, :, 0:ATT_HEAD_DIM]

    w = ML_WIDTH
    ml0 = 3 * ATT_WIDTH
    mq = _dot(hb, win_ref[:, ml0:ml0 + w])
    mk = _dot(hb, win_ref[:, ml0 + w:ml0 + 2 * w]) * ML_HEAD_DIM ** -0.5
    if latent:
        cos, sin = cos_ref[...], sin_ref[...]
        for hh in range(N_ML_HEADS):
            sl = slice(hh * ML_HEAD_DIM, (hh + 1) * ML_HEAD_DIM)
            mq_ref[:, sl] = _rope(mq[:, sl], cos, sin).astype(BF16)
            mk_ref[:, sl] = _rope(mk[:, sl], cos, sin).astype(BF16)
    else:
        mq_ref[...] = mq.astype(BF16)
        mk_ref[...] = mk.astype(BF16)
    mv_ref[...] = _dot(hb, win_ref[:, ml0 + 2 * w:ml0 + 3 * w]).T.astype(BF16)
    mo_ref[...] = _dot(hb, win_ref[:, ml0 + 3 * w:ml0 + 4 * w]).T
    gates = _dot(hb, wg_ref[...]) + bg_ref[...]
    gtt_ref[...] = gates.T[0:2 * GATE_ROWS]


def _in_projection_parts(x, mod, mod_row, seq_len, wts, rope, *, latent):
    n_tok, d = x.shape
    tm = _token_tile(n_tok, seq_len)
    tiles_per_seq = max(seq_len // tm, 1)

    def tok(width):
        return pl.BlockSpec((tm, width), lambda i: (i, 0))

    in_specs = [tok(d),
                pl.BlockSpec((1, 1, mod.shape[-1]), lambda i: (mod_row(i, tm), 0, 0)),
                _const_spec((1, d)),
                _const_spec(wts["w_in"].shape),
                _const_spec(wts["w_g"].shape), _const_spec((1, LANES)),
                _const_spec((1, ATT_WIDTH)), _const_spec((1, ATT_WIDTH)),
                _const_spec((ATT_WIDTH, ATT_WIDTH))]
    args = [x, mod, wts["norm1_g"], wts["w_in"], wts["w_g"], wts["b_g"],
            wts["q_g"], wts["k_g"], wts["blockdiag"]]
    if latent:
        in_specs += [pl.BlockSpec((tm, LANES), lambda i: (i % tiles_per_seq, 0))] * 2
        args += [rope[0], rope[1]]
    out_shape = [jax.ShapeDtypeStruct((n_tok, ATT_WIDTH), BF16),
                 jax.ShapeDtypeStruct((n_tok, ATT_WIDTH), BF16),
                 jax.ShapeDtypeStruct((ATT_WIDTH, n_tok) if latent else (n_tok, ATT_WIDTH), BF16),
                 jax.ShapeDtypeStruct((n_tok, ML_WIDTH), BF16),
                 jax.ShapeDtypeStruct((n_tok, ML_WIDTH), BF16),
                 jax.ShapeDtypeStruct((ML_WIDTH, n_tok), BF16),
                 jax.ShapeDtypeStruct((ML_WIDTH, n_tok), F32),
                 jax.ShapeDtypeStruct((2 * GATE_ROWS, n_tok), F32)]

    def tok_t(height):
        return pl.BlockSpec((height, tm), lambda i: (0, i))

    av_spec = tok_t(ATT_WIDTH) if latent else tok(ATT_WIDTH)
    out_specs = ([tok(ATT_WIDTH)] * 2 + [av_spec] + [tok(ML_WIDTH)] * 2 + [tok_t(ML_WIDTH)] * 2
                 + [tok_t(2 * GATE_ROWS)])
    if not latent:
        cache = jax.ShapeDtypeStruct((n_tok, N_ATT_HEADS, ATT_HEAD_DIM), F32)
        out_shape += [cache, cache]
        out_specs += [pl.BlockSpec((tm, N_ATT_HEADS, ATT_HEAD_DIM), lambda i: (i, 0, 0))] * 2
    scratch = [] if latent else [pltpu.VMEM((tm * N_ATT_HEADS, LANES), F32)] * 2
    return n_tok // tm, in_specs, args, out_shape, out_specs, scratch


def _in_projection(x, mod, mod_row, seq_len, wts, rope, *, latent):
    n_tiles, in_specs, args, out_shape, out_specs, scratch = _in_projection_parts(
        x, mod, mod_row, seq_len, wts, rope, latent=latent)
    return pl.pallas_call(
        functools.partial(_inproj_kernel, latent=latent),
        grid=(n_tiles,),
        in_specs=in_specs,
        out_specs=out_specs,
        out_shape=out_shape,
        scratch_shapes=scratch,
        compiler_params=_params(1),
        name="in_proj_latent" if latent else "in_proj_ctx",
    )(*args)


def _inproj_cast_kernel(*refs, n_in, n_out):
    ip_in, (wup_ref, wdn_ref) = refs[:n_in], refs[n_in:n_in + 2]
    outs = refs[n_in + 2:]
    ip_out, (wup_o, wdn_o), scratch = outs[:n_out], outs[n_out:n_out + 2], outs[n_out + 2:]
    _inproj_kernel(*ip_in, *ip_out, *scratch, latent=False)
    _retile_cast_kernel(wup_ref, wup_o)
    wdn_o[0] = wdn_ref[...].astype(wdn_o.dtype)


def _ctx_in_projection_with_weight_casts(x, mod, mod_row, seq_len, wts, w_up, w_down, tn):
    n_tiles, in_specs, args, out_shape, out_specs, scratch = _in_projection_parts(
        x, mod, mod_row, seq_len, wts, None, latent=False)
    d, two_dff = w_up.shape
    nct = w_down.shape[0] // tn
    n_grp = two_dff // (2 * tn)
    assert two_dff % (2 * tn) == 0 and max(n_grp, nct) <= n_tiles
    cast_in = [pl.BlockSpec((d, 2 * tn), lambda i: (0, jnp.minimum(i, n_grp - 1))),
               pl.BlockSpec((tn, d), lambda i: (jnp.minimum(i, nct - 1), 0))]
    cast_out = [pl.BlockSpec((2, d, tn), lambda i: (jnp.minimum(i, n_grp - 1), 0, 0)),
                pl.BlockSpec((1, tn, d), lambda i: (jnp.minimum(i, nct - 1), 0, 0))]
    cast_shape = [jax.ShapeDtypeStruct((two_dff // tn, d, tn), BF16), jax.ShapeDtypeStruct((nct, tn, d), BF16)]
    outs = pl.pallas_call(
        functools.partial(_inproj_cast_kernel, n_in=len(in_specs), n_out=len(out_specs)),
        grid=(n_tiles,),
        in_specs=in_specs + cast_in,
        out_specs=out_specs + cast_out,
        out_shape=out_shape + cast_shape,
        scratch_shapes=scratch,
        compiler_params=_params(1),
        name="in_proj_ctx_weight_casts",
    )(*args, w_up, w_down)
    return outs[:len(out_specs)], outs[len(out_specs)], outs[len(out_specs) + 1]


def _ctx_attn_kernel(q_ref, k_ref, v_ref, o_ref):
    n = q_ref.shape[0]
    gw = HEAD_GROUP * ATT_HEAD_DIM
    lane_head = lax.broadcasted_iota(jnp.int32, (n, gw), 1) // ATT_HEAD_DIM
    for g in range(N_ATT_HEADS // HEAD_GROUP):
        sl = slice(g * gw, (g + 1) * gw)
        q4 = q_ref[:, sl]
        qbd = jnp.concatenate([jnp.where(lane_head == hl, q4, jnp.zeros_like(q4))
                               for hl in range(HEAD_GROUP)], axis=0)
        s = _dot_nt(k_ref[:, sl], qbd)
        p = jnp.exp(s - jnp.max(s, axis=0, keepdims=True))
        p = p / jnp.sum(p, axis=0, keepdims=True)
        o4 = _dot_tn(p.astype(BF16), v_ref[:, sl])
        out = jnp.where(lane_head == 0, o4[0:n], 0.0)
        for hl in range(1, HEAD_GROUP):
            out = out + jnp.where(lane_head == hl, o4[hl * n:(hl + 1) * n], 0.0)
        o_ref[:, sl] = out.astype(o_ref.dtype)


def _context_attention(q, k, v, seq_len):
    n_tok = q.shape[0]
    spec = pl.BlockSpec((seq_len, ATT_WIDTH), lambda b: (b, 0))
    return pl.pallas_call(
        _ctx_attn_kernel,
        grid=(n_tok // seq_len,),
        in_specs=[spec, spec, spec],
        out_specs=spec,
        out_shape=jax.ShapeDtypeStruct((n_tok, ATT_WIDTH), BF16),
        compiler_params=_params(1),
        name="ctx_attn",
    )(q, k, v)


def _nbr_attn_kernel(q_ref, k_ref, vt_ref, kc_ref, vct_ref, bias_ref, o_ref, pt_ref, *, rows, kr):
    _nbr_row(pl.program_id(1), q_ref, k_ref, vt_ref, kc_ref, vct_ref, bias_ref, o_ref, pt_ref, rows=rows, kr=kr)


def _nbr_row(r, q_ref, k_ref, vt_ref, kc_ref, vct_ref, bias_ref, o_ref, pt_ref, *, rows, kr):
    n_win = kr + 2
    n_loc = kr * GRID_W
    gw = HEAD_GROUP * ATT_HEAD_DIM
    groups = N_ATT_HEADS // HEAD_GROUP
    lane_head = lax.broadcasted_iota(jnp.int32, (GRID_W, gw), 1) // ATT_HEAD_DIM
    rs = jnp.clip(r - kr // 2, 0, rows - kr)
    start = jnp.minimum(rs - (rs & 1), rows - n_win)
    delta = rs - start
    bias0 = pl.multiple_of((WIN_ROWS - 1 - (r - rs)) * GRID_W, GRID_W)
    zeros2 = jnp.zeros((2 * GRID_W, gw), BF16)
    for g in range(groups):
        sl = slice(g * gw, (g + 1) * gw)
        q4 = q_ref[:, sl]
        qbd = jnp.concatenate([jnp.where(lane_head == hl, q4, jnp.zeros_like(q4))
                               for hl in range(HEAD_GROUP)], axis=0)
        s_loc = _dot_nt(k_ref[pl.ds(pl.multiple_of(rs * GRID_W, GRID_W), n_loc), sl], qbd)
        s_loc = s_loc + bias_ref[g, pl.ds(bias0, n_loc), :]
        s_ctx = _dot_nt(kc_ref[0, :, sl], qbd)
        m = jnp.maximum(jnp.max(s_loc, axis=0, keepdims=True), jnp.max(s_ctx, axis=0, keepdims=True))
        p_loc = jnp.exp(s_loc - m)
        p_ctx = jnp.exp(s_ctx - m)
        l = jnp.sum(p_loc, axis=0, keepdims=True) + jnp.sum(p_ctx, axis=0, keepdims=True)
        pt_ref[g, 0:2 * GRID_W, :] = zeros2
        pt_ref[g, n_loc:n_loc + 2 * GRID_W, :] = zeros2
        pt_ref[g, pl.ds(pl.multiple_of(delta * GRID_W, GRID_W), n_loc), :] = p_loc.astype(BF16)
        vt_win = vt_ref[sl, pl.ds(pl.multiple_of(start * GRID_W, 2 * GRID_W), n_win * GRID_W)]
        ot = _dot(vt_win, pt_ref[g]) + _dot(vct_ref[0, sl, :], p_ctx.astype(BF16))
        o4 = (ot / l).T
        out = jnp.where(lane_head == 0, o4[0:GRID_W], 0.0)
        for hl in range(1, HEAD_GROUP):
            out = out + jnp.where(lane_head == hl, o4[hl * GRID_W:(hl + 1) * GRID_W], 0.0)
        o_ref[:, sl] = out.astype(o_ref.dtype)


def _nbr_bias_table(rpb):
    n_heads, n_dr, n_dc = rpb.shape
    c = WIN_COLS - 1
    assert n_heads == N_ATT_HEADS and n_dc == 2 * c + 1 and 2 * GRID_W == LANES
    f = rpb.astype(F32)
    packed = jnp.concatenate([f[..., c::-1], jnp.zeros((n_heads, n_dr, LANES - n_dc), F32), f[..., :c:-1]], axis=-1)
    n_groups = N_ATT_HEADS // HEAD_GROUP

    def sublane_groups(heads):
        return jnp.repeat(packed[heads], 8, axis=1).reshape(-1, LANES)

    return pl.pallas_call(
        functools.partial(_nbr_bias_kernel, n_dr=n_dr),
        out_shape=jax.ShapeDtypeStruct((n_groups, n_dr * GRID_W, HEAD_GROUP * GRID_W), F32),
        name="nbr_bias_table",
    )(sublane_groups(slice(0, None, 2)), sublane_groups(slice(1, None, 2)))


def _nbr_bias_kernel(even_ref, odd_ref, o_ref, *, n_dr):
    n_rows = even_ref.shape[0]
    s = lax.broadcasted_iota(jnp.int32, (n_rows, LANES), 0) % 8
    lane = lax.broadcasted_iota(jnp.int32, (n_rows, LANES), 1)
    cs = jnp.clip(lane % GRID_W - WIN_COLS // 2, 0, GRID_W - WIN_COLS)

    def sheared(ref):
        x = ref[...]
        for bit in (1, 2, 4):
            x = jnp.where((s & bit) != 0, pltpu.roll(x, bit, 1), x)
        return x

    even, odd = sheared(even_ref), sheared(odd_ref)
    for i in range(GRID_W // 8):
        key = 8 * i + s
        t = jnp.where(lane < GRID_W, pltpu.roll(even, 8 * i, 1) if i else even,
                      pltpu.roll(odd, (8 * i + GRID_W) % LANES, 1))
        t = jnp.where((key >= cs) & (key < cs + WIN_COLS), t, -jnp.inf)
        for p in range(n_rows // (8 * n_dr)):
            g, pair = divmod(p, HEAD_GROUP // 2)
            for dr in range(n_dr):
                r0 = (p * n_dr + dr) * 8
                o_ref[g, dr * GRID_W + 8 * i:dr * GRID_W + 8 * i + 8, pair * LANES:(pair + 1) * LANES] = t[r0:r0 + 8]


def _neighborhood_attention(q, k, vt, k_ctx, vt_ctx, rpb, n_batch, seq_len):
    rows = seq_len // GRID_W
    kr = min(WIN_ROWS, rows)
    n_win = kr + 2
    assert rows >= n_win and (rows - n_win) % 2 == 0
    bias = _nbr_bias_table(rpb)
    past = k_ctx.shape[1]
    gw = HEAD_GROUP * ATT_HEAD_DIM
    n_groups = N_ATT_HEADS // HEAD_GROUP
    row_spec = pl.BlockSpec((GRID_W, ATT_WIDTH), lambda b, r: (b * rows + r, 0))
    return pl.pallas_call(
        functools.partial(_nbr_attn_kernel, rows=rows, kr=kr),
        grid=(n_batch, rows),
        in_specs=[row_spec,
                  pl.BlockSpec((seq_len, ATT_WIDTH), lambda b, r: (b, 0)),
                  pl.BlockSpec((ATT_WIDTH, seq_len), lambda b, r: (0, b)),
                  pl.BlockSpec((1, past, ATT_WIDTH), lambda b, r: (b, 0, 0)),
                  pl.BlockSpec((1, ATT_WIDTH, past), lambda b, r: (b, 0, 0)),
                  _const_spec(bias.shape)],
        out_specs=row_spec,
        out_shape=jax.ShapeDtypeStruct((n_batch * seq_len, ATT_WIDTH), BF16),
        scratch_shapes=[pltpu.VMEM((n_groups, n_win * GRID_W, gw), BF16)],
        compiler_params=_params(2),
        name="nbr_attn",
    )(q, k, vt, k_ctx, vt_ctx, bias)


def _mlstm_kernel(*refs, chunk, n_chunks, has_state):
    dh = ML_HEAD_DIM
    nrep = 2 * GATE_ROWS
    if has_state:
        (q_ref, k_ref, vt_ref, mot_ref, gtt_ref, g_ref, c0_ref, n0_ref, m0_ref,
         o_ref, c_ref, n_ref, m_ref, hf_ref, hb_ref, cn_ref, mrun_ref) = refs
        for d in range(2):
            for hh in range(N_ML_HEADS):
                cn_ref[d, hh, 0:dh, :] = c0_ref[0, d, hh].T
                cn_ref[d, hh, dh:dh + nrep, :] = jnp.broadcast_to(n0_ref[0, d, hh], (nrep, dh))
        mrun_ref[...] = m0_ref[0]
    else:
        (q_ref, k_ref, vt_ref, mot_ref, gtt_ref, g_ref,
         o_ref, c_ref, n_ref, m_ref, hf_ref, hb_ref, cn_ref, mrun_ref) = refs
        cn_ref[...] = jnp.zeros_like(cn_ref)
        mrun_ref[...] = jnp.zeros_like(mrun_ref)
    use_state = has_state or n_chunks > 1

    i0 = lax.broadcasted_iota(jnp.int32, (chunk, chunk), 0)
    i1 = lax.broadcasted_iota(jnp.int32, (chunk, chunk), 1)
    row_id = lax.broadcasted_iota(jnp.int32, (GATE_ROWS, 1), 0)
    instances = [(hh, d) for d in range(2) for hh in range(N_ML_HEADS)]

    le = [i0 <= i1, i0 >= i1]

    def split3(x):
        hi = x.astype(BF16).astype(F32)
        r1 = x - hi
        mid = r1.astype(BF16).astype(F32)
        return jnp.concatenate([hi, mid, r1 - mid], axis=0).astype(BF16)

    def gate_rows(gi, lf_pre):
        n_rows = gi.shape[0]
        lf = _log_sigmoid(lf_pre)
        b_last = jnp.sum(lf, axis=1, keepdims=True)
        terms = split3(lf)
        per_dir = []
        for d in range(2):
            tri = jnp.where(le[d], 1.0, 0.0).astype(BF16)
            parts = _dot(terms, tri)
            cs = parts[0:n_rows] + parts[n_rows:2 * n_rows] + parts[2 * n_rows:3 * n_rows]
            w_end = b_last + gi - cs
            pad = jnp.zeros((LANES - n_rows, chunk), F32)
            per_dir.append((cs, w_end, jnp.max(w_end, axis=1, keepdims=True),
                            jnp.concatenate([gi - cs, pad], axis=0).T))
        return b_last, per_dir

    def chunk_rows(lo):
        return jnp.concatenate([gtt_ref[lo:lo + GATE_ROWS, c * chunk:(c + 1) * chunk] for c in range(n_chunks)],
                               axis=0)

    def body(i, gates):
        chunks = (i, n_chunks - 1 - i)
        rows = [pl.ds(pl.multiple_of(c * chunk, chunk), chunk) for c in chunks]
        m_prev = mrun_ref[...][:, 0:1]

        def qkv(hh, d):
            sl = slice(hh * dh, (hh + 1) * dh)
            return q_ref[rows[d], sl], k_ref[rows[d], sl], vt_ref[sl, rows[d]]

        st = [_dot_nt(qkv(hh, d)[1], qkv(hh, d)[0]) for hh, d in instances]
        if use_state:
            cn = [cn_ref[d, hh] for hh, d in instances]
            qct = [_dot_nt(cn[n].astype(BF16), qkv(hh, d)[0]) for n, (hh, d) in enumerate(instances)]

        if gates is None:
            gates = gate_rows(*[jnp.concatenate([gtt_ref[lo:lo + GATE_ROWS, r] for r in rows], axis=0)
                                for lo in (0, GATE_ROWS)])
            row0 = [0, GATE_ROWS]
        else:
            row0 = [GATE_ROWS * c for c in chunks]
        b_last_all, per_dir = gates
        a_col, cs8, w8, decay8, m_new = [], [], [], [], []
        for d in range(2):
            r = slice(row0[d], row0[d] + GATE_ROWS)
            cs_all, w_end_all, w_max_all, a_col_d = per_dir[d]
            b_last = b_last_all[r]
            m_d = jnp.maximum(b_last + m_prev, w_max_all[r])
            cs8.append(cs_all[r])
            decay8.append(jnp.exp(b_last + m_prev - m_d))
            w8.append(jnp.exp(w_end_all[r] - m_d))
            m_new.append(m_d)
            a_col.append(a_col_d)
        m_next = jnp.where((row_id & 1) == 0, m_new[0], m_new[1])
        mrun_ref[...] = jnp.broadcast_to(m_next, (GATE_ROWS, LANES))

        pt_all, g_all, den_all = [], [], []
        for n, (hh, d) in enumerate(instances):
            j = 2 * hh + d
            lane = row0[d] + j
            a = jnp.where(le[d], a_col[d][:, lane:lane + 1], -jnp.inf)
            g = jnp.maximum(jnp.max(a, axis=0, keepdims=True), m_prev[j:j + 1])
            pt = st[n] * jnp.exp(a - g)
            pt_all.append(pt)
            g_all.append(g)
            den_all.append(jnp.sum(pt, axis=0, keepdims=True))

        for n, (hh, d) in enumerate(instances):
            j = 2 * hh + d
            sl = slice(hh * dh, (hh + 1) * dh)
            _, k, vt = qkv(hh, d)
            g, den = g_all[n], den_all[n]
            num = _dot(vt, pt_all[n].astype(BF16))
            if use_state:
                w_inter = jnp.exp(m_prev[j:j + 1] - g)
                num = num + w_inter * qct[n][0:dh]
                den = den + w_inter * qct[n][dh:dh + 1]
            scale = 1.0 / jnp.maximum(jnp.abs(den), jnp.exp(-(cs8[d][j:j + 1] + g)))
            (hb_ref if d else hf_ref)[sl, rows[d]] = num * scale
            w_row = w8[d][j:j + 1]
            lhs = jnp.concatenate([(vt.astype(F32) * w_row).astype(BF16),
                                   jnp.broadcast_to(w_row, (nrep, chunk)).astype(BF16)], axis=0)
            upd = _dot(lhs, k)
            if use_state:
                upd = decay8[d][j:j + 1] * cn[n] + upd
            cn_ref[d, hh] = upd
        return 0

    def finish(c, _):
        rows = pl.ds(pl.multiple_of(c * chunk, chunk), chunk)
        for hh in range(N_ML_HEADS):
            sl = slice(hh * dh, (hh + 1) * dh)
            hs = hf_ref[sl, rows] + hb_ref[sl, rows]
            y = hs * lax.rsqrt(jnp.mean(hs * hs, axis=0, keepdims=True) + EPS) * g_ref[sl, :]
            y = y * jax.nn.sigmoid(mot_ref[sl, rows])
            o_ref[rows, sl] = y.T.astype(o_ref.dtype)
        return 0

    if n_chunks <= MLSTM_UNROLL:
        gates = gate_rows(chunk_rows(0), chunk_rows(GATE_ROWS))
        for c in range(n_chunks):
            body(c, gates)
        for c in range(n_chunks):
            finish(c, 0)
    else:
        lax.fori_loop(0, n_chunks, lambda i, _: body(i, None), 0)
        lax.fori_loop(0, n_chunks, finish, 0)
    for d in range(2):
        for hh in range(N_ML_HEADS):
            cn = cn_ref[d, hh]
            c_ref[0, d, hh] = cn[0:dh].T
            n_ref[0, d, hh] = cn[dh:dh + 1]
    m_ref[0] = mrun_ref[...]


def _mlstm(mq, mk, mvt, mot, gates_t, ml_g, state, n_batch, seq_len, chunk):
    dh = ML_HEAD_DIM
    nh = N_ML_HEADS
    width = nh * dh
    has_state = state is not None
    seq = pl.BlockSpec((seq_len, width), lambda b: (b, 0))
    seq_t = pl.BlockSpec((width, seq_len), lambda b: (0, b))
    st_c = pl.BlockSpec((1, 2, nh, dh, dh), lambda b: (b, 0, 0, 0, 0))
    st_n = pl.BlockSpec((1, 2, nh, 1, dh), lambda b: (b, 0, 0, 0, 0))
    st_m = pl.BlockSpec((1, GATE_ROWS, LANES), lambda b: (b, 0, 0))
    in_specs = [seq, seq, seq_t, seq_t,
                pl.BlockSpec((2 * GATE_ROWS, seq_len), lambda b: (0, b)),
                _const_spec((width, chunk))]
    args = [mq, mk, mvt, mot, gates_t, jnp.broadcast_to(ml_g.reshape(width, 1), (width, chunk))]
    if has_state:
        in_specs += [st_c, st_n, st_m]
        args += list(state)
    out_shape = [jax.ShapeDtypeStruct((n_batch * seq_len, width), BF16),
                 jax.ShapeDtypeStruct((n_batch, 2, nh, dh, dh), F32),
                 jax.ShapeDtypeStruct((n_batch, 2, nh, 1, dh), F32),
                 jax.ShapeDtypeStruct((n_batch, GATE_ROWS, LANES), F32)]
    return pl.pallas_call(
        functools.partial(_mlstm_kernel, chunk=chunk, n_chunks=seq_len // chunk, has_state=has_state),
        grid=(n_batch,),
        in_specs=in_specs,
        out_specs=[seq, st_c, st_n, st_m],
        out_shape=out_shape,
        scratch_shapes=[pltpu.VMEM((width, seq_len), F32), pltpu.VMEM((width, seq_len), F32),
                        pltpu.VMEM((2, nh, dh + 2 * GATE_ROWS, dh), F32), pltpu.VMEM((GATE_ROWS, LANES), F32)],
        compiler_params=_params(1),
        name="mlstm_latent" if has_state else "mlstm_ctx",
    )(*args)


def _pack_gate_rows(m):
    b = m.shape[0]
    packed = m.transpose(0, 2, 1).reshape(b, 2 * N_ML_HEADS, 1)
    return jnp.broadcast_to(packed, (b, GATE_ROWS, LANES))


def _unpack_gate_rows(m):
    b = m.shape[0]
    return m[:, :, 0].reshape(b, N_ML_HEADS, 2).transpose(0, 2, 1)


def _inproj_ctxmix_kernel(*refs, n_in, n_out, n_seq, seq_len, chunk):
    ip_in = refs[:n_in]
    q_ref, k_ref, v_ref, mq_ref, mk_ref, mvt_ref, mot_ref, gtt_ref, g_ref = refs[n_in:n_in + 9]
    outs = refs[n_in + 9:]
    ip_out = outs[:n_out]
    oatt_ref, oml_ref, c_ref, n_ref, m_ref = outs[n_out:n_out + 5]
    scratch = outs[n_out + 5:]
    for s in range(n_seq):
        if s == n_seq // 2:
            _inproj_kernel(*ip_in, *ip_out, latent=True)
        rows = slice(s * seq_len, (s + 1) * seq_len)
        _ctx_attn_kernel(q_ref.at[rows, :], k_ref.at[rows, :], v_ref.at[rows, :], oatt_ref.at[rows, :])
        _mlstm_kernel(mq_ref.at[rows, :], mk_ref.at[rows, :], mvt_ref.at[:, rows], mot_ref.at[:, rows],
                      gtt_ref.at[:, rows], g_ref, oml_ref.at[rows, :], c_ref.at[s:s + 1], n_ref.at[s:s + 1],
                      m_ref.at[s:s + 1], *scratch[4 * s:4 * s + 4],
                      chunk=chunk, n_chunks=seq_len // chunk, has_state=False)


def _latent_in_projection_with_ctx_mixers(x, mod, mod_row, seq_len, wts, rope, ctx, n_ctx_batch, ctx_seq):
    n_tiles, in_specs, args, out_shape, out_specs, scratch = _in_projection_parts(
        x, mod, mod_row, seq_len, wts, rope, latent=True)
    n_seq = n_ctx_batch // n_tiles
    rows = n_seq * ctx_seq
    n_ctx_tok = n_ctx_batch * ctx_seq
    chunk = min(ML_CHUNK, ctx_seq)
    nh, dh = N_ML_HEADS, ML_HEAD_DIM

    def tok(width):
        return pl.BlockSpec((rows, width), lambda i: (i, 0))

    def tok_t(height):
        return pl.BlockSpec((height, rows), lambda i: (0, i))

    def per_seq(*tail):
        return pl.BlockSpec((n_seq,) + tail, lambda i: (i,) + (0,) * len(tail))

    aq, ak, av, mq, mk, mvt, mot, gtt = ctx
    mix_in_specs = ([tok(ATT_WIDTH)] * 3 + [tok(ML_WIDTH)] * 2 + [tok_t(ML_WIDTH)] * 2
                    + [tok_t(2 * GATE_ROWS), _const_spec((ML_WIDTH, chunk))])
    mix_args = [aq, ak, av, mq, mk, mvt, mot, gtt,
                jnp.broadcast_to(wts["ml_g"].reshape(ML_WIDTH, 1), (ML_WIDTH, chunk))]
    mix_out_shape = [jax.ShapeDtypeStruct((n_ctx_tok, ATT_WIDTH), BF16),
                     jax.ShapeDtypeStruct((n_ctx_tok, ML_WIDTH), BF16),
                     jax.ShapeDtypeStruct((n_ctx_batch, 2, nh, dh, dh), F32),
                     jax.ShapeDtypeStruct((n_ctx_batch, 2, nh, 1, dh), F32),
                     jax.ShapeDtypeStruct((n_ctx_batch, GATE_ROWS, LANES), F32)]
    mix_out_specs = [tok(ATT_WIDTH), tok(ML_WIDTH), per_seq(2, nh, dh, dh), per_seq(2, nh, 1, dh),
                     per_seq(GATE_ROWS, LANES)]
    mix_scratch = [pltpu.VMEM((ML_WIDTH, ctx_seq), F32), pltpu.VMEM((ML_WIDTH, ctx_seq), F32),
                   pltpu.VMEM((2, nh, dh + 2 * GATE_ROWS, dh), F32), pltpu.VMEM((GATE_ROWS, LANES), F32)] * n_seq
    outs = pl.pallas_call(
        functools.partial(_inproj_ctxmix_kernel, n_in=len(in_specs), n_out=len(out_specs), n_seq=n_seq,
                          seq_len=ctx_seq, chunk=chunk),
        grid=(n_tiles,),
        in_specs=in_specs + mix_in_specs,
        out_specs=out_specs + mix_out_specs,
        out_shape=out_shape + mix_out_shape,
        scratch_shapes=scratch + mix_scratch,
        compiler_params=_params(1),
        name="in_proj_latent_ctx_mixers",
    )(*args, *mix_args)
    return outs[:len(out_specs)], outs[len(out_specs):]


def _ffn_kernel(oa_ref, oap_ref, oan_ref, om_ref, omp_ref, omn_ref, x_ref, xp_ref, xn_ref, mod_ref, g2_ref,
                wout_ref, wup_ref, cw_ref, cb_ref, wd_ref,
                y_ref, oc_ref, lhs_ref, x1_ref, ug_ref, uv_ref, acc_ref, *, seq_len, n_col_tiles, side_work=None):
    tm, d = x_ref.shape
    aw = oa_ref.shape[1]

    def halo(next_ref, prev_ref):
        row = lax.broadcasted_iota(jnp.int32, next_ref.shape, 0)
        return jnp.where(row < HALO // 2, next_ref[...], prev_ref[...])

    oc_ref[0:tm, 0:aw] = oa_ref[...]
    oc_ref[0:tm, aw:] = om_ref[...]
    oc_ref[tm:tm + HALO, 0:aw] = halo(oan_ref, oap_ref)
    oc_ref[tm:tm + HALO, aw:] = halo(omn_ref, omp_ref)
    mod = mod_ref[0]
    g1 = mod[:, 2 * d:3 * d]
    sh2, sc2 = mod[:, 3 * d:4 * d], mod[:, 4 * d:5 * d]
    out = _dot(oc_ref[...], wout_ref[...])

    def norm2(x1):
        y = x1 * lax.rsqrt(jnp.mean(x1 * x1, axis=-1, keepdims=True) + EPS)
        return ((y * g2_ref[...]) * (1.0 + sc2) + sh2).astype(BF16)

    x1 = x_ref[...] + g1 * out[0:tm]
    x1_ref[...] = x1
    lhs_ref[0:tm, :] = norm2(x1)
    lhs_ref[tm:tm + HALO, :] = norm2(halo(xn_ref, xp_ref) + g1 * out[tm:tm + HALO])
    acc_ref[...] = jnp.zeros_like(acc_ref)
    sub = 8
    period = min(seq_len, tm)
    first_groups = sorted({r // sub for r in range(0, tm, period)})
    last_groups = sorted({(r + period - 1) // sub for r in range(0, tm, period)})
    tile0 = pl.program_id(0) * tm

    def zero_rows(x, groups, target):
        parts, at = [], 0
        for grp in groups:
            lo = grp * sub
            if lo > at:
                parts.append(x[at:lo])
            pos = (tile0 + lo + lax.broadcasted_iota(jnp.int32, (sub, 1), 0)) % seq_len
            parts.append(jnp.where(pos == target, 0.0, x[lo:lo + sub]))
            at = lo + sub
        if at < tm:
            parts.append(x[at:tm])
        return jnp.concatenate(parts, axis=0)

    def conv(u, cw, cb):
        prev = zero_rows(pltpu.roll(u, 1, 0)[0:tm], first_groups, 0)
        nxt = zero_rows(pltpu.roll(u, tm + HALO - 1, 0)[0:tm], last_groups, seq_len - 1)
        return prev * cw[0:1] + u[0:tm] * cw[1:2] + nxt * cw[2:3] + cb

    def up(j, slot):
        lhs = lhs_ref[...]
        ug_ref[slot] = _dot(lhs, wup_ref[j])
        uv_ref[slot] = _dot(lhs, wup_ref[n_col_tiles + j])

    def act(j, slot):
        gate = conv(ug_ref[slot], cw_ref[j], cb_ref[j])
        val = conv(uv_ref[slot], cw_ref[n_col_tiles + j], cb_ref[n_col_tiles + j])
        return ((gate * jax.nn.sigmoid(gate)) * val).astype(BF16)

    def stage(j, slot, next_up):
        a = act(j, slot)
        if next_up:
            up(j + 1, 1 - slot)
        return _dot(a, wd_ref[j])

    def pair(i, _):
        acc_ref[...] += stage(2 * i, 0, True)
        acc_ref[...] += stage(2 * i + 1, 1, True)
        return 0

    def finish(tail):
        g2 = mod_ref[0][:, 5 * d:6 * d]
        y_ref[...] = x1_ref[...] + g2 * (acc_ref[...] + tail)

    up(0, 0)
    if side_work:
        n_points = n_col_tiles + 1

        def hosted(point):
            for i, work in enumerate(side_work):
                if (i * n_points) // len(side_work) == point:
                    work()

        hosted(0)
        for j in range(n_col_tiles - 1):
            acc_ref[...] += stage(j, j % 2, True)
            hosted(j + 1)
        finish(stage(n_col_tiles - 1, (n_col_tiles - 1) % 2, False))
        hosted(n_col_tiles)
        return
    n_pairs = (n_col_tiles - 1) // 2
    if n_pairs:
        lax.fori_loop(0, n_pairs, pair, 0)
    j0 = 2 * n_pairs
    if n_col_tiles - j0 == 2:
        acc_ref[...] += stage(j0, 0, True)
        finish(stage(j0 + 1, 1, False))
    else:
        finish(stage(j0, 0, False))


def _mix_ffn_parts(o_att, o_ml, x, mod, mod_row, seq_len, wts):
    n_tok, d = x.shape
    tm = _token_tile(n_tok, seq_len)
    hpt = tm // HALO
    n_halo = n_tok // HALO
    nct, tn, _ = wts["w_down"].shape
    mix = o_att.shape[1] + o_ml.shape[1]

    def tile_and_halos(width):
        return [pl.BlockSpec((tm, width), lambda i: (i, 0)),
                pl.BlockSpec((HALO, width), lambda i: (jnp.maximum(i * hpt - 1, 0), 0)),
                pl.BlockSpec((HALO, width), lambda i: (jnp.minimum((i + 1) * hpt, n_halo - 1), 0))]

    in_specs = (tile_and_halos(o_att.shape[1]) + tile_and_halos(o_ml.shape[1]) + tile_and_halos(d) + [
        pl.BlockSpec((1, 1, mod.shape[-1]), lambda i: (mod_row(i, tm), 0, 0)),
        _const_spec((1, d)), _const_spec((mix, d)),
        _const_spec((2 * nct, d, tn)), _const_spec((2 * nct, 3, tn)), _const_spec((2 * nct, 1, tn)),
        _const_spec((nct, tn, d))])
    args = [o_att, o_att, o_att, o_ml, o_ml, o_ml, x, x, x, mod, wts["norm2_g"], wts["w_out"],
            wts["w_up"], wts["conv_w"], wts["conv_b"], wts["w_down"]]
    scratch = [pltpu.VMEM((tm + HALO, mix), BF16), pltpu.VMEM((tm + HALO, d), BF16), pltpu.VMEM((tm, d), F32),
               pltpu.VMEM((2, tm + HALO, tn), F32), pltpu.VMEM((2, tm + HALO, tn), F32), pltpu.VMEM((tm, d), F32)]
    return (n_tok // tm, nct, in_specs, args, pl.BlockSpec((tm, d), lambda i: (i, 0)),
            jax.ShapeDtypeStruct((n_tok, d), F32), scratch)


def _mix_ffn(o_att, o_ml, x, mod, mod_row, seq_len, wts):
    n_tiles, nct, in_specs, args, out_spec, out_shape, scratch = _mix_ffn_parts(
        o_att, o_ml, x, mod, mod_row, seq_len, wts)
    return pl.pallas_call(
        functools.partial(_ffn_kernel, seq_len=seq_len, n_col_tiles=nct),
        grid=(n_tiles,),
        in_specs=in_specs,
        out_specs=out_spec,
        out_shape=out_shape,
        scratch_shapes=scratch,
        compiler_params=_params(1),
        name="mix_ffn",
    )(*args)


def _ffn_nbr_kernel(*refs, n_in, seq_len, n_col_tiles, rows, kr, rows_per_tile, tiles_per_batch):
    ffn_in = refs[:n_in]
    q_ref, k_ref, vt_ref, kc_ref, vct_ref, bias_ref = refs[n_in:n_in + 6]
    y_ref, o_ref = refs[n_in + 6:n_in + 8]
    scratch = refs[n_in + 8:]
    ffn_scratch, pt_refs = scratch[:-2], scratch[-2:]
    base = (pl.program_id(0) % tiles_per_batch) * rows_per_tile

    def row(local, pt_ref):
        rr = pl.ds(pl.multiple_of(local * GRID_W, GRID_W), GRID_W)
        _nbr_row(base + local, q_ref.at[rr, :], k_ref, vt_ref, kc_ref, vct_ref, bias_ref, o_ref.at[rr, :], pt_ref,
                 rows=rows, kr=kr)

    side_work = [functools.partial(row, local, pt_refs[local % 2]) for local in range(rows_per_tile)]
    _ffn_kernel(*ffn_in, y_ref, *ffn_scratch, seq_len=seq_len, n_col_tiles=n_col_tiles, side_work=side_work)


def _mix_ffn_with_nbr_attention(o_att, o_ml, x, mod, mod_row, seq_len, wts, q, k, vt, k_ctx, vt_ctx, rpb,
                                n_batch, nbr_seq):
    n_tiles, nct, in_specs, args, out_spec, out_shape, scratch = _mix_ffn_parts(
        o_att, o_ml, x, mod, mod_row, seq_len, wts)
    rows = nbr_seq // GRID_W
    kr = min(WIN_ROWS, rows)
    n_win = kr + 2
    assert rows >= n_win and (rows - n_win) % 2 == 0
    rpt = (n_batch * rows) // n_tiles
    tpb = n_tiles // n_batch
    bias = _nbr_bias_table(rpb)
    past = k_ctx.shape[1]
    gw = HEAD_GROUP * ATT_HEAD_DIM
    n_groups = N_ATT_HEADS // HEAD_GROUP
    row_blk = pl.BlockSpec((rpt * GRID_W, ATT_WIDTH), lambda i: (i, 0))
    once = pl.Buffered(1)
    nbr_specs = [row_blk,
                 pl.BlockSpec((nbr_seq, ATT_WIDTH), lambda i: (i // tpb, 0), pipeline_mode=once),
                 pl.BlockSpec((ATT_WIDTH, nbr_seq), lambda i: (0, i // tpb), pipeline_mode=once),
                 pl.BlockSpec((1, past, ATT_WIDTH), lambda i: (i // tpb, 0, 0), pipeline_mode=once),
                 pl.BlockSpec((1, ATT_WIDTH, past), lambda i: (i // tpb, 0, 0), pipeline_mode=once),
                 _const_spec(bias.shape)]
    pt = pltpu.VMEM((n_groups, n_win * GRID_W, gw), BF16)
    y, o_lat = pl.pallas_call(
        functools.partial(_ffn_nbr_kernel, n_in=len(in_specs), seq_len=seq_len, n_col_tiles=nct, rows=rows, kr=kr,
                          rows_per_tile=rpt, tiles_per_batch=tpb),
        grid=(n_tiles,),
        in_specs=in_specs + nbr_specs,
        out_specs=[out_spec, row_blk],
        out_shape=[out_shape, jax.ShapeDtypeStruct((n_batch * nbr_seq, ATT_WIDTH), BF16)],
        scratch_shapes=scratch + [pt, pt],
        compiler_params=_params(1),
        name="mix_ffn_ctx_nbr_attn",
    )(*args, q, k, vt, k_ctx, vt_ctx, bias)
    return y, o_lat


def _retile_cast_kernel(w_ref, o_ref):
    tn = o_ref.shape[-1]
    for t in range(o_ref.shape[0]):
        o_ref[t] = w_ref[:, t * tn:(t + 1) * tn].astype(o_ref.dtype)


def _retile_cast(w, tn):
    r, c = w.shape
    group = 2 if (c // tn) % 2 == 0 else 1
    return pl.pallas_call(
        _retile_cast_kernel,
        grid=(c // (tn * group),),
        in_specs=[pl.BlockSpec((r, tn * group), lambda j: (0, j))],
        out_specs=pl.BlockSpec((group, r, tn), lambda j: (j, 0, 0)),
        out_shape=jax.ShapeDtypeStruct((c // tn, r, tn), BF16),
        compiler_params=_params(1),
        name="retile_cast",
    )(w)


def _cast_kernel(w_ref, o_ref):
    o_ref[...] = w_ref[...].astype(o_ref.dtype)


def _cast_rows(w, block_rows):
    r, c = w.shape
    spec = pl.BlockSpec((block_rows, c), lambda i: (i, 0))
    return pl.pallas_call(
        _cast_kernel,
        grid=(r // block_rows,),
        in_specs=[spec],
        out_specs=spec,
        out_shape=jax.ShapeDtypeStruct((r, c), BF16),
        compiler_params=_params(1),
        name="cast_rows",
    )(w)


def _layer_weights(norm1_g, norm2_g, w_in, b_gate, q_g, k_g, ml_g, w_out, w_up, conv_w, conv_b, w_down):
    d = w_in.shape[0]
    a, m, nh = ATT_WIDTH, ML_WIDTH, N_ML_HEADS
    def gate_lanes(g):
        g = g.reshape(g.shape[0], 2, 2, nh).transpose(0, 2, 3, 1).reshape(g.shape[0], 4 * nh)
        return jnp.pad(g, ((0, 0), (0, LANES - 4 * nh)))

    w_g = gate_lanes(w_in[:, 3 * a + 4 * m:])
    b_g = gate_lanes(b_gate.astype(F32).reshape(1, 4 * nh))
    head_id = jnp.arange(a) // ATT_HEAD_DIM
    d_ff = w_down.shape[0]
    tn = MXU_WIDTH
    nct = d_ff // tn

    def col_tiles(w):
        return w.reshape(w.shape[0], 2 * nct, tn).transpose(1, 0, 2)

    return {
        "norm1_g": norm1_g.reshape(1, d), "norm2_g": norm2_g.reshape(1, d),
        "w_in": w_in.astype(BF16),
        "w_g": w_g.astype(BF16), "b_g": b_g,
        "q_g": jnp.tile(q_g, N_ATT_HEADS).reshape(1, a), "k_g": jnp.tile(k_g, N_ATT_HEADS).reshape(1, a),
        "blockdiag": (head_id[:, None] == head_id[None, :]).astype(BF16),
        "ml_g": ml_g.reshape(1, m),
        "w_out": w_out.astype(BF16),
        "conv_w": col_tiles(conv_w.astype(F32)), "conv_b": col_tiles(conv_b.astype(F32).reshape(1, -1)),
        "w_up_f32": w_up, "w_down_f32": w_down, "ffn_tile": tn,
    }


def _rope_tables(seq_len):
    quarter = ML_HEAD_DIM // 4
    pos = np.arange(seq_len)
    inv_freq = ROPE_THETA ** (-np.arange(quarter, dtype=np.float64) / quarter)
    ang_r = (pos // GRID_W).astype(np.float64)[:, None] * inv_freq[None, :]
    ang_c = (pos % GRID_W).astype(np.float64)[:, None] * inv_freq[None, :]
    cos = np.concatenate([np.cos(ang_r)] * 2 + [np.cos(ang_c)] * 2, axis=-1)
    sin = np.concatenate([-np.sin(ang_r), np.sin(ang_r), -np.sin(ang_c), np.sin(ang_c)], axis=-1)
    return jnp.asarray(cos, F32), jnp.asarray(sin, F32)


def _layer(xp, xs, mod, wts, rope, ctx_kv, rpb, state, batch, seq, dec_batch, dec_seq):
    def ctx_row(i, tm):
        return 0 * i

    def lat_row(i, tm):
        return 1 + (i * tm) // dec_seq

    w_up, w_down, tn = wts["w_up_f32"], wts["w_down_f32"], wts["ffn_tile"]
    n_ctx_tiles = (batch * seq) // _token_tile(batch * seq, seq)
    if n_ctx_tiles >= max(w_up.shape[1] // (2 * tn), w_down.shape[0] // tn) and (w_up.shape[1] // tn) % 2 == 0:
        ctx, w_up_t, w_down_t = _ctx_in_projection_with_weight_casts(xp, mod, ctx_row, seq, wts, w_up, w_down, tn)
    else:
        ctx = _in_projection(xp, mod, ctx_row, seq, wts, None, latent=False)
        w_up_t = _retile_cast(w_up, tn)
        w_down_t = _cast_rows(w_down, tn).reshape(w_down.shape[0] // tn, tn, w_down.shape[1])
    wts = dict(wts, w_up=w_up_t, w_down=w_down_t)
    cache_k, cache_v = ctx[8], ctx[9]
    n_lat_tiles = (dec_batch * dec_seq) // _token_tile(dec_batch * dec_seq, dec_seq)
    if batch % n_lat_tiles == 0:
        lat, (o_att_c, o_ml_c, c_f, n_f, m_f) = _latent_in_projection_with_ctx_mixers(
            xs, mod, lat_row, dec_seq, wts, rope, ctx[:8], batch, seq)
    else:
        aq, ak, av, mq, mk, mvt, mot, gtt = ctx[:8]
        o_att_c = _context_attention(aq, ak, av, seq)
        o_ml_c, c_f, n_f, m_f = _mlstm(mq, mk, mvt, mot, gtt, wts["ml_g"], None, batch, seq, min(ML_CHUNK, seq))
        lat = _in_projection(xs, mod, lat_row, dec_seq, wts, rope, latent=True)
    aq, ak, avt, mq, mk, mvt, mot, gtt = lat[:8]
    n_rows = dec_batch * (dec_seq // GRID_W)
    if n_rows % n_ctx_tiles == 0 and n_ctx_tiles % dec_batch == 0:
        xp, o_att = _mix_ffn_with_nbr_attention(o_att_c, o_ml_c, xp, mod, ctx_row, seq, wts, aq, ak, avt,
                                                ctx_kv[0], ctx_kv[1], rpb, dec_batch, dec_seq)
    else:
        xp = _mix_ffn(o_att_c, o_ml_c, xp, mod, ctx_row, seq, wts)
        o_att = _neighborhood_attention(aq, ak, avt, ctx_kv[0], ctx_kv[1], rpb, dec_batch, dec_seq)
    o_ml = _mlstm(mq, mk, mvt, mot, gtt, wts["ml_g"], state, dec_batch, dec_seq, min(ML_CHUNK, dec_seq))[0]
    xs = _mix_ffn(o_att, o_ml, xs, mod, lat_row, dec_seq, wts)
    return xp, xs, cache_k, cache_v, (c_f, n_f[:, :, :, 0, :], _unpack_gate_rows(m_f))


def kernel(x_prompt, x_sample, cache_k, cache_v, state_C, state_n, state_m, c, c_ctx, w_mod, b_mod, norm1_g,
           norm2_g, w_in, b_gate, q_norm_g, k_norm_g, rpb, ml_norm_g, w_out, w_up, conv_w, conv_b, w_down):
    batch, seq, d = x_prompt.shape
    dec_batch, dec_seq, _ = x_sample.shape
    depth = w_mod.shape[0]
    past = cache_k.shape[2]
    cvecs = jnp.concatenate([c_ctx[None, :], c], axis=0).astype(F32)
    rope = _rope_tables(dec_seq)

    xp = x_prompt.reshape(batch * seq, d)
    xs = x_sample.reshape(dec_batch * dec_seq, d)
    ks, vs, cs, ns, ms = [], [], [], [], []
    for l in range(depth):
        wts = _layer_weights(norm1_g[l], norm2_g[l], w_in[l], b_gate[l], q_norm_g[l], k_norm_g[l],
                             ml_norm_g[l], w_out[l], w_up[l], conv_w[l], conv_b[l], w_down[l])
        mod = _modulation(cvecs, w_mod[l], b_mod[l])

        state = (state_C[:, l].astype(F32),
                 state_n[:, l].astype(F32)[:, :, :, None, :],
                 _pack_gate_rows(state_m[:, l].astype(F32)))
        ctx_kv = (cache_k[:, l].reshape(dec_batch, past, ATT_WIDTH).astype(BF16),
                  cache_v[:, l].reshape(dec_batch, past, ATT_WIDTH).transpose(0, 2, 1).astype(BF16))
        xp, xs, ak, av, st = _layer(xp, xs, mod, wts, rope, ctx_kv, rpb[l], state, batch, seq, dec_batch, dec_seq)
        ks.append(ak.reshape(batch, seq, N_ATT_HEADS, ATT_HEAD_DIM))
        vs.append(av.reshape(batch, seq, N_ATT_HEADS, ATT_HEAD_DIM))
        cs.append(st[0])
        ns.append(st[1])
        ms.append(st[2])
    return (xp.reshape(batch, seq, d), xs.reshape(dec_batch, dec_seq, d),
            jnp.stack(ks, axis=1), jnp.stack(vs, axis=1),
            jnp.stack(cs, axis=1), jnp.stack(ns, axis=1), jnp.stack(ms, axis=1))
```
